```python
import math
import jax, jax.numpy as jnp
from jax import lax
import numpy as np

D_MODEL = 1024
BATCH = 32
SEQ = 256
DEPTH = 2
DEC_BATCH = 2
DEC_SEQ = 1024
PAST_LEN = 256

GRID_W = 64
N_AB = (DEPTH + 1) // 2
N_CD = DEPTH // 2
MIX_A = D_MODEL // 2
MIX_B = D_MODEL // 2
MIX_C = D_MODEL // 2
MIX_D = D_MODEL // 2
H_A = 4
HD_A = MIX_A // (2 * H_A)
VD_A = 2 * HD_A
H_B = MIX_B // 64
HD_B = 64
LORA_W = 64
LORA_A = 64
LORA_G = 128
S5_GROUP = 16
G_C = MIX_C // S5_GROUP
P_C = 64
H_D = 4
HD_D = MIX_D // H_D
CHUNK = 128
Q_BLOCK = 128
D_FF = (((8 * D_MODEL + 2) // 3 + 255) // 256) * 256
A_QK = 2 * H_A * HD_A
A_V = H_A * VD_A
B_COLS = 3 * MIX_B + LORA_W + LORA_A + LORA_G
IN_AB = 2 * A_QK + A_V + B_COLS
OUT_AB = A_V + MIX_B
IN_CD = MIX_C + 4 * MIX_D
OUT_CD = MIX_C + MIX_D
ROPE_THETA = 10000.0
NORM_EPS = 1e-6
RWKV_GN_EPS = 64e-5
RET_GN_EPS = 1e-5
F32 = jnp.float32

kernel_name = 'hybrid_diffattn_rwkv7_s5_retention_prefix_step'


def rms_norm(x, g, eps=NORM_EPS):
    xf = x.astype(F32)
    y = xf * lax.rsqrt(jnp.mean(xf * xf, axis=-1, keepdims=True) + eps)
    return (y * g.astype(F32)).astype(x.dtype)


def head_group_norm(x, w, b, eps):
    xf = x.astype(F32)
    xc = xf - jnp.mean(xf, axis=-1, keepdims=True)
    y = xc * lax.rsqrt(jnp.mean(xc * xc, axis=-1, keepdims=True) + eps)
    return y.reshape(x.shape[:-2] + (-1,)) * w.astype(F32) + b.astype(F32)


def split_cols(x, sizes):
    return jnp.split(x, np.cumsum(sizes)[:-1].tolist(), axis=-1)


def axial_rope(n_tok, dim):
    rows = n_tok // GRID_W
    n_freq = dim // 4
    inv = 1.0 / (ROPE_THETA ** (jnp.arange(n_freq, dtype=F32) / n_freq))
    row = jnp.repeat(jnp.arange(rows, dtype=F32), GRID_W)
    col = jnp.tile(jnp.arange(GRID_W, dtype=F32), rows)
    ang = jnp.concatenate([row[:, None] * inv, col[:, None] * inv], axis=-1)
    return jnp.cos(ang), jnp.sin(ang)


def apply_rope(x, cos, sin):
    shape = (1, x.shape[1]) + (1,) * (x.ndim - 3) + (cos.shape[-1],)
    cos = cos.reshape(shape).astype(x.dtype)
    sin = sin.reshape(shape).astype(x.dtype)
    x1, x2 = jnp.split(x, 2, axis=-1)
    return jnp.concatenate([x1 * cos - x2 * sin, x1 * sin + x2 * cos], axis=-1)


def centred_shift(x, mu):
    prev = jnp.pad(x[:, :-1], ((0, 0), (1, 0), (0, 0)))
    nxt = jnp.pad(x[:, 1:], ((0, 0), (0, 1), (0, 0)))
    return x + (0.5 * (prev + nxt) - x) * mu


def modulation(cvec, w, b):
    m = jax.nn.silu(cvec) @ w + b
    return jnp.split(m[:, None, :], 6, axis=-1)


def swiglu(h, wg, wu, wd):
    return (jax.nn.silu(h @ wg) * (h @ wu)) @ wd


def diff_attention(q, k, v, lam, subln_g, lam_init):
    bsz, t = q.shape[0], q.shape[1]
    nb = t // Q_BLOCK
    qb = jnp.moveaxis(q.reshape((bsz, nb, Q_BLOCK) + q.shape[2:]), 1, 0)
    scale = HD_A ** -0.5

    def block(qi):
        s = jnp.einsum('bqhcd,bkhcd->bchqk', qi, k).astype(F32) * scale
        p = jax.nn.softmax(s, axis=-1)
        att = p[:, 0] - lam * p[:, 1]
        return jnp.einsum('bhqk,bkhe->bqhe', att.astype(v.dtype), v)

    o = jnp.moveaxis(lax.map(block, qb), 0, 1).reshape(bsz, t, H_A, VD_A)
    o = rms_norm(o, subln_g) * (1.0 - lam_init)
    return o.reshape(bsz, t, A_V)


def rwkv7_scan(r, decay, k, v, kk, a, s0):
    xs = tuple(jnp.moveaxis(z.astype(F32), 1, 0) for z in (r, decay, k, v, kk, a))

    def step(s, inp):
        r_t, w_t, k_t, v_t, kk_t, a_t = inp
        sa = jnp.einsum('bhvk,bhk->bhv', s, -kk_t)
        s = (s * w_t[:, :, None, :] + sa[..., None] * (kk_t * a_t)[:, :, None, :]
             + v_t[..., None] * k_t[:, :, None, :])
        return s, jnp.einsum('bhvk,bhk->bhv', s, r_t)

    s_fin, ys = lax.scan(step, s0.astype(F32), xs)
    return jnp.moveaxis(ys, 0, 1), s_fin


def retention_dir(q, k, v, log_gamma, r0, inclusive):
    bsz, t, nh, _ = q.shape
    dv = v.shape[-1]
    nc = t // CHUNK
    pos = jnp.arange(CHUNK, dtype=F32)
    diff = pos[:, None] - pos[None, :]
    mask = (diff >= 0) if inclusive else (diff > 0)
    dmat = jnp.where(mask[None], jnp.exp(log_gamma[:, None, None] * jnp.maximum(diff, 0.0)[None]), 0.0)
    xi = jnp.exp(log_gamma[None, :] * (pos[:, None] + 1.0))
    zeta = jnp.exp(log_gamma[None, :] * (CHUNK - 1.0 - pos[:, None]))
    g_chunk = jnp.exp(log_gamma * CHUNK)

    def chunks(z):
        return jnp.moveaxis(z.astype(F32).reshape(bsz, nc, CHUNK, nh, z.shape[-1]), 1, 0)

    def step(r, inp):
        qi, ki, vi = inp
        inner = jnp.einsum('bqhd,bkhd->bhqk', qi, ki) * dmat
        o = (jnp.einsum('bhqk,bkhe->bqhe', inner, vi)
             + jnp.einsum('bqhd,bhde->bqhe', qi, r) * xi[None, :, :, None])
        r = r * g_chunk[None, :, None, None] + jnp.einsum('bkhd,bkhe->bhde', ki * zeta[None, :, :, None], vi)
        return r, o

    r_fin, o = lax.scan(step, r0.astype(F32), (chunks(q), chunks(k), chunks(v)))
    return jnp.moveaxis(o, 0, 1).reshape(bsz, t, nh, dv), r_fin


def complex_affine_combine(e1, e2):
    a1r, a1i, b1r, b1i = e1
    a2r, a2i, b2r, b2i = e2
    return (a2r * a1r - a2i * a1i, a2r * a1i + a2i * a1r,
            a2r * b1r - a2i * b1i + b2r, a2r * b1i + a2i * b1r + b2i)


def s5_scan_dir(ug, lam_re, lam_im, log_dt, b_re, b_im, c_re, c_im, h0_re, h0_im):
    dt = jnp.exp(log_dt.astype(F32))[:, None]
    lr = lam_re.astype(F32)
    li = lam_im.astype(F32)
    mag = jnp.exp(lr * dt)
    ab_re = mag * jnp.cos(li * dt)
    ab_im = mag * jnp.sin(li * dt)
    den = lr * lr + li * li
    nr = ab_re - 1.0
    f_re = (nr * lr + ab_im * li) / den
    f_im = (ab_im * lr - nr * li) / den
    br = b_re.astype(F32)
    bi = b_im.astype(F32)
    bb_re = f_re[..., None] * br - f_im[..., None] * bi
    bb_im = f_re[..., None] * bi + f_im[..., None] * br
    bu_re = jnp.einsum('gps,btgs->btgp', bb_re, ug)
    bu_im = jnp.einsum('gps,btgs->btgp', bb_im, ug)
    h0r = h0_re.astype(F32)
    h0i = h0_im.astype(F32)
    bu_re = bu_re.at[:, 0].add(ab_re * h0r - ab_im * h0i)
    bu_im = bu_im.at[:, 0].add(ab_re * h0i + ab_im * h0r)
    a_re = jnp.broadcast_to(ab_re, bu_re.shape)
    a_im = jnp.broadcast_to(ab_im, bu_im.shape)
    _, _, h_re, h_im = lax.associative_scan(complex_affine_combine, (a_re, a_im, bu_re, bu_im), axis=1)
    y = (jnp.einsum('gsp,btgp->btgs', c_re.astype(F32), h_re)
         - jnp.einsum('gsp,btgp->btgs', c_im.astype(F32), h_im))
    return y, h_re[:, -1], h_im[:, -1]


def s5_mixer(u, lam_re, lam_im, log_dt, b_re, b_im, c_re, c_im, d_skip, w_glu, h0_re, h0_im):
    bsz, t = u.shape[0], u.shape[1]
    ug = u.astype(F32).reshape(bsz, t, G_C, S5_GROUP)
    y_f, fr, fi = s5_scan_dir(ug, lam_re[0], lam_im[0], log_dt[0], b_re[0], b_im[0],
                              c_re[0], c_im[0], h0_re[:, 0], h0_im[:, 0])
    y_b, br, bi = s5_scan_dir(jnp.flip(ug, 1), lam_re[1], lam_im[1], log_dt[1], b_re[1], b_im[1],
                              c_re[1], c_im[1], h0_re[:, 1], h0_im[:, 1])
    y = (y_f + jnp.flip(y_b, 1)).reshape(bsz, t, MIX_C) + d_skip.astype(F32) * u.astype(F32)
    z = jax.nn.gelu(y)
    out = z * jax.nn.sigmoid(z @ w_glu.astype(F32))
    return out.astype(u.dtype), jnp.stack([fr, br], axis=1), jnp.stack([fi, bi], axis=1)


def mixer_ab(h, prm, lam_init, cache):
    (w_in, w_out, qk_gain, lambda_qk, subln_g, mu, k_k, k_a, r_k,
     w0, w_up, a0, a_up, g_up, ln_w, ln_b) = prm
    bsz, t = h.shape[0], h.shape[1]
    qa, ka, va, pb = split_cols(h @ w_in, [A_QK, A_QK, A_V, B_COLS])
    q = rms_norm(qa.reshape(bsz, t, H_A, 2, HD_A), qk_gain[0])
    k = rms_norm(ka.reshape(bsz, t, H_A, 2, HD_A), qk_gain[1])
    v = va.reshape(bsz, t, H_A, VD_A)
    if cache is None:
        k_all, v_all = k, v
        s0 = jnp.zeros((bsz, 2, H_B, HD_B, HD_B), F32)
    else:
        cos, sin = axial_rope(t, HD_A)
        q = apply_rope(q, cos, sin)
        k = apply_rope(k, cos, sin)
        k_all = jnp.concatenate([cache[0].astype(k.dtype), k], axis=1)
        v_all = jnp.concatenate([cache[1].astype(v.dtype), v], axis=1)
        s0 = cache[2]
    lv = lambda_qk.astype(F32)
    lam = jnp.exp(jnp.sum(lv[0] * lv[1])) - jnp.exp(jnp.sum(lv[2] * lv[3])) + lam_init
    o_a = diff_attention(q, k_all, v_all, lam, subln_g, lam_init)
    r, kb, vb, xw, xa, xg = split_cols(centred_shift(pb, mu), [MIX_B, MIX_B, MIX_B, LORA_W, LORA_A, LORA_G])
    hs = (bsz, t, H_B, HD_B)
    r = r.reshape(hs)
    vb = vb.reshape(hs)
    kb = kb.reshape(hs)
    kk = (kb * k_k.reshape(H_B, HD_B)).astype(F32)
    kk = kk * lax.rsqrt(jnp.sum(kk * kk, axis=-1, keepdims=True) + 1e-12)
    a = jax.nn.sigmoid((a0 + xa @ a_up).astype(F32)).reshape(hs)
    kb = kb.astype(F32) * (1.0 + (a - 1.0) * k_a.reshape(H_B, HD_B).astype(F32))
    lora_w = jnp.tanh(xw)
    ys, fins = [], []
    for dr in range(2):
        logw = -jax.nn.softplus(-(w0[dr] + lora_w @ w_up[dr]).astype(F32)) - 0.5
        decay = jnp.exp(-jnp.exp(logw)).reshape(hs)
        seqs = (r, decay, kb, vb, kk, a)
        if dr == 1:
            seqs = tuple(jnp.flip(z, 1) for z in seqs)
        y_d, s_fin = rwkv7_scan(*seqs, s0[:, dr])
        ys.append(y_d if dr == 0 else jnp.flip(y_d, 1))
        fins.append(s_fin)
    gate = jax.nn.sigmoid(xg) @ g_up
    bonus = jnp.sum(r.astype(F32) * kb * r_k.astype(F32), axis=-1, keepdims=True) * vb.astype(F32)
    o_b = ((head_group_norm(ys[0] + ys[1], ln_w, ln_b, RWKV_GN_EPS)
            + bonus.reshape(bsz, t, MIX_B)) * gate.astype(F32)).astype(h.dtype)
    out = jnp.concatenate([o_a, o_b], axis=-1) @ w_out
    new = (k, v, jnp.stack(fins, axis=1).astype(h.dtype)) if cache is None else None
    return out, new


def mixer_cd(h, prm, cache):
    (w_in, w_out, lam_re, lam_im, log_dt, b_re, b_im, c_re, c_im, d_skip, w_glu,
     decay_logit, ln_w, ln_b) = prm
    bsz, t = h.shape[0], h.shape[1]
    u, q, k, v, g = split_cols(h @ w_in, [MIX_C, MIX_D, MIX_D, MIX_D, MIX_D])
    if cache is None:
        h0_re = jnp.zeros((bsz, 2, G_C, P_C), F32)
        h0_im = jnp.zeros((bsz, 2, G_C, P_C), F32)
        r0 = jnp.zeros((bsz, 2, H_D, HD_D, HD_D), F32)
    else:
        h0_re, h0_im, r0 = cache
    o_c, s_re, s_im = s5_mixer(u, lam_re, lam_im, log_dt, b_re, b_im, c_re, c_im, d_skip, w_glu, h0_re, h0_im)
    hs = (bsz, t, H_D, HD_D)
    q = q.reshape(hs)
    k = k.reshape(hs) * (HD_D ** -0.5)
    v = v.reshape(hs)
    if cache is not None:
        cos, sin = axial_rope(t, HD_D)
        q = apply_rope(q, cos, sin)
        k = apply_rope(k, cos, sin)
    log_gamma = jax.nn.log_sigmoid(decay_logit.astype(F32))
    o_f, r_f = retention_dir(q, k, v, log_gamma[0], r0[:, 0], True)
    o_b, r_b = retention_dir(jnp.flip(q, 1), jnp.flip(k, 1), jnp.flip(v, 1), log_gamma[1], r0[:, 1], False)
    ret = head_group_norm(o_f + jnp.flip(o_b, 1), ln_w, ln_b, RET_GN_EPS)
    o_d = (ret * jax.nn.silu(g.astype(F32))).astype(h.dtype)
    out = jnp.concatenate([o_c, o_d], axis=-1) @ w_out
    new = ((s_re.astype(h.dtype), s_im.astype(h.dtype), jnp.stack([r_f, r_b], axis=1).astype(h.dtype))
           if cache is None else None)
    return out, new


def trunk(x, cvec, caches, shared, ab_params, cd_params):
    norm1_g, norm2_g, w_mod, b_mod, w_ff_gate, w_ff_up, w_ff_down = shared
    new_ab, new_cd = [], []
    for i in range(DEPTH):
        j = i // 2
        sh1, sc1, g1, sh2, sc2, g2 = modulation(cvec, w_mod[i], b_mod[i])
        hm = rms_norm(x, norm1_g[i]) * (1.0 + sc1) + sh1
        if i % 2 == 0:
            cache_i = None if caches is None else (caches[0][:, j], caches[1][:, j], caches[2][:, j])
            o, new = mixer_ab(hm, tuple(p[j] for p in ab_params), 0.8 - 0.6 * math.exp(-0.3 * i), cache_i)
            new_ab.append(new)
        else:
            cache_i = None if caches is None else (caches[3][:, j], caches[4][:, j], caches[5][:, j])
            o, new = mixer_cd(hm, tuple(p[j] for p in cd_params), cache_i)
            new_cd.append(new)
        x = x + g1 * o
        hf = rms_norm(x, norm2_g[i]) * (1.0 + sc2) + sh2
        x = x + g2 * swiglu(hf, w_ff_gate[i], w_ff_up[i], w_ff_down[i])
    return x, new_ab, new_cd


def setup_inputs(seed: int = 0) -> dict:
    key = jax.random.key(seed)
    ks = iter(jax.random.split(key, 64))

    def nrm(shape, scale=1.0):
        return jax.random.normal(next(ks), shape, F32) * scale

    def gain(shape):
        return 1.0 + nrm(shape, 0.02)

    d = D_MODEL
    w0_base = jnp.linspace(-6.0, -1.0, MIX_B, dtype=F32)
    eps_ret = 2.0 ** (-5.0 - np.arange(H_D, dtype=np.float32))
    ret_logit0 = jnp.asarray(np.log((1.0 - eps_ret) / eps_ret), F32)
    lam_im0 = jnp.pi * jnp.arange(P_C, dtype=F32)
    return {
        'x_prompt': nrm((BATCH, SEQ, d)),
        'x_sample': nrm((DEC_BATCH, DEC_SEQ, d)),
        'cache_k_ab': nrm((DEC_BATCH, N_AB, PAST_LEN, H_A, 2, HD_A)),
        'cache_v_ab': nrm((DEC_BATCH, N_AB, PAST_LEN, H_A, VD_A)),
        'state_rwkv': nrm((DEC_BATCH, N_AB, 2, H_B, HD_B, HD_B), 0.5),
        'state_s5_re': nrm((DEC_BATCH, N_CD, 2, G_C, P_C), 0.1),
        'state_s5_im': nrm((DEC_BATCH, N_CD, 2, G_C, P_C), 0.1),
        'state_ret': nrm((DEC_BATCH, N_CD, 2, H_D, HD_D, HD_D), 0.5),
        'c': nrm((DEC_BATCH, d)),
        'c_ctx': nrm((d,)),
        'norm1_g': gain((DEPTH, d)),
        'norm2_g': gain((DEPTH, d)),
        'w_mod': nrm((DEPTH, d, 6 * d), 0.5 * d ** -0.5),
        'b_mod': nrm((DEPTH, 6 * d), 0.01),
        'w_ff_gate': nrm((DEPTH, d, D_FF), d ** -0.5),
        'w_ff_up': nrm((DEPTH, d, D_FF), d ** -0.5),
        'w_ff_down': nrm((DEPTH, D_FF, d), D_FF ** -0.5),
        'w_in_ab': nrm((N_AB, d, IN_AB), d ** -0.5),
        'w_out_ab': nrm((N_AB, OUT_AB, d), OUT_AB ** -0.5),
        'qk_gain_a': gain((N_AB, 2, HD_A)),
        'lambda_qk': nrm((N_AB, 4, HD_A), 0.1),
        'subln_g': gain((N_AB, VD_A)),
        'rwkv_mu': jax.random.uniform(next(ks), (N_AB, B_COLS), F32),
        'rwkv_k_k': 0.85 + nrm((N_AB, MIX_B), 0.02),
        'rwkv_k_a': 1.0 + nrm((N_AB, MIX_B), 0.02),
        'rwkv_r_k': nrm((N_AB, H_B, HD_B), 0.1),
        'rwkv_w0': w0_base + nrm((N_AB, 2, MIX_B), 0.1),
        'rwkv_w_up': nrm((N_AB, 2, LORA_W, MIX_B), 0.1 * LORA_W ** -0.5),
        'rwkv_a0': nrm((N_AB, MIX_B), 0.1),
        'rwkv_a_up': nrm((N_AB, LORA_A, MIX_B), 0.3 * LORA_A ** -0.5),
        'rwkv_g_up': nrm((N_AB, LORA_G, MIX_B), LORA_G ** -0.5),
        'rwkv_ln_w': gain((N_AB, MIX_B)),
        'rwkv_ln_b': nrm((N_AB, MIX_B), 0.01),
        'w_in_cd': nrm((N_CD, d, IN_CD), d ** -0.5),
        'w_out_cd': nrm((N_CD, OUT_CD, d), OUT_CD ** -0.5),
        's5_lam_re': -0.5 + nrm((N_CD, 2, G_C, P_C), 0.01),
        's5_lam_im': lam_im0 + nrm((N_CD, 2, G_C, P_C), 0.01),
        's5_log_dt': jax.random.uniform(next(ks), (N_CD, 2, G_C), F32, math.log(1e-3), math.log(1e-1)),
        's5_b_re': nrm((N_CD, 2, G_C, P_C, S5_GROUP), (2 * S5_GROUP) ** -0.5),
        's5_b_im': nrm((N_CD, 2, G_C, P_C, S5_GROUP), (2 * S5_GROUP) ** -0.5),
        's5_c_re': nrm((N_CD, 2, G_C, S5_GROUP, P_C), P_C ** -0.5),
        's5_c_im': nrm((N_CD, 2, G_C, S5_GROUP, P_C), P_C ** -0.5),
        's5_d': nrm((N_CD, MIX_C)),
        's5_w_glu': nrm((N_CD, MIX_C, MIX_C), MIX_C ** -0.5),
        'ret_decay_logit': ret_logit0 + nrm((N_CD, 2, H_D), 0.05),
        'ret_ln_w': gain((N_CD, MIX_D)),
        'ret_ln_b': nrm((N_CD, MIX_D), 0.01),
    }


def reference(x_prompt, x_sample, cache_k_ab, cache_v_ab, state_rwkv, state_s5_re, state_s5_im, state_ret,
              c, c_ctx, norm1_g, norm2_g, w_mod, b_mod, w_ff_gate, w_ff_up, w_ff_down,
              w_in_ab, w_out_ab, qk_gain_a, lambda_qk, subln_g, rwkv_mu, rwkv_k_k, rwkv_k_a, rwkv_r_k,
              rwkv_w0, rwkv_w_up, rwkv_a0, rwkv_a_up, rwkv_g_up, rwkv_ln_w, rwkv_ln_b,
              w_in_cd, w_out_cd, s5_lam_re, s5_lam_im, s5_log_dt, s5_b_re, s5_b_im, s5_c_re, s5_c_im,
              s5_d, s5_w_glu, ret_decay_logit, ret_ln_w, ret_ln_b):
    shared = (norm1_g, norm2_g, w_mod, b_mod, w_ff_gate, w_ff_up, w_ff_down)
    ab_params = (w_in_ab, w_out_ab, qk_gain_a, lambda_qk, subln_g, rwkv_mu, rwkv_k_k, rwkv_k_a, rwkv_r_k,
                 rwkv_w0, rwkv_w_up, rwkv_a0, rwkv_a_up, rwkv_g_up, rwkv_ln_w, rwkv_ln_b)
    cd_params = (w_in_cd, w_out_cd, s5_lam_re, s5_lam_im, s5_log_dt, s5_b_re, s5_b_im, s5_c_re, s5_c_im,
                 s5_d, s5_w_glu, ret_decay_logit, ret_ln_w, ret_ln_b)
    y_prompt, ctx_ab, ctx_cd = trunk(x_prompt, c_ctx[None, :], None, shared, ab_params, cd_params)
    caches = (cache_k_ab, cache_v_ab, state_rwkv, state_s5_re, state_s5_im, state_ret)
    y_sample, _, _ = trunk(x_sample, c, caches, shared, ab_params, cd_params)
    new_k_ab = jnp.stack([e[0] for e in ctx_ab], axis=1)
    new_v_ab = jnp.stack([e[1] for e in ctx_ab], axis=1)
    new_rwkv = jnp.stack([e[2] for e in ctx_ab], axis=1)
    new_s5_re = jnp.stack([e[0] for e in ctx_cd], axis=1)
    new_s5_im = jnp.stack([e[1] for e in ctx_cd], axis=1)
    new_ret = jnp.stack([e[2] for e in ctx_cd], axis=1)
    return (y_prompt, y_sample, new_k_ab, new_v_ab, new_rwkv, new_s5_re, new_s5_im, new_ret)
```

```python
import functools
import math

import numpy as np
import jax
import jax.numpy as jnp
from jax import lax
from jax.experimental import pallas as pl
from jax.experimental.pallas import tpu as pltpu

F32 = jnp.float32
BF16 = jnp.bfloat16
HIGHEST = lax.Precision.HIGHEST

D_MODEL = 1024
BATCH = 32
SEQ = 256
DEC_BATCH = 2
DEC_SEQ = 1024
PAST_LEN = 256
GRID_W = 64
H_A = 4
HD_A = 64
VD_A = 128
H_B = 8
HD_B = 64
MIX = 512
LORA_W = 64
LORA_A = 64
LORA_G = 128
S5_GROUP = 16
G_C = 32
P_C = 64
S5_STATE = G_C * P_C
H_D = 4
HD_D = 128
CHUNK = 128
D_FF = 2816
IN_AB = 3328
IN_CD = 2560
ROPE_THETA = 10000.0
NORM_EPS = 1e-6
RWKV_GN_EPS = 64e-5
RET_GN_EPS = 1e-5

M_CTX = BATCH * SEQ
M_LAT = DEC_BATCH * DEC_SEQ
M_ALL = M_CTX + M_LAT
MOD_ROWS = 8
RWKV_CHUNK = 64
VMEM_LIMIT = 56 * 1024 * 1024

NN = (((1,), (0,)), ((), ()))
NT = (((1,), (1,)), ((), ()))
TN = (((0,), (0,)), ((), ()))


def _params(*sem):
    return pltpu.CompilerParams(dimension_semantics=sem, vmem_limit_bytes=VMEM_LIMIT)


def _bdot(a, b, dims=NN):
    return lax.dot_general(a.astype(BF16), b.astype(BF16), dims, preferred_element_type=F32)


def _hdot(a, b, dims=NN):
    return lax.dot_general(a, b, dims, precision=HIGHEST, preferred_element_type=F32)


def _split_dot(x, m):
    hi = x.astype(BF16)
    lo = (x - hi.astype(F32)).astype(BF16)
    return (jnp.dot(hi, m, preferred_element_type=F32) + jnp.dot(lo, m, preferred_element_type=F32))


def _seg_matrix(n, shift, val):
    r = lax.broadcasted_iota(jnp.int32, (n, n), 0) >> shift
    c = lax.broadcasted_iota(jnp.int32, (n, n), 1) >> shift
    return jnp.where(r == c, val, 0.0).astype(BF16)


def _sigmoid(x):
    return jax.nn.sigmoid(x)


def _silu(x):
    return x * jax.nn.sigmoid(x)


def _softplus(x):
    return jnp.maximum(x, 0.0) + jnp.log(1.0 + jnp.exp(-jnp.abs(x)))


def _mod_row(tile, tm):
    r0 = tile * tm
    return jnp.where(r0 < M_CTX, 0, 1 + (r0 - M_CTX) // DEC_SEQ)


def _norm_mod(x, g, sc_ref, sh_ref, row):
    y = x * lax.rsqrt(jnp.mean(x * x, axis=-1, keepdims=True) + NORM_EPS) * g
    return y * (1.0 + sc_ref[pl.ds(row, 1), :]) + sh_ref[pl.ds(row, 1), :]


def _mod_kernel(c_ref, w_ref, b_ref, o_ref):
    o_ref[0] = _hdot(_silu(c_ref[...]), w_ref[0]) + b_ref[0]


def _modulation(cvec, w_mod, b_mod):
    depth, d, n6 = w_mod.shape
    tn = 1536
    return pl.pallas_call(
        _mod_kernel,
        grid=(depth, n6 // tn),
        in_specs=[pl.BlockSpec((MOD_ROWS, d), lambda l, j: (0, 0)),
                  pl.BlockSpec((1, d, tn), lambda l, j: (l, 0, j)),
                  pl.BlockSpec((1, 1, tn), lambda l, j: (l, 0, j))],
        out_specs=pl.BlockSpec((1, MOD_ROWS, tn), lambda l, j: (l, 0, j)),
        out_shape=jax.ShapeDtypeStruct((depth, MOD_ROWS, n6), F32),
        compiler_params=_params("parallel", "parallel"),
        name="modulation",
    )(cvec, w_mod, b_mod.reshape(depth, 1, n6))


def _norm_linear_kernel(x_ref, g_ref, sc_ref, sh_ref, w_ref, o_ref, *, tm):
    row = _mod_row(pl.program_id(0), tm)
    h = _norm_mod(x_ref[...], g_ref[...], sc_ref, sh_ref, row)
    o_ref[...] = jnp.dot(h.astype(BF16), w_ref[...], preferred_element_type=F32)


def _norm_linear(x, g, mods, w_bf16):
    tm = 256
    m, d = x.shape
    n = w_bf16.shape[1]
    return pl.pallas_call(
        functools.partial(_norm_linear_kernel, tm=tm),
        grid=(m // tm,),
        in_specs=[pl.BlockSpec((tm, d), lambda i: (i, 0)),
                  pl.BlockSpec((1, d), lambda i: (0, 0)),
                  pl.BlockSpec((MOD_ROWS, d), lambda i: (0, 1)),
                  pl.BlockSpec((MOD_ROWS, d), lambda i: (0, 0)),
                  pl.BlockSpec((d, n), lambda i: (0, 0))],
        out_specs=pl.BlockSpec((tm, n), lambda i: (i, 0)),
        out_shape=jax.ShapeDtypeStruct((m, n), F32),
        compiler_params=_params("parallel"),
        name="norm_linear",
    )(x, g.reshape(1, d), mods, mods, w_bf16)


def _proj_res_kernel(x_ref, a_ref, b_ref, wa_ref, wb_ref, gt_ref, o_ref, *, tm):
    row = _mod_row(pl.program_id(0), tm)
    o = (jnp.dot(a_ref[...].astype(BF16), wa_ref[...], preferred_element_type=F32)
         + jnp.dot(b_ref[...].astype(BF16), wb_ref[...], preferred_element_type=F32))
    o_ref[...] = x_ref[...] + gt_ref[pl.ds(row, 1), :] * o


def _proj_residual(x, oa, ob, w_bf16, mods):
    tm = 256
    m, d = x.shape
    return pl.pallas_call(
        functools.partial(_proj_res_kernel, tm=tm),
        grid=(m // tm,),
        in_specs=[pl.BlockSpec((tm, d), lambda i: (i, 0)),
                  pl.BlockSpec((tm, MIX), lambda i: (i, 0)),
                  pl.BlockSpec((tm, MIX), lambda i: (i, 0)),
                  pl.BlockSpec((MIX, d), lambda i: (0, 0)),
                  pl.BlockSpec((MIX, d), lambda i: (1, 0)),
                  pl.BlockSpec((MOD_ROWS, d), lambda i: (0, 2))],
        out_specs=pl.BlockSpec((tm, d), lambda i: (i, 0)),
        out_shape=jax.ShapeDtypeStruct((m, d), F32),
        compiler_params=_params("parallel"),
        name="proj_residual",
    )(x, oa, ob, w_bf16, w_bf16, mods)


def _ffn_kernel(x_ref, g_ref, sc_ref, sh_ref, gt_ref, wg_ref, wu_ref, wd_ref, o_ref, *, tm, ck):
    row = _mod_row(pl.program_id(0), tm)
    x = x_ref[...]
    h = _norm_mod(x, g_ref[...], sc_ref, sh_ref, row).astype(BF16)
    acc = jnp.zeros((tm, D_MODEL), F32)
    for c in range(D_FF // ck):
        gg = jnp.dot(h, wg_ref[:, c * ck:(c + 1) * ck], preferred_element_type=F32)
        uu = jnp.dot(h, wu_ref[:, c * ck:(c + 1) * ck], preferred_element_type=F32)
        act = (_silu(gg) * uu).astype(BF16)
        acc = acc + jnp.dot(act, wd_ref[c * ck:(c + 1) * ck, :], preferred_element_type=F32)
    o_ref[...] = x + gt_ref[pl.ds(row, 1), :] * acc


def _ffn(x, g, mods, wg, wu, wd):
    tm, ck = 512, 256
    m, d = x.shape
    return pl.pallas_call(
        functools.partial(_ffn_kernel, tm=tm, ck=ck),
        grid=(m // tm,),
        in_specs=[pl.BlockSpec((tm, d), lambda i: (i, 0)),
                  pl.BlockSpec((1, d), lambda i: (0, 0)),
                  pl.BlockSpec((MOD_ROWS, d), lambda i: (0, 4)),
                  pl.BlockSpec((MOD_ROWS, d), lambda i: (0, 3)),
                  pl.BlockSpec((MOD_ROWS, d), lambda i: (0, 5)),
                  pl.BlockSpec((d, D_FF), lambda i: (0, 0)),
                  pl.BlockSpec((d, D_FF), lambda i: (0, 0)),
                  pl.BlockSpec((D_FF, d), lambda i: (0, 0))],
        out_specs=pl.BlockSpec((tm, d), lambda i: (i, 0)),
        out_shape=jax.ShapeDtypeStruct((m, d), F32),
        compiler_params=_params("parallel"),
        name="ffn",
    )(x, g.reshape(1, d), mods, mods, mods, wg, wu, wd)


def _qk_norm(x, gain, segm):
    ms = _split_dot(x * x, segm)
    return x * lax.rsqrt(ms + NORM_EPS) * gain


def _rope_pairs(x, cosf, sinf, half):
    lane = lax.broadcasted_iota(jnp.int32, x.shape, 1)
    first = (lane & (2 * half - 1)) < half
    n = x.shape[1]
    partner = jnp.where(first, pltpu.roll(x, n - half, axis=1), pltpu.roll(x, half, axis=1))
    return x * cosf + partner * sinf


def _attn_kernel(*refs, latent, lam_init):
    if latent:
        (q_ref, k_ref, v_ref, ck_ref, cv_ref, cosq_ref, sinq_ref, cosk_ref, sink_ref,
         gain_ref, lam_ref, sub_ref, o_ref) = refs
    else:
        q_ref, k_ref, v_ref, gain_ref, lam_ref, sub_ref, o_ref, kn_ref = refs
    segm = _seg_matrix(2 * HD_A, 6, 1.0 / HD_A)
    gains = gain_ref[...]
    q = _qk_norm(q_ref[...], gains[0:1], segm)
    k = _qk_norm(k_ref[...], gains[1:2], segm)
    v = v_ref[...]
    if latent:
        q = _rope_pairs(q, cosq_ref[...], sinq_ref[...], HD_A // 2)
        k = _rope_pairs(k, cosk_ref[...], sink_ref[...], HD_A // 2)
        k = jnp.concatenate([ck_ref[...], k], axis=0)
        v = jnp.concatenate([cv_ref[...], v], axis=0)
    else:
        kn_ref[...] = k
    lv = lam_ref[...]
    lam = (jnp.exp(jnp.sum(lv[0:1] * lv[1:2], axis=1, keepdims=True))
           - jnp.exp(jnp.sum(lv[2:3] * lv[3:4], axis=1, keepdims=True)) + lam_init)
    scale = HD_A ** -0.5
    probs = []
    for c in range(2):
        s = _bdot(q[:, c * HD_A:(c + 1) * HD_A], k[:, c * HD_A:(c + 1) * HD_A], NT) * scale
        e = jnp.exp(s - jnp.max(s, axis=-1, keepdims=True))
        probs.append(e / jnp.sum(e, axis=-1, keepdims=True))
    att = probs[0] - lam * probs[1]
    o = _bdot(att, v)
    o = o * lax.rsqrt(jnp.mean(o * o, axis=-1, keepdims=True) + NORM_EPS) * sub_ref[...]
    o_ref[...] = o * (1.0 - lam_init)


def _attention(p, gain2, lambda_qk, subln_g, lam_init, latent, cache_k=None, cache_v=None, rope=None):
    w = 2 * HD_A
    if latent:
        nseq, t, tq, rb0 = DEC_BATCH, DEC_SEQ, 256, M_CTX // DEC_SEQ
    else:
        nseq, t, tq, rb0 = BATCH, SEQ, SEQ, 0
    nq = t // tq
    qoff = rb0 * nq
    in_specs = [pl.BlockSpec((tq, w), lambda b, h, i: (qoff + b * nq + i, h)),
                pl.BlockSpec((t, w), lambda b, h, i: (rb0 + b, H_A + h)),
                pl.BlockSpec((t, w), lambda b, h, i: (rb0 + b, 2 * H_A + h))]
    args = [p, p, p]
    if latent:
        cosf, sinf = rope
        in_specs += [pl.BlockSpec((PAST_LEN, w), lambda b, h, i: (b, h)),
                     pl.BlockSpec((PAST_LEN, w), lambda b, h, i: (b, h)),
                     pl.BlockSpec((tq, w), lambda b, h, i: (i, 0)),
                     pl.BlockSpec((tq, w), lambda b, h, i: (i, 0)),
                     pl.BlockSpec((t, w), lambda b, h, i: (0, 0)),
                     pl.BlockSpec((t, w), lambda b, h, i: (0, 0))]
        args += [cache_k, cache_v, cosf, sinf, cosf, sinf]
    in_specs += [pl.BlockSpec((2, w), lambda b, h, i: (0, 0)),
                 pl.BlockSpec((4, HD_A), lambda b, h, i: (0, 0)),
                 pl.BlockSpec((1, w), lambda b, h, i: (0, 0))]
    args += [gain2, lambda_qk, subln_g.reshape(1, w)]
    o_spec = pl.BlockSpec((tq, w), lambda b, h, i: (b * nq + i, h))
    o_shape = jax.ShapeDtypeStruct((nseq * t, MIX), F32)
    if latent:
        out_specs, out_shape = o_spec, o_shape
    else:
        out_specs = [o_spec, pl.BlockSpec((t, w), lambda b, h, i: (b, h))]
        out_shape = [o_shape, jax.ShapeDtypeStruct((nseq * t, MIX), F32)]
    return pl.pallas_call(
        functools.partial(_attn_kernel, latent=latent, lam_init=lam_init),
        grid=(nseq, H_A, nq),
        in_specs=in_specs, out_specs=out_specs, out_shape=out_shape,
        compiler_params=_params("parallel", "parallel", "arbitrary"),
        name="diff_attention_lat" if latent else "diff_attention_ctx",
    )(*args)


def _centred_shift(x, mu):
    t = x.shape[0]
    row = lax.broadcasted_iota(jnp.int32, x.shape, 0)
    prev = jnp.where(row == 0, 0.0, pltpu.roll(x, 1, axis=0))
    nxt = jnp.where(row == t - 1, 0.0, pltpu.roll(x, t - 1, axis=0))
    return x + (0.5 * (prev + nxt) - x) * mu


def _seg_sum(x, segm):
    return jnp.concatenate([_split_dot(x[:, j * 128:(j + 1) * 128], segm) for j in range(x.shape[1] // 128)],
                           axis=1)


def _rwkv_prep_kernel(r_ref, k_ref, v_ref, l_ref, mur_ref, muk_ref, muv_ref, mul_ref,
                      kk_ref, ka_ref, rk_ref, w0_ref, wup_ref, a0_ref, aup_ref, gup_ref,
                      ro_ref, ldf_ref, ldb_ref, kbo_ref, vbo_ref, kko_ref, ao_ref, gate_ref, bonus_ref):
    seg1 = _seg_matrix(128, 6, 1.0)
    r = _centred_shift(r_ref[...], mur_ref[...])
    kb = _centred_shift(k_ref[...], muk_ref[...])
    vb = _centred_shift(v_ref[...], muv_ref[...])
    lo = _centred_shift(l_ref[...], mul_ref[...])
    xw = lo[:, 0:LORA_W]
    xa = lo[:, LORA_W:LORA_W + LORA_A]
    xg = lo[:, LORA_W + LORA_A:]
    kk = kb * kk_ref[...]
    kk = kk * lax.rsqrt(_seg_sum(kk * kk, seg1) + 1e-12)
    a = _sigmoid(a0_ref[...] + _bdot(xa, aup_ref[...]))
    kb2 = kb * (1.0 + (a - 1.0) * ka_ref[...])
    lw = jnp.tanh(xw)
    for dr, ld_ref in enumerate((ldf_ref, ldb_ref)):
        z = w0_ref[dr:dr + 1, :] + _bdot(lw, wup_ref[dr])
        logw = -_softplus(-z) - 0.5
        ld_ref[...] = -jnp.exp(logw)
    gate_ref[...] = _bdot(_sigmoid(xg), gup_ref[...])
    bonus_ref[...] = _seg_sum(r * kb2 * rk_ref[...], seg1) * vb
    ro_ref[...] = r
    kbo_ref[...] = kb2
    vbo_ref[...] = vb
    kko_ref[...] = kk
    ao_ref[...] = a


def _rwkv_prep(p, prm, latent):
    mu, k_k, k_a, r_k, w0, w_up, a0, a_up, g_up = prm
    if latent:
        nseq, t, rb0 = DEC_BATCH, DEC_SEQ, M_CTX // DEC_SEQ
    else:
        nseq, t, rb0 = BATCH, SEQ, 0
    lw = LORA_W + LORA_A + LORA_G
    c0 = (IN_AB - 3 * MIX - lw) // MIX
    cl = (IN_AB - lw) // lw
    row = lambda n: pl.BlockSpec((1, n), lambda b: (0, 0))
    in_specs = [pl.BlockSpec((t, MIX), lambda b: (rb0 + b, c0)),
                pl.BlockSpec((t, MIX), lambda b: (rb0 + b, c0 + 1)),
                pl.BlockSpec((t, MIX), lambda b: (rb0 + b, c0 + 2)),
                pl.BlockSpec((t, lw), lambda b: (rb0 + b, cl)),
                row(MIX), row(MIX), row(MIX), row(lw),
                row(MIX), row(MIX), row(MIX),
                pl.BlockSpec((2, MIX), lambda b: (0, 0)),
                pl.BlockSpec((2, LORA_W, MIX), lambda b: (0, 0, 0)),
                row(MIX),
                pl.BlockSpec((LORA_A, MIX), lambda b: (0, 0)),
                pl.BlockSpec((LORA_G, MIX), lambda b: (0, 0))]
    o_spec = pl.BlockSpec((t, MIX), lambda b: (b, 0))
    o_shape = jax.ShapeDtypeStruct((nseq * t, MIX), F32)
    return pl.pallas_call(
        _rwkv_prep_kernel,
        grid=(nseq,),
        in_specs=in_specs, out_specs=[o_spec] * 9, out_shape=[o_shape] * 9,
        compiler_params=_params("parallel"),
        name="rwkv_prep_lat" if latent else "rwkv_prep_ctx",
    )(p, p, p, p,
      mu[None, 0:MIX], mu[None, MIX:2 * MIX], mu[None, 2 * MIX:3 * MIX], mu[None, 3 * MIX:],
      k_k.reshape(1, MIX), k_a.reshape(1, MIX), r_k.reshape(1, MIX), w0, w_up,
      a0.reshape(1, MIX), a_up, g_up)


def _rwkv_chunk(r, ld, kb, vb, kk, a, st, rev):
    c = r.shape[0]
    ti = lax.broadcasted_iota(jnp.int32, (c, c), 0)
    si = lax.broadcasted_iota(jnp.int32, (c, c), 1)
    incl = (si >= ti) if rev else (si <= ti)
    strict = (si > ti) if rev else (si < ti)
    lcum = _hdot(incl.astype(F32), ld)
    ltot = jnp.sum(ld, axis=0, keepdims=True)
    beta = kk * a
    abar = -kk * jnp.exp(lcum - ld)
    rbar = r * jnp.exp(lcum)
    eneg = jnp.exp(-lcum)
    bt = beta * eneg
    kt = kb * eneg
    erem = jnp.exp(ltot - lcum)
    a_ab = jnp.where(strict, _hdot(abar, bt, NT), 0.0)
    a_ak = jnp.where(strict, _hdot(abar, kt, NT), 0.0)
    a_rb = jnp.where(incl, _hdot(rbar, bt, NT), 0.0)
    a_rk = jnp.where(incl, _hdot(rbar, kt, NT), 0.0)
    u = _hdot(abar, st) + _hdot(a_ak, vb)
    apow = a_ab
    nlev = int(math.log2(c))
    for j in range(nlev):
        u = u + _hdot(apow, u)
        if j + 1 < nlev:
            apow = _hdot(apow, apow)
    y = _hdot(rbar, st) + _hdot(a_rb, u) + _hdot(a_rk, vb)
    kd = lax.broadcasted_iota(jnp.int32, st.shape, 0) == lax.broadcasted_iota(jnp.int32, st.shape, 1)
    dec = jnp.where(kd, jnp.broadcast_to(jnp.exp(ltot), st.shape), 0.0)
    st_new = _hdot(dec, st) + _hdot(beta * erem, u, TN) + _hdot(kb * erem, vb, TN)
    return y, st_new


def _rwkv_scan_kernel(r_ref, ldf_ref, ldb_ref, kb_ref, vb_ref, kk_ref, a_ref, bonus_ref, gate_ref,
                      lnw_ref, lnb_ref, s0_ref, o_ref, sfin_ref, yb_ref, *, t, c):
    n = t // c

    def body(ci, states):
        new_states = []
        for dr, (ld_ref, dst) in enumerate(((ldf_ref, o_ref), (ldb_ref, yb_ref))):
            cj = ci if dr == 0 else n - 1 - ci
            rows = pl.ds(pl.multiple_of(cj * c, c), c)
            r, ld, kb, vb, kk, a = (ref[rows, :] for ref in (r_ref, ld_ref, kb_ref, vb_ref, kk_ref, a_ref))
            ys = []
            for hd in range(2):
                sl = slice(hd * HD_B, (hd + 1) * HD_B)
                y, st = _rwkv_chunk(r[:, sl], ld[:, sl], kb[:, sl], vb[:, sl], kk[:, sl], a[:, sl],
                                    states[2 * dr + hd], rev=(dr == 1))
                ys.append(y)
                new_states.append(st)
            dst[rows, :] = jnp.concatenate(ys, axis=1)
        return tuple(new_states)

    init = tuple(s0_ref[0, dr, hd] for dr in range(2) for hd in range(2))
    fin = lax.fori_loop(0, n, body, init)
    for dr in range(2):
        for hd in range(2):
            sfin_ref[0, dr, hd] = fin[2 * dr + hd]
    segm = _seg_matrix(2 * HD_B, 6, 1.0 / HD_B)
    y = o_ref[...] + yb_ref[...]
    yc = y - _split_dot(y, segm)
    yn = yc * lax.rsqrt(_split_dot(yc * yc, segm) + RWKV_GN_EPS)
    o_ref[...] = (yn * lnw_ref[...] + lnb_ref[...] + bonus_ref[...]) * gate_ref[...]


def _rwkv_scan(prep, ln_w, ln_b, s0t, latent):
    r, ldf, ldb, kb, vb, kk, a, gate, bonus = prep
    nseq, t = (DEC_BATCH, DEC_SEQ) if latent else (BATCH, SEQ)
    w = 2 * HD_B
    blk = pl.BlockSpec((t, w), lambda b, h: (b, h))
    vec = pl.BlockSpec((1, w), lambda b, h: (0, h))
    st_spec = pl.BlockSpec((1, 2, 2, HD_B, HD_B), lambda b, h: (b, 0, h, 0, 0))
    return pl.pallas_call(
        functools.partial(_rwkv_scan_kernel, t=t, c=RWKV_CHUNK),
        grid=(nseq, H_B // 2),
        in_specs=[blk] * 9 + [vec, vec, st_spec],
        out_specs=[blk, st_spec],
        out_shape=[jax.ShapeDtypeStruct((nseq * t, MIX), F32),
                   jax.ShapeDtypeStruct((nseq, 2, H_B, HD_B, HD_B), F32)],
        scratch_shapes=[pltpu.VMEM((t, w), F32)],
        compiler_params=_params("parallel", "parallel"),
        name="rwkv_scan_lat" if latent else "rwkv_scan_ctx",
    )(r, ldf, ldb, kb, vb, kk, a, bonus, gate, ln_w.reshape(1, MIX), ln_b.reshape(1, MIX), s0t)


S5_KB = 2
S5_KW = MIX // S5_KB
S5_NW = S5_STATE // S5_KB


def _gelu_tanh(x):
    return 0.5 * x * (1.0 + jnp.tanh(math.sqrt(2.0 / math.pi) * (x + 0.044715 * (x * x * x))))


def _s5_kernel(u_ref, bbr_ref, bbi_ref, ccr_ref, cci_ref, lam_ref, h0r_ref, h0i_ref, d_ref, wglu_ref,
               o_ref, hfr_ref, hfi_ref, bur, bui, yacc, *, nseq, rows, cr):
    nch = rows // cr
    ngrp = cr // 8
    spg = 8 // nseq

    for dr in range(2):
        ar = lam_ref[dr, 0]
        ai = lam_ref[dr, 1]

        def chunk_body(ci, carry, dr=dr, ar=ar, ai=ai):
            cj = ci if dr == 0 else nch - 1 - ci
            r0 = pl.multiple_of(cj * cr, cr)
            ub = u_ref[0, pl.ds(r0, cr), :].astype(BF16)
            for kb in range(S5_KB):
                uk = ub[:, kb * S5_KW:(kb + 1) * S5_KW]
                bur[:, kb * S5_NW:(kb + 1) * S5_NW] = jnp.dot(uk, bbr_ref[dr, kb], preferred_element_type=F32)
                bui[:, kb * S5_NW:(kb + 1) * S5_NW] = jnp.dot(uk, bbi_ref[dr, kb], preferred_element_type=F32)

            def group_body(gi, hc):
                hr, hi = hc
                gj = gi if dr == 0 else ngrp - 1 - gi
                o8 = pl.multiple_of(gj * 8, 8)
                xr = bur[pl.ds(o8, 8), :]
                xi = bui[pl.ds(o8, 8), :]
                outr = [None] * spg
                outi = [None] * spg
                for s in (range(spg) if dr == 0 else range(spg - 1, -1, -1)):
                    nr = ar * hr - ai * hi + xr[s * nseq:(s + 1) * nseq]
                    ni = ar * hi + ai * hr + xi[s * nseq:(s + 1) * nseq]
                    hr, hi = nr, ni
                    outr[s] = hr
                    outi[s] = hi
                bur[pl.ds(o8, 8), :] = outr[0] if spg == 1 else jnp.concatenate(outr, axis=0)
                bui[pl.ds(o8, 8), :] = outi[0] if spg == 1 else jnp.concatenate(outi, axis=0)
                return hr, hi

            carry = lax.fori_loop(0, ngrp, group_body, carry)
            for kb in range(S5_KB):
                yk = (jnp.dot(bur[:, kb * S5_NW:(kb + 1) * S5_NW].astype(BF16), ccr_ref[dr, kb],
                              preferred_element_type=F32)
                      - jnp.dot(bui[:, kb * S5_NW:(kb + 1) * S5_NW].astype(BF16), cci_ref[dr, kb],
                                preferred_element_type=F32))
                cols = slice(kb * S5_KW, (kb + 1) * S5_KW)
                if dr == 0:
                    yacc[pl.ds(r0, cr), cols] = yk
                else:
                    yacc[pl.ds(r0, cr), cols] = yacc[pl.ds(r0, cr), cols] + yk
            return carry

        hr, hi = lax.fori_loop(0, nch, chunk_body, (h0r_ref[0, dr], h0i_ref[0, dr]))
        hfr_ref[0, dr] = hr
        hfi_ref[0, dr] = hi

    def out_body(ci, _):
        r0 = pl.multiple_of(ci * cr, cr)
        u = u_ref[0, pl.ds(r0, cr), :]
        z = _gelu_tanh(yacc[pl.ds(r0, cr), :] + d_ref[...] * u)
        o_ref[0, pl.ds(r0, cr), :] = z * _sigmoid(jnp.dot(z.astype(BF16), wglu_ref[...],
                                                          preferred_element_type=F32))
        return 0

    lax.fori_loop(0, nch, out_body, 0)


def _s5(u_tb, mats, h0r, h0i, d_skip, w_glu_bf16, nseq):
    bbr, bbi, ccr, cci, lam = mats
    groups, rows, _ = u_tb.shape
    cr = 256
    full = lambda shape: pl.BlockSpec(shape, lambda g: (0,) * len(shape))
    h_spec = pl.BlockSpec((1, 2, nseq, S5_STATE), lambda g: (g, 0, 0, 0))
    io_spec = pl.BlockSpec((1, rows, MIX), lambda g: (g, 0, 0))
    return pl.pallas_call(
        functools.partial(_s5_kernel, nseq=nseq, rows=rows, cr=cr),
        grid=(groups,),
        in_specs=[io_spec, full(bbr.shape), full(bbi.shape), full(ccr.shape), full(cci.shape),
                  full(lam.shape), h_spec, h_spec, full((1, MIX)), full((MIX, MIX))],
        out_specs=[io_spec, h_spec, h_spec],
        out_shape=[jax.ShapeDtypeStruct((groups, rows, MIX), F32),
                   jax.ShapeDtypeStruct((groups, 2, nseq, S5_STATE), F32),
                   jax.ShapeDtypeStruct((groups, 2, nseq, S5_STATE), F32)],
        scratch_shapes=[pltpu.VMEM((cr, S5_STATE), F32), pltpu.VMEM((cr, S5_STATE), F32),
                        pltpu.VMEM((rows, MIX), F32)],
        compiler_params=_params("parallel"),
        name="s5_lat" if nseq == DEC_BATCH else "s5_ctx",
    )(u_tb, bbr, bbi, ccr, cci, lam, h0r, h0i, d_skip.reshape(1, MIX), w_glu_bf16)


def _s5_matrices(lam_re, lam_im, log_dt, b_re, b_im, c_re, c_im):
    dt = jnp.exp(log_dt)[..., None]
    mag = jnp.exp(lam_re * dt)
    ab_re = mag * jnp.cos(lam_im * dt)
    ab_im = mag * jnp.sin(lam_im * dt)
    den = lam_re * lam_re + lam_im * lam_im
    nr = ab_re - 1.0
    f_re = (nr * lam_re + ab_im * lam_im) / den
    f_im = (ab_im * lam_re - nr * lam_im) / den
    bb_re = f_re[..., None] * b_re - f_im[..., None] * b_im
    bb_im = f_re[..., None] * b_im + f_im[..., None] * b_re
    gl = G_C // S5_KB
    eye = jnp.eye(gl, dtype=F32)

    def in_proj(bb):
        x = bb.reshape(2, S5_KB, gl, P_C, S5_GROUP)
        return jnp.einsum('dkgps,gh->dkgshp', x, eye).reshape(2, S5_KB, S5_KW, S5_NW).astype(BF16)

    def out_proj(cc):
        x = cc.reshape(2, S5_KB, gl, S5_GROUP, P_C)
        return jnp.einsum('dkgsp,gh->dkgphs', x, eye).reshape(2, S5_KB, S5_NW, S5_KW).astype(BF16)

    lam = jnp.stack([ab_re.reshape(2, 1, S5_STATE), ab_im.reshape(2, 1, S5_STATE)], axis=1)
    return in_proj(bb_re), in_proj(bb_im), out_proj(c_re), out_proj(c_im), lam


def _ret_kernel(*refs, t, latent):
    if latent:
        (q_ref, k_ref, v_ref, g_ref, dl_ref, lnw_ref, lnb_ref, r0_ref, cos_ref, sin_ref,
         o_ref, rfin_ref, ob_ref) = refs
    else:
        q_ref, k_ref, v_ref, g_ref, dl_ref, lnw_ref, lnb_ref, r0_ref, o_ref, rfin_ref, ob_ref = refs
    n = t // CHUNK
    lg = -_softplus(-dl_ref[0])
    jf = lax.broadcasted_iota(jnp.int32, (CHUNK, CHUNK), 0).astype(F32)
    kf = lax.broadcasted_iota(jnp.int32, (CHUNK, CHUNK), 1).astype(F32)
    diff = jf - kf
    tabs = []
    for dr in range(2):
        l = lg[dr:dr + 1]
        if dr == 0:
            dmat = jnp.where(diff >= 0, jnp.exp(l * jnp.maximum(diff, 0.0)), 0.0)
            xi = jnp.exp(l * (jf + 1.0))
            zeta = jnp.exp(l * (CHUNK - 1.0 - jf))
        else:
            dmat = jnp.where(diff < 0, jnp.exp(l * jnp.maximum(-diff, 0.0)), 0.0)
            xi = jnp.exp(l * (CHUNK - jf))
            zeta = jnp.exp(l * jf)
        tabs.append((dmat, xi, zeta, jnp.exp(l * CHUNK)))

    def body(ci, states):
        new_states = []
        for dr, dst in enumerate((o_ref, ob_ref)):
            dmat, xi, zeta, gch = tabs[dr]
            cj = ci if dr == 0 else n - 1 - ci
            rows = pl.ds(pl.multiple_of(cj * CHUNK, CHUNK), CHUNK)
            q = q_ref[rows, :]
            k = k_ref[rows, :] * (HD_D ** -0.5)
            v = v_ref[rows, :]
            if latent:
                q = _rope_pairs(q, cos_ref[rows, :], sin_ref[rows, :], HD_D // 2)
                k = _rope_pairs(k, cos_ref[rows, :], sin_ref[rows, :], HD_D // 2)
            st = states[dr]
            inner = _bdot(q, k, NT) * dmat
            dst[rows, :] = _bdot(inner, v) + _bdot(q, st) * xi
            new_states.append(st * gch + _bdot(k * zeta, v, TN))
        return tuple(new_states)

    fin = lax.fori_loop(0, n, body, (r0_ref[0, 0, 0], r0_ref[0, 1, 0]))
    rfin_ref[0, 0, 0] = fin[0]
    rfin_ref[0, 1, 0] = fin[1]
    y = o_ref[...] + ob_ref[...]
    yc = y - jnp.mean(y, axis=-1, keepdims=True)
    yn = yc * lax.rsqrt(jnp.mean(yc * yc, axis=-1, keepdims=True) + RET_GN_EPS)
    o_ref[...] = (yn * lnw_ref[...] + lnb_ref[...]) * _silu(g_ref[...])


def _retention(p, decay_logit, ln_w, ln_b, r0, latent, rope=None):
    if latent:
        nseq, t, rb0 = DEC_BATCH, DEC_SEQ, M_CTX // DEC_SEQ
    else:
        nseq, t, rb0 = BATCH, SEQ, 0
    w = HD_D
    dl = jnp.broadcast_to(decay_logit.T[:, :, None], (H_D, 2, w))
    in_specs = [pl.BlockSpec((t, w), lambda b, h: (rb0 + b, H_D + h)),
                pl.BlockSpec((t, w), lambda b, h: (rb0 + b, 2 * H_D + h)),
                pl.BlockSpec((t, w), lambda b, h: (rb0 + b, 3 * H_D + h)),
                pl.BlockSpec((t, w), lambda b, h: (rb0 + b, 4 * H_D + h)),
                pl.BlockSpec((1, 2, w), lambda b, h: (h, 0, 0)),
                pl.BlockSpec((1, w), lambda b, h: (0, h)),
                pl.BlockSpec((1, w), lambda b, h: (0, h)),
                pl.BlockSpec((1, 2, 1, w, w), lambda b, h: (b, 0, h, 0, 0))]
    args = [p, p, p, p, dl, ln_w.reshape(1, MIX), ln_b.reshape(1, MIX), r0]
    if latent:
        in_specs += [pl.BlockSpec((t, w), lambda b, h: (0, 0))] * 2
        args += list(rope)
    return pl.pallas_call(
        functools.partial(_ret_kernel, t=t, latent=latent),
        grid=(nseq, H_D),
        in_specs=in_specs,
        out_specs=[pl.BlockSpec((t, w), lambda b, h: (b, h)),
                   pl.BlockSpec((1, 2, 1, w, w), lambda b, h: (b, 0, h, 0, 0))],
        out_shape=[jax.ShapeDtypeStruct((nseq * t, MIX), F32),
                   jax.ShapeDtypeStruct((nseq, 2, H_D, w, w), F32)],
        scratch_shapes=[pltpu.VMEM((t, w), F32)],
        compiler_params=_params("parallel", "parallel"),
        name="retention_lat" if latent else "retention_ctx",
    )(*args)


def _rope_tables(n_tok, dim, reps):
    rows = n_tok // GRID_W
    n_freq = dim // 4
    inv = 1.0 / (ROPE_THETA ** (jnp.arange(n_freq, dtype=F32) / n_freq))
    row = jnp.repeat(jnp.arange(rows, dtype=F32), GRID_W)
    col = jnp.tile(jnp.arange(GRID_W, dtype=F32), rows)
    ang = jnp.concatenate([row[:, None] * inv, col[:, None] * inv], axis=-1)
    cos, sin = jnp.cos(ang), jnp.sin(ang)
    return jnp.tile(jnp.concatenate([cos, cos], axis=1), (1, reps)), \
        jnp.tile(jnp.concatenate([-sin, sin], axis=1), (1, reps))


def _time_major(x, nseq, group):
    t = x.shape[0] // nseq
    x = x.reshape(nseq // group, group, t, x.shape[1])
    return jnp.swapaxes(x, 1, 2).reshape(nseq // group, t * group, x.shape[-1])


def _seq_major(x, nseq, group):
    t = x.shape[1] // group
    x = x.reshape(nseq // group, t, group, x.shape[-1])
    return jnp.swapaxes(x, 1, 2).reshape(nseq * t, x.shape[-1])


def kernel(x_prompt, x_sample, cache_k_ab, cache_v_ab, state_rwkv, state_s5_re, state_s5_im, state_ret, c, c_ctx, norm1_g, norm2_g, w_mod, b_mod, w_ff_gate, w_ff_up, w_ff_down, w_in_ab, w_out_ab, qk_gain_a, lambda_qk, subln_g, rwkv_mu, rwkv_k_k, rwkv_k_a, rwkv_r_k, rwkv_w0, rwkv_w_up, rwkv_a0, rwkv_a_up, rwkv_g_up, rwkv_ln_w, rwkv_ln_b, w_in_cd, w_out_cd, s5_lam_re, s5_lam_im, s5_log_dt, s5_b_re, s5_b_im, s5_c_re, s5_c_im, s5_d, s5_w_glu, ret_decay_logit, ret_ln_w, ret_ln_b):
    d = D_MODEL
    x = jnp.concatenate([x_prompt.reshape(M_CTX, d), x_sample.reshape(M_LAT, d)], axis=0)
    cvec = jnp.zeros((MOD_ROWS, d), F32).at[0].set(c_ctx).at[1:1 + DEC_BATCH].set(c)
    mods = _modulation(cvec, w_mod, b_mod)

    lam_init = 0.8 - 0.6 * math.exp(-0.3 * 0)
    p = _norm_linear(x, norm1_g[0], mods[0], w_in_ab[0].astype(BF16))
    gain2 = jnp.tile(qk_gain_a[0], (1, 2))
    rope_a = _rope_tables(DEC_SEQ, HD_A, 2)
    ck = cache_k_ab[:, 0].reshape(DEC_BATCH * PAST_LEN, MIX)
    cv = cache_v_ab[:, 0].reshape(DEC_BATCH * PAST_LEN, MIX)
    oa_ctx, k_ctx = _attention(p, gain2, lambda_qk[0], subln_g[0], lam_init, latent=False)
    oa_lat = _attention(p, gain2, lambda_qk[0], subln_g[0], lam_init, latent=True,
                        cache_k=ck, cache_v=cv, rope=rope_a)
    rw_prm = (rwkv_mu[0], rwkv_k_k[0], rwkv_k_a[0], rwkv_r_k[0], rwkv_w0[0], rwkv_w_up[0],
              rwkv_a0[0], rwkv_a_up[0], rwkv_g_up[0])
    s0_ctx = jnp.zeros((BATCH, 2, H_B, HD_B, HD_B), F32)
    s0_lat = jnp.swapaxes(state_rwkv[:, 0], -1, -2)
    ob_ctx, sfin_ctx = _rwkv_scan(_rwkv_prep(p, rw_prm, latent=False), rwkv_ln_w[0], rwkv_ln_b[0],
                                  s0_ctx, latent=False)
    ob_lat, _ = _rwkv_scan(_rwkv_prep(p, rw_prm, latent=True), rwkv_ln_w[0], rwkv_ln_b[0],
                           s0_lat, latent=True)
    x = _proj_residual(x, jnp.concatenate([oa_ctx, oa_lat], axis=0),
                       jnp.concatenate([ob_ctx, ob_lat], axis=0), w_out_ab[0].astype(BF16), mods[0])
    x = _ffn(x, norm2_g[0], mods[0], w_ff_gate[0].astype(BF16), w_ff_up[0].astype(BF16),
             w_ff_down[0].astype(BF16))

    p2 = _norm_linear(x, norm1_g[1], mods[1], w_in_cd[0].astype(BF16))
    mats = _s5_matrices(s5_lam_re[0], s5_lam_im[0], s5_log_dt[0], s5_b_re[0], s5_b_im[0],
                        s5_c_re[0], s5_c_im[0])
    wglu = s5_w_glu[0].astype(BF16)
    grp = 8
    u_ctx = _time_major(p2[:M_CTX, :MIX], BATCH, grp)
    u_lat = _time_major(p2[M_CTX:, :MIX], DEC_BATCH, DEC_BATCH)
    z_ctx = jnp.zeros((BATCH // grp, 2, grp, S5_STATE), F32)
    oc_ctx, hfr, hfi = _s5(u_ctx, mats, z_ctx, z_ctx, s5_d[0], wglu, grp)
    h0r = jnp.swapaxes(state_s5_re[:, 0].reshape(DEC_BATCH, 2, S5_STATE), 0, 1)[None]
    h0i = jnp.swapaxes(state_s5_im[:, 0].reshape(DEC_BATCH, 2, S5_STATE), 0, 1)[None]
    oc_lat, _, _ = _s5(u_lat, mats, h0r, h0i, s5_d[0], wglu, DEC_BATCH)
    oc = jnp.concatenate([_seq_major(oc_ctx, BATCH, grp), _seq_major(oc_lat, DEC_BATCH, DEC_BATCH)], axis=0)
    rope_d = _rope_tables(DEC_SEQ, HD_D, 1)
    r0_ctx = jnp.zeros((BATCH, 2, H_D, HD_D, HD_D), F32)
    od_ctx, rfin = _retention(p2, ret_decay_logit[0], ret_ln_w[0], ret_ln_b[0], r0_ctx, latent=False)
    od_lat, _ = _retention(p2, ret_decay_logit[0], ret_ln_w[0], ret_ln_b[0], state_ret[:, 0],
                           latent=True, rope=rope_d)
    x = _proj_residual(x, oc, jnp.concatenate([od_ctx, od_lat], axis=0), w_out_cd[0].astype(BF16), mods[1])
    x = _ffn(x, norm2_g[1], mods[1], w_ff_gate[1].astype(BF16), w_ff_up[1].astype(BF16),
             w_ff_down[1].astype(BF16))

    y_prompt = x[:M_CTX].reshape(BATCH, SEQ, d)
    y_sample = x[M_CTX:].reshape(DEC_BATCH, DEC_SEQ, d)
    new_k = k_ctx.reshape(BATCH, 1, SEQ, H_A, 2, HD_A)
    new_v = p[:M_CTX, 2 * MIX:3 * MIX].reshape(BATCH, 1, SEQ, H_A, VD_A)
    new_rwkv = jnp.swapaxes(sfin_ctx, -1, -2)[:, None]
    s5_shape = (BATCH, 2, G_C, P_C)
    new_s5_re = jnp.swapaxes(hfr, 1, 2).reshape(s5_shape)[:, None]
    new_s5_im = jnp.swapaxes(hfi, 1, 2).reshape(s5_shape)[:, None]
    new_ret = rfin[:, None]
    return (y_prompt, y_sample, new_k, new_v, new_rwkv, new_s5_re, new_s5_im, new_ret)
```

```python
import functools
import math

import numpy as np
import jax
import jax.numpy as jnp
from jax import lax
from jax.experimental import pallas as pl
from jax.experimental.pallas import tpu as pltpu

F32 = jnp.float32
BF16 = jnp.bfloat16
HIGHEST = lax.Precision.HIGHEST

D_MODEL = 1024
BATCH = 32
SEQ = 256
DEC_BATCH = 2
DEC_SEQ = 1024
PAST_LEN = 256
GRID_W = 64
H_A = 4
HD_A = 64
VD_A = 128
H_B = 8
HD_B = 64
MIX = 512
LORA_W = 64
LORA_A = 64
LORA_G = 128
S5_GROUP = 16
G_C = 32
P_C = 64
S5_STATE = G_C * P_C
H_D = 4
HD_D = 128
CHUNK = 128
D_FF = 2816
IN_AB = 3328
IN_CD = 2560
ROPE_THETA = 10000.0
NORM_EPS = 1e-6
RWKV_GN_EPS = 64e-5
RET_GN_EPS = 1e-5

M_CTX = BATCH * SEQ
M_LAT = DEC_BATCH * DEC_SEQ
M_ALL = M_CTX + M_LAT
MOD_ROWS = 8
RWKV_CHUNK = 64
VMEM_LIMIT = 56 * 1024 * 1024

NN = (((1,), (0,)), ((), ()))
NT = (((1,), (1,)), ((), ()))
TN = (((0,), (0,)), ((), ()))


def _params(*sem):
    return pltpu.CompilerParams(dimension_semantics=sem, vmem_limit_bytes=VMEM_LIMIT)


def _bdot(a, b, dims=NN):
    return lax.dot_general(a.astype(BF16), b.astype(BF16), dims, preferred_element_type=F32)


def _hdot(a, b, dims=NN):
    return lax.dot_general(a, b, dims, precision=HIGHEST, preferred_element_type=F32)


def _split_dot(x, m):
    hi = x.astype(BF16)
    lo = (x - hi.astype(F32)).astype(BF16)
    return (jnp.dot(hi, m, preferred_element_type=F32) + jnp.dot(lo, m, preferred_element_type=F32))


def _seg_matrix(n, shift, val):
    r = lax.broadcasted_iota(jnp.int32, (n, n), 0) >> shift
    c = lax.broadcasted_iota(jnp.int32, (n, n), 1) >> shift
    return jnp.where(r == c, val, 0.0).astype(BF16)


def _sigmoid(x):
    return jax.nn.sigmoid(x)


def _silu(x):
    return x * jax.nn.sigmoid(x)


def _softplus(x):
    return jnp.maximum(x, 0.0) + jnp.log(1.0 + jnp.exp(-jnp.abs(x)))


def _mod_row(tile, tm):
    r0 = tile * tm
    return jnp.where(r0 < M_CTX, 0, 1 + (r0 - M_CTX) // DEC_SEQ)


def _norm_mod(x, g, sc_ref, sh_ref, row):
    y = x * lax.rsqrt(jnp.mean(x * x, axis=-1, keepdims=True) + NORM_EPS) * g
    return y * (1.0 + sc_ref[pl.ds(row, 1), :]) + sh_ref[pl.ds(row, 1), :]


def _mod_kernel(c_ref, w_ref, b_ref, o_ref):
    o_ref[0] = _hdot(_silu(c_ref[...]), w_ref[0]) + b_ref[0]


def _modulation(cvec, w_mod, b_mod):
    depth, d, n6 = w_mod.shape
    tn = 1536
    return pl.pallas_call(
        _mod_kernel,
        grid=(depth, n6 // tn),
        in_specs=[pl.BlockSpec((MOD_ROWS, d), lambda l, j: (0, 0)),
                  pl.BlockSpec((1, d, tn), lambda l, j: (l, 0, j)),
                  pl.BlockSpec((1, 1, tn), lambda l, j: (l, 0, j))],
        out_specs=pl.BlockSpec((1, MOD_ROWS, tn), lambda l, j: (l, 0, j)),
        out_shape=jax.ShapeDtypeStruct((depth, MOD_ROWS, n6), F32),
        compiler_params=_params("parallel", "parallel"),
        name="modulation",
    )(cvec, w_mod, b_mod.reshape(depth, 1, n6))


def _norm_linear_kernel(x_ref, g_ref, sc_ref, sh_ref, w_ref, o_ref, *, tm):
    row = _mod_row(pl.program_id(0), tm)
    h = _norm_mod(x_ref[...], g_ref[...], sc_ref, sh_ref, row)
    o_ref[...] = jnp.dot(h.astype(BF16), w_ref[...], preferred_element_type=F32)


def _norm_linear(x, g, mods, w_bf16):
    tm = 256
    m, d = x.shape
    n = w_bf16.shape[1]
    return pl.pallas_call(
        functools.partial(_norm_linear_kernel, tm=tm),
        grid=(m // tm,),
        in_specs=[pl.BlockSpec((tm, d), lambda i: (i, 0)),
                  pl.BlockSpec((1, d), lambda i: (0, 0)),
                  pl.BlockSpec((MOD_ROWS, d), lambda i: (0, 1)),
                  pl.BlockSpec((MOD_ROWS, d), lambda i: (0, 0)),
                  pl.BlockSpec((d, n), lambda i: (0, 0))],
        out_specs=pl.BlockSpec((tm, n), lambda i: (i, 0)),
        out_shape=jax.ShapeDtypeStruct((m, n), F32),
        compiler_params=_params("parallel"),
        name="norm_linear",
    )(x, g.reshape(1, d), mods, mods, w_bf16)


def _proj_res_kernel(x_ref, a_ref, b_ref, wa_ref, wb_ref, gt_ref, o_ref, *, tm):
    row = _mod_row(pl.program_id(0), tm)
    o = (jnp.dot(a_ref[...].astype(BF16), wa_ref[...], preferred_element_type=F32)
         + jnp.dot(b_ref[...].astype(BF16), wb_ref[...], preferred_element_type=F32))
    o_ref[...] = x_ref[...] + gt_ref[pl.ds(row, 1), :] * o


def _proj_residual(x, oa, ob, w_bf16, mods):
    tm = 256
    m, d = x.shape
    return pl.pallas_call(
        functools.partial(_proj_res_kernel, tm=tm),
        grid=(m // tm,),
        in_specs=[pl.BlockSpec((tm, d), lambda i: (i, 0)),
                  pl.BlockSpec((tm, MIX), lambda i: (i, 0)),
                  pl.BlockSpec((tm, MIX), lambda i: (i, 0)),
                  pl.BlockSpec((MIX, d), lambda i: (0, 0)),
                  pl.BlockSpec((MIX, d), lambda i: (1, 0)),
                  pl.BlockSpec((MOD_ROWS, d), lambda i: (0, 2))],
        out_specs=pl.BlockSpec((tm, d), lambda i: (i, 0)),
        out_shape=jax.ShapeDtypeStruct((m, d), F32),
        compiler_params=_params("parallel"),
        name="proj_residual",
    )(x, oa, ob, w_bf16, w_bf16, mods)


def _ffn_kernel(x_ref, g_ref, sc_ref, sh_ref, gt_ref, wg_ref, wu_ref, wd_ref, o_ref, *, tm, ck):
    row = _mod_row(pl.program_id(0), tm)
    x = x_ref[...]
    h = _norm_mod(x, g_ref[...], sc_ref, sh_ref, row).astype(BF16)
    acc = jnp.zeros((tm, D_MODEL), F32)
    for c in range(D_FF // ck):
        gg = jnp.dot(h, wg_ref[:, c * ck:(c + 1) * ck], preferred_element_type=F32)
        uu = jnp.dot(h, wu_ref[:, c * ck:(c + 1) * ck], preferred_element_type=F32)
        act = (_silu(gg) * uu).astype(BF16)
        acc = acc + jnp.dot(act, wd_ref[c * ck:(c + 1) * ck, :], preferred_element_type=F32)
    o_ref[...] = x + gt_ref[pl.ds(row, 1), :] * acc


def _ffn(x, g, mods, wg, wu, wd):
    tm, ck = 512, 256
    m, d = x.shape
    return pl.pallas_call(
        functools.partial(_ffn_kernel, tm=tm, ck=ck),
        grid=(m // tm,),
        in_specs=[pl.BlockSpec((tm, d), lambda i: (i, 0)),
                  pl.BlockSpec((1, d), lambda i: (0, 0)),
                  pl.BlockSpec((MOD_ROWS, d), lambda i: (0, 4)),
                  pl.BlockSpec((MOD_ROWS, d), lambda i: (0, 3)),
                  pl.BlockSpec((MOD_ROWS, d), lambda i: (0, 5)),
                  pl.BlockSpec((d, D_FF), lambda i: (0, 0)),
                  pl.BlockSpec((d, D_FF), lambda i: (0, 0)),
                  pl.BlockSpec((D_FF, d), lambda i: (0, 0))],
        out_specs=pl.BlockSpec((tm, d), lambda i: (i, 0)),
        out_shape=jax.ShapeDtypeStruct((m, d), F32),
        compiler_params=_params("parallel"),
        name="ffn",
    )(x, g.reshape(1, d), mods, mods, mods, wg, wu, wd)


def _qk_norm(x, gain, segm):
    ms = _split_dot(x * x, segm)
    return x * lax.rsqrt(ms + NORM_EPS) * gain


def _rope_pairs(x, cosf, sinf, half):
    lane = lax.broadcasted_iota(jnp.int32, x.shape, 1)
    first = (lane & (2 * half - 1)) < half
    n = x.shape[1]
    partner = jnp.where(first, pltpu.roll(x, n - half, axis=1), pltpu.roll(x, half, axis=1))
    return x * cosf + partner * sinf


def _attn_kernel(*refs, latent, lam_init):
    if latent:
        (q_ref, k_ref, v_ref, ck_ref, cv_ref, cosq_ref, sinq_ref, cosk_ref, sink_ref,
         gain_ref, lam_ref, sub_ref, o_ref) = refs
    else:
        q_ref, k_ref, v_ref, gain_ref, lam_ref, sub_ref, o_ref, kn_ref = refs
    segm = _seg_matrix(2 * HD_A, 6, 1.0 / HD_A)
    gains = gain_ref[...]
    q = _qk_norm(q_ref[...], gains[0:1], segm)
    k = _qk_norm(k_ref[...], gains[1:2], segm)
    v = v_ref[...]
    if latent:
        q = _rope_pairs(q, cosq_ref[...], sinq_ref[...], HD_A // 2)
        k = _rope_pairs(k, cosk_ref[...], sink_ref[...], HD_A // 2)
        k = jnp.concatenate([ck_ref[...], k], axis=0)
        v = jnp.concatenate([cv_ref[...], v], axis=0)
    else:
        kn_ref[...] = k
    lv = lam_ref[...]
    lam = (jnp.exp(jnp.sum(lv[0:1] * lv[1:2], axis=1, keepdims=True))
           - jnp.exp(jnp.sum(lv[2:3] * lv[3:4], axis=1, keepdims=True)) + lam_init)
    scale = HD_A ** -0.5
    probs = []
    for c in range(2):
        s = _bdot(q[:, c * HD_A:(c + 1) * HD_A], k[:, c * HD_A:(c + 1) * HD_A], NT) * scale
        e = jnp.exp(s - jnp.max(s, axis=-1, keepdims=True))
        probs.append(e / jnp.sum(e, axis=-1, keepdims=True))
    att = probs[0] - lam * probs[1]
    o = _bdot(att, v)
    o = o * lax.rsqrt(jnp.mean(o * o, axis=-1, keepdims=True) + NORM_EPS) * sub_ref[...]
    o_ref[...] = o * (1.0 - lam_init)


def _attention(p, gain2, lambda_qk, subln_g, lam_init, latent, cache_k=None, cache_v=None, rope=None):
    w = 2 * HD_A
    if latent:
        nseq, t, tq, rb0 = DEC_BATCH, DEC_SEQ, 256, M_CTX // DEC_SEQ
    else:
        nseq, t, tq, rb0 = BATCH, SEQ, SEQ, 0
    nq = t // tq
    qoff = rb0 * nq
    in_specs = [pl.BlockSpec((tq, w), lambda b, h, i: (qoff + b * nq + i, h)),
                pl.BlockSpec((t, w), lambda b, h, i: (rb0 + b, H_A + h)),
                pl.BlockSpec((t, w), lambda b, h, i: (rb0 + b, 2 * H_A + h))]
    args = [p, p, p]
    if latent:
        cosf, sinf = rope
        in_specs += [pl.BlockSpec((PAST_LEN, w), lambda b, h, i: (b, h)),
                     pl.BlockSpec((PAST_LEN, w), lambda b, h, i: (b, h)),
                     pl.BlockSpec((tq, w), lambda b, h, i: (i, 0)),
                     pl.BlockSpec((tq, w), lambda b, h, i: (i, 0)),
                     pl.BlockSpec((t, w), lambda b, h, i: (0, 0)),
                     pl.BlockSpec((t, w), lambda b, h, i: (0, 0))]
        args += [cache_k, cache_v, cosf, sinf, cosf, sinf]
    in_specs += [pl.BlockSpec((2, w), lambda b, h, i: (0, 0)),
                 pl.BlockSpec((4, HD_A), lambda b, h, i: (0, 0)),
                 pl.BlockSpec((1, w), lambda b, h, i: (0, 0))]
    args += [gain2, lambda_qk, subln_g.reshape(1, w)]
    o_spec = pl.BlockSpec((tq, w), lambda b, h, i: (b * nq + i, h))
    o_shape = jax.ShapeDtypeStruct((nseq * t, MIX), F32)
    if latent:
        out_specs, out_shape = o_spec, o_shape
    else:
        out_specs = [o_spec, pl.BlockSpec((t, w), lambda b, h, i: (b, h))]
        out_shape = [o_shape, jax.ShapeDtypeStruct((nseq * t, MIX), F32)]
    return pl.pallas_call(
        functools.partial(_attn_kernel, latent=latent, lam_init=lam_init),
        grid=(nseq, H_A, nq),
        in_specs=in_specs, out_specs=out_specs, out_shape=out_shape,
        compiler_params=_params("parallel", "parallel", "arbitrary"),
        name="diff_attention_lat" if latent else "diff_attention_ctx",
    )(*args)


def _centred_shift(x, mu):
    t = x.shape[0]
    row = lax.broadcasted_iota(jnp.int32, x.shape, 0)
    prev = jnp.where(row == 0, 0.0, pltpu.roll(x, 1, axis=0))
    nxt = jnp.where(row == t - 1, 0.0, pltpu.roll(x, t - 1, axis=0))
    return x + (0.5 * (prev + nxt) - x) * mu


def _seg_sum(x, segm):
    return jnp.concatenate([_split_dot(x[:, j * 128:(j + 1) * 128], segm) for j in range(x.shape[1] // 128)],
                           axis=1)


def _rwkv_prep_kernel(r_ref, k_ref, v_ref, l_ref, mur_ref, muk_ref, muv_ref, mul_ref,
                      kk_ref, ka_ref, rk_ref, w0_ref, wup_ref, a0_ref, aup_ref, gup_ref,
                      ro_ref, ldf_ref, ldb_ref, kbo_ref, vbo_ref, kko_ref, ao_ref, gate_ref, bonus_ref):
    seg1 = _seg_matrix(128, 6, 1.0)
    r = _centred_shift(r_ref[...], mur_ref[...])
    kb = _centred_shift(k_ref[...], muk_ref[...])
    vb = _centred_shift(v_ref[...], muv_ref[...])
    lo = _centred_shift(l_ref[...], mul_ref[...])
    xw = lo[:, 0:LORA_W]
    xa = lo[:, LORA_W:LORA_W + LORA_A]
    xg = lo[:, LORA_W + LORA_A:]
    kk = kb * kk_ref[...]
    kk = kk * lax.rsqrt(_seg_sum(kk * kk, seg1) + 1e-12)
    a = _sigmoid(a0_ref[...] + _bdot(xa, aup_ref[...]))
    kb2 = kb * (1.0 + (a - 1.0) * ka_ref[...])
    lw = jnp.tanh(xw)
    for dr, ld_ref in enumerate((ldf_ref, ldb_ref)):
        z = w0_ref[dr:dr + 1, :] + _bdot(lw, wup_ref[dr])
        logw = -_softplus(-z) - 0.5
        ld_ref[...] = -jnp.exp(logw)
    gate_ref[...] = _bdot(_sigmoid(xg), gup_ref[...])
    bonus_ref[...] = _seg_sum(r * kb2 * rk_ref[...], seg1) * vb
    ro_ref[...] = r
    kbo_ref[...] = kb2
    vbo_ref[...] = vb
    kko_ref[...] = kk
    ao_ref[...] = a


def _rwkv_prep(p, prm, latent):
    mu, k_k, k_a, r_k, w0, w_up, a0, a_up, g_up = prm
    if latent:
        nseq, t, rb0 = DEC_BATCH, DEC_SEQ, M_CTX // DEC_SEQ
    else:
        nseq, t, rb0 = BATCH, SEQ, 0
    lw = LORA_W + LORA_A + LORA_G
    c0 = (IN_AB - 3 * MIX - lw) // MIX
    cl = (IN_AB - lw) // lw
    row = lambda n: pl.BlockSpec((1, n), lambda b: (0, 0))
    in_specs = [pl.BlockSpec((t, MIX), lambda b: (rb0 + b, c0)),
                pl.BlockSpec((t, MIX), lambda b: (rb0 + b, c0 + 1)),
                pl.BlockSpec((t, MIX), lambda b: (rb0 + b, c0 + 2)),
                pl.BlockSpec((t, lw), lambda b: (rb0 + b, cl)),
                row(MIX), row(MIX), row(MIX), row(lw),
                row(MIX), row(MIX), row(MIX),
                pl.BlockSpec((2, MIX), lambda b: (0, 0)),
                pl.BlockSpec((2, LORA_W, MIX), lambda b: (0, 0, 0)),
                row(MIX),
                pl.BlockSpec((LORA_A, MIX), lambda b: (0, 0)),
                pl.BlockSpec((LORA_G, MIX), lambda b: (0, 0))]
    o_spec = pl.BlockSpec((t, MIX), lambda b: (b, 0))
    o_shape = jax.ShapeDtypeStruct((nseq * t, MIX), F32)
    return pl.pallas_call(
        _rwkv_prep_kernel,
        grid=(nseq,),
        in_specs=in_specs, out_specs=[o_spec] * 9, out_shape=[o_shape] * 9,
        compiler_params=_params("parallel"),
        name="rwkv_prep_lat" if latent else "rwkv_prep_ctx",
    )(p, p, p, p,
      mu[None, 0:MIX], mu[None, MIX:2 * MIX], mu[None, 2 * MIX:3 * MIX], mu[None, 3 * MIX:],
      k_k.reshape(1, MIX), k_a.reshape(1, MIX), r_k.reshape(1, MIX), w0, w_up,
      a0.reshape(1, MIX), a_up, g_up)


def _split2(x):
    hi = x.astype(BF16)
    return hi, (x - hi.astype(F32)).astype(BF16)


RWKV_INV_BLOCK = 16


def _rwkv_masks(c, rev):
    ti = lax.broadcasted_iota(jnp.int32, (c, c), 0)
    si = lax.broadcasted_iota(jnp.int32, (c, c), 1)
    tri = jnp.where((si >= ti) if rev else (si <= ti), 1.0, 0.0).astype(BF16)
    t4 = lax.broadcasted_iota(jnp.int32, (4 * c, 4 * c), 0)
    s4 = lax.broadcasted_iota(jnp.int32, (4 * c, 4 * c), 1)
    tm, sm = t4 & (c - 1), s4 & (c - 1)
    strict = (sm > tm) if rev else (sm < tm)
    incl = (sm >= tm) if rev else (sm <= tm)
    same_head = ((t4 // c) & 1) == ((s4 // c) & 1)
    top = t4 < 2 * c
    gmask = same_head & ((top & strict) | (~top & incl))
    t2 = lax.broadcasted_iota(jnp.int32, (2 * c, 2 * c), 0)
    s2 = lax.broadcasted_iota(jnp.int32, (2 * c, 2 * c), 1)
    same = lambda n: (t2 // n) == (s2 // n)
    levels = []
    n = RWKV_INV_BLOCK
    while n < c:
        levels.append(same(2 * n) & ~same(n))
        n *= 2
    f = lambda m: jnp.where(m, 1.0, 0.0)
    return tri, f(gmask), f(same(RWKV_INV_BLOCK)), tuple(f(m) for m in levels), f(same(c)), f(t2 == s2)


def _keep(mask01, x):
    return jnp.where(mask01 > 0.5, x, 0.0)


def _tri_inverse(a, diag_blk, levels, eye):
    n = a.shape[0]
    d = _keep(diag_blk, a)
    t = eye + d
    p = _bdot(d, d)
    for _ in range(int(math.log2(RWKV_INV_BLOCK)) - 2):
        res = _bdot(jnp.concatenate([p, t], axis=0), p)
        p = res[:n]
        t = t + res[n:]
    t = t + _bdot(t, p)
    for off in levels:
        t = t + _bdot(_bdot(t, _keep(off, a)), t)
    return t


def _rwkv_pair_chunk(r, ld, kb, vb, kk, a, st, tri, gmask, diag_blk, levels, same_head, eye):
    c, w = r.shape
    hd = w // 2
    ld_hi, ld_lo = _split2(ld)
    lcum = (jnp.dot(tri, ld_hi, preferred_element_type=F32)
            + jnp.dot(tri, ld_lo, preferred_element_type=F32))
    ltot = jnp.sum(ld, axis=0, keepdims=True)
    ones = jnp.ones((c, w), BF16)
    pcol = jnp.exp(lax.dot_general(ld_hi, ones, TN, preferred_element_type=F32)
                   + lax.dot_general(ld_lo, ones, TN, preferred_element_type=F32))
    beta = kk * a
    eneg = jnp.exp(-lcum)
    erem = jnp.exp(ltot - lcum)
    abar = -kk * jnp.exp(lcum - ld)
    rbar = r * jnp.exp(lcum)
    bt = (beta * eneg).astype(BF16)
    kt = (kb * eneg).astype(BF16)
    vbb = vb.astype(BF16)
    head0 = lax.broadcasted_iota(jnp.int32, (c, w), 1) < hd
    pick = lambda res: jnp.where(head0, res[:c], res[c:])
    arst = _bdot(jnp.concatenate([abar, rbar], axis=0), st)
    lhs = jnp.concatenate([jnp.where(head0, abar, 0.0), jnp.where(head0, 0.0, abar),
                           jnp.where(head0, rbar, 0.0), jnp.where(head0, 0.0, rbar)], axis=0)
    g = _keep(gmask, _bdot(lhs, jnp.concatenate([bt, bt, kt, kt], axis=0), NT))
    x = arst[:c] + pick(_bdot(g[:2 * c, 2 * c:], jnp.concatenate([vbb, vbb], axis=0)))
    tinv = _tri_inverse(g[:2 * c, :2 * c], diag_blk, levels, eye)
    u = pick(_bdot(tinv, jnp.concatenate([x, x], axis=0)))
    ub = u.astype(BF16)
    y = arst[c:] + pick(_bdot(g[2 * c:], jnp.concatenate([ub, ub, vbb, vbb], axis=0)))
    bkh = jnp.concatenate([beta * erem, kb * erem], axis=0)
    st_new = pcol * st + _keep(same_head, _bdot(bkh, jnp.concatenate([ub, vbb], axis=0), TN))
    return y, st_new


def _rwkv_scan_kernel(r_ref, ldf_ref, ldb_ref, kb_ref, vb_ref, kk_ref, a_ref, bonus_ref, gate_ref,
                      lnw_ref, lnb_ref, s0_ref, o_ref, sfin_ref, yb_ref, *, t, c):
    n = t // c
    masks = [_rwkv_masks(c, rev) for rev in (False, True)]

    def body(ci, states):
        new_states = []
        for dr, (ld_ref, dst) in enumerate(((ldf_ref, o_ref), (ldb_ref, yb_ref))):
            cj = ci if dr == 0 else n - 1 - ci
            rows = pl.ds(pl.multiple_of(cj * c, c), c)
            r, ld, kb, vb, kk, a = (ref[rows, :] for ref in (r_ref, ld_ref, kb_ref, vb_ref, kk_ref, a_ref))
            y, st = _rwkv_pair_chunk(r, ld, kb, vb, kk, a, states[dr], *masks[dr])
            dst[rows, :] = y
            new_states.append(st)
        return tuple(new_states)

    zero = jnp.zeros((HD_B, HD_B), F32)
    init = tuple(jnp.concatenate([jnp.concatenate([s0_ref[0, dr, 0], zero], axis=1),
                                  jnp.concatenate([zero, s0_ref[0, dr, 1]], axis=1)], axis=0)
                 for dr in range(2))
    fin = lax.fori_loop(0, n, body, init)
    for dr in range(2):
        sfin_ref[0, dr, 0] = fin[dr][:HD_B, :HD_B]
        sfin_ref[0, dr, 1] = fin[dr][HD_B:, HD_B:]
    segm = _seg_matrix(2 * HD_B, 6, 1.0 / HD_B)
    y = o_ref[...] + yb_ref[...]
    yc = y - _split_dot(y, segm)
    yn = yc * lax.rsqrt(_split_dot(yc * yc, segm) + RWKV_GN_EPS)
    o_ref[...] = (yn * lnw_ref[...] + lnb_ref[...] + bonus_ref[...]) * gate_ref[...]


def _rwkv_scan(prep, ln_w, ln_b, s0t, latent):
    r, ldf, ldb, kb, vb, kk, a, gate, bonus = prep
    nseq, t = (DEC_BATCH, DEC_SEQ) if latent else (BATCH, SEQ)
    w = 2 * HD_B
    blk = pl.BlockSpec((t, w), lambda b, h: (b, h))
    vec = pl.BlockSpec((1, w), lambda b, h: (0, h))
    st_spec = pl.BlockSpec((1, 2, 2, HD_B, HD_B), lambda b, h: (b, 0, h, 0, 0))
    return pl.pallas_call(
        functools.partial(_rwkv_scan_kernel, t=t, c=RWKV_CHUNK),
        grid=(nseq, H_B // 2),
        in_specs=[blk] * 9 + [vec, vec, st_spec],
        out_specs=[blk, st_spec],
        out_shape=[jax.ShapeDtypeStruct((nseq * t, MIX), F32),
                   jax.ShapeDtypeStruct((nseq, 2, H_B, HD_B, HD_B), F32)],
        scratch_shapes=[pltpu.VMEM((t, w), F32)],
        compiler_params=_params("parallel", "parallel"),
        name="rwkv_scan_lat" if latent else "rwkv_scan_ctx",
    )(r, ldf, ldb, kb, vb, kk, a, bonus, gate, ln_w.reshape(1, MIX), ln_b.reshape(1, MIX), s0t)


S5_KB = 2
S5_KW = MIX // S5_KB
S5_NW = S5_STATE // S5_KB


def _gelu_tanh(x):
    return 0.5 * x * (1.0 + jnp.tanh(math.sqrt(2.0 / math.pi) * (x + 0.044715 * (x * x * x))))


def _s5_kernel(u_ref, bbr_ref, bbi_ref, ccr_ref, cci_ref, lam_ref, h0r_ref, h0i_ref, d_ref, wglu_ref,
               o_ref, hfr_ref, hfi_ref, bur, bui, yacc, *, nseq, rows, cr):
    nch = rows // cr
    ngrp = cr // 8
    spg = 8 // nseq

    for dr in range(2):
        ar = lam_ref[dr, 0]
        ai = lam_ref[dr, 1]

        def chunk_body(ci, carry, dr=dr, ar=ar, ai=ai):
            cj = ci if dr == 0 else nch - 1 - ci
            r0 = pl.multiple_of(cj * cr, cr)
            ub = u_ref[0, pl.ds(r0, cr), :].astype(BF16)
            for kb in range(S5_KB):
                uk = ub[:, kb * S5_KW:(kb + 1) * S5_KW]
                bur[:, kb * S5_NW:(kb + 1) * S5_NW] = jnp.dot(uk, bbr_ref[dr, kb], preferred_element_type=F32)
                bui[:, kb * S5_NW:(kb + 1) * S5_NW] = jnp.dot(uk, bbi_ref[dr, kb], preferred_element_type=F32)

            def group_body(gi, hc):
                hr, hi = hc
                gj = gi if dr == 0 else ngrp - 1 - gi
                o8 = pl.multiple_of(gj * 8, 8)
                xr = bur[pl.ds(o8, 8), :]
                xi = bui[pl.ds(o8, 8), :]
                outr = [None] * spg
                outi = [None] * spg
                for s in (range(spg) if dr == 0 else range(spg - 1, -1, -1)):
                    nr = ar * hr - ai * hi + xr[s * nseq:(s + 1) * nseq]
                    ni = ar * hi + ai * hr + xi[s * nseq:(s + 1) * nseq]
                    hr, hi = nr, ni
                    outr[s] = hr
                    outi[s] = hi
                bur[pl.ds(o8, 8), :] = outr[0] if spg == 1 else jnp.concatenate(outr, axis=0)
                bui[pl.ds(o8, 8), :] = outi[0] if spg == 1 else jnp.concatenate(outi, axis=0)
                return hr, hi

            carry = lax.fori_loop(0, ngrp, group_body, carry)
            for kb in range(S5_KB):
                yk = (jnp.dot(bur[:, kb * S5_NW:(kb + 1) * S5_NW].astype(BF16), ccr_ref[dr, kb],
                              preferred_element_type=F32)
                      - jnp.dot(bui[:, kb * S5_NW:(kb + 1) * S5_NW].astype(BF16), cci_ref[dr, kb],
                                preferred_element_type=F32))
                cols = slice(kb * S5_KW, (kb + 1) * S5_KW)
                if dr == 0:
                    yacc[pl.ds(r0, cr), cols] = yk
                else:
                    yacc[pl.ds(r0, cr), cols] = yacc[pl.ds(r0, cr), cols] + yk
            return carry

        hr, hi = lax.fori_loop(0, nch, chunk_body, (h0r_ref[0, dr], h0i_ref[0, dr]))
        hfr_ref[0, dr] = hr
        hfi_ref[0, dr] = hi

    def out_body(ci, _):
        r0 = pl.multiple_of(ci * cr, cr)
        u = u_ref[0, pl.ds(r0, cr), :]
        z = _gelu_tanh(yacc[pl.ds(r0, cr), :] + d_ref[...] * u)
        o_ref[0, pl.ds(r0, cr), :] = z * _sigmoid(jnp.dot(z.astype(BF16), wglu_ref[...],
                                                          preferred_element_type=F32))
        return 0

    lax.fori_loop(0, nch, out_body, 0)


def _s5(u_tb, mats, h0r, h0i, d_skip, w_glu_bf16, nseq):
    bbr, bbi, ccr, cci, lam = mats
    groups, rows, _ = u_tb.shape
    cr = 256
    full = lambda shape: pl.BlockSpec(shape, lambda g: (0,) * len(shape))
    h_spec = pl.BlockSpec((1, 2, nseq, S5_STATE), lambda g: (g, 0, 0, 0))
    io_spec = pl.BlockSpec((1, rows, MIX), lambda g: (g, 0, 0))
    return pl.pallas_call(
        functools.partial(_s5_kernel, nseq=nseq, rows=rows, cr=cr),
        grid=(groups,),
        in_specs=[io_spec, full(bbr.shape), full(bbi.shape), full(ccr.shape), full(cci.shape),
                  full(lam.shape), h_spec, h_spec, full((1, MIX)), full((MIX, MIX))],
        out_specs=[io_spec, h_spec, h_spec],
        out_shape=[jax.ShapeDtypeStruct((groups, rows, MIX), F32),
                   jax.ShapeDtypeStruct((groups, 2, nseq, S5_STATE), F32),
                   jax.ShapeDtypeStruct((groups, 2, nseq, S5_STATE), F32)],
        scratch_shapes=[pltpu.VMEM((cr, S5_STATE), F32), pltpu.VMEM((cr, S5_STATE), F32),
                        pltpu.VMEM((rows, MIX), F32)],
        compiler_params=_params("parallel"),
        name="s5_lat" if nseq == DEC_BATCH else "s5_ctx",
    )(u_tb, bbr, bbi, ccr, cci, lam, h0r, h0i, d_skip.reshape(1, MIX), w_glu_bf16)


def _s5_matrices(lam_re, lam_im, log_dt, b_re, b_im, c_re, c_im):
    dt = jnp.exp(log_dt)[..., None]
    mag = jnp.exp(lam_re * dt)
    ab_re = mag * jnp.cos(lam_im * dt)
    ab_im = mag * jnp.sin(lam_im * dt)
    den = lam_re * lam_re + lam_im * lam_im
    nr = ab_re - 1.0
    f_re = (nr * lam_re + ab_im * lam_im) / den
    f_im = (ab_im * lam_re - nr * lam_im) / den
    bb_re = f_re[..., None] * b_re - f_im[..., None] * b_im
    bb_im = f_re[..., None] * b_im + f_im[..., None] * b_re
    gl = G_C // S5_KB
    eye = jnp.eye(gl, dtype=F32)

    def in_proj(bb):
        x = bb.reshape(2, S5_KB, gl, P_C, S5_GROUP)
        return jnp.einsum('dkgps,gh->dkgshp', x, eye).reshape(2, S5_KB, S5_KW, S5_NW).astype(BF16)

    def out_proj(cc):
        x = cc.reshape(2, S5_KB, gl, S5_GROUP, P_C)
        return jnp.einsum('dkgsp,gh->dkgphs', x, eye).reshape(2, S5_KB, S5_NW, S5_KW).astype(BF16)

    lam = jnp.stack([ab_re.reshape(2, 1, S5_STATE), ab_im.reshape(2, 1, S5_STATE)], axis=1)
    return in_proj(bb_re), in_proj(bb_im), out_proj(c_re), out_proj(c_im), lam


def _ret_kernel(*refs, t, latent):
    if latent:
        (q_ref, k_ref, v_ref, g_ref, dl_ref, lnw_ref, lnb_ref, r0_ref, cos_ref, sin_ref,
         o_ref, rfin_ref, ob_ref) = refs
    else:
        q_ref, k_ref, v_ref, g_ref, dl_ref, lnw_ref, lnb_ref, r0_ref, o_ref, rfin_ref, ob_ref = refs
    n = t // CHUNK
    lg = -_softplus(-dl_ref[0])
    jf = lax.broadcasted_iota(jnp.int32, (CHUNK, CHUNK), 0).astype(F32)
    kf = lax.broadcasted_iota(jnp.int32, (CHUNK, CHUNK), 1).astype(F32)
    diff = jf - kf
    tabs = []
    for dr in range(2):
        l = lg[dr:dr + 1]
        if dr == 0:
            dmat = jnp.where(diff >= 0, jnp.exp(l * jnp.maximum(diff, 0.0)), 0.0)
            xi = jnp.exp(l * (jf + 1.0))
            zeta = jnp.exp(l * (CHUNK - 1.0 - jf))
        else:
            dmat = jnp.where(diff < 0, jnp.exp(l * jnp.maximum(-diff, 0.0)), 0.0)
            xi = jnp.exp(l * (CHUNK - jf))
            zeta = jnp.exp(l * jf)
        tabs.append((dmat, xi, zeta, jnp.exp(l * CHUNK)))

    def body(ci, states):
        new_states = []
        for dr, dst in enumerate((o_ref, ob_ref)):
            dmat, xi, zeta, gch = tabs[dr]
            cj = ci if dr == 0 else n - 1 - ci
            rows = pl.ds(pl.multiple_of(cj * CHUNK, CHUNK), CHUNK)
            q = q_ref[rows, :]
            k = k_ref[rows, :] * (HD_D ** -0.5)
            v = v_ref[rows, :]
            if latent:
                q = _rope_pairs(q, cos_ref[rows, :], sin_ref[rows, :], HD_D // 2)
                k = _rope_pairs(k, cos_ref[rows, :], sin_ref[rows, :], HD_D // 2)
            st = states[dr]
            inner = _bdot(q, k, NT) * dmat
            dst[rows, :] = _bdot(inner, v) + _bdot(q, st) * xi
            new_states.append(st * gch + _bdot(k * zeta, v, TN))
        return tuple(new_states)

    fin = lax.fori_loop(0, n, body, (r0_ref[0, 0, 0], r0_ref[0, 1, 0]))
    rfin_ref[0, 0, 0] = fin[0]
    rfin_ref[0, 1, 0] = fin[1]
    y = o_ref[...] + ob_ref[...]
    yc = y - jnp.mean(y, axis=-1, keepdims=True)
    yn = yc * lax.rsqrt(jnp.mean(yc * yc, axis=-1, keepdims=True) + RET_GN_EPS)
    o_ref[...] = (yn * lnw_ref[...] + lnb_ref[...]) * _silu(g_ref[...])


def _retention(p, decay_logit, ln_w, ln_b, r0, latent, rope=None):
    if latent:
        nseq, t, rb0 = DEC_BATCH, DEC_SEQ, M_CTX // DEC_SEQ
    else:
        nseq, t, rb0 = BATCH, SEQ, 0
    w = HD_D
    dl = jnp.broadcast_to(decay_logit.T[:, :, None], (H_D, 2, w))
    in_specs = [pl.BlockSpec((t, w), lambda b, h: (rb0 + b, H_D + h)),
                pl.BlockSpec((t, w), lambda b, h: (rb0 + b, 2 * H_D + h)),
                pl.BlockSpec((t, w), lambda b, h: (rb0 + b, 3 * H_D + h)),
                pl.BlockSpec((t, w), lambda b, h: (rb0 + b, 4 * H_D + h)),
                pl.BlockSpec((1, 2, w), lambda b, h: (h, 0, 0)),
                pl.BlockSpec((1, w), lambda b, h: (0, h)),
                pl.BlockSpec((1, w), lambda b, h: (0, h)),
                pl.BlockSpec((1, 2, 1, w, w), lambda b, h: (b, 0, h, 0, 0))]
    args = [p, p, p, p, dl, ln_w.reshape(1, MIX), ln_b.reshape(1, MIX), r0]
    if latent:
        in_specs += [pl.BlockSpec((t, w), lambda b, h: (0, 0))] * 2
        args += list(rope)
    return pl.pallas_call(
        functools.partial(_ret_kernel, t=t, latent=latent),
        grid=(nseq, H_D),
        in_specs=in_specs,
        out_specs=[pl.BlockSpec((t, w), lambda b, h: (b, h)),
                   pl.BlockSpec((1, 2, 1, w, w), lambda b, h: (b, 0, h, 0, 0))],
        out_shape=[jax.ShapeDtypeStruct((nseq * t, MIX), F32),
                   jax.ShapeDtypeStruct((nseq, 2, H_D, w, w), F32)],
        scratch_shapes=[pltpu.VMEM((t, w), F32)],
        compiler_params=_params("parallel", "parallel"),
        name="retention_lat" if latent else "retention_ctx",
    )(*args)


def _rope_tables(n_tok, dim, reps):
    rows = n_tok // GRID_W
    n_freq = dim // 4
    inv = 1.0 / (ROPE_THETA ** (jnp.arange(n_freq, dtype=F32) / n_freq))
    row = jnp.repeat(jnp.arange(rows, dtype=F32), GRID_W)
    col = jnp.tile(jnp.arange(GRID_W, dtype=F32), rows)
    ang = jnp.concatenate([row[:, None] * inv, col[:, None] * inv], axis=-1)
    cos, sin = jnp.cos(ang), jnp.sin(ang)
    return jnp.tile(jnp.concatenate([cos, cos], axis=1), (1, reps)), \
        jnp.tile(jnp.concatenate([-sin, sin], axis=1), (1, reps))


def _time_major(x, nseq, group):
    t = x.shape[0] // nseq
    x = x.reshape(nseq // group, group, t, x.shape[1])
    return jnp.swapaxes(x, 1, 2).reshape(nseq // group, t * group, x.shape[-1])


def _seq_major(x, nseq, group):
    t = x.shape[1] // group
    x = x.reshape(nseq // group, t, group, x.shape[-1])
    return jnp.swapaxes(x, 1, 2).reshape(nseq * t, x.shape[-1])


def kernel(x_prompt, x_sample, cache_k_ab, cache_v_ab, state_rwkv, state_s5_re, state_s5_im, state_ret, c, c_ctx, norm1_g, norm2_g, w_mod, b_mod, w_ff_gate, w_ff_up, w_ff_down, w_in_ab, w_out_ab, qk_gain_a, lambda_qk, subln_g, rwkv_mu, rwkv_k_k, rwkv_k_a, rwkv_r_k, rwkv_w0, rwkv_w_up, rwkv_a0, rwkv_a_up, rwkv_g_up, rwkv_ln_w, rwkv_ln_b, w_in_cd, w_out_cd, s5_lam_re, s5_lam_im, s5_log_dt, s5_b_re, s5_b_im, s5_c_re, s5_c_im, s5_d, s5_w_glu, ret_decay_logit, ret_ln_w, ret_ln_b):
    d = D_MODEL
    x = jnp.concatenate([x_prompt.reshape(M_CTX, d), x_sample.reshape(M_LAT, d)], axis=0)
    cvec = jnp.zeros((MOD_ROWS, d), F32).at[0].set(c_ctx).at[1:1 + DEC_BATCH].set(c)
    mods = _modulation(cvec, w_mod, b_mod)

    lam_init = 0.8 - 0.6 * math.exp(-0.3 * 0)
    p = _norm_linear(x, norm1_g[0], mods[0], w_in_ab[0].astype(BF16))
    gain2 = jnp.tile(qk_gain_a[0], (1, 2))
    rope_a = _rope_tables(DEC_SEQ, HD_A, 2)
    ck = cache_k_ab[:, 0].reshape(DEC_BATCH * PAST_LEN, MIX)
    cv = cache_v_ab[:, 0].reshape(DEC_BATCH * PAST_LEN, MIX)
    oa_ctx, k_ctx = _attention(p, gain2, lambda_qk[0], subln_g[0], lam_init, latent=False)
    oa_lat = _attention(p, gain2, lambda_qk[0], subln_g[0], lam_init, latent=True,
                        cache_k=ck, cache_v=cv, rope=rope_a)
    rw_prm = (rwkv_mu[0], rwkv_k_k[0], rwkv_k_a[0], rwkv_r_k[0], rwkv_w0[0], rwkv_w_up[0],
              rwkv_a0[0], rwkv_a_up[0], rwkv_g_up[0])
    s0_ctx = jnp.zeros((BATCH, 2, H_B, HD_B, HD_B), F32)
    s0_lat = jnp.swapaxes(state_rwkv[:, 0], -1, -2)
    ob_ctx, sfin_ctx = _rwkv_scan(_rwkv_prep(p, rw_prm, latent=False), rwkv_ln_w[0], rwkv_ln_b[0],
                                  s0_ctx, latent=False)
    ob_lat, _ = _rwkv_scan(_rwkv_prep(p, rw_prm, latent=True), rwkv_ln_w[0], rwkv_ln_b[0],
                           s0_lat, latent=True)
    x = _proj_residual(x, jnp.concatenate([oa_ctx, oa_lat], axis=0),
                       jnp.concatenate([ob_ctx, ob_lat], axis=0), w_out_ab[0].astype(BF16), mods[0])
    x = _ffn(x, norm2_g[0], mods[0], w_ff_gate[0].astype(BF16), w_ff_up[0].astype(BF16),
             w_ff_down[0].astype(BF16))

    p2 = _norm_linear(x, norm1_g[1], mods[1], w_in_cd[0].astype(BF16))
    mats = _s5_matrices(s5_lam_re[0], s5_lam_im[0], s5_log_dt[0], s5_b_re[0], s5_b_im[0],
                        s5_c_re[0], s5_c_im[0])
    wglu = s5_w_glu[0].astype(BF16)
    grp = 8
    u_ctx = _time_major(p2[:M_CTX, :MIX], BATCH, grp)
    u_lat = _time_major(p2[M_CTX:, :MIX], DEC_BATCH, DEC_BATCH)
    z_ctx = jnp.zeros((BATCH // grp, 2, grp, S5_STATE), F32)
    oc_ctx, hfr, hfi = _s5(u_ctx, mats, z_ctx, z_ctx, s5_d[0], wglu, grp)
    h0r = jnp.swapaxes(state_s5_re[:, 0].reshape(DEC_BATCH, 2, S5_STATE), 0, 1)[None]
    h0i = jnp.swapaxes(state_s5_im[:, 0].reshape(DEC_BATCH, 2, S5_STATE), 0, 1)[None]
    oc_lat, _, _ = _s5(u_lat, mats, h0r, h0i, s5_d[0], wglu, DEC_BATCH)
    oc = jnp.concatenate([_seq_major(oc_ctx, BATCH, grp), _seq_major(oc_lat, DEC_BATCH, DEC_BATCH)], axis=0)
    rope_d = _rope_tables(DEC_SEQ, HD_D, 1)
    r0_ctx = jnp.zeros((BATCH, 2, H_D, HD_D, HD_D), F32)
    od_ctx, rfin = _retention(p2, ret_decay_logit[0], ret_ln_w[0], ret_ln_b[0], r0_ctx, latent=False)
    od_lat, _ = _retention(p2, ret_decay_logit[0], ret_ln_w[0], ret_ln_b[0], state_ret[:, 0],
                           latent=True, rope=rope_d)
    x = _proj_residual(x, oc, jnp.concatenate([od_ctx, od_lat], axis=0), w_out_cd[0].astype(BF16), mods[1])
    x = _ffn(x, norm2_g[1], mods[1], w_ff_gate[1].astype(BF16), w_ff_up[1].astype(BF16),
             w_ff_down[1].astype(BF16))

    y_prompt = x[:M_CTX].reshape(BATCH, SEQ, d)
    y_sample = x[M_CTX:].reshape(DEC_BATCH, DEC_SEQ, d)
    new_k = k_ctx.reshape(BATCH, 1, SEQ, H_A, 2, HD_A)
    new_v = p[:M_CTX, 2 * MIX:3 * MIX].reshape(BATCH, 1, SEQ, H_A, VD_A)
    new_rwkv = jnp.swapaxes(sfin_ctx, -1, -2)[:, None]
    s5_shape = (BATCH, 2, G_C, P_C)
    new_s5_re = jnp.swapaxes(hfr, 1, 2).reshape(s5_shape)[:, None]
    new_s5_im = jnp.swapaxes(hfi, 1, 2).reshape(s5_shape)[:, None]
    new_ret = rfin[:, None]
    return (y_prompt, y_sample, new_k, new_v, new_rwkv, new_s5_re, new_s5_im, new_ret)
```

```python
import functools
import math

import numpy as np
import jax
import jax.numpy as jnp
from jax import lax
from jax.experimental import pallas as pl
from jax.experimental.pallas import tpu as pltpu

F32 = jnp.float32
BF16 = jnp.bfloat16
HIGHEST = lax.Precision.HIGHEST

D_MODEL = 1024
BATCH = 32
SEQ = 256
DEC_BATCH = 2
DEC_SEQ = 1024
PAST_LEN = 256
GRID_W = 64
H_A = 4
HD_A = 64
VD_A = 128
H_B = 8
HD_B = 64
MIX = 512
LORA_W = 64
LORA_A = 64
LORA_G = 128
S5_GROUP = 16
G_C = 32
P_C = 64
S5_STATE = G_C * P_C
H_D = 4
HD_D = 128
CHUNK = 128
D_FF = 2816
IN_AB = 3328
IN_CD = 2560
ROPE_THETA = 10000.0
NORM_EPS = 1e-6
RWKV_GN_EPS = 64e-5
RET_GN_EPS = 1e-5

M_CTX = BATCH * SEQ
M_LAT = DEC_BATCH * DEC_SEQ
M_ALL = M_CTX + M_LAT
MOD_ROWS = 8
RWKV_CHUNK = 64
VMEM_LIMIT = 56 * 1024 * 1024

NN = (((1,), (0,)), ((), ()))
NT = (((1,), (1,)), ((), ()))
TN = (((0,), (0,)), ((), ()))


def _params(*sem):
    return pltpu.CompilerParams(dimension_semantics=sem, vmem_limit_bytes=VMEM_LIMIT)


def _bdot(a, b, dims=NN):
    return lax.dot_general(a.astype(BF16), b.astype(BF16), dims, preferred_element_type=F32)


def _hdot(a, b, dims=NN):
    return lax.dot_general(a, b, dims, precision=HIGHEST, preferred_element_type=F32)


def _split_dot(x, m):
    hi = x.astype(BF16)
    lo = (x - hi.astype(F32)).astype(BF16)
    return (jnp.dot(hi, m, preferred_element_type=F32) + jnp.dot(lo, m, preferred_element_type=F32))


def _seg_matrix(n, shift, val):
    r = lax.broadcasted_iota(jnp.int32, (n, n), 0) >> shift
    c = lax.broadcasted_iota(jnp.int32, (n, n), 1) >> shift
    return jnp.where(r == c, val, 0.0).astype(BF16)


def _sigmoid(x):
    return jax.nn.sigmoid(x)


def _silu(x):
    return x * jax.nn.sigmoid(x)


def _softplus(x):
    return jnp.maximum(x, 0.0) + jnp.log(1.0 + jnp.exp(-jnp.abs(x)))


def _mod_row(tile, tm):
    r0 = tile * tm
    return jnp.where(r0 < M_CTX, 0, 1 + (r0 - M_CTX) // DEC_SEQ)


def _norm_mod(x, g, sc_ref, sh_ref, row):
    y = x * lax.rsqrt(jnp.mean(x * x, axis=-1, keepdims=True) + NORM_EPS) * g
    return y * (1.0 + sc_ref[pl.ds(row, 1), :]) + sh_ref[pl.ds(row, 1), :]


def _mod_kernel(c_ref, w_ref, b_ref, o_ref):
    o_ref[0] = _hdot(_silu(c_ref[...]), w_ref[0]) + b_ref[0]


def _modulation(cvec, w_mod, b_mod):
    depth, d, n6 = w_mod.shape
    tn = 1536
    return pl.pallas_call(
        _mod_kernel,
        grid=(depth, n6 // tn),
        in_specs=[pl.BlockSpec((MOD_ROWS, d), lambda l, j: (0, 0)),
                  pl.BlockSpec((1, d, tn), lambda l, j: (l, 0, j)),
                  pl.BlockSpec((1, 1, tn), lambda l, j: (l, 0, j))],
        out_specs=pl.BlockSpec((1, MOD_ROWS, tn), lambda l, j: (l, 0, j)),
        out_shape=jax.ShapeDtypeStruct((depth, MOD_ROWS, n6), F32),
        compiler_params=_params("parallel", "parallel"),
        name="modulation",
    )(cvec, w_mod, b_mod.reshape(depth, 1, n6))


def _norm_linear_kernel(x_ref, g_ref, sc_ref, sh_ref, w_ref, o_ref, *, tm):
    row = _mod_row(pl.program_id(0), tm)
    h = _norm_mod(x_ref[...], g_ref[...], sc_ref, sh_ref, row)
    o_ref[...] = jnp.dot(h.astype(BF16), w_ref[...], preferred_element_type=F32)


def _norm_linear(x, g, mods, w_bf16):
    tm = 256
    m, d = x.shape
    n = w_bf16.shape[1]
    return pl.pallas_call(
        functools.partial(_norm_linear_kernel, tm=tm),
        grid=(m // tm,),
        in_specs=[pl.BlockSpec((tm, d), lambda i: (i, 0)),
                  pl.BlockSpec((1, d), lambda i: (0, 0)),
                  pl.BlockSpec((MOD_ROWS, d), lambda i: (0, 1)),
                  pl.BlockSpec((MOD_ROWS, d), lambda i: (0, 0)),
                  pl.BlockSpec((d, n), lambda i: (0, 0))],
        out_specs=pl.BlockSpec((tm, n), lambda i: (i, 0)),
        out_shape=jax.ShapeDtypeStruct((m, n), F32),
        compiler_params=_params("parallel"),
        name="norm_linear",
    )(x, g.reshape(1, d), mods, mods, w_bf16)


def _proj_res_kernel(x_ref, a_ref, b_ref, wa_ref, wb_ref, gt_ref, o_ref, *, tm):
    row = _mod_row(pl.program_id(0), tm)
    o = (jnp.dot(a_ref[...].astype(BF16), wa_ref[...], preferred_element_type=F32)
         + jnp.dot(b_ref[...].astype(BF16), wb_ref[...], preferred_element_type=F32))
    o_ref[...] = x_ref[...] + gt_ref[pl.ds(row, 1), :] * o


def _proj_residual(x, oa, ob, w_bf16, mods):
    tm = 256
    m, d = x.shape
    return pl.pallas_call(
        functools.partial(_proj_res_kernel, tm=tm),
        grid=(m // tm,),
        in_specs=[pl.BlockSpec((tm, d), lambda i: (i, 0)),
                  pl.BlockSpec((tm, MIX), lambda i: (i, 0)),
                  pl.BlockSpec((tm, MIX), lambda i: (i, 0)),
                  pl.BlockSpec((MIX, d), lambda i: (0, 0)),
                  pl.BlockSpec((MIX, d), lambda i: (1, 0)),
                  pl.BlockSpec((MOD_ROWS, d), lambda i: (0, 2))],
        out_specs=pl.BlockSpec((tm, d), lambda i: (i, 0)),
        out_shape=jax.ShapeDtypeStruct((m, d), F32),
        compiler_params=_params("parallel"),
        name="proj_residual",
    )(x, oa, ob, w_bf16, w_bf16, mods)


def _ffn_kernel(x_ref, g_ref, sc_ref, sh_ref, gt_ref, wg_ref, wu_ref, wd_ref, o_ref, *, tm, ck):
    row = _mod_row(pl.program_id(0), tm)
    x = x_ref[...]
    h = _norm_mod(x, g_ref[...], sc_ref, sh_ref, row).astype(BF16)
    acc = jnp.zeros((tm, D_MODEL), F32)
    for c in range(D_FF // ck):
        gg = jnp.dot(h, wg_ref[:, c * ck:(c + 1) * ck], preferred_element_type=F32)
        uu = jnp.dot(h, wu_ref[:, c * ck:(c + 1) * ck], preferred_element_type=F32)
        act = (_silu(gg) * uu).astype(BF16)
        acc = acc + jnp.dot(act, wd_ref[c * ck:(c + 1) * ck, :], preferred_element_type=F32)
    o_ref[...] = x + gt_ref[pl.ds(row, 1), :] * acc


def _ffn(x, g, mods, wg, wu, wd):
    tm, ck = 512, 256
    m, d = x.shape
    return pl.pallas_call(
        functools.partial(_ffn_kernel, tm=tm, ck=ck),
        grid=(m // tm,),
        in_specs=[pl.BlockSpec((tm, d), lambda i: (i, 0)),
                  pl.BlockSpec((1, d), lambda i: (0, 0)),
                  pl.BlockSpec((MOD_ROWS, d), lambda i: (0, 4)),
                  pl.BlockSpec((MOD_ROWS, d), lambda i: (0, 3)),
                  pl.BlockSpec((MOD_ROWS, d), lambda i: (0, 5)),
                  pl.BlockSpec((d, D_FF), lambda i: (0, 0)),
                  pl.BlockSpec((d, D_FF), lambda i: (0, 0)),
                  pl.BlockSpec((D_FF, d), lambda i: (0, 0))],
        out_specs=pl.BlockSpec((tm, d), lambda i: (i, 0)),
        out_shape=jax.ShapeDtypeStruct((m, d), F32),
        compiler_params=_params("parallel"),
        name="ffn",
    )(x, g.reshape(1, d), mods, mods, mods, wg, wu, wd)


def _qk_norm(x, gain, segm):
    ms = _split_dot(x * x, segm)
    return x * lax.rsqrt(ms + NORM_EPS) * gain


def _rope_pairs(x, cosf, sinf, half):
    lane = lax.broadcasted_iota(jnp.int32, x.shape, 1)
    first = (lane & (2 * half - 1)) < half
    n = x.shape[1]
    partner = jnp.where(first, pltpu.roll(x, n - half, axis=1), pltpu.roll(x, half, axis=1))
    return x * cosf + partner * sinf


def _attn_kernel(*refs, latent, lam_init):
    if latent:
        (q_ref, k_ref, v_ref, ck_ref, cv_ref, cosq_ref, sinq_ref, cosk_ref, sink_ref,
         gain_ref, lam_ref, sub_ref, o_ref) = refs
    else:
        q_ref, k_ref, v_ref, gain_ref, lam_ref, sub_ref, o_ref, kn_ref = refs
    segm = _seg_matrix(2 * HD_A, 6, 1.0 / HD_A)
    gains = gain_ref[...]
    q = _qk_norm(q_ref[...], gains[0:1], segm)
    k = _qk_norm(k_ref[...], gains[1:2], segm)
    v = v_ref[...]
    if latent:
        q = _rope_pairs(q, cosq_ref[...], sinq_ref[...], HD_A // 2)
        k = _rope_pairs(k, cosk_ref[...], sink_ref[...], HD_A // 2)
        k = jnp.concatenate([ck_ref[...], k], axis=0)
        v = jnp.concatenate([cv_ref[...], v], axis=0)
    else:
        kn_ref[...] = k
    lv = lam_ref[...]
    lam = (jnp.exp(jnp.sum(lv[0:1] * lv[1:2], axis=1, keepdims=True))
           - jnp.exp(jnp.sum(lv[2:3] * lv[3:4], axis=1, keepdims=True)) + lam_init)
    scale = HD_A ** -0.5
    probs = []
    for c in range(2):
        s = _bdot(q[:, c * HD_A:(c + 1) * HD_A], k[:, c * HD_A:(c + 1) * HD_A], NT) * scale
        e = jnp.exp(s - jnp.max(s, axis=-1, keepdims=True))
        probs.append(e / jnp.sum(e, axis=-1, keepdims=True))
    att = probs[0] - lam * probs[1]
    o = _bdot(att, v)
    o = o * lax.rsqrt(jnp.mean(o * o, axis=-1, keepdims=True) + NORM_EPS) * sub_ref[...]
    o_ref[...] = o * (1.0 - lam_init)


def _attention(p, gain2, lambda_qk, subln_g, lam_init, latent, cache_k=None, cache_v=None, rope=None):
    w = 2 * HD_A
    if latent:
        nseq, t, tq, rb0 = DEC_BATCH, DEC_SEQ, 256, M_CTX // DEC_SEQ
    else:
        nseq, t, tq, rb0 = BATCH, SEQ, SEQ, 0
    nq = t // tq
    qoff = rb0 * nq
    in_specs = [pl.BlockSpec((tq, w), lambda b, h, i: (qoff + b * nq + i, h)),
                pl.BlockSpec((t, w), lambda b, h, i: (rb0 + b, H_A + h)),
                pl.BlockSpec((t, w), lambda b, h, i: (rb0 + b, 2 * H_A + h))]
    args = [p, p, p]
    if latent:
        cosf, sinf = rope
        in_specs += [pl.BlockSpec((PAST_LEN, w), lambda b, h, i: (b, h)),
                     pl.BlockSpec((PAST_LEN, w), lambda b, h, i: (b, h)),
                     pl.BlockSpec((tq, w), lambda b, h, i: (i, 0)),
                     pl.BlockSpec((tq, w), lambda b, h, i: (i, 0)),
                     pl.BlockSpec((t, w), lambda b, h, i: (0, 0)),
                     pl.BlockSpec((t, w), lambda b, h, i: (0, 0))]
        args += [cache_k, cache_v, cosf, sinf, cosf, sinf]
    in_specs += [pl.BlockSpec((2, w), lambda b, h, i: (0, 0)),
                 pl.BlockSpec((4, HD_A), lambda b, h, i: (0, 0)),
                 pl.BlockSpec((1, w), lambda b, h, i: (0, 0))]
    args += [gain2, lambda_qk, subln_g.reshape(1, w)]
    o_spec = pl.BlockSpec((tq, w), lambda b, h, i: (b * nq + i, h))
    o_shape = jax.ShapeDtypeStruct((nseq * t, MIX), F32)
    if latent:
        out_specs, out_shape = o_spec, o_shape
    else:
        out_specs = [o_spec, pl.BlockSpec((t, w), lambda b, h, i: (b, h))]
        out_shape = [o_shape, jax.ShapeDtypeStruct((nseq * t, MIX), F32)]
    return pl.pallas_call(
        functools.partial(_attn_kernel, latent=latent, lam_init=lam_init),
        grid=(nseq, H_A, nq),
        in_specs=in_specs, out_specs=out_specs, out_shape=out_shape,
        compiler_params=_params("parallel", "parallel", "arbitrary"),
        name="diff_attention_lat" if latent else "diff_attention_ctx",
    )(*args)


def _centred_shift(x, mu):
    t = x.shape[0]
    row = lax.broadcasted_iota(jnp.int32, x.shape, 0)
    prev = jnp.where(row == 0, 0.0, pltpu.roll(x, 1, axis=0))
    nxt = jnp.where(row == t - 1, 0.0, pltpu.roll(x, t - 1, axis=0))
    return x + (0.5 * (prev + nxt) - x) * mu


def _seg_sum(x, segm):
    return jnp.concatenate([_split_dot(x[:, j * 128:(j + 1) * 128], segm) for j in range(x.shape[1] // 128)],
                           axis=1)


def _rwkv_prep_kernel(r_ref, k_ref, v_ref, l_ref, mur_ref, muk_ref, muv_ref, mul_ref,
                      kk_ref, ka_ref, rk_ref, w0_ref, wup_ref, a0_ref, aup_ref, gup_ref,
                      ro_ref, ldf_ref, ldb_ref, kbo_ref, vbo_ref, kko_ref, ao_ref, gate_ref, bonus_ref):
    seg1 = _seg_matrix(128, 6, 1.0)
    r = _centred_shift(r_ref[...], mur_ref[...])
    kb = _centred_shift(k_ref[...], muk_ref[...])
    vb = _centred_shift(v_ref[...], muv_ref[...])
    lo = _centred_shift(l_ref[...], mul_ref[...])
    xw = lo[:, 0:LORA_W]
    xa = lo[:, LORA_W:LORA_W + LORA_A]
    xg = lo[:, LORA_W + LORA_A:]
    kk = kb * kk_ref[...]
    kk = kk * lax.rsqrt(_seg_sum(kk * kk, seg1) + 1e-12)
    a = _sigmoid(a0_ref[...] + _bdot(xa, aup_ref[...]))
    kb2 = kb * (1.0 + (a - 1.0) * ka_ref[...])
    lw = jnp.tanh(xw)
    for dr, ld_ref in enumerate((ldf_ref, ldb_ref)):
        z = w0_ref[dr:dr + 1, :] + _bdot(lw, wup_ref[dr])
        logw = -_softplus(-z) - 0.5
        ld_ref[...] = -jnp.exp(logw)
    gate_ref[...] = _bdot(_sigmoid(xg), gup_ref[...])
    bonus_ref[...] = _seg_sum(r * kb2 * rk_ref[...], seg1) * vb
    ro_ref[...] = r
    kbo_ref[...] = kb2
    vbo_ref[...] = vb
    kko_ref[...] = kk
    ao_ref[...] = a


def _rwkv_prep(p, prm, latent):
    mu, k_k, k_a, r_k, w0, w_up, a0, a_up, g_up = prm
    if latent:
        nseq, t, rb0 = DEC_BATCH, DEC_SEQ, M_CTX // DEC_SEQ
    else:
        nseq, t, rb0 = BATCH, SEQ, 0
    lw = LORA_W + LORA_A + LORA_G
    c0 = (IN_AB - 3 * MIX - lw) // MIX
    cl = (IN_AB - lw) // lw
    row = lambda n: pl.BlockSpec((1, n), lambda b: (0, 0))
    in_specs = [pl.BlockSpec((t, MIX), lambda b: (rb0 + b, c0)),
                pl.BlockSpec((t, MIX), lambda b: (rb0 + b, c0 + 1)),
                pl.BlockSpec((t, MIX), lambda b: (rb0 + b, c0 + 2)),
                pl.BlockSpec((t, lw), lambda b: (rb0 + b, cl)),
                row(MIX), row(MIX), row(MIX), row(lw),
                row(MIX), row(MIX), row(MIX),
                pl.BlockSpec((2, MIX), lambda b: (0, 0)),
                pl.BlockSpec((2, LORA_W, MIX), lambda b: (0, 0, 0)),
                row(MIX),
                pl.BlockSpec((LORA_A, MIX), lambda b: (0, 0)),
                pl.BlockSpec((LORA_G, MIX), lambda b: (0, 0))]
    o_spec = pl.BlockSpec((t, MIX), lambda b: (b, 0))
    o_shape = jax.ShapeDtypeStruct((nseq * t, MIX), F32)
    return pl.pallas_call(
        _rwkv_prep_kernel,
        grid=(nseq,),
        in_specs=in_specs, out_specs=[o_spec] * 9, out_shape=[o_shape] * 9,
        compiler_params=_params("parallel"),
        name="rwkv_prep_lat" if latent else "rwkv_prep_ctx",
    )(p, p, p, p,
      mu[None, 0:MIX], mu[None, MIX:2 * MIX], mu[None, 2 * MIX:3 * MIX], mu[None, 3 * MIX:],
      k_k.reshape(1, MIX), k_a.reshape(1, MIX), r_k.reshape(1, MIX), w0, w_up,
      a0.reshape(1, MIX), a_up, g_up)


def _split2(x):
    hi = x.astype(BF16)
    return hi, (x - hi.astype(F32)).astype(BF16)


RWKV_INV_BLOCK = 16


def _rwkv_masks(c, rev):
    ti = lax.broadcasted_iota(jnp.int32, (c, c), 0)
    si = lax.broadcasted_iota(jnp.int32, (c, c), 1)
    tri = jnp.where((si >= ti) if rev else (si <= ti), 1.0, 0.0).astype(BF16)
    t4 = lax.broadcasted_iota(jnp.int32, (4 * c, 4 * c), 0)
    s4 = lax.broadcasted_iota(jnp.int32, (4 * c, 4 * c), 1)
    tm, sm = t4 & (c - 1), s4 & (c - 1)
    strict = (sm > tm) if rev else (sm < tm)
    incl = (sm >= tm) if rev else (sm <= tm)
    same_head = ((t4 // c) & 1) == ((s4 // c) & 1)
    top = t4 < 2 * c
    gmask = same_head & ((top & strict) | (~top & incl))
    t2 = lax.broadcasted_iota(jnp.int32, (2 * c, 2 * c), 0)
    s2 = lax.broadcasted_iota(jnp.int32, (2 * c, 2 * c), 1)
    same = lambda n: (t2 // n) == (s2 // n)
    levels = []
    n = RWKV_INV_BLOCK
    while n < c:
        levels.append(same(2 * n) & ~same(n))
        n *= 2
    f = lambda m: jnp.where(m, 1.0, 0.0)
    return tri, f(gmask), f(same(RWKV_INV_BLOCK)), tuple(f(m) for m in levels), f(same(c)), f(t2 == s2)


def _keep(mask01, x):
    return jnp.where(mask01 > 0.5, x, 0.0)


def _tri_inverse(a, diag_blk, levels, eye):
    n = a[0].shape[0]
    d = [_keep(diag_blk, x) for x in a]
    t = [eye + x for x in d]
    p = [_bdot(x, x) for x in d]
    for _ in range(int(math.log2(RWKV_INV_BLOCK)) - 2):
        res = [_bdot(jnp.concatenate([pi, ti], axis=0), pi) for pi, ti in zip(p, t)]
        p = [x[:n] for x in res]
        t = [ti + x[n:] for ti, x in zip(t, res)]
    t = [ti + _bdot(ti, pi) for ti, pi in zip(t, p)]
    for off in levels:
        half = [_bdot(ti, _keep(off, x)) for ti, x in zip(t, a)]
        t = [ti + _bdot(x, ti) for ti, x in zip(t, half)]
    return t


def _rwkv_pair_chunks(ins, sts, tris, gmasks, diag_blk, levels, same_head, eye):
    c, w = ins[0][0].shape
    hd = w // 2
    nch = range(len(ins))
    r, ld, kb, vb, kk, a = (list(z) for z in zip(*ins))
    split = [_split2(x) for x in ld]
    lcum = [jnp.dot(tris[i], split[i][0], preferred_element_type=F32)
            + jnp.dot(tris[i], split[i][1], preferred_element_type=F32) for i in nch]
    ones = jnp.ones((c, w), BF16)
    pcol = [jnp.exp(lax.dot_general(hi, ones, TN, preferred_element_type=F32)
                    + lax.dot_general(lo, ones, TN, preferred_element_type=F32)) for hi, lo in split]
    ltot = [jnp.sum(x, axis=0, keepdims=True) for x in ld]
    beta = [kk[i] * a[i] for i in nch]
    eneg = [jnp.exp(-x) for x in lcum]
    abar = [-kk[i] * jnp.exp(lcum[i] - ld[i]) for i in nch]
    rbar = [r[i] * jnp.exp(lcum[i]) for i in nch]
    bt = [(beta[i] * eneg[i]).astype(BF16) for i in nch]
    kt = [(kb[i] * eneg[i]).astype(BF16) for i in nch]
    vbb = [x.astype(BF16) for x in vb]
    head0 = lax.broadcasted_iota(jnp.int32, (c, w), 1) < hd
    pick = lambda res: jnp.where(head0, res[:c], res[c:])
    arst = [_bdot(jnp.concatenate([abar[i], rbar[i]], axis=0), sts[i]) for i in nch]
    lhs = [jnp.concatenate([jnp.where(head0, abar[i], 0.0), jnp.where(head0, 0.0, abar[i]),
                            jnp.where(head0, rbar[i], 0.0), jnp.where(head0, 0.0, rbar[i])], axis=0) for i in nch]
    g = [_keep(gmasks[i], _bdot(lhs[i], jnp.concatenate([bt[i], bt[i], kt[i], kt[i]], axis=0), NT))
         for i in nch]
    x = [arst[i][:c] + pick(_bdot(g[i][:2 * c, 2 * c:], jnp.concatenate([vbb[i], vbb[i]], axis=0))) for i in nch]
    tinv = _tri_inverse([gi[:2 * c, :2 * c] for gi in g], diag_blk, levels, eye)
    u = [pick(_bdot(tinv[i], jnp.concatenate([x[i], x[i]], axis=0))) for i in nch]
    ub = [z.astype(BF16) for z in u]
    y = [arst[i][c:] + pick(_bdot(g[i][2 * c:], jnp.concatenate([ub[i], ub[i], vbb[i], vbb[i]], axis=0)))
         for i in nch]
    erem = [jnp.exp(ltot[i] - lcum[i]) for i in nch]
    bkh = [jnp.concatenate([beta[i] * erem[i], kb[i] * erem[i]], axis=0) for i in nch]
    st_new = [pcol[i] * sts[i] + _keep(same_head, _bdot(bkh[i], jnp.concatenate([ub[i], vbb[i]], axis=0), TN))
              for i in nch]
    return y, st_new


def _rwkv_scan_kernel(r_ref, ldf_ref, ldb_ref, kb_ref, vb_ref, kk_ref, a_ref, bonus_ref, gate_ref,
                      lnw_ref, lnb_ref, s0_ref, o_ref, sfin_ref, yb_ref, *, t, c, npair):
    n = t // c
    w = 2 * HD_B
    masks = [_rwkv_masks(c, rev) for rev in (False, True)]

    def body(ci, states):
        ins, tris, gmasks, dsts = [], [], [], []
        for dr, (ld_ref, dst) in enumerate(((ldf_ref, o_ref), (ldb_ref, yb_ref))):
            cj = ci if dr == 0 else n - 1 - ci
            rows = pl.ds(pl.multiple_of(cj * c, c), c)
            for p in range(npair):
                cols = slice(p * w, (p + 1) * w)
                ins.append(tuple(ref[rows, cols] for ref in (r_ref, ld_ref, kb_ref, vb_ref, kk_ref, a_ref)))
                tris.append(masks[dr][0])
                gmasks.append(masks[dr][1])
                dsts.append((dst, rows, cols))
        ys, new_states = _rwkv_pair_chunks(ins, list(states), tris, gmasks, *masks[0][2:])
        for (dst, rows, cols), y in zip(dsts, ys):
            dst[rows, cols] = y
        return tuple(new_states)

    zero = jnp.zeros((HD_B, HD_B), F32)
    init = tuple(jnp.concatenate([jnp.concatenate([s0_ref[0, dr, 2 * p], zero], axis=1),
                                  jnp.concatenate([zero, s0_ref[0, dr, 2 * p + 1]], axis=1)], axis=0)
                 for dr in range(2) for p in range(npair))
    fin = lax.fori_loop(0, n, body, init)
    segm = _seg_matrix(w, 6, 1.0 / HD_B)
    for p in range(npair):
        for dr in range(2):
            st = fin[dr * npair + p]
            sfin_ref[0, dr, 2 * p] = st[:HD_B, :HD_B]
            sfin_ref[0, dr, 2 * p + 1] = st[HD_B:, HD_B:]
        cols = slice(p * w, (p + 1) * w)
        y = o_ref[:, cols] + yb_ref[:, cols]
        yc = y - _split_dot(y, segm)
        yn = yc * lax.rsqrt(_split_dot(yc * yc, segm) + RWKV_GN_EPS)
        o_ref[:, cols] = (yn * lnw_ref[:, cols] + lnb_ref[:, cols] + bonus_ref[:, cols]) * gate_ref[:, cols]


def _rwkv_scan(prep, ln_w, ln_b, s0t, latent):
    r, ldf, ldb, kb, vb, kk, a, gate, bonus = prep
    nseq, t = (DEC_BATCH, DEC_SEQ) if latent else (BATCH, SEQ)
    npair = H_B // 2
    w = npair * 2 * HD_B
    blk = pl.BlockSpec((t, w), lambda b, h: (b, h))
    vec = pl.BlockSpec((1, w), lambda b, h: (0, h))
    st_spec = pl.BlockSpec((1, 2, 2 * npair, HD_B, HD_B), lambda b, h: (b, 0, h, 0, 0))
    return pl.pallas_call(
        functools.partial(_rwkv_scan_kernel, t=t, c=RWKV_CHUNK, npair=npair),
        grid=(nseq, MIX // w),
        in_specs=[blk] * 9 + [vec, vec, st_spec],
        out_specs=[blk, st_spec],
        out_shape=[jax.ShapeDtypeStruct((nseq * t, MIX), F32),
                   jax.ShapeDtypeStruct((nseq, 2, H_B, HD_B, HD_B), F32)],
        scratch_shapes=[pltpu.VMEM((t, w), F32)],
        compiler_params=_params("parallel", "parallel"),
        name="rwkv_scan_lat" if latent else "rwkv_scan_ctx",
    )(r, ldf, ldb, kb, vb, kk, a, bonus, gate, ln_w.reshape(1, MIX), ln_b.reshape(1, MIX), s0t)


S5_KB = 2
S5_KW = MIX // S5_KB
S5_NW = S5_STATE // S5_KB


def _gelu_tanh(x):
    return 0.5 * x * (1.0 + jnp.tanh(math.sqrt(2.0 / math.pi) * (x + 0.044715 * (x * x * x))))


def _s5_kernel(u_ref, bbr_ref, bbi_ref, ccr_ref, cci_ref, lam_ref, h0r_ref, h0i_ref, d_ref, wglu_ref,
               o_ref, hfr_ref, hfi_ref, bur, bui, yacc, *, nseq, rows, cr):
    nch = rows // cr
    ngrp = cr // 8
    spg = 8 // nseq

    for dr in range(2):
        ar = lam_ref[dr, 0]
        ai = lam_ref[dr, 1]

        def chunk_body(ci, carry, dr=dr, ar=ar, ai=ai):
            cj = ci if dr == 0 else nch - 1 - ci
            r0 = pl.multiple_of(cj * cr, cr)
            ub = u_ref[0, pl.ds(r0, cr), :].astype(BF16)
            for kb in range(S5_KB):
                uk = ub[:, kb * S5_KW:(kb + 1) * S5_KW]
                bur[:, kb * S5_NW:(kb + 1) * S5_NW] = jnp.dot(uk, bbr_ref[dr, kb], preferred_element_type=F32)
                bui[:, kb * S5_NW:(kb + 1) * S5_NW] = jnp.dot(uk, bbi_ref[dr, kb], preferred_element_type=F32)

            def group_body(gi, hc):
                hr, hi = hc
                gj = gi if dr == 0 else ngrp - 1 - gi
                o8 = pl.multiple_of(gj * 8, 8)
                xr = bur[pl.ds(o8, 8), :]
                xi = bui[pl.ds(o8, 8), :]
                outr = [None] * spg
                outi = [None] * spg
                for s in (range(spg) if dr == 0 else range(spg - 1, -1, -1)):
                    nr = ar * hr - ai * hi + xr[s * nseq:(s + 1) * nseq]
                    ni = ar * hi + ai * hr + xi[s * nseq:(s + 1) * nseq]
                    hr, hi = nr, ni
                    outr[s] = hr
                    outi[s] = hi
                bur[pl.ds(o8, 8), :] = outr[0] if spg == 1 else jnp.concatenate(outr, axis=0)
                bui[pl.ds(o8, 8), :] = outi[0] if spg == 1 else jnp.concatenate(outi, axis=0)
                return hr, hi

            carry = lax.fori_loop(0, ngrp, group_body, carry)
            for kb in range(S5_KB):
                yk = (jnp.dot(bur[:, kb * S5_NW:(kb + 1) * S5_NW].astype(BF16), ccr_ref[dr, kb],
                              preferred_element_type=F32)
                      - jnp.dot(bui[:, kb * S5_NW:(kb + 1) * S5_NW].astype(BF16), cci_ref[dr, kb],
                                preferred_element_type=F32))
                cols = slice(kb * S5_KW, (kb + 1) * S5_KW)
                if dr == 0:
                    yacc[pl.ds(r0, cr), cols] = yk
                else:
                    yacc[pl.ds(r0, cr), cols] = yacc[pl.ds(r0, cr), cols] + yk
            return carry

        hr, hi = lax.fori_loop(0, nch, chunk_body, (h0r_ref[0, dr], h0i_ref[0, dr]))
        hfr_ref[0, dr] = hr
        hfi_ref[0, dr] = hi

    def out_body(ci, _):
        r0 = pl.multiple_of(ci * cr, cr)
        u = u_ref[0, pl.ds(r0, cr), :]
        z = _gelu_tanh(yacc[pl.ds(r0, cr), :] + d_ref[...] * u)
        o_ref[0, pl.ds(r0, cr), :] = z * _sigmoid(jnp.dot(z.astype(BF16), wglu_ref[...],
                                                          preferred_element_type=F32))
        return 0

    lax.fori_loop(0, nch, out_body, 0)


def _s5(u_tb, mats, h0r, h0i, d_skip, w_glu_bf16, nseq):
    bbr, bbi, ccr, cci, lam = mats
    groups, rows, _ = u_tb.shape
    cr = 256
    full = lambda shape: pl.BlockSpec(shape, lambda g: (0,) * len(shape))
    h_spec = pl.BlockSpec((1, 2, nseq, S5_STATE), lambda g: (g, 0, 0, 0))
    io_spec = pl.BlockSpec((1, rows, MIX), lambda g: (g, 0, 0))
    return pl.pallas_call(
        functools.partial(_s5_kernel, nseq=nseq, rows=rows, cr=cr),
        grid=(groups,),
        in_specs=[io_spec, full(bbr.shape), full(bbi.shape), full(ccr.shape), full(cci.shape),
                  full(lam.shape), h_spec, h_spec, full((1, MIX)), full((MIX, MIX))],
        out_specs=[io_spec, h_spec, h_spec],
        out_shape=[jax.ShapeDtypeStruct((groups, rows, MIX), F32),
                   jax.ShapeDtypeStruct((groups, 2, nseq, S5_STATE), F32),
                   jax.ShapeDtypeStruct((groups, 2, nseq, S5_STATE), F32)],
        scratch_shapes=[pltpu.VMEM((cr, S5_STATE), F32), pltpu.VMEM((cr, S5_STATE), F32),
                        pltpu.VMEM((rows, MIX), F32)],
        compiler_params=_params("parallel"),
        name="s5_lat" if nseq == DEC_BATCH else "s5_ctx",
    )(u_tb, bbr, bbi, ccr, cci, lam, h0r, h0i, d_skip.reshape(1, MIX), w_glu_bf16)


def _s5_matrices(lam_re, lam_im, log_dt, b_re, b_im, c_re, c_im):
    dt = jnp.exp(log_dt)[..., None]
    mag = jnp.exp(lam_re * dt)
    ab_re = mag * jnp.cos(lam_im * dt)
    ab_im = mag * jnp.sin(lam_im * dt)
    den = lam_re * lam_re + lam_im * lam_im
    nr = ab_re - 1.0
    f_re = (nr * lam_re + ab_im * lam_im) / den
    f_im = (ab_im * lam_re - nr * lam_im) / den
    bb_re = f_re[..., None] * b_re - f_im[..., None] * b_im
    bb_im = f_re[..., None] * b_im + f_im[..., None] * b_re
    gl = G_C // S5_KB
    eye = jnp.eye(gl, dtype=F32)

    def in_proj(bb):
        x = bb.reshape(2, S5_KB, gl, P_C, S5_GROUP)
        return jnp.einsum('dkgps,gh->dkgshp', x, eye).reshape(2, S5_KB, S5_KW, S5_NW).astype(BF16)

    def out_proj(cc):
        x = cc.reshape(2, S5_KB, gl, S5_GROUP, P_C)
        return jnp.einsum('dkgsp,gh->dkgphs', x, eye).reshape(2, S5_KB, S5_NW, S5_KW).astype(BF16)

    lam = jnp.stack([ab_re.reshape(2, 1, S5_STATE), ab_im.reshape(2, 1, S5_STATE)], axis=1)
    return in_proj(bb_re), in_proj(bb_im), out_proj(c_re), out_proj(c_im), lam


def _ret_kernel(*refs, t, latent):
    if latent:
        (q_ref, k_ref, v_ref, g_ref, dl_ref, lnw_ref, lnb_ref, r0_ref, cos_ref, sin_ref,
         o_ref, rfin_ref, ob_ref) = refs
    else:
        q_ref, k_ref, v_ref, g_ref, dl_ref, lnw_ref, lnb_ref, r0_ref, o_ref, rfin_ref, ob_ref = refs
    n = t // CHUNK
    lg = -_softplus(-dl_ref[0])
    jf = lax.broadcasted_iota(jnp.int32, (CHUNK, CHUNK), 0).astype(F32)
    kf = lax.broadcasted_iota(jnp.int32, (CHUNK, CHUNK), 1).astype(F32)
    diff = jf - kf
    tabs = []
    for dr in range(2):
        l = lg[dr:dr + 1]
        if dr == 0:
            dmat = jnp.where(diff >= 0, jnp.exp(l * jnp.maximum(diff, 0.0)), 0.0)
            xi = jnp.exp(l * (jf + 1.0))
            zeta = jnp.exp(l * (CHUNK - 1.0 - jf))
        else:
            dmat = jnp.where(diff < 0, jnp.exp(l * jnp.maximum(-diff, 0.0)), 0.0)
            xi = jnp.exp(l * (CHUNK - jf))
            zeta = jnp.exp(l * jf)
        tabs.append((dmat, xi, zeta, jnp.exp(l * CHUNK)))

    def body(ci, states):
        new_states = []
        for dr, dst in enumerate((o_ref, ob_ref)):
            dmat, xi, zeta, gch = tabs[dr]
            cj = ci if dr == 0 else n - 1 - ci
            rows = pl.ds(pl.multiple_of(cj * CHUNK, CHUNK), CHUNK)
            q = q_ref[rows, :]
            k = k_ref[rows, :] * (HD_D ** -0.5)
            v = v_ref[rows, :]
            if latent:
                q = _rope_pairs(q, cos_ref[rows, :], sin_ref[rows, :], HD_D // 2)
                k = _rope_pairs(k, cos_ref[rows, :], sin_ref[rows, :], HD_D // 2)
            st = states[dr]
            inner = _bdot(q, k, NT) * dmat
            dst[rows, :] = _bdot(inner, v) + _bdot(q, st) * xi
            new_states.append(st * gch + _bdot(k * zeta, v, TN))
        return tuple(new_states)

    fin = lax.fori_loop(0, n, body, (r0_ref[0, 0, 0], r0_ref[0, 1, 0]))
    rfin_ref[0, 0, 0] = fin[0]
    rfin_ref[0, 1, 0] = fin[1]
    y = o_ref[...] + ob_ref[...]
    yc = y - jnp.mean(y, axis=-1, keepdims=True)
    yn = yc * lax.rsqrt(jnp.mean(yc * yc, axis=-1, keepdims=True) + RET_GN_EPS)
    o_ref[...] = (yn * lnw_ref[...] + lnb_ref[...]) * _silu(g_ref[...])


def _retention(p, decay_logit, ln_w, ln_b, r0, latent, rope=None):
    if latent:
        nseq, t, rb0 = DEC_BATCH, DEC_SEQ, M_CTX // DEC_SEQ
    else:
        nseq, t, rb0 = BATCH, SEQ, 0
    w = HD_D
    dl = jnp.broadcast_to(decay_logit.T[:, :, None], (H_D, 2, w))
    in_specs = [pl.BlockSpec((t, w), lambda b, h: (rb0 + b, H_D + h)),
                pl.BlockSpec((t, w), lambda b, h: (rb0 + b, 2 * H_D + h)),
                pl.BlockSpec((t, w), lambda b, h: (rb0 + b, 3 * H_D + h)),
                pl.BlockSpec((t, w), lambda b, h: (rb0 + b, 4 * H_D + h)),
                pl.BlockSpec((1, 2, w), lambda b, h: (h, 0, 0)),
                pl.BlockSpec((1, w), lambda b, h: (0, h)),
                pl.BlockSpec((1, w), lambda b, h: (0, h)),
                pl.BlockSpec((1, 2, 1, w, w), lambda b, h: (b, 0, h, 0, 0))]
    args = [p, p, p, p, dl, ln_w.reshape(1, MIX), ln_b.reshape(1, MIX), r0]
    if latent:
        in_specs += [pl.BlockSpec((t, w), lambda b, h: (0, 0))] * 2
        args += list(rope)
    return pl.pallas_call(
        functools.partial(_ret_kernel, t=t, latent=latent),
        grid=(nseq, H_D),
        in_specs=in_specs,
        out_specs=[pl.BlockSpec((t, w), lambda b, h: (b, h)),
                   pl.BlockSpec((1, 2, 1, w, w), lambda b, h: (b, 0, h, 0, 0))],
        out_shape=[jax.ShapeDtypeStruct((nseq * t, MIX), F32),
                   jax.ShapeDtypeStruct((nseq, 2, H_D, w, w), F32)],
        scratch_shapes=[pltpu.VMEM((t, w), F32)],
        compiler_params=_params("parallel", "parallel"),
        name="retention_lat" if latent else "retention_ctx",
    )(*args)


def _rope_tables(n_tok, dim, reps):
    rows = n_tok // GRID_W
    n_freq = dim // 4
    inv = 1.0 / (ROPE_THETA ** (jnp.arange(n_freq, dtype=F32) / n_freq))
    row = jnp.repeat(jnp.arange(rows, dtype=F32), GRID_W)
    col = jnp.tile(jnp.arange(GRID_W, dtype=F32), rows)
    ang = jnp.concatenate([row[:, None] * inv, col[:, None] * inv], axis=-1)
    cos, sin = jnp.cos(ang), jnp.sin(ang)
    return jnp.tile(jnp.concatenate([cos, cos], axis=1), (1, reps)), \
        jnp.tile(jnp.concatenate([-sin, sin], axis=1), (1, reps))


def _time_major(x, nseq, group):
    t = x.shape[0] // nseq
    x = x.reshape(nseq // group, group, t, x.shape[1])
    return jnp.swapaxes(x, 1, 2).reshape(nseq // group, t * group, x.shape[-1])


def _seq_major(x, nseq, group):
    t = x.shape[1] // group
    x = x.reshape(nseq // group, t, group, x.shape[-1])
    return jnp.swapaxes(x, 1, 2).reshape(nseq * t, x.shape[-1])


def kernel(x_prompt, x_sample, cache_k_ab, cache_v_ab, state_rwkv, state_s5_re, state_s5_im, state_ret, c, c_ctx, norm1_g, norm2_g, w_mod, b_mod, w_ff_gate, w_ff_up, w_ff_down, w_in_ab, w_out_ab, qk_gain_a, lambda_qk, subln_g, rwkv_mu, rwkv_k_k, rwkv_k_a, rwkv_r_k, rwkv_w0, rwkv_w_up, rwkv_a0, rwkv_a_up, rwkv_g_up, rwkv_ln_w, rwkv_ln_b, w_in_cd, w_out_cd, s5_lam_re, s5_lam_im, s5_log_dt, s5_b_re, s5_b_im, s5_c_re, s5_c_im, s5_d, s5_w_glu, ret_decay_logit, ret_ln_w, ret_ln_b):
    d = D_MODEL
    x = jnp.concatenate([x_prompt.reshape(M_CTX, d), x_sample.reshape(M_LAT, d)], axis=0)
    cvec = jnp.zeros((MOD_ROWS, d), F32).at[0].set(c_ctx).at[1:1 + DEC_BATCH].set(c)
    mods = _modulation(cvec, w_mod, b_mod)

    lam_init = 0.8 - 0.6 * math.exp(-0.3 * 0)
    p = _norm_linear(x, norm1_g[0], mods[0], w_in_ab[0].astype(BF16))
    gain2 = jnp.tile(qk_gain_a[0], (1, 2))
    rope_a = _rope_tables(DEC_SEQ, HD_A, 2)
    ck = cache_k_ab[:, 0].reshape(DEC_BATCH * PAST_LEN, MIX)
    cv = cache_v_ab[:, 0].reshape(DEC_BATCH * PAST_LEN, MIX)
    oa_ctx, k_ctx = _attention(p, gain2, lambda_qk[0], subln_g[0], lam_init, latent=False)
    oa_lat = _attention(p, gain2, lambda_qk[0], subln_g[0], lam_init, latent=True,
                        cache_k=ck, cache_v=cv, rope=rope_a)
    rw_prm = (rwkv_mu[0], rwkv_k_k[0], rwkv_k_a[0], rwkv_r_k[0], rwkv_w0[0], rwkv_w_up[0],
              rwkv_a0[0], rwkv_a_up[0], rwkv_g_up[0])
    s0_ctx = jnp.zeros((BATCH, 2, H_B, HD_B, HD_B), F32)
    s0_lat = jnp.swapaxes(state_rwkv[:, 0], -1, -2)
    ob_ctx, sfin_ctx = _rwkv_scan(_rwkv_prep(p, rw_prm, latent=False), rwkv_ln_w[0], rwkv_ln_b[0],
                                  s0_ctx, latent=False)
    ob_lat, _ = _rwkv_scan(_rwkv_prep(p, rw_prm, latent=True), rwkv_ln_w[0], rwkv_ln_b[0],
                           s0_lat, latent=True)
    x = _proj_residual(x, jnp.concatenate([oa_ctx, oa_lat], axis=0),
                       jnp.concatenate([ob_ctx, ob_lat], axis=0), w_out_ab[0].astype(BF16), mods[0])
    x = _ffn(x, norm2_g[0], mods[0], w_ff_gate[0].astype(BF16), w_ff_up[0].astype(BF16),
             w_ff_down[0].astype(BF16))

    p2 = _norm_linear(x, norm1_g[1], mods[1], w_in_cd[0].astype(BF16))
    mats = _s5_matrices(s5_lam_re[0], s5_lam_im[0], s5_log_dt[0], s5_b_re[0], s5_b_im[0],
                        s5_c_re[0], s5_c_im[0])
    wglu = s5_w_glu[0].astype(BF16)
    grp = 8
    u_ctx = _time_major(p2[:M_CTX, :MIX], BATCH, grp)
    u_lat = _time_major(p2[M_CTX:, :MIX], DEC_BATCH, DEC_BATCH)
    z_ctx = jnp.zeros((BATCH // grp, 2, grp, S5_STATE), F32)
    oc_ctx, hfr, hfi = _s5(u_ctx, mats, z_ctx, z_ctx, s5_d[0], wglu, grp)
    h0r = jnp.swapaxes(state_s5_re[:, 0].reshape(DEC_BATCH, 2, S5_STATE), 0, 1)[None]
    h0i = jnp.swapaxes(state_s5_im[:, 0].reshape(DEC_BATCH, 2, S5_STATE), 0, 1)[None]
    oc_lat, _, _ = _s5(u_lat, mats, h0r, h0i, s5_d[0], wglu, DEC_BATCH)
    oc = jnp.concatenate([_seq_major(oc_ctx, BATCH, grp), _seq_major(oc_lat, DEC_BATCH, DEC_BATCH)], axis=0)
    rope_d = _rope_tables(DEC_SEQ, HD_D, 1)
    r0_ctx = jnp.zeros((BATCH, 2, H_D, HD_D, HD_D), F32)
    od_ctx, rfin = _retention(p2, ret_decay_logit[0], ret_ln_w[0], ret_ln_b[0], r0_ctx, latent=False)
    od_lat, _ = _retention(p2, ret_decay_logit[0], ret_ln_w[0], ret_ln_b[0], state_ret[:, 0],
                           latent=True, rope=rope_d)
    x = _proj_residual(x, oc, jnp.concatenate([od_ctx, od_lat], axis=0), w_out_cd[0].astype(BF16), mods[1])
    x = _ffn(x, norm2_g[1], mods[1], w_ff_gate[1].astype(BF16), w_ff_up[1].astype(BF16),
             w_ff_down[1].astype(BF16))

    y_prompt = x[:M_CTX].reshape(BATCH, SEQ, d)
    y_sample = x[M_CTX:].reshape(DEC_BATCH, DEC_SEQ, d)
    new_k = k_ctx.reshape(BATCH, 1, SEQ, H_A, 2, HD_A)
    new_v = p[:M_CTX, 2 * MIX:3 * MIX].reshape(BATCH, 1, SEQ, H_A, VD_A)
    new_rwkv = jnp.swapaxes(sfin_ctx, -1, -2)[:, None]
    s5_shape = (BATCH, 2, G_C, P_C)
    new_s5_re = jnp.swapaxes(hfr, 1, 2).reshape(s5_shape)[:, None]
    new_s5_im = jnp.swapaxes(hfi, 1, 2).reshape(s5_shape)[:, None]
    new_ret = rfin[:, None]
    return (y_prompt, y_sample, new_k, new_v, new_rwkv, new_s5_re, new_s5_im, new_ret)
```

```python
import functools
import math

import numpy as np
import jax
import jax.numpy as jnp
from jax import lax
from jax.experimental import pallas as pl
from jax.experimental.pallas import tpu as pltpu

F32 = jnp.float32
BF16 = jnp.bfloat16
HIGHEST = lax.Precision.HIGHEST

D_MODEL = 1024
BATCH = 32
SEQ = 256
DEC_BATCH = 2
DEC_SEQ = 1024
PAST_LEN = 256
GRID_W = 64
H_A = 4
HD_A = 64
VD_A = 128
H_B = 8
HD_B = 64
MIX = 512
LORA_W = 64
LORA_A = 64
LORA_G = 128
S5_GROUP = 16
G_C = 32
P_C = 64
S5_STATE = G_C * P_C
H_D = 4
HD_D = 128
CHUNK = 128
D_FF = 2816
IN_AB = 3328
IN_CD = 2560
ROPE_THETA = 10000.0
NORM_EPS = 1e-6
RWKV_GN_EPS = 64e-5
RET_GN_EPS = 1e-5

M_CTX = BATCH * SEQ
M_LAT = DEC_BATCH * DEC_SEQ
M_ALL = M_CTX + M_LAT
MOD_ROWS = 8
RWKV_CHUNK = 64
VMEM_LIMIT = 56 * 1024 * 1024

NN = (((1,), (0,)), ((), ()))
NT = (((1,), (1,)), ((), ()))
TN = (((0,), (0,)), ((), ()))


def _params(*sem):
    return pltpu.CompilerParams(dimension_semantics=sem, vmem_limit_bytes=VMEM_LIMIT)


def _bdot(a, b, dims=NN):
    return lax.dot_general(a.astype(BF16), b.astype(BF16), dims, preferred_element_type=F32)


def _hdot(a, b, dims=NN):
    return lax.dot_general(a, b, dims, precision=HIGHEST, preferred_element_type=F32)


def _split_dot(x, m):
    hi = x.astype(BF16)
    lo = (x - hi.astype(F32)).astype(BF16)
    return (jnp.dot(hi, m, preferred_element_type=F32) + jnp.dot(lo, m, preferred_element_type=F32))


def _seg_matrix(n, shift, val):
    r = lax.broadcasted_iota(jnp.int32, (n, n), 0) >> shift
    c = lax.broadcasted_iota(jnp.int32, (n, n), 1) >> shift
    return jnp.where(r == c, val, 0.0).astype(BF16)


def _sigmoid(x):
    return jax.nn.sigmoid(x)


def _silu(x):
    return x * jax.nn.sigmoid(x)


def _softplus(x):
    return jnp.maximum(x, 0.0) + jnp.log(1.0 + jnp.exp(-jnp.abs(x)))


def _mod_row(tile, tm):
    r0 = tile * tm
    return jnp.where(r0 < M_CTX, 0, 1 + (r0 - M_CTX) // DEC_SEQ)


def _norm_mod(x, g, sc_ref, sh_ref, row):
    y = x * lax.rsqrt(jnp.mean(x * x, axis=-1, keepdims=True) + NORM_EPS) * g
    return y * (1.0 + sc_ref[pl.ds(row, 1), :]) + sh_ref[pl.ds(row, 1), :]


def _mod_kernel(c_ref, w_ref, b_ref, o_ref):
    o_ref[0] = _hdot(_silu(c_ref[...]), w_ref[0]) + b_ref[0]


def _modulation(cvec, w_mod, b_mod):
    depth, d, n6 = w_mod.shape
    tn = 1536
    return pl.pallas_call(
        _mod_kernel,
        grid=(depth, n6 // tn),
        in_specs=[pl.BlockSpec((MOD_ROWS, d), lambda l, j: (0, 0)),
                  pl.BlockSpec((1, d, tn), lambda l, j: (l, 0, j)),
                  pl.BlockSpec((1, 1, tn), lambda l, j: (l, 0, j))],
        out_specs=pl.BlockSpec((1, MOD_ROWS, tn), lambda l, j: (l, 0, j)),
        out_shape=jax.ShapeDtypeStruct((depth, MOD_ROWS, n6), F32),
        compiler_params=_params("parallel", "parallel"),
        name="modulation",
    )(cvec, w_mod, b_mod.reshape(depth, 1, n6))


def _norm_linear_kernel(x_ref, g_ref, sc_ref, sh_ref, w_ref, o_ref, *, tm):
    row = _mod_row(pl.program_id(0), tm)
    h = _norm_mod(x_ref[...], g_ref[...], sc_ref, sh_ref, row)
    o_ref[...] = jnp.dot(h.astype(BF16), w_ref[...], preferred_element_type=F32)


def _norm_linear(x, g, mods, w_bf16):
    tm = 256
    m, d = x.shape
    n = w_bf16.shape[1]
    return pl.pallas_call(
        functools.partial(_norm_linear_kernel, tm=tm),
        grid=(m // tm,),
        in_specs=[pl.BlockSpec((tm, d), lambda i: (i, 0)),
                  pl.BlockSpec((1, d), lambda i: (0, 0)),
                  pl.BlockSpec((MOD_ROWS, d), lambda i: (0, 1)),
                  pl.BlockSpec((MOD_ROWS, d), lambda i: (0, 0)),
                  pl.BlockSpec((d, n), lambda i: (0, 0))],
        out_specs=pl.BlockSpec((tm, n), lambda i: (i, 0)),
        out_shape=jax.ShapeDtypeStruct((m, n), F32),
        compiler_params=_params("parallel"),
        name="norm_linear",
    )(x, g.reshape(1, d), mods, mods, w_bf16)


def _proj_res_kernel(x_ref, a_ref, b_ref, wa_ref, wb_ref, gt_ref, o_ref, *, tm):
    row = _mod_row(pl.program_id(0), tm)
    o = (jnp.dot(a_ref[...].astype(BF16), wa_ref[...], preferred_element_type=F32)
         + jnp.dot(b_ref[...].astype(BF16), wb_ref[...], preferred_element_type=F32))
    o_ref[...] = x_ref[...] + gt_ref[pl.ds(row, 1), :] * o


def _proj_residual(x, oa, ob, w_bf16, mods):
    tm = 256
    m, d = x.shape
    return pl.pallas_call(
        functools.partial(_proj_res_kernel, tm=tm),
        grid=(m // tm,),
        in_specs=[pl.BlockSpec((tm, d), lambda i: (i, 0)),
                  pl.BlockSpec((tm, MIX), lambda i: (i, 0)),
                  pl.BlockSpec((tm, MIX), lambda i: (i, 0)),
                  pl.BlockSpec((MIX, d), lambda i: (0, 0)),
                  pl.BlockSpec((MIX, d), lambda i: (1, 0)),
                  pl.BlockSpec((MOD_ROWS, d), lambda i: (0, 2))],
        out_specs=pl.BlockSpec((tm, d), lambda i: (i, 0)),
        out_shape=jax.ShapeDtypeStruct((m, d), F32),
        compiler_params=_params("parallel"),
        name="proj_residual",
    )(x, oa, ob, w_bf16, w_bf16, mods)


def _ffn_kernel(x_ref, g_ref, sc_ref, sh_ref, gt_ref, wg_ref, wu_ref, wd_ref, o_ref, *, tm, ck):
    row = _mod_row(pl.program_id(0), tm)
    x = x_ref[...]
    h = _norm_mod(x, g_ref[...], sc_ref, sh_ref, row).astype(BF16)
    acc = jnp.zeros((tm, D_MODEL), F32)
    for c in range(D_FF // ck):
        gg = jnp.dot(h, wg_ref[:, c * ck:(c + 1) * ck], preferred_element_type=F32)
        uu = jnp.dot(h, wu_ref[:, c * ck:(c + 1) * ck], preferred_element_type=F32)
        act = (_silu(gg) * uu).astype(BF16)
        acc = acc + jnp.dot(act, wd_ref[c * ck:(c + 1) * ck, :], preferred_element_type=F32)
    o_ref[...] = x + gt_ref[pl.ds(row, 1), :] * acc


def _ffn(x, g, mods, wg, wu, wd):
    tm, ck = 512, 256
    m, d = x.shape
    return pl.pallas_call(
        functools.partial(_ffn_kernel, tm=tm, ck=ck),
        grid=(m // tm,),
        in_specs=[pl.BlockSpec((tm, d), lambda i: (i, 0)),
                  pl.BlockSpec((1, d), lambda i: (0, 0)),
                  pl.BlockSpec((MOD_ROWS, d), lambda i: (0, 4)),
                  pl.BlockSpec((MOD_ROWS, d), lambda i: (0, 3)),
                  pl.BlockSpec((MOD_ROWS, d), lambda i: (0, 5)),
                  pl.BlockSpec((d, D_FF), lambda i: (0, 0)),
                  pl.BlockSpec((d, D_FF), lambda i: (0, 0)),
                  pl.BlockSpec((D_FF, d), lambda i: (0, 0))],
        out_specs=pl.BlockSpec((tm, d), lambda i: (i, 0)),
        out_shape=jax.ShapeDtypeStruct((m, d), F32),
        compiler_params=_params("parallel"),
        name="ffn",
    )(x, g.reshape(1, d), mods, mods, mods, wg, wu, wd)


def _qk_norm(x, gain, segm):
    ms = _split_dot(x * x, segm)
    return x * lax.rsqrt(ms + NORM_EPS) * gain


def _rope_pairs(x, cosf, sinf, half):
    lane = lax.broadcasted_iota(jnp.int32, x.shape, 1)
    first = (lane & (2 * half - 1)) < half
    n = x.shape[1]
    partner = jnp.where(first, pltpu.roll(x, n - half, axis=1), pltpu.roll(x, half, axis=1))
    return x * cosf + partner * sinf


def _attn_kernel(*refs, latent, lam_init):
    if latent:
        (q_ref, k_ref, v_ref, ck_ref, cv_ref, cosq_ref, sinq_ref, cosk_ref, sink_ref,
         gain_ref, lam_ref, sub_ref, o_ref, kall, vall) = refs
    else:
        q_ref, k_ref, v_ref, gain_ref, lam_ref, sub_ref, o_ref, kn_ref, kall, vall = refs
    w = 2 * HD_A
    segm = _seg_matrix(w, 6, 1.0 / HD_A)
    gains = gain_ref[...]
    cols = [slice(h * w, (h + 1) * w) for h in range(H_A)]

    @pl.when(pl.program_id(1) == 0)
    def _():
        k = [_qk_norm(k_ref[:, c], gains[1:2], segm) for c in cols]
        if latent:
            k = [_rope_pairs(x, cosk_ref[...], sink_ref[...], HD_A // 2) for x in k]
            for c, x in zip(cols, k):
                kall[0:PAST_LEN, c] = ck_ref[:, c].astype(BF16)
                kall[PAST_LEN:, c] = x.astype(BF16)
                vall[0:PAST_LEN, c] = cv_ref[:, c].astype(BF16)
                vall[PAST_LEN:, c] = v_ref[:, c].astype(BF16)
        else:
            for c, x in zip(cols, k):
                kn_ref[:, c] = x
                kall[:, c] = x.astype(BF16)
                vall[:, c] = v_ref[:, c].astype(BF16)

    q = [_qk_norm(q_ref[:, c], gains[0:1], segm) for c in cols]
    if latent:
        q = [_rope_pairs(x, cosq_ref[...], sinq_ref[...], HD_A // 2) for x in q]
    lv = lam_ref[...]
    lam = (jnp.exp(jnp.sum(lv[0:1] * lv[1:2], axis=1, keepdims=True))
           - jnp.exp(jnp.sum(lv[2:3] * lv[3:4], axis=1, keepdims=True)) + lam_init)
    scale = HD_A ** -0.5
    comp0 = lax.broadcasted_iota(jnp.int32, q[0].shape, 1) < HD_A
    qc = [jnp.where(comp0, *sel).astype(BF16) for x in q for sel in ((x, 0.0), (0.0, x))]
    s = [lax.dot_general(qc[i], kall[:, cols[i // 2]], NT, preferred_element_type=F32) * scale
         for i in range(2 * H_A)]
    e = [jnp.exp(x - jnp.max(x, axis=-1, keepdims=True)) for x in s]
    p = [x / jnp.sum(x, axis=-1, keepdims=True) for x in e]
    att = [(p[2 * h] - lam * p[2 * h + 1]).astype(BF16) for h in range(H_A)]
    o = [jnp.dot(att[h], vall[:, cols[h]], preferred_element_type=F32) for h in range(H_A)]
    o = [x * lax.rsqrt(jnp.mean(x * x, axis=-1, keepdims=True) + NORM_EPS) * sub_ref[...] for x in o]
    for c, x in zip(cols, o):
        o_ref[:, c] = x * (1.0 - lam_init)


def _attention(p, gain2, lambda_qk, subln_g, lam_init, latent, cache_k=None, cache_v=None, rope=None):
    w = 2 * HD_A
    if latent:
        nseq, t, tq, rb0, s_len = DEC_BATCH, DEC_SEQ, 128, M_CTX // DEC_SEQ, PAST_LEN + DEC_SEQ
    else:
        nseq, t, tq, rb0, s_len = BATCH, SEQ, SEQ, 0, SEQ
    nq = t // tq
    qoff = rb0 * nq
    full = lambda shape: pl.BlockSpec(shape, lambda b, i: (0,) * len(shape))
    in_specs = [pl.BlockSpec((tq, MIX), lambda b, i: (qoff + b * nq + i, 0)),
                pl.BlockSpec((t, MIX), lambda b, i: (rb0 + b, 1)),
                pl.BlockSpec((t, MIX), lambda b, i: (rb0 + b, 2))]
    args = [p, p, p]
    if latent:
        cosf, sinf = rope
        in_specs += [pl.BlockSpec((PAST_LEN, MIX), lambda b, i: (b, 0)),
                     pl.BlockSpec((PAST_LEN, MIX), lambda b, i: (b, 0)),
                     pl.BlockSpec((tq, w), lambda b, i: (i, 0)),
                     pl.BlockSpec((tq, w), lambda b, i: (i, 0)),
                     full((t, w)), full((t, w))]
        args += [cache_k, cache_v, cosf, sinf, cosf, sinf]
    in_specs += [full((2, w)), full((4, HD_A)), full((1, w))]
    args += [gain2, lambda_qk, subln_g.reshape(1, w)]
    o_spec = pl.BlockSpec((tq, MIX), lambda b, i: (b * nq + i, 0))
    o_shape = jax.ShapeDtypeStruct((nseq * t, MIX), F32)
    if latent:
        out_specs, out_shape = o_spec, o_shape
    else:
        out_specs = [o_spec, pl.BlockSpec((t, MIX), lambda b, i: (b, 0))]
        out_shape = [o_shape, jax.ShapeDtypeStruct((nseq * t, MIX), F32)]
    return pl.pallas_call(
        functools.partial(_attn_kernel, latent=latent, lam_init=lam_init),
        grid=(nseq, nq),
        in_specs=in_specs, out_specs=out_specs, out_shape=out_shape,
        scratch_shapes=[pltpu.VMEM((s_len, MIX), BF16), pltpu.VMEM((s_len, MIX), BF16)],
        compiler_params=_params("parallel", "arbitrary"),
        name="diff_attention_lat" if latent else "diff_attention_ctx",
    )(*args)


def _centred_shift(x, mu):
    t = x.shape[0]
    row = lax.broadcasted_iota(jnp.int32, x.shape, 0)
    prev = jnp.where(row == 0, 0.0, pltpu.roll(x, 1, axis=0))
    nxt = jnp.where(row == t - 1, 0.0, pltpu.roll(x, t - 1, axis=0))
    return x + (0.5 * (prev + nxt) - x) * mu


def _seg_sum(x, segm):
    return jnp.concatenate([_split_dot(x[:, j * 128:(j + 1) * 128], segm) for j in range(x.shape[1] // 128)],
                           axis=1)


def _rwkv_prep_kernel(r_ref, k_ref, v_ref, l_ref, mur_ref, muk_ref, muv_ref, mul_ref,
                      kk_ref, ka_ref, rk_ref, w0_ref, wup_ref, a0_ref, aup_ref, gup_ref,
                      ro_ref, ldf_ref, ldb_ref, kbo_ref, vbo_ref, kko_ref, ao_ref, gate_ref, bonus_ref):
    seg1 = _seg_matrix(128, 6, 1.0)
    r = _centred_shift(r_ref[...], mur_ref[...])
    kb = _centred_shift(k_ref[...], muk_ref[...])
    vb = _centred_shift(v_ref[...], muv_ref[...])
    lo = _centred_shift(l_ref[...], mul_ref[...])
    xw = lo[:, 0:LORA_W]
    xa = lo[:, LORA_W:LORA_W + LORA_A]
    xg = lo[:, LORA_W + LORA_A:]
    kk = kb * kk_ref[...]
    kk = kk * lax.rsqrt(_seg_sum(kk * kk, seg1) + 1e-12)
    a = _sigmoid(a0_ref[...] + _bdot(xa, aup_ref[...]))
    kb2 = kb * (1.0 + (a - 1.0) * ka_ref[...])
    lw = jnp.tanh(xw)
    for dr, ld_ref in enumerate((ldf_ref, ldb_ref)):
        z = w0_ref[dr:dr + 1, :] + _bdot(lw, wup_ref[dr])
        logw = -_softplus(-z) - 0.5
        ld_ref[...] = -jnp.exp(logw)
    gate_ref[...] = _bdot(_sigmoid(xg), gup_ref[...])
    bonus_ref[...] = _seg_sum(r * kb2 * rk_ref[...], seg1) * vb
    ro_ref[...] = r
    kbo_ref[...] = kb2
    vbo_ref[...] = vb
    kko_ref[...] = kk
    ao_ref[...] = a


def _rwkv_prep(p, prm, latent):
    mu, k_k, k_a, r_k, w0, w_up, a0, a_up, g_up = prm
    if latent:
        nseq, t, rb0 = DEC_BATCH, DEC_SEQ, M_CTX // DEC_SEQ
    else:
        nseq, t, rb0 = BATCH, SEQ, 0
    lw = LORA_W + LORA_A + LORA_G
    c0 = (IN_AB - 3 * MIX - lw) // MIX
    cl = (IN_AB - lw) // lw
    row = lambda n: pl.BlockSpec((1, n), lambda b: (0, 0))
    in_specs = [pl.BlockSpec((t, MIX), lambda b: (rb0 + b, c0)),
                pl.BlockSpec((t, MIX), lambda b: (rb0 + b, c0 + 1)),
                pl.BlockSpec((t, MIX), lambda b: (rb0 + b, c0 + 2)),
                pl.BlockSpec((t, lw), lambda b: (rb0 + b, cl)),
                row(MIX), row(MIX), row(MIX), row(lw),
                row(MIX), row(MIX), row(MIX),
                pl.BlockSpec((2, MIX), lambda b: (0, 0)),
                pl.BlockSpec((2, LORA_W, MIX), lambda b: (0, 0, 0)),
                row(MIX),
                pl.BlockSpec((LORA_A, MIX), lambda b: (0, 0)),
                pl.BlockSpec((LORA_G, MIX), lambda b: (0, 0))]
    o_spec = pl.BlockSpec((t, MIX), lambda b: (b, 0))
    o_shape = jax.ShapeDtypeStruct((nseq * t, MIX), F32)
    return pl.pallas_call(
        _rwkv_prep_kernel,
        grid=(nseq,),
        in_specs=in_specs, out_specs=[o_spec] * 9, out_shape=[o_shape] * 9,
        compiler_params=_params("parallel"),
        name="rwkv_prep_lat" if latent else "rwkv_prep_ctx",
    )(p, p, p, p,
      mu[None, 0:MIX], mu[None, MIX:2 * MIX], mu[None, 2 * MIX:3 * MIX], mu[None, 3 * MIX:],
      k_k.reshape(1, MIX), k_a.reshape(1, MIX), r_k.reshape(1, MIX), w0, w_up,
      a0.reshape(1, MIX), a_up, g_up)


def _split2(x):
    hi = x.astype(BF16)
    return hi, (x - hi.astype(F32)).astype(BF16)


RWKV_INV_BLOCK = 16


def _rwkv_masks(c, rev):
    ti = lax.broadcasted_iota(jnp.int32, (c, c), 0)
    si = lax.broadcasted_iota(jnp.int32, (c, c), 1)
    tri = jnp.where((si >= ti) if rev else (si <= ti), 1.0, 0.0).astype(BF16)
    t4 = lax.broadcasted_iota(jnp.int32, (4 * c, 4 * c), 0)
    s4 = lax.broadcasted_iota(jnp.int32, (4 * c, 4 * c), 1)
    tm, sm = t4 & (c - 1), s4 & (c - 1)
    strict = (sm > tm) if rev else (sm < tm)
    incl = (sm >= tm) if rev else (sm <= tm)
    same_head = ((t4 // c) & 1) == ((s4 // c) & 1)
    top = t4 < 2 * c
    gmask = same_head & ((top & strict) | (~top & incl))
    t2 = lax.broadcasted_iota(jnp.int32, (2 * c, 2 * c), 0)
    s2 = lax.broadcasted_iota(jnp.int32, (2 * c, 2 * c), 1)
    same = lambda n: (t2 // n) == (s2 // n)
    levels = []
    n = RWKV_INV_BLOCK
    while n < c:
        levels.append(same(2 * n) & ~same(n))
        n *= 2
    f = lambda m: jnp.where(m, 1.0, 0.0)
    return tri, f(gmask), f(same(RWKV_INV_BLOCK)), tuple(f(m) for m in levels), f(same(c)), f(t2 == s2)


def _keep(mask01, x):
    return jnp.where(mask01 > 0.5, x, 0.0)


def _tri_inverse(a, diag_blk, levels, eye):
    n = a[0].shape[0]
    d = [_keep(diag_blk, x) for x in a]
    t = [eye + x for x in d]
    p = [_bdot(x, x) for x in d]
    for _ in range(int(math.log2(RWKV_INV_BLOCK)) - 2):
        res = [_bdot(jnp.concatenate([pi, ti], axis=0), pi) for pi, ti in zip(p, t)]
        p = [x[:n] for x in res]
        t = [ti + x[n:] for ti, x in zip(t, res)]
    t = [ti + _bdot(ti, pi) for ti, pi in zip(t, p)]
    for off in levels:
        half = [_bdot(ti, _keep(off, x)) for ti, x in zip(t, a)]
        t = [ti + _bdot(x, ti) for ti, x in zip(t, half)]
    return t


def _rwkv_pair_chunks(ins, sts, tris, gmasks, diag_blk, levels, same_head, eye):
    c, w = ins[0][0].shape
    hd = w // 2
    nch = range(len(ins))
    r, ld, kb, vb, kk, a = (list(z) for z in zip(*ins))
    split = [_split2(x) for x in ld]
    lcum = [jnp.dot(tris[i], split[i][0], preferred_element_type=F32)
            + jnp.dot(tris[i], split[i][1], preferred_element_type=F32) for i in nch]
    ones = jnp.ones((c, w), BF16)
    pcol = [jnp.exp(lax.dot_general(hi, ones, TN, preferred_element_type=F32)
                    + lax.dot_general(lo, ones, TN, preferred_element_type=F32)) for hi, lo in split]
    ltot = [jnp.sum(x, axis=0, keepdims=True) for x in ld]
    beta = [kk[i] * a[i] for i in nch]
    eneg = [jnp.exp(-x) for x in lcum]
    abar = [-kk[i] * jnp.exp(lcum[i] - ld[i]) for i in nch]
    rbar = [r[i] * jnp.exp(lcum[i]) for i in nch]
    bt = [(beta[i] * eneg[i]).astype(BF16) for i in nch]
    kt = [(kb[i] * eneg[i]).astype(BF16) for i in nch]
    vbb = [x.astype(BF16) for x in vb]
    head0 = lax.broadcasted_iota(jnp.int32, (c, w), 1) < hd
    pick = lambda res: jnp.where(head0, res[:c], res[c:])
    arst = [_bdot(jnp.concatenate([abar[i], rbar[i]], axis=0), sts[i]) for i in nch]
    lhs = [jnp.concatenate([jnp.where(head0, abar[i], 0.0), jnp.where(head0, 0.0, abar[i]),
                            jnp.where(head0, rbar[i], 0.0), jnp.where(head0, 0.0, rbar[i])], axis=0) for i in nch]
    g = [_keep(gmasks[i], _bdot(lhs[i], jnp.concatenate([bt[i], bt[i], kt[i], kt[i]], axis=0), NT))
         for i in nch]
    x = [arst[i][:c] + pick(_bdot(g[i][:2 * c, 2 * c:], jnp.concatenate([vbb[i], vbb[i]], axis=0))) for i in nch]
    tinv = _tri_inverse([gi[:2 * c, :2 * c] for gi in g], diag_blk, levels, eye)
    u = [pick(_bdot(tinv[i], jnp.concatenate([x[i], x[i]], axis=0))) for i in nch]
    ub = [z.astype(BF16) for z in u]
    y = [arst[i][c:] + pick(_bdot(g[i][2 * c:], jnp.concatenate([ub[i], ub[i], vbb[i], vbb[i]], axis=0)))
         for i in nch]
    erem = [jnp.exp(ltot[i] - lcum[i]) for i in nch]
    bkh = [jnp.concatenate([beta[i] * erem[i], kb[i] * erem[i]], axis=0) for i in nch]
    st_new = [pcol[i] * sts[i] + _keep(same_head, _bdot(bkh[i], jnp.concatenate([ub[i], vbb[i]], axis=0), TN))
              for i in nch]
    return y, st_new


def _rwkv_scan_kernel(r_ref, ldf_ref, ldb_ref, kb_ref, vb_ref, kk_ref, a_ref, bonus_ref, gate_ref,
                      lnw_ref, lnb_ref, s0_ref, o_ref, sfin_ref, yb_ref, *, t, c, npair):
    n = t // c
    w = 2 * HD_B
    masks = [_rwkv_masks(c, rev) for rev in (False, True)]

    def body(ci, states):
        ins, tris, gmasks, dsts = [], [], [], []
        for dr, (ld_ref, dst) in enumerate(((ldf_ref, o_ref), (ldb_ref, yb_ref))):
            cj = ci if dr == 0 else n - 1 - ci
            rows = pl.ds(pl.multiple_of(cj * c, c), c)
            for p in range(npair):
                cols = slice(p * w, (p + 1) * w)
                ins.append(tuple(ref[rows, cols] for ref in (r_ref, ld_ref, kb_ref, vb_ref, kk_ref, a_ref)))
                tris.append(masks[dr][0])
                gmasks.append(masks[dr][1])
                dsts.append((dst, rows, cols))
        ys, new_states = _rwkv_pair_chunks(ins, list(states), tris, gmasks, *masks[0][2:])
        for (dst, rows, cols), y in zip(dsts, ys):
            dst[rows, cols] = y
        return tuple(new_states)

    zero = jnp.zeros((HD_B, HD_B), F32)
    init = tuple(jnp.concatenate([jnp.concatenate([s0_ref[0, dr, 2 * p], zero], axis=1),
                                  jnp.concatenate([zero, s0_ref[0, dr, 2 * p + 1]], axis=1)], axis=0)
                 for dr in range(2) for p in range(npair))
    fin = lax.fori_loop(0, n, body, init)
    segm = _seg_matrix(w, 6, 1.0 / HD_B)
    for p in range(npair):
        for dr in range(2):
            st = fin[dr * npair + p].T
            sfin_ref[0, dr, 2 * p] = st[:HD_B, :HD_B]
            sfin_ref[0, dr, 2 * p + 1] = st[HD_B:, HD_B:]
        cols = slice(p * w, (p + 1) * w)
        y = o_ref[:, cols] + yb_ref[:, cols]
        yc = y - _split_dot(y, segm)
        yn = yc * lax.rsqrt(_split_dot(yc * yc, segm) + RWKV_GN_EPS)
        o_ref[:, cols] = (yn * lnw_ref[:, cols] + lnb_ref[:, cols] + bonus_ref[:, cols]) * gate_ref[:, cols]


def _rwkv_scan(prep, ln_w, ln_b, s0t, latent):
    r, ldf, ldb, kb, vb, kk, a, gate, bonus = prep
    nseq, t = (DEC_BATCH, DEC_SEQ) if latent else (BATCH, SEQ)
    npair = H_B // 2
    w = npair * 2 * HD_B
    blk = pl.BlockSpec((t, w), lambda b, h: (b, h))
    vec = pl.BlockSpec((1, w), lambda b, h: (0, h))
    st_spec = pl.BlockSpec((1, 2, 2 * npair, HD_B, HD_B), lambda b, h: (b, 0, h, 0, 0))
    return pl.pallas_call(
        functools.partial(_rwkv_scan_kernel, t=t, c=RWKV_CHUNK, npair=npair),
        grid=(nseq, MIX // w),
        in_specs=[blk] * 9 + [vec, vec, st_spec],
        out_specs=[blk, st_spec],
        out_shape=[jax.ShapeDtypeStruct((nseq * t, MIX), F32),
                   jax.ShapeDtypeStruct((nseq, 2, H_B, HD_B, HD_B), F32)],
        scratch_shapes=[pltpu.VMEM((t, w), F32)],
        compiler_params=_params("parallel", "parallel"),
        name="rwkv_scan_lat" if latent else "rwkv_scan_ctx",
    )(r, ldf, ldb, kb, vb, kk, a, bonus, gate, ln_w.reshape(1, MIX), ln_b.reshape(1, MIX), s0t)


S5_KB = 2
S5_KW = MIX // S5_KB
S5_NW = S5_STATE // S5_KB


def _gelu_tanh(x):
    return 0.5 * x * (1.0 + jnp.tanh(math.sqrt(2.0 / math.pi) * (x + 0.044715 * (x * x * x))))


def _s5_kernel(u_ref, bbr_ref, bbi_ref, ccr_ref, cci_ref, lam_ref, h0r_ref, h0i_ref, d_ref, wglu_ref,
               o_ref, hfr_ref, hfi_ref, bur, bui, yacc, *, nseq, rows, cr):
    nch = rows // cr
    ngrp = cr // 8
    spg = 8 // nseq

    for dr in range(2):
        ar = lam_ref[dr, 0]
        ai = lam_ref[dr, 1]

        def chunk_body(ci, carry, dr=dr, ar=ar, ai=ai):
            cj = ci if dr == 0 else nch - 1 - ci
            r0 = pl.multiple_of(cj * cr, cr)
            ub = u_ref[0, pl.ds(r0, cr), :].astype(BF16)
            for kb in range(S5_KB):
                uk = ub[:, kb * S5_KW:(kb + 1) * S5_KW]
                bur[:, kb * S5_NW:(kb + 1) * S5_NW] = jnp.dot(uk, bbr_ref[dr, kb], preferred_element_type=F32)
                bui[:, kb * S5_NW:(kb + 1) * S5_NW] = jnp.dot(uk, bbi_ref[dr, kb], preferred_element_type=F32)

            def group_body(gi, hc):
                hr, hi = hc
                gj = gi if dr == 0 else ngrp - 1 - gi
                o8 = pl.multiple_of(gj * 8, 8)
                xr = bur[pl.ds(o8, 8), :]
                xi = bui[pl.ds(o8, 8), :]
                outr = [None] * spg
                outi = [None] * spg
                for s in (range(spg) if dr == 0 else range(spg - 1, -1, -1)):
                    nr = ar * hr - ai * hi + xr[s * nseq:(s + 1) * nseq]
                    ni = ar * hi + ai * hr + xi[s * nseq:(s + 1) * nseq]
                    hr, hi = nr, ni
                    outr[s] = hr
                    outi[s] = hi
                bur[pl.ds(o8, 8), :] = outr[0] if spg == 1 else jnp.concatenate(outr, axis=0)
                bui[pl.ds(o8, 8), :] = outi[0] if spg == 1 else jnp.concatenate(outi, axis=0)
                return hr, hi

            carry = lax.fori_loop(0, ngrp, group_body, carry)
            for kb in range(S5_KB):
                yk = (jnp.dot(bur[:, kb * S5_NW:(kb + 1) * S5_NW].astype(BF16), ccr_ref[dr, kb],
                              preferred_element_type=F32)
                      - jnp.dot(bui[:, kb * S5_NW:(kb + 1) * S5_NW].astype(BF16), cci_ref[dr, kb],
                                preferred_element_type=F32))
                cols = slice(kb * S5_KW, (kb + 1) * S5_KW)
                if dr == 0:
                    yacc[pl.ds(r0, cr), cols] = yk
                else:
                    yacc[pl.ds(r0, cr), cols] = yacc[pl.ds(r0, cr), cols] + yk
            return carry

        hr, hi = lax.fori_loop(0, nch, chunk_body, (h0r_ref[0, dr], h0i_ref[0, dr]))
        hfr_ref[0, dr] = hr
        hfi_ref[0, dr] = hi

    def out_body(ci, _):
        r0 = pl.multiple_of(ci * cr, cr)
        u = u_ref[0, pl.ds(r0, cr), :]
        z = _gelu_tanh(yacc[pl.ds(r0, cr), :] + d_ref[...] * u)
        o_ref[0, pl.ds(r0, cr), :] = z * _sigmoid(jnp.dot(z.astype(BF16), wglu_ref[...],
                                                          preferred_element_type=F32))
        return 0

    lax.fori_loop(0, nch, out_body, 0)


def _s5(u_tb, mats, h0r, h0i, d_skip, w_glu_bf16, nseq):
    bbr, bbi, ccr, cci, lam = mats
    groups, rows, _ = u_tb.shape
    cr = 256
    full = lambda shape: pl.BlockSpec(shape, lambda g: (0,) * len(shape))
    h_spec = pl.BlockSpec((1, 2, nseq, S5_STATE), lambda g: (g, 0, 0, 0))
    io_spec = pl.BlockSpec((1, rows, MIX), lambda g: (g, 0, 0))
    return pl.pallas_call(
        functools.partial(_s5_kernel, nseq=nseq, rows=rows, cr=cr),
        grid=(groups,),
        in_specs=[io_spec, full(bbr.shape), full(bbi.shape), full(ccr.shape), full(cci.shape),
                  full(lam.shape), h_spec, h_spec, full((1, MIX)), full((MIX, MIX))],
        out_specs=[io_spec, h_spec, h_spec],
        out_shape=[jax.ShapeDtypeStruct((groups, rows, MIX), F32),
                   jax.ShapeDtypeStruct((groups, 2, nseq, S5_STATE), F32),
                   jax.ShapeDtypeStruct((groups, 2, nseq, S5_STATE), F32)],
        scratch_shapes=[pltpu.VMEM((cr, S5_STATE), F32), pltpu.VMEM((cr, S5_STATE), F32),
                        pltpu.VMEM((rows, MIX), F32)],
        compiler_params=_params("parallel"),
        name="s5_lat" if nseq == DEC_BATCH else "s5_ctx",
    )(u_tb, bbr, bbi, ccr, cci, lam, h0r, h0i, d_skip.reshape(1, MIX), w_glu_bf16)


def _s5_matrices(lam_re, lam_im, log_dt, b_re, b_im, c_re, c_im):
    dt = jnp.exp(log_dt)[..., None]
    mag = jnp.exp(lam_re * dt)
    ab_re = mag * jnp.cos(lam_im * dt)
    ab_im = mag * jnp.sin(lam_im * dt)
    den = lam_re * lam_re + lam_im * lam_im
    nr = ab_re - 1.0
    f_re = (nr * lam_re + ab_im * lam_im) / den
    f_im = (ab_im * lam_re - nr * lam_im) / den
    bb_re = f_re[..., None] * b_re - f_im[..., None] * b_im
    bb_im = f_re[..., None] * b_im + f_im[..., None] * b_re
    gl = G_C // S5_KB
    eye = jnp.eye(gl, dtype=F32)

    def in_proj(bb):
        x = bb.reshape(2, S5_KB, gl, P_C, S5_GROUP)
        return jnp.einsum('dkgps,gh->dkgshp', x, eye).reshape(2, S5_KB, S5_KW, S5_NW).astype(BF16)

    def out_proj(cc):
        x = cc.reshape(2, S5_KB, gl, S5_GROUP, P_C)
        return jnp.einsum('dkgsp,gh->dkgphs', x, eye).reshape(2, S5_KB, S5_NW, S5_KW).astype(BF16)

    lam = jnp.stack([ab_re.reshape(2, 1, S5_STATE), ab_im.reshape(2, 1, S5_STATE)], axis=1)
    return in_proj(bb_re), in_proj(bb_im), out_proj(c_re), out_proj(c_im), lam


def _ret_kernel(*refs, t, latent):
    if latent:
        (q_ref, k_ref, v_ref, g_ref, dl_ref, lnw_ref, lnb_ref, r0_ref, cos_ref, sin_ref,
         o_ref, rfin_ref, ob_ref) = refs
    else:
        q_ref, k_ref, v_ref, g_ref, dl_ref, lnw_ref, lnb_ref, r0_ref, o_ref, rfin_ref, ob_ref = refs
    n = t // CHUNK
    w = HD_D
    jf = lax.broadcasted_iota(jnp.int32, (CHUNK, CHUNK), 0).astype(F32)
    kf = lax.broadcasted_iota(jnp.int32, (CHUNK, CHUNK), 1).astype(F32)
    diff = jf - kf
    tabs = []
    for dr in range(2):
        for h in range(H_D):
            l = -_softplus(-dl_ref[h, dr:dr + 1, :])
            if dr == 0:
                dmat = jnp.where(diff >= 0, jnp.exp(l * jnp.maximum(diff, 0.0)), 0.0)
                xi = jnp.exp(l * (jf + 1.0))
                zeta = jnp.exp(l * (CHUNK - 1.0 - jf))
            else:
                dmat = jnp.where(diff < 0, jnp.exp(l * jnp.maximum(-diff, 0.0)), 0.0)
                xi = jnp.exp(l * (CHUNK - jf))
                zeta = jnp.exp(l * jf)
            tabs.append((dmat, xi, zeta, jnp.exp(l * CHUNK)))
    chains = [(dr, h) for dr in range(2) for h in range(H_D)]

    def body(ci, states):
        q, k, v, where = [], [], [], []
        for dr, h in chains:
            cj = ci if dr == 0 else n - 1 - ci
            rows = pl.ds(pl.multiple_of(cj * CHUNK, CHUNK), CHUNK)
            cols = slice(h * w, (h + 1) * w)
            qi = q_ref[rows, cols]
            ki = k_ref[rows, cols] * (HD_D ** -0.5)
            if latent:
                qi = _rope_pairs(qi, cos_ref[rows, :], sin_ref[rows, :], HD_D // 2)
                ki = _rope_pairs(ki, cos_ref[rows, :], sin_ref[rows, :], HD_D // 2)
            q.append(qi.astype(BF16))
            k.append(ki)
            v.append(v_ref[rows, cols].astype(BF16))
            where.append((o_ref if dr == 0 else ob_ref, rows, cols))
        nch = range(len(chains))
        inner = [(_bdot(q[i], k[i], NT) * tabs[i][0]).astype(BF16) for i in nch]
        cross = [_bdot(q[i], states[i]) * tabs[i][1] for i in nch]
        kz = [(k[i] * tabs[i][2]).astype(BF16) for i in nch]
        out = [jnp.dot(inner[i], v[i], preferred_element_type=F32) + cross[i] for i in nch]
        new_states = [states[i] * tabs[i][3] + lax.dot_general(kz[i], v[i], TN, preferred_element_type=F32)
                      for i in nch]
        for (dst, rows, cols), y in zip(where, out):
            dst[rows, cols] = y
        return tuple(new_states)

    fin = lax.fori_loop(0, n, body, tuple(r0_ref[0, dr, h] for dr, h in chains))
    for i, (dr, h) in enumerate(chains):
        rfin_ref[0, dr, h] = fin[i]
    for h in range(H_D):
        cols = slice(h * w, (h + 1) * w)
        y = o_ref[:, cols] + ob_ref[:, cols]
        yc = y - jnp.mean(y, axis=-1, keepdims=True)
        yn = yc * lax.rsqrt(jnp.mean(yc * yc, axis=-1, keepdims=True) + RET_GN_EPS)
        o_ref[:, cols] = (yn * lnw_ref[:, cols] + lnb_ref[:, cols]) * _silu(g_ref[:, cols])


def _retention(p, decay_logit, ln_w, ln_b, r0, latent, rope=None):
    if latent:
        nseq, t, rb0 = DEC_BATCH, DEC_SEQ, M_CTX // DEC_SEQ
    else:
        nseq, t, rb0 = BATCH, SEQ, 0
    w = HD_D
    dl = jnp.broadcast_to(decay_logit.T[:, :, None], (H_D, 2, w))
    full = lambda shape: pl.BlockSpec(shape, lambda b: (0,) * len(shape))
    st_spec = pl.BlockSpec((1, 2, H_D, w, w), lambda b: (b, 0, 0, 0, 0))
    in_specs = [pl.BlockSpec((t, MIX), lambda b, j=j: (rb0 + b, j)) for j in (1, 2, 3, 4)]
    in_specs += [full((H_D, 2, w)), full((1, MIX)), full((1, MIX)), st_spec]
    args = [p, p, p, p, dl, ln_w.reshape(1, MIX), ln_b.reshape(1, MIX), r0]
    if latent:
        in_specs += [full((t, w))] * 2
        args += list(rope)
    return pl.pallas_call(
        functools.partial(_ret_kernel, t=t, latent=latent),
        grid=(nseq,),
        in_specs=in_specs,
        out_specs=[pl.BlockSpec((t, MIX), lambda b: (b, 0)), st_spec],
        out_shape=[jax.ShapeDtypeStruct((nseq * t, MIX), F32),
                   jax.ShapeDtypeStruct((nseq, 2, H_D, w, w), F32)],
        scratch_shapes=[pltpu.VMEM((t, MIX), F32)],
        compiler_params=_params("parallel"),
        name="retention_lat" if latent else "retention_ctx",
    )(*args)


def _rope_tables(n_tok, dim, reps):
    rows = n_tok // GRID_W
    n_freq = dim // 4
    inv = 1.0 / (ROPE_THETA ** (jnp.arange(n_freq, dtype=F32) / n_freq))
    row = jnp.repeat(jnp.arange(rows, dtype=F32), GRID_W)
    col = jnp.tile(jnp.arange(GRID_W, dtype=F32), rows)
    ang = jnp.concatenate([row[:, None] * inv, col[:, None] * inv], axis=-1)
    cos, sin = jnp.cos(ang), jnp.sin(ang)
    return jnp.tile(jnp.concatenate([cos, cos], axis=1), (1, reps)), \
        jnp.tile(jnp.concatenate([-sin, sin], axis=1), (1, reps))


def _time_major(x, nseq, group):
    t = x.shape[0] // nseq
    x = x.reshape(nseq // group, group, t, x.shape[1])
    return jnp.swapaxes(x, 1, 2).reshape(nseq // group, t * group, x.shape[-1])


def _seq_major(x, nseq, group):
    t = x.shape[1] // group
    x = x.reshape(nseq // group, t, group, x.shape[-1])
    return jnp.swapaxes(x, 1, 2).reshape(nseq * t, x.shape[-1])


def kernel(x_prompt, x_sample, cache_k_ab, cache_v_ab, state_rwkv, state_s5_re, state_s5_im, state_ret, c, c_ctx, norm1_g, norm2_g, w_mod, b_mod, w_ff_gate, w_ff_up, w_ff_down, w_in_ab, w_out_ab, qk_gain_a, lambda_qk, subln_g, rwkv_mu, rwkv_k_k, rwkv_k_a, rwkv_r_k, rwkv_w0, rwkv_w_up, rwkv_a0, rwkv_a_up, rwkv_g_up, rwkv_ln_w, rwkv_ln_b, w_in_cd, w_out_cd, s5_lam_re, s5_lam_im, s5_log_dt, s5_b_re, s5_b_im, s5_c_re, s5_c_im, s5_d, s5_w_glu, ret_decay_logit, ret_ln_w, ret_ln_b):
    d = D_MODEL
    x = jnp.concatenate([x_prompt.reshape(M_CTX, d), x_sample.reshape(M_LAT, d)], axis=0)
    cvec = jnp.zeros((MOD_ROWS, d), F32).at[0].set(c_ctx).at[1:1 + DEC_BATCH].set(c)
    mods = _modulation(cvec, w_mod, b_mod)

    lam_init = 0.8 - 0.6 * math.exp(-0.3 * 0)
    p = _norm_linear(x, norm1_g[0], mods[0], w_in_ab[0].astype(BF16))
    gain2 = jnp.tile(qk_gain_a[0], (1, 2))
    rope_a = _rope_tables(DEC_SEQ, HD_A, 2)
    ck = cache_k_ab[:, 0].reshape(DEC_BATCH * PAST_LEN, MIX)
    cv = cache_v_ab[:, 0].reshape(DEC_BATCH * PAST_LEN, MIX)
    oa_ctx, k_ctx = _attention(p, gain2, lambda_qk[0], subln_g[0], lam_init, latent=False)
    oa_lat = _attention(p, gain2, lambda_qk[0], subln_g[0], lam_init, latent=True,
                        cache_k=ck, cache_v=cv, rope=rope_a)
    rw_prm = (rwkv_mu[0], rwkv_k_k[0], rwkv_k_a[0], rwkv_r_k[0], rwkv_w0[0], rwkv_w_up[0],
              rwkv_a0[0], rwkv_a_up[0], rwkv_g_up[0])
    s0_ctx = jnp.zeros((BATCH, 2, H_B, HD_B, HD_B), F32)
    s0_lat = jnp.swapaxes(state_rwkv[:, 0], -1, -2)
    ob_ctx, sfin_ctx = _rwkv_scan(_rwkv_prep(p, rw_prm, latent=False), rwkv_ln_w[0], rwkv_ln_b[0],
                                  s0_ctx, latent=False)
    ob_lat, _ = _rwkv_scan(_rwkv_prep(p, rw_prm, latent=True), rwkv_ln_w[0], rwkv_ln_b[0],
                           s0_lat, latent=True)
    x = _proj_residual(x, jnp.concatenate([oa_ctx, oa_lat], axis=0),
                       jnp.concatenate([ob_ctx, ob_lat], axis=0), w_out_ab[0].astype(BF16), mods[0])
    x = _ffn(x, norm2_g[0], mods[0], w_ff_gate[0].astype(BF16), w_ff_up[0].astype(BF16),
             w_ff_down[0].astype(BF16))

    p2 = _norm_linear(x, norm1_g[1], mods[1], w_in_cd[0].astype(BF16))
    mats = _s5_matrices(s5_lam_re[0], s5_lam_im[0], s5_log_dt[0], s5_b_re[0], s5_b_im[0],
                        s5_c_re[0], s5_c_im[0])
    wglu = s5_w_glu[0].astype(BF16)
    grp = 8
    u_ctx = _time_major(p2[:M_CTX, :MIX], BATCH, grp)
    u_lat = _time_major(p2[M_CTX:, :MIX], DEC_BATCH, DEC_BATCH)
    z_ctx = jnp.zeros((BATCH // grp, 2, grp, S5_STATE), F32)
    oc_ctx, hfr, hfi = _s5(u_ctx, mats, z_ctx, z_ctx, s5_d[0], wglu, grp)
    h0r = jnp.swapaxes(state_s5_re[:, 0].reshape(DEC_BATCH, 2, S5_STATE), 0, 1)[None]
    h0i = jnp.swapaxes(state_s5_im[:, 0].reshape(DEC_BATCH, 2, S5_STATE), 0, 1)[None]
    oc_lat, _, _ = _s5(u_lat, mats, h0r, h0i, s5_d[0], wglu, DEC_BATCH)
    oc = jnp.concatenate([_seq_major(oc_ctx, BATCH, grp), _seq_major(oc_lat, DEC_BATCH, DEC_BATCH)], axis=0)
    rope_d = _rope_tables(DEC_SEQ, HD_D, 1)
    r0_ctx = jnp.zeros((BATCH, 2, H_D, HD_D, HD_D), F32)
    od_ctx, rfin = _retention(p2, ret_decay_logit[0], ret_ln_w[0], ret_ln_b[0], r0_ctx, latent=False)
    od_lat, _ = _retention(p2, ret_decay_logit[0], ret_ln_w[0], ret_ln_b[0], state_ret[:, 0],
                           latent=True, rope=rope_d)
    x = _proj_residual(x, oc, jnp.concatenate([od_ctx, od_lat], axis=0), w_out_cd[0].astype(BF16), mods[1])
    x = _ffn(x, norm2_g[1], mods[1], w_ff_gate[1].astype(BF16), w_ff_up[1].astype(BF16),
             w_ff_down[1].astype(BF16))

    y_prompt = x[:M_CTX].reshape(BATCH, SEQ, d)
    y_sample = x[M_CTX:].reshape(DEC_BATCH, DEC_SEQ, d)
    new_k = k_ctx.reshape(BATCH, 1, SEQ, H_A, 2, HD_A)
    new_v = p[:M_CTX, 2 * MIX:3 * MIX].reshape(BATCH, 1, SEQ, H_A, VD_A)
    new_rwkv = sfin_ctx[:, None]
    s5_shape = (BATCH, 2, G_C, P_C)
    new_s5_re = jnp.swapaxes(hfr, 1, 2).reshape(s5_shape)[:, None]
    new_s5_im = jnp.swapaxes(hfi, 1, 2).reshape(s5_shape)[:, None]
    new_ret = rfin[:, None]
    return (y_prompt, y_sample, new_k, new_v, new_rwkv, new_s5_re, new_s5_im, new_ret)
```

```python
import functools
import math

import numpy as np
import jax
import jax.numpy as jnp
from jax import lax
from jax.experimental import pallas as pl
from jax.experimental.pallas import tpu as pltpu

F32 = jnp.float32
BF16 = jnp.bfloat16
HIGHEST = lax.Precision.HIGHEST

D_MODEL = 1024
BATCH = 32
SEQ = 256
DEC_BATCH = 2
DEC_SEQ = 1024
PAST_LEN = 256
GRID_W = 64
H_A = 4
HD_A = 64
VD_A = 128
H_B = 8
HD_B = 64
MIX = 512
LORA_W = 64
LORA_A = 64
LORA_G = 128
S5_GROUP = 16
G_C = 32
P_C = 64
S5_STATE = G_C * P_C
H_D = 4
HD_D = 128
CHUNK = 128
D_FF = 2816
IN_AB = 3328
IN_CD = 2560
ROPE_THETA = 10000.0
NORM_EPS = 1e-6
RWKV_GN_EPS = 64e-5
RET_GN_EPS = 1e-5

M_CTX = BATCH * SEQ
M_LAT = DEC_BATCH * DEC_SEQ
M_ALL = M_CTX + M_LAT
MOD_ROWS = 8
RWKV_CHUNK = 64
VMEM_LIMIT = 56 * 1024 * 1024

NN = (((1,), (0,)), ((), ()))
NT = (((1,), (1,)), ((), ()))
TN = (((0,), (0,)), ((), ()))


def _params(*sem):
    return pltpu.CompilerParams(dimension_semantics=sem, vmem_limit_bytes=VMEM_LIMIT)


def _bdot(a, b, dims=NN):
    return lax.dot_general(a.astype(BF16), b.astype(BF16), dims, preferred_element_type=F32)


def _hdot(a, b, dims=NN):
    return lax.dot_general(a, b, dims, precision=HIGHEST, preferred_element_type=F32)


def _split_dot(x, m):
    hi = x.astype(BF16)
    lo = (x - hi.astype(F32)).astype(BF16)
    return (jnp.dot(hi, m, preferred_element_type=F32) + jnp.dot(lo, m, preferred_element_type=F32))


def _seg_matrix(n, shift, val):
    r = lax.broadcasted_iota(jnp.int32, (n, n), 0) >> shift
    c = lax.broadcasted_iota(jnp.int32, (n, n), 1) >> shift
    return jnp.where(r == c, val, 0.0).astype(BF16)


def _sigmoid(x):
    return jax.nn.sigmoid(x)


def _silu(x):
    return x * jax.nn.sigmoid(x)


def _softplus(x):
    return jnp.maximum(x, 0.0) + jnp.log(1.0 + jnp.exp(-jnp.abs(x)))


def _mod_row(tile, tm):
    r0 = tile * tm
    return jnp.where(r0 < M_CTX, 0, 1 + (r0 - M_CTX) // DEC_SEQ)


def _norm_mod(x, g, sc_ref, sh_ref, row):
    y = x * lax.rsqrt(jnp.mean(x * x, axis=-1, keepdims=True) + NORM_EPS) * g
    return y * (1.0 + sc_ref[pl.ds(row, 1), :]) + sh_ref[pl.ds(row, 1), :]


def _mod_kernel(c_ref, w_ref, b_ref, o_ref):
    o_ref[0] = _hdot(_silu(c_ref[...]), w_ref[0]) + b_ref[0]


def _modulation(cvec, w_mod, b_mod):
    depth, d, n6 = w_mod.shape
    tn = 1536
    return pl.pallas_call(
        _mod_kernel,
        grid=(depth, n6 // tn),
        in_specs=[pl.BlockSpec((MOD_ROWS, d), lambda l, j: (0, 0)),
                  pl.BlockSpec((1, d, tn), lambda l, j: (l, 0, j)),
                  pl.BlockSpec((1, 1, tn), lambda l, j: (l, 0, j))],
        out_specs=pl.BlockSpec((1, MOD_ROWS, tn), lambda l, j: (l, 0, j)),
        out_shape=jax.ShapeDtypeStruct((depth, MOD_ROWS, n6), F32),
        compiler_params=_params("parallel", "parallel"),
        name="modulation",
    )(cvec, w_mod, b_mod.reshape(depth, 1, n6))


def _tile_specs(tm, d):
    nc = M_CTX // tm
    return [pl.BlockSpec((tm, d), lambda i: (jnp.minimum(i, nc - 1), 0)),
            pl.BlockSpec((tm, d), lambda i: (jnp.maximum(i - nc, 0), 0))]


def _pick_rows(refs, tm):
    if len(refs) == 1:
        return refs[0][...]
    return jnp.where(pl.program_id(0) < M_CTX // tm, refs[0][...], refs[1][...])


def _norm_linear_kernel(*refs, tm):
    *x_refs, g_ref, sc_ref, sh_ref, w_ref, o_ref = refs
    row = _mod_row(pl.program_id(0), tm)
    h = _norm_mod(_pick_rows(x_refs, tm), g_ref[...], sc_ref, sh_ref, row)
    o_ref[...] = jnp.dot(h.astype(BF16), w_ref[...], preferred_element_type=F32)


def _norm_linear(xs, g, mods, w_bf16):
    tm = 256
    d = D_MODEL
    n = w_bf16.shape[1]
    x_specs = _tile_specs(tm, d) if len(xs) == 2 else [pl.BlockSpec((tm, d), lambda i: (i, 0))]
    return pl.pallas_call(
        functools.partial(_norm_linear_kernel, tm=tm),
        grid=(M_ALL // tm,),
        in_specs=x_specs + [pl.BlockSpec((1, d), lambda i: (0, 0)),
                            pl.BlockSpec((MOD_ROWS, d), lambda i: (0, 1)),
                            pl.BlockSpec((MOD_ROWS, d), lambda i: (0, 0)),
                            pl.BlockSpec((d, n), lambda i: (0, 0))],
        out_specs=pl.BlockSpec((tm, n), lambda i: (i, 0)),
        out_shape=jax.ShapeDtypeStruct((M_ALL, n), F32),
        compiler_params=_params("arbitrary"),
        name="norm_linear",
    )(*xs, g.reshape(1, d), mods, mods, w_bf16)


def _mix_ffn_kernel(*refs, tm, ck, n_in, n_out):
    x_refs = refs[:n_in]
    (a_ref, b_ref, wa_ref, wb_ref, g1_ref, ng_ref, sc_ref, sh_ref, g2_ref,
     wg_ref, wu_ref, wd_ref) = refs[n_in:n_in + 12]
    o_refs = refs[n_in + 12:]
    row = _mod_row(pl.program_id(0), tm)
    mix = (jnp.dot(a_ref[...].astype(BF16), wa_ref[...], preferred_element_type=F32)
           + jnp.dot(b_ref[...].astype(BF16), wb_ref[...], preferred_element_type=F32))
    x = _pick_rows(x_refs, tm) + g1_ref[pl.ds(row, 1), :] * mix
    h = _norm_mod(x, ng_ref[...], sc_ref, sh_ref, row).astype(BF16)
    acc = jnp.zeros((tm, D_MODEL), F32)
    for c in range(D_FF // ck):
        gg = jnp.dot(h, wg_ref[:, c * ck:(c + 1) * ck], preferred_element_type=F32)
        uu = jnp.dot(h, wu_ref[:, c * ck:(c + 1) * ck], preferred_element_type=F32)
        act = (_silu(gg) * uu).astype(BF16)
        acc = acc + jnp.dot(act, wd_ref[c * ck:(c + 1) * ck, :], preferred_element_type=F32)
    y = x + g2_ref[pl.ds(row, 1), :] * acc
    if n_out == 1:
        o_refs[0][...] = y
    else:
        is_ctx = pl.program_id(0) < M_CTX // tm

        @pl.when(is_ctx)
        def _():
            o_refs[0][...] = y

        @pl.when(jnp.logical_not(is_ctx))
        def _():
            o_refs[1][...] = y


def _mix_ffn(xs, oa, ob, w_out_bf16, norm_g, mods, wg, wu, wd, split_out):
    tm, ck = 512, 256
    d = D_MODEL
    row_spec = lambda n: pl.BlockSpec((tm, n), lambda i: (i, 0))
    const = lambda shape, idx: pl.BlockSpec(shape, lambda i: idx, pipeline_mode=pl.Buffered(1))
    mod = lambda j: pl.BlockSpec((MOD_ROWS, d), lambda i: (0, j))
    x_specs = _tile_specs(tm, d) if len(xs) == 2 else [row_spec(d)]
    if split_out:
        out_specs = _tile_specs(tm, d)
        out_shape = [jax.ShapeDtypeStruct((M_CTX, d), F32), jax.ShapeDtypeStruct((M_LAT, d), F32)]
    else:
        out_specs, out_shape = row_spec(d), jax.ShapeDtypeStruct((M_ALL, d), F32)
    return pl.pallas_call(
        functools.partial(_mix_ffn_kernel, tm=tm, ck=ck, n_in=len(xs), n_out=2 if split_out else 1),
        grid=(M_ALL // tm,),
        in_specs=x_specs + [row_spec(MIX), row_spec(MIX),
                            const((MIX, d), (0, 0)), const((MIX, d), (1, 0)),
                            mod(2),
                            pl.BlockSpec((1, d), lambda i: (0, 0)),
                            mod(4), mod(3), mod(5),
                            const((d, D_FF), (0, 0)), const((d, D_FF), (0, 0)), const((D_FF, d), (0, 0))],
        out_specs=out_specs, out_shape=out_shape,
        compiler_params=_params("arbitrary"),
        name="mix_ffn",
    )(*xs, oa, ob, w_out_bf16, w_out_bf16, mods, norm_g.reshape(1, d), mods, mods, mods, wg, wu, wd)


def _qk_norm(x, gain, segm):
    ms = _split_dot(x * x, segm)
    return x * lax.rsqrt(ms + NORM_EPS) * gain


def _rope_pairs(x, cosf, sinf, half):
    lane = lax.broadcasted_iota(jnp.int32, x.shape, 1)
    first = (lane & (2 * half - 1)) < half
    n = x.shape[1]
    partner = jnp.where(first, pltpu.roll(x, n - half, axis=1), pltpu.roll(x, half, axis=1))
    return x * cosf + partner * sinf


def _attn_kernel(*refs, latent, lam_init):
    if latent:
        (q_ref, k_ref, v_ref, ck_ref, cv_ref, cosq_ref, sinq_ref, cosk_ref, sink_ref,
         gain_ref, lam_ref, sub_ref, _, o_ref, kall, vall) = refs
    else:
        q_ref, k_ref, v_ref, gain_ref, lam_ref, sub_ref, o_ref, kn_ref, kall, vall = refs
    w = 2 * HD_A
    segm = _seg_matrix(w, 6, 1.0 / HD_A)
    gains = gain_ref[...]
    cols = [slice(h * w, (h + 1) * w) for h in range(H_A)]

    @pl.when(pl.program_id(1) == 0)
    def _():
        k = [_qk_norm(k_ref[:, c], gains[1:2], segm) for c in cols]
        if latent:
            k = [_rope_pairs(x, cosk_ref[...], sink_ref[...], HD_A // 2) for x in k]
            for c, x in zip(cols, k):
                kall[0:PAST_LEN, c] = ck_ref[:, c].astype(BF16)
                kall[PAST_LEN:, c] = x.astype(BF16)
                vall[0:PAST_LEN, c] = cv_ref[:, c].astype(BF16)
                vall[PAST_LEN:, c] = v_ref[:, c].astype(BF16)
        else:
            for c, x in zip(cols, k):
                kn_ref[:, c] = x
                kall[:, c] = x.astype(BF16)
                vall[:, c] = v_ref[:, c].astype(BF16)

    q = [_qk_norm(q_ref[:, c], gains[0:1], segm) for c in cols]
    if latent:
        q = [_rope_pairs(x, cosq_ref[...], sinq_ref[...], HD_A // 2) for x in q]
    lv = lam_ref[...]
    lam = (jnp.exp(jnp.sum(lv[0:1] * lv[1:2], axis=1, keepdims=True))
           - jnp.exp(jnp.sum(lv[2:3] * lv[3:4], axis=1, keepdims=True)) + lam_init)
    scale = HD_A ** -0.5
    comp0 = lax.broadcasted_iota(jnp.int32, q[0].shape, 1) < HD_A
    qc = [jnp.where(comp0, *sel).astype(BF16) for x in q for sel in ((x, 0.0), (0.0, x))]
    s = [lax.dot_general(qc[i], kall[:, cols[i // 2]], NT, preferred_element_type=F32) * scale
         for i in range(2 * H_A)]
    e = [jnp.exp(x - jnp.max(x, axis=-1, keepdims=True)) for x in s]
    p = [x / jnp.sum(x, axis=-1, keepdims=True) for x in e]
    att = [(p[2 * h] - lam * p[2 * h + 1]).astype(BF16) for h in range(H_A)]
    o = [jnp.dot(att[h], vall[:, cols[h]], preferred_element_type=F32) for h in range(H_A)]
    o = [x * lax.rsqrt(jnp.mean(x * x, axis=-1, keepdims=True) + NORM_EPS) * sub_ref[...] for x in o]
    for c, x in zip(cols, o):
        o_ref[:, c] = x * (1.0 - lam_init)


def _attention(p, gain2, lambda_qk, subln_g, lam_init, latent, cache_k=None, cache_v=None, rope=None,
               dst=None):
    w = 2 * HD_A
    if latent:
        nseq, t, tq, rb0, s_len = DEC_BATCH, DEC_SEQ, 128, M_CTX // DEC_SEQ, PAST_LEN + DEC_SEQ
    else:
        nseq, t, tq, rb0, s_len = BATCH, SEQ, SEQ, 0, SEQ
    nq = t // tq
    qoff = rb0 * nq
    full = lambda shape: pl.BlockSpec(shape, lambda b, i: (0,) * len(shape))
    in_specs = [pl.BlockSpec((tq, MIX), lambda b, i: (qoff + b * nq + i, 0)),
                pl.BlockSpec((t, MIX), lambda b, i: (rb0 + b, 1)),
                pl.BlockSpec((t, MIX), lambda b, i: (rb0 + b, 2))]
    args = [p, p, p]
    if latent:
        cosf, sinf = rope
        in_specs += [pl.BlockSpec((PAST_LEN, MIX), lambda b, i: (b, 0)),
                     pl.BlockSpec((PAST_LEN, MIX), lambda b, i: (b, 0)),
                     pl.BlockSpec((tq, w), lambda b, i: (i, 0)),
                     pl.BlockSpec((tq, w), lambda b, i: (i, 0)),
                     full((t, w)), full((t, w))]
        args += [cache_k, cache_v, cosf, sinf, cosf, sinf]
    in_specs += [full((2, w)), full((4, HD_A)), full((1, w))]
    args += [gain2, lambda_qk, subln_g.reshape(1, w)]
    o_spec = pl.BlockSpec((tq, MIX), lambda b, i: (qoff + b * nq + i, 0))
    o_shape = jax.ShapeDtypeStruct((M_ALL, MIX), F32)
    aliases = {}
    if latent:
        out_specs, out_shape = o_spec, o_shape
        in_specs.append(pl.BlockSpec(memory_space=pl.ANY))
        args.append(dst)
        aliases = {len(args) - 1: 0}
    else:
        out_specs = [o_spec, pl.BlockSpec((t, MIX), lambda b, i: (b, 0))]
        out_shape = [o_shape, jax.ShapeDtypeStruct((nseq * t, MIX), F32)]
    return pl.pallas_call(
        functools.partial(_attn_kernel, latent=latent, lam_init=lam_init),
        grid=(nseq, nq),
        in_specs=in_specs, out_specs=out_specs, out_shape=out_shape,
        scratch_shapes=[pltpu.VMEM((s_len, MIX), BF16), pltpu.VMEM((s_len, MIX), BF16)],
        input_output_aliases=aliases,
        compiler_params=_params("parallel", "arbitrary"),
        name="diff_attention_lat" if latent else "diff_attention_ctx",
    )(*args)


def _centred_shift(x, mu):
    t = x.shape[0]
    row = lax.broadcasted_iota(jnp.int32, x.shape, 0)
    prev = jnp.where(row == 0, 0.0, pltpu.roll(x, 1, axis=0))
    nxt = jnp.where(row == t - 1, 0.0, pltpu.roll(x, t - 1, axis=0))
    return x + (0.5 * (prev + nxt) - x) * mu


def _seg_sum(x, segm):
    return jnp.concatenate([_split_dot(x[:, j * 128:(j + 1) * 128], segm) for j in range(x.shape[1] // 128)],
                           axis=1)


def _rwkv_prep_kernel(r_ref, k_ref, v_ref, l_ref, mur_ref, muk_ref, muv_ref, mul_ref,
                      kk_ref, ka_ref, rk_ref, w0_ref, wup_ref, a0_ref, aup_ref, gup_ref,
                      ro_ref, ldf_ref, ldb_ref, kbo_ref, vbo_ref, kko_ref, ao_ref, gate_ref, bonus_ref):
    seg1 = _seg_matrix(128, 6, 1.0)
    r = _centred_shift(r_ref[...], mur_ref[...])
    kb = _centred_shift(k_ref[...], muk_ref[...])
    vb = _centred_shift(v_ref[...], muv_ref[...])
    lo = _centred_shift(l_ref[...], mul_ref[...])
    xw = lo[:, 0:LORA_W]
    xa = lo[:, LORA_W:LORA_W + LORA_A]
    xg = lo[:, LORA_W + LORA_A:]
    kk = kb * kk_ref[...]
    kk = kk * lax.rsqrt(_seg_sum(kk * kk, seg1) + 1e-12)
    a = _sigmoid(a0_ref[...] + _bdot(xa, aup_ref[...]))
    kb2 = kb * (1.0 + (a - 1.0) * ka_ref[...])
    lw = jnp.tanh(xw)
    for dr, ld_ref in enumerate((ldf_ref, ldb_ref)):
        z = w0_ref[dr:dr + 1, :] + _bdot(lw, wup_ref[dr])
        logw = -_softplus(-z) - 0.5
        ld_ref[...] = -jnp.exp(logw)
    gate_ref[...] = _bdot(_sigmoid(xg), gup_ref[...])
    bonus_ref[...] = _seg_sum(r * kb2 * rk_ref[...], seg1) * vb
    ro_ref[...] = r
    kbo_ref[...] = kb2
    vbo_ref[...] = vb
    kko_ref[...] = kk
    ao_ref[...] = a


def _rwkv_prep(p, prm, latent):
    mu, k_k, k_a, r_k, w0, w_up, a0, a_up, g_up = prm
    if latent:
        nseq, t, rb0 = DEC_BATCH, DEC_SEQ, M_CTX // DEC_SEQ
    else:
        nseq, t, rb0 = BATCH, SEQ, 0
    lw = LORA_W + LORA_A + LORA_G
    c0 = (IN_AB - 3 * MIX - lw) // MIX
    cl = (IN_AB - lw) // lw
    row = lambda n: pl.BlockSpec((1, n), lambda b: (0, 0))
    in_specs = [pl.BlockSpec((t, MIX), lambda b: (rb0 + b, c0)),
                pl.BlockSpec((t, MIX), lambda b: (rb0 + b, c0 + 1)),
                pl.BlockSpec((t, MIX), lambda b: (rb0 + b, c0 + 2)),
                pl.BlockSpec((t, lw), lambda b: (rb0 + b, cl)),
                row(MIX), row(MIX), row(MIX), row(lw),
                row(MIX), row(MIX), row(MIX),
                pl.BlockSpec((2, MIX), lambda b: (0, 0)),
                pl.BlockSpec((2, LORA_W, MIX), lambda b: (0, 0, 0)),
                row(MIX),
                pl.BlockSpec((LORA_A, MIX), lambda b: (0, 0)),
                pl.BlockSpec((LORA_G, MIX), lambda b: (0, 0))]
    o_spec = pl.BlockSpec((t, MIX), lambda b: (b, 0))
    o_shape = jax.ShapeDtypeStruct((nseq * t, MIX), F32)
    return pl.pallas_call(
        _rwkv_prep_kernel,
        grid=(nseq,),
        in_specs=in_specs, out_specs=[o_spec] * 9, out_shape=[o_shape] * 9,
        compiler_params=_params("parallel"),
        name="rwkv_prep_lat" if latent else "rwkv_prep_ctx",
    )(p, p, p, p,
      mu[None, 0:MIX], mu[None, MIX:2 * MIX], mu[None, 2 * MIX:3 * MIX], mu[None, 3 * MIX:],
      k_k.reshape(1, MIX), k_a.reshape(1, MIX), r_k.reshape(1, MIX), w0, w_up,
      a0.reshape(1, MIX), a_up, g_up)


def _split2(x):
    hi = x.astype(BF16)
    return hi, (x - hi.astype(F32)).astype(BF16)


RWKV_INV_BLOCK = 16


def _rwkv_masks(c, rev):
    ti = lax.broadcasted_iota(jnp.int32, (c, c), 0)
    si = lax.broadcasted_iota(jnp.int32, (c, c), 1)
    tri = jnp.where((si >= ti) if rev else (si <= ti), 1.0, 0.0).astype(BF16)
    t4 = lax.broadcasted_iota(jnp.int32, (4 * c, 4 * c), 0)
    s4 = lax.broadcasted_iota(jnp.int32, (4 * c, 4 * c), 1)
    tm, sm = t4 & (c - 1), s4 & (c - 1)
    strict = (sm > tm) if rev else (sm < tm)
    incl = (sm >= tm) if rev else (sm <= tm)
    same_head = ((t4 // c) & 1) == ((s4 // c) & 1)
    top = t4 < 2 * c
    gmask = same_head & ((top & strict) | (~top & incl))
    t2 = lax.broadcasted_iota(jnp.int32, (2 * c, 2 * c), 0)
    s2 = lax.broadcasted_iota(jnp.int32, (2 * c, 2 * c), 1)
    same = lambda n: (t2 // n) == (s2 // n)
    levels = []
    n = RWKV_INV_BLOCK
    while n < c:
        levels.append(same(2 * n) & ~same(n))
        n *= 2
    f = lambda m: jnp.where(m, 1.0, 0.0)
    return tri, f(gmask), f(same(RWKV_INV_BLOCK)), tuple(f(m) for m in levels), f(same(c)), f(t2 == s2)


def _keep(mask01, x):
    return jnp.where(mask01 > 0.5, x, 0.0)


def _tri_inverse(a, diag_blk, levels, eye):
    n = a[0].shape[0]
    d = [_keep(diag_blk, x) for x in a]
    t = [eye + x for x in d]
    p = [_bdot(x, x) for x in d]
    for _ in range(int(math.log2(RWKV_INV_BLOCK)) - 2):
        res = [_bdot(jnp.concatenate([pi, ti], axis=0), pi) for pi, ti in zip(p, t)]
        p = [x[:n] for x in res]
        t = [ti + x[n:] for ti, x in zip(t, res)]
    t = [ti + _bdot(ti, pi) for ti, pi in zip(t, p)]
    for off in levels:
        half = [_bdot(ti, _keep(off, x)) for ti, x in zip(t, a)]
        t = [ti + _bdot(x, ti) for ti, x in zip(t, half)]
    return t


def _rwkv_pair_chunks(ins, sts, tris, gmasks, diag_blk, levels, same_head, eye):
    c, w = ins[0][0].shape
    hd = w // 2
    nch = range(len(ins))
    r, ld, kb, vb, kk, a = (list(z) for z in zip(*ins))
    split = [_split2(x) for x in ld]
    lcum = [jnp.dot(tris[i], split[i][0], preferred_element_type=F32)
            + jnp.dot(tris[i], split[i][1], preferred_element_type=F32) for i in nch]
    ones = jnp.ones((c, w), BF16)
    pcol = [jnp.exp(lax.dot_general(hi, ones, TN, preferred_element_type=F32)
                    + lax.dot_general(lo, ones, TN, preferred_element_type=F32)) for hi, lo in split]
    ltot = [jnp.sum(x, axis=0, keepdims=True) for x in ld]
    beta = [kk[i] * a[i] for i in nch]
    eneg = [jnp.exp(-x) for x in lcum]
    abar = [-kk[i] * jnp.exp(lcum[i] - ld[i]) for i in nch]
    rbar = [r[i] * jnp.exp(lcum[i]) for i in nch]
    bt = [(beta[i] * eneg[i]).astype(BF16) for i in nch]
    kt = [(kb[i] * eneg[i]).astype(BF16) for i in nch]
    vbb = [x.astype(BF16) for x in vb]
    head0 = lax.broadcasted_iota(jnp.int32, (c, w), 1) < hd
    pick = lambda res: jnp.where(head0, res[:c], res[c:])
    arst = [_bdot(jnp.concatenate([abar[i], rbar[i]], axis=0), sts[i]) for i in nch]
    lhs = [jnp.concatenate([jnp.where(head0, abar[i], 0.0), jnp.where(head0, 0.0, abar[i]),
                            jnp.where(head0, rbar[i], 0.0), jnp.where(head0, 0.0, rbar[i])], axis=0) for i in nch]
    g = [_keep(gmasks[i], _bdot(lhs[i], jnp.concatenate([bt[i], bt[i], kt[i], kt[i]], axis=0), NT))
         for i in nch]
    x = [arst[i][:c] + pick(_bdot(g[i][:2 * c, 2 * c:], jnp.concatenate([vbb[i], vbb[i]], axis=0))) for i in nch]
    tinv = _tri_inverse([gi[:2 * c, :2 * c] for gi in g], diag_blk, levels, eye)
    u = [pick(_bdot(tinv[i], jnp.concatenate([x[i], x[i]], axis=0))) for i in nch]
    ub = [z.astype(BF16) for z in u]
    y = [arst[i][c:] + pick(_bdot(g[i][2 * c:], jnp.concatenate([ub[i], ub[i], vbb[i], vbb[i]], axis=0)))
         for i in nch]
    erem = [jnp.exp(ltot[i] - lcum[i]) for i in nch]
    bkh = [jnp.concatenate([beta[i] * erem[i], kb[i] * erem[i]], axis=0) for i in nch]
    st_new = [pcol[i] * sts[i] + _keep(same_head, _bdot(bkh[i], jnp.concatenate([ub[i], vbb[i]], axis=0), TN))
              for i in nch]
    return y, st_new


def _rwkv_scan_kernel(*refs, t, c, npair, latent):
    r_ref, ldf_ref, ldb_ref, kb_ref, vb_ref, kk_ref, a_ref, bonus_ref, gate_ref, lnw_ref, lnb_ref = refs[:11]
    if latent:
        s0_ref, _, o_ref, yb_ref = refs[11:]
    else:
        o_ref, sfin_ref, yb_ref = refs[11:]
    n = t // c
    w = 2 * HD_B
    masks = [_rwkv_masks(c, rev) for rev in (False, True)]

    def body(ci, states):
        ins, tris, gmasks, dsts = [], [], [], []
        for dr, (ld_ref, dst) in enumerate(((ldf_ref, o_ref), (ldb_ref, yb_ref))):
            cj = ci if dr == 0 else n - 1 - ci
            rows = pl.ds(pl.multiple_of(cj * c, c), c)
            for p in range(npair):
                cols = slice(p * w, (p + 1) * w)
                ins.append(tuple(ref[rows, cols] for ref in (r_ref, ld_ref, kb_ref, vb_ref, kk_ref, a_ref)))
                tris.append(masks[dr][0])
                gmasks.append(masks[dr][1])
                dsts.append((dst, rows, cols))
        ys, new_states = _rwkv_pair_chunks(ins, list(states), tris, gmasks, *masks[0][2:])
        for (dst, rows, cols), y in zip(dsts, ys):
            dst[rows, cols] = y
        return tuple(new_states)

    if latent:
        zero = jnp.zeros((HD_B, HD_B), F32)
        init = tuple(jnp.concatenate([jnp.concatenate([s0_ref[0, dr, 2 * p], zero], axis=1),
                                      jnp.concatenate([zero, s0_ref[0, dr, 2 * p + 1]], axis=1)], axis=0)
                     for dr in range(2) for p in range(npair))
    else:
        init = tuple(jnp.zeros((w, w), F32) for _ in range(2 * npair))
    fin = lax.fori_loop(0, n, body, init)
    segm = _seg_matrix(w, 6, 1.0 / HD_B)
    for p in range(npair):
        if not latent:
            for dr in range(2):
                st = fin[dr * npair + p].T
                sfin_ref[0, dr, 2 * p] = st[:HD_B, :HD_B]
                sfin_ref[0, dr, 2 * p + 1] = st[HD_B:, HD_B:]
        cols = slice(p * w, (p + 1) * w)
        y = o_ref[:, cols] + yb_ref[:, cols]
        yc = y - _split_dot(y, segm)
        yn = yc * lax.rsqrt(_split_dot(yc * yc, segm) + RWKV_GN_EPS)
        o_ref[:, cols] = (yn * lnw_ref[:, cols] + lnb_ref[:, cols] + bonus_ref[:, cols]) * gate_ref[:, cols]


def _rwkv_scan(prep, ln_w, ln_b, latent, s0t=None, dst=None):
    r, ldf, ldb, kb, vb, kk, a, gate, bonus = prep
    nseq, t, rb0 = (DEC_BATCH, DEC_SEQ, M_CTX // DEC_SEQ) if latent else (BATCH, SEQ, 0)
    npair = H_B // 2
    w = npair * 2 * HD_B
    blk = pl.BlockSpec((t, w), lambda b, h: (b, h))
    vec = pl.BlockSpec((1, w), lambda b, h: (0, h))
    st_spec = pl.BlockSpec((1, 2, 2 * npair, HD_B, HD_B), lambda b, h: (b, 0, h, 0, 0))
    o_spec = pl.BlockSpec((t, w), lambda b, h: (rb0 + b, h))
    o_shape = jax.ShapeDtypeStruct((M_ALL, MIX), F32)
    args = [r, ldf, ldb, kb, vb, kk, a, bonus, gate, ln_w.reshape(1, MIX), ln_b.reshape(1, MIX)]
    in_specs = [blk] * 9 + [vec, vec]
    if latent:
        args += [s0t, dst]
        in_specs += [st_spec, pl.BlockSpec(memory_space=pl.ANY)]
        out_specs, out_shape, aliases = o_spec, o_shape, {len(args) - 1: 0}
    else:
        out_specs = [o_spec, st_spec]
        out_shape = [o_shape, jax.ShapeDtypeStruct((nseq, 2, H_B, HD_B, HD_B), F32)]
        aliases = {}
    return pl.pallas_call(
        functools.partial(_rwkv_scan_kernel, t=t, c=RWKV_CHUNK, npair=npair, latent=latent),
        grid=(nseq, MIX // w),
        in_specs=in_specs, out_specs=out_specs, out_shape=out_shape,
        scratch_shapes=[pltpu.VMEM((t, w), F32)],
        input_output_aliases=aliases,
        compiler_params=_params("parallel", "parallel"),
        name="rwkv_scan_lat" if latent else "rwkv_scan_ctx",
    )(*args)


S5_KB = 2
S5_KW = MIX // S5_KB
S5_NW = S5_STATE // S5_KB


def _gelu_tanh(x):
    return 0.5 * x * (1.0 + jnp.tanh(math.sqrt(2.0 / math.pi) * (x + 0.044715 * (x * x * x))))


def _s5_kernel(u_ref, bbr_ref, bbi_ref, ccr_ref, cci_ref, lam_ref, h0r_ref, h0i_ref, d_ref, wglu_ref,
               o_ref, hfr_ref, hfi_ref, bur, bui, yacc, *, nseq, rows, cr):
    nch = rows // cr
    ngrp = cr // 8
    spg = 8 // nseq

    for dr in range(2):
        ar = lam_ref[dr, 0]
        ai = lam_ref[dr, 1]

        def chunk_body(ci, carry, dr=dr, ar=ar, ai=ai):
            cj = ci if dr == 0 else nch - 1 - ci
            r0 = pl.multiple_of(cj * cr, cr)
            ub = u_ref[0, pl.ds(r0, cr), :].astype(BF16)
            for kb in range(S5_KB):
                uk = ub[:, kb * S5_KW:(kb + 1) * S5_KW]
                bur[:, kb * S5_NW:(kb + 1) * S5_NW] = jnp.dot(uk, bbr_ref[dr, kb], preferred_element_type=F32)
                bui[:, kb * S5_NW:(kb + 1) * S5_NW] = jnp.dot(uk, bbi_ref[dr, kb], preferred_element_type=F32)

            def group_body(gi, hc):
                hr, hi = hc
                gj = gi if dr == 0 else ngrp - 1 - gi
                o8 = pl.multiple_of(gj * 8, 8)
                xr = bur[pl.ds(o8, 8), :]
                xi = bui[pl.ds(o8, 8), :]
                outr = [None] * spg
                outi = [None] * spg
                for s in (range(spg) if dr == 0 else range(spg - 1, -1, -1)):
                    nr = ar * hr - ai * hi + xr[s * nseq:(s + 1) * nseq]
                    ni = ar * hi + ai * hr + xi[s * nseq:(s + 1) * nseq]
                    hr, hi = nr, ni
                    outr[s] = hr
                    outi[s] = hi
                bur[pl.ds(o8, 8), :] = outr[0] if spg == 1 else jnp.concatenate(outr, axis=0)
                bui[pl.ds(o8, 8), :] = outi[0] if spg == 1 else jnp.concatenate(outi, axis=0)
                return hr, hi

            carry = lax.fori_loop(0, ngrp, group_body, carry)
            for kb in range(S5_KB):
                yk = (jnp.dot(bur[:, kb * S5_NW:(kb + 1) * S5_NW].astype(BF16), ccr_ref[dr, kb],
                              preferred_element_type=F32)
                      - jnp.dot(bui[:, kb * S5_NW:(kb + 1) * S5_NW].astype(BF16), cci_ref[dr, kb],
                                preferred_element_type=F32))
                cols = slice(kb * S5_KW, (kb + 1) * S5_KW)
                if dr == 0:
                    yacc[pl.ds(r0, cr), cols] = yk
                else:
                    yacc[pl.ds(r0, cr), cols] = yacc[pl.ds(r0, cr), cols] + yk
            return carry

        hr, hi = lax.fori_loop(0, nch, chunk_body, (h0r_ref[0, dr], h0i_ref[0, dr]))
        hfr_ref[0, dr] = hr
        hfi_ref[0, dr] = hi

    def out_body(ci, _):
        r0 = pl.multiple_of(ci * cr, cr)
        u = u_ref[0, pl.ds(r0, cr), :]
        z = _gelu_tanh(yacc[pl.ds(r0, cr), :] + d_ref[...] * u)
        o_ref[0, pl.ds(r0, cr), :] = z * _sigmoid(jnp.dot(z.astype(BF16), wglu_ref[...],
                                                          preferred_element_type=F32))
        return 0

    lax.fori_loop(0, nch, out_body, 0)


def _s5(u_tb, mats, h0r, h0i, d_skip, w_glu_bf16, nseq):
    bbr, bbi, ccr, cci, lam = mats
    groups, rows, _ = u_tb.shape
    cr = 256
    full = lambda shape: pl.BlockSpec(shape, lambda g: (0,) * len(shape))
    h_spec = pl.BlockSpec((1, 2, nseq, S5_STATE), lambda g: (g, 0, 0, 0))
    io_spec = pl.BlockSpec((1, rows, MIX), lambda g: (g, 0, 0))
    return pl.pallas_call(
        functools.partial(_s5_kernel, nseq=nseq, rows=rows, cr=cr),
        grid=(groups,),
        in_specs=[io_spec, full(bbr.shape), full(bbi.shape), full(ccr.shape), full(cci.shape),
                  full(lam.shape), h_spec, h_spec, full((1, MIX)), full((MIX, MIX))],
        out_specs=[io_spec, h_spec, h_spec],
        out_shape=[jax.ShapeDtypeStruct((groups, rows, MIX), F32),
                   jax.ShapeDtypeStruct((groups, 2, nseq, S5_STATE), F32),
                   jax.ShapeDtypeStruct((groups, 2, nseq, S5_STATE), F32)],
        scratch_shapes=[pltpu.VMEM((cr, S5_STATE), F32), pltpu.VMEM((cr, S5_STATE), F32),
                        pltpu.VMEM((rows, MIX), F32)],
        compiler_params=_params("parallel"),
        name="s5_lat" if nseq == DEC_BATCH else "s5_ctx",
    )(u_tb, bbr, bbi, ccr, cci, lam, h0r, h0i, d_skip.reshape(1, MIX), w_glu_bf16)


def _s5_matrices(lam_re, lam_im, log_dt, b_re, b_im, c_re, c_im):
    dt = jnp.exp(log_dt)[..., None]
    mag = jnp.exp(lam_re * dt)
    ab_re = mag * jnp.cos(lam_im * dt)
    ab_im = mag * jnp.sin(lam_im * dt)
    den = lam_re * lam_re + lam_im * lam_im
    nr = ab_re - 1.0
    f_re = (nr * lam_re + ab_im * lam_im) / den
    f_im = (ab_im * lam_re - nr * lam_im) / den
    bb_re = f_re[..., None] * b_re - f_im[..., None] * b_im
    bb_im = f_re[..., None] * b_im + f_im[..., None] * b_re
    gl = G_C // S5_KB
    eye = jnp.eye(gl, dtype=F32)

    def in_proj(bb):
        x = bb.reshape(2, S5_KB, gl, P_C, S5_GROUP)
        return jnp.einsum('dkgps,gh->dkgshp', x, eye).reshape(2, S5_KB, S5_KW, S5_NW).astype(BF16)

    def out_proj(cc):
        x = cc.reshape(2, S5_KB, gl, S5_GROUP, P_C)
        return jnp.einsum('dkgsp,gh->dkgphs', x, eye).reshape(2, S5_KB, S5_NW, S5_KW).astype(BF16)

    lam = jnp.stack([ab_re.reshape(2, 1, S5_STATE), ab_im.reshape(2, 1, S5_STATE)], axis=1)
    return in_proj(bb_re), in_proj(bb_im), out_proj(c_re), out_proj(c_im), lam


def _ret_kernel(*refs, t, latent):
    if latent:
        (q_ref, k_ref, v_ref, g_ref, dl_ref, lnw_ref, lnb_ref, r0_ref, cos_ref, sin_ref, _,
         o_ref, ob_ref) = refs
    else:
        q_ref, k_ref, v_ref, g_ref, dl_ref, lnw_ref, lnb_ref, o_ref, rfin_ref, ob_ref = refs
    n = t // CHUNK
    w = HD_D
    jf = lax.broadcasted_iota(jnp.int32, (CHUNK, CHUNK), 0).astype(F32)
    kf = lax.broadcasted_iota(jnp.int32, (CHUNK, CHUNK), 1).astype(F32)
    diff = jf - kf
    tabs = []
    for dr in range(2):
        for h in range(H_D):
            l = -_softplus(-dl_ref[h, dr:dr + 1, :])
            if dr == 0:
                dmat = jnp.where(diff >= 0, jnp.exp(l * jnp.maximum(diff, 0.0)), 0.0)
                xi = jnp.exp(l * (jf + 1.0))
                zeta = jnp.exp(l * (CHUNK - 1.0 - jf))
            else:
                dmat = jnp.where(diff < 0, jnp.exp(l * jnp.maximum(-diff, 0.0)), 0.0)
                xi = jnp.exp(l * (CHUNK - jf))
                zeta = jnp.exp(l * jf)
            tabs.append((dmat, xi, zeta, jnp.exp(l * CHUNK)))
    chains = [(dr, h) for dr in range(2) for h in range(H_D)]

    def body(ci, states):
        q, k, v, where = [], [], [], []
        for dr, h in chains:
            cj = ci if dr == 0 else n - 1 - ci
            rows = pl.ds(pl.multiple_of(cj * CHUNK, CHUNK), CHUNK)
            cols = slice(h * w, (h + 1) * w)
            qi = q_ref[rows, cols]
            ki = k_ref[rows, cols] * (HD_D ** -0.5)
            if latent:
                qi = _rope_pairs(qi, cos_ref[rows, :], sin_ref[rows, :], HD_D // 2)
                ki = _rope_pairs(ki, cos_ref[rows, :], sin_ref[rows, :], HD_D // 2)
            q.append(qi.astype(BF16))
            k.append(ki)
            v.append(v_ref[rows, cols].astype(BF16))
            where.append((o_ref if dr == 0 else ob_ref, rows, cols))
        nch = range(len(chains))
        inner = [(_bdot(q[i], k[i], NT) * tabs[i][0]).astype(BF16) for i in nch]
        cross = [_bdot(q[i], states[i]) * tabs[i][1] for i in nch]
        kz = [(k[i] * tabs[i][2]).astype(BF16) for i in nch]
        out = [jnp.dot(inner[i], v[i], preferred_element_type=F32) + cross[i] for i in nch]
        new_states = [states[i] * tabs[i][3] + lax.dot_general(kz[i], v[i], TN, preferred_element_type=F32)
                      for i in nch]
        for (dst, rows, cols), y in zip(where, out):
            dst[rows, cols] = y
        return tuple(new_states)

    if latent:
        lax.fori_loop(0, n, body, tuple(r0_ref[0, dr, h] for dr, h in chains))
    else:
        fin = lax.fori_loop(0, n, body, tuple(jnp.zeros((w, w), F32) for _ in chains))
        for i, (dr, h) in enumerate(chains):
            rfin_ref[0, dr, h] = fin[i]
    for h in range(H_D):
        cols = slice(h * w, (h + 1) * w)
        y = o_ref[:, cols] + ob_ref[:, cols]
        yc = y - jnp.mean(y, axis=-1, keepdims=True)
        yn = yc * lax.rsqrt(jnp.mean(yc * yc, axis=-1, keepdims=True) + RET_GN_EPS)
        o_ref[:, cols] = (yn * lnw_ref[:, cols] + lnb_ref[:, cols]) * _silu(g_ref[:, cols])


def _retention(p, decay_logit, ln_w, ln_b, latent, r0=None, rope=None, dst=None):
    if latent:
        nseq, t, rb0 = DEC_BATCH, DEC_SEQ, M_CTX // DEC_SEQ
    else:
        nseq, t, rb0 = BATCH, SEQ, 0
    w = HD_D
    dl = jnp.broadcast_to(decay_logit.T[:, :, None], (H_D, 2, w))
    full = lambda shape: pl.BlockSpec(shape, lambda b: (0,) * len(shape))
    st_spec = pl.BlockSpec((1, 2, H_D, w, w), lambda b: (b, 0, 0, 0, 0))
    in_specs = [pl.BlockSpec((t, MIX), lambda b, j=j: (rb0 + b, j)) for j in (1, 2, 3, 4)]
    in_specs += [full((H_D, 2, w)), full((1, MIX)), full((1, MIX))]
    args = [p, p, p, p, dl, ln_w.reshape(1, MIX), ln_b.reshape(1, MIX)]
    o_spec = pl.BlockSpec((t, MIX), lambda b: (rb0 + b, 0))
    o_shape = jax.ShapeDtypeStruct((M_ALL, MIX), F32)
    if latent:
        in_specs += [st_spec, full((t, w)), full((t, w)), pl.BlockSpec(memory_space=pl.ANY)]
        args += [r0, *rope, dst]
        out_specs, out_shape, aliases = o_spec, o_shape, {len(args) - 1: 0}
    else:
        out_specs = [o_spec, st_spec]
        out_shape = [o_shape, jax.ShapeDtypeStruct((nseq, 2, H_D, w, w), F32)]
        aliases = {}
    return pl.pallas_call(
        functools.partial(_ret_kernel, t=t, latent=latent),
        grid=(nseq,),
        in_specs=in_specs, out_specs=out_specs, out_shape=out_shape,
        scratch_shapes=[pltpu.VMEM((t, MIX), F32)],
        input_output_aliases=aliases,
        compiler_params=_params("parallel"),
        name="retention_lat" if latent else "retention_ctx",
    )(*args)


def _rope_tables(n_tok, dim, reps):
    rows = n_tok // GRID_W
    n_freq = dim // 4
    inv = 1.0 / (ROPE_THETA ** (jnp.arange(n_freq, dtype=F32) / n_freq))
    row = jnp.repeat(jnp.arange(rows, dtype=F32), GRID_W)
    col = jnp.tile(jnp.arange(GRID_W, dtype=F32), rows)
    ang = jnp.concatenate([row[:, None] * inv, col[:, None] * inv], axis=-1)
    cos, sin = jnp.cos(ang), jnp.sin(ang)
    return jnp.tile(jnp.concatenate([cos, cos], axis=1), (1, reps)), \
        jnp.tile(jnp.concatenate([-sin, sin], axis=1), (1, reps))


def _time_major(x, nseq, group):
    t = x.shape[0] // nseq
    x = x.reshape(nseq // group, group, t, x.shape[1])
    return jnp.swapaxes(x, 1, 2).reshape(nseq // group, t * group, x.shape[-1])


def _seq_major(x, nseq, group):
    t = x.shape[1] // group
    x = x.reshape(nseq // group, t, group, x.shape[-1])
    return jnp.swapaxes(x, 1, 2).reshape(nseq * t, x.shape[-1])


def kernel(x_prompt, x_sample, cache_k_ab, cache_v_ab, state_rwkv, state_s5_re, state_s5_im, state_ret, c, c_ctx, norm1_g, norm2_g, w_mod, b_mod, w_ff_gate, w_ff_up, w_ff_down, w_in_ab, w_out_ab, qk_gain_a, lambda_qk, subln_g, rwkv_mu, rwkv_k_k, rwkv_k_a, rwkv_r_k, rwkv_w0, rwkv_w_up, rwkv_a0, rwkv_a_up, rwkv_g_up, rwkv_ln_w, rwkv_ln_b, w_in_cd, w_out_cd, s5_lam_re, s5_lam_im, s5_log_dt, s5_b_re, s5_b_im, s5_c_re, s5_c_im, s5_d, s5_w_glu, ret_decay_logit, ret_ln_w, ret_ln_b):
    d = D_MODEL
    xs = [x_prompt.reshape(M_CTX, d), x_sample.reshape(M_LAT, d)]
    cvec = jnp.zeros((MOD_ROWS, d), F32).at[0].set(c_ctx).at[1:1 + DEC_BATCH].set(c)
    mods = _modulation(cvec, w_mod, b_mod)

    lam_init = 0.8 - 0.6 * math.exp(-0.3 * 0)
    p = _norm_linear(xs, norm1_g[0], mods[0], w_in_ab[0].astype(BF16))
    gain2 = jnp.tile(qk_gain_a[0], (1, 2))
    rope_a = _rope_tables(DEC_SEQ, HD_A, 2)
    ck = cache_k_ab[:, 0].reshape(DEC_BATCH * PAST_LEN, MIX)
    cv = cache_v_ab[:, 0].reshape(DEC_BATCH * PAST_LEN, MIX)
    oa, k_ctx = _attention(p, gain2, lambda_qk[0], subln_g[0], lam_init, latent=False)
    oa = _attention(p, gain2, lambda_qk[0], subln_g[0], lam_init, latent=True,
                    cache_k=ck, cache_v=cv, rope=rope_a, dst=oa)
    rw_prm = (rwkv_mu[0], rwkv_k_k[0], rwkv_k_a[0], rwkv_r_k[0], rwkv_w0[0], rwkv_w_up[0],
              rwkv_a0[0], rwkv_a_up[0], rwkv_g_up[0])
    s0_lat = jnp.swapaxes(state_rwkv[:, 0], -1, -2)
    ob, sfin_ctx = _rwkv_scan(_rwkv_prep(p, rw_prm, latent=False), rwkv_ln_w[0], rwkv_ln_b[0], latent=False)
    ob = _rwkv_scan(_rwkv_prep(p, rw_prm, latent=True), rwkv_ln_w[0], rwkv_ln_b[0], latent=True,
                    s0t=s0_lat, dst=ob)
    x = _mix_ffn(xs, oa, ob, w_out_ab[0].astype(BF16), norm2_g[0], mods[0], w_ff_gate[0].astype(BF16),
                 w_ff_up[0].astype(BF16), w_ff_down[0].astype(BF16), split_out=False)

    p2 = _norm_linear([x], norm1_g[1], mods[1], w_in_cd[0].astype(BF16))
    mats = _s5_matrices(s5_lam_re[0], s5_lam_im[0], s5_log_dt[0], s5_b_re[0], s5_b_im[0],
                        s5_c_re[0], s5_c_im[0])
    wglu = s5_w_glu[0].astype(BF16)
    grp = 8
    u_ctx = _time_major(p2[:M_CTX, :MIX], BATCH, grp)
    u_lat = _time_major(p2[M_CTX:, :MIX], DEC_BATCH, DEC_BATCH)
    z_ctx = jnp.zeros((BATCH // grp, 2, grp, S5_STATE), F32)
    oc_ctx, hfr, hfi = _s5(u_ctx, mats, z_ctx, z_ctx, s5_d[0], wglu, grp)
    h0r = jnp.swapaxes(state_s5_re[:, 0].reshape(DEC_BATCH, 2, S5_STATE), 0, 1)[None]
    h0i = jnp.swapaxes(state_s5_im[:, 0].reshape(DEC_BATCH, 2, S5_STATE), 0, 1)[None]
    oc_lat, _, _ = _s5(u_lat, mats, h0r, h0i, s5_d[0], wglu, DEC_BATCH)
    oc = jnp.concatenate([_seq_major(oc_ctx, BATCH, grp), _seq_major(oc_lat, DEC_BATCH, DEC_BATCH)], axis=0)
    rope_d = _rope_tables(DEC_SEQ, HD_D, 1)
    od, rfin = _retention(p2, ret_decay_logit[0], ret_ln_w[0], ret_ln_b[0], latent=False)
    od = _retention(p2, ret_decay_logit[0], ret_ln_w[0], ret_ln_b[0], latent=True, r0=state_ret[:, 0],
                    rope=rope_d, dst=od)
    y_ctx, y_lat = _mix_ffn([x], oc, od, w_out_cd[0].astype(BF16), norm2_g[1], mods[1],
                            w_ff_gate[1].astype(BF16), w_ff_up[1].astype(BF16), w_ff_down[1].astype(BF16),
                            split_out=True)
    y_prompt = y_ctx.reshape(BATCH, SEQ, d)
    y_sample = y_lat.reshape(DEC_BATCH, DEC_SEQ, d)
    new_k = k_ctx.reshape(BATCH, 1, SEQ, H_A, 2, HD_A)
    new_v = p[:M_CTX, 2 * MIX:3 * MIX].reshape(BATCH, 1, SEQ, H_A, VD_A)
    new_rwkv = sfin_ctx[:, None]
    s5_shape = (BATCH, 2, G_C, P_C)
    new_s5_re = jnp.swapaxes(hfr, 1, 2).reshape(s5_shape)[:, None]
    new_s5_im = jnp.swapaxes(hfi, 1, 2).reshape(s5_shape)[:, None]
    new_ret = rfin[:, None]
    return (y_prompt, y_sample, new_k, new_v, new_rwkv, new_s5_re, new_s5_im, new_ret)
```

```python
import functools
import math

import numpy as np
import jax
import jax.numpy as jnp
from jax import lax
from jax.experimental import pallas as pl
from jax.experimental.pallas import tpu as pltpu

F32 = jnp.float32
BF16 = jnp.bfloat16
HIGHEST = lax.Precision.HIGHEST

D_MODEL = 1024
BATCH = 32
SEQ = 256
DEC_BATCH = 2
DEC_SEQ = 1024
PAST_LEN = 256
GRID_W = 64
H_A = 4
HD_A = 64
VD_A = 128
H_B = 8
HD_B = 64
MIX = 512
LORA_W = 64
LORA_A = 64
LORA_G = 128
S5_GROUP = 16
G_C = 32
P_C = 64
S5_STATE = G_C * P_C
H_D = 4
HD_D = 128
CHUNK = 128
D_FF = 2816
IN_AB = 3328
IN_CD = 2560
ROPE_THETA = 10000.0
NORM_EPS = 1e-6
RWKV_GN_EPS = 64e-5
RET_GN_EPS = 1e-5

M_CTX = BATCH * SEQ
M_LAT = DEC_BATCH * DEC_SEQ
M_ALL = M_CTX + M_LAT
MOD_ROWS = 8
RWKV_CHUNK = 64
VMEM_LIMIT = 56 * 1024 * 1024

NN = (((1,), (0,)), ((), ()))
NT = (((1,), (1,)), ((), ()))
TN = (((0,), (0,)), ((), ()))


def _params(*sem):
    return pltpu.CompilerParams(dimension_semantics=sem, vmem_limit_bytes=VMEM_LIMIT)


def _bdot(a, b, dims=NN):
    return lax.dot_general(a.astype(BF16), b.astype(BF16), dims, preferred_element_type=F32)


def _hdot(a, b, dims=NN):
    return lax.dot_general(a, b, dims, precision=HIGHEST, preferred_element_type=F32)


def _split_dot(x, m):
    hi = x.astype(BF16)
    lo = (x - hi.astype(F32)).astype(BF16)
    return (jnp.dot(hi, m, preferred_element_type=F32) + jnp.dot(lo, m, preferred_element_type=F32))


def _seg_matrix(n, shift, val):
    r = lax.broadcasted_iota(jnp.int32, (n, n), 0) >> shift
    c = lax.broadcasted_iota(jnp.int32, (n, n), 1) >> shift
    return jnp.where(r == c, val, 0.0).astype(BF16)


def _sigmoid(x):
    return jax.nn.sigmoid(x)


def _silu(x):
    return x * jax.nn.sigmoid(x)


def _softplus(x):
    return jnp.maximum(x, 0.0) + jnp.log(1.0 + jnp.exp(-jnp.abs(x)))


def _mod_row(tile, tm):
    r0 = tile * tm
    return jnp.where(r0 < M_CTX, 0, 1 + (r0 - M_CTX) // DEC_SEQ)


def _norm_mod(x, g, sc_ref, sh_ref, row):
    y = x * lax.rsqrt(jnp.mean(x * x, axis=-1, keepdims=True) + NORM_EPS) * g
    return y * (1.0 + sc_ref[pl.ds(row, 1), :]) + sh_ref[pl.ds(row, 1), :]


def _mod_kernel(c_ref, w_ref, b_ref, o_ref):
    o_ref[0] = _hdot(_silu(c_ref[...]), w_ref[0]) + b_ref[0]


def _modulation(cvec, w_mod, b_mod):
    depth, d, n6 = w_mod.shape
    tn = 1536
    return pl.pallas_call(
        _mod_kernel,
        grid=(depth, n6 // tn),
        in_specs=[pl.BlockSpec((MOD_ROWS, d), lambda l, j: (0, 0)),
                  pl.BlockSpec((1, d, tn), lambda l, j: (l, 0, j)),
                  pl.BlockSpec((1, 1, tn), lambda l, j: (l, 0, j))],
        out_specs=pl.BlockSpec((1, MOD_ROWS, tn), lambda l, j: (l, 0, j)),
        out_shape=jax.ShapeDtypeStruct((depth, MOD_ROWS, n6), F32),
        compiler_params=_params("parallel", "parallel"),
        name="modulation",
    )(cvec, w_mod, b_mod.reshape(depth, 1, n6))


def _tile_specs(tm, d):
    nc = M_CTX // tm
    return [pl.BlockSpec((tm, d), lambda i: (jnp.minimum(i, nc - 1), 0)),
            pl.BlockSpec((tm, d), lambda i: (jnp.maximum(i - nc, 0), 0))]


def _pick_rows(refs, tm):
    if len(refs) == 1:
        return refs[0][...]
    return jnp.where(pl.program_id(0) < M_CTX // tm, refs[0][...], refs[1][...])


def _norm_linear_kernel(*refs, tm):
    *x_refs, g_ref, sc_ref, sh_ref, w_ref, o_ref = refs
    row = _mod_row(pl.program_id(0), tm)
    h = _norm_mod(_pick_rows(x_refs, tm), g_ref[...], sc_ref, sh_ref, row)
    o_ref[...] = jnp.dot(h.astype(BF16), w_ref[...], preferred_element_type=F32)


def _norm_linear(xs, g, mods, w_bf16):
    tm = 256
    d = D_MODEL
    n = w_bf16.shape[1]
    x_specs = _tile_specs(tm, d) if len(xs) == 2 else [pl.BlockSpec((tm, d), lambda i: (i, 0))]
    return pl.pallas_call(
        functools.partial(_norm_linear_kernel, tm=tm),
        grid=(M_ALL // tm,),
        in_specs=x_specs + [pl.BlockSpec((1, d), lambda i: (0, 0)),
                            pl.BlockSpec((MOD_ROWS, d), lambda i: (0, 1)),
                            pl.BlockSpec((MOD_ROWS, d), lambda i: (0, 0)),
                            pl.BlockSpec((d, n), lambda i: (0, 0))],
        out_specs=pl.BlockSpec((tm, n), lambda i: (i, 0)),
        out_shape=jax.ShapeDtypeStruct((M_ALL, n), F32),
        compiler_params=_params("arbitrary"),
        name="norm_linear",
    )(*xs, g.reshape(1, d), mods, mods, w_bf16)


def _mix_ffn_kernel(*refs, tm, ck, n_in, n_out):
    x_refs = refs[:n_in]
    (a_ref, b_ref, wa_ref, wb_ref, g1_ref, ng_ref, sc_ref, sh_ref, g2_ref,
     wg_ref, wu_ref, wd_ref) = refs[n_in:n_in + 12]
    o_refs = refs[n_in + 12:]
    row = _mod_row(pl.program_id(0), tm)
    mix = (jnp.dot(a_ref[...].astype(BF16), wa_ref[...], preferred_element_type=F32)
           + jnp.dot(b_ref[...].astype(BF16), wb_ref[...], preferred_element_type=F32))
    x = _pick_rows(x_refs, tm) + g1_ref[pl.ds(row, 1), :] * mix
    h = _norm_mod(x, ng_ref[...], sc_ref, sh_ref, row).astype(BF16)
    acc = jnp.zeros((tm, D_MODEL), F32)
    for c in range(D_FF // ck):
        gg = jnp.dot(h, wg_ref[:, c * ck:(c + 1) * ck], preferred_element_type=F32)
        uu = jnp.dot(h, wu_ref[:, c * ck:(c + 1) * ck], preferred_element_type=F32)
        act = (_silu(gg) * uu).astype(BF16)
        acc = acc + jnp.dot(act, wd_ref[c * ck:(c + 1) * ck, :], preferred_element_type=F32)
    y = x + g2_ref[pl.ds(row, 1), :] * acc
    if n_out == 1:
        o_refs[0][...] = y
    else:
        is_ctx = pl.program_id(0) < M_CTX // tm

        @pl.when(is_ctx)
        def _():
            o_refs[0][...] = y

        @pl.when(jnp.logical_not(is_ctx))
        def _():
            o_refs[1][...] = y


def _mix_ffn(xs, oa, ob, w_out_bf16, norm_g, mods, wg, wu, wd, split_out):
    tm, ck = 512, 256
    d = D_MODEL
    row_spec = lambda n: pl.BlockSpec((tm, n), lambda i: (i, 0))
    const = lambda shape, idx: pl.BlockSpec(shape, lambda i: idx, pipeline_mode=pl.Buffered(1))
    mod = lambda j: pl.BlockSpec((MOD_ROWS, d), lambda i: (0, j))
    x_specs = _tile_specs(tm, d) if len(xs) == 2 else [row_spec(d)]
    if split_out:
        out_specs = _tile_specs(tm, d)
        out_shape = [jax.ShapeDtypeStruct((M_CTX, d), F32), jax.ShapeDtypeStruct((M_LAT, d), F32)]
    else:
        out_specs, out_shape = row_spec(d), jax.ShapeDtypeStruct((M_ALL, d), F32)
    return pl.pallas_call(
        functools.partial(_mix_ffn_kernel, tm=tm, ck=ck, n_in=len(xs), n_out=2 if split_out else 1),
        grid=(M_ALL // tm,),
        in_specs=x_specs + [row_spec(MIX), row_spec(MIX),
                            const((MIX, d), (0, 0)), const((MIX, d), (1, 0)),
                            mod(2),
                            pl.BlockSpec((1, d), lambda i: (0, 0)),
                            mod(4), mod(3), mod(5),
                            const((d, D_FF), (0, 0)), const((d, D_FF), (0, 0)), const((D_FF, d), (0, 0))],
        out_specs=out_specs, out_shape=out_shape,
        compiler_params=_params("arbitrary"),
        name="mix_ffn",
    )(*xs, oa, ob, w_out_bf16, w_out_bf16, mods, norm_g.reshape(1, d), mods, mods, mods, wg, wu, wd)


def _qk_norm(x, gain, segm):
    ms = _split_dot(x * x, segm)
    return x * lax.rsqrt(ms + NORM_EPS) * gain


def _rope_pairs(x, cosf, sinf, half):
    lane = lax.broadcasted_iota(jnp.int32, x.shape, 1)
    first = (lane & (2 * half - 1)) < half
    n = x.shape[1]
    partner = jnp.where(first, pltpu.roll(x, n - half, axis=1), pltpu.roll(x, half, axis=1))
    return x * cosf + partner * sinf


def _attn_kernel(*refs, latent, lam_init):
    if latent:
        (q_ref, k_ref, v_ref, ck_ref, cv_ref, cosq_ref, sinq_ref, cosk_ref, sink_ref,
         gain_ref, lam_ref, sub_ref, _, o_ref, kall, vall) = refs
    else:
        q_ref, k_ref, v_ref, gain_ref, lam_ref, sub_ref, o_ref, kn_ref, kall, vall = refs
    w = 2 * HD_A
    segm = _seg_matrix(w, 6, 1.0 / HD_A)
    gains = gain_ref[...]
    cols = [slice(h * w, (h + 1) * w) for h in range(H_A)]

    @pl.when(pl.program_id(1) == 0)
    def _():
        k = [_qk_norm(k_ref[:, c], gains[1:2], segm) for c in cols]
        if latent:
            k = [_rope_pairs(x, cosk_ref[...], sink_ref[...], HD_A // 2) for x in k]
            for c, x in zip(cols, k):
                kall[0:PAST_LEN, c] = ck_ref[:, c].astype(BF16)
                kall[PAST_LEN:, c] = x.astype(BF16)
                vall[0:PAST_LEN, c] = cv_ref[:, c].astype(BF16)
                vall[PAST_LEN:, c] = v_ref[:, c].astype(BF16)
        else:
            for c, x in zip(cols, k):
                kn_ref[:, c] = x
                kall[:, c] = x.astype(BF16)
                vall[:, c] = v_ref[:, c].astype(BF16)

    q = [_qk_norm(q_ref[:, c], gains[0:1], segm) for c in cols]
    if latent:
        q = [_rope_pairs(x, cosq_ref[...], sinq_ref[...], HD_A // 2) for x in q]
    lv = lam_ref[...]
    lam = (jnp.exp(jnp.sum(lv[0:1] * lv[1:2], axis=1, keepdims=True))
           - jnp.exp(jnp.sum(lv[2:3] * lv[3:4], axis=1, keepdims=True)) + lam_init)
    scale = HD_A ** -0.5
    comp0 = lax.broadcasted_iota(jnp.int32, q[0].shape, 1) < HD_A
    qc = [jnp.where(comp0, *sel).astype(BF16) for x in q for sel in ((x, 0.0), (0.0, x))]
    s = [lax.dot_general(qc[i], kall[:, cols[i // 2]], NT, preferred_element_type=F32) * scale
         for i in range(2 * H_A)]
    e = [jnp.exp(x - jnp.max(x, axis=-1, keepdims=True)) for x in s]
    p = [x / jnp.sum(x, axis=-1, keepdims=True) for x in e]
    att = [(p[2 * h] - lam * p[2 * h + 1]).astype(BF16) for h in range(H_A)]
    o = [jnp.dot(att[h], vall[:, cols[h]], preferred_element_type=F32) for h in range(H_A)]
    o = [x * lax.rsqrt(jnp.mean(x * x, axis=-1, keepdims=True) + NORM_EPS) * sub_ref[...] for x in o]
    for c, x in zip(cols, o):
        o_ref[:, c] = x * (1.0 - lam_init)


def _attention(p, gain2, lambda_qk, subln_g, lam_init, latent, cache_k=None, cache_v=None, rope=None,
               dst=None):
    w = 2 * HD_A
    if latent:
        nseq, t, tq, rb0, s_len = DEC_BATCH, DEC_SEQ, 128, M_CTX // DEC_SEQ, PAST_LEN + DEC_SEQ
    else:
        nseq, t, tq, rb0, s_len = BATCH, SEQ, SEQ, 0, SEQ
    nq = t // tq
    qoff = rb0 * nq
    full = lambda shape: pl.BlockSpec(shape, lambda b, i: (0,) * len(shape))
    in_specs = [pl.BlockSpec((tq, MIX), lambda b, i: (qoff + b * nq + i, 0)),
                pl.BlockSpec((t, MIX), lambda b, i: (rb0 + b, 1)),
                pl.BlockSpec((t, MIX), lambda b, i: (rb0 + b, 2))]
    args = [p, p, p]
    if latent:
        cosf, sinf = rope
        in_specs += [pl.BlockSpec((PAST_LEN, MIX), lambda b, i: (b, 0)),
                     pl.BlockSpec((PAST_LEN, MIX), lambda b, i: (b, 0)),
                     pl.BlockSpec((tq, w), lambda b, i: (i, 0)),
                     pl.BlockSpec((tq, w), lambda b, i: (i, 0)),
                     full((t, w)), full((t, w))]
        args += [cache_k, cache_v, cosf, sinf, cosf, sinf]
    in_specs += [full((2, w)), full((4, HD_A)), full((1, w))]
    args += [gain2, lambda_qk, subln_g.reshape(1, w)]
    o_spec = pl.BlockSpec((tq, MIX), lambda b, i: (qoff + b * nq + i, 0))
    o_shape = jax.ShapeDtypeStruct((M_ALL, MIX), F32)
    aliases = {}
    if latent:
        out_specs, out_shape = o_spec, o_shape
        in_specs.append(pl.BlockSpec(memory_space=pl.ANY))
        args.append(dst)
        aliases = {len(args) - 1: 0}
    else:
        out_specs = [o_spec, pl.BlockSpec((t, MIX), lambda b, i: (b, 0))]
        out_shape = [o_shape, jax.ShapeDtypeStruct((nseq * t, MIX), F32)]
    return pl.pallas_call(
        functools.partial(_attn_kernel, latent=latent, lam_init=lam_init),
        grid=(nseq, nq),
        in_specs=in_specs, out_specs=out_specs, out_shape=out_shape,
        scratch_shapes=[pltpu.VMEM((s_len, MIX), BF16), pltpu.VMEM((s_len, MIX), BF16)],
        input_output_aliases=aliases,
        compiler_params=_params("parallel", "arbitrary"),
        name="diff_attention_lat" if latent else "diff_attention_ctx",
    )(*args)


def _centred_shift(x, mu):
    t = x.shape[0]
    row = lax.broadcasted_iota(jnp.int32, x.shape, 0)
    prev = jnp.where(row == 0, 0.0, pltpu.roll(x, 1, axis=0))
    nxt = jnp.where(row == t - 1, 0.0, pltpu.roll(x, t - 1, axis=0))
    return x + (0.5 * (prev + nxt) - x) * mu


def _seg_sum(x, segm):
    return jnp.concatenate([_split_dot(x[:, j * 128:(j + 1) * 128], segm) for j in range(x.shape[1] // 128)],
                           axis=1)


def _rwkv_prep_kernel(r_ref, k_ref, v_ref, l_ref, mur_ref, muk_ref, muv_ref, mul_ref,
                      kk_ref, ka_ref, rk_ref, w0_ref, wup_ref, a0_ref, aup_ref, gup_ref,
                      ro_ref, ldf_ref, ldb_ref, kbo_ref, vbo_ref, kko_ref, ao_ref, gate_ref, bonus_ref):
    seg1 = _seg_matrix(128, 6, 1.0)
    r = _centred_shift(r_ref[...], mur_ref[...])
    kb = _centred_shift(k_ref[...], muk_ref[...])
    vb = _centred_shift(v_ref[...], muv_ref[...])
    lo = _centred_shift(l_ref[...], mul_ref[...])
    xw = lo[:, 0:LORA_W]
    xa = lo[:, LORA_W:LORA_W + LORA_A]
    xg = lo[:, LORA_W + LORA_A:]
    kk = kb * kk_ref[...]
    kk = kk * lax.rsqrt(_seg_sum(kk * kk, seg1) + 1e-12)
    a = _sigmoid(a0_ref[...] + _bdot(xa, aup_ref[...]))
    kb2 = kb * (1.0 + (a - 1.0) * ka_ref[...])
    lw = jnp.tanh(xw)
    for dr, ld_ref in enumerate((ldf_ref, ldb_ref)):
        z = w0_ref[dr:dr + 1, :] + _bdot(lw, wup_ref[dr])
        logw = -_softplus(-z) - 0.5
        ld_ref[...] = -jnp.exp(logw)
    gate_ref[...] = _bdot(_sigmoid(xg), gup_ref[...])
    bonus_ref[...] = _seg_sum(r * kb2 * rk_ref[...], seg1) * vb
    ro_ref[...] = r
    kbo_ref[...] = kb2
    vbo_ref[...] = vb
    kko_ref[...] = kk
    ao_ref[...] = a


def _rwkv_prep(p, prm, latent):
    mu, k_k, k_a, r_k, w0, w_up, a0, a_up, g_up = prm
    if latent:
        nseq, t, rb0 = DEC_BATCH, DEC_SEQ, M_CTX // DEC_SEQ
    else:
        nseq, t, rb0 = BATCH, SEQ, 0
    lw = LORA_W + LORA_A + LORA_G
    c0 = (IN_AB - 3 * MIX - lw) // MIX
    cl = (IN_AB - lw) // lw
    row = lambda n: pl.BlockSpec((1, n), lambda b: (0, 0))
    in_specs = [pl.BlockSpec((t, MIX), lambda b: (rb0 + b, c0)),
                pl.BlockSpec((t, MIX), lambda b: (rb0 + b, c0 + 1)),
                pl.BlockSpec((t, MIX), lambda b: (rb0 + b, c0 + 2)),
                pl.BlockSpec((t, lw), lambda b: (rb0 + b, cl)),
                row(MIX), row(MIX), row(MIX), row(lw),
                row(MIX), row(MIX), row(MIX),
                pl.BlockSpec((2, MIX), lambda b: (0, 0)),
                pl.BlockSpec((2, LORA_W, MIX), lambda b: (0, 0, 0)),
                row(MIX),
                pl.BlockSpec((LORA_A, MIX), lambda b: (0, 0)),
                pl.BlockSpec((LORA_G, MIX), lambda b: (0, 0))]
    o_spec = pl.BlockSpec((t, MIX), lambda b: (b, 0))
    o_shape = jax.ShapeDtypeStruct((nseq * t, MIX), F32)
    return pl.pallas_call(
        _rwkv_prep_kernel,
        grid=(nseq,),
        in_specs=in_specs, out_specs=[o_spec] * 9, out_shape=[o_shape] * 9,
        compiler_params=_params("parallel"),
        name="rwkv_prep_lat" if latent else "rwkv_prep_ctx",
    )(p, p, p, p,
      mu[None, 0:MIX], mu[None, MIX:2 * MIX], mu[None, 2 * MIX:3 * MIX], mu[None, 3 * MIX:],
      k_k.reshape(1, MIX), k_a.reshape(1, MIX), r_k.reshape(1, MIX), w0, w_up,
      a0.reshape(1, MIX), a_up, g_up)


def _split2(x):
    hi = x.astype(BF16)
    return hi, (x - hi.astype(F32)).astype(BF16)


RWKV_INV_BLOCK = 16


def _rwkv_masks(c, rev):
    ti = lax.broadcasted_iota(jnp.int32, (c, c), 0)
    si = lax.broadcasted_iota(jnp.int32, (c, c), 1)
    tri = jnp.where((si >= ti) if rev else (si <= ti), 1.0, 0.0).astype(BF16)
    t4 = lax.broadcasted_iota(jnp.int32, (4 * c, 4 * c), 0)
    s4 = lax.broadcasted_iota(jnp.int32, (4 * c, 4 * c), 1)
    tm, sm = t4 & (c - 1), s4 & (c - 1)
    strict = (sm > tm) if rev else (sm < tm)
    incl = (sm >= tm) if rev else (sm <= tm)
    same_head = ((t4 // c) & 1) == ((s4 // c) & 1)
    top = t4 < 2 * c
    gmask = same_head & ((top & strict) | (~top & incl))
    t2 = lax.broadcasted_iota(jnp.int32, (2 * c, 2 * c), 0)
    s2 = lax.broadcasted_iota(jnp.int32, (2 * c, 2 * c), 1)
    same = lambda n: (t2 // n) == (s2 // n)
    levels = []
    n = RWKV_INV_BLOCK
    while n < c:
        levels.append(same(2 * n) & ~same(n))
        n *= 2
    f = lambda m: jnp.where(m, 1.0, 0.0)
    return tri, f(gmask), f(same(RWKV_INV_BLOCK)), tuple(f(m) for m in levels), f(same(c)), f(t2 == s2)


def _keep(mask01, x):
    return jnp.where(mask01 > 0.5, x, 0.0)


def _tri_inverse(a, diag_blk, levels, eye):
    n = a[0].shape[0]
    d = [_keep(diag_blk, x) for x in a]
    t = [eye + x for x in d]
    p = [_bdot(x, x) for x in d]
    for _ in range(int(math.log2(RWKV_INV_BLOCK)) - 2):
        res = [_bdot(jnp.concatenate([pi, ti], axis=0), pi) for pi, ti in zip(p, t)]
        p = [x[:n] for x in res]
        t = [ti + x[n:] for ti, x in zip(t, res)]
    t = [ti + _bdot(ti, pi) for ti, pi in zip(t, p)]
    for off in levels:
        half = [_bdot(ti, _keep(off, x)) for ti, x in zip(t, a)]
        t = [ti + _bdot(x, ti) for ti, x in zip(t, half)]
    return t


def _rwkv_pair_chunks(ins, sts, tris, gmasks, diag_blk, levels, same_head, eye):
    c, w = ins[0][0].shape
    hd = w // 2
    nch = range(len(ins))
    r, ld, kb, vb, kk, a = (list(z) for z in zip(*ins))
    split = [_split2(x) for x in ld]
    lcum = [jnp.dot(tris[i], split[i][0], preferred_element_type=F32)
            + jnp.dot(tris[i], split[i][1], preferred_element_type=F32) for i in nch]
    ones = jnp.ones((c, w), BF16)
    pcol = [jnp.exp(lax.dot_general(hi, ones, TN, preferred_element_type=F32)
                    + lax.dot_general(lo, ones, TN, preferred_element_type=F32)) for hi, lo in split]
    ltot = [jnp.sum(x, axis=0, keepdims=True) for x in ld]
    beta = [kk[i] * a[i] for i in nch]
    eneg = [jnp.exp(-x) for x in lcum]
    abar = [-kk[i] * jnp.exp(lcum[i] - ld[i]) for i in nch]
    rbar = [r[i] * jnp.exp(lcum[i]) for i in nch]
    bt = [(beta[i] * eneg[i]).astype(BF16) for i in nch]
    kt = [(kb[i] * eneg[i]).astype(BF16) for i in nch]
    vbb = [x.astype(BF16) for x in vb]
    head0 = lax.broadcasted_iota(jnp.int32, (c, w), 1) < hd
    pick = lambda res: jnp.where(head0, res[:c], res[c:])
    arst = [_bdot(jnp.concatenate([abar[i], rbar[i]], axis=0), sts[i]) for i in nch]
    lhs = [jnp.concatenate([jnp.where(head0, abar[i], 0.0), jnp.where(head0, 0.0, abar[i]),
                            jnp.where(head0, rbar[i], 0.0), jnp.where(head0, 0.0, rbar[i])], axis=0) for i in nch]
    g = [_keep(gmasks[i], _bdot(lhs[i], jnp.concatenate([bt[i], bt[i], kt[i], kt[i]], axis=0), NT))
         for i in nch]
    x = [arst[i][:c] + pick(_bdot(g[i][:2 * c, 2 * c:], jnp.concatenate([vbb[i], vbb[i]], axis=0))) for i in nch]
    tinv = _tri_inverse([gi[:2 * c, :2 * c] for gi in g], diag_blk, levels, eye)
    u = [pick(_bdot(tinv[i], jnp.concatenate([x[i], x[i]], axis=0))) for i in nch]
    ub = [z.astype(BF16) for z in u]
    y = [arst[i][c:] + pick(_bdot(g[i][2 * c:], jnp.concatenate([ub[i], ub[i], vbb[i], vbb[i]], axis=0)))
         for i in nch]
    erem = [jnp.exp(ltot[i] - lcum[i]) for i in nch]
    bkh = [jnp.concatenate([beta[i] * erem[i], kb[i] * erem[i]], axis=0) for i in nch]
    st_new = [pcol[i] * sts[i] + _keep(same_head, _bdot(bkh[i], jnp.concatenate([ub[i], vbb[i]], axis=0), TN))
              for i in nch]
    return y, st_new


def _rwkv_scan_kernel(*refs, t, c, npair, latent):
    r_ref, ldf_ref, ldb_ref, kb_ref, vb_ref, kk_ref, a_ref, bonus_ref, gate_ref, lnw_ref, lnb_ref = refs[:11]
    if latent:
        s0_ref, _, o_ref, yb_ref = refs[11:]
    else:
        o_ref, sfin_ref, yb_ref = refs[11:]
    n = t // c
    w = 2 * HD_B
    masks = [_rwkv_masks(c, rev) for rev in (False, True)]

    def body(ci, states):
        ins, tris, gmasks, dsts = [], [], [], []
        for dr, (ld_ref, dst) in enumerate(((ldf_ref, o_ref), (ldb_ref, yb_ref))):
            cj = ci if dr == 0 else n - 1 - ci
            rows = pl.ds(pl.multiple_of(cj * c, c), c)
            for p in range(npair):
                cols = slice(p * w, (p + 1) * w)
                ins.append(tuple(ref[rows, cols] for ref in (r_ref, ld_ref, kb_ref, vb_ref, kk_ref, a_ref)))
                tris.append(masks[dr][0])
                gmasks.append(masks[dr][1])
                dsts.append((dst, rows, cols))
        ys, new_states = _rwkv_pair_chunks(ins, list(states), tris, gmasks, *masks[0][2:])
        for (dst, rows, cols), y in zip(dsts, ys):
            dst[rows, cols] = y
        return tuple(new_states)

    if latent:
        zero = jnp.zeros((HD_B, HD_B), F32)
        init = tuple(jnp.concatenate([jnp.concatenate([s0_ref[0, dr, 2 * p], zero], axis=1),
                                      jnp.concatenate([zero, s0_ref[0, dr, 2 * p + 1]], axis=1)], axis=0)
                     for dr in range(2) for p in range(npair))
    else:
        init = tuple(jnp.zeros((w, w), F32) for _ in range(2 * npair))
    fin = lax.fori_loop(0, n, body, init)
    segm = _seg_matrix(w, 6, 1.0 / HD_B)
    for p in range(npair):
        if not latent:
            for dr in range(2):
                st = fin[dr * npair + p].T
                sfin_ref[0, dr, 2 * p] = st[:HD_B, :HD_B]
                sfin_ref[0, dr, 2 * p + 1] = st[HD_B:, HD_B:]
        cols = slice(p * w, (p + 1) * w)
        y = o_ref[:, cols] + yb_ref[:, cols]
        yc = y - _split_dot(y, segm)
        yn = yc * lax.rsqrt(_split_dot(yc * yc, segm) + RWKV_GN_EPS)
        o_ref[:, cols] = (yn * lnw_ref[:, cols] + lnb_ref[:, cols] + bonus_ref[:, cols]) * gate_ref[:, cols]


def _rwkv_scan(prep, ln_w, ln_b, latent, s0t=None, dst=None):
    r, ldf, ldb, kb, vb, kk, a, gate, bonus = prep
    nseq, t, rb0 = (DEC_BATCH, DEC_SEQ, M_CTX // DEC_SEQ) if latent else (BATCH, SEQ, 0)
    npair = H_B // 2
    w = npair * 2 * HD_B
    blk = pl.BlockSpec((t, w), lambda b, h: (b, h))
    vec = pl.BlockSpec((1, w), lambda b, h: (0, h))
    st_spec = pl.BlockSpec((1, 2, 2 * npair, HD_B, HD_B), lambda b, h: (b, 0, h, 0, 0))
    o_spec = pl.BlockSpec((t, w), lambda b, h: (rb0 + b, h))
    o_shape = jax.ShapeDtypeStruct((M_ALL, MIX), F32)
    args = [r, ldf, ldb, kb, vb, kk, a, bonus, gate, ln_w.reshape(1, MIX), ln_b.reshape(1, MIX)]
    in_specs = [blk] * 9 + [vec, vec]
    if latent:
        args += [s0t, dst]
        in_specs += [st_spec, pl.BlockSpec(memory_space=pl.ANY)]
        out_specs, out_shape, aliases = o_spec, o_shape, {len(args) - 1: 0}
    else:
        out_specs = [o_spec, st_spec]
        out_shape = [o_shape, jax.ShapeDtypeStruct((nseq, 2, H_B, HD_B, HD_B), F32)]
        aliases = {}
    return pl.pallas_call(
        functools.partial(_rwkv_scan_kernel, t=t, c=RWKV_CHUNK, npair=npair, latent=latent),
        grid=(nseq, MIX // w),
        in_specs=in_specs, out_specs=out_specs, out_shape=out_shape,
        scratch_shapes=[pltpu.VMEM((t, w), F32)],
        input_output_aliases=aliases,
        compiler_params=_params("parallel", "parallel"),
        name="rwkv_scan_lat" if latent else "rwkv_scan_ctx",
    )(*args)


S5_L = 16
S5_ROWS = M_ALL // S5_L
S5_CW = S5_L * S5_GROUP
S5_GS = 4
S5_PW = 128
S5_TAB = 24
S5_PRM = 2 * S5_GROUP + 8
S5_NC_CTX = SEQ // S5_L
S5_NC_LAT = DEC_SEQ // S5_L
S5_CTX_ROWS = BATCH * S5_NC_CTX


def _gelu_tanh(x):
    return 0.5 * x * (1.0 + jnp.tanh(math.sqrt(2.0 / math.pi) * (x + 0.044715 * (x * x * x))))


def _cmul(ar, ai, br, bi):
    return ar * br - ai * bi, ar * bi + ai * br


def _dot3(a, b, dims):
    ah, al = _split2(a)
    bh, bl = _split2(b)
    d = lambda x, y: lax.dot_general(x, y, dims, preferred_element_type=F32)
    return d(ah, bh) + d(ah, bl) + d(al, bh)


def _s5_param_kernel(prm_ref, tab_ref, r_ref, st_ref):
    nl, ns = S5_L, S5_GROUP
    lane_blk = lax.broadcasted_iota(jnp.int32, (ns, S5_CW), 1) // ns
    t_rows = [jnp.zeros((ns, S5_CW), F32) for _ in range(nl)]
    e_parts = []
    for d in range(2):
        part = lambda lo, n: (prm_ref[0, d, 0, lo:lo + n, :], prm_ref[0, d, 1, lo:lo + n, :])
        bb = _cmul(*part(2 * ns, 1), *part(0, ns))
        cc = part(ns, ns)
        pw = lambda k: (tab_ref[0, d, 0, k:k + 1, :], tab_ref[0, d, 1, k:k + 1, :])
        steps = range(nl)
        if d == 0:
            ke, ks, kk = [nl - 1 - j for j in steps], [j + 1 for j in steps], list(steps)
        else:
            ke, ks, kk = list(steps), [nl - j for j in steps], [nl - 1 - j for j in steps]
        stack = lambda xs: (jnp.concatenate([x[0] for x in xs], axis=0), jnp.concatenate([x[1] for x in xs], axis=0))
        e_re, e_im = stack([_cmul(*bb, *pw(k)) for k in ke])
        s_re, s_im = stack([_cmul(*cc, *pw(k)) for k in ks])
        k_re, k_im = stack([_cmul(*cc, *pw(k)) for k in kk])
        spad = jnp.zeros((S5_CW, S5_PW - P_C), F32)
        st_ref[0, 2 * d] = jnp.concatenate([s_re, spad], axis=1).astype(BF16)
        st_ref[0, 2 * d + 1] = jnp.concatenate([-s_im, spad], axis=1).astype(BF16)
        krow = _dot3(bb[0], k_re, NT) - _dot3(bb[1], k_im, NT)
        for j in steps:
            if d == 0:
                shifted, keep = pltpu.roll(krow, ns * j, axis=1), lane_blk >= j
            else:
                shifted, keep = pltpu.roll(krow, (S5_CW - ns * (nl - 1 - j)) % S5_CW, axis=1), lane_blk <= j
            t_rows[j] = t_rows[j] + jnp.where(keep, shifted, 0.0)
        e_parts += [e_re, e_im]
    zero = jnp.zeros((S5_CW, P_C), F32)
    pieces = [jnp.concatenate(t_rows, axis=0)]
    for e in e_parts:
        pieces += [e, zero]
    r_ref[0] = jnp.concatenate(pieces, axis=1).astype(BF16)


def _s5_params(prm, tab):
    return pl.pallas_call(
        _s5_param_kernel,
        grid=(G_C,),
        in_specs=[pl.BlockSpec((1, 2, 2, S5_PRM, P_C), lambda g: (g, 0, 0, 0, 0)),
                  pl.BlockSpec((1, 2, 2, S5_TAB, P_C), lambda g: (g, 0, 0, 0, 0))],
        out_specs=[pl.BlockSpec((1, S5_CW, 3 * S5_CW), lambda g: (g, 0, 0)),
                   pl.BlockSpec((1, 4, S5_CW, S5_PW), lambda g: (g, 0, 0, 0))],
        out_shape=[jax.ShapeDtypeStruct((G_C, S5_CW, 3 * S5_CW), BF16),
                   jax.ShapeDtypeStruct((G_C, 4, S5_CW, S5_PW), BF16)],
        compiler_params=_params("parallel"),
        name="s5_params",
    )(prm, tab)


def _s5_core_kernel(u_ref, r_ref, st_ref, tab_ref, h0_ref, y_ref, hfin_ref, e_scr, hp_scr):
    chains = [(g, d) for g in range(S5_GS) for d in range(2)]
    for g in range(S5_GS):
        a = jnp.dot(u_ref[g].astype(BF16), r_ref[g], preferred_element_type=F32)
        y_ref[g] = a[:, 0:S5_CW]
        for q in range(4):
            e_scr[g, q] = a[:, S5_CW + q * S5_PW:S5_CW + (q + 1) * S5_PW]
    pad = jnp.zeros((1, S5_PW - P_C), F32)
    lam = [tuple(jnp.concatenate([tab_ref[g, d, ri, S5_L:S5_L + 1, :], pad], axis=1) for ri in range(2))
           for g, d in chains]

    def advance(c, hs, latent):
        nc = S5_NC_LAT if latent else S5_NC_CTX
        out = []
        for i, (g, d) in enumerate(chains):
            cc = c if d == 0 else nc - 1 - c
            rows = (pl.ds(S5_CTX_ROWS + cc, DEC_BATCH, stride=nc) if latent
                    else pl.ds(cc, BATCH, stride=nc))
            hr, hi = hs[i]
            er = e_scr[g, 2 * d, rows, :]
            ei = e_scr[g, 2 * d + 1, rows, :]
            hp_scr[g, 2 * d, rows, :] = hr
            hp_scr[g, 2 * d + 1, rows, :] = hi
            lr, li = lam[i]
            out.append((lr * hr - li * hi + er, lr * hi + li * hr + ei))
        return out

    def both(c, carry):
        hc, hl = carry
        return tuple(advance(c, hc, False)), tuple(advance(c, hl, True))

    zeros = jnp.zeros((BATCH, S5_PW), F32)
    h_ctx = tuple((zeros, zeros) for _ in chains)
    h_lat = tuple((h0_ref[g, 2 * d], h0_ref[g, 2 * d + 1]) for g, d in chains)
    h_ctx, h_lat = lax.fori_loop(0, S5_NC_CTX, both, (h_ctx, h_lat))
    lax.fori_loop(S5_NC_CTX, S5_NC_LAT, lambda c, hl: tuple(advance(c, hl, True)), h_lat)
    for i, (g, d) in enumerate(chains):
        hfin_ref[g, 2 * d] = h_ctx[i][0]
        hfin_ref[g, 2 * d + 1] = h_ctx[i][1]
    for g in range(S5_GS):
        y = y_ref[g]
        for q in range(4):
            y = y + lax.dot_general(hp_scr[g, q].astype(BF16), st_ref[g, q], NT, preferred_element_type=F32)
        y_ref[g] = y


def _s5_core(ug, r, st, tab, h0):
    blk = lambda *tail: pl.BlockSpec((S5_GS,) + tail, lambda g: (g,) + (0,) * len(tail))
    return pl.pallas_call(
        _s5_core_kernel,
        grid=(G_C // S5_GS,),
        in_specs=[blk(S5_ROWS, S5_CW), blk(S5_CW, 3 * S5_CW), blk(4, S5_CW, S5_PW),
                  blk(2, 2, S5_TAB, P_C), blk(4, DEC_BATCH, S5_PW)],
        out_specs=[blk(S5_ROWS, S5_CW), blk(4, BATCH, S5_PW)],
        out_shape=[jax.ShapeDtypeStruct((G_C, S5_ROWS, S5_CW), F32),
                   jax.ShapeDtypeStruct((G_C, 4, BATCH, S5_PW), F32)],
        scratch_shapes=[pltpu.VMEM((S5_GS, 4, S5_ROWS, S5_PW), F32),
                        pltpu.VMEM((S5_GS, 4, S5_ROWS, S5_PW), F32)],
        compiler_params=_params("parallel"),
        name="s5_core",
    )(ug, r, st, tab, h0)


def _s5_out_kernel(y_ref, u_ref, d_ref, w_ref, o_ref):
    z = _gelu_tanh(y_ref[...] + d_ref[...] * u_ref[...])
    o_ref[...] = z * _sigmoid(jnp.dot(z.astype(BF16), w_ref[...], preferred_element_type=F32))


def _s5_out(y_tok, p2, d_skip, w_glu_bf16):
    tm = 512
    return pl.pallas_call(
        _s5_out_kernel,
        grid=(M_ALL // tm,),
        in_specs=[pl.BlockSpec((tm, MIX), lambda i: (i, 0)),
                  pl.BlockSpec((tm, MIX), lambda i: (i, 0)),
                  pl.BlockSpec((1, MIX), lambda i: (0, 0)),
                  pl.BlockSpec((MIX, MIX), lambda i: (0, 0))],
        out_specs=pl.BlockSpec((tm, MIX), lambda i: (i, 0)),
        out_shape=jax.ShapeDtypeStruct((M_ALL, MIX), F32),
        compiler_params=_params("parallel"),
        name="s5_out",
    )(y_tok, p2, d_skip.reshape(1, MIX), w_glu_bf16)


def _s5_tables(lam_re, lam_im, log_dt, b_re, b_im, c_re, c_im):
    dt = jnp.exp(log_dt)[:, None, :, None]
    k = jnp.arange(S5_TAB, dtype=F32)[None, :, None, None]
    mag = jnp.exp(lam_re[:, None] * dt * k)
    pw_re = mag * jnp.cos(lam_im[:, None] * dt * k)
    pw_im = mag * jnp.sin(lam_im[:, None] * dt * k)
    ab_re, ab_im = pw_re[:, 1], pw_im[:, 1]
    den = lam_re * lam_re + lam_im * lam_im
    nr = ab_re - 1.0
    f_re = (nr * lam_re + ab_im * lam_im) / den
    f_im = (ab_im * lam_re - nr * lam_im) / den
    rows = lambda b, c, f: jnp.concatenate(
        [jnp.swapaxes(b, -1, -2), c, jnp.broadcast_to(f[:, :, None, :], (2, G_C, 8, P_C))], axis=2)
    prm = jnp.stack([rows(b_re, c_re, f_re), rows(b_im, c_im, f_im)], axis=1)
    tab = jnp.stack([pw_re, pw_im], axis=1)
    return jnp.transpose(prm, (2, 0, 1, 3, 4)), jnp.transpose(tab, (3, 0, 1, 2, 4))


def _s5_mixer(p2, tables, d_skip, w_glu_bf16, state_re, state_im):
    prm, tab = tables
    r, st = _s5_params(prm, tab)
    ug = p2[:, :MIX].reshape(S5_ROWS, S5_L, G_C, S5_GROUP)
    ug = jnp.transpose(ug, (2, 0, 1, 3)).reshape(G_C, S5_ROWS, S5_CW)
    h0 = jnp.stack([state_re, state_im], axis=2)
    h0 = jnp.transpose(h0, (3, 1, 2, 0, 4)).reshape(G_C, 4, DEC_BATCH, P_C)
    y, hfin = _s5_core(ug, r, st, tab, jnp.pad(h0, ((0, 0), (0, 0), (0, 0), (0, S5_PW - P_C))))
    hfin = hfin[..., :P_C]
    y_tok = jnp.transpose(y.reshape(G_C, S5_ROWS, S5_L, S5_GROUP), (1, 2, 0, 3)).reshape(M_ALL, MIX)
    oc = _s5_out(y_tok, p2, d_skip, w_glu_bf16)
    hfin = jnp.transpose(hfin.reshape(G_C, 2, 2, BATCH, P_C), (3, 1, 2, 0, 4))
    return oc, hfin[:, :, 0], hfin[:, :, 1]


def _ret_kernel(*refs, t, latent):
    if latent:
        (q_ref, k_ref, v_ref, g_ref, dl_ref, lnw_ref, lnb_ref, r0_ref, cos_ref, sin_ref, _,
         o_ref, ob_ref) = refs
    else:
        q_ref, k_ref, v_ref, g_ref, dl_ref, lnw_ref, lnb_ref, o_ref, rfin_ref, ob_ref = refs
    n = t // CHUNK
    w = HD_D
    jf = lax.broadcasted_iota(jnp.int32, (CHUNK, CHUNK), 0).astype(F32)
    kf = lax.broadcasted_iota(jnp.int32, (CHUNK, CHUNK), 1).astype(F32)
    diff = jf - kf
    tabs = []
    for dr in range(2):
        for h in range(H_D):
            l = -_softplus(-dl_ref[h, dr:dr + 1, :])
            if dr == 0:
                dmat = jnp.where(diff >= 0, jnp.exp(l * jnp.maximum(diff, 0.0)), 0.0)
                xi = jnp.exp(l * (jf + 1.0))
                zeta = jnp.exp(l * (CHUNK - 1.0 - jf))
            else:
                dmat = jnp.where(diff < 0, jnp.exp(l * jnp.maximum(-diff, 0.0)), 0.0)
                xi = jnp.exp(l * (CHUNK - jf))
                zeta = jnp.exp(l * jf)
            tabs.append((dmat, xi, zeta, jnp.exp(l * CHUNK)))
    chains = [(dr, h) for dr in range(2) for h in range(H_D)]

    def body(ci, states):
        q, k, v, where = [], [], [], []
        for dr, h in chains:
            cj = ci if dr == 0 else n - 1 - ci
            rows = pl.ds(pl.multiple_of(cj * CHUNK, CHUNK), CHUNK)
            cols = slice(h * w, (h + 1) * w)
            qi = q_ref[rows, cols]
            ki = k_ref[rows, cols] * (HD_D ** -0.5)
            if latent:
                qi = _rope_pairs(qi, cos_ref[rows, :], sin_ref[rows, :], HD_D // 2)
                ki = _rope_pairs(ki, cos_ref[rows, :], sin_ref[rows, :], HD_D // 2)
            q.append(qi.astype(BF16))
            k.append(ki)
            v.append(v_ref[rows, cols].astype(BF16))
            where.append((o_ref if dr == 0 else ob_ref, rows, cols))
        nch = range(len(chains))
        inner = [(_bdot(q[i], k[i], NT) * tabs[i][0]).astype(BF16) for i in nch]
        cross = [_bdot(q[i], states[i]) * tabs[i][1] for i in nch]
        kz = [(k[i] * tabs[i][2]).astype(BF16) for i in nch]
        out = [jnp.dot(inner[i], v[i], preferred_element_type=F32) + cross[i] for i in nch]
        new_states = [states[i] * tabs[i][3] + lax.dot_general(kz[i], v[i], TN, preferred_element_type=F32)
                      for i in nch]
        for (dst, rows, cols), y in zip(where, out):
            dst[rows, cols] = y
        return tuple(new_states)

    if latent:
        lax.fori_loop(0, n, body, tuple(r0_ref[0, dr, h] for dr, h in chains))
    else:
        fin = lax.fori_loop(0, n, body, tuple(jnp.zeros((w, w), F32) for _ in chains))
        for i, (dr, h) in enumerate(chains):
            rfin_ref[0, dr, h] = fin[i]
    for h in range(H_D):
        cols = slice(h * w, (h + 1) * w)
        y = o_ref[:, cols] + ob_ref[:, cols]
        yc = y - jnp.mean(y, axis=-1, keepdims=True)
        yn = yc * lax.rsqrt(jnp.mean(yc * yc, axis=-1, keepdims=True) + RET_GN_EPS)
        o_ref[:, cols] = (yn * lnw_ref[:, cols] + lnb_ref[:, cols]) * _silu(g_ref[:, cols])


def _retention(p, decay_logit, ln_w, ln_b, latent, r0=None, rope=None, dst=None):
    if latent:
        nseq, t, rb0 = DEC_BATCH, DEC_SEQ, M_CTX // DEC_SEQ
    else:
        nseq, t, rb0 = BATCH, SEQ, 0
    w = HD_D
    dl = jnp.broadcast_to(decay_logit.T[:, :, None], (H_D, 2, w))
    full = lambda shape: pl.BlockSpec(shape, lambda b: (0,) * len(shape))
    st_spec = pl.BlockSpec((1, 2, H_D, w, w), lambda b: (b, 0, 0, 0, 0))
    in_specs = [pl.BlockSpec((t, MIX), lambda b, j=j: (rb0 + b, j)) for j in (1, 2, 3, 4)]
    in_specs += [full((H_D, 2, w)), full((1, MIX)), full((1, MIX))]
    args = [p, p, p, p, dl, ln_w.reshape(1, MIX), ln_b.reshape(1, MIX)]
    o_spec = pl.BlockSpec((t, MIX), lambda b: (rb0 + b, 0))
    o_shape = jax.ShapeDtypeStruct((M_ALL, MIX), F32)
    if latent:
        in_specs += [st_spec, full((t, w)), full((t, w)), pl.BlockSpec(memory_space=pl.ANY)]
        args += [r0, *rope, dst]
        out_specs, out_shape, aliases = o_spec, o_shape, {len(args) - 1: 0}
    else:
        out_specs = [o_spec, st_spec]
        out_shape = [o_shape, jax.ShapeDtypeStruct((nseq, 2, H_D, w, w), F32)]
        aliases = {}
    return pl.pallas_call(
        functools.partial(_ret_kernel, t=t, latent=latent),
        grid=(nseq,),
        in_specs=in_specs, out_specs=out_specs, out_shape=out_shape,
        scratch_shapes=[pltpu.VMEM((t, MIX), F32)],
        input_output_aliases=aliases,
        compiler_params=_params("parallel"),
        name="retention_lat" if latent else "retention_ctx",
    )(*args)


def _rope_tables(n_tok, dim, reps):
    rows = n_tok // GRID_W
    n_freq = dim // 4
    inv = 1.0 / (ROPE_THETA ** (jnp.arange(n_freq, dtype=F32) / n_freq))
    row = jnp.repeat(jnp.arange(rows, dtype=F32), GRID_W)
    col = jnp.tile(jnp.arange(GRID_W, dtype=F32), rows)
    ang = jnp.concatenate([row[:, None] * inv, col[:, None] * inv], axis=-1)
    cos, sin = jnp.cos(ang), jnp.sin(ang)
    return jnp.tile(jnp.concatenate([cos, cos], axis=1), (1, reps)), \
        jnp.tile(jnp.concatenate([-sin, sin], axis=1), (1, reps))


def kernel(x_prompt, x_sample, cache_k_ab, cache_v_ab, state_rwkv, state_s5_re, state_s5_im, state_ret, c, c_ctx, norm1_g, norm2_g, w_mod, b_mod, w_ff_gate, w_ff_up, w_ff_down, w_in_ab, w_out_ab, qk_gain_a, lambda_qk, subln_g, rwkv_mu, rwkv_k_k, rwkv_k_a, rwkv_r_k, rwkv_w0, rwkv_w_up, rwkv_a0, rwkv_a_up, rwkv_g_up, rwkv_ln_w, rwkv_ln_b, w_in_cd, w_out_cd, s5_lam_re, s5_lam_im, s5_log_dt, s5_b_re, s5_b_im, s5_c_re, s5_c_im, s5_d, s5_w_glu, ret_decay_logit, ret_ln_w, ret_ln_b):
    d = D_MODEL
    xs = [x_prompt.reshape(M_CTX, d), x_sample.reshape(M_LAT, d)]
    cvec = jnp.zeros((MOD_ROWS, d), F32).at[0].set(c_ctx).at[1:1 + DEC_BATCH].set(c)
    mods = _modulation(cvec, w_mod, b_mod)

    lam_init = 0.8 - 0.6 * math.exp(-0.3 * 0)
    p = _norm_linear(xs, norm1_g[0], mods[0], w_in_ab[0].astype(BF16))
    gain2 = jnp.tile(qk_gain_a[0], (1, 2))
    rope_a = _rope_tables(DEC_SEQ, HD_A, 2)
    ck = cache_k_ab[:, 0].reshape(DEC_BATCH * PAST_LEN, MIX)
    cv = cache_v_ab[:, 0].reshape(DEC_BATCH * PAST_LEN, MIX)
    oa, k_ctx = _attention(p, gain2, lambda_qk[0], subln_g[0], lam_init, latent=False)
    oa = _attention(p, gain2, lambda_qk[0], subln_g[0], lam_init, latent=True,
                    cache_k=ck, cache_v=cv, rope=rope_a, dst=oa)
    rw_prm = (rwkv_mu[0], rwkv_k_k[0], rwkv_k_a[0], rwkv_r_k[0], rwkv_w0[0], rwkv_w_up[0],
              rwkv_a0[0], rwkv_a_up[0], rwkv_g_up[0])
    s0_lat = jnp.swapaxes(state_rwkv[:, 0], -1, -2)
    ob, sfin_ctx = _rwkv_scan(_rwkv_prep(p, rw_prm, latent=False), rwkv_ln_w[0], rwkv_ln_b[0], latent=False)
    ob = _rwkv_scan(_rwkv_prep(p, rw_prm, latent=True), rwkv_ln_w[0], rwkv_ln_b[0], latent=True,
                    s0t=s0_lat, dst=ob)
    x = _mix_ffn(xs, oa, ob, w_out_ab[0].astype(BF16), norm2_g[0], mods[0], w_ff_gate[0].astype(BF16),
                 w_ff_up[0].astype(BF16), w_ff_down[0].astype(BF16), split_out=False)

    p2 = _norm_linear([x], norm1_g[1], mods[1], w_in_cd[0].astype(BF16))
    tables = _s5_tables(s5_lam_re[0], s5_lam_im[0], s5_log_dt[0], s5_b_re[0], s5_b_im[0],
                        s5_c_re[0], s5_c_im[0])
    oc, s5_fin_re, s5_fin_im = _s5_mixer(p2, tables, s5_d[0], s5_w_glu[0].astype(BF16),
                                         state_s5_re[:, 0], state_s5_im[:, 0])
    rope_d = _rope_tables(DEC_SEQ, HD_D, 1)
    od, rfin = _retention(p2, ret_decay_logit[0], ret_ln_w[0], ret_ln_b[0], latent=False)
    od = _retention(p2, ret_decay_logit[0], ret_ln_w[0], ret_ln_b[0], latent=True, r0=state_ret[:, 0],
                    rope=rope_d, dst=od)
    y_ctx, y_lat = _mix_ffn([x], oc, od, w_out_cd[0].astype(BF16), norm2_g[1], mods[1],
                            w_ff_gate[1].astype(BF16), w_ff_up[1].astype(BF16), w_ff_down[1].astype(BF16),
                            split_out=True)
    y_prompt = y_ctx.reshape(BATCH, SEQ, d)
    y_sample = y_lat.reshape(DEC_BATCH, DEC_SEQ, d)
    new_k = k_ctx.reshape(BATCH, 1, SEQ, H_A, 2, HD_A)
    new_v = p[:M_CTX, 2 * MIX:3 * MIX].reshape(BATCH, 1, SEQ, H_A, VD_A)
    new_rwkv = sfin_ctx[:, None]
    new_s5_re = s5_fin_re[:, None]
    new_s5_im = s5_fin_im[:, None]
    new_ret = rfin[:, None]
    return (y_prompt, y_sample, new_k, new_v, new_rwkv, new_s5_re, new_s5_im, new_ret)
```

```python
import functools
import math

import numpy as np
import jax
import jax.numpy as jnp
from jax import lax
from jax.experimental import pallas as pl
from jax.experimental.pallas import tpu as pltpu

F32 = jnp.float32
BF16 = jnp.bfloat16
HIGHEST = lax.Precision.HIGHEST

D_MODEL = 1024
BATCH = 32
SEQ = 256
DEC_BATCH = 2
DEC_SEQ = 1024
PAST_LEN = 256
GRID_W = 64
H_A = 4
HD_A = 64
VD_A = 128
H_B = 8
HD_B = 64
MIX = 512
LORA_W = 64
LORA_A = 64
LORA_G = 128
S5_GROUP = 16
G_C = 32
P_C = 64
S5_STATE = G_C * P_C
H_D = 4
HD_D = 128
CHUNK = 128
D_FF = 2816
IN_AB = 3328
IN_CD = 2560
ROPE_THETA = 10000.0
NORM_EPS = 1e-6
RWKV_GN_EPS = 64e-5
RET_GN_EPS = 1e-5

M_CTX = BATCH * SEQ
M_LAT = DEC_BATCH * DEC_SEQ
M_ALL = M_CTX + M_LAT
MOD_ROWS = 8
RWKV_CHUNK = 64
VMEM_LIMIT = 56 * 1024 * 1024

NN = (((1,), (0,)), ((), ()))
NT = (((1,), (1,)), ((), ()))
TN = (((0,), (0,)), ((), ()))


def _params(*sem):
    return pltpu.CompilerParams(dimension_semantics=sem, vmem_limit_bytes=VMEM_LIMIT)


def _bdot(a, b, dims=NN):
    return lax.dot_general(a.astype(BF16), b.astype(BF16), dims, preferred_element_type=F32)


def _hdot(a, b, dims=NN):
    return lax.dot_general(a, b, dims, precision=HIGHEST, preferred_element_type=F32)


def _split_dot(x, m):
    hi = x.astype(BF16)
    lo = (x - hi.astype(F32)).astype(BF16)
    return (jnp.dot(hi, m, preferred_element_type=F32) + jnp.dot(lo, m, preferred_element_type=F32))


def _seg_matrix(n, shift, val):
    r = lax.broadcasted_iota(jnp.int32, (n, n), 0) >> shift
    c = lax.broadcasted_iota(jnp.int32, (n, n), 1) >> shift
    return jnp.where(r == c, val, 0.0).astype(BF16)


def _sigmoid(x):
    return jax.nn.sigmoid(x)


def _silu(x):
    return x * jax.nn.sigmoid(x)


def _softplus(x):
    return jnp.maximum(x, 0.0) + jnp.log(1.0 + jnp.exp(-jnp.abs(x)))


def _mod_row(tile, tm):
    r0 = tile * tm
    return jnp.where(r0 < M_CTX, 0, 1 + (r0 - M_CTX) // DEC_SEQ)


def _norm_mod(x, g, sc_ref, sh_ref, row):
    y = x * lax.rsqrt(jnp.mean(x * x, axis=-1, keepdims=True) + NORM_EPS) * g
    return y * (1.0 + sc_ref[pl.ds(row, 1), :]) + sh_ref[pl.ds(row, 1), :]


def _mod_kernel(c_ref, w_ref, b_ref, o_ref):
    o_ref[0] = _hdot(_silu(c_ref[...]), w_ref[0]) + b_ref[0]


def _modulation(cvec, w_mod, b_mod):
    depth, d, n6 = w_mod.shape
    tn = 1536
    return pl.pallas_call(
        _mod_kernel,
        grid=(depth, n6 // tn),
        in_specs=[pl.BlockSpec((MOD_ROWS, d), lambda l, j: (0, 0)),
                  pl.BlockSpec((1, d, tn), lambda l, j: (l, 0, j)),
                  pl.BlockSpec((1, 1, tn), lambda l, j: (l, 0, j))],
        out_specs=pl.BlockSpec((1, MOD_ROWS, tn), lambda l, j: (l, 0, j)),
        out_shape=jax.ShapeDtypeStruct((depth, MOD_ROWS, n6), F32),
        compiler_params=_params("parallel", "parallel"),
        name="modulation",
    )(cvec, w_mod, b_mod.reshape(depth, 1, n6))


def _tile_specs(tm, d):
    nc = M_CTX // tm
    return [pl.BlockSpec((tm, d), lambda i: (jnp.minimum(i, nc - 1), 0)),
            pl.BlockSpec((tm, d), lambda i: (jnp.maximum(i - nc, 0), 0))]


def _pick_rows(refs, tm):
    if len(refs) == 1:
        return refs[0][...]
    return jnp.where(pl.program_id(0) < M_CTX // tm, refs[0][...], refs[1][...])


def _norm_linear_kernel(*refs, tm):
    *x_refs, g_ref, sc_ref, sh_ref, w_ref, o_ref = refs
    row = _mod_row(pl.program_id(0), tm)
    h = _norm_mod(_pick_rows(x_refs, tm), g_ref[...], sc_ref, sh_ref, row)
    o_ref[...] = jnp.dot(h.astype(BF16), w_ref[...], preferred_element_type=F32)


def _norm_linear(xs, g, mods, w_bf16):
    tm = 512
    d = D_MODEL
    n = w_bf16.shape[1]
    x_specs = _tile_specs(tm, d) if len(xs) == 2 else [pl.BlockSpec((tm, d), lambda i: (i, 0))]
    return pl.pallas_call(
        functools.partial(_norm_linear_kernel, tm=tm),
        grid=(M_ALL // tm,),
        in_specs=x_specs + [pl.BlockSpec((1, d), lambda i: (0, 0)),
                            pl.BlockSpec((MOD_ROWS, d), lambda i: (0, 1)),
                            pl.BlockSpec((MOD_ROWS, d), lambda i: (0, 0)),
                            pl.BlockSpec((d, n), lambda i: (0, 0))],
        out_specs=pl.BlockSpec((tm, n), lambda i: (i, 0)),
        out_shape=jax.ShapeDtypeStruct((M_ALL, n), F32),
        compiler_params=_params("arbitrary"),
        name="norm_linear",
    )(*xs, g.reshape(1, d), mods, mods, w_bf16)


def _mix_ffn_kernel(*refs, tm, ck, n_in, n_out):
    x_refs = refs[:n_in]
    (a_ref, b_ref, wa_ref, wb_ref, g1_ref, ng_ref, sc_ref, sh_ref, g2_ref,
     wg_ref, wu_ref, wd_ref) = refs[n_in:n_in + 12]
    o_refs = refs[n_in + 12:]
    row = _mod_row(pl.program_id(0), tm)
    mix = (jnp.dot(a_ref[...].astype(BF16), wa_ref[...], preferred_element_type=F32)
           + jnp.dot(b_ref[...].astype(BF16), wb_ref[...], preferred_element_type=F32))
    x = _pick_rows(x_refs, tm) + g1_ref[pl.ds(row, 1), :] * mix
    h = _norm_mod(x, ng_ref[...], sc_ref, sh_ref, row).astype(BF16)
    acc = jnp.zeros((tm, D_MODEL), F32)
    for c in range(D_FF // ck):
        gg = jnp.dot(h, wg_ref[:, c * ck:(c + 1) * ck], preferred_element_type=F32)
        uu = jnp.dot(h, wu_ref[:, c * ck:(c + 1) * ck], preferred_element_type=F32)
        act = (_silu(gg) * uu).astype(BF16)
        acc = acc + jnp.dot(act, wd_ref[c * ck:(c + 1) * ck, :], preferred_element_type=F32)
    y = x + g2_ref[pl.ds(row, 1), :] * acc
    if n_out == 1:
        o_refs[0][...] = y
    else:
        is_ctx = pl.program_id(0) < M_CTX // tm

        @pl.when(is_ctx)
        def _():
            o_refs[0][...] = y

        @pl.when(jnp.logical_not(is_ctx))
        def _():
            o_refs[1][...] = y


def _mix_ffn(xs, oa, ob, w_out_bf16, norm_g, mods, wg, wu, wd, split_out):
    tm, ck = 512, 256
    d = D_MODEL
    row_spec = lambda n: pl.BlockSpec((tm, n), lambda i: (i, 0))
    const = lambda shape, idx: pl.BlockSpec(shape, lambda i: idx, pipeline_mode=pl.Buffered(1))
    mod = lambda j: pl.BlockSpec((MOD_ROWS, d), lambda i: (0, j))
    x_specs = _tile_specs(tm, d) if len(xs) == 2 else [row_spec(d)]
    if split_out:
        out_specs = _tile_specs(tm, d)
        out_shape = [jax.ShapeDtypeStruct((M_CTX, d), F32), jax.ShapeDtypeStruct((M_LAT, d), F32)]
    else:
        out_specs, out_shape = row_spec(d), jax.ShapeDtypeStruct((M_ALL, d), F32)
    return pl.pallas_call(
        functools.partial(_mix_ffn_kernel, tm=tm, ck=ck, n_in=len(xs), n_out=2 if split_out else 1),
        grid=(M_ALL // tm,),
        in_specs=x_specs + [row_spec(MIX), row_spec(MIX),
                            const((MIX, d), (0, 0)), const((MIX, d), (1, 0)),
                            mod(2),
                            pl.BlockSpec((1, d), lambda i: (0, 0)),
                            mod(4), mod(3), mod(5),
                            const((d, D_FF), (0, 0)), const((d, D_FF), (0, 0)), const((D_FF, d), (0, 0))],
        out_specs=out_specs, out_shape=out_shape,
        compiler_params=_params("arbitrary"),
        name="mix_ffn",
    )(*xs, oa, ob, w_out_bf16, w_out_bf16, mods, norm_g.reshape(1, d), mods, mods, mods, wg, wu, wd)


def _qk_norm(x, gain, segm):
    ms = _split_dot(x * x, segm)
    return x * lax.rsqrt(ms + NORM_EPS) * gain


def _rope_pairs(x, cosf, sinf, half):
    lane = lax.broadcasted_iota(jnp.int32, x.shape, 1)
    first = (lane & (2 * half - 1)) < half
    n = x.shape[1]
    partner = jnp.where(first, pltpu.roll(x, n - half, axis=1), pltpu.roll(x, half, axis=1))
    return x * cosf + partner * sinf


def _attn_kernel(*refs, latent, lam_init):
    if latent:
        (q_ref, k_ref, v_ref, ck_ref, cv_ref, cosq_ref, sinq_ref, cosk_ref, sink_ref,
         gain_ref, lam_ref, sub_ref, _, o_ref, kall, vall) = refs
    else:
        q_ref, k_ref, v_ref, gain_ref, lam_ref, sub_ref, o_ref, kn_ref, kall, vall = refs
    w = 2 * HD_A
    segm = _seg_matrix(w, 6, 1.0 / HD_A)
    gains = gain_ref[...]
    cols = [slice(h * w, (h + 1) * w) for h in range(H_A)]

    @pl.when(pl.program_id(1) == 0)
    def _():
        k = [_qk_norm(k_ref[:, c], gains[1:2], segm) for c in cols]
        if latent:
            k = [_rope_pairs(x, cosk_ref[...], sink_ref[...], HD_A // 2) for x in k]
            for c, x in zip(cols, k):
                kall[0:PAST_LEN, c] = ck_ref[:, c].astype(BF16)
                kall[PAST_LEN:, c] = x.astype(BF16)
                vall[0:PAST_LEN, c] = cv_ref[:, c].astype(BF16)
                vall[PAST_LEN:, c] = v_ref[:, c].astype(BF16)
        else:
            for c, x in zip(cols, k):
                kn_ref[:, c] = x
                kall[:, c] = x.astype(BF16)
                vall[:, c] = v_ref[:, c].astype(BF16)

    q = [_qk_norm(q_ref[:, c], gains[0:1], segm) for c in cols]
    if latent:
        q = [_rope_pairs(x, cosq_ref[...], sinq_ref[...], HD_A // 2) for x in q]
    lv = lam_ref[...]
    lam = (jnp.exp(jnp.sum(lv[0:1] * lv[1:2], axis=1, keepdims=True))
           - jnp.exp(jnp.sum(lv[2:3] * lv[3:4], axis=1, keepdims=True)) + lam_init)
    scale = HD_A ** -0.5
    comp0 = lax.broadcasted_iota(jnp.int32, q[0].shape, 1) < HD_A
    qc = [jnp.where(comp0, *sel).astype(BF16) for x in q for sel in ((x, 0.0), (0.0, x))]
    s = [lax.dot_general(qc[i], kall[:, cols[i // 2]], NT, preferred_element_type=F32) * scale
         for i in range(2 * H_A)]
    e = [jnp.exp(x - jnp.max(x, axis=-1, keepdims=True)) for x in s]
    p = [x / jnp.sum(x, axis=-1, keepdims=True) for x in e]
    att = [(p[2 * h] - lam * p[2 * h + 1]).astype(BF16) for h in range(H_A)]
    o = [jnp.dot(att[h], vall[:, cols[h]], preferred_element_type=F32) for h in range(H_A)]
    o = [x * lax.rsqrt(jnp.mean(x * x, axis=-1, keepdims=True) + NORM_EPS) * sub_ref[...] for x in o]
    for c, x in zip(cols, o):
        o_ref[:, c] = x * (1.0 - lam_init)


def _attention(p, gain2, lambda_qk, subln_g, lam_init, latent, cache_k=None, cache_v=None, rope=None,
               dst=None):
    w = 2 * HD_A
    if latent:
        nseq, t, tq, rb0, s_len = DEC_BATCH, DEC_SEQ, 128, M_CTX // DEC_SEQ, PAST_LEN + DEC_SEQ
    else:
        nseq, t, tq, rb0, s_len = BATCH, SEQ, SEQ, 0, SEQ
    nq = t // tq
    qoff = rb0 * nq
    full = lambda shape: pl.BlockSpec(shape, lambda b, i: (0,) * len(shape))
    in_specs = [pl.BlockSpec((tq, MIX), lambda b, i: (qoff + b * nq + i, 0)),
                pl.BlockSpec((t, MIX), lambda b, i: (rb0 + b, 1)),
                pl.BlockSpec((t, MIX), lambda b, i: (rb0 + b, 2))]
    args = [p, p, p]
    if latent:
        cosf, sinf = rope
        in_specs += [pl.BlockSpec((PAST_LEN, MIX), lambda b, i: (b, 0)),
                     pl.BlockSpec((PAST_LEN, MIX), lambda b, i: (b, 0)),
                     pl.BlockSpec((tq, w), lambda b, i: (i, 0)),
                     pl.BlockSpec((tq, w), lambda b, i: (i, 0)),
                     full((t, w)), full((t, w))]
        args += [cache_k, cache_v, cosf, sinf, cosf, sinf]
    in_specs += [full((2, w)), full((4, HD_A)), full((1, w))]
    args += [gain2, lambda_qk, subln_g.reshape(1, w)]
    o_spec = pl.BlockSpec((tq, MIX), lambda b, i: (qoff + b * nq + i, 0))
    o_shape = jax.ShapeDtypeStruct((M_ALL, MIX), F32)
    aliases = {}
    if latent:
        out_specs, out_shape = o_spec, o_shape
        in_specs.append(pl.BlockSpec(memory_space=pl.ANY))
        args.append(dst)
        aliases = {len(args) - 1: 0}
    else:
        out_specs = [o_spec, pl.BlockSpec((t, MIX), lambda b, i: (b, 0))]
        out_shape = [o_shape, jax.ShapeDtypeStruct((nseq * t, MIX), F32)]
    return pl.pallas_call(
        functools.partial(_attn_kernel, latent=latent, lam_init=lam_init),
        grid=(nseq, nq),
        in_specs=in_specs, out_specs=out_specs, out_shape=out_shape,
        scratch_shapes=[pltpu.VMEM((s_len, MIX), BF16), pltpu.VMEM((s_len, MIX), BF16)],
        input_output_aliases=aliases,
        compiler_params=_params("parallel", "arbitrary"),
        name="diff_attention_lat" if latent else "diff_attention_ctx",
    )(*args)


def _centred_shift(x, mu):
    t = x.shape[0]
    row = lax.broadcasted_iota(jnp.int32, x.shape, 0)
    prev = jnp.where(row == 0, 0.0, pltpu.roll(x, 1, axis=0))
    nxt = jnp.where(row == t - 1, 0.0, pltpu.roll(x, t - 1, axis=0))
    return x + (0.5 * (prev + nxt) - x) * mu


def _seg_sum(x, segm):
    return jnp.concatenate([_split_dot(x[:, j * 128:(j + 1) * 128], segm) for j in range(x.shape[1] // 128)],
                           axis=1)


def _rwkv_prep_kernel(r_ref, k_ref, v_ref, l_ref, mur_ref, muk_ref, muv_ref, mul_ref,
                      kk_ref, ka_ref, rk_ref, w0_ref, wup_ref, a0_ref, aup_ref, gup_ref,
                      ro_ref, ldf_ref, ldb_ref, kbo_ref, vbo_ref, kko_ref, ao_ref, gate_ref, bonus_ref):
    seg1 = _seg_matrix(128, 6, 1.0)
    r = _centred_shift(r_ref[...], mur_ref[...])
    kb = _centred_shift(k_ref[...], muk_ref[...])
    vb = _centred_shift(v_ref[...], muv_ref[...])
    lo = _centred_shift(l_ref[...], mul_ref[...])
    xw = lo[:, 0:LORA_W]
    xa = lo[:, LORA_W:LORA_W + LORA_A]
    xg = lo[:, LORA_W + LORA_A:]
    kk = kb * kk_ref[...]
    kk = kk * lax.rsqrt(_seg_sum(kk * kk, seg1) + 1e-12)
    a = _sigmoid(a0_ref[...] + _bdot(xa, aup_ref[...]))
    kb2 = kb * (1.0 + (a - 1.0) * ka_ref[...])
    lw = jnp.tanh(xw)
    for dr, ld_ref in enumerate((ldf_ref, ldb_ref)):
        z = w0_ref[dr:dr + 1, :] + _bdot(lw, wup_ref[dr])
        logw = -_softplus(-z) - 0.5
        ld_ref[...] = -jnp.exp(logw)
    gate_ref[...] = _bdot(_sigmoid(xg), gup_ref[...])
    bonus_ref[...] = _seg_sum(r * kb2 * rk_ref[...], seg1) * vb
    ro_ref[...] = r
    kbo_ref[...] = kb2
    vbo_ref[...] = vb
    kko_ref[...] = kk
    ao_ref[...] = a


def _rwkv_prep(p, prm, latent):
    mu, k_k, k_a, r_k, w0, w_up, a0, a_up, g_up = prm
    if latent:
        nseq, t, rb0 = DEC_BATCH, DEC_SEQ, M_CTX // DEC_SEQ
    else:
        nseq, t, rb0 = BATCH, SEQ, 0
    lw = LORA_W + LORA_A + LORA_G
    c0 = (IN_AB - 3 * MIX - lw) // MIX
    cl = (IN_AB - lw) // lw
    row = lambda n: pl.BlockSpec((1, n), lambda b: (0, 0))
    in_specs = [pl.BlockSpec((t, MIX), lambda b: (rb0 + b, c0)),
                pl.BlockSpec((t, MIX), lambda b: (rb0 + b, c0 + 1)),
                pl.BlockSpec((t, MIX), lambda b: (rb0 + b, c0 + 2)),
                pl.BlockSpec((t, lw), lambda b: (rb0 + b, cl)),
                row(MIX), row(MIX), row(MIX), row(lw),
                row(MIX), row(MIX), row(MIX),
                pl.BlockSpec((2, MIX), lambda b: (0, 0)),
                pl.BlockSpec((2, LORA_W, MIX), lambda b: (0, 0, 0)),
                row(MIX),
                pl.BlockSpec((LORA_A, MIX), lambda b: (0, 0)),
                pl.BlockSpec((LORA_G, MIX), lambda b: (0, 0))]
    o_spec = pl.BlockSpec((t, MIX), lambda b: (b, 0))
    o_shape = jax.ShapeDtypeStruct((nseq * t, MIX), F32)
    return pl.pallas_call(
        _rwkv_prep_kernel,
        grid=(nseq,),
        in_specs=in_specs, out_specs=[o_spec] * 9, out_shape=[o_shape] * 9,
        compiler_params=_params("parallel"),
        name="rwkv_prep_lat" if latent else "rwkv_prep_ctx",
    )(p, p, p, p,
      mu[None, 0:MIX], mu[None, MIX:2 * MIX], mu[None, 2 * MIX:3 * MIX], mu[None, 3 * MIX:],
      k_k.reshape(1, MIX), k_a.reshape(1, MIX), r_k.reshape(1, MIX), w0, w_up,
      a0.reshape(1, MIX), a_up, g_up)


def _split2(x):
    hi = x.astype(BF16)
    return hi, (x - hi.astype(F32)).astype(BF16)


RWKV_INV_BLOCK = 16


def _rwkv_masks(c, rev):
    ti = lax.broadcasted_iota(jnp.int32, (c, c), 0)
    si = lax.broadcasted_iota(jnp.int32, (c, c), 1)
    tri = jnp.where((si >= ti) if rev else (si <= ti), 1.0, 0.0).astype(BF16)
    t4 = lax.broadcasted_iota(jnp.int32, (4 * c, 4 * c), 0)
    s4 = lax.broadcasted_iota(jnp.int32, (4 * c, 4 * c), 1)
    tm, sm = t4 & (c - 1), s4 & (c - 1)
    strict = (sm > tm) if rev else (sm < tm)
    incl = (sm >= tm) if rev else (sm <= tm)
    same_head = ((t4 // c) & 1) == ((s4 // c) & 1)
    top = t4 < 2 * c
    gmask = same_head & ((top & strict) | (~top & incl))
    t2 = lax.broadcasted_iota(jnp.int32, (2 * c, 2 * c), 0)
    s2 = lax.broadcasted_iota(jnp.int32, (2 * c, 2 * c), 1)
    same = lambda n: (t2 // n) == (s2 // n)
    levels = []
    n = RWKV_INV_BLOCK
    while n < c:
        levels.append(same(2 * n) & ~same(n))
        n *= 2
    f = lambda m: jnp.where(m, 1.0, 0.0)
    return tri, f(gmask), f(same(RWKV_INV_BLOCK)), tuple(f(m) for m in levels), f(same(c)), f(t2 == s2)


def _keep(mask01, x):
    return jnp.where(mask01 > 0.5, x, 0.0)


def _tri_inverse(a, diag_blk, levels, eye):
    n = a[0].shape[0]
    d = [_keep(diag_blk, x) for x in a]
    t = [eye + x for x in d]
    p = [_bdot(x, x) for x in d]
    for _ in range(int(math.log2(RWKV_INV_BLOCK)) - 2):
        res = [_bdot(jnp.concatenate([pi, ti], axis=0), pi) for pi, ti in zip(p, t)]
        p = [x[:n] for x in res]
        t = [ti + x[n:] for ti, x in zip(t, res)]
    t = [ti + _bdot(ti, pi) for ti, pi in zip(t, p)]
    for off in levels:
        half = [_bdot(ti, _keep(off, x)) for ti, x in zip(t, a)]
        t = [ti + _bdot(x, ti) for ti, x in zip(t, half)]
    return t


def _rwkv_pair_chunks(ins, sts, tris, gmasks, diag_blk, levels, same_head, eye):
    c, w = ins[0][0].shape
    hd = w // 2
    nch = range(len(ins))
    r, ld, kb, vb, kk, a = (list(z) for z in zip(*ins))
    split = [_split2(x) for x in ld]
    lcum = [jnp.dot(tris[i], split[i][0], preferred_element_type=F32)
            + jnp.dot(tris[i], split[i][1], preferred_element_type=F32) for i in nch]
    ones = jnp.ones((c, w), BF16)
    pcol = [jnp.exp(lax.dot_general(hi, ones, TN, preferred_element_type=F32)
                    + lax.dot_general(lo, ones, TN, preferred_element_type=F32)) for hi, lo in split]
    ltot = [jnp.sum(x, axis=0, keepdims=True) for x in ld]
    beta = [kk[i] * a[i] for i in nch]
    eneg = [jnp.exp(-x) for x in lcum]
    abar = [-kk[i] * jnp.exp(lcum[i] - ld[i]) for i in nch]
    rbar = [r[i] * jnp.exp(lcum[i]) for i in nch]
    bt = [(beta[i] * eneg[i]).astype(BF16) for i in nch]
    kt = [(kb[i] * eneg[i]).astype(BF16) for i in nch]
    vbb = [x.astype(BF16) for x in vb]
    head0 = lax.broadcasted_iota(jnp.int32, (c, w), 1) < hd
    pick = lambda res: jnp.where(head0, res[:c], res[c:])
    arst = [_bdot(jnp.concatenate([abar[i], rbar[i]], axis=0), sts[i]) for i in nch]
    lhs = [jnp.concatenate([jnp.where(head0, abar[i], 0.0), jnp.where(head0, 0.0, abar[i]),
                            jnp.where(head0, rbar[i], 0.0), jnp.where(head0, 0.0, rbar[i])], axis=0) for i in nch]
    g = [_keep(gmasks[i], _bdot(lhs[i], jnp.concatenate([bt[i], bt[i], kt[i], kt[i]], axis=0), NT))
         for i in nch]
    x = [arst[i][:c] + pick(_bdot(g[i][:2 * c, 2 * c:], jnp.concatenate([vbb[i], vbb[i]], axis=0))) for i in nch]
    tinv = _tri_inverse([gi[:2 * c, :2 * c] for gi in g], diag_blk, levels, eye)
    u = [pick(_bdot(tinv[i], jnp.concatenate([x[i], x[i]], axis=0))) for i in nch]
    ub = [z.astype(BF16) for z in u]
    y = [arst[i][c:] + pick(_bdot(g[i][2 * c:], jnp.concatenate([ub[i], ub[i], vbb[i], vbb[i]], axis=0)))
         for i in nch]
    erem = [jnp.exp(ltot[i] - lcum[i]) for i in nch]
    bkh = [jnp.concatenate([beta[i] * erem[i], kb[i] * erem[i]], axis=0) for i in nch]
    st_new = [pcol[i] * sts[i] + _keep(same_head, _bdot(bkh[i], jnp.concatenate([ub[i], vbb[i]], axis=0), TN))
              for i in nch]
    return y, st_new


def _rwkv_scan_kernel(*refs, t, c, npair, latent):
    r_ref, ldf_ref, ldb_ref, kb_ref, vb_ref, kk_ref, a_ref, bonus_ref, gate_ref, lnw_ref, lnb_ref = refs[:11]
    if latent:
        s0_ref, _, o_ref, yb_ref = refs[11:]
    else:
        o_ref, sfin_ref, yb_ref = refs[11:]
    n = t // c
    w = 2 * HD_B
    masks = [_rwkv_masks(c, rev) for rev in (False, True)]

    def body(ci, states):
        ins, tris, gmasks, dsts = [], [], [], []
        for dr, (ld_ref, dst) in enumerate(((ldf_ref, o_ref), (ldb_ref, yb_ref))):
            cj = ci if dr == 0 else n - 1 - ci
            rows = pl.ds(pl.multiple_of(cj * c, c), c)
            for p in range(npair):
                cols = slice(p * w, (p + 1) * w)
                ins.append(tuple(ref[rows, cols] for ref in (r_ref, ld_ref, kb_ref, vb_ref, kk_ref, a_ref)))
                tris.append(masks[dr][0])
                gmasks.append(masks[dr][1])
                dsts.append((dst, rows, cols))
        ys, new_states = _rwkv_pair_chunks(ins, list(states), tris, gmasks, *masks[0][2:])
        for (dst, rows, cols), y in zip(dsts, ys):
            dst[rows, cols] = y
        return tuple(new_states)

    if latent:
        zero = jnp.zeros((HD_B, HD_B), F32)
        init = tuple(jnp.concatenate([jnp.concatenate([s0_ref[0, dr, 2 * p], zero], axis=1),
                                      jnp.concatenate([zero, s0_ref[0, dr, 2 * p + 1]], axis=1)], axis=0)
                     for dr in range(2) for p in range(npair))
    else:
        init = tuple(jnp.zeros((w, w), F32) for _ in range(2 * npair))
    fin = lax.fori_loop(0, n, body, init)
    segm = _seg_matrix(w, 6, 1.0 / HD_B)
    for p in range(npair):
        if not latent:
            for dr in range(2):
                st = fin[dr * npair + p].T
                sfin_ref[0, dr, 2 * p] = st[:HD_B, :HD_B]
                sfin_ref[0, dr, 2 * p + 1] = st[HD_B:, HD_B:]
        cols = slice(p * w, (p + 1) * w)
        y = o_ref[:, cols] + yb_ref[:, cols]
        yc = y - _split_dot(y, segm)
        yn = yc * lax.rsqrt(_split_dot(yc * yc, segm) + RWKV_GN_EPS)
        o_ref[:, cols] = (yn * lnw_ref[:, cols] + lnb_ref[:, cols] + bonus_ref[:, cols]) * gate_ref[:, cols]


def _rwkv_scan(prep, ln_w, ln_b, latent, s0t=None, dst=None):
    r, ldf, ldb, kb, vb, kk, a, gate, bonus = prep
    nseq, t, rb0 = (DEC_BATCH, DEC_SEQ, M_CTX // DEC_SEQ) if latent else (BATCH, SEQ, 0)
    npair = H_B // 2
    w = npair * 2 * HD_B
    blk = pl.BlockSpec((t, w), lambda b, h: (b, h))
    vec = pl.BlockSpec((1, w), lambda b, h: (0, h))
    st_spec = pl.BlockSpec((1, 2, 2 * npair, HD_B, HD_B), lambda b, h: (b, 0, h, 0, 0))
    o_spec = pl.BlockSpec((t, w), lambda b, h: (rb0 + b, h))
    o_shape = jax.ShapeDtypeStruct((M_ALL, MIX), F32)
    args = [r, ldf, ldb, kb, vb, kk, a, bonus, gate, ln_w.reshape(1, MIX), ln_b.reshape(1, MIX)]
    in_specs = [blk] * 9 + [vec, vec]
    if latent:
        args += [s0t, dst]
        in_specs += [st_spec, pl.BlockSpec(memory_space=pl.ANY)]
        out_specs, out_shape, aliases = o_spec, o_shape, {len(args) - 1: 0}
    else:
        out_specs = [o_spec, st_spec]
        out_shape = [o_shape, jax.ShapeDtypeStruct((nseq, 2, H_B, HD_B, HD_B), F32)]
        aliases = {}
    return pl.pallas_call(
        functools.partial(_rwkv_scan_kernel, t=t, c=RWKV_CHUNK, npair=npair, latent=latent),
        grid=(nseq, MIX // w),
        in_specs=in_specs, out_specs=out_specs, out_shape=out_shape,
        scratch_shapes=[pltpu.VMEM((t, w), F32)],
        input_output_aliases=aliases,
        compiler_params=_params("parallel", "parallel"),
        name="rwkv_scan_lat" if latent else "rwkv_scan_ctx",
    )(*args)


S5_L = 16
S5_ROWS = M_ALL // S5_L
S5_CW = S5_L * S5_GROUP
S5_GS = 8
S5_PW = 2 * P_C
S5_TAB = 24
S5_PRM = 2 * S5_GROUP + 8
S5_NC_CTX = SEQ // S5_L
S5_NC_LAT = DEC_SEQ // S5_L
S5_CTX_ROWS = BATCH * S5_NC_CTX


def _gelu_tanh(x):
    return 0.5 * x * (1.0 + jnp.tanh(math.sqrt(2.0 / math.pi) * (x + 0.044715 * (x * x * x))))


def _cmul(ar, ai, br, bi):
    return ar * br - ai * bi, ar * bi + ai * br


def _dot3(a, b, dims):
    ah, al = _split2(a)
    bh, bl = _split2(b)
    d = lambda x, y: lax.dot_general(x, y, dims, preferred_element_type=F32)
    return d(ah, bh) + d(ah, bl) + d(al, bh)


def _s5_param_kernel(prm_ref, tab_ref, r_ref, st_ref):
    nl, ns = S5_L, S5_GROUP
    lane_blk = lax.broadcasted_iota(jnp.int32, (ns, S5_CW), 1) // ns
    t_rows = [jnp.zeros((ns, S5_CW), F32) for _ in range(nl)]
    e_parts = []
    for d in range(2):
        part = lambda lo, n: (prm_ref[0, d, 0, lo:lo + n, :], prm_ref[0, d, 1, lo:lo + n, :])
        bb = _cmul(*part(2 * ns, 1), *part(0, ns))
        cc = part(ns, ns)
        pw = lambda k: (tab_ref[0, d, 0, k:k + 1, :], tab_ref[0, d, 1, k:k + 1, :])
        steps = range(nl)
        if d == 0:
            ke, ks, kk = [nl - 1 - j for j in steps], [j + 1 for j in steps], list(steps)
        else:
            ke, ks, kk = list(steps), [nl - j for j in steps], [nl - 1 - j for j in steps]
        stack = lambda xs: (jnp.concatenate([x[0] for x in xs], axis=0), jnp.concatenate([x[1] for x in xs], axis=0))
        e_re, e_im = stack([_cmul(*bb, *pw(k)) for k in ke])
        s_re, s_im = stack([_cmul(*cc, *pw(k)) for k in ks])
        k_re, k_im = stack([_cmul(*cc, *pw(k)) for k in kk])
        st_ref[0, d] = jnp.concatenate([s_re, -s_im], axis=1).astype(BF16)
        krow = _dot3(bb[0], k_re, NT) - _dot3(bb[1], k_im, NT)
        for j in steps:
            if d == 0:
                shifted, keep = pltpu.roll(krow, ns * j, axis=1), lane_blk >= j
            else:
                shifted, keep = pltpu.roll(krow, (S5_CW - ns * (nl - 1 - j)) % S5_CW, axis=1), lane_blk <= j
            t_rows[j] = t_rows[j] + jnp.where(keep, shifted, 0.0)
        e_parts += [e_re, e_im]
    r_ref[0] = jnp.concatenate([jnp.concatenate(t_rows, axis=0)] + e_parts, axis=1).astype(BF16)


def _s5_params(prm, tab):
    return pl.pallas_call(
        _s5_param_kernel,
        grid=(G_C,),
        in_specs=[pl.BlockSpec((1, 2, 2, S5_PRM, P_C), lambda g: (g, 0, 0, 0, 0)),
                  pl.BlockSpec((1, 2, 2, S5_TAB, P_C), lambda g: (g, 0, 0, 0, 0))],
        out_specs=[pl.BlockSpec((1, S5_CW, 2 * S5_CW), lambda g: (g, 0, 0)),
                   pl.BlockSpec((1, 2, S5_CW, S5_PW), lambda g: (g, 0, 0, 0))],
        out_shape=[jax.ShapeDtypeStruct((G_C, S5_CW, 2 * S5_CW), BF16),
                   jax.ShapeDtypeStruct((G_C, 2, S5_CW, S5_PW), BF16)],
        compiler_params=_params("parallel"),
        name="s5_params",
    )(prm, tab)


def _s5_core_kernel(u_ref, r_ref, st_ref, tab_ref, h0_ref, o_ref, hfin_ref, ug_scr, yg_scr, e_scr, hp_scr):
    ns, nl = S5_GROUP, S5_L
    per_v = 128 // ns
    blk = lax.broadcasted_iota(jnp.int32, (8, 128), 1) // ns
    tok8 = 8 * nl

    def regroup_in(ti, _):
        base = pl.multiple_of(ti * tok8, tok8)
        r8 = pl.ds(pl.multiple_of(ti * 8, 8), 8)
        uj = [u_ref[pl.ds(base + j, 8, stride=nl), :] for j in range(nl)]
        for g in range(S5_GS):
            for jh in range(nl // per_v):
                acc = jnp.zeros((8, 128), F32)
                for jj in range(per_v):
                    piece = pltpu.roll(uj[jh * per_v + jj], ((jj - g) * ns) % 128, axis=1)
                    acc = jnp.where(blk == jj, piece, acc)
                ug_scr[g, r8, jh * 128:(jh + 1) * 128] = acc
        return 0

    lax.fori_loop(0, S5_ROWS // 8, regroup_in, 0)

    chains = [(g, d) for g in range(S5_GS) for d in range(2)]
    for g in range(S5_GS):
        a = jnp.dot(ug_scr[g].astype(BF16), r_ref[g], preferred_element_type=F32)
        yg_scr[g] = a[:, 0:S5_CW]
        for d in range(2):
            e_scr[g, d] = a[:, S5_CW + d * S5_PW:S5_CW + (d + 1) * S5_PW]
    lam = []
    for g, d in chains:
        lr, li = (tab_ref[g, d, ri, nl:nl + 1, :] for ri in range(2))
        lam.append((jnp.concatenate([lr, lr], axis=1), jnp.concatenate([-li, li], axis=1)))

    def advance(c, hs, latent):
        nc = S5_NC_LAT if latent else S5_NC_CTX
        out = []
        for i, (g, d) in enumerate(chains):
            cc = c if d == 0 else nc - 1 - c
            rows = (pl.ds(S5_CTX_ROWS + cc, DEC_BATCH, stride=nc) if latent
                    else pl.ds(cc, BATCH, stride=nc))
            hp_scr[g, d, rows, :] = hs[i]
            out.append(lam[i][0] * hs[i] + lam[i][1] * pltpu.roll(hs[i], P_C, axis=1) + e_scr[g, d, rows, :])
        return out

    def both(c, carry):
        hc, hl = carry
        return tuple(advance(c, hc, False)), tuple(advance(c, hl, True))

    h_ctx = tuple(jnp.zeros((BATCH, S5_PW), F32) for _ in chains)
    h_lat = tuple(h0_ref[g, d] for g, d in chains)
    h_ctx, h_lat = lax.fori_loop(0, S5_NC_CTX, both, (h_ctx, h_lat))
    lax.fori_loop(S5_NC_CTX, S5_NC_LAT, lambda c, hl: tuple(advance(c, hl, True)), h_lat)
    for i, (g, d) in enumerate(chains):
        hfin_ref[g, d] = h_ctx[i]
    for g in range(S5_GS):
        y = yg_scr[g]
        for d in range(2):
            y = y + lax.dot_general(hp_scr[g, d].astype(BF16), st_ref[g, d], NT, preferred_element_type=F32)
        yg_scr[g] = y

    def regroup_out(ti, _):
        base = pl.multiple_of(ti * tok8, tok8)
        r8 = pl.ds(pl.multiple_of(ti * 8, 8), 8)
        yv = [[yg_scr[g, r8, jh * 128:(jh + 1) * 128] for jh in range(nl // per_v)] for g in range(S5_GS)]
        for j in range(nl):
            jh, jj = divmod(j, per_v)
            acc = jnp.zeros((8, 128), F32)
            for g in range(S5_GS):
                piece = pltpu.roll(yv[g][jh], ((g - jj) * ns) % 128, axis=1)
                acc = jnp.where(blk == g, piece, acc)
            o_ref[pl.ds(base + j, 8, stride=nl), :] = acc
        return 0

    lax.fori_loop(0, S5_ROWS // 8, regroup_out, 0)


def _s5_core(p2, r, st, tab, h0):
    blk = lambda *tail: pl.BlockSpec((S5_GS,) + tail, lambda s: (s,) + (0,) * len(tail))
    col = pl.BlockSpec((M_ALL, S5_GS * S5_GROUP), lambda s: (0, s))
    return pl.pallas_call(
        _s5_core_kernel,
        grid=(G_C // S5_GS,),
        in_specs=[col, blk(S5_CW, 2 * S5_CW), blk(2, S5_CW, S5_PW),
                  blk(2, 2, S5_TAB, P_C), blk(2, DEC_BATCH, S5_PW)],
        out_specs=[col, blk(2, BATCH, S5_PW)],
        out_shape=[jax.ShapeDtypeStruct((M_ALL, MIX), F32),
                   jax.ShapeDtypeStruct((G_C, 2, BATCH, S5_PW), F32)],
        scratch_shapes=[pltpu.VMEM((S5_GS, S5_ROWS, S5_CW), F32),
                        pltpu.VMEM((S5_GS, S5_ROWS, S5_CW), F32),
                        pltpu.VMEM((S5_GS, 2, S5_ROWS, S5_PW), F32),
                        pltpu.VMEM((S5_GS, 2, S5_ROWS, S5_PW), F32)],
        compiler_params=_params("parallel"),
        name="s5_core",
    )(p2, r, st, tab, h0)


def _s5_out_kernel(y_ref, u_ref, d_ref, w_ref, o_ref):
    z = _gelu_tanh(y_ref[...] + d_ref[...] * u_ref[...])
    o_ref[...] = z * _sigmoid(jnp.dot(z.astype(BF16), w_ref[...], preferred_element_type=F32))


def _s5_out(y_tok, p2, d_skip, w_glu_bf16):
    tm = 512
    return pl.pallas_call(
        _s5_out_kernel,
        grid=(M_ALL // tm,),
        in_specs=[pl.BlockSpec((tm, MIX), lambda i: (i, 0)),
                  pl.BlockSpec((tm, MIX), lambda i: (i, 0)),
                  pl.BlockSpec((1, MIX), lambda i: (0, 0)),
                  pl.BlockSpec((MIX, MIX), lambda i: (0, 0))],
        out_specs=pl.BlockSpec((tm, MIX), lambda i: (i, 0)),
        out_shape=jax.ShapeDtypeStruct((M_ALL, MIX), F32),
        compiler_params=_params("parallel"),
        name="s5_out",
    )(y_tok, p2, d_skip.reshape(1, MIX), w_glu_bf16)


def _s5_tables(lam_re, lam_im, log_dt, b_re, b_im, c_re, c_im):
    dt = jnp.exp(log_dt)[:, None, :, None]
    k = jnp.arange(S5_TAB, dtype=F32)[None, :, None, None]
    mag = jnp.exp(lam_re[:, None] * dt * k)
    pw_re = mag * jnp.cos(lam_im[:, None] * dt * k)
    pw_im = mag * jnp.sin(lam_im[:, None] * dt * k)
    ab_re, ab_im = pw_re[:, 1], pw_im[:, 1]
    den = lam_re * lam_re + lam_im * lam_im
    nr = ab_re - 1.0
    f_re = (nr * lam_re + ab_im * lam_im) / den
    f_im = (ab_im * lam_re - nr * lam_im) / den
    rows = lambda b, c, f: jnp.concatenate(
        [jnp.swapaxes(b, -1, -2), c, jnp.broadcast_to(f[:, :, None, :], (2, G_C, 8, P_C))], axis=2)
    prm = jnp.stack([rows(b_re, c_re, f_re), rows(b_im, c_im, f_im)], axis=1)
    tab = jnp.stack([pw_re, pw_im], axis=1)
    return jnp.transpose(prm, (2, 0, 1, 3, 4)), jnp.transpose(tab, (3, 0, 1, 2, 4))


def _s5_mixer(p2, tables, d_skip, w_glu_bf16, state_re, state_im):
    prm, tab = tables
    r, st = _s5_params(prm, tab)
    h0 = jnp.concatenate([state_re, state_im], axis=-1)
    y_tok, hfin = _s5_core(p2, r, st, tab, jnp.transpose(h0, (2, 1, 0, 3)))
    oc = _s5_out(y_tok, p2, d_skip, w_glu_bf16)
    hfin = jnp.transpose(hfin, (2, 1, 0, 3))
    return oc, hfin[..., :P_C], hfin[..., P_C:]


def _ret_kernel(*refs, t, latent):
    if latent:
        (q_ref, k_ref, v_ref, g_ref, dl_ref, lnw_ref, lnb_ref, r0_ref, cos_ref, sin_ref, _,
         o_ref, ob_ref) = refs
    else:
        q_ref, k_ref, v_ref, g_ref, dl_ref, lnw_ref, lnb_ref, o_ref, rfin_ref, ob_ref = refs
    n = t // CHUNK
    w = HD_D
    jf = lax.broadcasted_iota(jnp.int32, (CHUNK, CHUNK), 0).astype(F32)
    kf = lax.broadcasted_iota(jnp.int32, (CHUNK, CHUNK), 1).astype(F32)
    diff = jf - kf
    tabs = []
    for dr in range(2):
        for h in range(H_D):
            l = -_softplus(-dl_ref[h, dr:dr + 1, :])
            if dr == 0:
                dmat = jnp.where(diff >= 0, jnp.exp(l * jnp.maximum(diff, 0.0)), 0.0)
                xi = jnp.exp(l * (jf + 1.0))
                zeta = jnp.exp(l * (CHUNK - 1.0 - jf))
            else:
                dmat = jnp.where(diff < 0, jnp.exp(l * jnp.maximum(-diff, 0.0)), 0.0)
                xi = jnp.exp(l * (CHUNK - jf))
                zeta = jnp.exp(l * jf)
            tabs.append((dmat, xi, zeta, jnp.exp(l * CHUNK)))
    chains = [(dr, h) for dr in range(2) for h in range(H_D)]

    def body(ci, states):
        q, k, v, where = [], [], [], []
        for dr, h in chains:
            cj = ci if dr == 0 else n - 1 - ci
            rows = pl.ds(pl.multiple_of(cj * CHUNK, CHUNK), CHUNK)
            cols = slice(h * w, (h + 1) * w)
            qi = q_ref[rows, cols]
            ki = k_ref[rows, cols] * (HD_D ** -0.5)
            if latent:
                qi = _rope_pairs(qi, cos_ref[rows, :], sin_ref[rows, :], HD_D // 2)
                ki = _rope_pairs(ki, cos_ref[rows, :], sin_ref[rows, :], HD_D // 2)
            q.append(qi.astype(BF16))
            k.append(ki)
            v.append(v_ref[rows, cols].astype(BF16))
            where.append((o_ref if dr == 0 else ob_ref, rows, cols))
        nch = range(len(chains))
        inner = [(_bdot(q[i], k[i], NT) * tabs[i][0]).astype(BF16) for i in nch]
        cross = [_bdot(q[i], states[i]) * tabs[i][1] for i in nch]
        kz = [(k[i] * tabs[i][2]).astype(BF16) for i in nch]
        out = [jnp.dot(inner[i], v[i], preferred_element_type=F32) + cross[i] for i in nch]
        new_states = [states[i] * tabs[i][3] + lax.dot_general(kz[i], v[i], TN, preferred_element_type=F32)
                      for i in nch]
        for (dst, rows, cols), y in zip(where, out):
            dst[rows, cols] = y
        return tuple(new_states)

    if latent:
        lax.fori_loop(0, n, body, tuple(r0_ref[0, dr, h] for dr, h in chains))
    else:
        fin = lax.fori_loop(0, n, body, tuple(jnp.zeros((w, w), F32) for _ in chains))
        for i, (dr, h) in enumerate(chains):
            rfin_ref[0, dr, h] = fin[i]
    for h in range(H_D):
        cols = slice(h * w, (h + 1) * w)
        y = o_ref[:, cols] + ob_ref[:, cols]
        yc = y - jnp.mean(y, axis=-1, keepdims=True)
        yn = yc * lax.rsqrt(jnp.mean(yc * yc, axis=-1, keepdims=True) + RET_GN_EPS)
        o_ref[:, cols] = (yn * lnw_ref[:, cols] + lnb_ref[:, cols]) * _silu(g_ref[:, cols])


def _retention(p, decay_logit, ln_w, ln_b, latent, r0=None, rope=None, dst=None):
    if latent:
        nseq, t, rb0 = DEC_BATCH, DEC_SEQ, M_CTX // DEC_SEQ
    else:
        nseq, t, rb0 = BATCH, SEQ, 0
    w = HD_D
    dl = jnp.broadcast_to(decay_logit.T[:, :, None], (H_D, 2, w))
    full = lambda shape: pl.BlockSpec(shape, lambda b: (0,) * len(shape))
    st_spec = pl.BlockSpec((1, 2, H_D, w, w), lambda b: (b, 0, 0, 0, 0))
    in_specs = [pl.BlockSpec((t, MIX), lambda b, j=j: (rb0 + b, j)) for j in (1, 2, 3, 4)]
    in_specs += [full((H_D, 2, w)), full((1, MIX)), full((1, MIX))]
    args = [p, p, p, p, dl, ln_w.reshape(1, MIX), ln_b.reshape(1, MIX)]
    o_spec = pl.BlockSpec((t, MIX), lambda b: (rb0 + b, 0))
    o_shape = jax.ShapeDtypeStruct((M_ALL, MIX), F32)
    if latent:
        in_specs += [st_spec, full((t, w)), full((t, w)), pl.BlockSpec(memory_space=pl.ANY)]
        args += [r0, *rope, dst]
        out_specs, out_shape, aliases = o_spec, o_shape, {len(args) - 1: 0}
    else:
        out_specs = [o_spec, st_spec]
        out_shape = [o_shape, jax.ShapeDtypeStruct((nseq, 2, H_D, w, w), F32)]
        aliases = {}
    return pl.pallas_call(
        functools.partial(_ret_kernel, t=t, latent=latent),
        grid=(nseq,),
        in_specs=in_specs, out_specs=out_specs, out_shape=out_shape,
        scratch_shapes=[pltpu.VMEM((t, MIX), F32)],
        input_output_aliases=aliases,
        compiler_params=_params("parallel"),
        name="retention_lat" if latent else "retention_ctx",
    )(*args)


def _rope_tables(n_tok, dim, reps):
    rows = n_tok // GRID_W
    n_freq = dim // 4
    inv = 1.0 / (ROPE_THETA ** (jnp.arange(n_freq, dtype=F32) / n_freq))
    row = jnp.repeat(jnp.arange(rows, dtype=F32), GRID_W)
    col = jnp.tile(jnp.arange(GRID_W, dtype=F32), rows)
    ang = jnp.concatenate([row[:, None] * inv, col[:, None] * inv], axis=-1)
    cos, sin = jnp.cos(ang), jnp.sin(ang)
    return jnp.tile(jnp.concatenate([cos, cos], axis=1), (1, reps)), \
        jnp.tile(jnp.concatenate([-sin, sin], axis=1), (1, reps))


def kernel(x_prompt, x_sample, cache_k_ab, cache_v_ab, state_rwkv, state_s5_re, state_s5_im, state_ret, c, c_ctx, norm1_g, norm2_g, w_mod, b_mod, w_ff_gate, w_ff_up, w_ff_down, w_in_ab, w_out_ab, qk_gain_a, lambda_qk, subln_g, rwkv_mu, rwkv_k_k, rwkv_k_a, rwkv_r_k, rwkv_w0, rwkv_w_up, rwkv_a0, rwkv_a_up, rwkv_g_up, rwkv_ln_w, rwkv_ln_b, w_in_cd, w_out_cd, s5_lam_re, s5_lam_im, s5_log_dt, s5_b_re, s5_b_im, s5_c_re, s5_c_im, s5_d, s5_w_glu, ret_decay_logit, ret_ln_w, ret_ln_b):
    d = D_MODEL
    xs = [x_prompt.reshape(M_CTX, d), x_sample.reshape(M_LAT, d)]
    cvec = jnp.zeros((MOD_ROWS, d), F32).at[0].set(c_ctx).at[1:1 + DEC_BATCH].set(c)
    mods = _modulation(cvec, w_mod, b_mod)

    lam_init = 0.8 - 0.6 * math.exp(-0.3 * 0)
    p = _norm_linear(xs, norm1_g[0], mods[0], w_in_ab[0].astype(BF16))
    gain2 = jnp.tile(qk_gain_a[0], (1, 2))
    rope_a = _rope_tables(DEC_SEQ, HD_A, 2)
    ck = cache_k_ab[:, 0].reshape(DEC_BATCH * PAST_LEN, MIX)
    cv = cache_v_ab[:, 0].reshape(DEC_BATCH * PAST_LEN, MIX)
    oa, k_ctx = _attention(p, gain2, lambda_qk[0], subln_g[0], lam_init, latent=False)
    oa = _attention(p, gain2, lambda_qk[0], subln_g[0], lam_init, latent=True,
                    cache_k=ck, cache_v=cv, rope=rope_a, dst=oa)
    rw_prm = (rwkv_mu[0], rwkv_k_k[0], rwkv_k_a[0], rwkv_r_k[0], rwkv_w0[0], rwkv_w_up[0],
              rwkv_a0[0], rwkv_a_up[0], rwkv_g_up[0])
    s0_lat = jnp.swapaxes(state_rwkv[:, 0], -1, -2)
    ob, sfin_ctx = _rwkv_scan(_rwkv_prep(p, rw_prm, latent=False), rwkv_ln_w[0], rwkv_ln_b[0], latent=False)
    ob = _rwkv_scan(_rwkv_prep(p, rw_prm, latent=True), rwkv_ln_w[0], rwkv_ln_b[0], latent=True,
                    s0t=s0_lat, dst=ob)
    x = _mix_ffn(xs, oa, ob, w_out_ab[0].astype(BF16), norm2_g[0], mods[0], w_ff_gate[0].astype(BF16),
                 w_ff_up[0].astype(BF16), w_ff_down[0].astype(BF16), split_out=False)

    p2 = _norm_linear([x], norm1_g[1], mods[1], w_in_cd[0].astype(BF16))
    tables = _s5_tables(s5_lam_re[0], s5_lam_im[0], s5_log_dt[0], s5_b_re[0], s5_b_im[0],
                        s5_c_re[0], s5_c_im[0])
    oc, s5_fin_re, s5_fin_im = _s5_mixer(p2, tables, s5_d[0], s5_w_glu[0].astype(BF16),
                                         state_s5_re[:, 0], state_s5_im[:, 0])
    rope_d = _rope_tables(DEC_SEQ, HD_D, 1)
    od, rfin = _retention(p2, ret_decay_logit[0], ret_ln_w[0], ret_ln_b[0], latent=False)
    od = _retention(p2, ret_decay_logit[0], ret_ln_w[0], ret_ln_b[0], latent=True, r0=state_ret[:, 0],
                    rope=rope_d, dst=od)
    y_ctx, y_lat = _mix_ffn([x], oc, od, w_out_cd[0].astype(BF16), norm2_g[1], mods[1],
                            w_ff_gate[1].astype(BF16), w_ff_up[1].astype(BF16), w_ff_down[1].astype(BF16),
                            split_out=True)
    y_prompt = y_ctx.reshape(BATCH, SEQ, d)
    y_sample = y_lat.reshape(DEC_BATCH, DEC_SEQ, d)
    new_k = k_ctx.reshape(BATCH, 1, SEQ, H_A, 2, HD_A)
    new_v = p[:M_CTX, 2 * MIX:3 * MIX].reshape(BATCH, 1, SEQ, H_A, VD_A)
    new_rwkv = sfin_ctx[:, None]
    new_s5_re = s5_fin_re[:, None]
    new_s5_im = s5_fin_im[:, None]
    new_ret = rfin[:, None]
    return (y_prompt, y_sample, new_k, new_v, new_rwkv, new_s5_re, new_s5_im, new_ret)
```

```python
import functools
import math

import numpy as np
import jax
import jax.numpy as jnp
from jax import lax
from jax.experimental import pallas as pl
from jax.experimental.pallas import tpu as pltpu

F32 = jnp.float32
BF16 = jnp.bfloat16
HIGHEST = lax.Precision.HIGHEST

D_MODEL = 1024
BATCH = 32
SEQ = 256
DEC_BATCH = 2
DEC_SEQ = 1024
PAST_LEN = 256
GRID_W = 64
H_A = 4
HD_A = 64
VD_A = 128
H_B = 8
HD_B = 64
MIX = 512
LORA_W = 64
LORA_A = 64
LORA_G = 128
S5_GROUP = 16
G_C = 32
P_C = 64
S5_STATE = G_C * P_C
H_D = 4
HD_D = 128
CHUNK = 128
D_FF = 2816
IN_AB = 3328
IN_CD = 2560
ROPE_THETA = 10000.0
NORM_EPS = 1e-6
RWKV_GN_EPS = 64e-5
RET_GN_EPS = 1e-5

M_CTX = BATCH * SEQ
M_LAT = DEC_BATCH * DEC_SEQ
M_ALL = M_CTX + M_LAT
MOD_ROWS = 8
RWKV_CHUNK = 64
VMEM_LIMIT = 56 * 1024 * 1024

NN = (((1,), (0,)), ((), ()))
NT = (((1,), (1,)), ((), ()))
TN = (((0,), (0,)), ((), ()))


def _params(*sem):
    return pltpu.CompilerParams(dimension_semantics=sem, vmem_limit_bytes=VMEM_LIMIT)


def _bdot(a, b, dims=NN):
    return lax.dot_general(a.astype(BF16), b.astype(BF16), dims, preferred_element_type=F32)


def _hdot(a, b, dims=NN):
    return lax.dot_general(a, b, dims, precision=HIGHEST, preferred_element_type=F32)


def _split_dot(x, m):
    hi = x.astype(BF16)
    lo = (x - hi.astype(F32)).astype(BF16)
    return (jnp.dot(hi, m, preferred_element_type=F32) + jnp.dot(lo, m, preferred_element_type=F32))


def _seg_matrix(n, shift, val):
    r = lax.broadcasted_iota(jnp.int32, (n, n), 0) >> shift
    c = lax.broadcasted_iota(jnp.int32, (n, n), 1) >> shift
    return jnp.where(r == c, val, 0.0).astype(BF16)


def _sigmoid(x):
    return jax.nn.sigmoid(x)


def _silu(x):
    return x * jax.nn.sigmoid(x)


def _softplus(x):
    return jnp.maximum(x, 0.0) + jnp.log(1.0 + jnp.exp(-jnp.abs(x)))


def _mod_row(tile, tm):
    r0 = tile * tm
    return jnp.where(r0 < M_CTX, 0, 1 + (r0 - M_CTX) // DEC_SEQ)


def _norm_mod(x, g, sc_ref, sh_ref, row):
    y = x * lax.rsqrt(jnp.mean(x * x, axis=-1, keepdims=True) + NORM_EPS) * g
    return y * (1.0 + sc_ref[pl.ds(row, 1), :]) + sh_ref[pl.ds(row, 1), :]


def _mod_kernel(c_ref, w_ref, b_ref, o_ref):
    o_ref[0] = _hdot(_silu(c_ref[...]), w_ref[0]) + b_ref[0]


def _modulation(cvec, w_mod, b_mod):
    depth, d, n6 = w_mod.shape
    tn = 1536
    return pl.pallas_call(
        _mod_kernel,
        grid=(depth, n6 // tn),
        in_specs=[pl.BlockSpec((MOD_ROWS, d), lambda l, j: (0, 0)),
                  pl.BlockSpec((1, d, tn), lambda l, j: (l, 0, j)),
                  pl.BlockSpec((1, 1, tn), lambda l, j: (l, 0, j))],
        out_specs=pl.BlockSpec((1, MOD_ROWS, tn), lambda l, j: (l, 0, j)),
        out_shape=jax.ShapeDtypeStruct((depth, MOD_ROWS, n6), F32),
        compiler_params=_params("parallel", "parallel"),
        name="modulation",
    )(cvec, w_mod, b_mod.reshape(depth, 1, n6))


def _tile_specs(tm, d):
    nc = M_CTX // tm
    return [pl.BlockSpec((tm, d), lambda i: (jnp.minimum(i, nc - 1), 0)),
            pl.BlockSpec((tm, d), lambda i: (jnp.maximum(i - nc, 0), 0))]


def _pick_rows(refs, tm):
    if len(refs) == 1:
        return refs[0][...]
    return jnp.where(pl.program_id(0) < M_CTX // tm, refs[0][...], refs[1][...])


def _norm_linear_kernel(*refs, tm):
    *x_refs, g_ref, sc_ref, sh_ref, w_ref, o_ref = refs
    row = _mod_row(pl.program_id(0), tm)
    h = _norm_mod(_pick_rows(x_refs, tm), g_ref[...], sc_ref, sh_ref, row)
    o_ref[...] = jnp.dot(h.astype(BF16), w_ref[...], preferred_element_type=F32)


def _norm_linear(xs, g, mods, w_bf16):
    tm = 512
    d = D_MODEL
    n = w_bf16.shape[1]
    x_specs = _tile_specs(tm, d) if len(xs) == 2 else [pl.BlockSpec((tm, d), lambda i: (i, 0))]
    return pl.pallas_call(
        functools.partial(_norm_linear_kernel, tm=tm),
        grid=(M_ALL // tm,),
        in_specs=x_specs + [pl.BlockSpec((1, d), lambda i: (0, 0)),
                            pl.BlockSpec((MOD_ROWS, d), lambda i: (0, 1)),
                            pl.BlockSpec((MOD_ROWS, d), lambda i: (0, 0)),
                            pl.BlockSpec((d, n), lambda i: (0, 0))],
        out_specs=pl.BlockSpec((tm, n), lambda i: (i, 0)),
        out_shape=jax.ShapeDtypeStruct((M_ALL, n), F32),
        compiler_params=_params("arbitrary"),
        name="norm_linear",
    )(*xs, g.reshape(1, d), mods, mods, w_bf16)


def _mix_ffn_kernel(*refs, tm, ck, n_in, n_out):
    x_refs = refs[:n_in]
    (a_ref, b_ref, wa_ref, wb_ref, g1_ref, ng_ref, sc_ref, sh_ref, g2_ref,
     wg_ref, wu_ref, wd_ref) = refs[n_in:n_in + 12]
    o_refs = refs[n_in + 12:]
    row = _mod_row(pl.program_id(0), tm)
    mix = (jnp.dot(a_ref[...].astype(BF16), wa_ref[...], preferred_element_type=F32)
           + jnp.dot(b_ref[...].astype(BF16), wb_ref[...], preferred_element_type=F32))
    x = _pick_rows(x_refs, tm) + g1_ref[pl.ds(row, 1), :] * mix
    h = _norm_mod(x, ng_ref[...], sc_ref, sh_ref, row).astype(BF16)
    acc = jnp.zeros((tm, D_MODEL), F32)
    for c in range(D_FF // ck):
        gg = jnp.dot(h, wg_ref[:, c * ck:(c + 1) * ck], preferred_element_type=F32)
        uu = jnp.dot(h, wu_ref[:, c * ck:(c + 1) * ck], preferred_element_type=F32)
        act = (_silu(gg) * uu).astype(BF16)
        acc = acc + jnp.dot(act, wd_ref[c * ck:(c + 1) * ck, :], preferred_element_type=F32)
    y = x + g2_ref[pl.ds(row, 1), :] * acc
    if n_out == 1:
        o_refs[0][...] = y
    else:
        is_ctx = pl.program_id(0) < M_CTX // tm

        @pl.when(is_ctx)
        def _():
            o_refs[0][...] = y

        @pl.when(jnp.logical_not(is_ctx))
        def _():
            o_refs[1][...] = y


def _mix_ffn(xs, oa, ob, w_out_bf16, norm_g, mods, wg, wu, wd, split_out):
    tm, ck = 512, 256
    d = D_MODEL
    row_spec = lambda n: pl.BlockSpec((tm, n), lambda i: (i, 0))
    const = lambda shape, idx: pl.BlockSpec(shape, lambda i: idx, pipeline_mode=pl.Buffered(1))
    mod = lambda j: pl.BlockSpec((MOD_ROWS, d), lambda i: (0, j))
    x_specs = _tile_specs(tm, d) if len(xs) == 2 else [row_spec(d)]
    if split_out:
        out_specs = _tile_specs(tm, d)
        out_shape = [jax.ShapeDtypeStruct((M_CTX, d), F32), jax.ShapeDtypeStruct((M_LAT, d), F32)]
    else:
        out_specs, out_shape = row_spec(d), jax.ShapeDtypeStruct((M_ALL, d), F32)
    return pl.pallas_call(
        functools.partial(_mix_ffn_kernel, tm=tm, ck=ck, n_in=len(xs), n_out=2 if split_out else 1),
        grid=(M_ALL // tm,),
        in_specs=x_specs + [row_spec(MIX), row_spec(MIX),
                            const((MIX, d), (0, 0)), const((MIX, d), (1, 0)),
                            mod(2),
                            pl.BlockSpec((1, d), lambda i: (0, 0)),
                            mod(4), mod(3), mod(5),
                            const((d, D_FF), (0, 0)), const((d, D_FF), (0, 0)), const((D_FF, d), (0, 0))],
        out_specs=out_specs, out_shape=out_shape,
        compiler_params=_params("arbitrary"),
        name="mix_ffn",
    )(*xs, oa, ob, w_out_bf16, w_out_bf16, mods, norm_g.reshape(1, d), mods, mods, mods, wg, wu, wd)


def _qk_norm(x, gain, segm):
    ms = _split_dot(x * x, segm)
    return x * lax.rsqrt(ms + NORM_EPS) * gain


def _rope_pairs(x, cosf, sinf, half):
    lane = lax.broadcasted_iota(jnp.int32, x.shape, 1)
    first = (lane & (2 * half - 1)) < half
    n = x.shape[1]
    partner = jnp.where(first, pltpu.roll(x, n - half, axis=1), pltpu.roll(x, half, axis=1))
    return x * cosf + partner * sinf


def _attn_kernel(*refs, latent, lam_init):
    if latent:
        (q_ref, k_ref, v_ref, ck_ref, cv_ref, cosq_ref, sinq_ref, cosk_ref, sink_ref,
         gain_ref, lam_ref, sub_ref, _, o_ref, kall, vall) = refs
    else:
        q_ref, k_ref, v_ref, gain_ref, lam_ref, sub_ref, o_ref, kn_ref, kall, vall = refs
    w = 2 * HD_A
    segm = _seg_matrix(w, 6, 1.0 / HD_A)
    gains = gain_ref[...]
    cols = [slice(h * w, (h + 1) * w) for h in range(H_A)]

    @pl.when(pl.program_id(1) == 0)
    def _():
        k = [_qk_norm(k_ref[:, c], gains[1:2], segm) for c in cols]
        if latent:
            k = [_rope_pairs(x, cosk_ref[...], sink_ref[...], HD_A // 2) for x in k]
            for c, x in zip(cols, k):
                kall[0:PAST_LEN, c] = ck_ref[:, c].astype(BF16)
                kall[PAST_LEN:, c] = x.astype(BF16)
                vall[0:PAST_LEN, c] = cv_ref[:, c].astype(BF16)
                vall[PAST_LEN:, c] = v_ref[:, c].astype(BF16)
        else:
            for c, x in zip(cols, k):
                kn_ref[:, c] = x
                kall[:, c] = x.astype(BF16)
                vall[:, c] = v_ref[:, c].astype(BF16)

    q = [_qk_norm(q_ref[:, c], gains[0:1], segm) for c in cols]
    if latent:
        q = [_rope_pairs(x, cosq_ref[...], sinq_ref[...], HD_A // 2) for x in q]
    lv = lam_ref[...]
    lam = (jnp.exp(jnp.sum(lv[0:1] * lv[1:2], axis=1, keepdims=True))
           - jnp.exp(jnp.sum(lv[2:3] * lv[3:4], axis=1, keepdims=True)) + lam_init)
    scale = HD_A ** -0.5
    comp0 = lax.broadcasted_iota(jnp.int32, q[0].shape, 1) < HD_A
    qc = [jnp.where(comp0, *sel).astype(BF16) for x in q for sel in ((x, 0.0), (0.0, x))]
    s = [lax.dot_general(qc[i], kall[:, cols[i // 2]], NT, preferred_element_type=F32) * scale
         for i in range(2 * H_A)]
    e = [jnp.exp(x - jnp.max(x, axis=-1, keepdims=True)) for x in s]
    p = [x / jnp.sum(x, axis=-1, keepdims=True) for x in e]
    att = [(p[2 * h] - lam * p[2 * h + 1]).astype(BF16) for h in range(H_A)]
    o = [jnp.dot(att[h], vall[:, cols[h]], preferred_element_type=F32) for h in range(H_A)]
    o = [x * lax.rsqrt(jnp.mean(x * x, axis=-1, keepdims=True) + NORM_EPS) * sub_ref[...] for x in o]
    for c, x in zip(cols, o):
        o_ref[:, c] = x * (1.0 - lam_init)


def _attention(p, gain2, lambda_qk, subln_g, lam_init, latent, cache_k=None, cache_v=None, rope=None,
               dst=None):
    w = 2 * HD_A
    if latent:
        nseq, t, tq, rb0, s_len = DEC_BATCH, DEC_SEQ, 128, M_CTX // DEC_SEQ, PAST_LEN + DEC_SEQ
    else:
        nseq, t, tq, rb0, s_len = BATCH, SEQ, SEQ, 0, SEQ
    nq = t // tq
    qoff = rb0 * nq
    full = lambda shape: pl.BlockSpec(shape, lambda b, i: (0,) * len(shape))
    in_specs = [pl.BlockSpec((tq, MIX), lambda b, i: (qoff + b * nq + i, 0)),
                pl.BlockSpec((t, MIX), lambda b, i: (rb0 + b, 1)),
                pl.BlockSpec((t, MIX), lambda b, i: (rb0 + b, 2))]
    args = [p, p, p]
    if latent:
        cosf, sinf = rope
        in_specs += [pl.BlockSpec((PAST_LEN, MIX), lambda b, i: (b, 0)),
                     pl.BlockSpec((PAST_LEN, MIX), lambda b, i: (b, 0)),
                     pl.BlockSpec((tq, w), lambda b, i: (i, 0)),
                     pl.BlockSpec((tq, w), lambda b, i: (i, 0)),
                     full((t, w)), full((t, w))]
        args += [cache_k, cache_v, cosf, sinf, cosf, sinf]
    in_specs += [full((2, w)), full((4, HD_A)), full((1, w))]
    args += [gain2, lambda_qk, subln_g.reshape(1, w)]
    o_spec = pl.BlockSpec((tq, MIX), lambda b, i: (qoff + b * nq + i, 0))
    o_shape = jax.ShapeDtypeStruct((M_ALL, MIX), F32)
    aliases = {}
    if latent:
        out_specs, out_shape = o_spec, o_shape
        in_specs.append(pl.BlockSpec(memory_space=pl.ANY))
        args.append(dst)
        aliases = {len(args) - 1: 0}
    else:
        out_specs = [o_spec, pl.BlockSpec((t, MIX), lambda b, i: (b, 0))]
        out_shape = [o_shape, jax.ShapeDtypeStruct((nseq * t, MIX), F32)]
    return pl.pallas_call(
        functools.partial(_attn_kernel, latent=latent, lam_init=lam_init),
        grid=(nseq, nq),
        in_specs=in_specs, out_specs=out_specs, out_shape=out_shape,
        scratch_shapes=[pltpu.VMEM((s_len, MIX), BF16), pltpu.VMEM((s_len, MIX), BF16)],
        input_output_aliases=aliases,
        compiler_params=_params("parallel", "arbitrary"),
        name="diff_attention_lat" if latent else "diff_attention_ctx",
    )(*args)


def _centred_shift(x, mu):
    t = x.shape[0]
    row = lax.broadcasted_iota(jnp.int32, x.shape, 0)
    prev = jnp.where(row == 0, 0.0, pltpu.roll(x, 1, axis=0))
    nxt = jnp.where(row == t - 1, 0.0, pltpu.roll(x, t - 1, axis=0))
    return x + (0.5 * (prev + nxt) - x) * mu


def _seg_sum(x, segm):
    return jnp.concatenate([_split_dot(x[:, j * 128:(j + 1) * 128], segm) for j in range(x.shape[1] // 128)],
                           axis=1)


def _rwkv_prep_kernel(r_ref, k_ref, v_ref, l_ref, mur_ref, muk_ref, muv_ref, mul_ref,
                      kk_ref, ka_ref, rk_ref, w0_ref, wup_ref, a0_ref, aup_ref, gup_ref,
                      ro_ref, ldf_ref, ldb_ref, kbo_ref, vbo_ref, kko_ref, ao_ref, gate_ref, bonus_ref):
    seg1 = _seg_matrix(128, 6, 1.0)
    r = _centred_shift(r_ref[...], mur_ref[...])
    kb = _centred_shift(k_ref[...], muk_ref[...])
    vb = _centred_shift(v_ref[...], muv_ref[...])
    lo = _centred_shift(l_ref[...], mul_ref[...])
    xw = lo[:, 0:LORA_W]
    xa = lo[:, LORA_W:LORA_W + LORA_A]
    xg = lo[:, LORA_W + LORA_A:]
    kk = kb * kk_ref[...]
    kk = kk * lax.rsqrt(_seg_sum(kk * kk, seg1) + 1e-12)
    a = _sigmoid(a0_ref[...] + _bdot(xa, aup_ref[...]))
    kb2 = kb * (1.0 + (a - 1.0) * ka_ref[...])
    lw = jnp.tanh(xw)
    for dr, ld_ref in enumerate((ldf_ref, ldb_ref)):
        z = w0_ref[dr:dr + 1, :] + _bdot(lw, wup_ref[dr])
        logw = -_softplus(-z) - 0.5
        ld_ref[...] = -jnp.exp(logw)
    gate_ref[...] = _bdot(_sigmoid(xg), gup_ref[...])
    bonus_ref[...] = _seg_sum(r * kb2 * rk_ref[...], seg1) * vb
    ro_ref[...] = r
    kbo_ref[...] = kb2
    vbo_ref[...] = vb
    kko_ref[...] = kk
    ao_ref[...] = a


def _rwkv_prep(p, prm, latent):
    mu, k_k, k_a, r_k, w0, w_up, a0, a_up, g_up = prm
    if latent:
        nseq, t, rb0 = DEC_BATCH, DEC_SEQ, M_CTX // DEC_SEQ
    else:
        nseq, t, rb0 = BATCH, SEQ, 0
    lw = LORA_W + LORA_A + LORA_G
    c0 = (IN_AB - 3 * MIX - lw) // MIX
    cl = (IN_AB - lw) // lw
    row = lambda n: pl.BlockSpec((1, n), lambda b: (0, 0))
    in_specs = [pl.BlockSpec((t, MIX), lambda b: (rb0 + b, c0)),
                pl.BlockSpec((t, MIX), lambda b: (rb0 + b, c0 + 1)),
                pl.BlockSpec((t, MIX), lambda b: (rb0 + b, c0 + 2)),
                pl.BlockSpec((t, lw), lambda b: (rb0 + b, cl)),
                row(MIX), row(MIX), row(MIX), row(lw),
                row(MIX), row(MIX), row(MIX),
                pl.BlockSpec((2, MIX), lambda b: (0, 0)),
                pl.BlockSpec((2, LORA_W, MIX), lambda b: (0, 0, 0)),
                row(MIX),
                pl.BlockSpec((LORA_A, MIX), lambda b: (0, 0)),
                pl.BlockSpec((LORA_G, MIX), lambda b: (0, 0))]
    o_spec = pl.BlockSpec((t, MIX), lambda b: (b, 0))
    o_shape = jax.ShapeDtypeStruct((nseq * t, MIX), F32)
    return pl.pallas_call(
        _rwkv_prep_kernel,
        grid=(nseq,),
        in_specs=in_specs, out_specs=[o_spec] * 9, out_shape=[o_shape] * 9,
        compiler_params=_params("parallel"),
        name="rwkv_prep_lat" if latent else "rwkv_prep_ctx",
    )(p, p, p, p,
      mu[None, 0:MIX], mu[None, MIX:2 * MIX], mu[None, 2 * MIX:3 * MIX], mu[None, 3 * MIX:],
      k_k.reshape(1, MIX), k_a.reshape(1, MIX), r_k.reshape(1, MIX), w0, w_up,
      a0.reshape(1, MIX), a_up, g_up)


def _split2(x):
    hi = x.astype(BF16)
    return hi, (x - hi.astype(F32)).astype(BF16)


RWKV_INV_BLOCK = 16


def _rwkv_masks(c, rev):
    ti = lax.broadcasted_iota(jnp.int32, (c, c), 0)
    si = lax.broadcasted_iota(jnp.int32, (c, c), 1)
    tri = jnp.where((si >= ti) if rev else (si <= ti), 1.0, 0.0).astype(BF16)
    t4 = lax.broadcasted_iota(jnp.int32, (4 * c, 4 * c), 0)
    s4 = lax.broadcasted_iota(jnp.int32, (4 * c, 4 * c), 1)
    tm, sm = t4 & (c - 1), s4 & (c - 1)
    strict = (sm > tm) if rev else (sm < tm)
    incl = (sm >= tm) if rev else (sm <= tm)
    same_head = ((t4 // c) & 1) == ((s4 // c) & 1)
    top = t4 < 2 * c
    gmask = same_head & ((top & strict) | (~top & incl))
    t2 = lax.broadcasted_iota(jnp.int32, (2 * c, 2 * c), 0)
    s2 = lax.broadcasted_iota(jnp.int32, (2 * c, 2 * c), 1)
    same = lambda n: (t2 // n) == (s2 // n)
    levels = []
    n = RWKV_INV_BLOCK
    while n < c:
        levels.append(same(2 * n) & ~same(n))
        n *= 2
    f = lambda m: jnp.where(m, 1.0, 0.0)
    return tri, f(gmask), f(same(RWKV_INV_BLOCK)), tuple(f(m) for m in levels), f(same(c)), f(t2 == s2)


def _keep(mask01, x):
    return jnp.where(mask01 > 0.5, x, 0.0)


def _tri_inverse(a, diag_blk, levels, eye):
    n = a[0].shape[0]
    d = [_keep(diag_blk, x) for x in a]
    t = [eye + x for x in d]
    p = [_bdot(x, x) for x in d]
    for _ in range(int(math.log2(RWKV_INV_BLOCK)) - 2):
        res = [_bdot(jnp.concatenate([pi, ti], axis=0), pi) for pi, ti in zip(p, t)]
        p = [x[:n] for x in res]
        t = [ti + x[n:] for ti, x in zip(t, res)]
    t = [ti + _bdot(ti, pi) for ti, pi in zip(t, p)]
    for off in levels:
        half = [_bdot(ti, _keep(off, x)) for ti, x in zip(t, a)]
        t = [ti + _bdot(x, ti) for ti, x in zip(t, half)]
    return t


def _rwkv_pair_chunks(ins, sts, tris, gmasks, diag_blk, levels, same_head, eye):
    c, w = ins[0][0].shape
    hd = w // 2
    nch = range(len(ins))
    r, ld, kb, vb, kk, a = (list(z) for z in zip(*ins))
    split = [_split2(x) for x in ld]
    lcum = [jnp.dot(tris[i], split[i][0], preferred_element_type=F32)
            + jnp.dot(tris[i], split[i][1], preferred_element_type=F32) for i in nch]
    ones = jnp.ones((c, w), BF16)
    pcol = [jnp.exp(lax.dot_general(hi, ones, TN, preferred_element_type=F32)
                    + lax.dot_general(lo, ones, TN, preferred_element_type=F32)) for hi, lo in split]
    ltot = [jnp.sum(x, axis=0, keepdims=True) for x in ld]
    beta = [kk[i] * a[i] for i in nch]
    eneg = [jnp.exp(-x) for x in lcum]
    abar = [-kk[i] * jnp.exp(lcum[i] - ld[i]) for i in nch]
    rbar = [r[i] * jnp.exp(lcum[i]) for i in nch]
    bt = [(beta[i] * eneg[i]).astype(BF16) for i in nch]
    kt = [(kb[i] * eneg[i]).astype(BF16) for i in nch]
    vbb = [x.astype(BF16) for x in vb]
    head0 = lax.broadcasted_iota(jnp.int32, (c, w), 1) < hd
    pick = lambda res: jnp.where(head0, res[:c], res[c:])
    arst = [_bdot(jnp.concatenate([abar[i], rbar[i]], axis=0), sts[i]) for i in nch]
    lhs = [jnp.concatenate([jnp.where(head0, abar[i], 0.0), jnp.where(head0, 0.0, abar[i]),
                            jnp.where(head0, rbar[i], 0.0), jnp.where(head0, 0.0, rbar[i])], axis=0) for i in nch]
    g = [_keep(gmasks[i], _bdot(lhs[i], jnp.concatenate([bt[i], bt[i], kt[i], kt[i]], axis=0), NT))
         for i in nch]
    x = [arst[i][:c] + pick(_bdot(g[i][:2 * c, 2 * c:], jnp.concatenate([vbb[i], vbb[i]], axis=0))) for i in nch]
    tinv = _tri_inverse([gi[:2 * c, :2 * c] for gi in g], diag_blk, levels, eye)
    u = [pick(_bdot(tinv[i], jnp.concatenate([x[i], x[i]], axis=0))) for i in nch]
    ub = [z.astype(BF16) for z in u]
    y = [arst[i][c:] + pick(_bdot(g[i][2 * c:], jnp.concatenate([ub[i], ub[i], vbb[i], vbb[i]], axis=0)))
         for i in nch]
    erem = [jnp.exp(ltot[i] - lcum[i]) for i in nch]
    bkh = [jnp.concatenate([beta[i] * erem[i], kb[i] * erem[i]], axis=0) for i in nch]
    st_new = [pcol[i] * sts[i] + _keep(same_head, _bdot(bkh[i], jnp.concatenate([ub[i], vbb[i]], axis=0), TN))
              for i in nch]
    return y, st_new


def _rwkv_scan_kernel(*refs, t, c, npair, latent):
    r_ref, ldf_ref, ldb_ref, kb_ref, vb_ref, kk_ref, a_ref, bonus_ref, gate_ref, lnw_ref, lnb_ref = refs[:11]
    if latent:
        s0_ref, _, o_ref, yb_ref = refs[11:]
    else:
        o_ref, sfin_ref, yb_ref = refs[11:]
    n = t // c
    w = 2 * HD_B
    masks = [_rwkv_masks(c, rev) for rev in (False, True)]

    def body(ci, states):
        ins, tris, gmasks, dsts = [], [], [], []
        for dr, (ld_ref, dst) in enumerate(((ldf_ref, o_ref), (ldb_ref, yb_ref))):
            cj = ci if dr == 0 else n - 1 - ci
            rows = pl.ds(pl.multiple_of(cj * c, c), c)
            for p in range(npair):
                cols = slice(p * w, (p + 1) * w)
                ins.append(tuple(ref[rows, cols] for ref in (r_ref, ld_ref, kb_ref, vb_ref, kk_ref, a_ref)))
                tris.append(masks[dr][0])
                gmasks.append(masks[dr][1])
                dsts.append((dst, rows, cols))
        ys, new_states = _rwkv_pair_chunks(ins, list(states), tris, gmasks, *masks[0][2:])
        for (dst, rows, cols), y in zip(dsts, ys):
            dst[rows, cols] = y
        return tuple(new_states)

    if latent:
        zero = jnp.zeros((HD_B, HD_B), F32)
        init = tuple(jnp.concatenate([jnp.concatenate([s0_ref[0, dr, 2 * p], zero], axis=1),
                                      jnp.concatenate([zero, s0_ref[0, dr, 2 * p + 1]], axis=1)], axis=0)
                     for dr in range(2) for p in range(npair))
    else:
        init = tuple(jnp.zeros((w, w), F32) for _ in range(2 * npair))
    fin = lax.fori_loop(0, n, body, init)
    segm = _seg_matrix(w, 6, 1.0 / HD_B)
    for p in range(npair):
        if not latent:
            for dr in range(2):
                st = fin[dr * npair + p].T
                sfin_ref[0, dr, 2 * p] = st[:HD_B, :HD_B]
                sfin_ref[0, dr, 2 * p + 1] = st[HD_B:, HD_B:]
        cols = slice(p * w, (p + 1) * w)
        y = o_ref[:, cols] + yb_ref[:, cols]
        yc = y - _split_dot(y, segm)
        yn = yc * lax.rsqrt(_split_dot(yc * yc, segm) + RWKV_GN_EPS)
        o_ref[:, cols] = (yn * lnw_ref[:, cols] + lnb_ref[:, cols] + bonus_ref[:, cols]) * gate_ref[:, cols]


def _rwkv_scan(prep, ln_w, ln_b, latent, s0t=None, dst=None):
    r, ldf, ldb, kb, vb, kk, a, gate, bonus = prep
    nseq, t, rb0 = (DEC_BATCH, DEC_SEQ, M_CTX // DEC_SEQ) if latent else (BATCH, SEQ, 0)
    npair = H_B // 2
    w = npair * 2 * HD_B
    blk = pl.BlockSpec((t, w), lambda b, h: (b, h))
    vec = pl.BlockSpec((1, w), lambda b, h: (0, h))
    st_spec = pl.BlockSpec((1, 2, 2 * npair, HD_B, HD_B), lambda b, h: (b, 0, h, 0, 0))
    o_spec = pl.BlockSpec((t, w), lambda b, h: (rb0 + b, h))
    o_shape = jax.ShapeDtypeStruct((M_ALL, MIX), F32)
    args = [r, ldf, ldb, kb, vb, kk, a, bonus, gate, ln_w.reshape(1, MIX), ln_b.reshape(1, MIX)]
    in_specs = [blk] * 9 + [vec, vec]
    if latent:
        args += [s0t, dst]
        in_specs += [st_spec, pl.BlockSpec(memory_space=pl.ANY)]
        out_specs, out_shape, aliases = o_spec, o_shape, {len(args) - 1: 0}
    else:
        out_specs = [o_spec, st_spec]
        out_shape = [o_shape, jax.ShapeDtypeStruct((nseq, 2, H_B, HD_B, HD_B), F32)]
        aliases = {}
    return pl.pallas_call(
        functools.partial(_rwkv_scan_kernel, t=t, c=RWKV_CHUNK, npair=npair, latent=latent),
        grid=(nseq, MIX // w),
        in_specs=in_specs, out_specs=out_specs, out_shape=out_shape,
        scratch_shapes=[pltpu.VMEM((t, w), F32)],
        input_output_aliases=aliases,
        compiler_params=_params("parallel", "parallel"),
        name="rwkv_scan_lat" if latent else "rwkv_scan_ctx",
    )(*args)


S5_L = 16
S5_ROWS = M_ALL // S5_L
S5_CW = S5_L * S5_GROUP
S5_GS = 8
S5_PW = 2 * P_C
S5_TAB = 24
S5_PRM = 2 * S5_GROUP + 8
S5_NC_CTX = SEQ // S5_L
S5_NC_LAT = DEC_SEQ // S5_L
S5_CTX_ROWS = BATCH * S5_NC_CTX


def _gelu_tanh(x):
    return 0.5 * x * (1.0 + jnp.tanh(math.sqrt(2.0 / math.pi) * (x + 0.044715 * (x * x * x))))


def _cmul(ar, ai, br, bi):
    return ar * br - ai * bi, ar * bi + ai * br


def _dot3(a, b, dims):
    ah, al = _split2(a)
    bh, bl = _split2(b)
    d = lambda x, y: lax.dot_general(x, y, dims, preferred_element_type=F32)
    return d(ah, bh) + d(ah, bl) + d(al, bh)


def _s5_param_kernel(prm_ref, tab_ref, r_ref, st_ref):
    nl, ns = S5_L, S5_GROUP
    lane_blk = lax.broadcasted_iota(jnp.int32, (ns, S5_CW), 1) // ns
    t_rows = [jnp.zeros((ns, S5_CW), F32) for _ in range(nl)]
    e_parts = []
    for d in range(2):
        part = lambda lo, n: (prm_ref[0, d, 0, lo:lo + n, :], prm_ref[0, d, 1, lo:lo + n, :])
        bb = _cmul(*part(2 * ns, 1), *part(0, ns))
        cc = part(ns, ns)
        pw = lambda k: (tab_ref[0, d, 0, k:k + 1, :], tab_ref[0, d, 1, k:k + 1, :])
        steps = range(nl)
        if d == 0:
            ke, ks, kk = [nl - 1 - j for j in steps], [j + 1 for j in steps], list(steps)
        else:
            ke, ks, kk = list(steps), [nl - j for j in steps], [nl - 1 - j for j in steps]
        stack = lambda xs: (jnp.concatenate([x[0] for x in xs], axis=0), jnp.concatenate([x[1] for x in xs], axis=0))
        e_re, e_im = stack([_cmul(*bb, *pw(k)) for k in ke])
        s_re, s_im = stack([_cmul(*cc, *pw(k)) for k in ks])
        k_re, k_im = stack([_cmul(*cc, *pw(k)) for k in kk])
        st_ref[0, d] = jnp.concatenate([s_re, -s_im], axis=1).astype(BF16)
        krow = _dot3(bb[0], k_re, NT) - _dot3(bb[1], k_im, NT)
        for j in steps:
            if d == 0:
                shifted, keep = pltpu.roll(krow, ns * j, axis=1), lane_blk >= j
            else:
                shifted, keep = pltpu.roll(krow, (S5_CW - ns * (nl - 1 - j)) % S5_CW, axis=1), lane_blk <= j
            t_rows[j] = t_rows[j] + jnp.where(keep, shifted, 0.0)
        e_parts += [e_re, e_im]
    r_ref[0] = jnp.concatenate([jnp.concatenate(t_rows, axis=0)] + e_parts, axis=1).astype(BF16)


def _s5_params(prm, tab):
    return pl.pallas_call(
        _s5_param_kernel,
        grid=(G_C,),
        in_specs=[pl.BlockSpec((1, 2, 2, S5_PRM, P_C), lambda g: (g, 0, 0, 0, 0)),
                  pl.BlockSpec((1, 2, 2, S5_TAB, P_C), lambda g: (g, 0, 0, 0, 0))],
        out_specs=[pl.BlockSpec((1, S5_CW, 2 * S5_CW), lambda g: (g, 0, 0)),
                   pl.BlockSpec((1, 2, S5_CW, S5_PW), lambda g: (g, 0, 0, 0))],
        out_shape=[jax.ShapeDtypeStruct((G_C, S5_CW, 2 * S5_CW), BF16),
                   jax.ShapeDtypeStruct((G_C, 2, S5_CW, S5_PW), BF16)],
        compiler_params=_params("parallel"),
        name="s5_params",
    )(prm, tab)


def _s5_core_kernel(u_ref, r_ref, st_ref, tab_ref, h0_ref, o_ref, hfin_ref, ug_scr, yg_scr, e_scr, hp_scr):
    ns, nl = S5_GROUP, S5_L
    per_v = 128 // ns
    blk = lax.broadcasted_iota(jnp.int32, (8, 128), 1) // ns

    def merge(select):
        acc = select(0)
        for b in range(1, per_v):
            acc = jnp.where(blk == b, select(b), acc)
        return acc

    def shuffle(srcs):
        rolled = []
        for s in range(per_v):
            m = merge(lambda b: srcs[(b + s) % per_v])
            rolled.append(m if s == 0 else pltpu.roll(m, s * ns, axis=1))
        return [merge(lambda b: rolled[(b - a) % per_v]) for a in range(per_v)]

    def tiles(regroup):
        def ctx_tile(ti, _):
            c, b0 = ti // (BATCH // 8), (ti % (BATCH // 8)) * 8
            regroup(b0 * SEQ + c * nl, SEQ, pl.multiple_of(c * BATCH + b0, 8))
            return 0

        def lat_tile(ti, _):
            b, c0 = ti // (S5_NC_LAT // 8), (ti % (S5_NC_LAT // 8)) * 8
            regroup(M_CTX + b * DEC_SEQ + c0 * nl, nl, pl.multiple_of(S5_CTX_ROWS + b * S5_NC_LAT + c0, 8))
            return 0

        lax.fori_loop(0, S5_CTX_ROWS // 8, ctx_tile, 0, unroll=4)
        lax.fori_loop(0, (S5_ROWS - S5_CTX_ROWS) // 8, lat_tile, 0, unroll=4)

    def regroup_in(tok, stride, row):
        uj = [u_ref[pl.ds(tok + j, 8, stride=stride), :] for j in range(nl)]
        for jh in range(nl // per_v):
            for g, v in enumerate(shuffle(uj[jh * per_v:(jh + 1) * per_v])):
                ug_scr[g, pl.ds(row, 8), jh * 128:(jh + 1) * 128] = v

    tiles(regroup_in)

    chains = [(g, d) for g in range(S5_GS) for d in range(2)]
    for g in range(S5_GS):
        a = jnp.dot(ug_scr[g].astype(BF16), r_ref[g], preferred_element_type=F32)
        yg_scr[g] = a[:, 0:S5_CW]
        for d in range(2):
            e_scr[g, d] = a[:, S5_CW + d * S5_PW:S5_CW + (d + 1) * S5_PW]
    lam = []
    for g, d in chains:
        lr, li = (tab_ref[g, d, ri, nl:nl + 1, :] for ri in range(2))
        lam.append((jnp.concatenate([lr, lr], axis=1), jnp.concatenate([-li, li], axis=1)))

    def advance(c, hs, latent):
        nc = S5_NC_LAT if latent else S5_NC_CTX
        out = []
        for i, (g, d) in enumerate(chains):
            cc = c if d == 0 else nc - 1 - c
            rows = (pl.ds(S5_CTX_ROWS + cc, DEC_BATCH, stride=nc) if latent
                    else pl.ds(pl.multiple_of(cc * BATCH, BATCH), BATCH))
            hp_scr[g, d, rows, :] = hs[i]
            out.append(lam[i][0] * hs[i] + lam[i][1] * pltpu.roll(hs[i], P_C, axis=1) + e_scr[g, d, rows, :])
        return out

    def both(c, carry):
        hc, hl = carry
        return tuple(advance(c, hc, False)), tuple(advance(c, hl, True))

    h_ctx = tuple(jnp.zeros((BATCH, S5_PW), F32) for _ in chains)
    h_lat = tuple(h0_ref[g, d] for g, d in chains)
    h_ctx, h_lat = lax.fori_loop(0, S5_NC_CTX, both, (h_ctx, h_lat))
    lax.fori_loop(S5_NC_CTX, S5_NC_LAT, lambda c, hl: tuple(advance(c, hl, True)), h_lat)
    for i, (g, d) in enumerate(chains):
        hfin_ref[g, d] = h_ctx[i]
    for g in range(S5_GS):
        y = yg_scr[g]
        for d in range(2):
            y = y + lax.dot_general(hp_scr[g, d].astype(BF16), st_ref[g, d], NT, preferred_element_type=F32)
        yg_scr[g] = y

    def regroup_out(tok, stride, row):
        for jh in range(nl // per_v):
            yv = [yg_scr[g, pl.ds(row, 8), jh * 128:(jh + 1) * 128] for g in range(S5_GS)]
            for jj, v in enumerate(shuffle(yv)):
                o_ref[pl.ds(tok + jh * per_v + jj, 8, stride=stride), :] = v

    tiles(regroup_out)


def _s5_core(p2, r, st, tab, h0):
    blk = lambda *tail: pl.BlockSpec((S5_GS,) + tail, lambda s: (s,) + (0,) * len(tail))
    col = pl.BlockSpec((M_ALL, S5_GS * S5_GROUP), lambda s: (0, s))
    return pl.pallas_call(
        _s5_core_kernel,
        grid=(G_C // S5_GS,),
        in_specs=[col, blk(S5_CW, 2 * S5_CW), blk(2, S5_CW, S5_PW),
                  blk(2, 2, S5_TAB, P_C), blk(2, DEC_BATCH, S5_PW)],
        out_specs=[col, blk(2, BATCH, S5_PW)],
        out_shape=[jax.ShapeDtypeStruct((M_ALL, MIX), F32),
                   jax.ShapeDtypeStruct((G_C, 2, BATCH, S5_PW), F32)],
        scratch_shapes=[pltpu.VMEM((S5_GS, S5_ROWS, S5_CW), F32),
                        pltpu.VMEM((S5_GS, S5_ROWS, S5_CW), F32),
                        pltpu.VMEM((S5_GS, 2, S5_ROWS, S5_PW), F32),
                        pltpu.VMEM((S5_GS, 2, S5_ROWS, S5_PW), F32)],
        compiler_params=_params("parallel"),
        name="s5_core",
    )(p2, r, st, tab, h0)


def _s5_out_kernel(y_ref, u_ref, d_ref, w_ref, o_ref):
    z = _gelu_tanh(y_ref[...] + d_ref[...] * u_ref[...])
    o_ref[...] = z * _sigmoid(jnp.dot(z.astype(BF16), w_ref[...], preferred_element_type=F32))


def _s5_out(y_tok, p2, d_skip, w_glu_bf16):
    tm = 512
    return pl.pallas_call(
        _s5_out_kernel,
        grid=(M_ALL // tm,),
        in_specs=[pl.BlockSpec((tm, MIX), lambda i: (i, 0)),
                  pl.BlockSpec((tm, MIX), lambda i: (i, 0)),
                  pl.BlockSpec((1, MIX), lambda i: (0, 0)),
                  pl.BlockSpec((MIX, MIX), lambda i: (0, 0))],
        out_specs=pl.BlockSpec((tm, MIX), lambda i: (i, 0)),
        out_shape=jax.ShapeDtypeStruct((M_ALL, MIX), F32),
        compiler_params=_params("parallel"),
        name="s5_out",
    )(y_tok, p2, d_skip.reshape(1, MIX), w_glu_bf16)


def _s5_tables(lam_re, lam_im, log_dt, b_re, b_im, c_re, c_im):
    dt = jnp.exp(log_dt)[:, None, :, None]
    k = jnp.arange(S5_TAB, dtype=F32)[None, :, None, None]
    mag = jnp.exp(lam_re[:, None] * dt * k)
    pw_re = mag * jnp.cos(lam_im[:, None] * dt * k)
    pw_im = mag * jnp.sin(lam_im[:, None] * dt * k)
    ab_re, ab_im = pw_re[:, 1], pw_im[:, 1]
    den = lam_re * lam_re + lam_im * lam_im
    nr = ab_re - 1.0
    f_re = (nr * lam_re + ab_im * lam_im) / den
    f_im = (ab_im * lam_re - nr * lam_im) / den
    rows = lambda b, c, f: jnp.concatenate(
        [jnp.swapaxes(b, -1, -2), c, jnp.broadcast_to(f[:, :, None, :], (2, G_C, 8, P_C))], axis=2)
    prm = jnp.stack([rows(b_re, c_re, f_re), rows(b_im, c_im, f_im)], axis=1)
    tab = jnp.stack([pw_re, pw_im], axis=1)
    return jnp.transpose(prm, (2, 0, 1, 3, 4)), jnp.transpose(tab, (3, 0, 1, 2, 4))


def _s5_mixer(p2, tables, d_skip, w_glu_bf16, state_re, state_im):
    prm, tab = tables
    r, st = _s5_params(prm, tab)
    h0 = jnp.concatenate([state_re, state_im], axis=-1)
    y_tok, hfin = _s5_core(p2, r, st, tab, jnp.transpose(h0, (2, 1, 0, 3)))
    oc = _s5_out(y_tok, p2, d_skip, w_glu_bf16)
    hfin = jnp.transpose(hfin, (2, 1, 0, 3))
    return oc, hfin[..., :P_C], hfin[..., P_C:]


def _ret_kernel(*refs, t, latent):
    if latent:
        (q_ref, k_ref, v_ref, g_ref, dl_ref, lnw_ref, lnb_ref, r0_ref, cos_ref, sin_ref, _,
         o_ref, ob_ref) = refs
    else:
        q_ref, k_ref, v_ref, g_ref, dl_ref, lnw_ref, lnb_ref, o_ref, rfin_ref, ob_ref = refs
    n = t // CHUNK
    w = HD_D
    jf = lax.broadcasted_iota(jnp.int32, (CHUNK, CHUNK), 0).astype(F32)
    kf = lax.broadcasted_iota(jnp.int32, (CHUNK, CHUNK), 1).astype(F32)
    diff = jf - kf
    tabs = []
    for dr in range(2):
        for h in range(H_D):
            l = -_softplus(-dl_ref[h, dr:dr + 1, :])
            if dr == 0:
                dmat = jnp.where(diff >= 0, jnp.exp(l * jnp.maximum(diff, 0.0)), 0.0)
                xi = jnp.exp(l * (jf + 1.0))
                zeta = jnp.exp(l * (CHUNK - 1.0 - jf))
            else:
                dmat = jnp.where(diff < 0, jnp.exp(l * jnp.maximum(-diff, 0.0)), 0.0)
                xi = jnp.exp(l * (CHUNK - jf))
                zeta = jnp.exp(l * jf)
            tabs.append((dmat, xi, zeta, jnp.exp(l * CHUNK)))
    chains = [(dr, h) for dr in range(2) for h in range(H_D)]

    def body(ci, states):
        q, k, v, where = [], [], [], []
        for dr, h in chains:
            cj = ci if dr == 0 else n - 1 - ci
            rows = pl.ds(pl.multiple_of(cj * CHUNK, CHUNK), CHUNK)
            cols = slice(h * w, (h + 1) * w)
            qi = q_ref[rows, cols]
            ki = k_ref[rows, cols] * (HD_D ** -0.5)
            if latent:
                qi = _rope_pairs(qi, cos_ref[rows, :], sin_ref[rows, :], HD_D // 2)
                ki = _rope_pairs(ki, cos_ref[rows, :], sin_ref[rows, :], HD_D // 2)
            q.append(qi.astype(BF16))
            k.append(ki)
            v.append(v_ref[rows, cols].astype(BF16))
            where.append((o_ref if dr == 0 else ob_ref, rows, cols))
        nch = range(len(chains))
        inner = [(_bdot(q[i], k[i], NT) * tabs[i][0]).astype(BF16) for i in nch]
        cross = [_bdot(q[i], states[i]) * tabs[i][1] for i in nch]
        kz = [(k[i] * tabs[i][2]).astype(BF16) for i in nch]
        out = [jnp.dot(inner[i], v[i], preferred_element_type=F32) + cross[i] for i in nch]
        new_states = [states[i] * tabs[i][3] + lax.dot_general(kz[i], v[i], TN, preferred_element_type=F32)
                      for i in nch]
        for (dst, rows, cols), y in zip(where, out):
            dst[rows, cols] = y
        return tuple(new_states)

    if latent:
        lax.fori_loop(0, n, body, tuple(r0_ref[0, dr, h] for dr, h in chains))
    else:
        fin = lax.fori_loop(0, n, body, tuple(jnp.zeros((w, w), F32) for _ in chains))
        for i, (dr, h) in enumerate(chains):
            rfin_ref[0, dr, h] = fin[i]
    for h in range(H_D):
        cols = slice(h * w, (h + 1) * w)
        y = o_ref[:, cols] + ob_ref[:, cols]
        yc = y - jnp.mean(y, axis=-1, keepdims=True)
        yn = yc * lax.rsqrt(jnp.mean(yc * yc, axis=-1, keepdims=True) + RET_GN_EPS)
        o_ref[:, cols] = (yn * lnw_ref[:, cols] + lnb_ref[:, cols]) * _silu(g_ref[:, cols])


def _retention(p, decay_logit, ln_w, ln_b, latent, r0=None, rope=None, dst=None):
    if latent:
        nseq, t, rb0 = DEC_BATCH, DEC_SEQ, M_CTX // DEC_SEQ
    else:
        nseq, t, rb0 = BATCH, SEQ, 0
    w = HD_D
    dl = jnp.broadcast_to(decay_logit.T[:, :, None], (H_D, 2, w))
    full = lambda shape: pl.BlockSpec(shape, lambda b: (0,) * len(shape))
    st_spec = pl.BlockSpec((1, 2, H_D, w, w), lambda b: (b, 0, 0, 0, 0))
    in_specs = [pl.BlockSpec((t, MIX), lambda b, j=j: (rb0 + b, j)) for j in (1, 2, 3, 4)]
    in_specs += [full((H_D, 2, w)), full((1, MIX)), full((1, MIX))]
    args = [p, p, p, p, dl, ln_w.reshape(1, MIX), ln_b.reshape(1, MIX)]
    o_spec = pl.BlockSpec((t, MIX), lambda b: (rb0 + b, 0))
    o_shape = jax.ShapeDtypeStruct((M_ALL, MIX), F32)
    if latent:
        in_specs += [st_spec, full((t, w)), full((t, w)), pl.BlockSpec(memory_space=pl.ANY)]
        args += [r0, *rope, dst]
        out_specs, out_shape, aliases = o_spec, o_shape, {len(args) - 1: 0}
    else:
        out_specs = [o_spec, st_spec]
        out_shape = [o_shape, jax.ShapeDtypeStruct((nseq, 2, H_D, w, w), F32)]
        aliases = {}
    return pl.pallas_call(
        functools.partial(_ret_kernel, t=t, latent=latent),
        grid=(nseq,),
        in_specs=in_specs, out_specs=out_specs, out_shape=out_shape,
        scratch_shapes=[pltpu.VMEM((t, MIX), F32)],
        input_output_aliases=aliases,
        compiler_params=_params("parallel"),
        name="retention_lat" if latent else "retention_ctx",
    )(*args)


def _rope_tables(n_tok, dim, reps):
    rows = n_tok // GRID_W
    n_freq = dim // 4
    inv = 1.0 / (ROPE_THETA ** (jnp.arange(n_freq, dtype=F32) / n_freq))
    row = jnp.repeat(jnp.arange(rows, dtype=F32), GRID_W)
    col = jnp.tile(jnp.arange(GRID_W, dtype=F32), rows)
    ang = jnp.concatenate([row[:, None] * inv, col[:, None] * inv], axis=-1)
    cos, sin = jnp.cos(ang), jnp.sin(ang)
    return jnp.tile(jnp.concatenate([cos, cos], axis=1), (1, reps)), \
        jnp.tile(jnp.concatenate([-sin, sin], axis=1), (1, reps))


def kernel(x_prompt, x_sample, cache_k_ab, cache_v_ab, state_rwkv, state_s5_re, state_s5_im, state_ret, c, c_ctx, norm1_g, norm2_g, w_mod, b_mod, w_ff_gate, w_ff_up, w_ff_down, w_in_ab, w_out_ab, qk_gain_a, lambda_qk, subln_g, rwkv_mu, rwkv_k_k, rwkv_k_a, rwkv_r_k, rwkv_w0, rwkv_w_up, rwkv_a0, rwkv_a_up, rwkv_g_up, rwkv_ln_w, rwkv_ln_b, w_in_cd, w_out_cd, s5_lam_re, s5_lam_im, s5_log_dt, s5_b_re, s5_b_im, s5_c_re, s5_c_im, s5_d, s5_w_glu, ret_decay_logit, ret_ln_w, ret_ln_b):
    d = D_MODEL
    xs = [x_prompt.reshape(M_CTX, d), x_sample.reshape(M_LAT, d)]
    cvec = jnp.zeros((MOD_ROWS, d), F32).at[0].set(c_ctx).at[1:1 + DEC_BATCH].set(c)
    mods = _modulation(cvec, w_mod, b_mod)

    lam_init = 0.8 - 0.6 * math.exp(-0.3 * 0)
    p = _norm_linear(xs, norm1_g[0], mods[0], w_in_ab[0].astype(BF16))
    gain2 = jnp.tile(qk_gain_a[0], (1, 2))
    rope_a = _rope_tables(DEC_SEQ, HD_A, 2)
    ck = cache_k_ab[:, 0].reshape(DEC_BATCH * PAST_LEN, MIX)
    cv = cache_v_ab[:, 0].reshape(DEC_BATCH * PAST_LEN, MIX)
    oa, k_ctx = _attention(p, gain2, lambda_qk[0], subln_g[0], lam_init, latent=False)
    oa = _attention(p, gain2, lambda_qk[0], subln_g[0], lam_init, latent=True,
                    cache_k=ck, cache_v=cv, rope=rope_a, dst=oa)
    rw_prm = (rwkv_mu[0], rwkv_k_k[0], rwkv_k_a[0], rwkv_r_k[0], rwkv_w0[0], rwkv_w_up[0],
              rwkv_a0[0], rwkv_a_up[0], rwkv_g_up[0])
    s0_lat = jnp.swapaxes(state_rwkv[:, 0], -1, -2)
    ob, sfin_ctx = _rwkv_scan(_rwkv_prep(p, rw_prm, latent=False), rwkv_ln_w[0], rwkv_ln_b[0], latent=False)
    ob = _rwkv_scan(_rwkv_prep(p, rw_prm, latent=True), rwkv_ln_w[0], rwkv_ln_b[0], latent=True,
                    s0t=s0_lat, dst=ob)
    x = _mix_ffn(xs, oa, ob, w_out_ab[0].astype(BF16), norm2_g[0], mods[0], w_ff_gate[0].astype(BF16),
                 w_ff_up[0].astype(BF16), w_ff_down[0].astype(BF16), split_out=False)

    p2 = _norm_linear([x], norm1_g[1], mods[1], w_in_cd[0].astype(BF16))
    tables = _s5_tables(s5_lam_re[0], s5_lam_im[0], s5_log_dt[0], s5_b_re[0], s5_b_im[0],
                        s5_c_re[0], s5_c_im[0])
    oc, s5_fin_re, s5_fin_im = _s5_mixer(p2, tables, s5_d[0], s5_w_glu[0].astype(BF16),
                                         state_s5_re[:, 0], state_s5_im[:, 0])
    rope_d = _rope_tables(DEC_SEQ, HD_D, 1)
    od, rfin = _retention(p2, ret_decay_logit[0], ret_ln_w[0], ret_ln_b[0], latent=False)
    od = _retention(p2, ret_decay_logit[0], ret_ln_w[0], ret_ln_b[0], latent=True, r0=state_ret[:, 0],
                    rope=rope_d, dst=od)
    y_ctx, y_lat = _mix_ffn([x], oc, od, w_out_cd[0].astype(BF16), norm2_g[1], mods[1],
                            w_ff_gate[1].astype(BF16), w_ff_up[1].astype(BF16), w_ff_down[1].astype(BF16),
                            split_out=True)
    y_prompt = y_ctx.reshape(BATCH, SEQ, d)
    y_sample = y_lat.reshape(DEC_BATCH, DEC_SEQ, d)
    new_k = k_ctx.reshape(BATCH, 1, SEQ, H_A, 2, HD_A)
    new_v = p[:M_CTX, 2 * MIX:3 * MIX].reshape(BATCH, 1, SEQ, H_A, VD_A)
    new_rwkv = sfin_ctx[:, None]
    new_s5_re = s5_fin_re[:, None]
    new_s5_im = s5_fin_im[:, None]
    new_ret = rfin[:, None]
    return (y_prompt, y_sample, new_k, new_v, new_rwkv, new_s5_re, new_s5_im, new_ret)
```

```python
import functools
import math

import numpy as np
import jax
import jax.numpy as jnp
from jax import lax
from jax.experimental import pallas as pl
from jax.experimental.pallas import tpu as pltpu

F32 = jnp.float32
BF16 = jnp.bfloat16
HIGHEST = lax.Precision.HIGHEST

D_MODEL = 1024
BATCH = 32
SEQ = 256
DEC_BATCH = 2
DEC_SEQ = 1024
PAST_LEN = 256
GRID_W = 64
H_A = 4
HD_A = 64
VD_A = 128
H_B = 8
HD_B = 64
MIX = 512
LORA_W = 64
LORA_A = 64
LORA_G = 128
S5_GROUP = 16
G_C = 32
P_C = 64
S5_STATE = G_C * P_C
H_D = 4
HD_D = 128
CHUNK = 128
D_FF = 2816
IN_AB = 3328
IN_CD = 2560
ROPE_THETA = 10000.0
NORM_EPS = 1e-6
RWKV_GN_EPS = 64e-5
RET_GN_EPS = 1e-5

M_CTX = BATCH * SEQ
M_LAT = DEC_BATCH * DEC_SEQ
M_ALL = M_CTX + M_LAT
MOD_ROWS = 8
RWKV_CHUNK = 64
VMEM_LIMIT = 56 * 1024 * 1024

NN = (((1,), (0,)), ((), ()))
NT = (((1,), (1,)), ((), ()))
TN = (((0,), (0,)), ((), ()))


def _params(*sem):
    return pltpu.CompilerParams(dimension_semantics=sem, vmem_limit_bytes=VMEM_LIMIT)


def _bdot(a, b, dims=NN):
    return lax.dot_general(a.astype(BF16), b.astype(BF16), dims, preferred_element_type=F32)


def _hdot(a, b, dims=NN):
    return lax.dot_general(a, b, dims, precision=HIGHEST, preferred_element_type=F32)


def _split_dot(x, m):
    hi = x.astype(BF16)
    lo = (x - hi.astype(F32)).astype(BF16)
    return (jnp.dot(hi, m, preferred_element_type=F32) + jnp.dot(lo, m, preferred_element_type=F32))


def _seg_matrix(n, shift, val):
    r = lax.broadcasted_iota(jnp.int32, (n, n), 0) >> shift
    c = lax.broadcasted_iota(jnp.int32, (n, n), 1) >> shift
    return jnp.where(r == c, val, 0.0).astype(BF16)


def _sigmoid(x):
    return jax.nn.sigmoid(x)


def _silu(x):
    return x * jax.nn.sigmoid(x)


def _softplus(x):
    return jnp.maximum(x, 0.0) + jnp.log(1.0 + jnp.exp(-jnp.abs(x)))


def _mod_row(tile, tm):
    r0 = tile * tm
    return jnp.where(r0 < M_CTX, 0, 1 + (r0 - M_CTX) // DEC_SEQ)


def _norm_mod(x, g, sc_ref, sh_ref, row):
    y = x * lax.rsqrt(jnp.mean(x * x, axis=-1, keepdims=True) + NORM_EPS) * g
    return y * (1.0 + sc_ref[pl.ds(row, 1), :]) + sh_ref[pl.ds(row, 1), :]


def _mod_kernel(c_ref, w_ref, b_ref, o_ref):
    o_ref[0] = _hdot(_silu(c_ref[...]), w_ref[0]) + b_ref[0]


def _modulation(cvec, w_mod, b_mod):
    depth, d, n6 = w_mod.shape
    tn = 1536
    return pl.pallas_call(
        _mod_kernel,
        grid=(depth, n6 // tn),
        in_specs=[pl.BlockSpec((MOD_ROWS, d), lambda l, j: (0, 0)),
                  pl.BlockSpec((1, d, tn), lambda l, j: (l, 0, j)),
                  pl.BlockSpec((1, 1, tn), lambda l, j: (l, 0, j))],
        out_specs=pl.BlockSpec((1, MOD_ROWS, tn), lambda l, j: (l, 0, j)),
        out_shape=jax.ShapeDtypeStruct((depth, MOD_ROWS, n6), F32),
        compiler_params=_params("parallel", "parallel"),
        name="modulation",
    )(cvec, w_mod, b_mod.reshape(depth, 1, n6))


def _tile_specs(tm, d):
    nc = M_CTX // tm
    return [pl.BlockSpec((tm, d), lambda i: (jnp.minimum(i, nc - 1), 0)),
            pl.BlockSpec((tm, d), lambda i: (jnp.maximum(i - nc, 0), 0))]


def _pick_rows(refs, tm):
    if len(refs) == 1:
        return refs[0][...]
    return jnp.where(pl.program_id(0) < M_CTX // tm, refs[0][...], refs[1][...])


def _norm_linear_kernel(*refs, tm):
    *x_refs, g_ref, sc_ref, sh_ref, w_ref, o_ref = refs
    row = _mod_row(pl.program_id(0), tm)
    h = _norm_mod(_pick_rows(x_refs, tm), g_ref[...], sc_ref, sh_ref, row)
    o_ref[...] = jnp.dot(h.astype(BF16), w_ref[...], preferred_element_type=F32)


def _norm_linear(xs, g, mods, w_bf16):
    tm = 512
    d = D_MODEL
    n = w_bf16.shape[1]
    x_specs = _tile_specs(tm, d) if len(xs) == 2 else [pl.BlockSpec((tm, d), lambda i: (i, 0))]
    return pl.pallas_call(
        functools.partial(_norm_linear_kernel, tm=tm),
        grid=(M_ALL // tm,),
        in_specs=x_specs + [pl.BlockSpec((1, d), lambda i: (0, 0)),
                            pl.BlockSpec((MOD_ROWS, d), lambda i: (0, 1)),
                            pl.BlockSpec((MOD_ROWS, d), lambda i: (0, 0)),
                            pl.BlockSpec((d, n), lambda i: (0, 0))],
        out_specs=pl.BlockSpec((tm, n), lambda i: (i, 0)),
        out_shape=jax.ShapeDtypeStruct((M_ALL, n), F32),
        compiler_params=_params("arbitrary"),
        name="norm_linear",
    )(*xs, g.reshape(1, d), mods, mods, w_bf16)


def _mix_ffn_kernel(*refs, tm, ck, n_in, n_out):
    x_refs = refs[:n_in]
    (a_ref, b_ref, wa_ref, wb_ref, g1_ref, ng_ref, sc_ref, sh_ref, g2_ref,
     wg_ref, wu_ref, wd_ref) = refs[n_in:n_in + 12]
    o_refs = refs[n_in + 12:]
    row = _mod_row(pl.program_id(0), tm)
    mix = (jnp.dot(a_ref[...].astype(BF16), wa_ref[...], preferred_element_type=F32)
           + jnp.dot(b_ref[...].astype(BF16), wb_ref[...], preferred_element_type=F32))
    x = _pick_rows(x_refs, tm) + g1_ref[pl.ds(row, 1), :] * mix
    h = _norm_mod(x, ng_ref[...], sc_ref, sh_ref, row).astype(BF16)
    acc = jnp.zeros((tm, D_MODEL), F32)
    for c in range(D_FF // ck):
        gg = jnp.dot(h, wg_ref[:, c * ck:(c + 1) * ck], preferred_element_type=F32)
        uu = jnp.dot(h, wu_ref[:, c * ck:(c + 1) * ck], preferred_element_type=F32)
        act = (_silu(gg) * uu).astype(BF16)
        acc = acc + jnp.dot(act, wd_ref[c * ck:(c + 1) * ck, :], preferred_element_type=F32)
    y = x + g2_ref[pl.ds(row, 1), :] * acc
    if n_out == 1:
        o_refs[0][...] = y
    else:
        is_ctx = pl.program_id(0) < M_CTX // tm

        @pl.when(is_ctx)
        def _():
            o_refs[0][...] = y

        @pl.when(jnp.logical_not(is_ctx))
        def _():
            o_refs[1][...] = y


def _mix_ffn(xs, oa, ob, w_out_bf16, norm_g, mods, wg, wu, wd, split_out):
    tm, ck = 512, 256
    d = D_MODEL
    row_spec = lambda n: pl.BlockSpec((tm, n), lambda i: (i, 0))
    const = lambda shape, idx: pl.BlockSpec(shape, lambda i: idx, pipeline_mode=pl.Buffered(1))
    mod = lambda j: pl.BlockSpec((MOD_ROWS, d), lambda i: (0, j))
    x_specs = _tile_specs(tm, d) if len(xs) == 2 else [row_spec(d)]
    if split_out:
        out_specs = _tile_specs(tm, d)
        out_shape = [jax.ShapeDtypeStruct((M_CTX, d), F32), jax.ShapeDtypeStruct((M_LAT, d), F32)]
    else:
        out_specs, out_shape = row_spec(d), jax.ShapeDtypeStruct((M_ALL, d), F32)
    return pl.pallas_call(
        functools.partial(_mix_ffn_kernel, tm=tm, ck=ck, n_in=len(xs), n_out=2 if split_out else 1),
        grid=(M_ALL // tm,),
        in_specs=x_specs + [row_spec(MIX), row_spec(MIX),
                            const((MIX, d), (0, 0)), const((MIX, d), (1, 0)),
                            mod(2),
                            pl.BlockSpec((1, d), lambda i: (0, 0)),
                            mod(4), mod(3), mod(5),
                            const((d, D_FF), (0, 0)), const((d, D_FF), (0, 0)), const((D_FF, d), (0, 0))],
        out_specs=out_specs, out_shape=out_shape,
        compiler_params=_params("arbitrary"),
        name="mix_ffn",
    )(*xs, oa, ob, w_out_bf16, w_out_bf16, mods, norm_g.reshape(1, d), mods, mods, mods, wg, wu, wd)


def _qk_norm(x, gain, segm):
    ms = _split_dot(x * x, segm)
    return x * lax.rsqrt(ms + NORM_EPS) * gain


def _rope_pairs(x, cosf, sinf, half):
    lane = lax.broadcasted_iota(jnp.int32, x.shape, 1)
    first = (lane & (2 * half - 1)) < half
    n = x.shape[1]
    partner = jnp.where(first, pltpu.roll(x, n - half, axis=1), pltpu.roll(x, half, axis=1))
    return x * cosf + partner * sinf


def _attn_kernel(*refs, latent, lam_init):
    if latent:
        (q_ref, k_ref, v_ref, ck_ref, cv_ref, cosq_ref, sinq_ref, cosk_ref, sink_ref,
         gain_ref, lam_ref, sub_ref, _, o_ref, kall, vall) = refs
    else:
        q_ref, k_ref, v_ref, gain_ref, lam_ref, sub_ref, o_ref, kn_ref, kall, vall = refs
    w = 2 * HD_A
    segm = _seg_matrix(w, 6, 1.0 / HD_A)
    gains = gain_ref[...]
    cols = [slice(h * w, (h + 1) * w) for h in range(H_A)]

    @pl.when(pl.program_id(1) == 0)
    def _():
        k = [_qk_norm(k_ref[:, c], gains[1:2], segm) for c in cols]
        if latent:
            k = [_rope_pairs(x, cosk_ref[...], sink_ref[...], HD_A // 2) for x in k]
            for c, x in zip(cols, k):
                kall[0:PAST_LEN, c] = ck_ref[:, c].astype(BF16)
                kall[PAST_LEN:, c] = x.astype(BF16)
                vall[0:PAST_LEN, c] = cv_ref[:, c].astype(BF16)
                vall[PAST_LEN:, c] = v_ref[:, c].astype(BF16)
        else:
            for c, x in zip(cols, k):
                kn_ref[:, c] = x
                kall[:, c] = x.astype(BF16)
                vall[:, c] = v_ref[:, c].astype(BF16)

    q = [_qk_norm(q_ref[:, c], gains[0:1], segm) for c in cols]
    if latent:
        q = [_rope_pairs(x, cosq_ref[...], sinq_ref[...], HD_A // 2) for x in q]
    lv = lam_ref[...]
    lam = (jnp.exp(jnp.sum(lv[0:1] * lv[1:2], axis=1, keepdims=True))
           - jnp.exp(jnp.sum(lv[2:3] * lv[3:4], axis=1, keepdims=True)) + lam_init)
    scale = HD_A ** -0.5
    comp0 = lax.broadcasted_iota(jnp.int32, q[0].shape, 1) < HD_A
    qc = [jnp.where(comp0, *sel).astype(BF16) for x in q for sel in ((x, 0.0), (0.0, x))]
    s = [lax.dot_general(qc[i], kall[:, cols[i // 2]], NT, preferred_element_type=F32) * scale
         for i in range(2 * H_A)]
    e = [jnp.exp(x - jnp.max(x, axis=-1, keepdims=True)) for x in s]
    p = [x / jnp.sum(x, axis=-1, keepdims=True) for x in e]
    att = [(p[2 * h] - lam * p[2 * h + 1]).astype(BF16) for h in range(H_A)]
    o = [jnp.dot(att[h], vall[:, cols[h]], preferred_element_type=F32) for h in range(H_A)]
    o = [x * lax.rsqrt(jnp.mean(x * x, axis=-1, keepdims=True) + NORM_EPS) * sub_ref[...] for x in o]
    for c, x in zip(cols, o):
        o_ref[:, c] = x * (1.0 - lam_init)


def _attention(p, gain2, lambda_qk, subln_g, lam_init, latent, cache_k=None, cache_v=None, rope=None,
               dst=None):
    w = 2 * HD_A
    if latent:
        nseq, t, tq, rb0, s_len = DEC_BATCH, DEC_SEQ, 128, M_CTX // DEC_SEQ, PAST_LEN + DEC_SEQ
    else:
        nseq, t, tq, rb0, s_len = BATCH, SEQ, SEQ, 0, SEQ
    nq = t // tq
    qoff = rb0 * nq
    full = lambda shape: pl.BlockSpec(shape, lambda b, i: (0,) * len(shape))
    in_specs = [pl.BlockSpec((tq, MIX), lambda b, i: (qoff + b * nq + i, 0)),
                pl.BlockSpec((t, MIX), lambda b, i: (rb0 + b, 1)),
                pl.BlockSpec((t, MIX), lambda b, i: (rb0 + b, 2))]
    args = [p, p, p]
    if latent:
        cosf, sinf = rope
        in_specs += [pl.BlockSpec((PAST_LEN, MIX), lambda b, i: (b, 0)),
                     pl.BlockSpec((PAST_LEN, MIX), lambda b, i: (b, 0)),
                     pl.BlockSpec((tq, w), lambda b, i: (i, 0)),
                     pl.BlockSpec((tq, w), lambda b, i: (i, 0)),
                     full((t, w)), full((t, w))]
        args += [cache_k, cache_v, cosf, sinf, cosf, sinf]
    in_specs += [full((2, w)), full((4, HD_A)), full((1, w))]
    args += [gain2, lambda_qk, subln_g.reshape(1, w)]
    o_spec = pl.BlockSpec((tq, MIX), lambda b, i: (qoff + b * nq + i, 0))
    o_shape = jax.ShapeDtypeStruct((M_ALL, MIX), F32)
    aliases = {}
    if latent:
        out_specs, out_shape = o_spec, o_shape
        in_specs.append(pl.BlockSpec(memory_space=pl.ANY))
        args.append(dst)
        aliases = {len(args) - 1: 0}
    else:
        out_specs = [o_spec, pl.BlockSpec((t, MIX), lambda b, i: (b, 0))]
        out_shape = [o_shape, jax.ShapeDtypeStruct((nseq * t, MIX), F32)]
    return pl.pallas_call(
        functools.partial(_attn_kernel, latent=latent, lam_init=lam_init),
        grid=(nseq, nq),
        in_specs=in_specs, out_specs=out_specs, out_shape=out_shape,
        scratch_shapes=[pltpu.VMEM((s_len, MIX), BF16), pltpu.VMEM((s_len, MIX), BF16)],
        input_output_aliases=aliases,
        compiler_params=_params("parallel", "arbitrary"),
        name="diff_attention_lat" if latent else "diff_attention_ctx",
    )(*args)


def _centred_shift(x, mu):
    t = x.shape[0]
    row = lax.broadcasted_iota(jnp.int32, x.shape, 0)
    prev = jnp.where(row == 0, 0.0, pltpu.roll(x, 1, axis=0))
    nxt = jnp.where(row == t - 1, 0.0, pltpu.roll(x, t - 1, axis=0))
    return x + (0.5 * (prev + nxt) - x) * mu


def _seg_sum(x, segm):
    return jnp.concatenate([_split_dot(x[:, j * 128:(j + 1) * 128], segm) for j in range(x.shape[1] // 128)],
                           axis=1)


def _rwkv_prep_kernel(r_ref, k_ref, v_ref, l_ref, mur_ref, muk_ref, muv_ref, mul_ref,
                      kk_ref, ka_ref, rk_ref, w0_ref, wup_ref, a0_ref, aup_ref, gup_ref,
                      ro_ref, ldf_ref, ldb_ref, kbo_ref, vbo_ref, kko_ref, ao_ref, gate_ref, bonus_ref):
    seg1 = _seg_matrix(128, 6, 1.0)
    r = _centred_shift(r_ref[...], mur_ref[...])
    kb = _centred_shift(k_ref[...], muk_ref[...])
    vb = _centred_shift(v_ref[...], muv_ref[...])
    lo = _centred_shift(l_ref[...], mul_ref[...])
    xw = lo[:, 0:LORA_W]
    xa = lo[:, LORA_W:LORA_W + LORA_A]
    xg = lo[:, LORA_W + LORA_A:]
    kk = kb * kk_ref[...]
    kk = kk * lax.rsqrt(_seg_sum(kk * kk, seg1) + 1e-12)
    a = _sigmoid(a0_ref[...] + _bdot(xa, aup_ref[...]))
    kb2 = kb * (1.0 + (a - 1.0) * ka_ref[...])
    lw = jnp.tanh(xw)
    for dr, ld_ref in enumerate((ldf_ref, ldb_ref)):
        z = w0_ref[dr:dr + 1, :] + _bdot(lw, wup_ref[dr])
        logw = -_softplus(-z) - 0.5
        ld_ref[...] = -jnp.exp(logw)
    gate_ref[...] = _bdot(_sigmoid(xg), gup_ref[...])
    bonus_ref[...] = _seg_sum(r * kb2 * rk_ref[...], seg1) * vb
    ro_ref[...] = r
    kbo_ref[...] = kb2
    vbo_ref[...] = vb
    kko_ref[...] = kk
    ao_ref[...] = a


N_PREP_IN = 16
N_PREP_OUT = 9


def _split2(x):
    hi = x.astype(BF16)
    return hi, (x - hi.astype(F32)).astype(BF16)


RWKV_INV_BLOCK = 16


def _rwkv_masks(c, rev):
    ti = lax.broadcasted_iota(jnp.int32, (c, c), 0)
    si = lax.broadcasted_iota(jnp.int32, (c, c), 1)
    tri = jnp.where((si >= ti) if rev else (si <= ti), 1.0, 0.0).astype(BF16)
    t4 = lax.broadcasted_iota(jnp.int32, (4 * c, 4 * c), 0)
    s4 = lax.broadcasted_iota(jnp.int32, (4 * c, 4 * c), 1)
    tm, sm = t4 & (c - 1), s4 & (c - 1)
    strict = (sm > tm) if rev else (sm < tm)
    incl = (sm >= tm) if rev else (sm <= tm)
    same_head = ((t4 // c) & 1) == ((s4 // c) & 1)
    top = t4 < 2 * c
    gmask = same_head & ((top & strict) | (~top & incl))
    t2 = lax.broadcasted_iota(jnp.int32, (2 * c, 2 * c), 0)
    s2 = lax.broadcasted_iota(jnp.int32, (2 * c, 2 * c), 1)
    same = lambda n: (t2 // n) == (s2 // n)
    levels = []
    n = RWKV_INV_BLOCK
    while n < c:
        levels.append(same(2 * n) & ~same(n))
        n *= 2
    f = lambda m: jnp.where(m, 1.0, 0.0)
    return tri, f(gmask), f(same(RWKV_INV_BLOCK)), tuple(f(m) for m in levels), f(same(c)), f(t2 == s2)


def _keep(mask01, x):
    return jnp.where(mask01 > 0.5, x, 0.0)


def _tri_inverse(a, diag_blk, levels, eye):
    n = a[0].shape[0]
    d = [_keep(diag_blk, x) for x in a]
    t = [eye + x for x in d]
    p = [_bdot(x, x) for x in d]
    for _ in range(int(math.log2(RWKV_INV_BLOCK)) - 2):
        res = [_bdot(jnp.concatenate([pi, ti], axis=0), pi) for pi, ti in zip(p, t)]
        p = [x[:n] for x in res]
        t = [ti + x[n:] for ti, x in zip(t, res)]
    t = [ti + _bdot(ti, pi) for ti, pi in zip(t, p)]
    for off in levels:
        half = [_bdot(ti, _keep(off, x)) for ti, x in zip(t, a)]
        t = [ti + _bdot(x, ti) for ti, x in zip(t, half)]
    return t


def _rwkv_pair_chunks(ins, sts, tris, gmasks, diag_blk, levels, same_head, eye):
    c, w = ins[0][0].shape
    hd = w // 2
    nch = range(len(ins))
    r, ld, kb, vb, kk, a = (list(z) for z in zip(*ins))
    split = [_split2(x) for x in ld]
    lcum = [jnp.dot(tris[i], split[i][0], preferred_element_type=F32)
            + jnp.dot(tris[i], split[i][1], preferred_element_type=F32) for i in nch]
    ltot = [jnp.sum(x, axis=0, keepdims=True) for x in ld]
    beta = [kk[i] * a[i] for i in nch]
    eneg = [jnp.exp(-x) for x in lcum]
    abar = [-kk[i] * jnp.exp(lcum[i] - ld[i]) for i in nch]
    rbar = [r[i] * jnp.exp(lcum[i]) for i in nch]
    bt = [(beta[i] * eneg[i]).astype(BF16) for i in nch]
    kt = [(kb[i] * eneg[i]).astype(BF16) for i in nch]
    vbb = [x.astype(BF16) for x in vb]
    head0 = lax.broadcasted_iota(jnp.int32, (c, w), 1) < hd
    pick = lambda res: jnp.where(head0, res[:c], res[c:])
    arst = [_bdot(jnp.concatenate([abar[i], rbar[i]], axis=0), sts[i], NT) for i in nch]
    lhs = [jnp.concatenate([jnp.where(head0, abar[i], 0.0), jnp.where(head0, 0.0, abar[i]),
                            jnp.where(head0, rbar[i], 0.0), jnp.where(head0, 0.0, rbar[i])], axis=0) for i in nch]
    g = [_keep(gmasks[i], _bdot(lhs[i], jnp.concatenate([bt[i], bt[i], kt[i], kt[i]], axis=0), NT))
         for i in nch]
    x = [arst[i][:c] + pick(_bdot(g[i][:2 * c, 2 * c:], jnp.concatenate([vbb[i], vbb[i]], axis=0))) for i in nch]
    tinv = _tri_inverse([gi[:2 * c, :2 * c] for gi in g], diag_blk, levels, eye)
    u = [pick(_bdot(tinv[i], jnp.concatenate([x[i], x[i]], axis=0))) for i in nch]
    ub = [z.astype(BF16) for z in u]
    y = [arst[i][c:] + pick(_bdot(g[i][2 * c:], jnp.concatenate([ub[i], ub[i], vbb[i], vbb[i]], axis=0)))
         for i in nch]
    erem = [jnp.exp(ltot[i] - lcum[i]) for i in nch]
    bkh = [jnp.concatenate([beta[i] * erem[i], kb[i] * erem[i]], axis=0) for i in nch]
    st_new = [jnp.exp(ltot[i]) * sts[i]
              + _keep(same_head, _bdot(jnp.concatenate([ub[i], vbb[i]], axis=0), bkh[i], TN)) for i in nch]
    return y, st_new


def _rwkv_scan_kernel(*refs, t, c, npair, latent):
    prep_in, (lnw_ref, lnb_ref), rest = refs[:N_PREP_IN], refs[N_PREP_IN:N_PREP_IN + 2], refs[N_PREP_IN + 2:]
    if latent:
        s0_ref, _, o_ref, *scratch = rest
    else:
        o_ref, sfin_ref, *scratch = rest
    *prep_out, yb_ref = scratch
    _rwkv_prep_kernel(*prep_in, *prep_out)
    r_ref, ldf_ref, ldb_ref, kb_ref, vb_ref, kk_ref, a_ref, gate_ref, bonus_ref = prep_out
    n = t // c
    w = 2 * HD_B
    masks = [_rwkv_masks(c, rev) for rev in (False, True)]

    def body(ci, states):
        ins, tris, gmasks, dsts = [], [], [], []
        for dr, (ld_ref, dst) in enumerate(((ldf_ref, o_ref), (ldb_ref, yb_ref))):
            cj = ci if dr == 0 else n - 1 - ci
            rows = pl.ds(pl.multiple_of(cj * c, c), c)
            for p in range(npair):
                cols = slice(p * w, (p + 1) * w)
                ins.append(tuple(ref[rows, cols] for ref in (r_ref, ld_ref, kb_ref, vb_ref, kk_ref, a_ref)))
                tris.append(masks[dr][0])
                gmasks.append(masks[dr][1])
                dsts.append((dst, rows, cols))
        ys, new_states = _rwkv_pair_chunks(ins, list(states), tris, gmasks, *masks[0][2:])
        for (dst, rows, cols), y in zip(dsts, ys):
            dst[rows, cols] = y
        return tuple(new_states)

    if latent:
        zero = jnp.zeros((HD_B, HD_B), F32)
        init = tuple(jnp.concatenate([jnp.concatenate([s0_ref[0, dr, 2 * p], zero], axis=1),
                                      jnp.concatenate([zero, s0_ref[0, dr, 2 * p + 1]], axis=1)], axis=0)
                     for dr in range(2) for p in range(npair))
    else:
        init = tuple(jnp.zeros((w, w), F32) for _ in range(2 * npair))
    fin = lax.fori_loop(0, n, body, init)
    segm = _seg_matrix(w, 6, 1.0 / HD_B)
    for p in range(npair):
        if not latent:
            for dr in range(2):
                st = fin[dr * npair + p]
                sfin_ref[0, dr, 2 * p] = st[:HD_B, :HD_B]
                sfin_ref[0, dr, 2 * p + 1] = st[HD_B:, HD_B:]
        cols = slice(p * w, (p + 1) * w)
        y = o_ref[:, cols] + yb_ref[:, cols]
        yc = y - _split_dot(y, segm)
        yn = yc * lax.rsqrt(_split_dot(yc * yc, segm) + RWKV_GN_EPS)
        o_ref[:, cols] = (yn * lnw_ref[:, cols] + lnb_ref[:, cols] + bonus_ref[:, cols]) * gate_ref[:, cols]


def _rwkv(p, prm, ln_w, ln_b, latent, s0=None, dst=None):
    mu, k_k, k_a, r_k, w0, w_up, a0, a_up, g_up = prm
    nseq, t, rb0 = (DEC_BATCH, DEC_SEQ, M_CTX // DEC_SEQ) if latent else (BATCH, SEQ, 0)
    npair = H_B // 2
    lw = LORA_W + LORA_A + LORA_G
    c0 = (IN_AB - 3 * MIX - lw) // MIX
    cl = (IN_AB - lw) // lw
    full = lambda *shape: pl.BlockSpec(shape, lambda b: (0,) * len(shape))
    st_spec = pl.BlockSpec((1, 2, H_B, HD_B, HD_B), lambda b: (b, 0, 0, 0, 0))
    o_spec = pl.BlockSpec((t, MIX), lambda b: (rb0 + b, 0))
    o_shape = jax.ShapeDtypeStruct((M_ALL, MIX), F32)
    in_specs = [pl.BlockSpec((t, MIX), lambda b: (rb0 + b, c0)),
                pl.BlockSpec((t, MIX), lambda b: (rb0 + b, c0 + 1)),
                pl.BlockSpec((t, MIX), lambda b: (rb0 + b, c0 + 2)),
                pl.BlockSpec((t, lw), lambda b: (rb0 + b, cl)),
                full(1, MIX), full(1, MIX), full(1, MIX), full(1, lw),
                full(1, MIX), full(1, MIX), full(1, MIX),
                full(2, MIX), full(2, LORA_W, MIX), full(1, MIX), full(LORA_A, MIX), full(LORA_G, MIX),
                full(1, MIX), full(1, MIX)]
    args = [p, p, p, p,
            mu[None, 0:MIX], mu[None, MIX:2 * MIX], mu[None, 2 * MIX:3 * MIX], mu[None, 3 * MIX:],
            k_k.reshape(1, MIX), k_a.reshape(1, MIX), r_k.reshape(1, MIX), w0, w_up,
            a0.reshape(1, MIX), a_up, g_up, ln_w.reshape(1, MIX), ln_b.reshape(1, MIX)]
    if latent:
        args += [s0, dst]
        in_specs += [st_spec, pl.BlockSpec(memory_space=pl.ANY)]
        out_specs, out_shape, aliases = o_spec, o_shape, {len(args) - 1: 0}
    else:
        out_specs = [o_spec, st_spec]
        out_shape = [o_shape, jax.ShapeDtypeStruct((nseq, 2, H_B, HD_B, HD_B), F32)]
        aliases = {}
    return pl.pallas_call(
        functools.partial(_rwkv_scan_kernel, t=t, c=RWKV_CHUNK, npair=npair, latent=latent),
        grid=(nseq,),
        in_specs=in_specs, out_specs=out_specs, out_shape=out_shape,
        scratch_shapes=[pltpu.VMEM((t, MIX), F32)] * (N_PREP_OUT + 1),
        input_output_aliases=aliases,
        compiler_params=_params("parallel"),
        name="rwkv_lat" if latent else "rwkv_ctx",
    )(*args)


S5_L = 16
S5_ROWS = M_ALL // S5_L
S5_CW = S5_L * S5_GROUP
S5_GS = 8
S5_PW = 2 * P_C
S5_TAB = 24
S5_PRM = 2 * S5_GROUP + 8
S5_NC_CTX = SEQ // S5_L
S5_NC_LAT = DEC_SEQ // S5_L
S5_CTX_ROWS = BATCH * S5_NC_CTX


def _gelu_tanh(x):
    return 0.5 * x * (1.0 + jnp.tanh(math.sqrt(2.0 / math.pi) * (x + 0.044715 * (x * x * x))))


def _cmul(ar, ai, br, bi):
    return ar * br - ai * bi, ar * bi + ai * br


def _dot3(a, b, dims):
    ah, al = _split2(a)
    bh, bl = _split2(b)
    d = lambda x, y: lax.dot_general(x, y, dims, preferred_element_type=F32)
    return d(ah, bh) + d(ah, bl) + d(al, bh)


def _s5_param_kernel(prm_ref, tab_ref, r_ref, st_ref):
    nl, ns = S5_L, S5_GROUP
    lane_blk = lax.broadcasted_iota(jnp.int32, (ns, S5_CW), 1) // ns
    t_rows = [jnp.zeros((ns, S5_CW), F32) for _ in range(nl)]
    e_parts = []
    for d in range(2):
        part = lambda lo, n: (prm_ref[0, d, 0, lo:lo + n, :], prm_ref[0, d, 1, lo:lo + n, :])
        bb = _cmul(*part(2 * ns, 1), *part(0, ns))
        cc = part(ns, ns)
        pw = lambda k: (tab_ref[0, d, 0, k:k + 1, :], tab_ref[0, d, 1, k:k + 1, :])
        steps = range(nl)
        if d == 0:
            ke, ks, kk = [nl - 1 - j for j in steps], [j + 1 for j in steps], list(steps)
        else:
            ke, ks, kk = list(steps), [nl - j for j in steps], [nl - 1 - j for j in steps]
        stack = lambda xs: (jnp.concatenate([x[0] for x in xs], axis=0), jnp.concatenate([x[1] for x in xs], axis=0))
        e_re, e_im = stack([_cmul(*bb, *pw(k)) for k in ke])
        s_re, s_im = stack([_cmul(*cc, *pw(k)) for k in ks])
        k_re, k_im = stack([_cmul(*cc, *pw(k)) for k in kk])
        st_ref[0, d] = jnp.concatenate([s_re, -s_im], axis=1).astype(BF16)
        krow = _dot3(bb[0], k_re, NT) - _dot3(bb[1], k_im, NT)
        for j in steps:
            if d == 0:
                shifted, keep = pltpu.roll(krow, ns * j, axis=1), lane_blk >= j
            else:
                shifted, keep = pltpu.roll(krow, (S5_CW - ns * (nl - 1 - j)) % S5_CW, axis=1), lane_blk <= j
            t_rows[j] = t_rows[j] + jnp.where(keep, shifted, 0.0)
        e_parts += [e_re, e_im]
    r_ref[0] = jnp.concatenate([jnp.concatenate(t_rows, axis=0)] + e_parts, axis=1).astype(BF16)


def _s5_params(prm, tab):
    return pl.pallas_call(
        _s5_param_kernel,
        grid=(G_C,),
        in_specs=[pl.BlockSpec((1, 2, 2, S5_PRM, P_C), lambda g: (g, 0, 0, 0, 0)),
                  pl.BlockSpec((1, 2, 2, S5_TAB, P_C), lambda g: (g, 0, 0, 0, 0))],
        out_specs=[pl.BlockSpec((1, S5_CW, 2 * S5_CW), lambda g: (g, 0, 0)),
                   pl.BlockSpec((1, 2, S5_CW, S5_PW), lambda g: (g, 0, 0, 0))],
        out_shape=[jax.ShapeDtypeStruct((G_C, S5_CW, 2 * S5_CW), BF16),
                   jax.ShapeDtypeStruct((G_C, 2, S5_CW, S5_PW), BF16)],
        compiler_params=_params("parallel"),
        name="s5_params",
    )(prm, tab)


def _s5_core_kernel(u_ref, r_ref, st_ref, tab_ref, h0_ref, o_ref, hfin_ref, ug_scr, yg_scr, e_scr, hp_scr):
    ns, nl = S5_GROUP, S5_L
    per_v = 128 // ns
    blk = lax.broadcasted_iota(jnp.int32, (8, 128), 1) // ns

    def merge(select):
        acc = select(0)
        for b in range(1, per_v):
            acc = jnp.where(blk == b, select(b), acc)
        return acc

    def shuffle(srcs):
        rolled = []
        for s in range(per_v):
            m = merge(lambda b: srcs[(b + s) % per_v])
            rolled.append(m if s == 0 else pltpu.roll(m, s * ns, axis=1))
        return [merge(lambda b: rolled[(b - a) % per_v]) for a in range(per_v)]

    def tiles(regroup):
        def ctx_tile(ti, _):
            c, b0 = ti // (BATCH // 8), (ti % (BATCH // 8)) * 8
            regroup(b0 * SEQ + c * nl, SEQ, pl.multiple_of(c * BATCH + b0, 8))
            return 0

        def lat_tile(ti, _):
            b, c0 = ti // (S5_NC_LAT // 8), (ti % (S5_NC_LAT // 8)) * 8
            regroup(M_CTX + b * DEC_SEQ + c0 * nl, nl, pl.multiple_of(S5_CTX_ROWS + b * S5_NC_LAT + c0, 8))
            return 0

        lax.fori_loop(0, S5_CTX_ROWS // 8, ctx_tile, 0, unroll=4)
        lax.fori_loop(0, (S5_ROWS - S5_CTX_ROWS) // 8, lat_tile, 0, unroll=4)

    def regroup_in(tok, stride, row):
        uj = [u_ref[pl.ds(tok + j, 8, stride=stride), :] for j in range(nl)]
        for jh in range(nl // per_v):
            for g, v in enumerate(shuffle(uj[jh * per_v:(jh + 1) * per_v])):
                ug_scr[g, pl.ds(row, 8), jh * 128:(jh + 1) * 128] = v

    tiles(regroup_in)

    chains = [(g, d) for g in range(S5_GS) for d in range(2)]
    for g in range(S5_GS):
        a = jnp.dot(ug_scr[g].astype(BF16), r_ref[g], preferred_element_type=F32)
        yg_scr[g] = a[:, 0:S5_CW]
        for d in range(2):
            e_scr[g, d] = a[:, S5_CW + d * S5_PW:S5_CW + (d + 1) * S5_PW]
    lam = []
    for g, d in chains:
        lr, li = (tab_ref[g, d, ri, nl:nl + 1, :] for ri in range(2))
        lam.append((jnp.concatenate([lr, lr], axis=1), jnp.concatenate([-li, li], axis=1)))

    def advance(c, hs, latent):
        nc = S5_NC_LAT if latent else S5_NC_CTX
        out = []
        for i, (g, d) in enumerate(chains):
            cc = c if d == 0 else nc - 1 - c
            rows = (pl.ds(S5_CTX_ROWS + cc, DEC_BATCH, stride=nc) if latent
                    else pl.ds(pl.multiple_of(cc * BATCH, BATCH), BATCH))
            hp_scr[g, d, rows, :] = hs[i]
            out.append(lam[i][0] * hs[i] + lam[i][1] * pltpu.roll(hs[i], P_C, axis=1) + e_scr[g, d, rows, :])
        return out

    def both(c, carry):
        hc, hl = carry
        return tuple(advance(c, hc, False)), tuple(advance(c, hl, True))

    h_ctx = tuple(jnp.zeros((BATCH, S5_PW), F32) for _ in chains)
    h_lat = tuple(h0_ref[g, d] for g, d in chains)
    h_ctx, h_lat = lax.fori_loop(0, S5_NC_CTX, both, (h_ctx, h_lat))
    lax.fori_loop(S5_NC_CTX, S5_NC_LAT, lambda c, hl: tuple(advance(c, hl, True)), h_lat)
    for i, (g, d) in enumerate(chains):
        hfin_ref[g, d] = h_ctx[i]
    for g in range(S5_GS):
        y = yg_scr[g]
        for d in range(2):
            y = y + lax.dot_general(hp_scr[g, d].astype(BF16), st_ref[g, d], NT, preferred_element_type=F32)
        yg_scr[g] = y

    def regroup_out(tok, stride, row):
        for jh in range(nl // per_v):
            yv = [yg_scr[g, pl.ds(row, 8), jh * 128:(jh + 1) * 128] for g in range(S5_GS)]
            for jj, v in enumerate(shuffle(yv)):
                o_ref[pl.ds(tok + jh * per_v + jj, 8, stride=stride), :] = v

    tiles(regroup_out)


def _s5_core(p2, r, st, tab, h0):
    blk = lambda *tail: pl.BlockSpec((S5_GS,) + tail, lambda s: (s,) + (0,) * len(tail))
    col = pl.BlockSpec((M_ALL, S5_GS * S5_GROUP), lambda s: (0, s))
    return pl.pallas_call(
        _s5_core_kernel,
        grid=(G_C // S5_GS,),
        in_specs=[col, blk(S5_CW, 2 * S5_CW), blk(2, S5_CW, S5_PW),
                  blk(2, 2, S5_TAB, P_C), blk(2, DEC_BATCH, S5_PW)],
        out_specs=[col, blk(2, BATCH, S5_PW)],
        out_shape=[jax.ShapeDtypeStruct((M_ALL, MIX), F32),
                   jax.ShapeDtypeStruct((G_C, 2, BATCH, S5_PW), F32)],
        scratch_shapes=[pltpu.VMEM((S5_GS, S5_ROWS, S5_CW), F32),
                        pltpu.VMEM((S5_GS, S5_ROWS, S5_CW), F32),
                        pltpu.VMEM((S5_GS, 2, S5_ROWS, S5_PW), F32),
                        pltpu.VMEM((S5_GS, 2, S5_ROWS, S5_PW), F32)],
        compiler_params=_params("parallel"),
        name="s5_core",
    )(p2, r, st, tab, h0)


def _s5_out_kernel(y_ref, u_ref, d_ref, w_ref, o_ref):
    z = _gelu_tanh(y_ref[...] + d_ref[...] * u_ref[...])
    o_ref[...] = z * _sigmoid(jnp.dot(z.astype(BF16), w_ref[...], preferred_element_type=F32))


def _s5_out(y_tok, p2, d_skip, w_glu_bf16):
    tm = 512
    return pl.pallas_call(
        _s5_out_kernel,
        grid=(M_ALL // tm,),
        in_specs=[pl.BlockSpec((tm, MIX), lambda i: (i, 0)),
                  pl.BlockSpec((tm, MIX), lambda i: (i, 0)),
                  pl.BlockSpec((1, MIX), lambda i: (0, 0)),
                  pl.BlockSpec((MIX, MIX), lambda i: (0, 0))],
        out_specs=pl.BlockSpec((tm, MIX), lambda i: (i, 0)),
        out_shape=jax.ShapeDtypeStruct((M_ALL, MIX), F32),
        compiler_params=_params("parallel"),
        name="s5_out",
    )(y_tok, p2, d_skip.reshape(1, MIX), w_glu_bf16)


def _s5_tables(lam_re, lam_im, log_dt, b_re, b_im, c_re, c_im):
    dt = jnp.exp(log_dt)[:, None, :, None]
    k = jnp.arange(S5_TAB, dtype=F32)[None, :, None, None]
    mag = jnp.exp(lam_re[:, None] * dt * k)
    pw_re = mag * jnp.cos(lam_im[:, None] * dt * k)
    pw_im = mag * jnp.sin(lam_im[:, None] * dt * k)
    ab_re, ab_im = pw_re[:, 1], pw_im[:, 1]
    den = lam_re * lam_re + lam_im * lam_im
    nr = ab_re - 1.0
    f_re = (nr * lam_re + ab_im * lam_im) / den
    f_im = (ab_im * lam_re - nr * lam_im) / den
    rows = lambda b, c, f: jnp.concatenate(
        [jnp.swapaxes(b, -1, -2), c, jnp.broadcast_to(f[:, :, None, :], (2, G_C, 8, P_C))], axis=2)
    prm = jnp.stack([rows(b_re, c_re, f_re), rows(b_im, c_im, f_im)], axis=1)
    tab = jnp.stack([pw_re, pw_im], axis=1)
    return jnp.transpose(prm, (2, 0, 1, 3, 4)), jnp.transpose(tab, (3, 0, 1, 2, 4))


def _s5_mixer(p2, tables, d_skip, w_glu_bf16, state_re, state_im):
    prm, tab = tables
    r, st = _s5_params(prm, tab)
    h0 = jnp.concatenate([state_re, state_im], axis=-1)
    y_tok, hfin = _s5_core(p2, r, st, tab, jnp.transpose(h0, (2, 1, 0, 3)))
    oc = _s5_out(y_tok, p2, d_skip, w_glu_bf16)
    hfin = jnp.transpose(hfin, (2, 1, 0, 3))
    return oc, hfin[..., :P_C], hfin[..., P_C:]


def _ret_kernel(*refs, t, latent):
    if latent:
        (q_ref, k_ref, v_ref, g_ref, dl_ref, lnw_ref, lnb_ref, r0_ref, cos_ref, sin_ref, _,
         o_ref, ob_ref) = refs
    else:
        q_ref, k_ref, v_ref, g_ref, dl_ref, lnw_ref, lnb_ref, o_ref, rfin_ref, ob_ref = refs
    n = t // CHUNK
    w = HD_D
    jf = lax.broadcasted_iota(jnp.int32, (CHUNK, CHUNK), 0).astype(F32)
    kf = lax.broadcasted_iota(jnp.int32, (CHUNK, CHUNK), 1).astype(F32)
    diff = jf - kf
    tabs = []
    for dr in range(2):
        for h in range(H_D):
            l = -_softplus(-dl_ref[h, dr:dr + 1, :])
            if dr == 0:
                dmat = jnp.where(diff >= 0, jnp.exp(l * jnp.maximum(diff, 0.0)), 0.0)
                xi = jnp.exp(l * (jf + 1.0))
                zeta = jnp.exp(l * (CHUNK - 1.0 - jf))
            else:
                dmat = jnp.where(diff < 0, jnp.exp(l * jnp.maximum(-diff, 0.0)), 0.0)
                xi = jnp.exp(l * (CHUNK - jf))
                zeta = jnp.exp(l * jf)
            tabs.append((dmat, xi, zeta, jnp.exp(l * CHUNK)))
    chains = [(dr, h) for dr in range(2) for h in range(H_D)]

    def body(ci, states):
        q, k, v, where = [], [], [], []
        for dr, h in chains:
            cj = ci if dr == 0 else n - 1 - ci
            rows = pl.ds(pl.multiple_of(cj * CHUNK, CHUNK), CHUNK)
            cols = slice(h * w, (h + 1) * w)
            qi = q_ref[rows, cols]
            ki = k_ref[rows, cols] * (HD_D ** -0.5)
            if latent:
                qi = _rope_pairs(qi, cos_ref[rows, :], sin_ref[rows, :], HD_D // 2)
                ki = _rope_pairs(ki, cos_ref[rows, :], sin_ref[rows, :], HD_D // 2)
            q.append(qi.astype(BF16))
            k.append(ki)
            v.append(v_ref[rows, cols].astype(BF16))
            where.append((o_ref if dr == 0 else ob_ref, rows, cols))
        nch = range(len(chains))
        inner = [(_bdot(q[i], k[i], NT) * tabs[i][0]).astype(BF16) for i in nch]
        cross = [_bdot(q[i], states[i]) * tabs[i][1] for i in nch]
        kz = [(k[i] * tabs[i][2]).astype(BF16) for i in nch]
        out = [jnp.dot(inner[i], v[i], preferred_element_type=F32) + cross[i] for i in nch]
        new_states = [states[i] * tabs[i][3] + lax.dot_general(kz[i], v[i], TN, preferred_element_type=F32)
                      for i in nch]
        for (dst, rows, cols), y in zip(where, out):
            dst[rows, cols] = y
        return tuple(new_states)

    if latent:
        lax.fori_loop(0, n, body, tuple(r0_ref[0, dr, h] for dr, h in chains))
    else:
        fin = lax.fori_loop(0, n, body, tuple(jnp.zeros((w, w), F32) for _ in chains))
        for i, (dr, h) in enumerate(chains):
            rfin_ref[0, dr, h] = fin[i]
    for h in range(H_D):
        cols = slice(h * w, (h + 1) * w)
        y = o_ref[:, cols] + ob_ref[:, cols]
        yc = y - jnp.mean(y, axis=-1, keepdims=True)
        yn = yc * lax.rsqrt(jnp.mean(yc * yc, axis=-1, keepdims=True) + RET_GN_EPS)
        o_ref[:, cols] = (yn * lnw_ref[:, cols] + lnb_ref[:, cols]) * _silu(g_ref[:, cols])


def _retention(p, decay_logit, ln_w, ln_b, latent, r0=None, rope=None, dst=None):
    if latent:
        nseq, t, rb0 = DEC_BATCH, DEC_SEQ, M_CTX // DEC_SEQ
    else:
        nseq, t, rb0 = BATCH, SEQ, 0
    w = HD_D
    dl = jnp.broadcast_to(decay_logit.T[:, :, None], (H_D, 2, w))
    full = lambda shape: pl.BlockSpec(shape, lambda b: (0,) * len(shape))
    st_spec = pl.BlockSpec((1, 2, H_D, w, w), lambda b: (b, 0, 0, 0, 0))
    in_specs = [pl.BlockSpec((t, MIX), lambda b, j=j: (rb0 + b, j)) for j in (1, 2, 3, 4)]
    in_specs += [full((H_D, 2, w)), full((1, MIX)), full((1, MIX))]
    args = [p, p, p, p, dl, ln_w.reshape(1, MIX), ln_b.reshape(1, MIX)]
    o_spec = pl.BlockSpec((t, MIX), lambda b: (rb0 + b, 0))
    o_shape = jax.ShapeDtypeStruct((M_ALL, MIX), F32)
    if latent:
        in_specs += [st_spec, full((t, w)), full((t, w)), pl.BlockSpec(memory_space=pl.ANY)]
        args += [r0, *rope, dst]
        out_specs, out_shape, aliases = o_spec, o_shape, {len(args) - 1: 0}
    else:
        out_specs = [o_spec, st_spec]
        out_shape = [o_shape, jax.ShapeDtypeStruct((nseq, 2, H_D, w, w), F32)]
        aliases = {}
    return pl.pallas_call(
        functools.partial(_ret_kernel, t=t, latent=latent),
        grid=(nseq,),
        in_specs=in_specs, out_specs=out_specs, out_shape=out_shape,
        scratch_shapes=[pltpu.VMEM((t, MIX), F32)],
        input_output_aliases=aliases,
        compiler_params=_params("parallel"),
        name="retention_lat" if latent else "retention_ctx",
    )(*args)


def _rope_tables(n_tok, dim, reps):
    rows = n_tok // GRID_W
    n_freq = dim // 4
    inv = 1.0 / (ROPE_THETA ** (jnp.arange(n_freq, dtype=F32) / n_freq))
    row = jnp.repeat(jnp.arange(rows, dtype=F32), GRID_W)
    col = jnp.tile(jnp.arange(GRID_W, dtype=F32), rows)
    ang = jnp.concatenate([row[:, None] * inv, col[:, None] * inv], axis=-1)
    cos, sin = jnp.cos(ang), jnp.sin(ang)
    return jnp.tile(jnp.concatenate([cos, cos], axis=1), (1, reps)), \
        jnp.tile(jnp.concatenate([-sin, sin], axis=1), (1, reps))


def kernel(x_prompt, x_sample, cache_k_ab, cache_v_ab, state_rwkv, state_s5_re, state_s5_im, state_ret, c, c_ctx, norm1_g, norm2_g, w_mod, b_mod, w_ff_gate, w_ff_up, w_ff_down, w_in_ab, w_out_ab, qk_gain_a, lambda_qk, subln_g, rwkv_mu, rwkv_k_k, rwkv_k_a, rwkv_r_k, rwkv_w0, rwkv_w_up, rwkv_a0, rwkv_a_up, rwkv_g_up, rwkv_ln_w, rwkv_ln_b, w_in_cd, w_out_cd, s5_lam_re, s5_lam_im, s5_log_dt, s5_b_re, s5_b_im, s5_c_re, s5_c_im, s5_d, s5_w_glu, ret_decay_logit, ret_ln_w, ret_ln_b):
    d = D_MODEL
    xs = [x_prompt.reshape(M_CTX, d), x_sample.reshape(M_LAT, d)]
    cvec = jnp.zeros((MOD_ROWS, d), F32).at[0].set(c_ctx).at[1:1 + DEC_BATCH].set(c)
    mods = _modulation(cvec, w_mod, b_mod)

    lam_init = 0.8 - 0.6 * math.exp(-0.3 * 0)
    p = _norm_linear(xs, norm1_g[0], mods[0], w_in_ab[0].astype(BF16))
    gain2 = jnp.tile(qk_gain_a[0], (1, 2))
    rope_a = _rope_tables(DEC_SEQ, HD_A, 2)
    ck = cache_k_ab[:, 0].reshape(DEC_BATCH * PAST_LEN, MIX)
    cv = cache_v_ab[:, 0].reshape(DEC_BATCH * PAST_LEN, MIX)
    oa, k_ctx = _attention(p, gain2, lambda_qk[0], subln_g[0], lam_init, latent=False)
    oa = _attention(p, gain2, lambda_qk[0], subln_g[0], lam_init, latent=True,
                    cache_k=ck, cache_v=cv, rope=rope_a, dst=oa)
    rw_prm = (rwkv_mu[0], rwkv_k_k[0], rwkv_k_a[0], rwkv_r_k[0], rwkv_w0[0], rwkv_w_up[0],
              rwkv_a0[0], rwkv_a_up[0], rwkv_g_up[0])
    ob, sfin_ctx = _rwkv(p, rw_prm, rwkv_ln_w[0], rwkv_ln_b[0], latent=False)
    ob = _rwkv(p, rw_prm, rwkv_ln_w[0], rwkv_ln_b[0], latent=True, s0=state_rwkv[:, 0], dst=ob)
    x = _mix_ffn(xs, oa, ob, w_out_ab[0].astype(BF16), norm2_g[0], mods[0], w_ff_gate[0].astype(BF16),
                 w_ff_up[0].astype(BF16), w_ff_down[0].astype(BF16), split_out=False)

    p2 = _norm_linear([x], norm1_g[1], mods[1], w_in_cd[0].astype(BF16))
    tables = _s5_tables(s5_lam_re[0], s5_lam_im[0], s5_log_dt[0], s5_b_re[0], s5_b_im[0],
                        s5_c_re[0], s5_c_im[0])
    oc, s5_fin_re, s5_fin_im = _s5_mixer(p2, tables, s5_d[0], s5_w_glu[0].astype(BF16),
                                         state_s5_re[:, 0], state_s5_im[:, 0])
    rope_d = _rope_tables(DEC_SEQ, HD_D, 1)
    od, rfin = _retention(p2, ret_decay_logit[0], ret_ln_w[0], ret_ln_b[0], latent=False)
    od = _retention(p2, ret_decay_logit[0], ret_ln_w[0], ret_ln_b[0], latent=True, r0=state_ret[:, 0],
                    rope=rope_d, dst=od)
    y_ctx, y_lat = _mix_ffn([x], oc, od, w_out_cd[0].astype(BF16), norm2_g[1], mods[1],
                            w_ff_gate[1].astype(BF16), w_ff_up[1].astype(BF16), w_ff_down[1].astype(BF16),
                            split_out=True)
    y_prompt = y_ctx.reshape(BATCH, SEQ, d)
    y_sample = y_lat.reshape(DEC_BATCH, DEC_SEQ, d)
    new_k = k_ctx.reshape(BATCH, 1, SEQ, H_A, 2, HD_A)
    new_v = p[:M_CTX, 2 * MIX:3 * MIX].reshape(BATCH, 1, SEQ, H_A, VD_A)
    new_rwkv = sfin_ctx[:, None]
    new_s5_re = s5_fin_re[:, None]
    new_s5_im = s5_fin_im[:, None]
    new_ret = rfin[:, None]
    return (y_prompt, y_sample, new_k, new_v, new_rwkv, new_s5_re, new_s5_im, new_ret)
```

```python
import functools
import math

import numpy as np
import jax
import jax.numpy as jnp
from jax import lax
from jax.experimental import pallas as pl
from jax.experimental.pallas import tpu as pltpu

F32 = jnp.float32
BF16 = jnp.bfloat16
HIGHEST = lax.Precision.HIGHEST

D_MODEL = 1024
BATCH = 32
SEQ = 256
DEC_BATCH = 2
DEC_SEQ = 1024
PAST_LEN = 256
GRID_W = 64
H_A = 4
HD_A = 64
VD_A = 128
H_B = 8
HD_B = 64
MIX = 512
LORA_W = 64
LORA_A = 64
LORA_G = 128
S5_GROUP = 16
G_C = 32
P_C = 64
S5_STATE = G_C * P_C
H_D = 4
HD_D = 128
CHUNK = 128
D_FF = 2816
IN_AB = 3328
IN_CD = 2560
ROPE_THETA = 10000.0
NORM_EPS = 1e-6
RWKV_GN_EPS = 64e-5
RET_GN_EPS = 1e-5

M_CTX = BATCH * SEQ
M_LAT = DEC_BATCH * DEC_SEQ
M_ALL = M_CTX + M_LAT
MOD_ROWS = 8
RWKV_CHUNK = 64
VMEM_LIMIT = 56 * 1024 * 1024

NN = (((1,), (0,)), ((), ()))
NT = (((1,), (1,)), ((), ()))
TN = (((0,), (0,)), ((), ()))


def _params(*sem):
    return pltpu.CompilerParams(dimension_semantics=sem, vmem_limit_bytes=VMEM_LIMIT)


def _bdot(a, b, dims=NN):
    return lax.dot_general(a.astype(BF16), b.astype(BF16), dims, preferred_element_type=F32)


def _hdot(a, b, dims=NN):
    return lax.dot_general(a, b, dims, precision=HIGHEST, preferred_element_type=F32)


def _split_dot(x, m):
    hi = x.astype(BF16)
    lo = (x - hi.astype(F32)).astype(BF16)
    return (jnp.dot(hi, m, preferred_element_type=F32) + jnp.dot(lo, m, preferred_element_type=F32))


def _seg_matrix(n, shift, val):
    r = lax.broadcasted_iota(jnp.int32, (n, n), 0) >> shift
    c = lax.broadcasted_iota(jnp.int32, (n, n), 1) >> shift
    return jnp.where(r == c, val, 0.0).astype(BF16)


def _sigmoid(x):
    return jax.nn.sigmoid(x)


def _silu(x):
    return x * jax.nn.sigmoid(x)


def _softplus(x):
    return jnp.maximum(x, 0.0) + jnp.log(1.0 + jnp.exp(-jnp.abs(x)))


def _mod_row(tile, tm):
    r0 = tile * tm
    return jnp.where(r0 < M_CTX, 0, 1 + (r0 - M_CTX) // DEC_SEQ)


def _norm_mod(x, g, sc_ref, sh_ref, row):
    y = x * lax.rsqrt(jnp.mean(x * x, axis=-1, keepdims=True) + NORM_EPS) * g
    return y * (1.0 + sc_ref[pl.ds(row, 1), :]) + sh_ref[pl.ds(row, 1), :]


def _mod_kernel(c_ref, w_ref, b_ref, o_ref):
    o_ref[0] = _bdot(_silu(c_ref[...]), w_ref[0]) + b_ref[0]


def _modulation(cvec, w_mod, b_mod):
    depth, d, n6 = w_mod.shape
    tn = 1536
    return pl.pallas_call(
        _mod_kernel,
        grid=(depth, n6 // tn),
        in_specs=[pl.BlockSpec((MOD_ROWS, d), lambda l, j: (0, 0)),
                  pl.BlockSpec((1, d, tn), lambda l, j: (l, 0, j)),
                  pl.BlockSpec((1, 1, tn), lambda l, j: (l, 0, j))],
        out_specs=pl.BlockSpec((1, MOD_ROWS, tn), lambda l, j: (l, 0, j)),
        out_shape=jax.ShapeDtypeStruct((depth, MOD_ROWS, n6), F32),
        compiler_params=_params("parallel", "parallel"),
        name="modulation",
    )(cvec, w_mod, b_mod.reshape(depth, 1, n6))


def _tile_specs(tm, d):
    nc = M_CTX // tm
    return [pl.BlockSpec((tm, d), lambda i: (jnp.minimum(i, nc - 1), 0)),
            pl.BlockSpec((tm, d), lambda i: (jnp.maximum(i - nc, 0), 0))]


def _pick_rows(refs, tm):
    if len(refs) == 1:
        return refs[0][...]
    return jnp.where(pl.program_id(0) < M_CTX // tm, refs[0][...], refs[1][...])


def _norm_linear_kernel(*refs, tm):
    *x_refs, g_ref, sc_ref, sh_ref, w_ref, o_ref = refs
    row = _mod_row(pl.program_id(0), tm)
    x = _pick_rows(x_refs, tm)
    nsub = 2
    rows = [slice(i * tm // nsub, (i + 1) * tm // nsub) for i in range(nsub)]
    hs = [_norm_mod(x[r], g_ref[...], sc_ref, sh_ref, row).astype(BF16) for r in rows]
    for r, h in zip(rows, hs):
        o_ref[r, :] = jnp.dot(h, w_ref[...], preferred_element_type=F32)


def _norm_linear(xs, g, mods, w_bf16):
    tm = 512
    d = D_MODEL
    n = w_bf16.shape[1]
    x_specs = _tile_specs(tm, d) if len(xs) == 2 else [pl.BlockSpec((tm, d), lambda i: (i, 0))]
    return pl.pallas_call(
        functools.partial(_norm_linear_kernel, tm=tm),
        grid=(M_ALL // tm,),
        in_specs=x_specs + [pl.BlockSpec((1, d), lambda i: (0, 0)),
                            pl.BlockSpec((MOD_ROWS, d), lambda i: (0, 1)),
                            pl.BlockSpec((MOD_ROWS, d), lambda i: (0, 0)),
                            pl.BlockSpec((d, n), lambda i: (0, 0))],
        out_specs=pl.BlockSpec((tm, n), lambda i: (i, 0)),
        out_shape=jax.ShapeDtypeStruct((M_ALL, n), F32),
        compiler_params=_params("arbitrary"),
        name="norm_linear",
    )(*xs, g.reshape(1, d), mods, mods, w_bf16)


def _mix_ffn_kernel(*refs, tm, ck, n_in, n_out):
    x_refs = refs[:n_in]
    (a_ref, b_ref, wa_ref, wb_ref, g1_ref, ng_ref, sc_ref, sh_ref, g2_ref,
     wg_ref, wu_ref, wd_ref) = refs[n_in:n_in + 12]
    o_refs = refs[n_in + 12:]
    row = _mod_row(pl.program_id(0), tm)
    mix = (jnp.dot(a_ref[...].astype(BF16), wa_ref[...], preferred_element_type=F32)
           + jnp.dot(b_ref[...].astype(BF16), wb_ref[...], preferred_element_type=F32))
    x = _pick_rows(x_refs, tm) + g1_ref[pl.ds(row, 1), :] * mix
    h = _norm_mod(x, ng_ref[...], sc_ref, sh_ref, row).astype(BF16)
    acc = jnp.zeros((tm, D_MODEL), F32)
    for c in range(D_FF // ck):
        gg = jnp.dot(h, wg_ref[:, c * ck:(c + 1) * ck], preferred_element_type=F32)
        uu = jnp.dot(h, wu_ref[:, c * ck:(c + 1) * ck], preferred_element_type=F32)
        act = (_silu(gg) * uu).astype(BF16)
        acc = acc + jnp.dot(act, wd_ref[c * ck:(c + 1) * ck, :], preferred_element_type=F32)
    y = x + g2_ref[pl.ds(row, 1), :] * acc
    if n_out == 1:
        o_refs[0][...] = y
    else:
        is_ctx = pl.program_id(0) < M_CTX // tm

        @pl.when(is_ctx)
        def _():
            o_refs[0][...] = y

        @pl.when(jnp.logical_not(is_ctx))
        def _():
            o_refs[1][...] = y


def _mix_ffn(xs, oa, ob, w_out_bf16, norm_g, mods, wg, wu, wd, split_out):
    tm, ck = 512, 256
    d = D_MODEL
    row_spec = lambda n: pl.BlockSpec((tm, n), lambda i: (i, 0))
    const = lambda shape, idx: pl.BlockSpec(shape, lambda i: idx, pipeline_mode=pl.Buffered(1))
    mod = lambda j: pl.BlockSpec((MOD_ROWS, d), lambda i: (0, j))
    x_specs = _tile_specs(tm, d) if len(xs) == 2 else [row_spec(d)]
    if split_out:
        out_specs = _tile_specs(tm, d)
        out_shape = [jax.ShapeDtypeStruct((M_CTX, d), F32), jax.ShapeDtypeStruct((M_LAT, d), F32)]
    else:
        out_specs, out_shape = row_spec(d), jax.ShapeDtypeStruct((M_ALL, d), F32)
    return pl.pallas_call(
        functools.partial(_mix_ffn_kernel, tm=tm, ck=ck, n_in=len(xs), n_out=2 if split_out else 1),
        grid=(M_ALL // tm,),
        in_specs=x_specs + [row_spec(MIX), row_spec(MIX),
                            const((MIX, d), (0, 0)), const((MIX, d), (1, 0)),
                            mod(2),
                            pl.BlockSpec((1, d), lambda i: (0, 0)),
                            mod(4), mod(3), mod(5),
                            const((d, D_FF), (0, 0)), const((d, D_FF), (0, 0)), const((D_FF, d), (0, 0))],
        out_specs=out_specs, out_shape=out_shape,
        compiler_params=_params("arbitrary"),
        name="mix_ffn",
    )(*xs, oa, ob, w_out_bf16, w_out_bf16, mods, norm_g.reshape(1, d), mods, mods, mods, wg, wu, wd)


def _qk_norm(x, gain, segm):
    ms = _split_dot(x * x, segm)
    return x * lax.rsqrt(ms + NORM_EPS) * gain


def _rope_pairs(x, cosf, sinf, half):
    lane = lax.broadcasted_iota(jnp.int32, x.shape, 1)
    first = (lane & (2 * half - 1)) < half
    n = x.shape[1]
    partner = jnp.where(first, pltpu.roll(x, n - half, axis=1), pltpu.roll(x, half, axis=1))
    return x * cosf + partner * sinf


def _attn_kernel(*refs, latent, lam_init):
    if latent:
        (q_ref, k_ref, v_ref, ck_ref, cv_ref, cosq_ref, sinq_ref, cosk_ref, sink_ref,
         gain_ref, lam_ref, sub_ref, _, o_ref, kall, vall) = refs
    else:
        q_ref, k_ref, v_ref, gain_ref, lam_ref, sub_ref, o_ref, kn_ref, vo_ref, kall, vall = refs
    w = 2 * HD_A
    segm = _seg_matrix(w, 6, 1.0 / HD_A)
    gains = gain_ref[...]
    cols = [slice(h * w, (h + 1) * w) for h in range(H_A)]

    @pl.when(pl.program_id(1) == 0)
    def _():
        k = [_qk_norm(k_ref[:, c], gains[1:2], segm) for c in cols]
        if latent:
            k = [_rope_pairs(x, cosk_ref[...], sink_ref[...], HD_A // 2) for x in k]
            for c, x in zip(cols, k):
                kall[0:PAST_LEN, c] = ck_ref[:, c].astype(BF16)
                kall[PAST_LEN:, c] = x.astype(BF16)
                vall[0:PAST_LEN, c] = cv_ref[:, c].astype(BF16)
                vall[PAST_LEN:, c] = v_ref[:, c].astype(BF16)
        else:
            for c, x in zip(cols, k):
                kn_ref[:, c] = x
                vo_ref[:, c] = v_ref[:, c]
                kall[:, c] = x.astype(BF16)
                vall[:, c] = v_ref[:, c].astype(BF16)

    q = [_qk_norm(q_ref[:, c], gains[0:1], segm) for c in cols]
    if latent:
        q = [_rope_pairs(x, cosq_ref[...], sinq_ref[...], HD_A // 2) for x in q]
    lv = lam_ref[...]
    lam = (jnp.exp(jnp.sum(lv[0:1] * lv[1:2], axis=1, keepdims=True))
           - jnp.exp(jnp.sum(lv[2:3] * lv[3:4], axis=1, keepdims=True)) + lam_init)
    scale = HD_A ** -0.5
    comp0 = lax.broadcasted_iota(jnp.int32, q[0].shape, 1) < HD_A
    qc = [jnp.where(comp0, *sel).astype(BF16) for x in q for sel in ((x, 0.0), (0.0, x))]
    s = [lax.dot_general(qc[i], kall[:, cols[i // 2]], NT, preferred_element_type=F32) * scale
         for i in range(2 * H_A)]
    e = [jnp.exp(x - jnp.max(x, axis=-1, keepdims=True)) for x in s]
    p = [x / jnp.sum(x, axis=-1, keepdims=True) for x in e]
    att = [(p[2 * h] - lam * p[2 * h + 1]).astype(BF16) for h in range(H_A)]
    o = [jnp.dot(att[h], vall[:, cols[h]], preferred_element_type=F32) for h in range(H_A)]
    o = [x * lax.rsqrt(jnp.mean(x * x, axis=-1, keepdims=True) + NORM_EPS) * sub_ref[...] for x in o]
    for c, x in zip(cols, o):
        o_ref[:, c] = x * (1.0 - lam_init)


def _attention(p, gain2, lambda_qk, subln_g, lam_init, latent, cache_k=None, cache_v=None, rope=None,
               dst=None):
    w = 2 * HD_A
    if latent:
        nseq, t, tq, rb0, s_len = DEC_BATCH, DEC_SEQ, 128, M_CTX // DEC_SEQ, PAST_LEN + DEC_SEQ
    else:
        nseq, t, tq, rb0, s_len = BATCH, SEQ, SEQ, 0, SEQ
    nq = t // tq
    qoff = rb0 * nq
    full = lambda shape: pl.BlockSpec(shape, lambda b, i: (0,) * len(shape))
    in_specs = [pl.BlockSpec((tq, MIX), lambda b, i: (qoff + b * nq + i, 0)),
                pl.BlockSpec((t, MIX), lambda b, i: (rb0 + b, 1)),
                pl.BlockSpec((t, MIX), lambda b, i: (rb0 + b, 2))]
    args = [p, p, p]
    if latent:
        cosf, sinf = rope
        in_specs += [pl.BlockSpec((PAST_LEN, MIX), lambda b, i: (b, 0)),
                     pl.BlockSpec((PAST_LEN, MIX), lambda b, i: (b, 0)),
                     pl.BlockSpec((tq, w), lambda b, i: (i, 0)),
                     pl.BlockSpec((tq, w), lambda b, i: (i, 0)),
                     full((t, w)), full((t, w))]
        args += [cache_k, cache_v, cosf, sinf, cosf, sinf]
    in_specs += [full((2, w)), full((4, HD_A)), full((1, w))]
    args += [gain2, lambda_qk, subln_g.reshape(1, w)]
    o_spec = pl.BlockSpec((tq, MIX), lambda b, i: (qoff + b * nq + i, 0))
    o_shape = jax.ShapeDtypeStruct((M_ALL, MIX), F32)
    aliases = {}
    if latent:
        out_specs, out_shape = o_spec, o_shape
        in_specs.append(pl.BlockSpec(memory_space=pl.ANY))
        args.append(dst)
        aliases = {len(args) - 1: 0}
    else:
        seq_spec = pl.BlockSpec((t, MIX), lambda b, i: (b, 0))
        out_specs = [o_spec, seq_spec, seq_spec]
        out_shape = [o_shape] + [jax.ShapeDtypeStruct((nseq * t, MIX), F32)] * 2
    return pl.pallas_call(
        functools.partial(_attn_kernel, latent=latent, lam_init=lam_init),
        grid=(nseq, nq),
        in_specs=in_specs, out_specs=out_specs, out_shape=out_shape,
        scratch_shapes=[pltpu.VMEM((s_len, MIX), BF16), pltpu.VMEM((s_len, MIX), BF16)],
        input_output_aliases=aliases,
        compiler_params=_params("parallel", "arbitrary"),
        name="diff_attention_lat" if latent else "diff_attention_ctx",
    )(*args)


def _centred_shift(x, mu):
    t = x.shape[0]
    row = lax.broadcasted_iota(jnp.int32, x.shape, 0)
    prev = jnp.where(row == 0, 0.0, pltpu.roll(x, 1, axis=0))
    nxt = jnp.where(row == t - 1, 0.0, pltpu.roll(x, t - 1, axis=0))
    return x + (0.5 * (prev + nxt) - x) * mu


def _seg_sum(x, segm):
    return jnp.concatenate([_split_dot(x[:, j * 128:(j + 1) * 128], segm) for j in range(x.shape[1] // 128)],
                           axis=1)


def _rwkv_prep_kernel(r_ref, k_ref, v_ref, l_ref, mur_ref, muk_ref, muv_ref, mul_ref,
                      kk_ref, ka_ref, rk_ref, w0_ref, wup_ref, a0_ref, aup_ref, gup_ref,
                      ro_ref, ldf_ref, ldb_ref, kbo_ref, vbo_ref, kko_ref, ao_ref, gate_ref, bonus_ref):
    seg1 = _seg_matrix(128, 6, 1.0)
    r = _centred_shift(r_ref[...], mur_ref[...])
    kb = _centred_shift(k_ref[...], muk_ref[...])
    vb = _centred_shift(v_ref[...], muv_ref[...])
    lo = _centred_shift(l_ref[...], mul_ref[...])
    xw = lo[:, 0:LORA_W]
    xa = lo[:, LORA_W:LORA_W + LORA_A]
    xg = lo[:, LORA_W + LORA_A:]
    kk = kb * kk_ref[...]
    kk = kk * lax.rsqrt(_seg_sum(kk * kk, seg1) + 1e-12)
    a = _sigmoid(a0_ref[...] + _bdot(xa, aup_ref[...]))
    kb2 = kb * (1.0 + (a - 1.0) * ka_ref[...])
    lw = jnp.tanh(xw)
    for dr, ld_ref in enumerate((ldf_ref, ldb_ref)):
        z = w0_ref[dr:dr + 1, :] + _bdot(lw, wup_ref[dr])
        logw = -_softplus(-z) - 0.5
        ld_ref[...] = -jnp.exp(logw)
    gate_ref[...] = _bdot(_sigmoid(xg), gup_ref[...])
    bonus_ref[...] = _seg_sum(r * kb2 * rk_ref[...], seg1) * vb
    ro_ref[...] = r
    kbo_ref[...] = kb2
    vbo_ref[...] = vb
    kko_ref[...] = kk
    ao_ref[...] = a


N_PREP_IN = 16
N_PREP_OUT = 9


def _split2(x):
    hi = x.astype(BF16)
    return hi, (x - hi.astype(F32)).astype(BF16)


RWKV_INV_BLOCK = 16


def _rwkv_masks(c, rev):
    ti = lax.broadcasted_iota(jnp.int32, (c, c), 0)
    si = lax.broadcasted_iota(jnp.int32, (c, c), 1)
    tri = jnp.where((si >= ti) if rev else (si <= ti), 1.0, 0.0).astype(BF16)
    t4 = lax.broadcasted_iota(jnp.int32, (4 * c, 4 * c), 0)
    s4 = lax.broadcasted_iota(jnp.int32, (4 * c, 4 * c), 1)
    tm, sm = t4 & (c - 1), s4 & (c - 1)
    strict = (sm > tm) if rev else (sm < tm)
    incl = (sm >= tm) if rev else (sm <= tm)
    same_head = ((t4 // c) & 1) == ((s4 // c) & 1)
    top = t4 < 2 * c
    gmask = same_head & ((top & strict) | (~top & incl))
    t2 = lax.broadcasted_iota(jnp.int32, (2 * c, 2 * c), 0)
    s2 = lax.broadcasted_iota(jnp.int32, (2 * c, 2 * c), 1)
    same = lambda n: (t2 // n) == (s2 // n)
    levels = []
    n = RWKV_INV_BLOCK
    while n < c:
        levels.append(same(2 * n) & ~same(n))
        n *= 2
    f = lambda m: jnp.where(m, 1.0, 0.0)
    return tri, f(gmask), f(same(RWKV_INV_BLOCK)), tuple(f(m) for m in levels), f(same(c)), f(t2 == s2)


def _keep(mask01, x):
    return jnp.where(mask01 > 0.5, x, 0.0)


def _tri_inverse(a, diag_blk, levels, eye):
    n = a[0].shape[0]
    d = [_keep(diag_blk, x) for x in a]
    t = [eye + x for x in d]
    p = [_bdot(x, x) for x in d]
    for _ in range(int(math.log2(RWKV_INV_BLOCK)) - 2):
        res = [_bdot(jnp.concatenate([pi, ti], axis=0), pi) for pi, ti in zip(p, t)]
        p = [x[:n] for x in res]
        t = [ti + x[n:] for ti, x in zip(t, res)]
    t = [ti + _bdot(ti, pi) for ti, pi in zip(t, p)]
    for off in levels:
        half = [_bdot(ti, _keep(off, x)) for ti, x in zip(t, a)]
        t = [ti + _bdot(x, ti) for ti, x in zip(t, half)]
    return t


def _rwkv_pair_chunks(ins, sts, tris, gmasks, diag_blk, levels, same_head, eye):
    c, w = ins[0][0].shape
    hd = w // 2
    nch = range(len(ins))
    r, ld, kb, vb, kk, a = (list(z) for z in zip(*ins))
    split = [_split2(x) for x in ld]
    lcum = [jnp.dot(tris[i], jnp.concatenate(split[i], axis=1), preferred_element_type=F32) for i in nch]
    lcum = [x[:, :w] + x[:, w:] for x in lcum]
    ltot = [jnp.sum(x, axis=0, keepdims=True) for x in ld]
    beta = [kk[i] * a[i] for i in nch]
    eneg = [jnp.exp(-x) for x in lcum]
    abar = [-kk[i] * jnp.exp(lcum[i] - ld[i]) for i in nch]
    rbar = [r[i] * jnp.exp(lcum[i]) for i in nch]
    bt = [(beta[i] * eneg[i]).astype(BF16) for i in nch]
    kt = [(kb[i] * eneg[i]).astype(BF16) for i in nch]
    vbb = [x.astype(BF16) for x in vb]
    head0 = lax.broadcasted_iota(jnp.int32, (c, w), 1) < hd
    pick = lambda res: jnp.where(head0, res[:c], res[c:])
    arst = [_bdot(jnp.concatenate([abar[i], rbar[i]], axis=0), sts[i], NT) for i in nch]
    lhs = [jnp.concatenate([jnp.where(head0, abar[i], 0.0), jnp.where(head0, 0.0, abar[i]),
                            jnp.where(head0, rbar[i], 0.0), jnp.where(head0, 0.0, rbar[i])], axis=0) for i in nch]
    g = [_keep(gmasks[i], _bdot(lhs[i], jnp.concatenate([bt[i], bt[i], kt[i], kt[i]], axis=0), NT))
         for i in nch]
    x = [arst[i][:c] + pick(_bdot(g[i][:2 * c, 2 * c:], jnp.concatenate([vbb[i], vbb[i]], axis=0))) for i in nch]
    tinv = _tri_inverse([gi[:2 * c, :2 * c] for gi in g], diag_blk, levels, eye)
    u = [pick(_bdot(tinv[i], jnp.concatenate([x[i], x[i]], axis=0))) for i in nch]
    ub = [z.astype(BF16) for z in u]
    y = [arst[i][c:] + pick(_bdot(g[i][2 * c:], jnp.concatenate([ub[i], ub[i], vbb[i], vbb[i]], axis=0)))
         for i in nch]
    erem = [jnp.exp(ltot[i] - lcum[i]) for i in nch]
    bkh = [jnp.concatenate([beta[i] * erem[i], kb[i] * erem[i]], axis=0) for i in nch]
    st_new = [jnp.exp(ltot[i]) * sts[i]
              + _keep(same_head, _bdot(jnp.concatenate([ub[i], vbb[i]], axis=0), bkh[i], TN)) for i in nch]
    return y, st_new


def _rwkv_scan_kernel(*refs, t, c, npair, latent):
    prep_in, (lnw_ref, lnb_ref), rest = refs[:N_PREP_IN], refs[N_PREP_IN:N_PREP_IN + 2], refs[N_PREP_IN + 2:]
    if latent:
        s0_ref, _, o_ref, *scratch = rest
    else:
        o_ref, sfin_ref, *scratch = rest
    *prep_out, yb_ref = scratch
    _rwkv_prep_kernel(*prep_in, *prep_out)
    r_ref, ldf_ref, ldb_ref, kb_ref, vb_ref, kk_ref, a_ref, gate_ref, bonus_ref = prep_out
    n = t // c
    w = 2 * HD_B
    masks = [_rwkv_masks(c, rev) for rev in (False, True)]

    def body(ci, states):
        ins, tris, gmasks, dsts = [], [], [], []
        for dr, (ld_ref, dst) in enumerate(((ldf_ref, o_ref), (ldb_ref, yb_ref))):
            cj = ci if dr == 0 else n - 1 - ci
            rows = pl.ds(pl.multiple_of(cj * c, c), c)
            for p in range(npair):
                cols = slice(p * w, (p + 1) * w)
                ins.append(tuple(ref[rows, cols] for ref in (r_ref, ld_ref, kb_ref, vb_ref, kk_ref, a_ref)))
                tris.append(masks[dr][0])
                gmasks.append(masks[dr][1])
                dsts.append((dst, rows, cols))
        ys, new_states = _rwkv_pair_chunks(ins, list(states), tris, gmasks, *masks[0][2:])
        for (dst, rows, cols), y in zip(dsts, ys):
            dst[rows, cols] = y
        return tuple(new_states)

    if latent:
        zero = jnp.zeros((HD_B, HD_B), F32)
        init = tuple(jnp.concatenate([jnp.concatenate([s0_ref[0, dr, 2 * p], zero], axis=1),
                                      jnp.concatenate([zero, s0_ref[0, dr, 2 * p + 1]], axis=1)], axis=0)
                     for dr in range(2) for p in range(npair))
    else:
        init = tuple(jnp.zeros((w, w), F32) for _ in range(2 * npair))
    fin = lax.fori_loop(0, n, body, init)
    segm = _seg_matrix(w, 6, 1.0 / HD_B)
    for p in range(npair):
        if not latent:
            for dr in range(2):
                st = fin[dr * npair + p]
                sfin_ref[0, dr, 2 * p] = st[:HD_B, :HD_B]
                sfin_ref[0, dr, 2 * p + 1] = st[HD_B:, HD_B:]
        cols = slice(p * w, (p + 1) * w)
        y = o_ref[:, cols] + yb_ref[:, cols]
        yc = y - _split_dot(y, segm)
        yn = yc * lax.rsqrt(_split_dot(yc * yc, segm) + RWKV_GN_EPS)
        o_ref[:, cols] = (yn * lnw_ref[:, cols] + lnb_ref[:, cols] + bonus_ref[:, cols]) * gate_ref[:, cols]


def _rwkv(p, prm, ln_w, ln_b, latent, s0=None, dst=None):
    mu, k_k, k_a, r_k, w0, w_up, a0, a_up, g_up = prm
    nseq, t, rb0 = (DEC_BATCH, DEC_SEQ, M_CTX // DEC_SEQ) if latent else (BATCH, SEQ, 0)
    npair = H_B // 2
    lw = LORA_W + LORA_A + LORA_G
    c0 = (IN_AB - 3 * MIX - lw) // MIX
    cl = (IN_AB - lw) // lw
    full = lambda *shape: pl.BlockSpec(shape, lambda b: (0,) * len(shape))
    st_spec = pl.BlockSpec((1, 2, H_B, HD_B, HD_B), lambda b: (b, 0, 0, 0, 0))
    o_spec = pl.BlockSpec((t, MIX), lambda b: (rb0 + b, 0))
    o_shape = jax.ShapeDtypeStruct((M_ALL, MIX), F32)
    in_specs = [pl.BlockSpec((t, MIX), lambda b: (rb0 + b, c0)),
                pl.BlockSpec((t, MIX), lambda b: (rb0 + b, c0 + 1)),
                pl.BlockSpec((t, MIX), lambda b: (rb0 + b, c0 + 2)),
                pl.BlockSpec((t, lw), lambda b: (rb0 + b, cl)),
                full(1, MIX), full(1, MIX), full(1, MIX), full(1, lw),
                full(1, MIX), full(1, MIX), full(1, MIX),
                full(2, MIX), full(2, LORA_W, MIX), full(1, MIX), full(LORA_A, MIX), full(LORA_G, MIX),
                full(1, MIX), full(1, MIX)]
    args = [p, p, p, p,
            mu[None, 0:MIX], mu[None, MIX:2 * MIX], mu[None, 2 * MIX:3 * MIX], mu[None, 3 * MIX:],
            k_k.reshape(1, MIX), k_a.reshape(1, MIX), r_k.reshape(1, MIX), w0, w_up,
            a0.reshape(1, MIX), a_up, g_up, ln_w.reshape(1, MIX), ln_b.reshape(1, MIX)]
    if latent:
        args += [s0, dst]
        in_specs += [st_spec, pl.BlockSpec(memory_space=pl.ANY)]
        out_specs, out_shape, aliases = o_spec, o_shape, {len(args) - 1: 0}
    else:
        out_specs = [o_spec, st_spec]
        out_shape = [o_shape, jax.ShapeDtypeStruct((nseq, 2, H_B, HD_B, HD_B), F32)]
        aliases = {}
    return pl.pallas_call(
        functools.partial(_rwkv_scan_kernel, t=t, c=RWKV_CHUNK, npair=npair, latent=latent),
        grid=(nseq,),
        in_specs=in_specs, out_specs=out_specs, out_shape=out_shape,
        scratch_shapes=[pltpu.VMEM((t, MIX), F32)] * (N_PREP_OUT + 1),
        input_output_aliases=aliases,
        compiler_params=_params("parallel"),
        name="rwkv_lat" if latent else "rwkv_ctx",
    )(*args)


S5_L = 16
S5_ROWS = M_ALL // S5_L
S5_CW = S5_L * S5_GROUP
S5_GS = 8
S5_PW = 2 * P_C
S5_TAB = 24
S5_PRM = 2 * S5_GROUP + 8
S5_NC_CTX = SEQ // S5_L
S5_NC_LAT = DEC_SEQ // S5_L
S5_CTX_ROWS = BATCH * S5_NC_CTX


def _gelu_tanh(x):
    return 0.5 * x * (1.0 + jnp.tanh(math.sqrt(2.0 / math.pi) * (x + 0.044715 * (x * x * x))))


def _cmul(ar, ai, br, bi):
    return ar * br - ai * bi, ar * bi + ai * br


def _dot3(a, b, dims):
    ah, al = _split2(a)
    bh, bl = _split2(b)
    d = lambda x, y: lax.dot_general(x, y, dims, preferred_element_type=F32)
    return d(ah, bh) + d(ah, bl) + d(al, bh)


def _s5_param_kernel(prm_ref, tab_ref, r_ref, st_ref):
    nl, ns = S5_L, S5_GROUP
    lane_blk = lax.broadcasted_iota(jnp.int32, (ns, S5_CW), 1) // ns
    t_rows = [jnp.zeros((ns, S5_CW), F32) for _ in range(nl)]
    e_parts = []
    for d in range(2):
        part = lambda lo, n: (prm_ref[0, d, 0, lo:lo + n, :], prm_ref[0, d, 1, lo:lo + n, :])
        bb = _cmul(*part(2 * ns, 1), *part(0, ns))
        cc = part(ns, ns)
        pw = lambda k: (tab_ref[0, d, 0, k:k + 1, :], tab_ref[0, d, 1, k:k + 1, :])
        steps = range(nl)
        if d == 0:
            ke, ks, kk = [nl - 1 - j for j in steps], [j + 1 for j in steps], list(steps)
        else:
            ke, ks, kk = list(steps), [nl - j for j in steps], [nl - 1 - j for j in steps]
        stack = lambda xs: (jnp.concatenate([x[0] for x in xs], axis=0), jnp.concatenate([x[1] for x in xs], axis=0))
        e_re, e_im = stack([_cmul(*bb, *pw(k)) for k in ke])
        s_re, s_im = stack([_cmul(*cc, *pw(k)) for k in ks])
        k_re, k_im = stack([_cmul(*cc, *pw(k)) for k in kk])
        st_ref[0, d] = jnp.concatenate([s_re, -s_im], axis=1).astype(BF16)
        krow = _dot3(bb[0], k_re, NT) - _dot3(bb[1], k_im, NT)
        for j in steps:
            if d == 0:
                shifted, keep = pltpu.roll(krow, ns * j, axis=1), lane_blk >= j
            else:
                shifted, keep = pltpu.roll(krow, (S5_CW - ns * (nl - 1 - j)) % S5_CW, axis=1), lane_blk <= j
            t_rows[j] = t_rows[j] + jnp.where(keep, shifted, 0.0)
        e_parts += [e_re, e_im]
    r_ref[0] = jnp.concatenate([jnp.concatenate(t_rows, axis=0)] + e_parts, axis=1).astype(BF16)


def _s5_params(prm, tab):
    return pl.pallas_call(
        _s5_param_kernel,
        grid=(G_C,),
        in_specs=[pl.BlockSpec((1, 2, 2, S5_PRM, P_C), lambda g: (g, 0, 0, 0, 0)),
                  pl.BlockSpec((1, 2, 2, S5_TAB, P_C), lambda g: (g, 0, 0, 0, 0))],
        out_specs=[pl.BlockSpec((1, S5_CW, 2 * S5_CW), lambda g: (g, 0, 0)),
                   pl.BlockSpec((1, 2, S5_CW, S5_PW), lambda g: (g, 0, 0, 0))],
        out_shape=[jax.ShapeDtypeStruct((G_C, S5_CW, 2 * S5_CW), BF16),
                   jax.ShapeDtypeStruct((G_C, 2, S5_CW, S5_PW), BF16)],
        compiler_params=_params("parallel"),
        name="s5_params",
    )(prm, tab)


def _s5_core_kernel(u_ref, r_ref, st_ref, tab_ref, h0_ref, o_ref, hfin_ref, ug_scr, yg_scr, e_scr, hp_scr):
    ns, nl = S5_GROUP, S5_L
    per_v = 128 // ns
    blk = lax.broadcasted_iota(jnp.int32, (8, 128), 1) // ns

    def merge(select):
        acc = select(0)
        for b in range(1, per_v):
            acc = jnp.where(blk == b, select(b), acc)
        return acc

    def shuffle(srcs):
        rolled = []
        for s in range(per_v):
            m = merge(lambda b: srcs[(b + s) % per_v])
            rolled.append(m if s == 0 else pltpu.roll(m, s * ns, axis=1))
        return [merge(lambda b: rolled[(b - a) % per_v]) for a in range(per_v)]

    def tiles(regroup):
        def ctx_tile(ti, _):
            c, b0 = ti // (BATCH // 8), (ti % (BATCH // 8)) * 8
            regroup(b0 * SEQ + c * nl, SEQ, pl.multiple_of(c * BATCH + b0, 8))
            return 0

        def lat_tile(ti, _):
            b, c0 = ti // (S5_NC_LAT // 8), (ti % (S5_NC_LAT // 8)) * 8
            regroup(M_CTX + b * DEC_SEQ + c0 * nl, nl, pl.multiple_of(S5_CTX_ROWS + b * S5_NC_LAT + c0, 8))
            return 0

        lax.fori_loop(0, S5_CTX_ROWS // 8, ctx_tile, 0, unroll=4)
        lax.fori_loop(0, (S5_ROWS - S5_CTX_ROWS) // 8, lat_tile, 0, unroll=4)

    def regroup_in(tok, stride, row):
        uj = [u_ref[pl.ds(tok + j, 8, stride=stride), :] for j in range(nl)]
        for jh in range(nl // per_v):
            for g, v in enumerate(shuffle(uj[jh * per_v:(jh + 1) * per_v])):
                ug_scr[g, pl.ds(row, 8), jh * 128:(jh + 1) * 128] = v

    tiles(regroup_in)

    chains = [(g, d) for g in range(S5_GS) for d in range(2)]
    for g in range(S5_GS):
        a = jnp.dot(ug_scr[g].astype(BF16), r_ref[g], preferred_element_type=F32)
        yg_scr[g] = a[:, 0:S5_CW]
        for d in range(2):
            e_scr[g, d] = a[:, S5_CW + d * S5_PW:S5_CW + (d + 1) * S5_PW]
    lam = []
    for g, d in chains:
        lr, li = (tab_ref[g, d, ri, nl:nl + 1, :] for ri in range(2))
        lam.append((jnp.concatenate([lr, lr], axis=1), jnp.concatenate([-li, li], axis=1)))

    def advance(c, hs, latent):
        nc = S5_NC_LAT if latent else S5_NC_CTX
        out = []
        for i, (g, d) in enumerate(chains):
            cc = c if d == 0 else nc - 1 - c
            rows = (pl.ds(S5_CTX_ROWS + cc, DEC_BATCH, stride=nc) if latent
                    else pl.ds(pl.multiple_of(cc * BATCH, BATCH), BATCH))
            hp_scr[g, d, rows, :] = hs[i]
            out.append(lam[i][0] * hs[i] + lam[i][1] * pltpu.roll(hs[i], P_C, axis=1) + e_scr[g, d, rows, :])
        return out

    def both(c, carry):
        hc, hl = carry
        return tuple(advance(c, hc, False)), tuple(advance(c, hl, True))

    h_ctx = tuple(jnp.zeros((BATCH, S5_PW), F32) for _ in chains)
    h_lat = tuple(h0_ref[g, d] for g, d in chains)
    h_ctx, h_lat = lax.fori_loop(0, S5_NC_CTX, both, (h_ctx, h_lat))
    lax.fori_loop(S5_NC_CTX, S5_NC_LAT, lambda c, hl: tuple(advance(c, hl, True)), h_lat)
    for i, (g, d) in enumerate(chains):
        hfin_ref[g, d] = h_ctx[i]
    for g in range(S5_GS):
        y = yg_scr[g]
        for d in range(2):
            y = y + lax.dot_general(hp_scr[g, d].astype(BF16), st_ref[g, d], NT, preferred_element_type=F32)
        yg_scr[g] = y

    def regroup_out(tok, stride, row):
        for jh in range(nl // per_v):
            yv = [yg_scr[g, pl.ds(row, 8), jh * 128:(jh + 1) * 128] for g in range(S5_GS)]
            for jj, v in enumerate(shuffle(yv)):
                o_ref[pl.ds(tok + jh * per_v + jj, 8, stride=stride), :] = v

    tiles(regroup_out)


def _s5_core(p2, r, st, tab, h0):
    blk = lambda *tail: pl.BlockSpec((S5_GS,) + tail, lambda s: (s,) + (0,) * len(tail))
    col = pl.BlockSpec((M_ALL, S5_GS * S5_GROUP), lambda s: (0, s))
    return pl.pallas_call(
        _s5_core_kernel,
        grid=(G_C // S5_GS,),
        in_specs=[col, blk(S5_CW, 2 * S5_CW), blk(2, S5_CW, S5_PW),
                  blk(2, 2, S5_TAB, P_C), blk(2, DEC_BATCH, S5_PW)],
        out_specs=[col, blk(2, BATCH, S5_PW)],
        out_shape=[jax.ShapeDtypeStruct((M_ALL, MIX), F32),
                   jax.ShapeDtypeStruct((G_C, 2, BATCH, S5_PW), F32)],
        scratch_shapes=[pltpu.VMEM((S5_GS, S5_ROWS, S5_CW), F32),
                        pltpu.VMEM((S5_GS, S5_ROWS, S5_CW), F32),
                        pltpu.VMEM((S5_GS, 2, S5_ROWS, S5_PW), F32),
                        pltpu.VMEM((S5_GS, 2, S5_ROWS, S5_PW), F32)],
        compiler_params=_params("parallel"),
        name="s5_core",
    )(p2, r, st, tab, h0)


def _s5_out_kernel(y_ref, u_ref, d_ref, w_ref, o_ref):
    z = _gelu_tanh(y_ref[...] + d_ref[...] * u_ref[...])
    o_ref[...] = z * _sigmoid(jnp.dot(z.astype(BF16), w_ref[...], preferred_element_type=F32))


def _s5_out(y_tok, p2, d_skip, w_glu_bf16):
    tm = 512
    return pl.pallas_call(
        _s5_out_kernel,
        grid=(M_ALL // tm,),
        in_specs=[pl.BlockSpec((tm, MIX), lambda i: (i, 0)),
                  pl.BlockSpec((tm, MIX), lambda i: (i, 0)),
                  pl.BlockSpec((1, MIX), lambda i: (0, 0)),
                  pl.BlockSpec((MIX, MIX), lambda i: (0, 0))],
        out_specs=pl.BlockSpec((tm, MIX), lambda i: (i, 0)),
        out_shape=jax.ShapeDtypeStruct((M_ALL, MIX), F32),
        compiler_params=_params("parallel"),
        name="s5_out",
    )(y_tok, p2, d_skip.reshape(1, MIX), w_glu_bf16)


def _s5_tables(lam_re, lam_im, log_dt, b_re, b_im, c_re, c_im):
    dt = jnp.exp(log_dt)[:, None, :, None]
    k = jnp.arange(S5_TAB, dtype=F32)[None, :, None, None]
    mag = jnp.exp(lam_re[:, None] * dt * k)
    pw_re = mag * jnp.cos(lam_im[:, None] * dt * k)
    pw_im = mag * jnp.sin(lam_im[:, None] * dt * k)
    ab_re, ab_im = pw_re[:, 1], pw_im[:, 1]
    den = lam_re * lam_re + lam_im * lam_im
    nr = ab_re - 1.0
    f_re = (nr * lam_re + ab_im * lam_im) / den
    f_im = (ab_im * lam_re - nr * lam_im) / den
    rows = lambda b, c, f: jnp.concatenate(
        [jnp.swapaxes(b, -1, -2), c, jnp.broadcast_to(f[:, :, None, :], (2, G_C, 8, P_C))], axis=2)
    prm = jnp.stack([rows(b_re, c_re, f_re), rows(b_im, c_im, f_im)], axis=1)
    tab = jnp.stack([pw_re, pw_im], axis=1)
    return jnp.transpose(prm, (2, 0, 1, 3, 4)), jnp.transpose(tab, (3, 0, 1, 2, 4))


def _s5_mixer(p2, tables, d_skip, w_glu_bf16, state_re, state_im):
    prm, tab = tables
    r, st = _s5_params(prm, tab)
    h0 = jnp.concatenate([state_re, state_im], axis=-1)
    y_tok, hfin = _s5_core(p2, r, st, tab, jnp.transpose(h0, (2, 1, 0, 3)))
    oc = _s5_out(y_tok, p2, d_skip, w_glu_bf16)
    hfin = jnp.transpose(hfin, (2, 1, 0, 3))
    return oc, hfin[..., :P_C], hfin[..., P_C:]


def _ret_kernel(*refs, t, latent):
    if latent:
        (q_ref, k_ref, v_ref, g_ref, dl_ref, lnw_ref, lnb_ref, r0_ref, cos_ref, sin_ref, _,
         o_ref, ob_ref) = refs
    else:
        q_ref, k_ref, v_ref, g_ref, dl_ref, lnw_ref, lnb_ref, o_ref, rfin_ref, ob_ref = refs
    n = t // CHUNK
    w = HD_D
    jf = lax.broadcasted_iota(jnp.int32, (CHUNK, CHUNK), 0).astype(F32)
    kf = lax.broadcasted_iota(jnp.int32, (CHUNK, CHUNK), 1).astype(F32)
    diff = jf - kf
    tabs = []
    for dr in range(2):
        for h in range(H_D):
            l = -_softplus(-dl_ref[h, dr:dr + 1, :])
            if dr == 0:
                dmat = jnp.where(diff >= 0, jnp.exp(l * jnp.maximum(diff, 0.0)), 0.0)
                xi = jnp.exp(l * (jf + 1.0))
                zeta = jnp.exp(l * (CHUNK - 1.0 - jf))
            else:
                dmat = jnp.where(diff < 0, jnp.exp(l * jnp.maximum(-diff, 0.0)), 0.0)
                xi = jnp.exp(l * (CHUNK - jf))
                zeta = jnp.exp(l * jf)
            tabs.append((dmat, xi, zeta, jnp.exp(l * CHUNK)))
    chains = [(dr, h) for dr in range(2) for h in range(H_D)]

    def body(ci, states):
        q, k, v, where = [], [], [], []
        for dr, h in chains:
            cj = ci if dr == 0 else n - 1 - ci
            rows = pl.ds(pl.multiple_of(cj * CHUNK, CHUNK), CHUNK)
            cols = slice(h * w, (h + 1) * w)
            qi = q_ref[rows, cols]
            ki = k_ref[rows, cols] * (HD_D ** -0.5)
            if latent:
                qi = _rope_pairs(qi, cos_ref[rows, :], sin_ref[rows, :], HD_D // 2)
                ki = _rope_pairs(ki, cos_ref[rows, :], sin_ref[rows, :], HD_D // 2)
            q.append(qi.astype(BF16))
            k.append(ki)
            v.append(v_ref[rows, cols].astype(BF16))
            where.append((o_ref if dr == 0 else ob_ref, rows, cols))
        nch = range(len(chains))
        inner = [(_bdot(q[i], k[i], NT) * tabs[i][0]).astype(BF16) for i in nch]
        cross = [_bdot(q[i], states[i]) * tabs[i][1] for i in nch]
        kz = [(k[i] * tabs[i][2]).astype(BF16) for i in nch]
        out = [jnp.dot(inner[i], v[i], preferred_element_type=F32) + cross[i] for i in nch]
        new_states = [states[i] * tabs[i][3] + lax.dot_general(kz[i], v[i], TN, preferred_element_type=F32)
                      for i in nch]
        for (dst, rows, cols), y in zip(where, out):
            dst[rows, cols] = y
        return tuple(new_states)

    if latent:
        lax.fori_loop(0, n, body, tuple(r0_ref[0, dr, h] for dr, h in chains))
    else:
        fin = lax.fori_loop(0, n, body, tuple(jnp.zeros((w, w), F32) for _ in chains))
        for i, (dr, h) in enumerate(chains):
            rfin_ref[0, dr, h] = fin[i]
    for h in range(H_D):
        cols = slice(h * w, (h + 1) * w)
        y = o_ref[:, cols] + ob_ref[:, cols]
        yc = y - jnp.mean(y, axis=-1, keepdims=True)
        yn = yc * lax.rsqrt(jnp.mean(yc * yc, axis=-1, keepdims=True) + RET_GN_EPS)
        o_ref[:, cols] = (yn * lnw_ref[:, cols] + lnb_ref[:, cols]) * _silu(g_ref[:, cols])


def _retention(p, decay_logit, ln_w, ln_b, latent, r0=None, rope=None, dst=None):
    if latent:
        nseq, t, rb0 = DEC_BATCH, DEC_SEQ, M_CTX // DEC_SEQ
    else:
        nseq, t, rb0 = BATCH, SEQ, 0
    w = HD_D
    dl = jnp.broadcast_to(decay_logit.T[:, :, None], (H_D, 2, w))
    full = lambda shape: pl.BlockSpec(shape, lambda b: (0,) * len(shape))
    st_spec = pl.BlockSpec((1, 2, H_D, w, w), lambda b: (b, 0, 0, 0, 0))
    in_specs = [pl.BlockSpec((t, MIX), lambda b, j=j: (rb0 + b, j)) for j in (1, 2, 3, 4)]
    in_specs += [full((H_D, 2, w)), full((1, MIX)), full((1, MIX))]
    args = [p, p, p, p, dl, ln_w.reshape(1, MIX), ln_b.reshape(1, MIX)]
    o_spec = pl.BlockSpec((t, MIX), lambda b: (rb0 + b, 0))
    o_shape = jax.ShapeDtypeStruct((M_ALL, MIX), F32)
    if latent:
        in_specs += [st_spec, full((t, w)), full((t, w)), pl.BlockSpec(memory_space=pl.ANY)]
        args += [r0, *rope, dst]
        out_specs, out_shape, aliases = o_spec, o_shape, {len(args) - 1: 0}
    else:
        out_specs = [o_spec, st_spec]
        out_shape = [o_shape, jax.ShapeDtypeStruct((nseq, 2, H_D, w, w), F32)]
        aliases = {}
    return pl.pallas_call(
        functools.partial(_ret_kernel, t=t, latent=latent),
        grid=(nseq,),
        in_specs=in_specs, out_specs=out_specs, out_shape=out_shape,
        scratch_shapes=[pltpu.VMEM((t, MIX), F32)],
        input_output_aliases=aliases,
        compiler_params=_params("parallel"),
        name="retention_lat" if latent else "retention_ctx",
    )(*args)


def _rope_tables(n_tok, dim, reps):
    rows = n_tok // GRID_W
    n_freq = dim // 4
    inv = 1.0 / (ROPE_THETA ** (jnp.arange(n_freq, dtype=F32) / n_freq))
    row = jnp.repeat(jnp.arange(rows, dtype=F32), GRID_W)
    col = jnp.tile(jnp.arange(GRID_W, dtype=F32), rows)
    ang = jnp.concatenate([row[:, None] * inv, col[:, None] * inv], axis=-1)
    cos, sin = jnp.cos(ang), jnp.sin(ang)
    return jnp.tile(jnp.concatenate([cos, cos], axis=1), (1, reps)), \
        jnp.tile(jnp.concatenate([-sin, sin], axis=1), (1, reps))


def kernel(x_prompt, x_sample, cache_k_ab, cache_v_ab, state_rwkv, state_s5_re, state_s5_im, state_ret, c, c_ctx, norm1_g, norm2_g, w_mod, b_mod, w_ff_gate, w_ff_up, w_ff_down, w_in_ab, w_out_ab, qk_gain_a, lambda_qk, subln_g, rwkv_mu, rwkv_k_k, rwkv_k_a, rwkv_r_k, rwkv_w0, rwkv_w_up, rwkv_a0, rwkv_a_up, rwkv_g_up, rwkv_ln_w, rwkv_ln_b, w_in_cd, w_out_cd, s5_lam_re, s5_lam_im, s5_log_dt, s5_b_re, s5_b_im, s5_c_re, s5_c_im, s5_d, s5_w_glu, ret_decay_logit, ret_ln_w, ret_ln_b):
    d = D_MODEL
    xs = [x_prompt.reshape(M_CTX, d), x_sample.reshape(M_LAT, d)]
    cvec = jnp.zeros((MOD_ROWS, d), F32).at[0].set(c_ctx).at[1:1 + DEC_BATCH].set(c)
    mods = _modulation(cvec, w_mod, b_mod)

    lam_init = 0.8 - 0.6 * math.exp(-0.3 * 0)
    p = _norm_linear(xs, norm1_g[0], mods[0], w_in_ab[0].astype(BF16))
    gain2 = jnp.tile(qk_gain_a[0], (1, 2))
    rope_a = _rope_tables(DEC_SEQ, HD_A, 2)
    ck = cache_k_ab[:, 0].reshape(DEC_BATCH * PAST_LEN, MIX)
    cv = cache_v_ab[:, 0].reshape(DEC_BATCH * PAST_LEN, MIX)
    oa, k_ctx, v_ctx = _attention(p, gain2, lambda_qk[0], subln_g[0], lam_init, latent=False)
    oa = _attention(p, gain2, lambda_qk[0], subln_g[0], lam_init, latent=True,
                    cache_k=ck, cache_v=cv, rope=rope_a, dst=oa)
    rw_prm = (rwkv_mu[0], rwkv_k_k[0], rwkv_k_a[0], rwkv_r_k[0], rwkv_w0[0], rwkv_w_up[0],
              rwkv_a0[0], rwkv_a_up[0], rwkv_g_up[0])
    ob, sfin_ctx = _rwkv(p, rw_prm, rwkv_ln_w[0], rwkv_ln_b[0], latent=False)
    ob = _rwkv(p, rw_prm, rwkv_ln_w[0], rwkv_ln_b[0], latent=True, s0=state_rwkv[:, 0], dst=ob)
    x = _mix_ffn(xs, oa, ob, w_out_ab[0].astype(BF16), norm2_g[0], mods[0], w_ff_gate[0].astype(BF16),
                 w_ff_up[0].astype(BF16), w_ff_down[0].astype(BF16), split_out=False)

    p2 = _norm_linear([x], norm1_g[1], mods[1], w_in_cd[0].astype(BF16))
    tables = _s5_tables(s5_lam_re[0], s5_lam_im[0], s5_log_dt[0], s5_b_re[0], s5_b_im[0],
                        s5_c_re[0], s5_c_im[0])
    oc, s5_fin_re, s5_fin_im = _s5_mixer(p2, tables, s5_d[0], s5_w_glu[0].astype(BF16),
                                         state_s5_re[:, 0], state_s5_im[:, 0])
    rope_d = _rope_tables(DEC_SEQ, HD_D, 1)
    od, rfin = _retention(p2, ret_decay_logit[0], ret_ln_w[0], ret_ln_b[0], latent=False)
    od = _retention(p2, ret_decay_logit[0], ret_ln_w[0], ret_ln_b[0], latent=True, r0=state_ret[:, 0],
                    rope=rope_d, dst=od)
    y_ctx, y_lat = _mix_ffn([x], oc, od, w_out_cd[0].astype(BF16), norm2_g[1], mods[1],
                            w_ff_gate[1].astype(BF16), w_ff_up[1].astype(BF16), w_ff_down[1].astype(BF16),
                            split_out=True)
    y_prompt = y_ctx.reshape(BATCH, SEQ, d)
    y_sample = y_lat.reshape(DEC_BATCH, DEC_SEQ, d)
    new_k = k_ctx.reshape(BATCH, 1, SEQ, H_A, 2, HD_A)
    new_v = v_ctx.reshape(BATCH, 1, SEQ, H_A, VD_A)
    new_rwkv = sfin_ctx[:, None]
    new_s5_re = s5_fin_re[:, None]
    new_s5_im = s5_fin_im[:, None]
    new_ret = rfin[:, None]
    return (y_prompt, y_sample, new_k, new_v, new_rwkv, new_s5_re, new_s5_im, new_ret)
```

```python
import functools
import math

import numpy as np
import jax
import jax.numpy as jnp
from jax import lax
from jax.experimental import pallas as pl
from jax.experimental.pallas import tpu as pltpu

F32 = jnp.float32
BF16 = jnp.bfloat16
HIGHEST = lax.Precision.HIGHEST

D_MODEL = 1024
BATCH = 32
SEQ = 256
DEC_BATCH = 2
DEC_SEQ = 1024
PAST_LEN = 256
GRID_W = 64
H_A = 4
HD_A = 64
VD_A = 128
H_B = 8
HD_B = 64
MIX = 512
LORA_W = 64
LORA_A = 64
LORA_G = 128
S5_GROUP = 16
G_C = 32
P_C = 64
S5_STATE = G_C * P_C
H_D = 4
HD_D = 128
CHUNK = 128
D_FF = 2816
IN_AB = 3328
IN_CD = 2560
ROPE_THETA = 10000.0
NORM_EPS = 1e-6
RWKV_GN_EPS = 64e-5
RET_GN_EPS = 1e-5

M_CTX = BATCH * SEQ
M_LAT = DEC_BATCH * DEC_SEQ
M_ALL = M_CTX + M_LAT
MOD_ROWS = 8
RWKV_CHUNK = 64
VMEM_LIMIT = 56 * 1024 * 1024

NN = (((1,), (0,)), ((), ()))
NT = (((1,), (1,)), ((), ()))
TN = (((0,), (0,)), ((), ()))


def _params(*sem):
    return pltpu.CompilerParams(dimension_semantics=sem, vmem_limit_bytes=VMEM_LIMIT)


def _bdot(a, b, dims=NN):
    return lax.dot_general(a.astype(BF16), b.astype(BF16), dims, preferred_element_type=F32)


def _hdot(a, b, dims=NN):
    return lax.dot_general(a, b, dims, precision=HIGHEST, preferred_element_type=F32)


def _split_dot(x, m):
    hi = x.astype(BF16)
    lo = (x - hi.astype(F32)).astype(BF16)
    return (jnp.dot(hi, m, preferred_element_type=F32) + jnp.dot(lo, m, preferred_element_type=F32))


def _seg_matrix(n, shift, val):
    r = lax.broadcasted_iota(jnp.int32, (n, n), 0) >> shift
    c = lax.broadcasted_iota(jnp.int32, (n, n), 1) >> shift
    return jnp.where(r == c, val, 0.0).astype(BF16)


def _sigmoid(x):
    return jax.nn.sigmoid(x)


def _silu(x):
    return x * jax.nn.sigmoid(x)


def _softplus(x):
    return jnp.maximum(x, 0.0) + jnp.log(1.0 + jnp.exp(-jnp.abs(x)))


def _mod_row(tile, tm):
    r0 = tile * tm
    return jnp.where(r0 < M_CTX, 0, 1 + (r0 - M_CTX) // DEC_SEQ)


def _norm_mod(x, g, sc_ref, sh_ref, row):
    y = x * lax.rsqrt(jnp.mean(x * x, axis=-1, keepdims=True) + NORM_EPS) * g
    return y * (1.0 + sc_ref[pl.ds(row, 1), :]) + sh_ref[pl.ds(row, 1), :]


def _mod_kernel(c_ref, w_ref, b_ref, o_ref):
    o_ref[0] = _bdot(_silu(c_ref[...]), w_ref[0]) + b_ref[0]


def _modulation(cvec, w_mod, b_mod):
    depth, d, n6 = w_mod.shape
    tn = 1536
    return pl.pallas_call(
        _mod_kernel,
        grid=(depth, n6 // tn),
        in_specs=[pl.BlockSpec((MOD_ROWS, d), lambda l, j: (0, 0)),
                  pl.BlockSpec((1, d, tn), lambda l, j: (l, 0, j)),
                  pl.BlockSpec((1, 1, tn), lambda l, j: (l, 0, j))],
        out_specs=pl.BlockSpec((1, MOD_ROWS, tn), lambda l, j: (l, 0, j)),
        out_shape=jax.ShapeDtypeStruct((depth, MOD_ROWS, n6), F32),
        compiler_params=_params("parallel", "parallel"),
        name="modulation",
    )(cvec, w_mod, b_mod.reshape(depth, 1, n6))


def _tile_specs(tm, d):
    nc = M_CTX // tm
    return [pl.BlockSpec((tm, d), lambda i: (jnp.minimum(i, nc - 1), 0)),
            pl.BlockSpec((tm, d), lambda i: (jnp.maximum(i - nc, 0), 0))]


def _pick_rows(refs, tm):
    if len(refs) == 1:
        return refs[0][...]
    return jnp.where(pl.program_id(0) < M_CTX // tm, refs[0][...], refs[1][...])


def _norm_linear_kernel(*refs, tm):
    *x_refs, g_ref, sc_ref, sh_ref, w_ref, o_ref = refs
    row = _mod_row(pl.program_id(0), tm)
    x = _pick_rows(x_refs, tm)
    nsub = 2
    rows = [slice(i * tm // nsub, (i + 1) * tm // nsub) for i in range(nsub)]
    hs = [_norm_mod(x[r], g_ref[...], sc_ref, sh_ref, row).astype(BF16) for r in rows]
    for r, h in zip(rows, hs):
        o_ref[r, :] = jnp.dot(h, w_ref[...], preferred_element_type=F32)


def _norm_linear(xs, g, mods, w_bf16):
    tm = 512
    d = D_MODEL
    n = w_bf16.shape[1]
    x_specs = _tile_specs(tm, d) if len(xs) == 2 else [pl.BlockSpec((tm, d), lambda i: (i, 0))]
    return pl.pallas_call(
        functools.partial(_norm_linear_kernel, tm=tm),
        grid=(M_ALL // tm,),
        in_specs=x_specs + [pl.BlockSpec((1, d), lambda i: (0, 0)),
                            pl.BlockSpec((MOD_ROWS, d), lambda i: (0, 1)),
                            pl.BlockSpec((MOD_ROWS, d), lambda i: (0, 0)),
                            pl.BlockSpec((d, n), lambda i: (0, 0))],
        out_specs=pl.BlockSpec((tm, n), lambda i: (i, 0)),
        out_shape=jax.ShapeDtypeStruct((M_ALL, n), F32),
        compiler_params=_params("arbitrary"),
        name="norm_linear",
    )(*xs, g.reshape(1, d), mods, mods, w_bf16)


def _mix_ffn_kernel(*refs, tm, ck, n_in, n_out):
    x_refs = refs[:n_in]
    (a_ref, b_ref, wa_ref, wb_ref, g1_ref, ng_ref, sc_ref, sh_ref, g2_ref,
     wg_ref, wu_ref, wd_ref) = refs[n_in:n_in + 12]
    o_refs = refs[n_in + 12:]
    row = _mod_row(pl.program_id(0), tm)
    mix = (jnp.dot(a_ref[...].astype(BF16), wa_ref[...], preferred_element_type=F32)
           + jnp.dot(b_ref[...].astype(BF16), wb_ref[...], preferred_element_type=F32))
    x = _pick_rows(x_refs, tm) + g1_ref[pl.ds(row, 1), :] * mix
    h = _norm_mod(x, ng_ref[...], sc_ref, sh_ref, row).astype(BF16)
    acc = jnp.zeros((tm, D_MODEL), F32)
    for c in range(D_FF // ck):
        gg = jnp.dot(h, wg_ref[:, c * ck:(c + 1) * ck], preferred_element_type=F32)
        uu = jnp.dot(h, wu_ref[:, c * ck:(c + 1) * ck], preferred_element_type=F32)
        act = (_silu(gg) * uu).astype(BF16)
        acc = acc + jnp.dot(act, wd_ref[c * ck:(c + 1) * ck, :], preferred_element_type=F32)
    y = x + g2_ref[pl.ds(row, 1), :] * acc
    if n_out == 1:
        o_refs[0][...] = y
    else:
        is_ctx = pl.program_id(0) < M_CTX // tm

        @pl.when(is_ctx)
        def _():
            o_refs[0][...] = y

        @pl.when(jnp.logical_not(is_ctx))
        def _():
            o_refs[1][...] = y


def _mix_ffn(xs, oa, ob, w_out_bf16, norm_g, mods, wg, wu, wd, split_out):
    tm, ck = 512, 256
    d = D_MODEL
    row_spec = lambda n: pl.BlockSpec((tm, n), lambda i: (i, 0))
    const = lambda shape, idx: pl.BlockSpec(shape, lambda i: idx, pipeline_mode=pl.Buffered(1))
    mod = lambda j: pl.BlockSpec((MOD_ROWS, d), lambda i: (0, j))
    x_specs = _tile_specs(tm, d) if len(xs) == 2 else [row_spec(d)]
    if split_out:
        out_specs = _tile_specs(tm, d)
        out_shape = [jax.ShapeDtypeStruct((M_CTX, d), F32), jax.ShapeDtypeStruct((M_LAT, d), F32)]
    else:
        out_specs, out_shape = row_spec(d), jax.ShapeDtypeStruct((M_ALL, d), F32)
    return pl.pallas_call(
        functools.partial(_mix_ffn_kernel, tm=tm, ck=ck, n_in=len(xs), n_out=2 if split_out else 1),
        grid=(M_ALL // tm,),
        in_specs=x_specs + [row_spec(MIX), row_spec(MIX),
                            const((MIX, d), (0, 0)), const((MIX, d), (1, 0)),
                            mod(2),
                            pl.BlockSpec((1, d), lambda i: (0, 0)),
                            mod(4), mod(3), mod(5),
                            const((d, D_FF), (0, 0)), const((d, D_FF), (0, 0)), const((D_FF, d), (0, 0))],
        out_specs=out_specs, out_shape=out_shape,
        compiler_params=_params("arbitrary"),
        name="mix_ffn",
    )(*xs, oa, ob, w_out_bf16, w_out_bf16, mods, norm_g.reshape(1, d), mods, mods, mods, wg, wu, wd)


def _qk_norm(x, gain, segm):
    ms = _split_dot(x * x, segm)
    return x * lax.rsqrt(ms + NORM_EPS) * gain


def _rope_pairs(x, cosf, sinf, half):
    lane = lax.broadcasted_iota(jnp.int32, x.shape, 1)
    first = (lane & (2 * half - 1)) < half
    n = x.shape[1]
    partner = jnp.where(first, pltpu.roll(x, n - half, axis=1), pltpu.roll(x, half, axis=1))
    return x * cosf + partner * sinf


def _attn_kernel(*refs, latent, lam_init):
    if latent:
        (q_ref, k_ref, v_ref, ck_ref, cv_ref, cosq_ref, sinq_ref, cosk_ref, sink_ref,
         gain_ref, lam_ref, sub_ref, _, o_ref, kall, vall) = refs
    else:
        q_ref, k_ref, v_ref, gain_ref, lam_ref, sub_ref, o_ref, kn_ref, vo_ref, kall, vall = refs
    w = 2 * HD_A
    segm = _seg_matrix(w, 6, 1.0 / HD_A)
    gains = gain_ref[...]
    cols = [slice(h * w, (h + 1) * w) for h in range(H_A)]

    @pl.when(pl.program_id(1) == 0)
    def _():
        k = [_qk_norm(k_ref[:, c], gains[1:2], segm) for c in cols]
        if latent:
            k = [_rope_pairs(x, cosk_ref[...], sink_ref[...], HD_A // 2) for x in k]
            for c, x in zip(cols, k):
                kall[0:PAST_LEN, c] = ck_ref[:, c].astype(BF16)
                kall[PAST_LEN:, c] = x.astype(BF16)
                vall[0:PAST_LEN, c] = cv_ref[:, c].astype(BF16)
                vall[PAST_LEN:, c] = v_ref[:, c].astype(BF16)
        else:
            for h, (c, x) in enumerate(zip(cols, k)):
                xt = x.T
                kn_ref[0, h, 0] = xt[:HD_A]
                kn_ref[0, h, 1] = xt[HD_A:]
                vo_ref[pl.ds(h, v_ref.shape[0], stride=H_A), :] = v_ref[:, c]
                kall[:, c] = x.astype(BF16)
                vall[:, c] = v_ref[:, c].astype(BF16)

    q = [_qk_norm(q_ref[:, c], gains[0:1], segm) for c in cols]
    if latent:
        q = [_rope_pairs(x, cosq_ref[...], sinq_ref[...], HD_A // 2) for x in q]
    lv = lam_ref[...]
    lam = (jnp.exp(jnp.sum(lv[0:1] * lv[1:2], axis=1, keepdims=True))
           - jnp.exp(jnp.sum(lv[2:3] * lv[3:4], axis=1, keepdims=True)) + lam_init)
    scale = HD_A ** -0.5
    comp0 = lax.broadcasted_iota(jnp.int32, q[0].shape, 1) < HD_A
    qc = [jnp.where(comp0, *sel).astype(BF16) for x in q for sel in ((x, 0.0), (0.0, x))]
    s = [lax.dot_general(qc[i], kall[:, cols[i // 2]], NT, preferred_element_type=F32) * scale
         for i in range(2 * H_A)]
    e = [jnp.exp(x - jnp.max(x, axis=-1, keepdims=True)) for x in s]
    p = [x / jnp.sum(x, axis=-1, keepdims=True) for x in e]
    att = [(p[2 * h] - lam * p[2 * h + 1]).astype(BF16) for h in range(H_A)]
    o = [jnp.dot(att[h], vall[:, cols[h]], preferred_element_type=F32) for h in range(H_A)]
    o = [x * lax.rsqrt(jnp.mean(x * x, axis=-1, keepdims=True) + NORM_EPS) * sub_ref[...] for x in o]
    for c, x in zip(cols, o):
        o_ref[:, c] = x * (1.0 - lam_init)


def _attention(p, gain2, lambda_qk, subln_g, lam_init, latent, cache_k=None, cache_v=None, rope=None,
               dst=None):
    w = 2 * HD_A
    if latent:
        nseq, t, tq, rb0, s_len = DEC_BATCH, DEC_SEQ, 128, M_CTX // DEC_SEQ, PAST_LEN + DEC_SEQ
    else:
        nseq, t, tq, rb0, s_len = BATCH, SEQ, SEQ, 0, SEQ
    nq = t // tq
    qoff = rb0 * nq
    full = lambda shape: pl.BlockSpec(shape, lambda b, i: (0,) * len(shape))
    in_specs = [pl.BlockSpec((tq, MIX), lambda b, i: (qoff + b * nq + i, 0)),
                pl.BlockSpec((t, MIX), lambda b, i: (rb0 + b, 1)),
                pl.BlockSpec((t, MIX), lambda b, i: (rb0 + b, 2))]
    args = [p, p, p]
    if latent:
        cosf, sinf = rope
        in_specs += [pl.BlockSpec((PAST_LEN, MIX), lambda b, i: (b, 0)),
                     pl.BlockSpec((PAST_LEN, MIX), lambda b, i: (b, 0)),
                     pl.BlockSpec((tq, w), lambda b, i: (i, 0)),
                     pl.BlockSpec((tq, w), lambda b, i: (i, 0)),
                     full((t, w)), full((t, w))]
        args += [cache_k, cache_v, cosf, sinf, cosf, sinf]
    in_specs += [full((2, w)), full((4, HD_A)), full((1, w))]
    args += [gain2, lambda_qk, subln_g.reshape(1, w)]
    o_spec = pl.BlockSpec((tq, MIX), lambda b, i: (qoff + b * nq + i, 0))
    o_shape = jax.ShapeDtypeStruct((M_ALL, MIX), F32)
    aliases = {}
    if latent:
        out_specs, out_shape = o_spec, o_shape
        in_specs.append(pl.BlockSpec(memory_space=pl.ANY))
        args.append(dst)
        aliases = {len(args) - 1: 0}
    else:
        out_specs = [o_spec, pl.BlockSpec((1, H_A, 2, HD_A, t), lambda b, i: (b, 0, 0, 0, 0)),
                     pl.BlockSpec((t * H_A, VD_A), lambda b, i: (b, 0))]
        out_shape = [o_shape, jax.ShapeDtypeStruct((nseq, H_A, 2, HD_A, t), F32),
                     jax.ShapeDtypeStruct((nseq * t * H_A, VD_A), F32)]
    return pl.pallas_call(
        functools.partial(_attn_kernel, latent=latent, lam_init=lam_init),
        grid=(nseq, nq),
        in_specs=in_specs, out_specs=out_specs, out_shape=out_shape,
        scratch_shapes=[pltpu.VMEM((s_len, MIX), BF16), pltpu.VMEM((s_len, MIX), BF16)],
        input_output_aliases=aliases,
        compiler_params=_params("parallel", "arbitrary"),
        name="diff_attention_lat" if latent else "diff_attention_ctx",
    )(*args)


def _centred_shift(x, mu):
    t = x.shape[0]
    row = lax.broadcasted_iota(jnp.int32, x.shape, 0)
    prev = jnp.where(row == 0, 0.0, pltpu.roll(x, 1, axis=0))
    nxt = jnp.where(row == t - 1, 0.0, pltpu.roll(x, t - 1, axis=0))
    return x + (0.5 * (prev + nxt) - x) * mu


def _seg_sum(x, segm):
    return jnp.concatenate([_split_dot(x[:, j * 128:(j + 1) * 128], segm) for j in range(x.shape[1] // 128)],
                           axis=1)


def _rwkv_prep_kernel(r_ref, k_ref, v_ref, l_ref, mur_ref, muk_ref, muv_ref, mul_ref,
                      kk_ref, ka_ref, rk_ref, w0_ref, wup_ref, a0_ref, aup_ref, gup_ref,
                      ro_ref, ldf_ref, ldb_ref, kbo_ref, vbo_ref, kko_ref, ao_ref, gate_ref, bonus_ref):
    seg1 = _seg_matrix(128, 6, 1.0)
    r = _centred_shift(r_ref[...], mur_ref[...])
    kb = _centred_shift(k_ref[...], muk_ref[...])
    vb = _centred_shift(v_ref[...], muv_ref[...])
    lo = _centred_shift(l_ref[...], mul_ref[...])
    xw = lo[:, 0:LORA_W]
    xa = lo[:, LORA_W:LORA_W + LORA_A]
    xg = lo[:, LORA_W + LORA_A:]
    kk = kb * kk_ref[...]
    kk = kk * lax.rsqrt(_seg_sum(kk * kk, seg1) + 1e-12)
    a = _sigmoid(a0_ref[...] + _bdot(xa, aup_ref[...]))
    kb2 = kb * (1.0 + (a - 1.0) * ka_ref[...])
    lw = jnp.tanh(xw)
    for dr, ld_ref in enumerate((ldf_ref, ldb_ref)):
        z = w0_ref[dr:dr + 1, :] + _bdot(lw, wup_ref[dr])
        logw = -_softplus(-z) - 0.5
        ld_ref[...] = -jnp.exp(logw)
    gate_ref[...] = _bdot(_sigmoid(xg), gup_ref[...])
    bonus_ref[...] = _seg_sum(r * kb2 * rk_ref[...], seg1) * vb
    ro_ref[...] = r
    kbo_ref[...] = kb2
    vbo_ref[...] = vb
    kko_ref[...] = kk
    ao_ref[...] = a


N_PREP_IN = 16
N_PREP_OUT = 9


def _split2(x):
    hi = x.astype(BF16)
    return hi, (x - hi.astype(F32)).astype(BF16)


RWKV_INV_BLOCK = 16


def _rwkv_masks(c, rev):
    ti = lax.broadcasted_iota(jnp.int32, (c, c), 0)
    si = lax.broadcasted_iota(jnp.int32, (c, c), 1)
    tri = jnp.where((si >= ti) if rev else (si <= ti), 1.0, 0.0).astype(BF16)
    t4 = lax.broadcasted_iota(jnp.int32, (4 * c, 4 * c), 0)
    s4 = lax.broadcasted_iota(jnp.int32, (4 * c, 4 * c), 1)
    tm, sm = t4 & (c - 1), s4 & (c - 1)
    strict = (sm > tm) if rev else (sm < tm)
    incl = (sm >= tm) if rev else (sm <= tm)
    same_head = ((t4 // c) & 1) == ((s4 // c) & 1)
    top = t4 < 2 * c
    gmask = same_head & ((top & strict) | (~top & incl))
    t2 = lax.broadcasted_iota(jnp.int32, (2 * c, 2 * c), 0)
    s2 = lax.broadcasted_iota(jnp.int32, (2 * c, 2 * c), 1)
    same = lambda n: (t2 // n) == (s2 // n)
    levels = []
    n = RWKV_INV_BLOCK
    while n < c:
        levels.append(same(2 * n) & ~same(n))
        n *= 2
    f = lambda m: jnp.where(m, 1.0, 0.0)
    return tri, f(gmask), f(same(RWKV_INV_BLOCK)), tuple(f(m) for m in levels), f(same(c)), f(t2 == s2)


def _keep(mask01, x):
    return jnp.where(mask01 > 0.5, x, 0.0)


def _tri_inverse(a, diag_blk, levels, eye):
    n = a[0].shape[0]
    d = [_keep(diag_blk, x) for x in a]
    t = [eye + x for x in d]
    p = [_bdot(x, x) for x in d]
    for _ in range(int(math.log2(RWKV_INV_BLOCK)) - 2):
        res = [_bdot(jnp.concatenate([pi, ti], axis=0), pi) for pi, ti in zip(p, t)]
        p = [x[:n] for x in res]
        t = [ti + x[n:] for ti, x in zip(t, res)]
    t = [ti + _bdot(ti, pi) for ti, pi in zip(t, p)]
    for off in levels:
        half = [_bdot(ti, _keep(off, x)) for ti, x in zip(t, a)]
        t = [ti + _bdot(x, ti) for ti, x in zip(t, half)]
    return t


def _rwkv_pair_chunks(ins, sts, tris, gmasks, diag_blk, levels, same_head, eye):
    c, w = ins[0][0].shape
    hd = w // 2
    nch = range(len(ins))
    r, ld, kb, vb, kk, a = (list(z) for z in zip(*ins))
    split = [_split2(x) for x in ld]
    lcum = [jnp.dot(tris[i], jnp.concatenate(split[i], axis=1), preferred_element_type=F32) for i in nch]
    lcum = [x[:, :w] + x[:, w:] for x in lcum]
    ltot = [jnp.sum(x, axis=0, keepdims=True) for x in ld]
    beta = [kk[i] * a[i] for i in nch]
    eneg = [jnp.exp(-x) for x in lcum]
    abar = [-kk[i] * jnp.exp(lcum[i] - ld[i]) for i in nch]
    rbar = [r[i] * jnp.exp(lcum[i]) for i in nch]
    bt = [(beta[i] * eneg[i]).astype(BF16) for i in nch]
    kt = [(kb[i] * eneg[i]).astype(BF16) for i in nch]
    vbb = [x.astype(BF16) for x in vb]
    head0 = lax.broadcasted_iota(jnp.int32, (c, w), 1) < hd
    pick = lambda res: jnp.where(head0, res[:c], res[c:])
    arst = [_bdot(jnp.concatenate([abar[i], rbar[i]], axis=0), sts[i], NT) for i in nch]
    lhs = [jnp.concatenate([jnp.where(head0, abar[i], 0.0), jnp.where(head0, 0.0, abar[i]),
                            jnp.where(head0, rbar[i], 0.0), jnp.where(head0, 0.0, rbar[i])], axis=0) for i in nch]
    g = [_keep(gmasks[i], _bdot(lhs[i], jnp.concatenate([bt[i], bt[i], kt[i], kt[i]], axis=0), NT))
         for i in nch]
    x = [arst[i][:c] + pick(_bdot(g[i][:2 * c, 2 * c:], jnp.concatenate([vbb[i], vbb[i]], axis=0))) for i in nch]
    tinv = _tri_inverse([gi[:2 * c, :2 * c] for gi in g], diag_blk, levels, eye)
    u = [pick(_bdot(tinv[i], jnp.concatenate([x[i], x[i]], axis=0))) for i in nch]
    ub = [z.astype(BF16) for z in u]
    y = [arst[i][c:] + pick(_bdot(g[i][2 * c:], jnp.concatenate([ub[i], ub[i], vbb[i], vbb[i]], axis=0)))
         for i in nch]
    erem = [jnp.exp(ltot[i] - lcum[i]) for i in nch]
    bkh = [jnp.concatenate([beta[i] * erem[i], kb[i] * erem[i]], axis=0) for i in nch]
    st_new = [jnp.exp(ltot[i]) * sts[i]
              + _keep(same_head, _bdot(jnp.concatenate([ub[i], vbb[i]], axis=0), bkh[i], TN)) for i in nch]
    return y, st_new


def _rwkv_scan_kernel(*refs, t, c, npair, latent):
    prep_in, (lnw_ref, lnb_ref), rest = refs[:N_PREP_IN], refs[N_PREP_IN:N_PREP_IN + 2], refs[N_PREP_IN + 2:]
    if latent:
        s0_ref, _, o_ref, *scratch = rest
    else:
        o_ref, sfin_ref, *scratch = rest
    *prep_out, yb_ref = scratch
    _rwkv_prep_kernel(*prep_in, *prep_out)
    r_ref, ldf_ref, ldb_ref, kb_ref, vb_ref, kk_ref, a_ref, gate_ref, bonus_ref = prep_out
    n = t // c
    w = 2 * HD_B
    masks = [_rwkv_masks(c, rev) for rev in (False, True)]

    def body(ci, states):
        ins, tris, gmasks, dsts = [], [], [], []
        for dr, (ld_ref, dst) in enumerate(((ldf_ref, o_ref), (ldb_ref, yb_ref))):
            cj = ci if dr == 0 else n - 1 - ci
            rows = pl.ds(pl.multiple_of(cj * c, c), c)
            for p in range(npair):
                cols = slice(p * w, (p + 1) * w)
                ins.append(tuple(ref[rows, cols] for ref in (r_ref, ld_ref, kb_ref, vb_ref, kk_ref, a_ref)))
                tris.append(masks[dr][0])
                gmasks.append(masks[dr][1])
                dsts.append((dst, rows, cols))
        ys, new_states = _rwkv_pair_chunks(ins, list(states), tris, gmasks, *masks[0][2:])
        for (dst, rows, cols), y in zip(dsts, ys):
            dst[rows, cols] = y
        return tuple(new_states)

    if latent:
        zero = jnp.zeros((HD_B, HD_B), F32)
        init = tuple(jnp.concatenate([jnp.concatenate([s0_ref[0, dr, 2 * p], zero], axis=1),
                                      jnp.concatenate([zero, s0_ref[0, dr, 2 * p + 1]], axis=1)], axis=0)
                     for dr in range(2) for p in range(npair))
    else:
        init = tuple(jnp.zeros((w, w), F32) for _ in range(2 * npair))
    fin = lax.fori_loop(0, n, body, init)
    segm = _seg_matrix(w, 6, 1.0 / HD_B)
    for p in range(npair):
        if not latent:
            for dr in range(2):
                st = fin[dr * npair + p]
                sfin_ref[0, dr, 2 * p] = st[:HD_B, :HD_B]
                sfin_ref[0, dr, 2 * p + 1] = st[HD_B:, HD_B:]
        cols = slice(p * w, (p + 1) * w)
        y = o_ref[:, cols] + yb_ref[:, cols]
        yc = y - _split_dot(y, segm)
        yn = yc * lax.rsqrt(_split_dot(yc * yc, segm) + RWKV_GN_EPS)
        o_ref[:, cols] = (yn * lnw_ref[:, cols] + lnb_ref[:, cols] + bonus_ref[:, cols]) * gate_ref[:, cols]


def _rwkv(p, prm, ln_w, ln_b, latent, s0=None, dst=None):
    mu, k_k, k_a, r_k, w0, w_up, a0, a_up, g_up = prm
    nseq, t, rb0 = (DEC_BATCH, DEC_SEQ, M_CTX // DEC_SEQ) if latent else (BATCH, SEQ, 0)
    npair = H_B // 2
    lw = LORA_W + LORA_A + LORA_G
    c0 = (IN_AB - 3 * MIX - lw) // MIX
    cl = (IN_AB - lw) // lw
    full = lambda *shape: pl.BlockSpec(shape, lambda b: (0,) * len(shape))
    st_spec = pl.BlockSpec((1, 2, H_B, HD_B, HD_B), lambda b: (b, 0, 0, 0, 0))
    o_spec = pl.BlockSpec((t, MIX), lambda b: (rb0 + b, 0))
    o_shape = jax.ShapeDtypeStruct((M_ALL, MIX), F32)
    in_specs = [pl.BlockSpec((t, MIX), lambda b: (rb0 + b, c0)),
                pl.BlockSpec((t, MIX), lambda b: (rb0 + b, c0 + 1)),
                pl.BlockSpec((t, MIX), lambda b: (rb0 + b, c0 + 2)),
                pl.BlockSpec((t, lw), lambda b: (rb0 + b, cl)),
                full(1, MIX), full(1, MIX), full(1, MIX), full(1, lw),
                full(1, MIX), full(1, MIX), full(1, MIX),
                full(2, MIX), full(2, LORA_W, MIX), full(1, MIX), full(LORA_A, MIX), full(LORA_G, MIX),
                full(1, MIX), full(1, MIX)]
    args = [p, p, p, p,
            mu[None, 0:MIX], mu[None, MIX:2 * MIX], mu[None, 2 * MIX:3 * MIX], mu[None, 3 * MIX:],
            k_k.reshape(1, MIX), k_a.reshape(1, MIX), r_k.reshape(1, MIX), w0, w_up,
            a0.reshape(1, MIX), a_up, g_up, ln_w.reshape(1, MIX), ln_b.reshape(1, MIX)]
    if latent:
        args += [s0, dst]
        in_specs += [st_spec, pl.BlockSpec(memory_space=pl.ANY)]
        out_specs, out_shape, aliases = o_spec, o_shape, {len(args) - 1: 0}
    else:
        out_specs = [o_spec, st_spec]
        out_shape = [o_shape, jax.ShapeDtypeStruct((nseq, 2, H_B, HD_B, HD_B), F32)]
        aliases = {}
    return pl.pallas_call(
        functools.partial(_rwkv_scan_kernel, t=t, c=RWKV_CHUNK, npair=npair, latent=latent),
        grid=(nseq,),
        in_specs=in_specs, out_specs=out_specs, out_shape=out_shape,
        scratch_shapes=[pltpu.VMEM((t, MIX), F32)] * (N_PREP_OUT + 1),
        input_output_aliases=aliases,
        compiler_params=_params("parallel"),
        name="rwkv_lat" if latent else "rwkv_ctx",
    )(*args)


S5_L = 16
S5_ROWS = M_ALL // S5_L
S5_CW = S5_L * S5_GROUP
S5_GS = 8
S5_PW = 2 * P_C
S5_TAB = 24
S5_PRM = 2 * S5_GROUP + 8
S5_NC_CTX = SEQ // S5_L
S5_NC_LAT = DEC_SEQ // S5_L
S5_CTX_ROWS = BATCH * S5_NC_CTX


def _gelu_tanh(x):
    return 0.5 * x * (1.0 + jnp.tanh(math.sqrt(2.0 / math.pi) * (x + 0.044715 * (x * x * x))))


def _cmul(ar, ai, br, bi):
    return ar * br - ai * bi, ar * bi + ai * br


def _dot3(a, b, dims):
    ah, al = _split2(a)
    bh, bl = _split2(b)
    d = lambda x, y: lax.dot_general(x, y, dims, preferred_element_type=F32)
    return d(ah, bh) + d(ah, bl) + d(al, bh)


def _s5_param_kernel(prm_ref, tab_ref, r_ref, st_ref):
    nl, ns = S5_L, S5_GROUP
    lane_blk = lax.broadcasted_iota(jnp.int32, (ns, S5_CW), 1) // ns
    t_rows = [jnp.zeros((ns, S5_CW), F32) for _ in range(nl)]
    e_parts = []
    for d in range(2):
        part = lambda lo, n: (prm_ref[0, d, 0, lo:lo + n, :], prm_ref[0, d, 1, lo:lo + n, :])
        bb = _cmul(*part(2 * ns, 1), *part(0, ns))
        cc = part(ns, ns)
        pw = lambda k: (tab_ref[0, d, 0, k:k + 1, :], tab_ref[0, d, 1, k:k + 1, :])
        steps = range(nl)
        if d == 0:
            ke, ks, kk = [nl - 1 - j for j in steps], [j + 1 for j in steps], list(steps)
        else:
            ke, ks, kk = list(steps), [nl - j for j in steps], [nl - 1 - j for j in steps]
        stack = lambda xs: (jnp.concatenate([x[0] for x in xs], axis=0), jnp.concatenate([x[1] for x in xs], axis=0))
        e_re, e_im = stack([_cmul(*bb, *pw(k)) for k in ke])
        s_re, s_im = stack([_cmul(*cc, *pw(k)) for k in ks])
        k_re, k_im = stack([_cmul(*cc, *pw(k)) for k in kk])
        st_ref[0, d] = jnp.concatenate([s_re, -s_im], axis=1).astype(BF16)
        krow = _dot3(bb[0], k_re, NT) - _dot3(bb[1], k_im, NT)
        for j in steps:
            if d == 0:
                shifted, keep = pltpu.roll(krow, ns * j, axis=1), lane_blk >= j
            else:
                shifted, keep = pltpu.roll(krow, (S5_CW - ns * (nl - 1 - j)) % S5_CW, axis=1), lane_blk <= j
            t_rows[j] = t_rows[j] + jnp.where(keep, shifted, 0.0)
        e_parts += [e_re, e_im]
    r_ref[0] = jnp.concatenate([jnp.concatenate(t_rows, axis=0)] + e_parts, axis=1).astype(BF16)


def _s5_params(prm, tab):
    return pl.pallas_call(
        _s5_param_kernel,
        grid=(G_C,),
        in_specs=[pl.BlockSpec((1, 2, 2, S5_PRM, P_C), lambda g: (g, 0, 0, 0, 0)),
                  pl.BlockSpec((1, 2, 2, S5_TAB, P_C), lambda g: (g, 0, 0, 0, 0))],
        out_specs=[pl.BlockSpec((1, S5_CW, 2 * S5_CW), lambda g: (g, 0, 0)),
                   pl.BlockSpec((1, 2, S5_CW, S5_PW), lambda g: (g, 0, 0, 0))],
        out_shape=[jax.ShapeDtypeStruct((G_C, S5_CW, 2 * S5_CW), BF16),
                   jax.ShapeDtypeStruct((G_C, 2, S5_CW, S5_PW), BF16)],
        compiler_params=_params("parallel"),
        name="s5_params",
    )(prm, tab)


def _s5_core_kernel(u_ref, r_ref, st_ref, tab_ref, h0_ref, o_ref, hfin_ref, ug_scr, yg_scr, e_scr, hp_scr):
    ns, nl = S5_GROUP, S5_L
    per_v = 128 // ns
    blk = lax.broadcasted_iota(jnp.int32, (8, 128), 1) // ns

    def merge(select):
        acc = select(0)
        for b in range(1, per_v):
            acc = jnp.where(blk == b, select(b), acc)
        return acc

    def shuffle(srcs):
        rolled = []
        for s in range(per_v):
            m = merge(lambda b: srcs[(b + s) % per_v])
            rolled.append(m if s == 0 else pltpu.roll(m, s * ns, axis=1))
        return [merge(lambda b: rolled[(b - a) % per_v]) for a in range(per_v)]

    def tiles(regroup):
        def ctx_tile(ti, _):
            c, b0 = ti // (BATCH // 8), (ti % (BATCH // 8)) * 8
            regroup(b0 * SEQ + c * nl, SEQ, pl.multiple_of(c * BATCH + b0, 8))
            return 0

        def lat_tile(ti, _):
            b, c0 = ti // (S5_NC_LAT // 8), (ti % (S5_NC_LAT // 8)) * 8
            regroup(M_CTX + b * DEC_SEQ + c0 * nl, nl, pl.multiple_of(S5_CTX_ROWS + b * S5_NC_LAT + c0, 8))
            return 0

        lax.fori_loop(0, S5_CTX_ROWS // 8, ctx_tile, 0, unroll=4)
        lax.fori_loop(0, (S5_ROWS - S5_CTX_ROWS) // 8, lat_tile, 0, unroll=4)

    def regroup_in(tok, stride, row):
        uj = [u_ref[pl.ds(tok + j, 8, stride=stride), :] for j in range(nl)]
        for jh in range(nl // per_v):
            for g, v in enumerate(shuffle(uj[jh * per_v:(jh + 1) * per_v])):
                ug_scr[g, pl.ds(row, 8), jh * 128:(jh + 1) * 128] = v

    tiles(regroup_in)

    chains = [(g, d) for g in range(S5_GS) for d in range(2)]
    for g in range(S5_GS):
        a = jnp.dot(ug_scr[g].astype(BF16), r_ref[g], preferred_element_type=F32)
        yg_scr[g] = a[:, 0:S5_CW]
        for d in range(2):
            e_scr[g, d] = a[:, S5_CW + d * S5_PW:S5_CW + (d + 1) * S5_PW]
    lam = []
    for g, d in chains:
        lr, li = (tab_ref[g, d, ri, nl:nl + 1, :] for ri in range(2))
        lam.append((jnp.concatenate([lr, lr], axis=1), jnp.concatenate([-li, li], axis=1)))

    def advance(c, hs, latent):
        nc = S5_NC_LAT if latent else S5_NC_CTX
        out = []
        for i, (g, d) in enumerate(chains):
            cc = c if d == 0 else nc - 1 - c
            rows = (pl.ds(S5_CTX_ROWS + cc, DEC_BATCH, stride=nc) if latent
                    else pl.ds(pl.multiple_of(cc * BATCH, BATCH), BATCH))
            hp_scr[g, d, rows, :] = hs[i]
            out.append(lam[i][0] * hs[i] + lam[i][1] * pltpu.roll(hs[i], P_C, axis=1) + e_scr[g, d, rows, :])
        return out

    def both(c, carry):
        hc, hl = carry
        return tuple(advance(c, hc, False)), tuple(advance(c, hl, True))

    h_ctx = tuple(jnp.zeros((BATCH, S5_PW), F32) for _ in chains)
    h_lat = tuple(h0_ref[g, d] for g, d in chains)
    h_ctx, h_lat = lax.fori_loop(0, S5_NC_CTX, both, (h_ctx, h_lat))
    lax.fori_loop(S5_NC_CTX, S5_NC_LAT, lambda c, hl: tuple(advance(c, hl, True)), h_lat)
    for i, (g, d) in enumerate(chains):
        hfin_ref[g, d] = h_ctx[i]
    for g in range(S5_GS):
        y = yg_scr[g]
        for d in range(2):
            y = y + lax.dot_general(hp_scr[g, d].astype(BF16), st_ref[g, d], NT, preferred_element_type=F32)
        yg_scr[g] = y

    def regroup_out(tok, stride, row):
        for jh in range(nl // per_v):
            yv = [yg_scr[g, pl.ds(row, 8), jh * 128:(jh + 1) * 128] for g in range(S5_GS)]
            for jj, v in enumerate(shuffle(yv)):
                o_ref[pl.ds(tok + jh * per_v + jj, 8, stride=stride), :] = v

    tiles(regroup_out)


def _s5_core(p2, r, st, tab, h0):
    blk = lambda *tail: pl.BlockSpec((S5_GS,) + tail, lambda s: (s,) + (0,) * len(tail))
    col = pl.BlockSpec((M_ALL, S5_GS * S5_GROUP), lambda s: (0, s))
    return pl.pallas_call(
        _s5_core_kernel,
        grid=(G_C // S5_GS,),
        in_specs=[col, blk(S5_CW, 2 * S5_CW), blk(2, S5_CW, S5_PW),
                  blk(2, 2, S5_TAB, P_C), blk(2, DEC_BATCH, S5_PW)],
        out_specs=[col, blk(2, BATCH, S5_PW)],
        out_shape=[jax.ShapeDtypeStruct((M_ALL, MIX), F32),
                   jax.ShapeDtypeStruct((G_C, 2, BATCH, S5_PW), F32)],
        scratch_shapes=[pltpu.VMEM((S5_GS, S5_ROWS, S5_CW), F32),
                        pltpu.VMEM((S5_GS, S5_ROWS, S5_CW), F32),
                        pltpu.VMEM((S5_GS, 2, S5_ROWS, S5_PW), F32),
                        pltpu.VMEM((S5_GS, 2, S5_ROWS, S5_PW), F32)],
        compiler_params=_params("parallel"),
        name="s5_core",
    )(p2, r, st, tab, h0)


def _s5_out_kernel(y_ref, u_ref, d_ref, w_ref, o_ref):
    z = _gelu_tanh(y_ref[...] + d_ref[...] * u_ref[...])
    o_ref[...] = z * _sigmoid(jnp.dot(z.astype(BF16), w_ref[...], preferred_element_type=F32))


def _s5_out(y_tok, p2, d_skip, w_glu_bf16):
    tm = 512
    return pl.pallas_call(
        _s5_out_kernel,
        grid=(M_ALL // tm,),
        in_specs=[pl.BlockSpec((tm, MIX), lambda i: (i, 0)),
                  pl.BlockSpec((tm, MIX), lambda i: (i, 0)),
                  pl.BlockSpec((1, MIX), lambda i: (0, 0)),
                  pl.BlockSpec((MIX, MIX), lambda i: (0, 0))],
        out_specs=pl.BlockSpec((tm, MIX), lambda i: (i, 0)),
        out_shape=jax.ShapeDtypeStruct((M_ALL, MIX), F32),
        compiler_params=_params("parallel"),
        name="s5_out",
    )(y_tok, p2, d_skip.reshape(1, MIX), w_glu_bf16)


def _s5_tables(lam_re, lam_im, log_dt, b_re, b_im, c_re, c_im):
    dt = jnp.exp(log_dt)[:, None, :, None]
    k = jnp.arange(S5_TAB, dtype=F32)[None, :, None, None]
    mag = jnp.exp(lam_re[:, None] * dt * k)
    pw_re = mag * jnp.cos(lam_im[:, None] * dt * k)
    pw_im = mag * jnp.sin(lam_im[:, None] * dt * k)
    ab_re, ab_im = pw_re[:, 1], pw_im[:, 1]
    den = lam_re * lam_re + lam_im * lam_im
    nr = ab_re - 1.0
    f_re = (nr * lam_re + ab_im * lam_im) / den
    f_im = (ab_im * lam_re - nr * lam_im) / den
    rows = lambda b, c, f: jnp.concatenate(
        [jnp.swapaxes(b, -1, -2), c, jnp.broadcast_to(f[:, :, None, :], (2, G_C, 8, P_C))], axis=2)
    prm = jnp.stack([rows(b_re, c_re, f_re), rows(b_im, c_im, f_im)], axis=1)
    tab = jnp.stack([pw_re, pw_im], axis=1)
    return jnp.transpose(prm, (2, 0, 1, 3, 4)), jnp.transpose(tab, (3, 0, 1, 2, 4))


def _s5_mixer(p2, tables, d_skip, w_glu_bf16, state_re, state_im):
    prm, tab = tables
    r, st = _s5_params(prm, tab)
    h0 = jnp.concatenate([state_re, state_im], axis=-1)
    y_tok, hfin = _s5_core(p2, r, st, tab, jnp.transpose(h0, (2, 1, 0, 3)))
    oc = _s5_out(y_tok, p2, d_skip, w_glu_bf16)
    hfin = jnp.transpose(hfin, (2, 1, 0, 3))
    return oc, hfin[..., :P_C], hfin[..., P_C:]


def _ret_kernel(*refs, t, latent):
    if latent:
        (q_ref, k_ref, v_ref, g_ref, dl_ref, lnw_ref, lnb_ref, r0_ref, cos_ref, sin_ref, _,
         o_ref, ob_ref) = refs
    else:
        q_ref, k_ref, v_ref, g_ref, dl_ref, lnw_ref, lnb_ref, o_ref, rfin_ref, ob_ref = refs
    n = t // CHUNK
    w = HD_D
    jf = lax.broadcasted_iota(jnp.int32, (CHUNK, CHUNK), 0).astype(F32)
    kf = lax.broadcasted_iota(jnp.int32, (CHUNK, CHUNK), 1).astype(F32)
    diff = jf - kf
    tabs = []
    for dr in range(2):
        for h in range(H_D):
            l = -_softplus(-dl_ref[h, dr:dr + 1, :])
            if dr == 0:
                dmat = jnp.where(diff >= 0, jnp.exp(l * jnp.maximum(diff, 0.0)), 0.0)
                xi = jnp.exp(l * (jf + 1.0))
                zeta = jnp.exp(l * (CHUNK - 1.0 - jf))
            else:
                dmat = jnp.where(diff < 0, jnp.exp(l * jnp.maximum(-diff, 0.0)), 0.0)
                xi = jnp.exp(l * (CHUNK - jf))
                zeta = jnp.exp(l * jf)
            tabs.append((dmat, xi, zeta, jnp.exp(l * CHUNK)))
    chains = [(dr, h) for dr in range(2) for h in range(H_D)]

    def body(ci, states):
        q, k, v, where = [], [], [], []
        for dr, h in chains:
            cj = ci if dr == 0 else n - 1 - ci
            rows = pl.ds(pl.multiple_of(cj * CHUNK, CHUNK), CHUNK)
            cols = slice(h * w, (h + 1) * w)
            qi = q_ref[rows, cols]
            ki = k_ref[rows, cols] * (HD_D ** -0.5)
            if latent:
                qi = _rope_pairs(qi, cos_ref[rows, :], sin_ref[rows, :], HD_D // 2)
                ki = _rope_pairs(ki, cos_ref[rows, :], sin_ref[rows, :], HD_D // 2)
            q.append(qi.astype(BF16))
            k.append(ki)
            v.append(v_ref[rows, cols].astype(BF16))
            where.append((o_ref if dr == 0 else ob_ref, rows, cols))
        nch = range(len(chains))
        inner = [(_bdot(q[i], k[i], NT) * tabs[i][0]).astype(BF16) for i in nch]
        cross = [_bdot(q[i], states[i]) * tabs[i][1] for i in nch]
        kz = [(k[i] * tabs[i][2]).astype(BF16) for i in nch]
        out = [jnp.dot(inner[i], v[i], preferred_element_type=F32) + cross[i] for i in nch]
        new_states = [states[i] * tabs[i][3] + lax.dot_general(kz[i], v[i], TN, preferred_element_type=F32)
                      for i in nch]
        for (dst, rows, cols), y in zip(where, out):
            dst[rows, cols] = y
        return tuple(new_states)

    if latent:
        lax.fori_loop(0, n, body, tuple(r0_ref[0, dr, h] for dr, h in chains))
    else:
        fin = lax.fori_loop(0, n, body, tuple(jnp.zeros((w, w), F32) for _ in chains))
        for i, (dr, h) in enumerate(chains):
            rfin_ref[0, dr, h] = fin[i]
    for h in range(H_D):
        cols = slice(h * w, (h + 1) * w)
        y = o_ref[:, cols] + ob_ref[:, cols]
        yc = y - jnp.mean(y, axis=-1, keepdims=True)
        yn = yc * lax.rsqrt(jnp.mean(yc * yc, axis=-1, keepdims=True) + RET_GN_EPS)
        o_ref[:, cols] = (yn * lnw_ref[:, cols] + lnb_ref[:, cols]) * _silu(g_ref[:, cols])


def _retention(p, decay_logit, ln_w, ln_b, latent, r0=None, rope=None, dst=None):
    if latent:
        nseq, t, rb0 = DEC_BATCH, DEC_SEQ, M_CTX // DEC_SEQ
    else:
        nseq, t, rb0 = BATCH, SEQ, 0
    w = HD_D
    dl = jnp.broadcast_to(decay_logit.T[:, :, None], (H_D, 2, w))
    full = lambda shape: pl.BlockSpec(shape, lambda b: (0,) * len(shape))
    st_spec = pl.BlockSpec((1, 2, H_D, w, w), lambda b: (b, 0, 0, 0, 0))
    in_specs = [pl.BlockSpec((t, MIX), lambda b, j=j: (rb0 + b, j)) for j in (1, 2, 3, 4)]
    in_specs += [full((H_D, 2, w)), full((1, MIX)), full((1, MIX))]
    args = [p, p, p, p, dl, ln_w.reshape(1, MIX), ln_b.reshape(1, MIX)]
    o_spec = pl.BlockSpec((t, MIX), lambda b: (rb0 + b, 0))
    o_shape = jax.ShapeDtypeStruct((M_ALL, MIX), F32)
    if latent:
        in_specs += [st_spec, full((t, w)), full((t, w)), pl.BlockSpec(memory_space=pl.ANY)]
        args += [r0, *rope, dst]
        out_specs, out_shape, aliases = o_spec, o_shape, {len(args) - 1: 0}
    else:
        out_specs = [o_spec, st_spec]
        out_shape = [o_shape, jax.ShapeDtypeStruct((nseq, 2, H_D, w, w), F32)]
        aliases = {}
    return pl.pallas_call(
        functools.partial(_ret_kernel, t=t, latent=latent),
        grid=(nseq,),
        in_specs=in_specs, out_specs=out_specs, out_shape=out_shape,
        scratch_shapes=[pltpu.VMEM((t, MIX), F32)],
        input_output_aliases=aliases,
        compiler_params=_params("parallel"),
        name="retention_lat" if latent else "retention_ctx",
    )(*args)


def _rope_tables(n_tok, dim, reps):
    rows = n_tok // GRID_W
    n_freq = dim // 4
    inv = 1.0 / (ROPE_THETA ** (jnp.arange(n_freq, dtype=F32) / n_freq))
    row = jnp.repeat(jnp.arange(rows, dtype=F32), GRID_W)
    col = jnp.tile(jnp.arange(GRID_W, dtype=F32), rows)
    ang = jnp.concatenate([row[:, None] * inv, col[:, None] * inv], axis=-1)
    cos, sin = jnp.cos(ang), jnp.sin(ang)
    return jnp.tile(jnp.concatenate([cos, cos], axis=1), (1, reps)), \
        jnp.tile(jnp.concatenate([-sin, sin], axis=1), (1, reps))


def kernel(x_prompt, x_sample, cache_k_ab, cache_v_ab, state_rwkv, state_s5_re, state_s5_im, state_ret, c, c_ctx, norm1_g, norm2_g, w_mod, b_mod, w_ff_gate, w_ff_up, w_ff_down, w_in_ab, w_out_ab, qk_gain_a, lambda_qk, subln_g, rwkv_mu, rwkv_k_k, rwkv_k_a, rwkv_r_k, rwkv_w0, rwkv_w_up, rwkv_a0, rwkv_a_up, rwkv_g_up, rwkv_ln_w, rwkv_ln_b, w_in_cd, w_out_cd, s5_lam_re, s5_lam_im, s5_log_dt, s5_b_re, s5_b_im, s5_c_re, s5_c_im, s5_d, s5_w_glu, ret_decay_logit, ret_ln_w, ret_ln_b):
    d = D_MODEL
    xs = [x_prompt.reshape(M_CTX, d), x_sample.reshape(M_LAT, d)]
    cvec = jnp.zeros((MOD_ROWS, d), F32).at[0].set(c_ctx).at[1:1 + DEC_BATCH].set(c)
    mods = _modulation(cvec, w_mod, b_mod)

    lam_init = 0.8 - 0.6 * math.exp(-0.3 * 0)
    p = _norm_linear(xs, norm1_g[0], mods[0], w_in_ab[0].astype(BF16))
    gain2 = jnp.tile(qk_gain_a[0], (1, 2))
    rope_a = _rope_tables(DEC_SEQ, HD_A, 2)
    ck = cache_k_ab[:, 0].reshape(DEC_BATCH * PAST_LEN, MIX)
    cv = cache_v_ab[:, 0].reshape(DEC_BATCH * PAST_LEN, MIX)
    oa, k_ctx, v_ctx = _attention(p, gain2, lambda_qk[0], subln_g[0], lam_init, latent=False)
    oa = _attention(p, gain2, lambda_qk[0], subln_g[0], lam_init, latent=True,
                    cache_k=ck, cache_v=cv, rope=rope_a, dst=oa)
    rw_prm = (rwkv_mu[0], rwkv_k_k[0], rwkv_k_a[0], rwkv_r_k[0], rwkv_w0[0], rwkv_w_up[0],
              rwkv_a0[0], rwkv_a_up[0], rwkv_g_up[0])
    ob, sfin_ctx = _rwkv(p, rw_prm, rwkv_ln_w[0], rwkv_ln_b[0], latent=False)
    ob = _rwkv(p, rw_prm, rwkv_ln_w[0], rwkv_ln_b[0], latent=True, s0=state_rwkv[:, 0], dst=ob)
    x = _mix_ffn(xs, oa, ob, w_out_ab[0].astype(BF16), norm2_g[0], mods[0], w_ff_gate[0].astype(BF16),
                 w_ff_up[0].astype(BF16), w_ff_down[0].astype(BF16), split_out=False)

    p2 = _norm_linear([x], norm1_g[1], mods[1], w_in_cd[0].astype(BF16))
    tables = _s5_tables(s5_lam_re[0], s5_lam_im[0], s5_log_dt[0], s5_b_re[0], s5_b_im[0],
                        s5_c_re[0], s5_c_im[0])
    oc, s5_fin_re, s5_fin_im = _s5_mixer(p2, tables, s5_d[0], s5_w_glu[0].astype(BF16),
                                         state_s5_re[:, 0], state_s5_im[:, 0])
    rope_d = _rope_tables(DEC_SEQ, HD_D, 1)
    od, rfin = _retention(p2, ret_decay_logit[0], ret_ln_w[0], ret_ln_b[0], latent=False)
    od = _retention(p2, ret_decay_logit[0], ret_ln_w[0], ret_ln_b[0], latent=True, r0=state_ret[:, 0],
                    rope=rope_d, dst=od)
    y_ctx, y_lat = _mix_ffn([x], oc, od, w_out_cd[0].astype(BF16), norm2_g[1], mods[1],
                            w_ff_gate[1].astype(BF16), w_ff_up[1].astype(BF16), w_ff_down[1].astype(BF16),
                            split_out=True)
    y_prompt = y_ctx.reshape(BATCH, SEQ, d)
    y_sample = y_lat.reshape(DEC_BATCH, DEC_SEQ, d)
    new_k = jnp.transpose(k_ctx, (0, 4, 1, 2, 3))[:, None]
    new_v = v_ctx.reshape(BATCH, 1, SEQ, H_A, VD_A)
    new_rwkv = sfin_ctx[:, None]
    new_s5_re = s5_fin_re[:, None]
    new_s5_im = s5_fin_im[:, None]
    new_ret = rfin[:, None]
    return (y_prompt, y_sample, new_k, new_v, new_rwkv, new_s5_re, new_s5_im, new_ret)
```

```python
import functools
import math

import numpy as np
import jax
import jax.numpy as jnp
from jax import lax
from jax.experimental import pallas as pl
from jax.experimental.pallas import tpu as pltpu

F32 = jnp.float32
BF16 = jnp.bfloat16
HIGHEST = lax.Precision.HIGHEST

D_MODEL = 1024
BATCH = 32
SEQ = 256
DEC_BATCH = 2
DEC_SEQ = 1024
PAST_LEN = 256
GRID_W = 64
H_A = 4
HD_A = 64
VD_A = 128
H_B = 8
HD_B = 64
MIX = 512
LORA_W = 64
LORA_A = 64
LORA_G = 128
S5_GROUP = 16
G_C = 32
P_C = 64
S5_STATE = G_C * P_C
H_D = 4
HD_D = 128
CHUNK = 128
D_FF = 2816
IN_AB = 3328
IN_CD = 2560
ROPE_THETA = 10000.0
NORM_EPS = 1e-6
RWKV_GN_EPS = 64e-5
RET_GN_EPS = 1e-5

M_CTX = BATCH * SEQ
M_LAT = DEC_BATCH * DEC_SEQ
M_ALL = M_CTX + M_LAT
MOD_ROWS = 8
RWKV_CHUNK = 64
VMEM_LIMIT = 56 * 1024 * 1024

NN = (((1,), (0,)), ((), ()))
NT = (((1,), (1,)), ((), ()))
TN = (((0,), (0,)), ((), ()))


def _params(*sem):
    return pltpu.CompilerParams(dimension_semantics=sem, vmem_limit_bytes=VMEM_LIMIT)


def _bdot(a, b, dims=NN):
    return lax.dot_general(a.astype(BF16), b.astype(BF16), dims, preferred_element_type=F32)


def _hdot(a, b, dims=NN):
    return lax.dot_general(a, b, dims, precision=HIGHEST, preferred_element_type=F32)


def _split_dot(x, m):
    hi = x.astype(BF16)
    lo = (x - hi.astype(F32)).astype(BF16)
    return (jnp.dot(hi, m, preferred_element_type=F32) + jnp.dot(lo, m, preferred_element_type=F32))


def _seg_matrix(n, shift, val):
    r = lax.broadcasted_iota(jnp.int32, (n, n), 0) >> shift
    c = lax.broadcasted_iota(jnp.int32, (n, n), 1) >> shift
    return jnp.where(r == c, val, 0.0).astype(BF16)


def _sigmoid(x):
    return jax.nn.sigmoid(x)


def _silu(x):
    return x * jax.nn.sigmoid(x)


def _softplus(x):
    return jnp.maximum(x, 0.0) + jnp.log(1.0 + jnp.exp(-jnp.abs(x)))


def _mod_row(tile, tm):
    r0 = tile * tm
    return jnp.where(r0 < M_CTX, 0, 1 + (r0 - M_CTX) // DEC_SEQ)


def _norm_mod(x, g, sc_ref, sh_ref, row):
    y = x * lax.rsqrt(jnp.mean(x * x, axis=-1, keepdims=True) + NORM_EPS) * g
    return y * (1.0 + sc_ref[pl.ds(row, 1), :]) + sh_ref[pl.ds(row, 1), :]


def _mod_kernel(c_ref, w_ref, b_ref, o_ref):
    o_ref[0] = _bdot(_silu(c_ref[...]), w_ref[0]) + b_ref[0]


def _modulation(cvec, w_mod, b_mod):
    depth, d, n6 = w_mod.shape
    tn = 1536
    return pl.pallas_call(
        _mod_kernel,
        grid=(depth, n6 // tn),
        in_specs=[pl.BlockSpec((MOD_ROWS, d), lambda l, j: (0, 0)),
                  pl.BlockSpec((1, d, tn), lambda l, j: (l, 0, j)),
                  pl.BlockSpec((1, 1, tn), lambda l, j: (l, 0, j))],
        out_specs=pl.BlockSpec((1, MOD_ROWS, tn), lambda l, j: (l, 0, j)),
        out_shape=jax.ShapeDtypeStruct((depth, MOD_ROWS, n6), F32),
        compiler_params=_params("parallel", "parallel"),
        name="modulation",
    )(cvec, w_mod, b_mod.reshape(depth, 1, n6))


def _tile_specs(tm, d):
    nc = M_CTX // tm
    return [pl.BlockSpec((tm, d), lambda i: (jnp.minimum(i, nc - 1), 0)),
            pl.BlockSpec((tm, d), lambda i: (jnp.maximum(i - nc, 0), 0))]


def _pick_rows(refs, tm):
    if len(refs) == 1:
        return refs[0][...]
    return jnp.where(pl.program_id(0) < M_CTX // tm, refs[0][...], refs[1][...])


def _norm_linear_kernel(*refs, tm):
    *x_refs, g_ref, sc_ref, sh_ref, w_ref, o_ref = refs
    row = _mod_row(pl.program_id(0), tm)
    x = _pick_rows(x_refs, tm)
    nsub = 2
    rows = [slice(i * tm // nsub, (i + 1) * tm // nsub) for i in range(nsub)]
    hs = [_norm_mod(x[r], g_ref[...], sc_ref, sh_ref, row).astype(BF16) for r in rows]
    for r, h in zip(rows, hs):
        o_ref[r, :] = jnp.dot(h, w_ref[...], preferred_element_type=F32)


def _norm_linear(xs, g, mods, w_bf16):
    tm = 512
    d = D_MODEL
    n = w_bf16.shape[1]
    x_specs = _tile_specs(tm, d) if len(xs) == 2 else [pl.BlockSpec((tm, d), lambda i: (i, 0))]
    return pl.pallas_call(
        functools.partial(_norm_linear_kernel, tm=tm),
        grid=(M_ALL // tm,),
        in_specs=x_specs + [pl.BlockSpec((1, d), lambda i: (0, 0)),
                            pl.BlockSpec((MOD_ROWS, d), lambda i: (0, 1)),
                            pl.BlockSpec((MOD_ROWS, d), lambda i: (0, 0)),
                            pl.BlockSpec((d, n), lambda i: (0, 0))],
        out_specs=pl.BlockSpec((tm, n), lambda i: (i, 0)),
        out_shape=jax.ShapeDtypeStruct((M_ALL, n), F32),
        compiler_params=_params("arbitrary"),
        name="norm_linear",
    )(*xs, g.reshape(1, d), mods, mods, w_bf16)


def _mix_ffn_kernel(*refs, tm, ck, n_in, n_out):
    x_refs = refs[:n_in]
    (a_ref, b_ref, wa_ref, wb_ref, g1_ref, ng_ref, sc_ref, sh_ref, g2_ref,
     wg_ref, wu_ref, wd_ref) = refs[n_in:n_in + 12]
    o_refs = refs[n_in + 12:]
    row = _mod_row(pl.program_id(0), tm)
    mix = (jnp.dot(a_ref[...].astype(BF16), wa_ref[...], preferred_element_type=F32)
           + jnp.dot(b_ref[...].astype(BF16), wb_ref[...], preferred_element_type=F32))
    x = _pick_rows(x_refs, tm) + g1_ref[pl.ds(row, 1), :] * mix
    h = _norm_mod(x, ng_ref[...], sc_ref, sh_ref, row).astype(BF16)
    acc = jnp.zeros((tm, D_MODEL), F32)
    for c in range(D_FF // ck):
        gg = jnp.dot(h, wg_ref[:, c * ck:(c + 1) * ck], preferred_element_type=F32)
        uu = jnp.dot(h, wu_ref[:, c * ck:(c + 1) * ck], preferred_element_type=F32)
        act = (_silu(gg) * uu).astype(BF16)
        acc = acc + jnp.dot(act, wd_ref[c * ck:(c + 1) * ck, :], preferred_element_type=F32)
    y = x + g2_ref[pl.ds(row, 1), :] * acc
    if n_out == 1:
        o_refs[0][...] = y
    else:
        is_ctx = pl.program_id(0) < M_CTX // tm

        @pl.when(is_ctx)
        def _():
            o_refs[0][...] = y

        @pl.when(jnp.logical_not(is_ctx))
        def _():
            o_refs[1][...] = y


def _mix_ffn(xs, oa, ob, w_out_bf16, norm_g, mods, wg, wu, wd, split_out):
    tm, ck = 512, 256
    d = D_MODEL
    row_spec = lambda n: pl.BlockSpec((tm, n), lambda i: (i, 0))
    const = lambda shape, idx: pl.BlockSpec(shape, lambda i: idx, pipeline_mode=pl.Buffered(1))
    mod = lambda j: pl.BlockSpec((MOD_ROWS, d), lambda i: (0, j))
    x_specs = _tile_specs(tm, d) if len(xs) == 2 else [row_spec(d)]
    if split_out:
        out_specs = _tile_specs(tm, d)
        out_shape = [jax.ShapeDtypeStruct((M_CTX, d), F32), jax.ShapeDtypeStruct((M_LAT, d), F32)]
    else:
        out_specs, out_shape = row_spec(d), jax.ShapeDtypeStruct((M_ALL, d), F32)
    return pl.pallas_call(
        functools.partial(_mix_ffn_kernel, tm=tm, ck=ck, n_in=len(xs), n_out=2 if split_out else 1),
        grid=(M_ALL // tm,),
        in_specs=x_specs + [row_spec(MIX), row_spec(MIX),
                            const((MIX, d), (0, 0)), const((MIX, d), (1, 0)),
                            mod(2),
                            pl.BlockSpec((1, d), lambda i: (0, 0)),
                            mod(4), mod(3), mod(5),
                            const((d, D_FF), (0, 0)), const((d, D_FF), (0, 0)), const((D_FF, d), (0, 0))],
        out_specs=out_specs, out_shape=out_shape,
        compiler_params=_params("arbitrary"),
        name="mix_ffn",
    )(*xs, oa, ob, w_out_bf16, w_out_bf16, mods, norm_g.reshape(1, d), mods, mods, mods, wg, wu, wd)


def _qk_norm(x, gain, segm):
    ms = _split_dot(x * x, segm)
    return x * lax.rsqrt(ms + NORM_EPS) * gain


def _rope_pairs(x, cosf, sinf, half):
    lane = lax.broadcasted_iota(jnp.int32, x.shape, 1)
    first = (lane & (2 * half - 1)) < half
    n = x.shape[1]
    partner = jnp.where(first, pltpu.roll(x, n - half, axis=1), pltpu.roll(x, half, axis=1))
    return x * cosf + partner * sinf


def _attn_kernel(*refs, latent, lam_init):
    if latent:
        (q_ref, k_ref, v_ref, ck_ref, cv_ref, cosq_ref, sinq_ref, cosk_ref, sink_ref,
         gain_ref, lam_ref, sub_ref, _, o_ref, kall, vall) = refs
    else:
        q_ref, k_ref, v_ref, gain_ref, lam_ref, sub_ref, o_ref, kn_ref, vo_ref, kall, vall = refs
    w = 2 * HD_A
    segm = _seg_matrix(w, 6, 1.0 / HD_A)
    gains = gain_ref[...]
    cols = [slice(h * w, (h + 1) * w) for h in range(H_A)]

    @pl.when(pl.program_id(1) == 0)
    def _():
        k = [_qk_norm(k_ref[:, c], gains[1:2], segm) for c in cols]
        if latent:
            k = [_rope_pairs(x, cosk_ref[...], sink_ref[...], HD_A // 2) for x in k]
            for c, x in zip(cols, k):
                kall[0:PAST_LEN, c] = ck_ref[:, c].astype(BF16)
                kall[PAST_LEN:, c] = x.astype(BF16)
                vall[0:PAST_LEN, c] = cv_ref[:, c].astype(BF16)
                vall[PAST_LEN:, c] = v_ref[:, c].astype(BF16)
        else:
            for h, (c, x) in enumerate(zip(cols, k)):
                xt = x.T
                kn_ref[0, h, 0] = xt[:HD_A]
                kn_ref[0, h, 1] = xt[HD_A:]
                vo_ref[pl.ds(h, v_ref.shape[0], stride=H_A), :] = v_ref[:, c]
                kall[:, c] = x.astype(BF16)
                vall[:, c] = v_ref[:, c].astype(BF16)

    q = [_qk_norm(q_ref[:, c], gains[0:1], segm) for c in cols]
    if latent:
        q = [_rope_pairs(x, cosq_ref[...], sinq_ref[...], HD_A // 2) for x in q]
    lv = lam_ref[...]
    lam = (jnp.exp(jnp.sum(lv[0:1] * lv[1:2], axis=1, keepdims=True))
           - jnp.exp(jnp.sum(lv[2:3] * lv[3:4], axis=1, keepdims=True)) + lam_init)
    scale = HD_A ** -0.5
    comp0 = lax.broadcasted_iota(jnp.int32, q[0].shape, 1) < HD_A
    qc = [jnp.where(comp0, *sel).astype(BF16) for x in q for sel in ((x, 0.0), (0.0, x))]
    s = [lax.dot_general(qc[i], kall[:, cols[i // 2]], NT, preferred_element_type=F32) * scale
         for i in range(2 * H_A)]
    e = [jnp.exp(x - jnp.max(x, axis=-1, keepdims=True)) for x in s]
    p = [x / jnp.sum(x, axis=-1, keepdims=True) for x in e]
    att = [(p[2 * h] - lam * p[2 * h + 1]).astype(BF16) for h in range(H_A)]
    o = [jnp.dot(att[h], vall[:, cols[h]], preferred_element_type=F32) for h in range(H_A)]
    o = [x * lax.rsqrt(jnp.mean(x * x, axis=-1, keepdims=True) + NORM_EPS) * sub_ref[...] for x in o]
    for c, x in zip(cols, o):
        o_ref[:, c] = x * (1.0 - lam_init)


def _attention(p, gain2, lambda_qk, subln_g, lam_init, latent, cache_k=None, cache_v=None, rope=None,
               dst=None):
    w = 2 * HD_A
    if latent:
        nseq, t, tq, rb0, s_len = DEC_BATCH, DEC_SEQ, 128, M_CTX // DEC_SEQ, PAST_LEN + DEC_SEQ
    else:
        nseq, t, tq, rb0, s_len = BATCH, SEQ, SEQ, 0, SEQ
    nq = t // tq
    qoff = rb0 * nq
    full = lambda shape: pl.BlockSpec(shape, lambda b, i: (0,) * len(shape))
    in_specs = [pl.BlockSpec((tq, MIX), lambda b, i: (qoff + b * nq + i, 0)),
                pl.BlockSpec((t, MIX), lambda b, i: (rb0 + b, 1)),
                pl.BlockSpec((t, MIX), lambda b, i: (rb0 + b, 2))]
    args = [p, p, p]
    if latent:
        cosf, sinf = rope
        in_specs += [pl.BlockSpec((PAST_LEN, MIX), lambda b, i: (b, 0)),
                     pl.BlockSpec((PAST_LEN, MIX), lambda b, i: (b, 0)),
                     pl.BlockSpec((tq, w), lambda b, i: (i, 0)),
                     pl.BlockSpec((tq, w), lambda b, i: (i, 0)),
                     full((t, w)), full((t, w))]
        args += [cache_k, cache_v, cosf, sinf, cosf, sinf]
    in_specs += [full((2, w)), full((4, HD_A)), full((1, w))]
    args += [gain2, lambda_qk, subln_g.reshape(1, w)]
    o_spec = pl.BlockSpec((tq, MIX), lambda b, i: (qoff + b * nq + i, 0))
    o_shape = jax.ShapeDtypeStruct((M_ALL, MIX), F32)
    aliases = {}
    if latent:
        out_specs, out_shape = o_spec, o_shape
        in_specs.append(pl.BlockSpec(memory_space=pl.ANY))
        args.append(dst)
        aliases = {len(args) - 1: 0}
    else:
        out_specs = [o_spec, pl.BlockSpec((1, H_A, 2, HD_A, t), lambda b, i: (b, 0, 0, 0, 0)),
                     pl.BlockSpec((t * H_A, VD_A), lambda b, i: (b, 0))]
        out_shape = [o_shape, jax.ShapeDtypeStruct((nseq, H_A, 2, HD_A, t), F32),
                     jax.ShapeDtypeStruct((nseq * t * H_A, VD_A), F32)]
    return pl.pallas_call(
        functools.partial(_attn_kernel, latent=latent, lam_init=lam_init),
        grid=(nseq, nq),
        in_specs=in_specs, out_specs=out_specs, out_shape=out_shape,
        scratch_shapes=[pltpu.VMEM((s_len, MIX), BF16), pltpu.VMEM((s_len, MIX), BF16)],
        input_output_aliases=aliases,
        compiler_params=_params("parallel", "arbitrary"),
        name="diff_attention_lat" if latent else "diff_attention_ctx",
    )(*args)


def _centred_shift(x, mu):
    t = x.shape[0]
    row = lax.broadcasted_iota(jnp.int32, x.shape, 0)
    prev = jnp.where(row == 0, 0.0, pltpu.roll(x, 1, axis=0))
    nxt = jnp.where(row == t - 1, 0.0, pltpu.roll(x, t - 1, axis=0))
    return x + (0.5 * (prev + nxt) - x) * mu


def _seg_sum(x, segm):
    return jnp.concatenate([_split_dot(x[:, j * 128:(j + 1) * 128], segm) for j in range(x.shape[1] // 128)],
                           axis=1)


def _rwkv_prep_kernel(r_ref, k_ref, v_ref, l_ref, mur_ref, muk_ref, muv_ref, mul_ref,
                      kk_ref, ka_ref, rk_ref, w0_ref, wup_ref, a0_ref, aup_ref, gup_ref,
                      ro_ref, ldf_ref, ldb_ref, kbo_ref, vbo_ref, kko_ref, ao_ref, gate_ref, bonus_ref):
    seg1 = _seg_matrix(128, 6, 1.0)
    r = _centred_shift(r_ref[...], mur_ref[...])
    kb = _centred_shift(k_ref[...], muk_ref[...])
    vb = _centred_shift(v_ref[...], muv_ref[...])
    lo = _centred_shift(l_ref[...], mul_ref[...])
    xw = lo[:, 0:LORA_W]
    xa = lo[:, LORA_W:LORA_W + LORA_A]
    xg = lo[:, LORA_W + LORA_A:]
    kk = kb * kk_ref[...]
    kk = kk * lax.rsqrt(_seg_sum(kk * kk, seg1) + 1e-12)
    a = _sigmoid(a0_ref[...] + _bdot(xa, aup_ref[...]))
    kb2 = kb * (1.0 + (a - 1.0) * ka_ref[...])
    lw = jnp.tanh(xw)
    for dr, ld_ref in enumerate((ldf_ref, ldb_ref)):
        z = w0_ref[dr:dr + 1, :] + _bdot(lw, wup_ref[dr])
        logw = -_softplus(-z) - 0.5
        ld_ref[...] = -jnp.exp(logw)
    gate_ref[...] = _bdot(_sigmoid(xg), gup_ref[...])
    bonus_ref[...] = _seg_sum(r * kb2 * rk_ref[...], seg1) * vb
    ro_ref[...] = r
    kbo_ref[...] = kb2
    vbo_ref[...] = vb
    kko_ref[...] = kk
    ao_ref[...] = a


N_PREP_IN = 16
N_PREP_OUT = 9


def _split2(x):
    hi = x.astype(BF16)
    return hi, (x - hi.astype(F32)).astype(BF16)


RWKV_INV_BLOCK = 16
RWKV_UNROLL_MAX = 4


def _rwkv_masks(c, rev):
    ti = lax.broadcasted_iota(jnp.int32, (c, c), 0)
    si = lax.broadcasted_iota(jnp.int32, (c, c), 1)
    tri = jnp.where((si >= ti) if rev else (si <= ti), 1.0, 0.0).astype(BF16)
    t4 = lax.broadcasted_iota(jnp.int32, (4 * c, 4 * c), 0)
    s4 = lax.broadcasted_iota(jnp.int32, (4 * c, 4 * c), 1)
    tm, sm = t4 & (c - 1), s4 & (c - 1)
    strict = (sm > tm) if rev else (sm < tm)
    incl = (sm >= tm) if rev else (sm <= tm)
    same_head = ((t4 // c) & 1) == ((s4 // c) & 1)
    top = t4 < 2 * c
    gmask = same_head & ((top & strict) | (~top & incl))
    t2 = lax.broadcasted_iota(jnp.int32, (2 * c, 2 * c), 0)
    s2 = lax.broadcasted_iota(jnp.int32, (2 * c, 2 * c), 1)
    same = lambda n: (t2 // n) == (s2 // n)
    levels = []
    n = RWKV_INV_BLOCK
    while n < c:
        levels.append(same(2 * n) & ~same(n))
        n *= 2
    f = lambda m: jnp.where(m, 1.0, 0.0)
    return tri, f(gmask), f(same(RWKV_INV_BLOCK)), tuple(f(m) for m in levels), f(same(c)), f(t2 == s2)


def _keep(mask01, x):
    return jnp.where(mask01 > 0.5, x, 0.0)


def _tri_inverse(a, diag_blk, levels, eye):
    n = a[0].shape[0]
    d = [_keep(diag_blk, x) for x in a]
    t = [eye + x for x in d]
    p = [_bdot(x, x) for x in d]
    for _ in range(int(math.log2(RWKV_INV_BLOCK)) - 2):
        res = [_bdot(jnp.concatenate([pi, ti], axis=0), pi) for pi, ti in zip(p, t)]
        p = [x[:n] for x in res]
        t = [ti + x[n:] for ti, x in zip(t, res)]
    t = [ti + _bdot(ti, pi) for ti, pi in zip(t, p)]
    for off in levels:
        half = [_bdot(ti, _keep(off, x)) for ti, x in zip(t, a)]
        t = [ti + _bdot(x, ti) for ti, x in zip(t, half)]
    return t


def _rwkv_pair_chunks(ins, sts, tris, gmasks, diag_blk, levels, same_head, eye):
    c, w = ins[0][0].shape
    hd = w // 2
    nch = range(len(ins))
    r, ld, kb, vb, kk, a = (list(z) for z in zip(*ins))
    split = [_split2(x) for x in ld]
    lcum = [jnp.dot(tris[i], jnp.concatenate(split[i], axis=1), preferred_element_type=F32) for i in nch]
    lcum = [x[:, :w] + x[:, w:] for x in lcum]
    ltot = [jnp.sum(x, axis=0, keepdims=True) for x in ld]
    beta = [kk[i] * a[i] for i in nch]
    eneg = [jnp.exp(-x) for x in lcum]
    abar = [-kk[i] * jnp.exp(lcum[i] - ld[i]) for i in nch]
    rbar = [r[i] * jnp.exp(lcum[i]) for i in nch]
    bt = [(beta[i] * eneg[i]).astype(BF16) for i in nch]
    kt = [(kb[i] * eneg[i]).astype(BF16) for i in nch]
    vbb = [x.astype(BF16) for x in vb]
    head0 = lax.broadcasted_iota(jnp.int32, (c, w), 1) < hd
    pick = lambda res: jnp.where(head0, res[:c], res[c:])
    arst = [_bdot(jnp.concatenate([abar[i], rbar[i]], axis=0), sts[i], NT) for i in nch]
    lhs = [jnp.concatenate([jnp.where(head0, abar[i], 0.0), jnp.where(head0, 0.0, abar[i]),
                            jnp.where(head0, rbar[i], 0.0), jnp.where(head0, 0.0, rbar[i])], axis=0) for i in nch]
    g = [_keep(gmasks[i], _bdot(lhs[i], jnp.concatenate([bt[i], bt[i], kt[i], kt[i]], axis=0), NT))
         for i in nch]
    x = [arst[i][:c] + pick(_bdot(g[i][:2 * c, 2 * c:], jnp.concatenate([vbb[i], vbb[i]], axis=0))) for i in nch]
    tinv = _tri_inverse([gi[:2 * c, :2 * c] for gi in g], diag_blk, levels, eye)
    u = [pick(_bdot(tinv[i], jnp.concatenate([x[i], x[i]], axis=0))) for i in nch]
    ub = [z.astype(BF16) for z in u]
    y = [arst[i][c:] + pick(_bdot(g[i][2 * c:], jnp.concatenate([ub[i], ub[i], vbb[i], vbb[i]], axis=0)))
         for i in nch]
    erem = [jnp.exp(ltot[i] - lcum[i]) for i in nch]
    bkh = [jnp.concatenate([beta[i] * erem[i], kb[i] * erem[i]], axis=0) for i in nch]
    st_new = [jnp.exp(ltot[i]) * sts[i]
              + _keep(same_head, _bdot(jnp.concatenate([ub[i], vbb[i]], axis=0), bkh[i], TN)) for i in nch]
    return y, st_new


def _rwkv_scan_kernel(*refs, t, c, npair, latent):
    prep_in, (lnw_ref, lnb_ref), rest = refs[:N_PREP_IN], refs[N_PREP_IN:N_PREP_IN + 2], refs[N_PREP_IN + 2:]
    if latent:
        s0_ref, _, o_ref, *scratch = rest
    else:
        o_ref, sfin_ref, *scratch = rest
    *prep_out, yb_ref = scratch
    _rwkv_prep_kernel(*prep_in, *prep_out)
    r_ref, ldf_ref, ldb_ref, kb_ref, vb_ref, kk_ref, a_ref, gate_ref, bonus_ref = prep_out
    n = t // c
    w = 2 * HD_B
    masks = [_rwkv_masks(c, rev) for rev in (False, True)]

    def body(ci, states):
        ins, tris, gmasks, dsts = [], [], [], []
        for dr, (ld_ref, dst) in enumerate(((ldf_ref, o_ref), (ldb_ref, yb_ref))):
            cj = ci if dr == 0 else n - 1 - ci
            rows = pl.ds(pl.multiple_of(cj * c, c), c)
            for p in range(npair):
                cols = slice(p * w, (p + 1) * w)
                ins.append(tuple(ref[rows, cols] for ref in (r_ref, ld_ref, kb_ref, vb_ref, kk_ref, a_ref)))
                tris.append(masks[dr][0])
                gmasks.append(masks[dr][1])
                dsts.append((dst, rows, cols))
        ys, new_states = _rwkv_pair_chunks(ins, list(states), tris, gmasks, *masks[0][2:])
        for (dst, rows, cols), y in zip(dsts, ys):
            dst[rows, cols] = y
        return tuple(new_states)

    if latent:
        zero = jnp.zeros((HD_B, HD_B), F32)
        init = tuple(jnp.concatenate([jnp.concatenate([s0_ref[0, dr, 2 * p], zero], axis=1),
                                      jnp.concatenate([zero, s0_ref[0, dr, 2 * p + 1]], axis=1)], axis=0)
                     for dr in range(2) for p in range(npair))
    else:
        init = tuple(jnp.zeros((w, w), F32) for _ in range(2 * npair))
    fin = lax.fori_loop(0, n, body, init, unroll=(n <= RWKV_UNROLL_MAX))
    segm = _seg_matrix(w, 6, 1.0 / HD_B)
    for p in range(npair):
        if not latent:
            for dr in range(2):
                st = fin[dr * npair + p]
                sfin_ref[0, dr, 2 * p] = st[:HD_B, :HD_B]
                sfin_ref[0, dr, 2 * p + 1] = st[HD_B:, HD_B:]
        cols = slice(p * w, (p + 1) * w)
        y = o_ref[:, cols] + yb_ref[:, cols]
        yc = y - _split_dot(y, segm)
        yn = yc * lax.rsqrt(_split_dot(yc * yc, segm) + RWKV_GN_EPS)
        o_ref[:, cols] = (yn * lnw_ref[:, cols] + lnb_ref[:, cols] + bonus_ref[:, cols]) * gate_ref[:, cols]


def _rwkv(p, prm, ln_w, ln_b, latent, s0=None, dst=None):
    mu, k_k, k_a, r_k, w0, w_up, a0, a_up, g_up = prm
    nseq, t, rb0 = (DEC_BATCH, DEC_SEQ, M_CTX // DEC_SEQ) if latent else (BATCH, SEQ, 0)
    npair = H_B // 2
    lw = LORA_W + LORA_A + LORA_G
    c0 = (IN_AB - 3 * MIX - lw) // MIX
    cl = (IN_AB - lw) // lw
    full = lambda *shape: pl.BlockSpec(shape, lambda b: (0,) * len(shape))
    st_spec = pl.BlockSpec((1, 2, H_B, HD_B, HD_B), lambda b: (b, 0, 0, 0, 0))
    o_spec = pl.BlockSpec((t, MIX), lambda b: (rb0 + b, 0))
    o_shape = jax.ShapeDtypeStruct((M_ALL, MIX), F32)
    in_specs = [pl.BlockSpec((t, MIX), lambda b: (rb0 + b, c0)),
                pl.BlockSpec((t, MIX), lambda b: (rb0 + b, c0 + 1)),
                pl.BlockSpec((t, MIX), lambda b: (rb0 + b, c0 + 2)),
                pl.BlockSpec((t, lw), lambda b: (rb0 + b, cl)),
                full(1, MIX), full(1, MIX), full(1, MIX), full(1, lw),
                full(1, MIX), full(1, MIX), full(1, MIX),
                full(2, MIX), full(2, LORA_W, MIX), full(1, MIX), full(LORA_A, MIX), full(LORA_G, MIX),
                full(1, MIX), full(1, MIX)]
    args = [p, p, p, p,
            mu[None, 0:MIX], mu[None, MIX:2 * MIX], mu[None, 2 * MIX:3 * MIX], mu[None, 3 * MIX:],
            k_k.reshape(1, MIX), k_a.reshape(1, MIX), r_k.reshape(1, MIX), w0, w_up,
            a0.reshape(1, MIX), a_up, g_up, ln_w.reshape(1, MIX), ln_b.reshape(1, MIX)]
    if latent:
        args += [s0, dst]
        in_specs += [st_spec, pl.BlockSpec(memory_space=pl.ANY)]
        out_specs, out_shape, aliases = o_spec, o_shape, {len(args) - 1: 0}
    else:
        out_specs = [o_spec, st_spec]
        out_shape = [o_shape, jax.ShapeDtypeStruct((nseq, 2, H_B, HD_B, HD_B), F32)]
        aliases = {}
    return pl.pallas_call(
        functools.partial(_rwkv_scan_kernel, t=t, c=RWKV_CHUNK, npair=npair, latent=latent),
        grid=(nseq,),
        in_specs=in_specs, out_specs=out_specs, out_shape=out_shape,
        scratch_shapes=[pltpu.VMEM((t, MIX), F32)] * (N_PREP_OUT + 1),
        input_output_aliases=aliases,
        compiler_params=_params("parallel"),
        name="rwkv_lat" if latent else "rwkv_ctx",
    )(*args)


S5_L = 16
S5_ROWS = M_ALL // S5_L
S5_CW = S5_L * S5_GROUP
S5_GS = 8
S5_PW = 2 * P_C
S5_TAB = 24
S5_PRM = 2 * S5_GROUP + 8
S5_NC_CTX = SEQ // S5_L
S5_NC_LAT = DEC_SEQ // S5_L
S5_CTX_ROWS = BATCH * S5_NC_CTX


def _gelu_tanh(x):
    return 0.5 * x * (1.0 + jnp.tanh(math.sqrt(2.0 / math.pi) * (x + 0.044715 * (x * x * x))))


def _cmul(ar, ai, br, bi):
    return ar * br - ai * bi, ar * bi + ai * br


def _dot3(a, b, dims):
    ah, al = _split2(a)
    bh, bl = _split2(b)
    d = lambda x, y: lax.dot_general(x, y, dims, preferred_element_type=F32)
    return d(ah, bh) + d(ah, bl) + d(al, bh)


def _s5_param_kernel(prm_ref, tab_ref, r_ref, st_ref):
    nl, ns = S5_L, S5_GROUP
    lane_blk = lax.broadcasted_iota(jnp.int32, (ns, S5_CW), 1) // ns
    t_rows = [jnp.zeros((ns, S5_CW), F32) for _ in range(nl)]
    e_parts = []
    for d in range(2):
        part = lambda lo, n: (prm_ref[0, d, 0, lo:lo + n, :], prm_ref[0, d, 1, lo:lo + n, :])
        bb = _cmul(*part(2 * ns, 1), *part(0, ns))
        cc = part(ns, ns)
        pw = lambda k: (tab_ref[0, d, 0, k:k + 1, :], tab_ref[0, d, 1, k:k + 1, :])
        steps = range(nl)
        if d == 0:
            ke, ks, kk = [nl - 1 - j for j in steps], [j + 1 for j in steps], list(steps)
        else:
            ke, ks, kk = list(steps), [nl - j for j in steps], [nl - 1 - j for j in steps]
        stack = lambda xs: (jnp.concatenate([x[0] for x in xs], axis=0), jnp.concatenate([x[1] for x in xs], axis=0))
        e_re, e_im = stack([_cmul(*bb, *pw(k)) for k in ke])
        s_re, s_im = stack([_cmul(*cc, *pw(k)) for k in ks])
        k_re, k_im = stack([_cmul(*cc, *pw(k)) for k in kk])
        st_ref[0, d] = jnp.concatenate([s_re, -s_im], axis=1).astype(BF16)
        krow = _dot3(bb[0], k_re, NT) - _dot3(bb[1], k_im, NT)
        for j in steps:
            if d == 0:
                shifted, keep = pltpu.roll(krow, ns * j, axis=1), lane_blk >= j
            else:
                shifted, keep = pltpu.roll(krow, (S5_CW - ns * (nl - 1 - j)) % S5_CW, axis=1), lane_blk <= j
            t_rows[j] = t_rows[j] + jnp.where(keep, shifted, 0.0)
        e_parts += [e_re, e_im]
    r_ref[0] = jnp.concatenate([jnp.concatenate(t_rows, axis=0)] + e_parts, axis=1).astype(BF16)


def _s5_params(prm, tab):
    return pl.pallas_call(
        _s5_param_kernel,
        grid=(G_C,),
        in_specs=[pl.BlockSpec((1, 2, 2, S5_PRM, P_C), lambda g: (g, 0, 0, 0, 0)),
                  pl.BlockSpec((1, 2, 2, S5_TAB, P_C), lambda g: (g, 0, 0, 0, 0))],
        out_specs=[pl.BlockSpec((1, S5_CW, 2 * S5_CW), lambda g: (g, 0, 0)),
                   pl.BlockSpec((1, 2, S5_CW, S5_PW), lambda g: (g, 0, 0, 0))],
        out_shape=[jax.ShapeDtypeStruct((G_C, S5_CW, 2 * S5_CW), BF16),
                   jax.ShapeDtypeStruct((G_C, 2, S5_CW, S5_PW), BF16)],
        compiler_params=_params("parallel"),
        name="s5_params",
    )(prm, tab)


def _s5_core_kernel(u_ref, r_ref, st_ref, tab_ref, h0_ref, o_ref, hfin_ref, ug_scr, yg_scr, e_scr, hp_scr):
    ns, nl = S5_GROUP, S5_L
    per_v = 128 // ns
    blk = lax.broadcasted_iota(jnp.int32, (8, 128), 1) // ns

    def merge(select):
        acc = select(0)
        for b in range(1, per_v):
            acc = jnp.where(blk == b, select(b), acc)
        return acc

    def shuffle(srcs):
        rolled = []
        for s in range(per_v):
            m = merge(lambda b: srcs[(b + s) % per_v])
            rolled.append(m if s == 0 else pltpu.roll(m, s * ns, axis=1))
        return [merge(lambda b: rolled[(b - a) % per_v]) for a in range(per_v)]

    def tiles(regroup):
        def ctx_tile(ti, _):
            c, b0 = ti // (BATCH // 8), (ti % (BATCH // 8)) * 8
            regroup(b0 * SEQ + c * nl, SEQ, pl.multiple_of(c * BATCH + b0, 8))
            return 0

        def lat_tile(ti, _):
            b, c0 = ti // (S5_NC_LAT // 8), (ti % (S5_NC_LAT // 8)) * 8
            regroup(M_CTX + b * DEC_SEQ + c0 * nl, nl, pl.multiple_of(S5_CTX_ROWS + b * S5_NC_LAT + c0, 8))
            return 0

        lax.fori_loop(0, S5_CTX_ROWS // 8, ctx_tile, 0, unroll=4)
        lax.fori_loop(0, (S5_ROWS - S5_CTX_ROWS) // 8, lat_tile, 0, unroll=4)

    def regroup_in(tok, stride, row):
        uj = [u_ref[pl.ds(tok + j, 8, stride=stride), :] for j in range(nl)]
        for jh in range(nl // per_v):
            for g, v in enumerate(shuffle(uj[jh * per_v:(jh + 1) * per_v])):
                ug_scr[g, pl.ds(row, 8), jh * 128:(jh + 1) * 128] = v

    tiles(regroup_in)

    chains = [(g, d) for g in range(S5_GS) for d in range(2)]
    for g in range(S5_GS):
        a = jnp.dot(ug_scr[g].astype(BF16), r_ref[g], preferred_element_type=F32)
        yg_scr[g] = a[:, 0:S5_CW]
        for d in range(2):
            e_scr[g, d] = a[:, S5_CW + d * S5_PW:S5_CW + (d + 1) * S5_PW]
    lam = []
    for g, d in chains:
        lr, li = (tab_ref[g, d, ri, nl:nl + 1, :] for ri in range(2))
        lam.append((jnp.concatenate([lr, lr], axis=1), jnp.concatenate([-li, li], axis=1)))

    def advance(c, hs, latent):
        nc = S5_NC_LAT if latent else S5_NC_CTX
        out = []
        for i, (g, d) in enumerate(chains):
            cc = c if d == 0 else nc - 1 - c
            rows = (pl.ds(S5_CTX_ROWS + cc, DEC_BATCH, stride=nc) if latent
                    else pl.ds(pl.multiple_of(cc * BATCH, BATCH), BATCH))
            hp_scr[g, d, rows, :] = hs[i]
            out.append(lam[i][0] * hs[i] + lam[i][1] * pltpu.roll(hs[i], P_C, axis=1) + e_scr[g, d, rows, :])
        return out

    def both(c, carry):
        hc, hl = carry
        return tuple(advance(c, hc, False)), tuple(advance(c, hl, True))

    h_ctx = tuple(jnp.zeros((BATCH, S5_PW), F32) for _ in chains)
    h_lat = tuple(h0_ref[g, d] for g, d in chains)
    h_ctx, h_lat = lax.fori_loop(0, S5_NC_CTX, both, (h_ctx, h_lat))
    lax.fori_loop(S5_NC_CTX, S5_NC_LAT, lambda c, hl: tuple(advance(c, hl, True)), h_lat)
    for i, (g, d) in enumerate(chains):
        hfin_ref[g, d] = h_ctx[i]
    for g in range(S5_GS):
        y = yg_scr[g]
        for d in range(2):
            y = y + lax.dot_general(hp_scr[g, d].astype(BF16), st_ref[g, d], NT, preferred_element_type=F32)
        yg_scr[g] = y

    def regroup_out(tok, stride, row):
        for jh in range(nl // per_v):
            yv = [yg_scr[g, pl.ds(row, 8), jh * 128:(jh + 1) * 128] for g in range(S5_GS)]
            for jj, v in enumerate(shuffle(yv)):
                o_ref[pl.ds(tok + jh * per_v + jj, 8, stride=stride), :] = v

    tiles(regroup_out)


def _s5_core(p2, r, st, tab, h0):
    blk = lambda *tail: pl.BlockSpec((S5_GS,) + tail, lambda s: (s,) + (0,) * len(tail))
    col = pl.BlockSpec((M_ALL, S5_GS * S5_GROUP), lambda s: (0, s))
    return pl.pallas_call(
        _s5_core_kernel,
        grid=(G_C // S5_GS,),
        in_specs=[col, blk(S5_CW, 2 * S5_CW), blk(2, S5_CW, S5_PW),
                  blk(2, 2, S5_TAB, P_C), blk(2, DEC_BATCH, S5_PW)],
        out_specs=[col, blk(2, BATCH, S5_PW)],
        out_shape=[jax.ShapeDtypeStruct((M_ALL, MIX), F32),
                   jax.ShapeDtypeStruct((G_C, 2, BATCH, S5_PW), F32)],
        scratch_shapes=[pltpu.VMEM((S5_GS, S5_ROWS, S5_CW), F32),
                        pltpu.VMEM((S5_GS, S5_ROWS, S5_CW), F32),
                        pltpu.VMEM((S5_GS, 2, S5_ROWS, S5_PW), F32),
                        pltpu.VMEM((S5_GS, 2, S5_ROWS, S5_PW), F32)],
        compiler_params=_params("parallel"),
        name="s5_core",
    )(p2, r, st, tab, h0)


def _s5_out_kernel(y_ref, u_ref, d_ref, w_ref, o_ref):
    z = _gelu_tanh(y_ref[...] + d_ref[...] * u_ref[...])
    o_ref[...] = z * _sigmoid(jnp.dot(z.astype(BF16), w_ref[...], preferred_element_type=F32))


def _s5_out(y_tok, p2, d_skip, w_glu_bf16):
    tm = 512
    return pl.pallas_call(
        _s5_out_kernel,
        grid=(M_ALL // tm,),
        in_specs=[pl.BlockSpec((tm, MIX), lambda i: (i, 0)),
                  pl.BlockSpec((tm, MIX), lambda i: (i, 0)),
                  pl.BlockSpec((1, MIX), lambda i: (0, 0)),
                  pl.BlockSpec((MIX, MIX), lambda i: (0, 0))],
        out_specs=pl.BlockSpec((tm, MIX), lambda i: (i, 0)),
        out_shape=jax.ShapeDtypeStruct((M_ALL, MIX), F32),
        compiler_params=_params("parallel"),
        name="s5_out",
    )(y_tok, p2, d_skip.reshape(1, MIX), w_glu_bf16)


def _s5_tables(lam_re, lam_im, log_dt, b_re, b_im, c_re, c_im):
    dt = jnp.exp(log_dt)[:, None, :, None]
    k = jnp.arange(S5_TAB, dtype=F32)[None, :, None, None]
    mag = jnp.exp(lam_re[:, None] * dt * k)
    pw_re = mag * jnp.cos(lam_im[:, None] * dt * k)
    pw_im = mag * jnp.sin(lam_im[:, None] * dt * k)
    ab_re, ab_im = pw_re[:, 1], pw_im[:, 1]
    den = lam_re * lam_re + lam_im * lam_im
    nr = ab_re - 1.0
    f_re = (nr * lam_re + ab_im * lam_im) / den
    f_im = (ab_im * lam_re - nr * lam_im) / den
    rows = lambda b, c, f: jnp.concatenate(
        [jnp.swapaxes(b, -1, -2), c, jnp.broadcast_to(f[:, :, None, :], (2, G_C, 8, P_C))], axis=2)
    prm = jnp.stack([rows(b_re, c_re, f_re), rows(b_im, c_im, f_im)], axis=1)
    tab = jnp.stack([pw_re, pw_im], axis=1)
    return jnp.transpose(prm, (2, 0, 1, 3, 4)), jnp.transpose(tab, (3, 0, 1, 2, 4))


def _s5_mixer(p2, tables, d_skip, w_glu_bf16, state_re, state_im):
    prm, tab = tables
    r, st = _s5_params(prm, tab)
    h0 = jnp.concatenate([state_re, state_im], axis=-1)
    y_tok, hfin = _s5_core(p2, r, st, tab, jnp.transpose(h0, (2, 1, 0, 3)))
    oc = _s5_out(y_tok, p2, d_skip, w_glu_bf16)
    hfin = jnp.transpose(hfin, (2, 1, 0, 3))
    return oc, hfin[..., :P_C], hfin[..., P_C:]


def _ret_kernel(*refs, t, latent):
    if latent:
        (q_ref, k_ref, v_ref, g_ref, dl_ref, lnw_ref, lnb_ref, r0_ref, cos_ref, sin_ref, _,
         o_ref, ob_ref) = refs
    else:
        q_ref, k_ref, v_ref, g_ref, dl_ref, lnw_ref, lnb_ref, o_ref, rfin_ref, ob_ref = refs
    n = t // CHUNK
    w = HD_D
    jf = lax.broadcasted_iota(jnp.int32, (CHUNK, CHUNK), 0).astype(F32)
    kf = lax.broadcasted_iota(jnp.int32, (CHUNK, CHUNK), 1).astype(F32)
    diff = jf - kf
    tabs = []
    for dr in range(2):
        for h in range(H_D):
            l = -_softplus(-dl_ref[h, dr:dr + 1, :])
            if dr == 0:
                dmat = jnp.where(diff >= 0, jnp.exp(l * jnp.maximum(diff, 0.0)), 0.0)
                xi = jnp.exp(l * (jf + 1.0))
                zeta = jnp.exp(l * (CHUNK - 1.0 - jf))
            else:
                dmat = jnp.where(diff < 0, jnp.exp(l * jnp.maximum(-diff, 0.0)), 0.0)
                xi = jnp.exp(l * (CHUNK - jf))
                zeta = jnp.exp(l * jf)
            tabs.append((dmat, xi, zeta, jnp.exp(l * CHUNK)))
    chains = [(dr, h) for dr in range(2) for h in range(H_D)]

    def body(ci, states):
        q, k, v, where = [], [], [], []
        for dr, h in chains:
            cj = ci if dr == 0 else n - 1 - ci
            rows = pl.ds(pl.multiple_of(cj * CHUNK, CHUNK), CHUNK)
            cols = slice(h * w, (h + 1) * w)
            qi = q_ref[rows, cols]
            ki = k_ref[rows, cols] * (HD_D ** -0.5)
            if latent:
                qi = _rope_pairs(qi, cos_ref[rows, :], sin_ref[rows, :], HD_D // 2)
                ki = _rope_pairs(ki, cos_ref[rows, :], sin_ref[rows, :], HD_D // 2)
            q.append(qi.astype(BF16))
            k.append(ki)
            v.append(v_ref[rows, cols].astype(BF16))
            where.append((o_ref if dr == 0 else ob_ref, rows, cols))
        nch = range(len(chains))
        inner = [(_bdot(q[i], k[i], NT) * tabs[i][0]).astype(BF16) for i in nch]
        cross = [_bdot(q[i], states[i]) * tabs[i][1] for i in nch]
        kz = [(k[i] * tabs[i][2]).astype(BF16) for i in nch]
        out = [jnp.dot(inner[i], v[i], preferred_element_type=F32) + cross[i] for i in nch]
        new_states = [states[i] * tabs[i][3] + lax.dot_general(kz[i], v[i], TN, preferred_element_type=F32)
                      for i in nch]
        for (dst, rows, cols), y in zip(where, out):
            dst[rows, cols] = y
        return tuple(new_states)

    if latent:
        lax.fori_loop(0, n, body, tuple(r0_ref[0, dr, h] for dr, h in chains))
    else:
        fin = lax.fori_loop(0, n, body, tuple(jnp.zeros((w, w), F32) for _ in chains))
        for i, (dr, h) in enumerate(chains):
            rfin_ref[0, dr, h] = fin[i]
    for h in range(H_D):
        cols = slice(h * w, (h + 1) * w)
        y = o_ref[:, cols] + ob_ref[:, cols]
        yc = y - jnp.mean(y, axis=-1, keepdims=True)
        yn = yc * lax.rsqrt(jnp.mean(yc * yc, axis=-1, keepdims=True) + RET_GN_EPS)
        o_ref[:, cols] = (yn * lnw_ref[:, cols] + lnb_ref[:, cols]) * _silu(g_ref[:, cols])


def _retention(p, decay_logit, ln_w, ln_b, latent, r0=None, rope=None, dst=None):
    if latent:
        nseq, t, rb0 = DEC_BATCH, DEC_SEQ, M_CTX // DEC_SEQ
    else:
        nseq, t, rb0 = BATCH, SEQ, 0
    w = HD_D
    dl = jnp.broadcast_to(decay_logit.T[:, :, None], (H_D, 2, w))
    full = lambda shape: pl.BlockSpec(shape, lambda b: (0,) * len(shape))
    st_spec = pl.BlockSpec((1, 2, H_D, w, w), lambda b: (b, 0, 0, 0, 0))
    in_specs = [pl.BlockSpec((t, MIX), lambda b, j=j: (rb0 + b, j)) for j in (1, 2, 3, 4)]
    in_specs += [full((H_D, 2, w)), full((1, MIX)), full((1, MIX))]
    args = [p, p, p, p, dl, ln_w.reshape(1, MIX), ln_b.reshape(1, MIX)]
    o_spec = pl.BlockSpec((t, MIX), lambda b: (rb0 + b, 0))
    o_shape = jax.ShapeDtypeStruct((M_ALL, MIX), F32)
    if latent:
        in_specs += [st_spec, full((t, w)), full((t, w)), pl.BlockSpec(memory_space=pl.ANY)]
        args += [r0, *rope, dst]
        out_specs, out_shape, aliases = o_spec, o_shape, {len(args) - 1: 0}
    else:
        out_specs = [o_spec, st_spec]
        out_shape = [o_shape, jax.ShapeDtypeStruct((nseq, 2, H_D, w, w), F32)]
        aliases = {}
    return pl.pallas_call(
        functools.partial(_ret_kernel, t=t, latent=latent),
        grid=(nseq,),
        in_specs=in_specs, out_specs=out_specs, out_shape=out_shape,
        scratch_shapes=[pltpu.VMEM((t, MIX), F32)],
        input_output_aliases=aliases,
        compiler_params=_params("parallel"),
        name="retention_lat" if latent else "retention_ctx",
    )(*args)


def _rope_tables(n_tok, dim, reps):
    rows = n_tok // GRID_W
    n_freq = dim // 4
    inv = 1.0 / (ROPE_THETA ** (jnp.arange(n_freq, dtype=F32) / n_freq))
    row = jnp.repeat(jnp.arange(rows, dtype=F32), GRID_W)
    col = jnp.tile(jnp.arange(GRID_W, dtype=F32), rows)
    ang = jnp.concatenate([row[:, None] * inv, col[:, None] * inv], axis=-1)
    cos, sin = jnp.cos(ang), jnp.sin(ang)
    return jnp.tile(jnp.concatenate([cos, cos], axis=1), (1, reps)), \
        jnp.tile(jnp.concatenate([-sin, sin], axis=1), (1, reps))


def kernel(x_prompt, x_sample, cache_k_ab, cache_v_ab, state_rwkv, state_s5_re, state_s5_im, state_ret, c, c_ctx, norm1_g, norm2_g, w_mod, b_mod, w_ff_gate, w_ff_up, w_ff_down, w_in_ab, w_out_ab, qk_gain_a, lambda_qk, subln_g, rwkv_mu, rwkv_k_k, rwkv_k_a, rwkv_r_k, rwkv_w0, rwkv_w_up, rwkv_a0, rwkv_a_up, rwkv_g_up, rwkv_ln_w, rwkv_ln_b, w_in_cd, w_out_cd, s5_lam_re, s5_lam_im, s5_log_dt, s5_b_re, s5_b_im, s5_c_re, s5_c_im, s5_d, s5_w_glu, ret_decay_logit, ret_ln_w, ret_ln_b):
    d = D_MODEL
    xs = [x_prompt.reshape(M_CTX, d), x_sample.reshape(M_LAT, d)]
    cvec = jnp.zeros((MOD_ROWS, d), F32).at[0].set(c_ctx).at[1:1 + DEC_BATCH].set(c)
    mods = _modulation(cvec, w_mod, b_mod)

    lam_init = 0.8 - 0.6 * math.exp(-0.3 * 0)
    p = _norm_linear(xs, norm1_g[0], mods[0], w_in_ab[0].astype(BF16))
    gain2 = jnp.tile(qk_gain_a[0], (1, 2))
    rope_a = _rope_tables(DEC_SEQ, HD_A, 2)
    ck = cache_k_ab[:, 0].reshape(DEC_BATCH * PAST_LEN, MIX)
    cv = cache_v_ab[:, 0].reshape(DEC_BATCH * PAST_LEN, MIX)
    oa, k_ctx, v_ctx = _attention(p, gain2, lambda_qk[0], subln_g[0], lam_init, latent=False)
    oa = _attention(p, gain2, lambda_qk[0], subln_g[0], lam_init, latent=True,
                    cache_k=ck, cache_v=cv, rope=rope_a, dst=oa)
    rw_prm = (rwkv_mu[0], rwkv_k_k[0], rwkv_k_a[0], rwkv_r_k[0], rwkv_w0[0], rwkv_w_up[0],
              rwkv_a0[0], rwkv_a_up[0], rwkv_g_up[0])
    ob, sfin_ctx = _rwkv(p, rw_prm, rwkv_ln_w[0], rwkv_ln_b[0], latent=False)
    ob = _rwkv(p, rw_prm, rwkv_ln_w[0], rwkv_ln_b[0], latent=True, s0=state_rwkv[:, 0], dst=ob)
    x = _mix_ffn(xs, oa, ob, w_out_ab[0].astype(BF16), norm2_g[0], mods[0], w_ff_gate[0].astype(BF16),
                 w_ff_up[0].astype(BF16), w_ff_down[0].astype(BF16), split_out=False)

    p2 = _norm_linear([x], norm1_g[1], mods[1], w_in_cd[0].astype(BF16))
    tables = _s5_tables(s5_lam_re[0], s5_lam_im[0], s5_log_dt[0], s5_b_re[0], s5_b_im[0],
                        s5_c_re[0], s5_c_im[0])
    oc, s5_fin_re, s5_fin_im = _s5_mixer(p2, tables, s5_d[0], s5_w_glu[0].astype(BF16),
                                         state_s5_re[:, 0], state_s5_im[:, 0])
    rope_d = _rope_tables(DEC_SEQ, HD_D, 1)
    od, rfin = _retention(p2, ret_decay_logit[0], ret_ln_w[0], ret_ln_b[0], latent=False)
    od = _retention(p2, ret_decay_logit[0], ret_ln_w[0], ret_ln_b[0], latent=True, r0=state_ret[:, 0],
                    rope=rope_d, dst=od)
    y_ctx, y_lat = _mix_ffn([x], oc, od, w_out_cd[0].astype(BF16), norm2_g[1], mods[1],
                            w_ff_gate[1].astype(BF16), w_ff_up[1].astype(BF16), w_ff_down[1].astype(BF16),
                            split_out=True)
    y_prompt = y_ctx.reshape(BATCH, SEQ, d)
    y_sample = y_lat.reshape(DEC_BATCH, DEC_SEQ, d)
    new_k = jnp.transpose(k_ctx, (0, 4, 1, 2, 3))[:, None]
    new_v = v_ctx.reshape(BATCH, 1, SEQ, H_A, VD_A)
    new_rwkv = sfin_ctx[:, None]
    new_s5_re = s5_fin_re[:, None]
    new_s5_im = s5_fin_im[:, None]
    new_ret = rfin[:, None]
    return (y_prompt, y_sample, new_k, new_v, new_rwkv, new_s5_re, new_s5_im, new_ret)
```

```python
import functools
import math

import numpy as np
import jax
import jax.numpy as jnp
from jax import lax
from jax.experimental import pallas as pl
from jax.experimental.pallas import tpu as pltpu

F32 = jnp.float32
BF16 = jnp.bfloat16
HIGHEST = lax.Precision.HIGHEST

D_MODEL = 1024
BATCH = 32
SEQ = 256
DEC_BATCH = 2
DEC_SEQ = 1024
PAST_LEN = 256
GRID_W = 64
H_A = 4
HD_A = 64
VD_A = 128
H_B = 8
HD_B = 64
MIX = 512
LORA_W = 64
LORA_A = 64
LORA_G = 128
S5_GROUP = 16
G_C = 32
P_C = 64
S5_STATE = G_C * P_C
H_D = 4
HD_D = 128
CHUNK = 128
D_FF = 2816
IN_AB = 3328
IN_CD = 2560
ROPE_THETA = 10000.0
NORM_EPS = 1e-6
RWKV_GN_EPS = 64e-5
RET_GN_EPS = 1e-5

M_CTX = BATCH * SEQ
M_LAT = DEC_BATCH * DEC_SEQ
M_ALL = M_CTX + M_LAT
MOD_ROWS = 8
RWKV_CHUNK = 64
VMEM_LIMIT = 56 * 1024 * 1024

NN = (((1,), (0,)), ((), ()))
NT = (((1,), (1,)), ((), ()))
TN = (((0,), (0,)), ((), ()))


def _params(*sem):
    return pltpu.CompilerParams(dimension_semantics=sem, vmem_limit_bytes=VMEM_LIMIT)


def _bdot(a, b, dims=NN):
    return lax.dot_general(a.astype(BF16), b.astype(BF16), dims, preferred_element_type=F32)


def _hdot(a, b, dims=NN):
    return lax.dot_general(a, b, dims, precision=HIGHEST, preferred_element_type=F32)


def _split_dot(x, m):
    hi = x.astype(BF16)
    lo = (x - hi.astype(F32)).astype(BF16)
    return (jnp.dot(hi, m, preferred_element_type=F32) + jnp.dot(lo, m, preferred_element_type=F32))


def _seg_matrix(n, shift, val):
    r = lax.broadcasted_iota(jnp.int32, (n, n), 0) >> shift
    c = lax.broadcasted_iota(jnp.int32, (n, n), 1) >> shift
    return jnp.where(r == c, val, 0.0).astype(BF16)


def _sigmoid(x):
    return jax.nn.sigmoid(x)


def _silu(x):
    return x * jax.nn.sigmoid(x)


def _softplus(x):
    return jnp.maximum(x, 0.0) + jnp.log(1.0 + jnp.exp(-jnp.abs(x)))


def _mod_row(tile, tm):
    r0 = tile * tm
    return jnp.where(r0 < M_CTX, 0, 1 + (r0 - M_CTX) // DEC_SEQ)


def _norm_mod(x, g, sc_ref, sh_ref, row):
    y = x * lax.rsqrt(jnp.mean(x * x, axis=-1, keepdims=True) + NORM_EPS) * g
    return y * (1.0 + sc_ref[pl.ds(row, 1), :]) + sh_ref[pl.ds(row, 1), :]


def _mod_kernel(c_ref, w_ref, b_ref, o_ref):
    o_ref[0] = _bdot(_silu(c_ref[...]), w_ref[0]) + b_ref[0]


def _modulation(cvec, w_mod, b_mod):
    depth, d, n6 = w_mod.shape
    tn = 1536
    return pl.pallas_call(
        _mod_kernel,
        grid=(depth, n6 // tn),
        in_specs=[pl.BlockSpec((MOD_ROWS, d), lambda l, j: (0, 0)),
                  pl.BlockSpec((1, d, tn), lambda l, j: (l, 0, j)),
                  pl.BlockSpec((1, 1, tn), lambda l, j: (l, 0, j))],
        out_specs=pl.BlockSpec((1, MOD_ROWS, tn), lambda l, j: (l, 0, j)),
        out_shape=jax.ShapeDtypeStruct((depth, MOD_ROWS, n6), F32),
        compiler_params=_params("parallel", "parallel"),
        name="modulation",
    )(cvec, w_mod, b_mod.reshape(depth, 1, n6))


def _tile_specs(tm, d):
    nc = M_CTX // tm
    return [pl.BlockSpec((tm, d), lambda i: (jnp.minimum(i, nc - 1), 0)),
            pl.BlockSpec((tm, d), lambda i: (jnp.maximum(i - nc, 0), 0))]


def _pick_rows(refs, tm):
    if len(refs) == 1:
        return refs[0][...]
    return jnp.where(pl.program_id(0) < M_CTX // tm, refs[0][...], refs[1][...])


def _norm_linear_kernel(*refs, tm):
    *x_refs, g_ref, sc_ref, sh_ref, w_ref, o_ref = refs
    row = _mod_row(pl.program_id(0), tm)
    x = _pick_rows(x_refs, tm)
    nsub = 2
    rows = [slice(i * tm // nsub, (i + 1) * tm // nsub) for i in range(nsub)]
    hs = [_norm_mod(x[r], g_ref[...], sc_ref, sh_ref, row).astype(BF16) for r in rows]
    for r, h in zip(rows, hs):
        o_ref[r, :] = jnp.dot(h, w_ref[...], preferred_element_type=F32)


def _norm_linear(xs, g, mods, w_bf16):
    tm = 512
    d = D_MODEL
    n = w_bf16.shape[1]
    x_specs = _tile_specs(tm, d) if len(xs) == 2 else [pl.BlockSpec((tm, d), lambda i: (i, 0))]
    return pl.pallas_call(
        functools.partial(_norm_linear_kernel, tm=tm),
        grid=(M_ALL // tm,),
        in_specs=x_specs + [pl.BlockSpec((1, d), lambda i: (0, 0)),
                            pl.BlockSpec((MOD_ROWS, d), lambda i: (0, 1)),
                            pl.BlockSpec((MOD_ROWS, d), lambda i: (0, 0)),
                            pl.BlockSpec((d, n), lambda i: (0, 0))],
        out_specs=pl.BlockSpec((tm, n), lambda i: (i, 0)),
        out_shape=jax.ShapeDtypeStruct((M_ALL, n), F32),
        compiler_params=_params("arbitrary"),
        name="norm_linear",
    )(*xs, g.reshape(1, d), mods, mods, w_bf16)


def _mix_ffn_kernel(*refs, tm, ck, n_in, n_out):
    x_refs = refs[:n_in]
    (a_ref, b_ref, wa_ref, wb_ref, g1_ref, ng_ref, sc_ref, sh_ref, g2_ref,
     wg_ref, wu_ref, wd_ref) = refs[n_in:n_in + 12]
    o_refs = refs[n_in + 12:]
    row = _mod_row(pl.program_id(0), tm)
    mix = (jnp.dot(a_ref[...].astype(BF16), wa_ref[...], preferred_element_type=F32)
           + jnp.dot(b_ref[...].astype(BF16), wb_ref[...], preferred_element_type=F32))
    x = _pick_rows(x_refs, tm) + g1_ref[pl.ds(row, 1), :] * mix
    h = _norm_mod(x, ng_ref[...], sc_ref, sh_ref, row).astype(BF16)
    acc = jnp.zeros((tm, D_MODEL), F32)
    for c in range(D_FF // ck):
        gg = jnp.dot(h, wg_ref[:, c * ck:(c + 1) * ck], preferred_element_type=F32)
        uu = jnp.dot(h, wu_ref[:, c * ck:(c + 1) * ck], preferred_element_type=F32)
        act = (_silu(gg) * uu).astype(BF16)
        acc = acc + jnp.dot(act, wd_ref[c * ck:(c + 1) * ck, :], preferred_element_type=F32)
    y = x + g2_ref[pl.ds(row, 1), :] * acc
    if n_out == 1:
        o_refs[0][...] = y
    else:
        is_ctx = pl.program_id(0) < M_CTX // tm

        @pl.when(is_ctx)
        def _():
            o_refs[0][...] = y

        @pl.when(jnp.logical_not(is_ctx))
        def _():
            o_refs[1][...] = y


def _mix_ffn(xs, oa, ob, w_out_bf16, norm_g, mods, wg, wu, wd, split_out):
    tm, ck = 512, 256
    d = D_MODEL
    row_spec = lambda n: pl.BlockSpec((tm, n), lambda i: (i, 0))
    const = lambda shape, idx: pl.BlockSpec(shape, lambda i: idx, pipeline_mode=pl.Buffered(1))
    mod = lambda j: pl.BlockSpec((MOD_ROWS, d), lambda i: (0, j))
    x_specs = _tile_specs(tm, d) if len(xs) == 2 else [row_spec(d)]
    if split_out:
        out_specs = _tile_specs(tm, d)
        out_shape = [jax.ShapeDtypeStruct((M_CTX, d), F32), jax.ShapeDtypeStruct((M_LAT, d), F32)]
    else:
        out_specs, out_shape = row_spec(d), jax.ShapeDtypeStruct((M_ALL, d), F32)
    return pl.pallas_call(
        functools.partial(_mix_ffn_kernel, tm=tm, ck=ck, n_in=len(xs), n_out=2 if split_out else 1),
        grid=(M_ALL // tm,),
        in_specs=x_specs + [row_spec(MIX), row_spec(MIX),
                            const((MIX, d), (0, 0)), const((MIX, d), (1, 0)),
                            mod(2),
                            pl.BlockSpec((1, d), lambda i: (0, 0)),
                            mod(4), mod(3), mod(5),
                            const((d, D_FF), (0, 0)), const((d, D_FF), (0, 0)), const((D_FF, d), (0, 0))],
        out_specs=out_specs, out_shape=out_shape,
        compiler_params=_params("arbitrary"),
        name="mix_ffn",
    )(*xs, oa, ob, w_out_bf16, w_out_bf16, mods, norm_g.reshape(1, d), mods, mods, mods, wg, wu, wd)


def _qk_norm(x, gain, segm):
    ms = _split_dot(x * x, segm)
    return x * lax.rsqrt(ms + NORM_EPS) * gain


def _rope_pairs(x, cosf, sinf, half):
    lane = lax.broadcasted_iota(jnp.int32, x.shape, 1)
    first = (lane & (2 * half - 1)) < half
    n = x.shape[1]
    partner = jnp.where(first, pltpu.roll(x, n - half, axis=1), pltpu.roll(x, half, axis=1))
    return x * cosf + partner * sinf


def _attn_kernel(*refs, latent, lam_init):
    if latent:
        (q_ref, k_ref, v_ref, ck_ref, cv_ref, cosq_ref, sinq_ref, cosk_ref, sink_ref,
         gain_ref, lam_ref, sub_ref, _, o_ref, kall, vall) = refs
    else:
        q_ref, k_ref, v_ref, gain_ref, lam_ref, sub_ref, o_ref, kn_ref, vo_ref, kall, vall = refs
    w = 2 * HD_A
    segm = _seg_matrix(w, 6, 1.0 / HD_A)
    gains = gain_ref[...]
    cols = [slice(h * w, (h + 1) * w) for h in range(H_A)]

    @pl.when(pl.program_id(1) == 0)
    def _():
        k = [_qk_norm(k_ref[:, c], gains[1:2], segm) for c in cols]
        if latent:
            k = [_rope_pairs(x, cosk_ref[...], sink_ref[...], HD_A // 2) for x in k]
            for c, x in zip(cols, k):
                kall[0:PAST_LEN, c] = ck_ref[:, c].astype(BF16)
                kall[PAST_LEN:, c] = x.astype(BF16)
                vall[0:PAST_LEN, c] = cv_ref[:, c].astype(BF16)
                vall[PAST_LEN:, c] = v_ref[:, c].astype(BF16)
        else:
            for h, (c, x) in enumerate(zip(cols, k)):
                xt = x.T
                kn_ref[0, h, 0] = xt[:HD_A]
                kn_ref[0, h, 1] = xt[HD_A:]
                vo_ref[pl.ds(h, v_ref.shape[0], stride=H_A), :] = v_ref[:, c]
                kall[:, c] = x.astype(BF16)
                vall[:, c] = v_ref[:, c].astype(BF16)

    q = [_qk_norm(q_ref[:, c], gains[0:1], segm) for c in cols]
    if latent:
        q = [_rope_pairs(x, cosq_ref[...], sinq_ref[...], HD_A // 2) for x in q]
    lv = lam_ref[...]
    lam = (jnp.exp(jnp.sum(lv[0:1] * lv[1:2], axis=1, keepdims=True))
           - jnp.exp(jnp.sum(lv[2:3] * lv[3:4], axis=1, keepdims=True)) + lam_init)
    scale = HD_A ** -0.5
    comp0 = lax.broadcasted_iota(jnp.int32, q[0].shape, 1) < HD_A
    qc = [jnp.where(comp0, *sel).astype(BF16) for x in q for sel in ((x, 0.0), (0.0, x))]
    s = [lax.dot_general(qc[i], kall[:, cols[i // 2]], NT, preferred_element_type=F32) * scale
         for i in range(2 * H_A)]
    e = [jnp.exp(x - jnp.max(x, axis=-1, keepdims=True)) for x in s]
    p = [x / jnp.sum(x, axis=-1, keepdims=True) for x in e]
    att = [(p[2 * h] - lam * p[2 * h + 1]).astype(BF16) for h in range(H_A)]
    o = [jnp.dot(att[h], vall[:, cols[h]], preferred_element_type=F32) for h in range(H_A)]
    o = [x * lax.rsqrt(jnp.mean(x * x, axis=-1, keepdims=True) + NORM_EPS) * sub_ref[...] for x in o]
    for c, x in zip(cols, o):
        o_ref[:, c] = x * (1.0 - lam_init)


def _attention(p, gain2, lambda_qk, subln_g, lam_init, latent, cache_k=None, cache_v=None, rope=None,
               dst=None):
    w = 2 * HD_A
    if latent:
        nseq, t, tq, rb0, s_len = DEC_BATCH, DEC_SEQ, 128, M_CTX // DEC_SEQ, PAST_LEN + DEC_SEQ
    else:
        nseq, t, tq, rb0, s_len = BATCH, SEQ, SEQ, 0, SEQ
    nq = t // tq
    qoff = rb0 * nq
    full = lambda shape: pl.BlockSpec(shape, lambda b, i: (0,) * len(shape))
    in_specs = [pl.BlockSpec((tq, MIX), lambda b, i: (qoff + b * nq + i, 0)),
                pl.BlockSpec((t, MIX), lambda b, i: (rb0 + b, 1)),
                pl.BlockSpec((t, MIX), lambda b, i: (rb0 + b, 2))]
    args = [p, p, p]
    if latent:
        cosf, sinf = rope
        in_specs += [pl.BlockSpec((PAST_LEN, MIX), lambda b, i: (b, 0)),
                     pl.BlockSpec((PAST_LEN, MIX), lambda b, i: (b, 0)),
                     pl.BlockSpec((tq, w), lambda b, i: (i, 0)),
                     pl.BlockSpec((tq, w), lambda b, i: (i, 0)),
                     full((t, w)), full((t, w))]
        args += [cache_k, cache_v, cosf, sinf, cosf, sinf]
    in_specs += [full((2, w)), full((4, HD_A)), full((1, w))]
    args += [gain2, lambda_qk, subln_g.reshape(1, w)]
    o_spec = pl.BlockSpec((tq, MIX), lambda b, i: (qoff + b * nq + i, 0))
    o_shape = jax.ShapeDtypeStruct((M_ALL, MIX), F32)
    aliases = {}
    if latent:
        out_specs, out_shape = o_spec, o_shape
        in_specs.append(pl.BlockSpec(memory_space=pl.ANY))
        args.append(dst)
        aliases = {len(args) - 1: 0}
    else:
        out_specs = [o_spec, pl.BlockSpec((1, H_A, 2, HD_A, t), lambda b, i: (b, 0, 0, 0, 0)),
                     pl.BlockSpec((t * H_A, VD_A), lambda b, i: (b, 0))]
        out_shape = [o_shape, jax.ShapeDtypeStruct((nseq, H_A, 2, HD_A, t), F32),
                     jax.ShapeDtypeStruct((nseq * t * H_A, VD_A), F32)]
    return pl.pallas_call(
        functools.partial(_attn_kernel, latent=latent, lam_init=lam_init),
        grid=(nseq, nq),
        in_specs=in_specs, out_specs=out_specs, out_shape=out_shape,
        scratch_shapes=[pltpu.VMEM((s_len, MIX), BF16), pltpu.VMEM((s_len, MIX), BF16)],
        input_output_aliases=aliases,
        compiler_params=_params("parallel", "arbitrary"),
        name="diff_attention_lat" if latent else "diff_attention_ctx",
    )(*args)


def _centred_shift(x, mu):
    t = x.shape[0]
    row = lax.broadcasted_iota(jnp.int32, x.shape, 0)
    prev = jnp.where(row == 0, 0.0, pltpu.roll(x, 1, axis=0))
    nxt = jnp.where(row == t - 1, 0.0, pltpu.roll(x, t - 1, axis=0))
    return x + (0.5 * (prev + nxt) - x) * mu


def _seg_sum(x, segm):
    return jnp.concatenate([_split_dot(x[:, j * 128:(j + 1) * 128], segm) for j in range(x.shape[1] // 128)],
                           axis=1)


def _rwkv_prep_kernel(r_ref, k_ref, v_ref, l_ref, mur_ref, muk_ref, muv_ref, mul_ref,
                      kk_ref, ka_ref, rk_ref, w0_ref, wup_ref, a0_ref, aup_ref, gup_ref,
                      ro_ref, ldf_ref, ldb_ref, kbo_ref, vbo_ref, kko_ref, ao_ref, gate_ref, bonus_ref):
    seg1 = _seg_matrix(128, 6, 1.0)
    r = _centred_shift(r_ref[...], mur_ref[...])
    kb = _centred_shift(k_ref[...], muk_ref[...])
    vb = _centred_shift(v_ref[...], muv_ref[...])
    lo = _centred_shift(l_ref[...], mul_ref[...])
    xw = lo[:, 0:LORA_W]
    xa = lo[:, LORA_W:LORA_W + LORA_A]
    xg = lo[:, LORA_W + LORA_A:]
    kk = kb * kk_ref[...]
    kk = kk * lax.rsqrt(_seg_sum(kk * kk, seg1) + 1e-12)
    a = _sigmoid(a0_ref[...] + _bdot(xa, aup_ref[...]))
    kb2 = kb * (1.0 + (a - 1.0) * ka_ref[...])
    lw = jnp.tanh(xw)
    for dr, ld_ref in enumerate((ldf_ref, ldb_ref)):
        z = w0_ref[dr:dr + 1, :] + _bdot(lw, wup_ref[dr])
        logw = -_softplus(-z) - 0.5
        ld_ref[...] = -jnp.exp(logw)
    gate_ref[...] = _bdot(_sigmoid(xg), gup_ref[...])
    bonus_ref[...] = _seg_sum(r * kb2 * rk_ref[...], seg1) * vb
    ro_ref[...] = r
    kbo_ref[...] = kb2
    vbo_ref[...] = vb
    kko_ref[...] = kk
    ao_ref[...] = a


N_PREP_IN = 16
N_PREP_OUT = 9


def _split2(x):
    hi = x.astype(BF16)
    return hi, (x - hi.astype(F32)).astype(BF16)


RWKV_INV_BLOCK = 16
RWKV_UNROLL_MAX = 4


def _rwkv_masks(c, rev):
    ti = lax.broadcasted_iota(jnp.int32, (c, c), 0)
    si = lax.broadcasted_iota(jnp.int32, (c, c), 1)
    tri = jnp.where((si >= ti) if rev else (si <= ti), 1.0, 0.0).astype(BF16)
    t4 = lax.broadcasted_iota(jnp.int32, (4 * c, 4 * c), 0)
    s4 = lax.broadcasted_iota(jnp.int32, (4 * c, 4 * c), 1)
    tm, sm = t4 & (c - 1), s4 & (c - 1)
    strict = (sm > tm) if rev else (sm < tm)
    incl = (sm >= tm) if rev else (sm <= tm)
    same_head = ((t4 // c) & 1) == ((s4 // c) & 1)
    top = t4 < 2 * c
    gmask = same_head & ((top & strict) | (~top & incl))
    t2 = lax.broadcasted_iota(jnp.int32, (2 * c, 2 * c), 0)
    s2 = lax.broadcasted_iota(jnp.int32, (2 * c, 2 * c), 1)
    same = lambda n: (t2 // n) == (s2 // n)
    levels = []
    n = RWKV_INV_BLOCK
    while n < c:
        levels.append(same(2 * n) & ~same(n))
        n *= 2
    f = lambda m: jnp.where(m, 1.0, 0.0)
    return tri, f(gmask), f(same(RWKV_INV_BLOCK)), tuple(f(m) for m in levels), f(same(c)), f(t2 == s2)


def _keep(mask01, x):
    return jnp.where(mask01 > 0.5, x, 0.0)


def _tri_inverse(a, diag_blk, levels, eye):
    n = a[0].shape[0]
    d = [_keep(diag_blk, x) for x in a]
    t = [eye + x for x in d]
    p = [_bdot(x, x) for x in d]
    for _ in range(int(math.log2(RWKV_INV_BLOCK)) - 2):
        res = [_bdot(jnp.concatenate([pi, ti], axis=0), pi) for pi, ti in zip(p, t)]
        p = [x[:n] for x in res]
        t = [ti + x[n:] for ti, x in zip(t, res)]
    t = [ti + _bdot(ti, pi) for ti, pi in zip(t, p)]
    for off in levels:
        half = [_bdot(ti, _keep(off, x)) for ti, x in zip(t, a)]
        t = [ti + _bdot(x, ti) for ti, x in zip(t, half)]
    return t


def _rwkv_pair_chunks(ins, sts, tris, gmasks, diag_blk, levels, same_head, eye):
    c, w = ins[0][0].shape
    hd = w // 2
    nch = range(len(ins))
    r, ld, kb, vb, kk, a = (list(z) for z in zip(*ins))
    split = [_split2(x) for x in ld]
    lcum = [jnp.dot(tris[i], jnp.concatenate(split[i], axis=1), preferred_element_type=F32) for i in nch]
    lcum = [x[:, :w] + x[:, w:] for x in lcum]
    ltot = [jnp.sum(x, axis=0, keepdims=True) for x in ld]
    beta = [kk[i] * a[i] for i in nch]
    eneg = [jnp.exp(-x) for x in lcum]
    abar = [-kk[i] * jnp.exp(lcum[i] - ld[i]) for i in nch]
    rbar = [r[i] * jnp.exp(lcum[i]) for i in nch]
    bt = [(beta[i] * eneg[i]).astype(BF16) for i in nch]
    kt = [(kb[i] * eneg[i]).astype(BF16) for i in nch]
    vbb = [x.astype(BF16) for x in vb]
    head0 = lax.broadcasted_iota(jnp.int32, (c, w), 1) < hd
    pick = lambda res: jnp.where(head0, res[:c], res[c:])
    arst = [_bdot(jnp.concatenate([abar[i], rbar[i]], axis=0), sts[i], NT) for i in nch]
    lhs = [jnp.concatenate([jnp.where(head0, abar[i], 0.0), jnp.where(head0, 0.0, abar[i]),
                            jnp.where(head0, rbar[i], 0.0), jnp.where(head0, 0.0, rbar[i])], axis=0) for i in nch]
    g = [_keep(gmasks[i], _bdot(lhs[i], jnp.concatenate([bt[i], bt[i], kt[i], kt[i]], axis=0), NT))
         for i in nch]
    x = [arst[i][:c] + pick(_bdot(g[i][:2 * c, 2 * c:], jnp.concatenate([vbb[i], vbb[i]], axis=0))) for i in nch]
    tinv = _tri_inverse([gi[:2 * c, :2 * c] for gi in g], diag_blk, levels, eye)
    u = [pick(_bdot(tinv[i], jnp.concatenate([x[i], x[i]], axis=0))) for i in nch]
    ub = [z.astype(BF16) for z in u]
    y = [arst[i][c:] + pick(_bdot(g[i][2 * c:], jnp.concatenate([ub[i], ub[i], vbb[i], vbb[i]], axis=0)))
         for i in nch]
    erem = [jnp.exp(ltot[i] - lcum[i]) for i in nch]
    bkh = [jnp.concatenate([beta[i] * erem[i], kb[i] * erem[i]], axis=0) for i in nch]
    st_new = [jnp.exp(ltot[i]) * sts[i]
              + _keep(same_head, _bdot(jnp.concatenate([ub[i], vbb[i]], axis=0), bkh[i], TN)) for i in nch]
    return y, st_new


def _rwkv_scan_kernel(*refs, t, c, npair, latent):
    prep_in, (lnw_ref, lnb_ref), rest = refs[:N_PREP_IN], refs[N_PREP_IN:N_PREP_IN + 2], refs[N_PREP_IN + 2:]
    if latent:
        s0_ref, _, o_ref, *scratch = rest
    else:
        o_ref, sfin_ref, *scratch = rest
    *prep_out, yb_ref = scratch
    _rwkv_prep_kernel(*prep_in, *prep_out)
    r_ref, ldf_ref, ldb_ref, kb_ref, vb_ref, kk_ref, a_ref, gate_ref, bonus_ref = prep_out
    n = t // c
    w = 2 * HD_B
    masks = [_rwkv_masks(c, rev) for rev in (False, True)]

    def body(ci, states):
        ins, tris, gmasks, dsts = [], [], [], []
        for dr, (ld_ref, dst) in enumerate(((ldf_ref, o_ref), (ldb_ref, yb_ref))):
            cj = ci if dr == 0 else n - 1 - ci
            rows = pl.ds(pl.multiple_of(cj * c, c), c)
            for p in range(npair):
                cols = slice(p * w, (p + 1) * w)
                ins.append(tuple(ref[rows, cols] for ref in (r_ref, ld_ref, kb_ref, vb_ref, kk_ref, a_ref)))
                tris.append(masks[dr][0])
                gmasks.append(masks[dr][1])
                dsts.append((dst, rows, cols))
        ys, new_states = _rwkv_pair_chunks(ins, list(states), tris, gmasks, *masks[0][2:])
        for (dst, rows, cols), y in zip(dsts, ys):
            dst[rows, cols] = y
        return tuple(new_states)

    if latent:
        zero = jnp.zeros((HD_B, HD_B), F32)
        init = tuple(jnp.concatenate([jnp.concatenate([s0_ref[0, dr, 2 * p], zero], axis=1),
                                      jnp.concatenate([zero, s0_ref[0, dr, 2 * p + 1]], axis=1)], axis=0)
                     for dr in range(2) for p in range(npair))
    else:
        init = tuple(jnp.zeros((w, w), F32) for _ in range(2 * npair))
    fin = lax.fori_loop(0, n, body, init, unroll=True if n <= RWKV_UNROLL_MAX else 2)
    segm = _seg_matrix(w, 6, 1.0 / HD_B)
    for p in range(npair):
        if not latent:
            for dr in range(2):
                st = fin[dr * npair + p]
                sfin_ref[0, dr, 2 * p] = st[:HD_B, :HD_B]
                sfin_ref[0, dr, 2 * p + 1] = st[HD_B:, HD_B:]
        cols = slice(p * w, (p + 1) * w)
        y = o_ref[:, cols] + yb_ref[:, cols]
        yc = y - _split_dot(y, segm)
        yn = yc * lax.rsqrt(_split_dot(yc * yc, segm) + RWKV_GN_EPS)
        o_ref[:, cols] = (yn * lnw_ref[:, cols] + lnb_ref[:, cols] + bonus_ref[:, cols]) * gate_ref[:, cols]


def _rwkv(p, prm, ln_w, ln_b, latent, s0=None, dst=None):
    mu, k_k, k_a, r_k, w0, w_up, a0, a_up, g_up = prm
    nseq, t, rb0 = (DEC_BATCH, DEC_SEQ, M_CTX // DEC_SEQ) if latent else (BATCH, SEQ, 0)
    npair = H_B // 2
    lw = LORA_W + LORA_A + LORA_G
    c0 = (IN_AB - 3 * MIX - lw) // MIX
    cl = (IN_AB - lw) // lw
    full = lambda *shape: pl.BlockSpec(shape, lambda b: (0,) * len(shape))
    st_spec = pl.BlockSpec((1, 2, H_B, HD_B, HD_B), lambda b: (b, 0, 0, 0, 0))
    o_spec = pl.BlockSpec((t, MIX), lambda b: (rb0 + b, 0))
    o_shape = jax.ShapeDtypeStruct((M_ALL, MIX), F32)
    in_specs = [pl.BlockSpec((t, MIX), lambda b: (rb0 + b, c0)),
                pl.BlockSpec((t, MIX), lambda b: (rb0 + b, c0 + 1)),
                pl.BlockSpec((t, MIX), lambda b: (rb0 + b, c0 + 2)),
                pl.BlockSpec((t, lw), lambda b: (rb0 + b, cl)),
                full(1, MIX), full(1, MIX), full(1, MIX), full(1, lw),
                full(1, MIX), full(1, MIX), full(1, MIX),
                full(2, MIX), full(2, LORA_W, MIX), full(1, MIX), full(LORA_A, MIX), full(LORA_G, MIX),
                full(1, MIX), full(1, MIX)]
    args = [p, p, p, p,
            mu[None, 0:MIX], mu[None, MIX:2 * MIX], mu[None, 2 * MIX:3 * MIX], mu[None, 3 * MIX:],
            k_k.reshape(1, MIX), k_a.reshape(1, MIX), r_k.reshape(1, MIX), w0, w_up,
            a0.reshape(1, MIX), a_up, g_up, ln_w.reshape(1, MIX), ln_b.reshape(1, MIX)]
    if latent:
        args += [s0, dst]
        in_specs += [st_spec, pl.BlockSpec(memory_space=pl.ANY)]
        out_specs, out_shape, aliases = o_spec, o_shape, {len(args) - 1: 0}
    else:
        out_specs = [o_spec, st_spec]
        out_shape = [o_shape, jax.ShapeDtypeStruct((nseq, 2, H_B, HD_B, HD_B), F32)]
        aliases = {}
    return pl.pallas_call(
        functools.partial(_rwkv_scan_kernel, t=t, c=RWKV_CHUNK, npair=npair, latent=latent),
        grid=(nseq,),
        in_specs=in_specs, out_specs=out_specs, out_shape=out_shape,
        scratch_shapes=[pltpu.VMEM((t, MIX), F32)] * (N_PREP_OUT + 1),
        input_output_aliases=aliases,
        compiler_params=_params("parallel"),
        name="rwkv_lat" if latent else "rwkv_ctx",
    )(*args)


S5_L = 16
S5_ROWS = M_ALL // S5_L
S5_CW = S5_L * S5_GROUP
S5_GS = 8
S5_PW = 2 * P_C
S5_TAB = 24
S5_PRM = 2 * S5_GROUP + 8
S5_NC_CTX = SEQ // S5_L
S5_NC_LAT = DEC_SEQ // S5_L
S5_CTX_ROWS = BATCH * S5_NC_CTX


def _gelu_tanh(x):
    return 0.5 * x * (1.0 + jnp.tanh(math.sqrt(2.0 / math.pi) * (x + 0.044715 * (x * x * x))))


def _cmul(ar, ai, br, bi):
    return ar * br - ai * bi, ar * bi + ai * br


def _dot3(a, b, dims):
    ah, al = _split2(a)
    bh, bl = _split2(b)
    d = lambda x, y: lax.dot_general(x, y, dims, preferred_element_type=F32)
    return d(ah, bh) + d(ah, bl) + d(al, bh)


def _s5_param_kernel(prm_ref, tab_ref, r_ref, st_ref):
    nl, ns = S5_L, S5_GROUP
    lane_blk = lax.broadcasted_iota(jnp.int32, (ns, S5_CW), 1) // ns
    t_rows = [jnp.zeros((ns, S5_CW), F32) for _ in range(nl)]
    e_parts = []
    for d in range(2):
        part = lambda lo, n: (prm_ref[0, d, 0, lo:lo + n, :], prm_ref[0, d, 1, lo:lo + n, :])
        bb = _cmul(*part(2 * ns, 1), *part(0, ns))
        cc = part(ns, ns)
        pw = lambda k: (tab_ref[0, d, 0, k:k + 1, :], tab_ref[0, d, 1, k:k + 1, :])
        steps = range(nl)
        if d == 0:
            ke, ks, kk = [nl - 1 - j for j in steps], [j + 1 for j in steps], list(steps)
        else:
            ke, ks, kk = list(steps), [nl - j for j in steps], [nl - 1 - j for j in steps]
        stack = lambda xs: (jnp.concatenate([x[0] for x in xs], axis=0), jnp.concatenate([x[1] for x in xs], axis=0))
        e_re, e_im = stack([_cmul(*bb, *pw(k)) for k in ke])
        s_re, s_im = stack([_cmul(*cc, *pw(k)) for k in ks])
        k_re, k_im = stack([_cmul(*cc, *pw(k)) for k in kk])
        st_ref[0, d] = jnp.concatenate([s_re, -s_im], axis=1).astype(BF16)
        krow = _dot3(bb[0], k_re, NT) - _dot3(bb[1], k_im, NT)
        for j in steps:
            if d == 0:
                shifted, keep = pltpu.roll(krow, ns * j, axis=1), lane_blk >= j
            else:
                shifted, keep = pltpu.roll(krow, (S5_CW - ns * (nl - 1 - j)) % S5_CW, axis=1), lane_blk <= j
            t_rows[j] = t_rows[j] + jnp.where(keep, shifted, 0.0)
        e_parts += [e_re, e_im]
    r_ref[0] = jnp.concatenate([jnp.concatenate(t_rows, axis=0)] + e_parts, axis=1).astype(BF16)


def _s5_params(prm, tab):
    return pl.pallas_call(
        _s5_param_kernel,
        grid=(G_C,),
        in_specs=[pl.BlockSpec((1, 2, 2, S5_PRM, P_C), lambda g: (g, 0, 0, 0, 0)),
                  pl.BlockSpec((1, 2, 2, S5_TAB, P_C), lambda g: (g, 0, 0, 0, 0))],
        out_specs=[pl.BlockSpec((1, S5_CW, 2 * S5_CW), lambda g: (g, 0, 0)),
                   pl.BlockSpec((1, 2, S5_CW, S5_PW), lambda g: (g, 0, 0, 0))],
        out_shape=[jax.ShapeDtypeStruct((G_C, S5_CW, 2 * S5_CW), BF16),
                   jax.ShapeDtypeStruct((G_C, 2, S5_CW, S5_PW), BF16)],
        compiler_params=_params("parallel"),
        name="s5_params",
    )(prm, tab)


def _s5_core_kernel(u_ref, r_ref, st_ref, tab_ref, h0_ref, o_ref, hfin_ref, ug_scr, yg_scr, e_scr, hp_scr):
    ns, nl = S5_GROUP, S5_L
    per_v = 128 // ns
    blk = lax.broadcasted_iota(jnp.int32, (8, 128), 1) // ns

    def merge(select):
        acc = select(0)
        for b in range(1, per_v):
            acc = jnp.where(blk == b, select(b), acc)
        return acc

    def shuffle(srcs):
        rolled = []
        for s in range(per_v):
            m = merge(lambda b: srcs[(b + s) % per_v])
            rolled.append(m if s == 0 else pltpu.roll(m, s * ns, axis=1))
        return [merge(lambda b: rolled[(b - a) % per_v]) for a in range(per_v)]

    def tiles(regroup):
        def ctx_tile(ti, _):
            c, b0 = ti // (BATCH // 8), (ti % (BATCH // 8)) * 8
            regroup(b0 * SEQ + c * nl, SEQ, pl.multiple_of(c * BATCH + b0, 8))
            return 0

        def lat_tile(ti, _):
            b, c0 = ti // (S5_NC_LAT // 8), (ti % (S5_NC_LAT // 8)) * 8
            regroup(M_CTX + b * DEC_SEQ + c0 * nl, nl, pl.multiple_of(S5_CTX_ROWS + b * S5_NC_LAT + c0, 8))
            return 0

        lax.fori_loop(0, S5_CTX_ROWS // 8, ctx_tile, 0, unroll=4)
        lax.fori_loop(0, (S5_ROWS - S5_CTX_ROWS) // 8, lat_tile, 0, unroll=4)

    def regroup_in(tok, stride, row):
        uj = [u_ref[pl.ds(tok + j, 8, stride=stride), :] for j in range(nl)]
        for jh in range(nl // per_v):
            for g, v in enumerate(shuffle(uj[jh * per_v:(jh + 1) * per_v])):
                ug_scr[g, pl.ds(row, 8), jh * 128:(jh + 1) * 128] = v

    tiles(regroup_in)

    chains = [(g, d) for g in range(S5_GS) for d in range(2)]
    for g in range(S5_GS):
        a = jnp.dot(ug_scr[g].astype(BF16), r_ref[g], preferred_element_type=F32)
        yg_scr[g] = a[:, 0:S5_CW]
        for d in range(2):
            e_scr[g, d] = a[:, S5_CW + d * S5_PW:S5_CW + (d + 1) * S5_PW]
    lam = []
    for g, d in chains:
        lr, li = (tab_ref[g, d, ri, nl:nl + 1, :] for ri in range(2))
        lam.append((jnp.concatenate([lr, lr], axis=1), jnp.concatenate([-li, li], axis=1)))

    def advance(c, hs, latent):
        nc = S5_NC_LAT if latent else S5_NC_CTX
        out = []
        for i, (g, d) in enumerate(chains):
            cc = c if d == 0 else nc - 1 - c
            rows = (pl.ds(S5_CTX_ROWS + cc, DEC_BATCH, stride=nc) if latent
                    else pl.ds(pl.multiple_of(cc * BATCH, BATCH), BATCH))
            hp_scr[g, d, rows, :] = hs[i]
            out.append(lam[i][0] * hs[i] + lam[i][1] * pltpu.roll(hs[i], P_C, axis=1) + e_scr[g, d, rows, :])
        return out

    def both(c, carry):
        hc, hl = carry
        return tuple(advance(c, hc, False)), tuple(advance(c, hl, True))

    h_ctx = tuple(jnp.zeros((BATCH, S5_PW), F32) for _ in chains)
    h_lat = tuple(h0_ref[g, d] for g, d in chains)
    h_ctx, h_lat = lax.fori_loop(0, S5_NC_CTX, both, (h_ctx, h_lat))
    lax.fori_loop(S5_NC_CTX, S5_NC_LAT, lambda c, hl: tuple(advance(c, hl, True)), h_lat)
    for i, (g, d) in enumerate(chains):
        hfin_ref[g, d] = h_ctx[i]
    for g in range(S5_GS):
        y = yg_scr[g]
        for d in range(2):
            y = y + lax.dot_general(hp_scr[g, d].astype(BF16), st_ref[g, d], NT, preferred_element_type=F32)
        yg_scr[g] = y

    def regroup_out(tok, stride, row):
        for jh in range(nl // per_v):
            yv = [yg_scr[g, pl.ds(row, 8), jh * 128:(jh + 1) * 128] for g in range(S5_GS)]
            for jj, v in enumerate(shuffle(yv)):
                o_ref[pl.ds(tok + jh * per_v + jj, 8, stride=stride), :] = v

    tiles(regroup_out)


def _s5_core(p2, r, st, tab, h0):
    blk = lambda *tail: pl.BlockSpec((S5_GS,) + tail, lambda s: (s,) + (0,) * len(tail))
    col = pl.BlockSpec((M_ALL, S5_GS * S5_GROUP), lambda s: (0, s))
    return pl.pallas_call(
        _s5_core_kernel,
        grid=(G_C // S5_GS,),
        in_specs=[col, blk(S5_CW, 2 * S5_CW), blk(2, S5_CW, S5_PW),
                  blk(2, 2, S5_TAB, P_C), blk(2, DEC_BATCH, S5_PW)],
        out_specs=[col, blk(2, BATCH, S5_PW)],
        out_shape=[jax.ShapeDtypeStruct((M_ALL, MIX), F32),
                   jax.ShapeDtypeStruct((G_C, 2, BATCH, S5_PW), F32)],
        scratch_shapes=[pltpu.VMEM((S5_GS, S5_ROWS, S5_CW), F32),
                        pltpu.VMEM((S5_GS, S5_ROWS, S5_CW), F32),
                        pltpu.VMEM((S5_GS, 2, S5_ROWS, S5_PW), F32),
                        pltpu.VMEM((S5_GS, 2, S5_ROWS, S5_PW), F32)],
        compiler_params=_params("parallel"),
        name="s5_core",
    )(p2, r, st, tab, h0)


def _s5_out_kernel(y_ref, u_ref, d_ref, w_ref, o_ref):
    z = _gelu_tanh(y_ref[...] + d_ref[...] * u_ref[...])
    o_ref[...] = z * _sigmoid(jnp.dot(z.astype(BF16), w_ref[...], preferred_element_type=F32))


def _s5_out(y_tok, p2, d_skip, w_glu_bf16):
    tm = 512
    return pl.pallas_call(
        _s5_out_kernel,
        grid=(M_ALL // tm,),
        in_specs=[pl.BlockSpec((tm, MIX), lambda i: (i, 0)),
                  pl.BlockSpec((tm, MIX), lambda i: (i, 0)),
                  pl.BlockSpec((1, MIX), lambda i: (0, 0)),
                  pl.BlockSpec((MIX, MIX), lambda i: (0, 0))],
        out_specs=pl.BlockSpec((tm, MIX), lambda i: (i, 0)),
        out_shape=jax.ShapeDtypeStruct((M_ALL, MIX), F32),
        compiler_params=_params("parallel"),
        name="s5_out",
    )(y_tok, p2, d_skip.reshape(1, MIX), w_glu_bf16)


def _s5_tables(lam_re, lam_im, log_dt, b_re, b_im, c_re, c_im):
    dt = jnp.exp(log_dt)[:, None, :, None]
    k = jnp.arange(S5_TAB, dtype=F32)[None, :, None, None]
    mag = jnp.exp(lam_re[:, None] * dt * k)
    pw_re = mag * jnp.cos(lam_im[:, None] * dt * k)
    pw_im = mag * jnp.sin(lam_im[:, None] * dt * k)
    ab_re, ab_im = pw_re[:, 1], pw_im[:, 1]
    den = lam_re * lam_re + lam_im * lam_im
    nr = ab_re - 1.0
    f_re = (nr * lam_re + ab_im * lam_im) / den
    f_im = (ab_im * lam_re - nr * lam_im) / den
    rows = lambda b, c, f: jnp.concatenate(
        [jnp.swapaxes(b, -1, -2), c, jnp.broadcast_to(f[:, :, None, :], (2, G_C, 8, P_C))], axis=2)
    prm = jnp.stack([rows(b_re, c_re, f_re), rows(b_im, c_im, f_im)], axis=1)
    tab = jnp.stack([pw_re, pw_im], axis=1)
    return jnp.transpose(prm, (2, 0, 1, 3, 4)), jnp.transpose(tab, (3, 0, 1, 2, 4))


def _s5_mixer(p2, tables, d_skip, w_glu_bf16, state_re, state_im):
    prm, tab = tables
    r, st = _s5_params(prm, tab)
    h0 = jnp.concatenate([state_re, state_im], axis=-1)
    y_tok, hfin = _s5_core(p2, r, st, tab, jnp.transpose(h0, (2, 1, 0, 3)))
    oc = _s5_out(y_tok, p2, d_skip, w_glu_bf16)
    hfin = jnp.transpose(hfin, (2, 1, 0, 3))
    return oc, hfin[..., :P_C], hfin[..., P_C:]


def _ret_kernel(*refs, t, latent):
    if latent:
        (q_ref, k_ref, v_ref, g_ref, dl_ref, lnw_ref, lnb_ref, r0_ref, cos_ref, sin_ref, _,
         o_ref, ob_ref) = refs
    else:
        q_ref, k_ref, v_ref, g_ref, dl_ref, lnw_ref, lnb_ref, o_ref, rfin_ref, ob_ref = refs
    n = t // CHUNK
    w = HD_D
    jf = lax.broadcasted_iota(jnp.int32, (CHUNK, CHUNK), 0).astype(F32)
    kf = lax.broadcasted_iota(jnp.int32, (CHUNK, CHUNK), 1).astype(F32)
    diff = jf - kf
    tabs = []
    for dr in range(2):
        for h in range(H_D):
            l = -_softplus(-dl_ref[h, dr:dr + 1, :])
            if dr == 0:
                dmat = jnp.where(diff >= 0, jnp.exp(l * jnp.maximum(diff, 0.0)), 0.0)
                xi = jnp.exp(l * (jf + 1.0))
                zeta = jnp.exp(l * (CHUNK - 1.0 - jf))
            else:
                dmat = jnp.where(diff < 0, jnp.exp(l * jnp.maximum(-diff, 0.0)), 0.0)
                xi = jnp.exp(l * (CHUNK - jf))
                zeta = jnp.exp(l * jf)
            tabs.append((dmat, xi, zeta, jnp.exp(l * CHUNK)))
    chains = [(dr, h) for dr in range(2) for h in range(H_D)]

    def body(ci, states):
        q, k, v, where = [], [], [], []
        for dr, h in chains:
            cj = ci if dr == 0 else n - 1 - ci
            rows = pl.ds(pl.multiple_of(cj * CHUNK, CHUNK), CHUNK)
            cols = slice(h * w, (h + 1) * w)
            qi = q_ref[rows, cols]
            ki = k_ref[rows, cols] * (HD_D ** -0.5)
            if latent:
                qi = _rope_pairs(qi, cos_ref[rows, :], sin_ref[rows, :], HD_D // 2)
                ki = _rope_pairs(ki, cos_ref[rows, :], sin_ref[rows, :], HD_D // 2)
            q.append(qi.astype(BF16))
            k.append(ki)
            v.append(v_ref[rows, cols].astype(BF16))
            where.append((o_ref if dr == 0 else ob_ref, rows, cols))
        nch = range(len(chains))
        inner = [(_bdot(q[i], k[i], NT) * tabs[i][0]).astype(BF16) for i in nch]
        cross = [_bdot(q[i], states[i]) * tabs[i][1] for i in nch]
        kz = [(k[i] * tabs[i][2]).astype(BF16) for i in nch]
        out = [jnp.dot(inner[i], v[i], preferred_element_type=F32) + cross[i] for i in nch]
        new_states = [states[i] * tabs[i][3] + lax.dot_general(kz[i], v[i], TN, preferred_element_type=F32)
                      for i in nch]
        for (dst, rows, cols), y in zip(where, out):
            dst[rows, cols] = y
        return tuple(new_states)

    if latent:
        lax.fori_loop(0, n, body, tuple(r0_ref[0, dr, h] for dr, h in chains), unroll=2)
    else:
        fin = lax.fori_loop(0, n, body, tuple(jnp.zeros((w, w), F32) for _ in chains), unroll=True)
        for i, (dr, h) in enumerate(chains):
            rfin_ref[0, dr, h] = fin[i]
    for h in range(H_D):
        cols = slice(h * w, (h + 1) * w)
        y = o_ref[:, cols] + ob_ref[:, cols]
        yc = y - jnp.mean(y, axis=-1, keepdims=True)
        yn = yc * lax.rsqrt(jnp.mean(yc * yc, axis=-1, keepdims=True) + RET_GN_EPS)
        o_ref[:, cols] = (yn * lnw_ref[:, cols] + lnb_ref[:, cols]) * _silu(g_ref[:, cols])


def _retention(p, decay_logit, ln_w, ln_b, latent, r0=None, rope=None, dst=None):
    if latent:
        nseq, t, rb0 = DEC_BATCH, DEC_SEQ, M_CTX // DEC_SEQ
    else:
        nseq, t, rb0 = BATCH, SEQ, 0
    w = HD_D
    dl = jnp.broadcast_to(decay_logit.T[:, :, None], (H_D, 2, w))
    full = lambda shape: pl.BlockSpec(shape, lambda b: (0,) * len(shape))
    st_spec = pl.BlockSpec((1, 2, H_D, w, w), lambda b: (b, 0, 0, 0, 0))
    in_specs = [pl.BlockSpec((t, MIX), lambda b, j=j: (rb0 + b, j)) for j in (1, 2, 3, 4)]
    in_specs += [full((H_D, 2, w)), full((1, MIX)), full((1, MIX))]
    args = [p, p, p, p, dl, ln_w.reshape(1, MIX), ln_b.reshape(1, MIX)]
    o_spec = pl.BlockSpec((t, MIX), lambda b: (rb0 + b, 0))
    o_shape = jax.ShapeDtypeStruct((M_ALL, MIX), F32)
    if latent:
        in_specs += [st_spec, full((t, w)), full((t, w)), pl.BlockSpec(memory_space=pl.ANY)]
        args += [r0, *rope, dst]
        out_specs, out_shape, aliases = o_spec, o_shape, {len(args) - 1: 0}
    else:
        out_specs = [o_spec, st_spec]
        out_shape = [o_shape, jax.ShapeDtypeStruct((nseq, 2, H_D, w, w), F32)]
        aliases = {}
    return pl.pallas_call(
        functools.partial(_ret_kernel, t=t, latent=latent),
        grid=(nseq,),
        in_specs=in_specs, out_specs=out_specs, out_shape=out_shape,
        scratch_shapes=[pltpu.VMEM((t, MIX), F32)],
        input_output_aliases=aliases,
        compiler_params=_params("parallel"),
        name="retention_lat" if latent else "retention_ctx",
    )(*args)


def _rope_tables(n_tok, dim, reps):
    rows = n_tok // GRID_W
    n_freq = dim // 4
    inv = 1.0 / (ROPE_THETA ** (jnp.arange(n_freq, dtype=F32) / n_freq))
    row = jnp.repeat(jnp.arange(rows, dtype=F32), GRID_W)
    col = jnp.tile(jnp.arange(GRID_W, dtype=F32), rows)
    ang = jnp.concatenate([row[:, None] * inv, col[:, None] * inv], axis=-1)
    cos, sin = jnp.cos(ang), jnp.sin(ang)
    return jnp.tile(jnp.concatenate([cos, cos], axis=1), (1, reps)), \
        jnp.tile(jnp.concatenate([-sin, sin], axis=1), (1, reps))


def kernel(x_prompt, x_sample, cache_k_ab, cache_v_ab, state_rwkv, state_s5_re, state_s5_im, state_ret, c, c_ctx, norm1_g, norm2_g, w_mod, b_mod, w_ff_gate, w_ff_up, w_ff_down, w_in_ab, w_out_ab, qk_gain_a, lambda_qk, subln_g, rwkv_mu, rwkv_k_k, rwkv_k_a, rwkv_r_k, rwkv_w0, rwkv_w_up, rwkv_a0, rwkv_a_up, rwkv_g_up, rwkv_ln_w, rwkv_ln_b, w_in_cd, w_out_cd, s5_lam_re, s5_lam_im, s5_log_dt, s5_b_re, s5_b_im, s5_c_re, s5_c_im, s5_d, s5_w_glu, ret_decay_logit, ret_ln_w, ret_ln_b):
    d = D_MODEL
    xs = [x_prompt.reshape(M_CTX, d), x_sample.reshape(M_LAT, d)]
    cvec = jnp.zeros((MOD_ROWS, d), F32).at[0].set(c_ctx).at[1:1 + DEC_BATCH].set(c)
    mods = _modulation(cvec, w_mod, b_mod)

    lam_init = 0.8 - 0.6 * math.exp(-0.3 * 0)
    p = _norm_linear(xs, norm1_g[0], mods[0], w_in_ab[0].astype(BF16))
    gain2 = jnp.tile(qk_gain_a[0], (1, 2))
    rope_a = _rope_tables(DEC_SEQ, HD_A, 2)
    ck = cache_k_ab[:, 0].reshape(DEC_BATCH * PAST_LEN, MIX)
    cv = cache_v_ab[:, 0].reshape(DEC_BATCH * PAST_LEN, MIX)
    oa, k_ctx, v_ctx = _attention(p, gain2, lambda_qk[0], subln_g[0], lam_init, latent=False)
    oa = _attention(p, gain2, lambda_qk[0], subln_g[0], lam_init, latent=True,
                    cache_k=ck, cache_v=cv, rope=rope_a, dst=oa)
    rw_prm = (rwkv_mu[0], rwkv_k_k[0], rwkv_k_a[0], rwkv_r_k[0], rwkv_w0[0], rwkv_w_up[0],
              rwkv_a0[0], rwkv_a_up[0], rwkv_g_up[0])
    ob, sfin_ctx = _rwkv(p, rw_prm, rwkv_ln_w[0], rwkv_ln_b[0], latent=False)
    ob = _rwkv(p, rw_prm, rwkv_ln_w[0], rwkv_ln_b[0], latent=True, s0=state_rwkv[:, 0], dst=ob)
    x = _mix_ffn(xs, oa, ob, w_out_ab[0].astype(BF16), norm2_g[0], mods[0], w_ff_gate[0].astype(BF16),
                 w_ff_up[0].astype(BF16), w_ff_down[0].astype(BF16), split_out=False)

    p2 = _norm_linear([x], norm1_g[1], mods[1], w_in_cd[0].astype(BF16))
    tables = _s5_tables(s5_lam_re[0], s5_lam_im[0], s5_log_dt[0], s5_b_re[0], s5_b_im[0],
                        s5_c_re[0], s5_c_im[0])
    oc, s5_fin_re, s5_fin_im = _s5_mixer(p2, tables, s5_d[0], s5_w_glu[0].astype(BF16),
                                         state_s5_re[:, 0], state_s5_im[:, 0])
    rope_d = _rope_tables(DEC_SEQ, HD_D, 1)
    od, rfin = _retention(p2, ret_decay_logit[0], ret_ln_w[0], ret_ln_b[0], latent=False)
    od = _retention(p2, ret_decay_logit[0], ret_ln_w[0], ret_ln_b[0], latent=True, r0=state_ret[:, 0],
                    rope=rope_d, dst=od)
    y_ctx, y_lat = _mix_ffn([x], oc, od, w_out_cd[0].astype(BF16), norm2_g[1], mods[1],
                            w_ff_gate[1].astype(BF16), w_ff_up[1].astype(BF16), w_ff_down[1].astype(BF16),
                            split_out=True)
    y_prompt = y_ctx.reshape(BATCH, SEQ, d)
    y_sample = y_lat.reshape(DEC_BATCH, DEC_SEQ, d)
    new_k = jnp.transpose(k_ctx, (0, 4, 1, 2, 3))[:, None]
    new_v = v_ctx.reshape(BATCH, 1, SEQ, H_A, VD_A)
    new_rwkv = sfin_ctx[:, None]
    new_s5_re = s5_fin_re[:, None]
    new_s5_im = s5_fin_im[:, None]
    new_ret = rfin[:, None]
    return (y_prompt, y_sample, new_k, new_v, new_rwkv, new_s5_re, new_s5_im, new_ret)
```

```python
import functools
import math

import numpy as np
import jax
import jax.numpy as jnp
from jax import lax
from jax.experimental import pallas as pl
from jax.experimental.pallas import tpu as pltpu

F32 = jnp.float32
BF16 = jnp.bfloat16
HIGHEST = lax.Precision.HIGHEST

D_MODEL = 1024
BATCH = 32
SEQ = 256
DEC_BATCH = 2
DEC_SEQ = 1024
PAST_LEN = 256
GRID_W = 64
H_A = 4
HD_A = 64
VD_A = 128
H_B = 8
HD_B = 64
MIX = 512
LORA_W = 64
LORA_A = 64
LORA_G = 128
S5_GROUP = 16
G_C = 32
P_C = 64
S5_STATE = G_C * P_C
H_D = 4
HD_D = 128
CHUNK = 128
D_FF = 2816
IN_AB = 3328
IN_CD = 2560
ROPE_THETA = 10000.0
NORM_EPS = 1e-6
RWKV_GN_EPS = 64e-5
RET_GN_EPS = 1e-5

M_CTX = BATCH * SEQ
M_LAT = DEC_BATCH * DEC_SEQ
M_ALL = M_CTX + M_LAT
MOD_ROWS = 8
RWKV_CHUNK = 64
VMEM_LIMIT = 56 * 1024 * 1024

NN = (((1,), (0,)), ((), ()))
NT = (((1,), (1,)), ((), ()))
TN = (((0,), (0,)), ((), ()))


def _params(*sem):
    return pltpu.CompilerParams(dimension_semantics=sem, vmem_limit_bytes=VMEM_LIMIT)


def _bdot(a, b, dims=NN):
    return lax.dot_general(a.astype(BF16), b.astype(BF16), dims, preferred_element_type=F32)


def _hdot(a, b, dims=NN):
    return lax.dot_general(a, b, dims, precision=HIGHEST, preferred_element_type=F32)


def _split_dot(x, m):
    hi = x.astype(BF16)
    lo = (x - hi.astype(F32)).astype(BF16)
    return (jnp.dot(hi, m, preferred_element_type=F32) + jnp.dot(lo, m, preferred_element_type=F32))


def _seg_matrix(n, shift, val):
    r = lax.broadcasted_iota(jnp.int32, (n, n), 0) >> shift
    c = lax.broadcasted_iota(jnp.int32, (n, n), 1) >> shift
    return jnp.where(r == c, val, 0.0).astype(BF16)


def _sigmoid(x):
    return jax.nn.sigmoid(x)


def _silu(x):
    return x * jax.nn.sigmoid(x)


def _softplus(x):
    return jnp.maximum(x, 0.0) + jnp.log(1.0 + jnp.exp(-jnp.abs(x)))


def _mod_row(tile, tm):
    r0 = tile * tm
    return jnp.where(r0 < M_CTX, 0, 1 + (r0 - M_CTX) // DEC_SEQ)


def _norm_mod(x, g, sc_ref, sh_ref, row):
    y = x * lax.rsqrt(jnp.mean(x * x, axis=-1, keepdims=True) + NORM_EPS) * g
    return y * (1.0 + sc_ref[pl.ds(row, 1), :]) + sh_ref[pl.ds(row, 1), :]


def _mod_kernel(c_ref, w_ref, b_ref, o_ref):
    o_ref[0] = _bdot(_silu(c_ref[...]), w_ref[0]) + b_ref[0]


def _modulation(cvec, w_mod, b_mod):
    depth, d, n6 = w_mod.shape
    tn = 1536
    return pl.pallas_call(
        _mod_kernel,
        grid=(depth, n6 // tn),
        in_specs=[pl.BlockSpec((MOD_ROWS, d), lambda l, j: (0, 0)),
                  pl.BlockSpec((1, d, tn), lambda l, j: (l, 0, j)),
                  pl.BlockSpec((1, 1, tn), lambda l, j: (l, 0, j))],
        out_specs=pl.BlockSpec((1, MOD_ROWS, tn), lambda l, j: (l, 0, j)),
        out_shape=jax.ShapeDtypeStruct((depth, MOD_ROWS, n6), F32),
        compiler_params=_params("parallel", "parallel"),
        name="modulation",
    )(cvec, w_mod, b_mod.reshape(depth, 1, n6))


def _tile_specs(tm, d):
    nc = M_CTX // tm
    return [pl.BlockSpec((tm, d), lambda i: (jnp.minimum(i, nc - 1), 0)),
            pl.BlockSpec((tm, d), lambda i: (jnp.maximum(i - nc, 0), 0))]


def _pick_rows(refs, tm):
    if len(refs) == 1:
        return refs[0][...]
    return jnp.where(pl.program_id(0) < M_CTX // tm, refs[0][...], refs[1][...])


def _norm_linear_kernel(*refs, tm):
    *x_refs, g_ref, sc_ref, sh_ref, w_ref, o_ref = refs
    row = _mod_row(pl.program_id(0), tm)
    x = _pick_rows(x_refs, tm)
    nsub = 2
    rows = [slice(i * tm // nsub, (i + 1) * tm // nsub) for i in range(nsub)]
    hs = [_norm_mod(x[r], g_ref[...], sc_ref, sh_ref, row).astype(BF16) for r in rows]
    for r, h in zip(rows, hs):
        o_ref[r, :] = jnp.dot(h, w_ref[...], preferred_element_type=F32)


def _norm_linear(xs, g, mods, w_bf16):
    tm = 512
    d = D_MODEL
    n = w_bf16.shape[1]
    x_specs = _tile_specs(tm, d) if len(xs) == 2 else [pl.BlockSpec((tm, d), lambda i: (i, 0))]
    return pl.pallas_call(
        functools.partial(_norm_linear_kernel, tm=tm),
        grid=(M_ALL // tm,),
        in_specs=x_specs + [pl.BlockSpec((1, d), lambda i: (0, 0)),
                            pl.BlockSpec((MOD_ROWS, d), lambda i: (0, 1)),
                            pl.BlockSpec((MOD_ROWS, d), lambda i: (0, 0)),
                            pl.BlockSpec((d, n), lambda i: (0, 0))],
        out_specs=pl.BlockSpec((tm, n), lambda i: (i, 0)),
        out_shape=jax.ShapeDtypeStruct((M_ALL, n), F32),
        compiler_params=_params("arbitrary"),
        name="norm_linear",
    )(*xs, g.reshape(1, d), mods, mods, w_bf16)


def _mix_ffn_kernel(*refs, tm, ck, n_in, n_out):
    x_refs = refs[:n_in]
    (a_ref, b_ref, wa_ref, wb_ref, g1_ref, ng_ref, sc_ref, sh_ref, g2_ref,
     wg_ref, wu_ref, wd_ref) = refs[n_in:n_in + 12]
    o_refs = refs[n_in + 12:]
    row = _mod_row(pl.program_id(0), tm)
    mix = (jnp.dot(a_ref[...].astype(BF16), wa_ref[...], preferred_element_type=F32)
           + jnp.dot(b_ref[...].astype(BF16), wb_ref[...], preferred_element_type=F32))
    x = _pick_rows(x_refs, tm) + g1_ref[pl.ds(row, 1), :] * mix
    h = _norm_mod(x, ng_ref[...], sc_ref, sh_ref, row).astype(BF16)
    acc = jnp.zeros((tm, D_MODEL), F32)
    for c in range(D_FF // ck):
        gg = jnp.dot(h, wg_ref[:, c * ck:(c + 1) * ck], preferred_element_type=F32)
        uu = jnp.dot(h, wu_ref[:, c * ck:(c + 1) * ck], preferred_element_type=F32)
        act = (_silu(gg) * uu).astype(BF16)
        acc = acc + jnp.dot(act, wd_ref[c * ck:(c + 1) * ck, :], preferred_element_type=F32)
    y = x + g2_ref[pl.ds(row, 1), :] * acc
    if n_out == 1:
        o_refs[0][...] = y
    else:
        is_ctx = pl.program_id(0) < M_CTX // tm

        @pl.when(is_ctx)
        def _():
            o_refs[0][...] = y

        @pl.when(jnp.logical_not(is_ctx))
        def _():
            o_refs[1][...] = y


def _mix_ffn(xs, oa, ob, w_out_bf16, norm_g, mods, wg, wu, wd, split_out):
    tm, ck = 512, 256
    d = D_MODEL
    row_spec = lambda n: pl.BlockSpec((tm, n), lambda i: (i, 0))
    const = lambda shape, idx: pl.BlockSpec(shape, lambda i: idx, pipeline_mode=pl.Buffered(1))
    mod = lambda j: pl.BlockSpec((MOD_ROWS, d), lambda i: (0, j))
    x_specs = _tile_specs(tm, d) if len(xs) == 2 else [row_spec(d)]
    if split_out:
        out_specs = _tile_specs(tm, d)
        out_shape = [jax.ShapeDtypeStruct((M_CTX, d), F32), jax.ShapeDtypeStruct((M_LAT, d), F32)]
    else:
        out_specs, out_shape = row_spec(d), jax.ShapeDtypeStruct((M_ALL, d), F32)
    return pl.pallas_call(
        functools.partial(_mix_ffn_kernel, tm=tm, ck=ck, n_in=len(xs), n_out=2 if split_out else 1),
        grid=(M_ALL // tm,),
        in_specs=x_specs + [row_spec(MIX), row_spec(MIX),
                            const((MIX, d), (0, 0)), const((MIX, d), (1, 0)),
                            mod(2),
                            pl.BlockSpec((1, d), lambda i: (0, 0)),
                            mod(4), mod(3), mod(5),
                            const((d, D_FF), (0, 0)), const((d, D_FF), (0, 0)), const((D_FF, d), (0, 0))],
        out_specs=out_specs, out_shape=out_shape,
        compiler_params=_params("arbitrary"),
        name="mix_ffn",
    )(*xs, oa, ob, w_out_bf16, w_out_bf16, mods, norm_g.reshape(1, d), mods, mods, mods, wg, wu, wd)


def _qk_norm(x, gain, segm):
    ms = _split_dot(x * x, segm)
    return x * lax.rsqrt(ms + NORM_EPS) * gain


def _rope_pairs(x, cosf, sinf, half):
    lane = lax.broadcasted_iota(jnp.int32, x.shape, 1)
    first = (lane & (2 * half - 1)) < half
    n = x.shape[1]
    partner = jnp.where(first, pltpu.roll(x, n - half, axis=1), pltpu.roll(x, half, axis=1))
    return x * cosf + partner * sinf


def _attn_kernel(*refs, latent, lam_init):
    if latent:
        (q_ref, k_ref, v_ref, ck_ref, cv_ref, cosq_ref, sinq_ref, cosk_ref, sink_ref,
         gain_ref, lam_ref, sub_ref, _, o_ref, kall, vall) = refs
    else:
        q_ref, k_ref, v_ref, gain_ref, lam_ref, sub_ref, o_ref, kn_ref, vo_ref, kall, vall = refs
    w = 2 * HD_A
    segm = _seg_matrix(w, 6, 1.0 / HD_A)
    gains = gain_ref[...]
    cols = [slice(h * w, (h + 1) * w) for h in range(H_A)]

    @pl.when(pl.program_id(1) == 0)
    def _():
        k = [_qk_norm(k_ref[:, c], gains[1:2], segm) for c in cols]
        if latent:
            k = [_rope_pairs(x, cosk_ref[...], sink_ref[...], HD_A // 2) for x in k]
            for c, x in zip(cols, k):
                kall[0:PAST_LEN, c] = ck_ref[:, c].astype(BF16)
                kall[PAST_LEN:, c] = x.astype(BF16)
                vall[0:PAST_LEN, c] = cv_ref[:, c].astype(BF16)
                vall[PAST_LEN:, c] = v_ref[:, c].astype(BF16)
        else:
            for h, (c, x) in enumerate(zip(cols, k)):
                xt = x.T
                kn_ref[0, h, 0] = xt[:HD_A]
                kn_ref[0, h, 1] = xt[HD_A:]
                vo_ref[pl.ds(h, v_ref.shape[0], stride=H_A), :] = v_ref[:, c]
                kall[:, c] = x.astype(BF16)
                vall[:, c] = v_ref[:, c].astype(BF16)

    q = [_qk_norm(q_ref[:, c], gains[0:1], segm) for c in cols]
    if latent:
        q = [_rope_pairs(x, cosq_ref[...], sinq_ref[...], HD_A // 2) for x in q]
    lv = lam_ref[...]
    lam = (jnp.exp(jnp.sum(lv[0:1] * lv[1:2], axis=1, keepdims=True))
           - jnp.exp(jnp.sum(lv[2:3] * lv[3:4], axis=1, keepdims=True)) + lam_init)
    scale = HD_A ** -0.5
    comp0 = lax.broadcasted_iota(jnp.int32, q[0].shape, 1) < HD_A
    qc = [jnp.where(comp0, *sel).astype(BF16) for x in q for sel in ((x, 0.0), (0.0, x))]
    s = [lax.dot_general(qc[i], kall[:, cols[i // 2]], NT, preferred_element_type=F32) * scale
         for i in range(2 * H_A)]
    e = [jnp.exp(x - jnp.max(x, axis=-1, keepdims=True)) for x in s]
    p = [x / jnp.sum(x, axis=-1, keepdims=True) for x in e]
    att = [(p[2 * h] - lam * p[2 * h + 1]).astype(BF16) for h in range(H_A)]
    o = [jnp.dot(att[h], vall[:, cols[h]], preferred_element_type=F32) for h in range(H_A)]
    o = [x * lax.rsqrt(jnp.mean(x * x, axis=-1, keepdims=True) + NORM_EPS) * sub_ref[...] for x in o]
    for c, x in zip(cols, o):
        o_ref[:, c] = x * (1.0 - lam_init)


def _attention(p, gain2, lambda_qk, subln_g, lam_init, latent, cache_k=None, cache_v=None, rope=None,
               dst=None):
    w = 2 * HD_A
    if latent:
        nseq, t, tq, rb0, s_len = DEC_BATCH, DEC_SEQ, 128, M_CTX // DEC_SEQ, PAST_LEN + DEC_SEQ
    else:
        nseq, t, tq, rb0, s_len = BATCH, SEQ, SEQ, 0, SEQ
    nq = t // tq
    qoff = rb0 * nq
    full = lambda shape: pl.BlockSpec(shape, lambda b, i: (0,) * len(shape))
    in_specs = [pl.BlockSpec((tq, MIX), lambda b, i: (qoff + b * nq + i, 0)),
                pl.BlockSpec((t, MIX), lambda b, i: (rb0 + b, 1)),
                pl.BlockSpec((t, MIX), lambda b, i: (rb0 + b, 2))]
    args = [p, p, p]
    if latent:
        cosf, sinf = rope
        in_specs += [pl.BlockSpec((PAST_LEN, MIX), lambda b, i: (b, 0)),
                     pl.BlockSpec((PAST_LEN, MIX), lambda b, i: (b, 0)),
                     pl.BlockSpec((tq, w), lambda b, i: (i, 0)),
                     pl.BlockSpec((tq, w), lambda b, i: (i, 0)),
                     full((t, w)), full((t, w))]
        args += [cache_k, cache_v, cosf, sinf, cosf, sinf]
    in_specs += [full((2, w)), full((4, HD_A)), full((1, w))]
    args += [gain2, lambda_qk, subln_g.reshape(1, w)]
    o_spec = pl.BlockSpec((tq, MIX), lambda b, i: (qoff + b * nq + i, 0))
    o_shape = jax.ShapeDtypeStruct((M_ALL, MIX), F32)
    aliases = {}
    if latent:
        out_specs, out_shape = o_spec, o_shape
        in_specs.append(pl.BlockSpec(memory_space=pl.ANY))
        args.append(dst)
        aliases = {len(args) - 1: 0}
    else:
        out_specs = [o_spec, pl.BlockSpec((1, H_A, 2, HD_A, t), lambda b, i: (b, 0, 0, 0, 0)),
                     pl.BlockSpec((t * H_A, VD_A), lambda b, i: (b, 0))]
        out_shape = [o_shape, jax.ShapeDtypeStruct((nseq, H_A, 2, HD_A, t), F32),
                     jax.ShapeDtypeStruct((nseq * t * H_A, VD_A), F32)]
    return pl.pallas_call(
        functools.partial(_attn_kernel, latent=latent, lam_init=lam_init),
        grid=(nseq, nq),
        in_specs=in_specs, out_specs=out_specs, out_shape=out_shape,
        scratch_shapes=[pltpu.VMEM((s_len, MIX), BF16), pltpu.VMEM((s_len, MIX), BF16)],
        input_output_aliases=aliases,
        compiler_params=_params("parallel", "arbitrary"),
        name="diff_attention_lat" if latent else "diff_attention_ctx",
    )(*args)


def _centred_shift(x, mu):
    t = x.shape[0]
    row = lax.broadcasted_iota(jnp.int32, x.shape, 0)
    prev = jnp.where(row == 0, 0.0, pltpu.roll(x, 1, axis=0))
    nxt = jnp.where(row == t - 1, 0.0, pltpu.roll(x, t - 1, axis=0))
    return x + (0.5 * (prev + nxt) - x) * mu


def _seg_sum(x, segm):
    return jnp.concatenate([_split_dot(x[:, j * 128:(j + 1) * 128], segm) for j in range(x.shape[1] // 128)],
                           axis=1)


def _rwkv_prep_kernel(r_ref, k_ref, v_ref, l_ref, mur_ref, muk_ref, muv_ref, mul_ref,
                      kk_ref, ka_ref, rk_ref, w0_ref, wup_ref, a0_ref, aup_ref, gup_ref,
                      ro_ref, ldf_ref, ldb_ref, kbo_ref, vbo_ref, kko_ref, ao_ref, gate_ref, bonus_ref):
    seg1 = _seg_matrix(128, 6, 1.0)
    r = _centred_shift(r_ref[...], mur_ref[...])
    kb = _centred_shift(k_ref[...], muk_ref[...])
    vb = _centred_shift(v_ref[...], muv_ref[...])
    lo = _centred_shift(l_ref[...], mul_ref[...])
    xw = lo[:, 0:LORA_W]
    xa = lo[:, LORA_W:LORA_W + LORA_A]
    xg = lo[:, LORA_W + LORA_A:]
    kk = kb * kk_ref[...]
    kk = kk * lax.rsqrt(_seg_sum(kk * kk, seg1) + 1e-12)
    a = _sigmoid(a0_ref[...] + _bdot(xa, aup_ref[...]))
    kb2 = kb * (1.0 + (a - 1.0) * ka_ref[...])
    lw = jnp.tanh(xw)
    for dr, ld_ref in enumerate((ldf_ref, ldb_ref)):
        z = w0_ref[dr:dr + 1, :] + _bdot(lw, wup_ref[dr])
        logw = -_softplus(-z) - 0.5
        ld_ref[...] = -jnp.exp(logw)
    gate_ref[...] = _bdot(_sigmoid(xg), gup_ref[...])
    bonus_ref[...] = _seg_sum(r * kb2 * rk_ref[...], seg1) * vb
    ro_ref[...] = r
    kbo_ref[...] = kb2
    vbo_ref[...] = vb
    kko_ref[...] = kk
    ao_ref[...] = a


N_PREP_IN = 16
N_PREP_OUT = 9


def _split2(x):
    hi = x.astype(BF16)
    return hi, (x - hi.astype(F32)).astype(BF16)


RWKV_INV_BLOCK = 16
RWKV_UNROLL_MAX = 4


def _rwkv_masks(c, rev):
    ti = lax.broadcasted_iota(jnp.int32, (c, c), 0)
    si = lax.broadcasted_iota(jnp.int32, (c, c), 1)
    tri = jnp.where((si >= ti) if rev else (si <= ti), 1.0, 0.0).astype(BF16)
    t4 = lax.broadcasted_iota(jnp.int32, (4 * c, 4 * c), 0)
    s4 = lax.broadcasted_iota(jnp.int32, (4 * c, 4 * c), 1)
    tm, sm = t4 & (c - 1), s4 & (c - 1)
    strict = (sm > tm) if rev else (sm < tm)
    incl = (sm >= tm) if rev else (sm <= tm)
    same_head = ((t4 // c) & 1) == ((s4 // c) & 1)
    top = t4 < 2 * c
    gmask = same_head & ((top & strict) | (~top & incl))
    t2 = lax.broadcasted_iota(jnp.int32, (2 * c, 2 * c), 0)
    s2 = lax.broadcasted_iota(jnp.int32, (2 * c, 2 * c), 1)
    same = lambda n: (t2 // n) == (s2 // n)
    levels = []
    n = RWKV_INV_BLOCK
    while n < c:
        levels.append(same(2 * n) & ~same(n))
        n *= 2
    f = lambda m: jnp.where(m, 1.0, 0.0)
    return tri, f(gmask), f(same(RWKV_INV_BLOCK)), tuple(f(m) for m in levels), f(same(c)), f(t2 == s2)


def _keep(mask01, x):
    return jnp.where(mask01 > 0.5, x, 0.0)


def _tri_inverse(a, diag_blk, levels, eye):
    n = a[0].shape[0]
    d = [_keep(diag_blk, x) for x in a]
    t = [eye + x for x in d]
    p = [_bdot(x, x) for x in d]
    for _ in range(int(math.log2(RWKV_INV_BLOCK)) - 2):
        res = [_bdot(jnp.concatenate([pi, ti], axis=0), pi) for pi, ti in zip(p, t)]
        p = [x[:n] for x in res]
        t = [ti + x[n:] for ti, x in zip(t, res)]
    t = [ti + _bdot(ti, pi) for ti, pi in zip(t, p)]
    for off in levels:
        half = [_bdot(ti, _keep(off, x)) for ti, x in zip(t, a)]
        t = [ti + _bdot(x, ti) for ti, x in zip(t, half)]
    return t


def _rwkv_pair_chunks(ins, sts, tris, gmasks, diag_blk, levels, same_head, eye):
    c, w = ins[0][0].shape
    hd = w // 2
    nch = range(len(ins))
    r, ld, kb, vb, kk, a = (list(z) for z in zip(*ins))
    split = [_split2(x) for x in ld]
    lcum = [jnp.dot(tris[i], jnp.concatenate(split[i], axis=1), preferred_element_type=F32) for i in nch]
    lcum = [x[:, :w] + x[:, w:] for x in lcum]
    ltot = [jnp.sum(x, axis=0, keepdims=True) for x in ld]
    beta = [kk[i] * a[i] for i in nch]
    eneg = [jnp.exp(-x) for x in lcum]
    abar = [-kk[i] * jnp.exp(lcum[i] - ld[i]) for i in nch]
    rbar = [r[i] * jnp.exp(lcum[i]) for i in nch]
    bt = [(beta[i] * eneg[i]).astype(BF16) for i in nch]
    kt = [(kb[i] * eneg[i]).astype(BF16) for i in nch]
    vbb = [x.astype(BF16) for x in vb]
    head0 = lax.broadcasted_iota(jnp.int32, (c, w), 1) < hd
    pick = lambda res: jnp.where(head0, res[:c], res[c:])
    arst = [_bdot(jnp.concatenate([abar[i], rbar[i]], axis=0), sts[i], NT) for i in nch]
    lhs = [jnp.concatenate([jnp.where(head0, abar[i], 0.0), jnp.where(head0, 0.0, abar[i]),
                            jnp.where(head0, rbar[i], 0.0), jnp.where(head0, 0.0, rbar[i])], axis=0) for i in nch]
    g = [_keep(gmasks[i], _bdot(lhs[i], jnp.concatenate([bt[i], bt[i], kt[i], kt[i]], axis=0), NT))
         for i in nch]
    x = [arst[i][:c] + pick(_bdot(g[i][:2 * c, 2 * c:], jnp.concatenate([vbb[i], vbb[i]], axis=0))) for i in nch]
    tinv = _tri_inverse([gi[:2 * c, :2 * c] for gi in g], diag_blk, levels, eye)
    u = [pick(_bdot(tinv[i], jnp.concatenate([x[i], x[i]], axis=0))) for i in nch]
    ub = [z.astype(BF16) for z in u]
    y = [arst[i][c:] + pick(_bdot(g[i][2 * c:], jnp.concatenate([ub[i], ub[i], vbb[i], vbb[i]], axis=0)))
         for i in nch]
    erem = [jnp.exp(ltot[i] - lcum[i]) for i in nch]
    bkh = [jnp.concatenate([beta[i] * erem[i], kb[i] * erem[i]], axis=0) for i in nch]
    st_new = [jnp.exp(ltot[i]) * sts[i]
              + _keep(same_head, _bdot(jnp.concatenate([ub[i], vbb[i]], axis=0), bkh[i], TN)) for i in nch]
    return y, st_new


def _rwkv_scan_kernel(*refs, t, c, npair, latent):
    prep_in, (lnw_ref, lnb_ref), rest = refs[:N_PREP_IN], refs[N_PREP_IN:N_PREP_IN + 2], refs[N_PREP_IN + 2:]
    if latent:
        s0_ref, _, o_ref, *scratch = rest
    else:
        o_ref, sfin_ref, *scratch = rest
    *prep_out, yb_ref = scratch
    _rwkv_prep_kernel(*prep_in, *prep_out)
    r_ref, ldf_ref, ldb_ref, kb_ref, vb_ref, kk_ref, a_ref, gate_ref, bonus_ref = prep_out
    n = t // c
    w = 2 * HD_B
    masks = [_rwkv_masks(c, rev) for rev in (False, True)]

    def body(ci, states):
        ins, tris, gmasks, dsts = [], [], [], []
        for dr, (ld_ref, dst) in enumerate(((ldf_ref, o_ref), (ldb_ref, yb_ref))):
            cj = ci if dr == 0 else n - 1 - ci
            rows = pl.ds(pl.multiple_of(cj * c, c), c)
            for p in range(npair):
                cols = slice(p * w, (p + 1) * w)
                ins.append(tuple(ref[rows, cols] for ref in (r_ref, ld_ref, kb_ref, vb_ref, kk_ref, a_ref)))
                tris.append(masks[dr][0])
                gmasks.append(masks[dr][1])
                dsts.append((dst, rows, cols))
        ys, new_states = _rwkv_pair_chunks(ins, list(states), tris, gmasks, *masks[0][2:])
        for (dst, rows, cols), y in zip(dsts, ys):
            dst[rows, cols] = y
        return tuple(new_states)

    if latent:
        zero = jnp.zeros((HD_B, HD_B), F32)
        init = tuple(jnp.concatenate([jnp.concatenate([s0_ref[0, dr, 2 * p], zero], axis=1),
                                      jnp.concatenate([zero, s0_ref[0, dr, 2 * p + 1]], axis=1)], axis=0)
                     for dr in range(2) for p in range(npair))
    else:
        init = tuple(jnp.zeros((w, w), F32) for _ in range(2 * npair))
    fin = lax.fori_loop(0, n, body, init, unroll=True if n <= RWKV_UNROLL_MAX else 2)
    segm = _seg_matrix(w, 6, 1.0 / HD_B)
    for p in range(npair):
        if not latent:
            for dr in range(2):
                st = fin[dr * npair + p]
                sfin_ref[0, dr, 2 * p] = st[:HD_B, :HD_B]
                sfin_ref[0, dr, 2 * p + 1] = st[HD_B:, HD_B:]
        cols = slice(p * w, (p + 1) * w)
        y = o_ref[:, cols] + yb_ref[:, cols]
        yc = y - _split_dot(y, segm)
        yn = yc * lax.rsqrt(_split_dot(yc * yc, segm) + RWKV_GN_EPS)
        o_ref[:, cols] = (yn * lnw_ref[:, cols] + lnb_ref[:, cols] + bonus_ref[:, cols]) * gate_ref[:, cols]


def _rwkv(p, prm, ln_w, ln_b, latent, s0=None, dst=None):
    mu, k_k, k_a, r_k, w0, w_up, a0, a_up, g_up = prm
    nseq, t, rb0 = (DEC_BATCH, DEC_SEQ, M_CTX // DEC_SEQ) if latent else (BATCH, SEQ, 0)
    npair = H_B // 2
    lw = LORA_W + LORA_A + LORA_G
    c0 = (IN_AB - 3 * MIX - lw) // MIX
    cl = (IN_AB - lw) // lw
    full = lambda *shape: pl.BlockSpec(shape, lambda b: (0,) * len(shape))
    st_spec = pl.BlockSpec((1, 2, H_B, HD_B, HD_B), lambda b: (b, 0, 0, 0, 0))
    o_spec = pl.BlockSpec((t, MIX), lambda b: (rb0 + b, 0))
    o_shape = jax.ShapeDtypeStruct((M_ALL, MIX), F32)
    in_specs = [pl.BlockSpec((t, MIX), lambda b: (rb0 + b, c0)),
                pl.BlockSpec((t, MIX), lambda b: (rb0 + b, c0 + 1)),
                pl.BlockSpec((t, MIX), lambda b: (rb0 + b, c0 + 2)),
                pl.BlockSpec((t, lw), lambda b: (rb0 + b, cl)),
                full(1, MIX), full(1, MIX), full(1, MIX), full(1, lw),
                full(1, MIX), full(1, MIX), full(1, MIX),
                full(2, MIX), full(2, LORA_W, MIX), full(1, MIX), full(LORA_A, MIX), full(LORA_G, MIX),
                full(1, MIX), full(1, MIX)]
    args = [p, p, p, p,
            mu[None, 0:MIX], mu[None, MIX:2 * MIX], mu[None, 2 * MIX:3 * MIX], mu[None, 3 * MIX:],
            k_k.reshape(1, MIX), k_a.reshape(1, MIX), r_k.reshape(1, MIX), w0, w_up,
            a0.reshape(1, MIX), a_up, g_up, ln_w.reshape(1, MIX), ln_b.reshape(1, MIX)]
    if latent:
        args += [s0, dst]
        in_specs += [st_spec, pl.BlockSpec(memory_space=pl.ANY)]
        out_specs, out_shape, aliases = o_spec, o_shape, {len(args) - 1: 0}
    else:
        out_specs = [o_spec, st_spec]
        out_shape = [o_shape, jax.ShapeDtypeStruct((nseq, 2, H_B, HD_B, HD_B), F32)]
        aliases = {}
    return pl.pallas_call(
        functools.partial(_rwkv_scan_kernel, t=t, c=RWKV_CHUNK, npair=npair, latent=latent),
        grid=(nseq,),
        in_specs=in_specs, out_specs=out_specs, out_shape=out_shape,
        scratch_shapes=[pltpu.VMEM((t, MIX), F32)] * (N_PREP_OUT + 1),
        input_output_aliases=aliases,
        compiler_params=_params("parallel"),
        name="rwkv_lat" if latent else "rwkv_ctx",
    )(*args)


S5_L = 16
S5_ROWS = M_ALL // S5_L
S5_CW = S5_L * S5_GROUP
S5_GS = 8
S5_PW = 2 * P_C
S5_TAB = 24
S5_PRM = 2 * S5_GROUP
S5_NC_CTX = SEQ // S5_L
S5_NC_LAT = DEC_SEQ // S5_L
S5_CTX_ROWS = BATCH * S5_NC_CTX


def _gelu_tanh(x):
    return 0.5 * x * (1.0 + jnp.tanh(math.sqrt(2.0 / math.pi) * (x + 0.044715 * (x * x * x))))


def _cmul(ar, ai, br, bi):
    return ar * br - ai * bi, ar * bi + ai * br


def _dot3(a, b, dims):
    ah, al = _split2(a)
    bh, bl = _split2(b)
    d = lambda x, y: lax.dot_general(x, y, dims, preferred_element_type=F32)
    return d(ah, bh) + d(ah, bl) + d(al, bh)


def _s5_param_kernel(prm_ref, lam_ref, tab_ref, r_ref, st_ref):
    nl, ns = S5_L, S5_GROUP
    lane_blk = lax.broadcasted_iota(jnp.int32, (ns, S5_CW), 1) // ns
    kf = lax.broadcasted_iota(jnp.int32, (S5_TAB, P_C), 0).astype(F32)
    t_rows = [jnp.zeros((ns, S5_CW), F32) for _ in range(nl)]
    e_parts = []
    for d in range(2):
        lr, li = lam_ref[0, d, 0:1, :], lam_ref[0, d, 1:2, :]
        dt = jnp.exp(lam_ref[0, d, 2:3, :])
        mag = jnp.exp(lr * dt * kf)
        pw_re, pw_im = mag * jnp.cos(li * dt * kf), mag * jnp.sin(li * dt * kf)
        tab_ref[0, d, 0] = pw_re
        tab_ref[0, d, 1] = pw_im
        nr, ab_im, den = pw_re[1:2] - 1.0, pw_im[1:2], lr * lr + li * li
        f = ((nr * lr + ab_im * li) / den, (ab_im * lr - nr * li) / den)
        part = lambda lo, n: (prm_ref[0, d, 0, lo:lo + n, :], prm_ref[0, d, 1, lo:lo + n, :])
        bb = _cmul(*f, *part(0, ns))
        cc = part(ns, ns)
        pw = lambda k: (pw_re[k:k + 1], pw_im[k:k + 1])
        steps = range(nl)
        if d == 0:
            ke, ks, kk = [nl - 1 - j for j in steps], [j + 1 for j in steps], list(steps)
        else:
            ke, ks, kk = list(steps), [nl - j for j in steps], [nl - 1 - j for j in steps]
        stack = lambda xs: (jnp.concatenate([x[0] for x in xs], axis=0), jnp.concatenate([x[1] for x in xs], axis=0))
        e_re, e_im = stack([_cmul(*bb, *pw(k)) for k in ke])
        s_re, s_im = stack([_cmul(*cc, *pw(k)) for k in ks])
        k_re, k_im = stack([_cmul(*cc, *pw(k)) for k in kk])
        st_ref[0, d] = jnp.concatenate([s_re, -s_im], axis=1).astype(BF16)
        krow = _dot3(bb[0], k_re, NT) - _dot3(bb[1], k_im, NT)
        for j in steps:
            if d == 0:
                shifted, keep = pltpu.roll(krow, ns * j, axis=1), lane_blk >= j
            else:
                shifted, keep = pltpu.roll(krow, (S5_CW - ns * (nl - 1 - j)) % S5_CW, axis=1), lane_blk <= j
            t_rows[j] = t_rows[j] + jnp.where(keep, shifted, 0.0)
        e_parts += [e_re, e_im]
    r_ref[0] = jnp.concatenate([jnp.concatenate(t_rows, axis=0)] + e_parts, axis=1).astype(BF16)


def _s5_params(prm, lam):
    tab_spec = pl.BlockSpec((1, 2, 2, S5_TAB, P_C), lambda g: (g, 0, 0, 0, 0))
    return pl.pallas_call(
        _s5_param_kernel,
        grid=(G_C,),
        in_specs=[pl.BlockSpec((1, 2, 2, S5_PRM, P_C), lambda g: (g, 0, 0, 0, 0)),
                  pl.BlockSpec((1, 2, 8, P_C), lambda g: (g, 0, 0, 0))],
        out_specs=[tab_spec,
                   pl.BlockSpec((1, S5_CW, 2 * S5_CW), lambda g: (g, 0, 0)),
                   pl.BlockSpec((1, 2, S5_CW, S5_PW), lambda g: (g, 0, 0, 0))],
        out_shape=[jax.ShapeDtypeStruct((G_C, 2, 2, S5_TAB, P_C), F32),
                   jax.ShapeDtypeStruct((G_C, S5_CW, 2 * S5_CW), BF16),
                   jax.ShapeDtypeStruct((G_C, 2, S5_CW, S5_PW), BF16)],
        compiler_params=_params("parallel"),
        name="s5_params",
    )(prm, lam)


def _s5_core_kernel(u_ref, r_ref, st_ref, tab_ref, h0_ref, o_ref, hfin_ref, ug_scr, yg_scr, e_scr, hp_scr):
    ns, nl = S5_GROUP, S5_L
    per_v = 128 // ns
    blk = lax.broadcasted_iota(jnp.int32, (8, 128), 1) // ns

    def merge(select):
        acc = select(0)
        for b in range(1, per_v):
            acc = jnp.where(blk == b, select(b), acc)
        return acc

    def shuffle(srcs):
        rolled = []
        for s in range(per_v):
            m = merge(lambda b: srcs[(b + s) % per_v])
            rolled.append(m if s == 0 else pltpu.roll(m, s * ns, axis=1))
        return [merge(lambda b: rolled[(b - a) % per_v]) for a in range(per_v)]

    def tiles(regroup):
        def ctx_tile(ti, _):
            c, b0 = ti // (BATCH // 8), (ti % (BATCH // 8)) * 8
            regroup(b0 * SEQ + c * nl, SEQ, pl.multiple_of(c * BATCH + b0, 8))
            return 0

        def lat_tile(ti, _):
            b, c0 = ti // (S5_NC_LAT // 8), (ti % (S5_NC_LAT // 8)) * 8
            regroup(M_CTX + b * DEC_SEQ + c0 * nl, nl, pl.multiple_of(S5_CTX_ROWS + b * S5_NC_LAT + c0, 8))
            return 0

        lax.fori_loop(0, S5_CTX_ROWS // 8, ctx_tile, 0, unroll=4)
        lax.fori_loop(0, (S5_ROWS - S5_CTX_ROWS) // 8, lat_tile, 0, unroll=4)

    def regroup_in(tok, stride, row):
        uj = [u_ref[pl.ds(tok + j, 8, stride=stride), :] for j in range(nl)]
        for jh in range(nl // per_v):
            for g, v in enumerate(shuffle(uj[jh * per_v:(jh + 1) * per_v])):
                ug_scr[g, pl.ds(row, 8), jh * 128:(jh + 1) * 128] = v

    tiles(regroup_in)

    chains = [(g, d) for g in range(S5_GS) for d in range(2)]
    for g in range(S5_GS):
        a = jnp.dot(ug_scr[g].astype(BF16), r_ref[g], preferred_element_type=F32)
        yg_scr[g] = a[:, 0:S5_CW]
        for d in range(2):
            e_scr[g, d] = a[:, S5_CW + d * S5_PW:S5_CW + (d + 1) * S5_PW]
    lam = []
    for g, d in chains:
        lr, li = (tab_ref[g, d, ri, nl:nl + 1, :] for ri in range(2))
        lam.append((jnp.concatenate([lr, lr], axis=1), jnp.concatenate([-li, li], axis=1)))

    def advance(c, hs, latent):
        nc = S5_NC_LAT if latent else S5_NC_CTX
        out = []
        for i, (g, d) in enumerate(chains):
            cc = c if d == 0 else nc - 1 - c
            rows = (pl.ds(S5_CTX_ROWS + cc, DEC_BATCH, stride=nc) if latent
                    else pl.ds(pl.multiple_of(cc * BATCH, BATCH), BATCH))
            hp_scr[g, d, rows, :] = hs[i]
            out.append(lam[i][0] * hs[i] + lam[i][1] * pltpu.roll(hs[i], P_C, axis=1) + e_scr[g, d, rows, :])
        return out

    def both(c, carry):
        hc, hl = carry
        return tuple(advance(c, hc, False)), tuple(advance(c, hl, True))

    h_ctx = tuple(jnp.zeros((BATCH, S5_PW), F32) for _ in chains)
    h_lat = tuple(h0_ref[g, d] for g, d in chains)
    h_ctx, h_lat = lax.fori_loop(0, S5_NC_CTX, both, (h_ctx, h_lat), unroll=2)
    lax.fori_loop(S5_NC_CTX, S5_NC_LAT, lambda c, hl: tuple(advance(c, hl, True)), h_lat, unroll=4)
    for i, (g, d) in enumerate(chains):
        hfin_ref[g, d] = h_ctx[i]
    for g in range(S5_GS):
        y = yg_scr[g]
        for d in range(2):
            y = y + lax.dot_general(hp_scr[g, d].astype(BF16), st_ref[g, d], NT, preferred_element_type=F32)
        yg_scr[g] = y

    def regroup_out(tok, stride, row):
        for jh in range(nl // per_v):
            yv = [yg_scr[g, pl.ds(row, 8), jh * 128:(jh + 1) * 128] for g in range(S5_GS)]
            for jj, v in enumerate(shuffle(yv)):
                o_ref[pl.ds(tok + jh * per_v + jj, 8, stride=stride), :] = v

    tiles(regroup_out)


def _s5_core(p2, r, st, tab, h0):
    blk = lambda *tail: pl.BlockSpec((S5_GS,) + tail, lambda s: (s,) + (0,) * len(tail))
    col = pl.BlockSpec((M_ALL, S5_GS * S5_GROUP), lambda s: (0, s))
    return pl.pallas_call(
        _s5_core_kernel,
        grid=(G_C // S5_GS,),
        in_specs=[col, blk(S5_CW, 2 * S5_CW), blk(2, S5_CW, S5_PW),
                  blk(2, 2, S5_TAB, P_C), blk(2, DEC_BATCH, S5_PW)],
        out_specs=[col, blk(2, BATCH, S5_PW)],
        out_shape=[jax.ShapeDtypeStruct((M_ALL, MIX), F32),
                   jax.ShapeDtypeStruct((G_C, 2, BATCH, S5_PW), F32)],
        scratch_shapes=[pltpu.VMEM((S5_GS, S5_ROWS, S5_CW), F32),
                        pltpu.VMEM((S5_GS, S5_ROWS, S5_CW), F32),
                        pltpu.VMEM((S5_GS, 2, S5_ROWS, S5_PW), F32),
                        pltpu.VMEM((S5_GS, 2, S5_ROWS, S5_PW), F32)],
        compiler_params=_params("parallel"),
        name="s5_core",
    )(p2, r, st, tab, h0)


def _s5_out_kernel(y_ref, u_ref, d_ref, w_ref, o_ref):
    z = _gelu_tanh(y_ref[...] + d_ref[...] * u_ref[...])
    o_ref[...] = z * _sigmoid(jnp.dot(z.astype(BF16), w_ref[...], preferred_element_type=F32))


def _s5_out(y_tok, p2, d_skip, w_glu_bf16):
    tm = 512
    return pl.pallas_call(
        _s5_out_kernel,
        grid=(M_ALL // tm,),
        in_specs=[pl.BlockSpec((tm, MIX), lambda i: (i, 0)),
                  pl.BlockSpec((tm, MIX), lambda i: (i, 0)),
                  pl.BlockSpec((1, MIX), lambda i: (0, 0)),
                  pl.BlockSpec((MIX, MIX), lambda i: (0, 0))],
        out_specs=pl.BlockSpec((tm, MIX), lambda i: (i, 0)),
        out_shape=jax.ShapeDtypeStruct((M_ALL, MIX), F32),
        compiler_params=_params("parallel"),
        name="s5_out",
    )(y_tok, p2, d_skip.reshape(1, MIX), w_glu_bf16)


def _s5_tables(lam_re, lam_im, log_dt, b_re, b_im, c_re, c_im):
    rows = lambda b, c: jnp.concatenate([jnp.swapaxes(b, -1, -2), c], axis=2)
    prm = jnp.stack([rows(b_re, c_re), rows(b_im, c_im)], axis=1)
    dt = jnp.broadcast_to(log_dt[:, :, None], lam_re.shape)
    lam = jnp.stack([lam_re, lam_im, dt] + [jnp.zeros_like(dt)] * 5, axis=2)
    return jnp.transpose(prm, (2, 0, 1, 3, 4)), jnp.transpose(lam, (1, 0, 2, 3))


def _s5_mixer(p2, tables, d_skip, w_glu_bf16, state_re, state_im):
    tab, r, st = _s5_params(*tables)
    h0 = jnp.concatenate([state_re, state_im], axis=-1)
    y_tok, hfin = _s5_core(p2, r, st, tab, jnp.transpose(h0, (2, 1, 0, 3)))
    oc = _s5_out(y_tok, p2, d_skip, w_glu_bf16)
    hfin = jnp.transpose(hfin, (2, 1, 0, 3))
    return oc, hfin[..., :P_C], hfin[..., P_C:]


def _ret_kernel(*refs, t, latent):
    if latent:
        (q_ref, k_ref, v_ref, g_ref, dl_ref, lnw_ref, lnb_ref, r0_ref, cos_ref, sin_ref, _,
         o_ref, ob_ref) = refs
    else:
        q_ref, k_ref, v_ref, g_ref, dl_ref, lnw_ref, lnb_ref, o_ref, rfin_ref, ob_ref = refs
    n = t // CHUNK
    w = HD_D
    jf = lax.broadcasted_iota(jnp.int32, (CHUNK, CHUNK), 0).astype(F32)
    kf = lax.broadcasted_iota(jnp.int32, (CHUNK, CHUNK), 1).astype(F32)
    diff = jf - kf
    tabs = []
    for dr in range(2):
        for h in range(H_D):
            l = -_softplus(-dl_ref[h, dr:dr + 1, :])
            if dr == 0:
                dmat = jnp.where(diff >= 0, jnp.exp(l * jnp.maximum(diff, 0.0)), 0.0)
                xi = jnp.exp(l * (jf + 1.0))
                zeta = jnp.exp(l * (CHUNK - 1.0 - jf))
            else:
                dmat = jnp.where(diff < 0, jnp.exp(l * jnp.maximum(-diff, 0.0)), 0.0)
                xi = jnp.exp(l * (CHUNK - jf))
                zeta = jnp.exp(l * jf)
            tabs.append((dmat, xi, zeta, jnp.exp(l * CHUNK)))
    chains = [(dr, h) for dr in range(2) for h in range(H_D)]

    def body(ci, states):
        q, k, v, where = [], [], [], []
        for dr, h in chains:
            cj = ci if dr == 0 else n - 1 - ci
            rows = pl.ds(pl.multiple_of(cj * CHUNK, CHUNK), CHUNK)
            cols = slice(h * w, (h + 1) * w)
            qi = q_ref[rows, cols]
            ki = k_ref[rows, cols] * (HD_D ** -0.5)
            if latent:
                qi = _rope_pairs(qi, cos_ref[rows, :], sin_ref[rows, :], HD_D // 2)
                ki = _rope_pairs(ki, cos_ref[rows, :], sin_ref[rows, :], HD_D // 2)
            q.append(qi.astype(BF16))
            k.append(ki)
            v.append(v_ref[rows, cols].astype(BF16))
            where.append((o_ref if dr == 0 else ob_ref, rows, cols))
        nch = range(len(chains))
        inner = [(_bdot(q[i], k[i], NT) * tabs[i][0]).astype(BF16) for i in nch]
        cross = [_bdot(q[i], states[i]) * tabs[i][1] for i in nch]
        kz = [(k[i] * tabs[i][2]).astype(BF16) for i in nch]
        out = [jnp.dot(inner[i], v[i], preferred_element_type=F32) + cross[i] for i in nch]
        new_states = [states[i] * tabs[i][3] + lax.dot_general(kz[i], v[i], TN, preferred_element_type=F32)
                      for i in nch]
        for (dst, rows, cols), y in zip(where, out):
            dst[rows, cols] = y
        return tuple(new_states)

    if latent:
        lax.fori_loop(0, n, body, tuple(r0_ref[0, dr, h] for dr, h in chains), unroll=2)
    else:
        fin = lax.fori_loop(0, n, body, tuple(jnp.zeros((w, w), F32) for _ in chains), unroll=True)
        for i, (dr, h) in enumerate(chains):
            rfin_ref[0, dr, h] = fin[i]
    for h in range(H_D):
        cols = slice(h * w, (h + 1) * w)
        y = o_ref[:, cols] + ob_ref[:, cols]
        yc = y - jnp.mean(y, axis=-1, keepdims=True)
        yn = yc * lax.rsqrt(jnp.mean(yc * yc, axis=-1, keepdims=True) + RET_GN_EPS)
        o_ref[:, cols] = (yn * lnw_ref[:, cols] + lnb_ref[:, cols]) * _silu(g_ref[:, cols])


def _retention(p, decay_logit, ln_w, ln_b, latent, r0=None, rope=None, dst=None):
    if latent:
        nseq, t, rb0 = DEC_BATCH, DEC_SEQ, M_CTX // DEC_SEQ
    else:
        nseq, t, rb0 = BATCH, SEQ, 0
    w = HD_D
    dl = jnp.broadcast_to(decay_logit.T[:, :, None], (H_D, 2, w))
    full = lambda shape: pl.BlockSpec(shape, lambda b: (0,) * len(shape))
    st_spec = pl.BlockSpec((1, 2, H_D, w, w), lambda b: (b, 0, 0, 0, 0))
    in_specs = [pl.BlockSpec((t, MIX), lambda b, j=j: (rb0 + b, j)) for j in (1, 2, 3, 4)]
    in_specs += [full((H_D, 2, w)), full((1, MIX)), full((1, MIX))]
    args = [p, p, p, p, dl, ln_w.reshape(1, MIX), ln_b.reshape(1, MIX)]
    o_spec = pl.BlockSpec((t, MIX), lambda b: (rb0 + b, 0))
    o_shape = jax.ShapeDtypeStruct((M_ALL, MIX), F32)
    if latent:
        in_specs += [st_spec, full((t, w)), full((t, w)), pl.BlockSpec(memory_space=pl.ANY)]
        args += [r0, *rope, dst]
        out_specs, out_shape, aliases = o_spec, o_shape, {len(args) - 1: 0}
    else:
        out_specs = [o_spec, st_spec]
        out_shape = [o_shape, jax.ShapeDtypeStruct((nseq, 2, H_D, w, w), F32)]
        aliases = {}
    return pl.pallas_call(
        functools.partial(_ret_kernel, t=t, latent=latent),
        grid=(nseq,),
        in_specs=in_specs, out_specs=out_specs, out_shape=out_shape,
        scratch_shapes=[pltpu.VMEM((t, MIX), F32)],
        input_output_aliases=aliases,
        compiler_params=_params("parallel"),
        name="retention_lat" if latent else "retention_ctx",
    )(*args)


def _rope_tables(n_tok, dim, reps):
    rows = n_tok // GRID_W
    n_freq = dim // 4
    inv = 1.0 / (ROPE_THETA ** (jnp.arange(n_freq, dtype=F32) / n_freq))
    row = jnp.repeat(jnp.arange(rows, dtype=F32), GRID_W)
    col = jnp.tile(jnp.arange(GRID_W, dtype=F32), rows)
    ang = jnp.concatenate([row[:, None] * inv, col[:, None] * inv], axis=-1)
    cos, sin = jnp.cos(ang), jnp.sin(ang)
    return jnp.tile(jnp.concatenate([cos, cos], axis=1), (1, reps)), \
        jnp.tile(jnp.concatenate([-sin, sin], axis=1), (1, reps))


def kernel(x_prompt, x_sample, cache_k_ab, cache_v_ab, state_rwkv, state_s5_re, state_s5_im, state_ret, c, c_ctx, norm1_g, norm2_g, w_mod, b_mod, w_ff_gate, w_ff_up, w_ff_down, w_in_ab, w_out_ab, qk_gain_a, lambda_qk, subln_g, rwkv_mu, rwkv_k_k, rwkv_k_a, rwkv_r_k, rwkv_w0, rwkv_w_up, rwkv_a0, rwkv_a_up, rwkv_g_up, rwkv_ln_w, rwkv_ln_b, w_in_cd, w_out_cd, s5_lam_re, s5_lam_im, s5_log_dt, s5_b_re, s5_b_im, s5_c_re, s5_c_im, s5_d, s5_w_glu, ret_decay_logit, ret_ln_w, ret_ln_b):
    d = D_MODEL
    xs = [x_prompt.reshape(M_CTX, d), x_sample.reshape(M_LAT, d)]
    cvec = jnp.zeros((MOD_ROWS, d), F32).at[0].set(c_ctx).at[1:1 + DEC_BATCH].set(c)
    mods = _modulation(cvec, w_mod, b_mod)

    lam_init = 0.8 - 0.6 * math.exp(-0.3 * 0)
    p = _norm_linear(xs, norm1_g[0], mods[0], w_in_ab[0].astype(BF16))
    gain2 = jnp.tile(qk_gain_a[0], (1, 2))
    rope_a = _rope_tables(DEC_SEQ, HD_A, 2)
    ck = cache_k_ab[:, 0].reshape(DEC_BATCH * PAST_LEN, MIX)
    cv = cache_v_ab[:, 0].reshape(DEC_BATCH * PAST_LEN, MIX)
    oa, k_ctx, v_ctx = _attention(p, gain2, lambda_qk[0], subln_g[0], lam_init, latent=False)
    oa = _attention(p, gain2, lambda_qk[0], subln_g[0], lam_init, latent=True,
                    cache_k=ck, cache_v=cv, rope=rope_a, dst=oa)
    rw_prm = (rwkv_mu[0], rwkv_k_k[0], rwkv_k_a[0], rwkv_r_k[0], rwkv_w0[0], rwkv_w_up[0],
              rwkv_a0[0], rwkv_a_up[0], rwkv_g_up[0])
    ob, sfin_ctx = _rwkv(p, rw_prm, rwkv_ln_w[0], rwkv_ln_b[0], latent=False)
    ob = _rwkv(p, rw_prm, rwkv_ln_w[0], rwkv_ln_b[0], latent=True, s0=state_rwkv[:, 0], dst=ob)
    x = _mix_ffn(xs, oa, ob, w_out_ab[0].astype(BF16), norm2_g[0], mods[0], w_ff_gate[0].astype(BF16),
                 w_ff_up[0].astype(BF16), w_ff_down[0].astype(BF16), split_out=False)

    p2 = _norm_linear([x], norm1_g[1], mods[1], w_in_cd[0].astype(BF16))
    tables = _s5_tables(s5_lam_re[0], s5_lam_im[0], s5_log_dt[0], s5_b_re[0], s5_b_im[0],
                        s5_c_re[0], s5_c_im[0])
    oc, s5_fin_re, s5_fin_im = _s5_mixer(p2, tables, s5_d[0], s5_w_glu[0].astype(BF16),
                                         state_s5_re[:, 0], state_s5_im[:, 0])
    rope_d = _rope_tables(DEC_SEQ, HD_D, 1)
    od, rfin = _retention(p2, ret_decay_logit[0], ret_ln_w[0], ret_ln_b[0], latent=False)
    od = _retention(p2, ret_decay_logit[0], ret_ln_w[0], ret_ln_b[0], latent=True, r0=state_ret[:, 0],
                    rope=rope_d, dst=od)
    y_ctx, y_lat = _mix_ffn([x], oc, od, w_out_cd[0].astype(BF16), norm2_g[1], mods[1],
                            w_ff_gate[1].astype(BF16), w_ff_up[1].astype(BF16), w_ff_down[1].astype(BF16),
                            split_out=True)
    y_prompt = y_ctx.reshape(BATCH, SEQ, d)
    y_sample = y_lat.reshape(DEC_BATCH, DEC_SEQ, d)
    new_k = jnp.transpose(k_ctx, (0, 4, 1, 2, 3))[:, None]
    new_v = v_ctx.reshape(BATCH, 1, SEQ, H_A, VD_A)
    new_rwkv = sfin_ctx[:, None]
    new_s5_re = s5_fin_re[:, None]
    new_s5_im = s5_fin_im[:, None]
    new_ret = rfin[:, None]
    return (y_prompt, y_sample, new_k, new_v, new_rwkv, new_s5_re, new_s5_im, new_ret)
```

```python
import functools
import math

import numpy as np
import jax
import jax.numpy as jnp
from jax import lax
from jax.experimental import pallas as pl
from jax.experimental.pallas import tpu as pltpu

F32 = jnp.float32
BF16 = jnp.bfloat16
HIGHEST = lax.Precision.HIGHEST

D_MODEL = 1024
BATCH = 32
SEQ = 256
DEC_BATCH = 2
DEC_SEQ = 1024
PAST_LEN = 256
GRID_W = 64
H_A = 4
HD_A = 64
VD_A = 128
H_B = 8
HD_B = 64
MIX = 512
LORA_W = 64
LORA_A = 64
LORA_G = 128
S5_GROUP = 16
G_C = 32
P_C = 64
S5_STATE = G_C * P_C
H_D = 4
HD_D = 128
CHUNK = 128
D_FF = 2816
IN_AB = 3328
IN_CD = 2560
ROPE_THETA = 10000.0
NORM_EPS = 1e-6
RWKV_GN_EPS = 64e-5
RET_GN_EPS = 1e-5

M_CTX = BATCH * SEQ
M_LAT = DEC_BATCH * DEC_SEQ
M_ALL = M_CTX + M_LAT
MOD_ROWS = 8
RWKV_CHUNK = 64
VMEM_LIMIT = 56 * 1024 * 1024

NN = (((1,), (0,)), ((), ()))
NT = (((1,), (1,)), ((), ()))
TN = (((0,), (0,)), ((), ()))


def _params(*sem):
    return pltpu.CompilerParams(dimension_semantics=sem, vmem_limit_bytes=VMEM_LIMIT)


def _bdot(a, b, dims=NN):
    return lax.dot_general(a.astype(BF16), b.astype(BF16), dims, preferred_element_type=F32)


def _hdot(a, b, dims=NN):
    return lax.dot_general(a, b, dims, precision=HIGHEST, preferred_element_type=F32)


def _split_dot(x, m):
    hi = x.astype(BF16)
    lo = (x - hi.astype(F32)).astype(BF16)
    return (jnp.dot(hi, m, preferred_element_type=F32) + jnp.dot(lo, m, preferred_element_type=F32))


def _seg_matrix(n, shift, val):
    r = lax.broadcasted_iota(jnp.int32, (n, n), 0) >> shift
    c = lax.broadcasted_iota(jnp.int32, (n, n), 1) >> shift
    return jnp.where(r == c, val, 0.0).astype(BF16)


def _sigmoid(x):
    return jax.nn.sigmoid(x)


def _silu(x):
    return x * jax.nn.sigmoid(x)


def _softplus(x):
    return jnp.maximum(x, 0.0) + jnp.log(1.0 + jnp.exp(-jnp.abs(x)))


def _mod_row(tile, tm):
    r0 = tile * tm
    return jnp.where(r0 < M_CTX, 0, 1 + (r0 - M_CTX) // DEC_SEQ)


def _norm_mod(x, g, sc_ref, sh_ref, row):
    y = x * lax.rsqrt(jnp.mean(x * x, axis=-1, keepdims=True) + NORM_EPS) * g
    return y * (1.0 + sc_ref[pl.ds(row, 1), :]) + sh_ref[pl.ds(row, 1), :]


def _mod_kernel(c_ref, w_ref, b_ref, o_ref):
    o_ref[0] = _bdot(_silu(c_ref[...]), w_ref[0]) + b_ref[0]


def _modulation(cvec, w_mod, b_mod):
    depth, d, n6 = w_mod.shape
    tn = 1536
    return pl.pallas_call(
        _mod_kernel,
        grid=(depth, n6 // tn),
        in_specs=[pl.BlockSpec((MOD_ROWS, d), lambda l, j: (0, 0)),
                  pl.BlockSpec((1, d, tn), lambda l, j: (l, 0, j)),
                  pl.BlockSpec((1, 1, tn), lambda l, j: (l, 0, j))],
        out_specs=pl.BlockSpec((1, MOD_ROWS, tn), lambda l, j: (l, 0, j)),
        out_shape=jax.ShapeDtypeStruct((depth, MOD_ROWS, n6), F32),
        compiler_params=_params("parallel", "parallel"),
        name="modulation",
    )(cvec, w_mod, b_mod.reshape(depth, 1, n6))


def _tile_specs(tm, d):
    nc = M_CTX // tm
    return [pl.BlockSpec((tm, d), lambda i: (jnp.minimum(i, nc - 1), 0)),
            pl.BlockSpec((tm, d), lambda i: (jnp.maximum(i - nc, 0), 0))]


def _pick_rows(refs, tm):
    if len(refs) == 1:
        return refs[0][...]
    return jnp.where(pl.program_id(0) < M_CTX // tm, refs[0][...], refs[1][...])


def _norm_linear_kernel(*refs, tm):
    *x_refs, g_ref, sc_ref, sh_ref, w_ref, o_ref, wbf = refs

    @pl.when(pl.program_id(0) == 0)
    def _():
        wbf[...] = w_ref[0].astype(BF16)

    row = _mod_row(pl.program_id(0), tm)
    x = _pick_rows(x_refs, tm)
    nsub = 2
    rows = [slice(i * tm // nsub, (i + 1) * tm // nsub) for i in range(nsub)]
    hs = [_norm_mod(x[r], g_ref[...], sc_ref, sh_ref, row).astype(BF16) for r in rows]
    for r, h in zip(rows, hs):
        o_ref[r, :] = jnp.dot(h, wbf[...], preferred_element_type=F32)


def _norm_linear(xs, g, mods, w):
    tm = 512
    d = D_MODEL
    n = w.shape[2]
    x_specs = _tile_specs(tm, d) if len(xs) == 2 else [pl.BlockSpec((tm, d), lambda i: (i, 0))]
    return pl.pallas_call(
        functools.partial(_norm_linear_kernel, tm=tm),
        grid=(M_ALL // tm,),
        in_specs=x_specs + [pl.BlockSpec((1, d), lambda i: (0, 0)),
                            pl.BlockSpec((MOD_ROWS, d), lambda i: (0, 1)),
                            pl.BlockSpec((MOD_ROWS, d), lambda i: (0, 0)),
                            pl.BlockSpec((1, d, n), lambda i: (0, 0, 0), pipeline_mode=pl.Buffered(1))],
        out_specs=pl.BlockSpec((tm, n), lambda i: (i, 0)),
        out_shape=jax.ShapeDtypeStruct((M_ALL, n), F32),
        scratch_shapes=[pltpu.VMEM((d, n), BF16)],
        compiler_params=_params("arbitrary"),
        name="norm_linear",
    )(*xs, g.reshape(1, d), mods, mods, w)


def _mix_ffn_kernel(*refs, tm, ck, n_in, n_out, glu):
    x_refs, refs = refs[:n_in], refs[n_in:]
    if glu:
        (y_ref, u_ref, d_ref, wglu_ref), refs = refs[:4], refs[4:]
        z = _gelu_tanh(y_ref[...] + d_ref[...] * u_ref[...])
        a = z * _sigmoid(jnp.dot(z.astype(BF16), wglu_ref[...], preferred_element_type=F32))
    else:
        a, refs = refs[0][...], refs[1:]
    b_ref, wa_ref, wb_ref, g1_ref, ng_ref, sc_ref, sh_ref, g2_ref, wg_ref, wu_ref, wd_ref = refs[:11]
    o_refs = refs[11:]
    row = _mod_row(pl.program_id(0), tm)
    mix = (jnp.dot(a.astype(BF16), wa_ref[...], preferred_element_type=F32)
           + jnp.dot(b_ref[...].astype(BF16), wb_ref[...], preferred_element_type=F32))
    x = _pick_rows(x_refs, tm) + g1_ref[pl.ds(row, 1), :] * mix
    h = _norm_mod(x, ng_ref[...], sc_ref, sh_ref, row).astype(BF16)
    acc = jnp.zeros((tm, D_MODEL), F32)
    for c in range(D_FF // ck):
        gg = jnp.dot(h, wg_ref[:, c * ck:(c + 1) * ck], preferred_element_type=F32)
        uu = jnp.dot(h, wu_ref[:, c * ck:(c + 1) * ck], preferred_element_type=F32)
        act = (_silu(gg) * uu).astype(BF16)
        acc = acc + jnp.dot(act, wd_ref[c * ck:(c + 1) * ck, :], preferred_element_type=F32)
    y = x + g2_ref[pl.ds(row, 1), :] * acc
    if n_out == 1:
        o_refs[0][...] = y
    else:
        is_ctx = pl.program_id(0) < M_CTX // tm

        @pl.when(is_ctx)
        def _():
            o_refs[0][...] = y

        @pl.when(jnp.logical_not(is_ctx))
        def _():
            o_refs[1][...] = y


def _mix_ffn(xs, oa, ob, w_out_bf16, norm_g, mods, wg, wu, wd, split_out):
    tm, ck = 512, 256
    d = D_MODEL
    row_spec = lambda n: pl.BlockSpec((tm, n), lambda i: (i, 0))
    const = lambda shape, idx: pl.BlockSpec(shape, lambda i: idx, pipeline_mode=pl.Buffered(1))
    mod = lambda j: pl.BlockSpec((MOD_ROWS, d), lambda i: (0, j))
    x_specs = _tile_specs(tm, d) if len(xs) == 2 else [row_spec(d)]
    glu = isinstance(oa, tuple)
    if glu:
        y_tok, p2, d_skip, w_glu = oa
        a_specs = [row_spec(MIX), row_spec(MIX), pl.BlockSpec((1, MIX), lambda i: (0, 0)), const((MIX, MIX), (0, 0))]
        a_args = [y_tok, p2, d_skip.reshape(1, MIX), w_glu]
    else:
        a_specs, a_args = [row_spec(MIX)], [oa]
    if split_out:
        out_specs = _tile_specs(tm, d)
        out_shape = [jax.ShapeDtypeStruct((M_CTX, d), F32), jax.ShapeDtypeStruct((M_LAT, d), F32)]
    else:
        out_specs, out_shape = row_spec(d), jax.ShapeDtypeStruct((M_ALL, d), F32)
    return pl.pallas_call(
        functools.partial(_mix_ffn_kernel, tm=tm, ck=ck, n_in=len(xs), n_out=2 if split_out else 1, glu=glu),
        grid=(M_ALL // tm,),
        in_specs=x_specs + a_specs + [row_spec(MIX),
                            const((MIX, d), (0, 0)), const((MIX, d), (1, 0)),
                            mod(2),
                            pl.BlockSpec((1, d), lambda i: (0, 0)),
                            mod(4), mod(3), mod(5),
                            const((d, D_FF), (0, 0)), const((d, D_FF), (0, 0)), const((D_FF, d), (0, 0))],
        out_specs=out_specs, out_shape=out_shape,
        compiler_params=_params("arbitrary"),
        name="mix_ffn",
    )(*xs, *a_args, ob, w_out_bf16, w_out_bf16, mods, norm_g.reshape(1, d), mods, mods, mods, wg, wu, wd)


def _qk_norm(x, gain, segm):
    ms = _split_dot(x * x, segm)
    return x * lax.rsqrt(ms + NORM_EPS) * gain


def _rope_pairs(x, cosf, sinf, half):
    lane = lax.broadcasted_iota(jnp.int32, x.shape, 1)
    first = (lane & (2 * half - 1)) < half
    n = x.shape[1]
    partner = jnp.where(first, pltpu.roll(x, n - half, axis=1), pltpu.roll(x, half, axis=1))
    return x * cosf + partner * sinf


def _attn_kernel(*refs, latent, lam_init):
    if latent:
        (q_ref, k_ref, v_ref, ck_ref, cv_ref, cosq_ref, sinq_ref, cosk_ref, sink_ref,
         gain_ref, lam_ref, sub_ref, _, o_ref, kall, vall) = refs
    else:
        q_ref, k_ref, v_ref, gain_ref, lam_ref, sub_ref, o_ref, kn_ref, vo_ref, kall, vall = refs
    w = 2 * HD_A
    segm = _seg_matrix(w, 6, 1.0 / HD_A)
    gains = gain_ref[...]
    cols = [slice(h * w, (h + 1) * w) for h in range(H_A)]

    @pl.when(pl.program_id(1) == 0)
    def _():
        k = [_qk_norm(k_ref[:, c], gains[1:2], segm) for c in cols]
        if latent:
            k = [_rope_pairs(x, cosk_ref[...], sink_ref[...], HD_A // 2) for x in k]
            for c, x in zip(cols, k):
                kall[0:PAST_LEN, c] = ck_ref[:, c].astype(BF16)
                kall[PAST_LEN:, c] = x.astype(BF16)
                vall[0:PAST_LEN, c] = cv_ref[:, c].astype(BF16)
                vall[PAST_LEN:, c] = v_ref[:, c].astype(BF16)
        else:
            for h, (c, x) in enumerate(zip(cols, k)):
                xt = x.T
                kn_ref[0, h, 0] = xt[:HD_A]
                kn_ref[0, h, 1] = xt[HD_A:]
                vo_ref[pl.ds(h, v_ref.shape[0], stride=H_A), :] = v_ref[:, c]
                kall[:, c] = x.astype(BF16)
                vall[:, c] = v_ref[:, c].astype(BF16)

    q = [_qk_norm(q_ref[:, c], gains[0:1], segm) for c in cols]
    if latent:
        q = [_rope_pairs(x, cosq_ref[...], sinq_ref[...], HD_A // 2) for x in q]
    lv = lam_ref[...]
    lam = (jnp.exp(jnp.sum(lv[0:1] * lv[1:2], axis=1, keepdims=True))
           - jnp.exp(jnp.sum(lv[2:3] * lv[3:4], axis=1, keepdims=True)) + lam_init)
    scale = HD_A ** -0.5
    comp0 = lax.broadcasted_iota(jnp.int32, q[0].shape, 1) < HD_A
    qc = [jnp.where(comp0, *sel).astype(BF16) for x in q for sel in ((x, 0.0), (0.0, x))]
    s = [lax.dot_general(qc[i], kall[:, cols[i // 2]], NT, preferred_element_type=F32) * scale
         for i in range(2 * H_A)]
    e = [jnp.exp(x - jnp.max(x, axis=-1, keepdims=True)) for x in s]
    p = [x / jnp.sum(x, axis=-1, keepdims=True) for x in e]
    att = [(p[2 * h] - lam * p[2 * h + 1]).astype(BF16) for h in range(H_A)]
    o = [jnp.dot(att[h], vall[:, cols[h]], preferred_element_type=F32) for h in range(H_A)]
    o = [x * lax.rsqrt(jnp.mean(x * x, axis=-1, keepdims=True) + NORM_EPS) * sub_ref[...] for x in o]
    for c, x in zip(cols, o):
        o_ref[:, c] = x * (1.0 - lam_init)


def _attention(p, gain2, lambda_qk, subln_g, lam_init, latent, cache_k=None, cache_v=None, rope=None,
               dst=None):
    w = 2 * HD_A
    if latent:
        nseq, t, tq, rb0, s_len = DEC_BATCH, DEC_SEQ, 128, M_CTX // DEC_SEQ, PAST_LEN + DEC_SEQ
    else:
        nseq, t, tq, rb0, s_len = BATCH, SEQ, SEQ, 0, SEQ
    nq = t // tq
    qoff = rb0 * nq
    full = lambda shape: pl.BlockSpec(shape, lambda b, i: (0,) * len(shape))
    in_specs = [pl.BlockSpec((tq, MIX), lambda b, i: (qoff + b * nq + i, 0)),
                pl.BlockSpec((t, MIX), lambda b, i: (rb0 + b, 1)),
                pl.BlockSpec((t, MIX), lambda b, i: (rb0 + b, 2))]
    args = [p, p, p]
    if latent:
        cosf, sinf = rope
        in_specs += [pl.BlockSpec((PAST_LEN, MIX), lambda b, i: (b, 0)),
                     pl.BlockSpec((PAST_LEN, MIX), lambda b, i: (b, 0)),
                     pl.BlockSpec((tq, w), lambda b, i: (i, 0)),
                     pl.BlockSpec((tq, w), lambda b, i: (i, 0)),
                     full((t, w)), full((t, w))]
        args += [cache_k, cache_v, cosf, sinf, cosf, sinf]
    in_specs += [full((2, w)), full((4, HD_A)), full((1, w))]
    args += [gain2, lambda_qk, subln_g.reshape(1, w)]
    o_spec = pl.BlockSpec((tq, MIX), lambda b, i: (qoff + b * nq + i, 0))
    o_shape = jax.ShapeDtypeStruct((M_ALL, MIX), F32)
    aliases = {}
    if latent:
        out_specs, out_shape = o_spec, o_shape
        in_specs.append(pl.BlockSpec(memory_space=pl.ANY))
        args.append(dst)
        aliases = {len(args) - 1: 0}
    else:
        out_specs = [o_spec, pl.BlockSpec((1, H_A, 2, HD_A, t), lambda b, i: (b, 0, 0, 0, 0)),
                     pl.BlockSpec((t * H_A, VD_A), lambda b, i: (b, 0))]
        out_shape = [o_shape, jax.ShapeDtypeStruct((nseq, H_A, 2, HD_A, t), F32),
                     jax.ShapeDtypeStruct((nseq * t * H_A, VD_A), F32)]
    return pl.pallas_call(
        functools.partial(_attn_kernel, latent=latent, lam_init=lam_init),
        grid=(nseq, nq),
        in_specs=in_specs, out_specs=out_specs, out_shape=out_shape,
        scratch_shapes=[pltpu.VMEM((s_len, MIX), BF16), pltpu.VMEM((s_len, MIX), BF16)],
        input_output_aliases=aliases,
        compiler_params=_params("parallel", "arbitrary"),
        name="diff_attention_lat" if latent else "diff_attention_ctx",
    )(*args)


def _centred_shift(x, mu):
    t = x.shape[0]
    row = lax.broadcasted_iota(jnp.int32, x.shape, 0)
    prev = jnp.where(row == 0, 0.0, pltpu.roll(x, 1, axis=0))
    nxt = jnp.where(row == t - 1, 0.0, pltpu.roll(x, t - 1, axis=0))
    return x + (0.5 * (prev + nxt) - x) * mu


def _seg_sum(x, segm):
    return jnp.concatenate([_split_dot(x[:, j * 128:(j + 1) * 128], segm) for j in range(x.shape[1] // 128)],
                           axis=1)


def _rwkv_prep_kernel(r_ref, k_ref, v_ref, l_ref, mur_ref, muk_ref, muv_ref, mul_ref,
                      kk_ref, ka_ref, rk_ref, w0_ref, wup_ref, a0_ref, aup_ref, gup_ref,
                      ro_ref, ldf_ref, ldb_ref, kbo_ref, vbo_ref, kko_ref, ao_ref, gate_ref, bonus_ref):
    seg1 = _seg_matrix(128, 6, 1.0)
    r = _centred_shift(r_ref[...], mur_ref[...])
    kb = _centred_shift(k_ref[...], muk_ref[...])
    vb = _centred_shift(v_ref[...], muv_ref[...])
    lo = _centred_shift(l_ref[...], mul_ref[...])
    xw = lo[:, 0:LORA_W]
    xa = lo[:, LORA_W:LORA_W + LORA_A]
    xg = lo[:, LORA_W + LORA_A:]
    kk = kb * kk_ref[...]
    kk = kk * lax.rsqrt(_seg_sum(kk * kk, seg1) + 1e-12)
    a = _sigmoid(a0_ref[...] + _bdot(xa, aup_ref[...]))
    kb2 = kb * (1.0 + (a - 1.0) * ka_ref[...])
    lw = jnp.tanh(xw)
    for dr, ld_ref in enumerate((ldf_ref, ldb_ref)):
        z = w0_ref[dr:dr + 1, :] + _bdot(lw, wup_ref[dr])
        logw = -_softplus(-z) - 0.5
        ld_ref[...] = -jnp.exp(logw)
    gate_ref[...] = _bdot(_sigmoid(xg), gup_ref[...])
    bonus_ref[...] = _seg_sum(r * kb2 * rk_ref[...], seg1) * vb
    ro_ref[...] = r
    kbo_ref[...] = kb2
    vbo_ref[...] = vb
    kko_ref[...] = kk
    ao_ref[...] = a


N_PREP_IN = 16
N_PREP_OUT = 9


def _split2(x):
    hi = x.astype(BF16)
    return hi, (x - hi.astype(F32)).astype(BF16)


RWKV_INV_BLOCK = 16
RWKV_UNROLL_MAX = 4


def _rwkv_masks(c, rev):
    ti = lax.broadcasted_iota(jnp.int32, (c, c), 0)
    si = lax.broadcasted_iota(jnp.int32, (c, c), 1)
    tri = jnp.where((si >= ti) if rev else (si <= ti), 1.0, 0.0).astype(BF16)
    t4 = lax.broadcasted_iota(jnp.int32, (4 * c, 4 * c), 0)
    s4 = lax.broadcasted_iota(jnp.int32, (4 * c, 4 * c), 1)
    tm, sm = t4 & (c - 1), s4 & (c - 1)
    strict = (sm > tm) if rev else (sm < tm)
    incl = (sm >= tm) if rev else (sm <= tm)
    same_head = ((t4 // c) & 1) == ((s4 // c) & 1)
    top = t4 < 2 * c
    gmask = same_head & ((top & strict) | (~top & incl))
    t2 = lax.broadcasted_iota(jnp.int32, (2 * c, 2 * c), 0)
    s2 = lax.broadcasted_iota(jnp.int32, (2 * c, 2 * c), 1)
    same = lambda n: (t2 // n) == (s2 // n)
    levels = []
    n = RWKV_INV_BLOCK
    while n < c:
        levels.append(same(2 * n) & ~same(n))
        n *= 2
    f = lambda m: jnp.where(m, 1.0, 0.0)
    return tri, f(gmask), f(same(RWKV_INV_BLOCK)), tuple(f(m) for m in levels), f(same(c)), f(t2 == s2)


def _keep(mask01, x):
    return jnp.where(mask01 > 0.5, x, 0.0)


def _tri_inverse(a, diag_blk, levels, eye):
    n = a[0].shape[0]
    d = [_keep(diag_blk, x) for x in a]
    t = [eye + x for x in d]
    p = [_bdot(x, x) for x in d]
    for _ in range(int(math.log2(RWKV_INV_BLOCK)) - 2):
        res = [_bdot(jnp.concatenate([pi, ti], axis=0), pi) for pi, ti in zip(p, t)]
        p = [x[:n] for x in res]
        t = [ti + x[n:] for ti, x in zip(t, res)]
    t = [ti + _bdot(ti, pi) for ti, pi in zip(t, p)]
    for off in levels:
        half = [_bdot(ti, _keep(off, x)) for ti, x in zip(t, a)]
        t = [ti + _bdot(x, ti) for ti, x in zip(t, half)]
    return t


def _rwkv_pair_chunks(ins, sts, tris, gmasks, diag_blk, levels, same_head, eye):
    c, w = ins[0][0].shape
    hd = w // 2
    nch = range(len(ins))
    r, ld, kb, vb, kk, a = (list(z) for z in zip(*ins))
    split = [_split2(x) for x in ld]
    lcum = [jnp.dot(tris[i], jnp.concatenate(split[i], axis=1), preferred_element_type=F32) for i in nch]
    lcum = [x[:, :w] + x[:, w:] for x in lcum]
    ltot = [jnp.sum(x, axis=0, keepdims=True) for x in ld]
    beta = [kk[i] * a[i] for i in nch]
    eneg = [jnp.exp(-x) for x in lcum]
    abar = [-kk[i] * jnp.exp(lcum[i] - ld[i]) for i in nch]
    rbar = [r[i] * jnp.exp(lcum[i]) for i in nch]
    bt = [(beta[i] * eneg[i]).astype(BF16) for i in nch]
    kt = [(kb[i] * eneg[i]).astype(BF16) for i in nch]
    vbb = [x.astype(BF16) for x in vb]
    head0 = lax.broadcasted_iota(jnp.int32, (c, w), 1) < hd
    pick = lambda res: jnp.where(head0, res[:c], res[c:])
    arst = [_bdot(jnp.concatenate([abar[i], rbar[i]], axis=0), sts[i], NT) for i in nch]
    lhs = [jnp.concatenate([jnp.where(head0, abar[i], 0.0), jnp.where(head0, 0.0, abar[i]),
                            jnp.where(head0, rbar[i], 0.0), jnp.where(head0, 0.0, rbar[i])], axis=0) for i in nch]
    g = [_keep(gmasks[i], _bdot(lhs[i], jnp.concatenate([bt[i], bt[i], kt[i], kt[i]], axis=0), NT))
         for i in nch]
    x = [arst[i][:c] + pick(_bdot(g[i][:2 * c, 2 * c:], jnp.concatenate([vbb[i], vbb[i]], axis=0))) for i in nch]
    tinv = _tri_inverse([gi[:2 * c, :2 * c] for gi in g], diag_blk, levels, eye)
    u = [pick(_bdot(tinv[i], jnp.concatenate([x[i], x[i]], axis=0))) for i in nch]
    ub = [z.astype(BF16) for z in u]
    y = [arst[i][c:] + pick(_bdot(g[i][2 * c:], jnp.concatenate([ub[i], ub[i], vbb[i], vbb[i]], axis=0)))
         for i in nch]
    erem = [jnp.exp(ltot[i] - lcum[i]) for i in nch]
    bkh = [jnp.concatenate([beta[i] * erem[i], kb[i] * erem[i]], axis=0) for i in nch]
    st_new = [jnp.exp(ltot[i]) * sts[i]
              + _keep(same_head, _bdot(jnp.concatenate([ub[i], vbb[i]], axis=0), bkh[i], TN)) for i in nch]
    return y, st_new


def _rwkv_scan_kernel(*refs, t, c, npair, latent):
    prep_in, (lnw_ref, lnb_ref), rest = refs[:N_PREP_IN], refs[N_PREP_IN:N_PREP_IN + 2], refs[N_PREP_IN + 2:]
    if latent:
        s0_ref, _, o_ref, *scratch = rest
    else:
        o_ref, sfin_ref, *scratch = rest
    *prep_out, yb_ref = scratch
    _rwkv_prep_kernel(*prep_in, *prep_out)
    r_ref, ldf_ref, ldb_ref, kb_ref, vb_ref, kk_ref, a_ref, gate_ref, bonus_ref = prep_out
    n = t // c
    w = 2 * HD_B
    masks = [_rwkv_masks(c, rev) for rev in (False, True)]

    def body(ci, states):
        ins, tris, gmasks, dsts = [], [], [], []
        for dr, (ld_ref, dst) in enumerate(((ldf_ref, o_ref), (ldb_ref, yb_ref))):
            cj = ci if dr == 0 else n - 1 - ci
            rows = pl.ds(pl.multiple_of(cj * c, c), c)
            for p in range(npair):
                cols = slice(p * w, (p + 1) * w)
                ins.append(tuple(ref[rows, cols] for ref in (r_ref, ld_ref, kb_ref, vb_ref, kk_ref, a_ref)))
                tris.append(masks[dr][0])
                gmasks.append(masks[dr][1])
                dsts.append((dst, rows, cols))
        ys, new_states = _rwkv_pair_chunks(ins, list(states), tris, gmasks, *masks[0][2:])
        for (dst, rows, cols), y in zip(dsts, ys):
            dst[rows, cols] = y
        return tuple(new_states)

    if latent:
        zero = jnp.zeros((HD_B, HD_B), F32)
        init = tuple(jnp.concatenate([jnp.concatenate([s0_ref[0, dr, 2 * p], zero], axis=1),
                                      jnp.concatenate([zero, s0_ref[0, dr, 2 * p + 1]], axis=1)], axis=0)
                     for dr in range(2) for p in range(npair))
    else:
        init = tuple(jnp.zeros((w, w), F32) for _ in range(2 * npair))
    fin = lax.fori_loop(0, n, body, init, unroll=True if n <= RWKV_UNROLL_MAX else 2)
    segm = _seg_matrix(w, 6, 1.0 / HD_B)
    for p in range(npair):
        if not latent:
            for dr in range(2):
                st = fin[dr * npair + p]
                sfin_ref[0, dr, 2 * p] = st[:HD_B, :HD_B]
                sfin_ref[0, dr, 2 * p + 1] = st[HD_B:, HD_B:]
        cols = slice(p * w, (p + 1) * w)
        y = o_ref[:, cols] + yb_ref[:, cols]
        yc = y - _split_dot(y, segm)
        yn = yc * lax.rsqrt(_split_dot(yc * yc, segm) + RWKV_GN_EPS)
        o_ref[:, cols] = (yn * lnw_ref[:, cols] + lnb_ref[:, cols] + bonus_ref[:, cols]) * gate_ref[:, cols]


def _rwkv(p, prm, ln_w, ln_b, latent, s0=None, dst=None):
    mu, k_k, k_a, r_k, w0, w_up, a0, a_up, g_up = prm
    nseq, t, rb0 = (DEC_BATCH, DEC_SEQ, M_CTX // DEC_SEQ) if latent else (BATCH, SEQ, 0)
    npair = H_B // 2
    lw = LORA_W + LORA_A + LORA_G
    c0 = (IN_AB - 3 * MIX - lw) // MIX
    cl = (IN_AB - lw) // lw
    full = lambda *shape: pl.BlockSpec(shape, lambda b: (0,) * len(shape))
    st_spec = pl.BlockSpec((1, 2, H_B, HD_B, HD_B), lambda b: (b, 0, 0, 0, 0))
    o_spec = pl.BlockSpec((t, MIX), lambda b: (rb0 + b, 0))
    o_shape = jax.ShapeDtypeStruct((M_ALL, MIX), F32)
    in_specs = [pl.BlockSpec((t, MIX), lambda b: (rb0 + b, c0)),
                pl.BlockSpec((t, MIX), lambda b: (rb0 + b, c0 + 1)),
                pl.BlockSpec((t, MIX), lambda b: (rb0 + b, c0 + 2)),
                pl.BlockSpec((t, lw), lambda b: (rb0 + b, cl)),
                full(1, MIX), full(1, MIX), full(1, MIX), full(1, lw),
                full(1, MIX), full(1, MIX), full(1, MIX),
                full(2, MIX), full(2, LORA_W, MIX), full(1, MIX), full(LORA_A, MIX), full(LORA_G, MIX),
                full(1, MIX), full(1, MIX)]
    args = [p, p, p, p,
            mu[None, 0:MIX], mu[None, MIX:2 * MIX], mu[None, 2 * MIX:3 * MIX], mu[None, 3 * MIX:],
            k_k.reshape(1, MIX), k_a.reshape(1, MIX), r_k.reshape(1, MIX), w0, w_up,
            a0.reshape(1, MIX), a_up, g_up, ln_w.reshape(1, MIX), ln_b.reshape(1, MIX)]
    if latent:
        args += [s0, dst]
        in_specs += [st_spec, pl.BlockSpec(memory_space=pl.ANY)]
        out_specs, out_shape, aliases = o_spec, o_shape, {len(args) - 1: 0}
    else:
        out_specs = [o_spec, st_spec]
        out_shape = [o_shape, jax.ShapeDtypeStruct((nseq, 2, H_B, HD_B, HD_B), F32)]
        aliases = {}
    return pl.pallas_call(
        functools.partial(_rwkv_scan_kernel, t=t, c=RWKV_CHUNK, npair=npair, latent=latent),
        grid=(nseq,),
        in_specs=in_specs, out_specs=out_specs, out_shape=out_shape,
        scratch_shapes=[pltpu.VMEM((t, MIX), F32)] * (N_PREP_OUT + 1),
        input_output_aliases=aliases,
        compiler_params=_params("parallel"),
        name="rwkv_lat" if latent else "rwkv_ctx",
    )(*args)


S5_L = 16
S5_ROWS = M_ALL // S5_L
S5_CW = S5_L * S5_GROUP
S5_GS = 8
S5_PW = 2 * P_C
S5_TAB = 24
S5_PRM = 2 * S5_GROUP
S5_NC_CTX = SEQ // S5_L
S5_NC_LAT = DEC_SEQ // S5_L
S5_CTX_ROWS = BATCH * S5_NC_CTX


def _gelu_tanh(x):
    return 0.5 * x * (1.0 + jnp.tanh(math.sqrt(2.0 / math.pi) * (x + 0.044715 * (x * x * x))))


def _cmul(ar, ai, br, bi):
    return ar * br - ai * bi, ar * bi + ai * br


def _dot3(a, b, dims):
    ah, al = _split2(a)
    bh, bl = _split2(b)
    d = lambda x, y: lax.dot_general(x, y, dims, preferred_element_type=F32)
    return d(ah, bh) + d(ah, bl) + d(al, bh)


def _s5_param_kernel(prm_ref, lam_ref, tab_ref, r_ref, st_ref):
    nl, ns = S5_L, S5_GROUP
    lane_blk = lax.broadcasted_iota(jnp.int32, (ns, S5_CW), 1) // ns
    kf = lax.broadcasted_iota(jnp.int32, (S5_TAB, P_C), 0).astype(F32)
    t_rows = [jnp.zeros((ns, S5_CW), F32) for _ in range(nl)]
    e_parts = []
    for d in range(2):
        lr, li = lam_ref[0, d, 0:1, :], lam_ref[0, d, 1:2, :]
        dt = jnp.exp(lam_ref[0, d, 2:3, :])
        mag = jnp.exp(lr * dt * kf)
        pw_re, pw_im = mag * jnp.cos(li * dt * kf), mag * jnp.sin(li * dt * kf)
        tab_ref[0, d, 0] = pw_re
        tab_ref[0, d, 1] = pw_im
        nr, ab_im, den = pw_re[1:2] - 1.0, pw_im[1:2], lr * lr + li * li
        f = ((nr * lr + ab_im * li) / den, (ab_im * lr - nr * li) / den)
        part = lambda lo, n: (prm_ref[0, d, 0, lo:lo + n, :], prm_ref[0, d, 1, lo:lo + n, :])
        bb = _cmul(*f, *part(0, ns))
        cc = part(ns, ns)
        pw = lambda k: (pw_re[k:k + 1], pw_im[k:k + 1])
        steps = range(nl)
        if d == 0:
            ke, ks, kk = [nl - 1 - j for j in steps], [j + 1 for j in steps], list(steps)
        else:
            ke, ks, kk = list(steps), [nl - j for j in steps], [nl - 1 - j for j in steps]
        stack = lambda xs: (jnp.concatenate([x[0] for x in xs], axis=0), jnp.concatenate([x[1] for x in xs], axis=0))
        e_re, e_im = stack([_cmul(*bb, *pw(k)) for k in ke])
        s_re, s_im = stack([_cmul(*cc, *pw(k)) for k in ks])
        k_re, k_im = stack([_cmul(*cc, *pw(k)) for k in kk])
        st_ref[0, d] = jnp.concatenate([s_re, -s_im], axis=1).astype(BF16)
        krow = _dot3(bb[0], k_re, NT) - _dot3(bb[1], k_im, NT)
        for j in steps:
            if d == 0:
                shifted, keep = pltpu.roll(krow, ns * j, axis=1), lane_blk >= j
            else:
                shifted, keep = pltpu.roll(krow, (S5_CW - ns * (nl - 1 - j)) % S5_CW, axis=1), lane_blk <= j
            t_rows[j] = t_rows[j] + jnp.where(keep, shifted, 0.0)
        e_parts += [e_re, e_im]
    r_ref[0] = jnp.concatenate([jnp.concatenate(t_rows, axis=0)] + e_parts, axis=1).astype(BF16)


def _s5_params(prm, lam):
    tab_spec = pl.BlockSpec((1, 2, 2, S5_TAB, P_C), lambda g: (g, 0, 0, 0, 0))
    return pl.pallas_call(
        _s5_param_kernel,
        grid=(G_C,),
        in_specs=[pl.BlockSpec((1, 2, 2, S5_PRM, P_C), lambda g: (g, 0, 0, 0, 0)),
                  pl.BlockSpec((1, 2, 8, P_C), lambda g: (g, 0, 0, 0))],
        out_specs=[tab_spec,
                   pl.BlockSpec((1, S5_CW, 2 * S5_CW), lambda g: (g, 0, 0)),
                   pl.BlockSpec((1, 2, S5_CW, S5_PW), lambda g: (g, 0, 0, 0))],
        out_shape=[jax.ShapeDtypeStruct((G_C, 2, 2, S5_TAB, P_C), F32),
                   jax.ShapeDtypeStruct((G_C, S5_CW, 2 * S5_CW), BF16),
                   jax.ShapeDtypeStruct((G_C, 2, S5_CW, S5_PW), BF16)],
        compiler_params=_params("parallel"),
        name="s5_params",
    )(prm, lam)


def _s5_core_kernel(u_ref, r_ref, st_ref, tab_ref, h0_ref, o_ref, hfin_ref, ug_scr, yg_scr, e_scr, hp_scr):
    ns, nl = S5_GROUP, S5_L
    per_v = 128 // ns
    blk = lax.broadcasted_iota(jnp.int32, (8, 128), 1) // ns

    def merge(select):
        acc = select(0)
        for b in range(1, per_v):
            acc = jnp.where(blk == b, select(b), acc)
        return acc

    def shuffle(srcs):
        rolled = []
        for s in range(per_v):
            m = merge(lambda b: srcs[(b + s) % per_v])
            rolled.append(m if s == 0 else pltpu.roll(m, s * ns, axis=1))
        return [merge(lambda b: rolled[(b - a) % per_v]) for a in range(per_v)]

    def tiles(regroup):
        def ctx_tile(ti, _):
            c, b0 = ti // (BATCH // 8), (ti % (BATCH // 8)) * 8
            regroup(b0 * SEQ + c * nl, SEQ, pl.multiple_of(c * BATCH + b0, 8))
            return 0

        def lat_tile(ti, _):
            b, c0 = ti // (S5_NC_LAT // 8), (ti % (S5_NC_LAT // 8)) * 8
            regroup(M_CTX + b * DEC_SEQ + c0 * nl, nl, pl.multiple_of(S5_CTX_ROWS + b * S5_NC_LAT + c0, 8))
            return 0

        lax.fori_loop(0, S5_CTX_ROWS // 8, ctx_tile, 0, unroll=4)
        lax.fori_loop(0, (S5_ROWS - S5_CTX_ROWS) // 8, lat_tile, 0, unroll=4)

    def regroup_in(tok, stride, row):
        uj = [u_ref[pl.ds(tok + j, 8, stride=stride), :] for j in range(nl)]
        for jh in range(nl // per_v):
            for g, v in enumerate(shuffle(uj[jh * per_v:(jh + 1) * per_v])):
                ug_scr[g, pl.ds(row, 8), jh * 128:(jh + 1) * 128] = v

    tiles(regroup_in)

    chains = [(g, d) for g in range(S5_GS) for d in range(2)]
    for g in range(S5_GS):
        a = jnp.dot(ug_scr[g].astype(BF16), r_ref[g], preferred_element_type=F32)
        yg_scr[g] = a[:, 0:S5_CW]
        for d in range(2):
            e_scr[g, d] = a[:, S5_CW + d * S5_PW:S5_CW + (d + 1) * S5_PW]
    lam = []
    for g, d in chains:
        lr, li = (tab_ref[g, d, ri, nl:nl + 1, :] for ri in range(2))
        lam.append((jnp.concatenate([lr, lr], axis=1), jnp.concatenate([-li, li], axis=1)))

    def advance(c, hs, latent):
        nc = S5_NC_LAT if latent else S5_NC_CTX
        out = []
        for i, (g, d) in enumerate(chains):
            cc = c if d == 0 else nc - 1 - c
            rows = (pl.ds(S5_CTX_ROWS + cc, DEC_BATCH, stride=nc) if latent
                    else pl.ds(pl.multiple_of(cc * BATCH, BATCH), BATCH))
            hp_scr[g, d, rows, :] = hs[i]
            out.append(lam[i][0] * hs[i] + lam[i][1] * pltpu.roll(hs[i], P_C, axis=1) + e_scr[g, d, rows, :])
        return out

    def both(c, carry):
        hc, hl = carry
        return tuple(advance(c, hc, False)), tuple(advance(c, hl, True))

    h_ctx = tuple(jnp.zeros((BATCH, S5_PW), F32) for _ in chains)
    h_lat = tuple(h0_ref[g, d] for g, d in chains)
    h_ctx, h_lat = lax.fori_loop(0, S5_NC_CTX, both, (h_ctx, h_lat), unroll=2)
    lax.fori_loop(S5_NC_CTX, S5_NC_LAT, lambda c, hl: tuple(advance(c, hl, True)), h_lat, unroll=4)
    for i, (g, d) in enumerate(chains):
        hfin_ref[g, d] = h_ctx[i]
    for g in range(S5_GS):
        y = yg_scr[g]
        for d in range(2):
            y = y + lax.dot_general(hp_scr[g, d].astype(BF16), st_ref[g, d], NT, preferred_element_type=F32)
        yg_scr[g] = y

    def regroup_out(tok, stride, row):
        for jh in range(nl // per_v):
            yv = [yg_scr[g, pl.ds(row, 8), jh * 128:(jh + 1) * 128] for g in range(S5_GS)]
            for jj, v in enumerate(shuffle(yv)):
                o_ref[pl.ds(tok + jh * per_v + jj, 8, stride=stride), :] = v

    tiles(regroup_out)


def _s5_core(p2, r, st, tab, h0):
    blk = lambda *tail: pl.BlockSpec((S5_GS,) + tail, lambda s: (s,) + (0,) * len(tail))
    col = pl.BlockSpec((M_ALL, S5_GS * S5_GROUP), lambda s: (0, s))
    return pl.pallas_call(
        _s5_core_kernel,
        grid=(G_C // S5_GS,),
        in_specs=[col, blk(S5_CW, 2 * S5_CW), blk(2, S5_CW, S5_PW),
                  blk(2, 2, S5_TAB, P_C), blk(2, DEC_BATCH, S5_PW)],
        out_specs=[col, blk(2, BATCH, S5_PW)],
        out_shape=[jax.ShapeDtypeStruct((M_ALL, MIX), F32),
                   jax.ShapeDtypeStruct((G_C, 2, BATCH, S5_PW), F32)],
        scratch_shapes=[pltpu.VMEM((S5_GS, S5_ROWS, S5_CW), F32),
                        pltpu.VMEM((S5_GS, S5_ROWS, S5_CW), F32),
                        pltpu.VMEM((S5_GS, 2, S5_ROWS, S5_PW), F32),
                        pltpu.VMEM((S5_GS, 2, S5_ROWS, S5_PW), F32)],
        compiler_params=_params("parallel"),
        name="s5_core",
    )(p2, r, st, tab, h0)


def _s5_tables(lam_re, lam_im, log_dt, b_re, b_im, c_re, c_im):
    rows = lambda b, c: jnp.concatenate([jnp.swapaxes(b, -1, -2), c], axis=2)
    prm = jnp.stack([rows(b_re, c_re), rows(b_im, c_im)], axis=1)
    dt = jnp.broadcast_to(log_dt[:, :, None], lam_re.shape)
    lam = jnp.stack([lam_re, lam_im, dt] + [jnp.zeros_like(dt)] * 5, axis=2)
    return jnp.transpose(prm, (2, 0, 1, 3, 4)), jnp.transpose(lam, (1, 0, 2, 3))


def _s5_mixer(p2, tables, state_re, state_im):
    tab, r, st = _s5_params(*tables)
    h0 = jnp.concatenate([state_re, state_im], axis=-1)
    y_tok, hfin = _s5_core(p2, r, st, tab, jnp.transpose(h0, (2, 1, 0, 3)))
    hfin = jnp.transpose(hfin, (2, 1, 0, 3))
    return y_tok, hfin[..., :P_C], hfin[..., P_C:]


def _ret_kernel(*refs, t, latent):
    if latent:
        (q_ref, k_ref, v_ref, g_ref, dl_ref, lnw_ref, lnb_ref, r0_ref, cos_ref, sin_ref, _,
         o_ref, ob_ref) = refs
    else:
        q_ref, k_ref, v_ref, g_ref, dl_ref, lnw_ref, lnb_ref, o_ref, rfin_ref, ob_ref = refs
    n = t // CHUNK
    w = HD_D
    jf = lax.broadcasted_iota(jnp.int32, (CHUNK, CHUNK), 0).astype(F32)
    kf = lax.broadcasted_iota(jnp.int32, (CHUNK, CHUNK), 1).astype(F32)
    diff = jf - kf
    tabs = []
    for dr in range(2):
        for h in range(H_D):
            l = -_softplus(-dl_ref[h, dr:dr + 1, :])
            if dr == 0:
                dmat = jnp.where(diff >= 0, jnp.exp(l * jnp.maximum(diff, 0.0)), 0.0)
                xi = jnp.exp(l * (jf + 1.0))
                zeta = jnp.exp(l * (CHUNK - 1.0 - jf))
            else:
                dmat = jnp.where(diff < 0, jnp.exp(l * jnp.maximum(-diff, 0.0)), 0.0)
                xi = jnp.exp(l * (CHUNK - jf))
                zeta = jnp.exp(l * jf)
            tabs.append((dmat, xi, zeta, jnp.exp(l * CHUNK)))
    chains = [(dr, h) for dr in range(2) for h in range(H_D)]

    def body(ci, states):
        q, k, v, where = [], [], [], []
        for dr, h in chains:
            cj = ci if dr == 0 else n - 1 - ci
            rows = pl.ds(pl.multiple_of(cj * CHUNK, CHUNK), CHUNK)
            cols = slice(h * w, (h + 1) * w)
            qi = q_ref[rows, cols]
            ki = k_ref[rows, cols] * (HD_D ** -0.5)
            if latent:
                qi = _rope_pairs(qi, cos_ref[rows, :], sin_ref[rows, :], HD_D // 2)
                ki = _rope_pairs(ki, cos_ref[rows, :], sin_ref[rows, :], HD_D // 2)
            q.append(qi.astype(BF16))
            k.append(ki)
            v.append(v_ref[rows, cols].astype(BF16))
            where.append((o_ref if dr == 0 else ob_ref, rows, cols))
        nch = range(len(chains))
        inner = [(_bdot(q[i], k[i], NT) * tabs[i][0]).astype(BF16) for i in nch]
        cross = [_bdot(q[i], states[i]) * tabs[i][1] for i in nch]
        kz = [(k[i] * tabs[i][2]).astype(BF16) for i in nch]
        out = [jnp.dot(inner[i], v[i], preferred_element_type=F32) + cross[i] for i in nch]
        new_states = [states[i] * tabs[i][3] + lax.dot_general(kz[i], v[i], TN, preferred_element_type=F32)
                      for i in nch]
        for (dst, rows, cols), y in zip(where, out):
            dst[rows, cols] = y
        return tuple(new_states)

    if latent:
        lax.fori_loop(0, n, body, tuple(r0_ref[0, dr, h] for dr, h in chains), unroll=2)
    else:
        fin = lax.fori_loop(0, n, body, tuple(jnp.zeros((w, w), F32) for _ in chains), unroll=True)
        for i, (dr, h) in enumerate(chains):
            rfin_ref[0, dr, h] = fin[i]
    for h in range(H_D):
        cols = slice(h * w, (h + 1) * w)
        y = o_ref[:, cols] + ob_ref[:, cols]
        yc = y - jnp.mean(y, axis=-1, keepdims=True)
        yn = yc * lax.rsqrt(jnp.mean(yc * yc, axis=-1, keepdims=True) + RET_GN_EPS)
        o_ref[:, cols] = (yn * lnw_ref[:, cols] + lnb_ref[:, cols]) * _silu(g_ref[:, cols])


def _retention(p, decay_logit, ln_w, ln_b, latent, r0=None, rope=None, dst=None):
    if latent:
        nseq, t, rb0 = DEC_BATCH, DEC_SEQ, M_CTX // DEC_SEQ
    else:
        nseq, t, rb0 = BATCH, SEQ, 0
    w = HD_D
    dl = jnp.broadcast_to(decay_logit.T[:, :, None], (H_D, 2, w))
    full = lambda shape: pl.BlockSpec(shape, lambda b: (0,) * len(shape))
    st_spec = pl.BlockSpec((1, 2, H_D, w, w), lambda b: (b, 0, 0, 0, 0))
    in_specs = [pl.BlockSpec((t, MIX), lambda b, j=j: (rb0 + b, j)) for j in (1, 2, 3, 4)]
    in_specs += [full((H_D, 2, w)), full((1, MIX)), full((1, MIX))]
    args = [p, p, p, p, dl, ln_w.reshape(1, MIX), ln_b.reshape(1, MIX)]
    o_spec = pl.BlockSpec((t, MIX), lambda b: (rb0 + b, 0))
    o_shape = jax.ShapeDtypeStruct((M_ALL, MIX), F32)
    if latent:
        in_specs += [st_spec, full((t, w)), full((t, w)), pl.BlockSpec(memory_space=pl.ANY)]
        args += [r0, *rope, dst]
        out_specs, out_shape, aliases = o_spec, o_shape, {len(args) - 1: 0}
    else:
        out_specs = [o_spec, st_spec]
        out_shape = [o_shape, jax.ShapeDtypeStruct((nseq, 2, H_D, w, w), F32)]
        aliases = {}
    return pl.pallas_call(
        functools.partial(_ret_kernel, t=t, latent=latent),
        grid=(nseq,),
        in_specs=in_specs, out_specs=out_specs, out_shape=out_shape,
        scratch_shapes=[pltpu.VMEM((t, MIX), F32)],
        input_output_aliases=aliases,
        compiler_params=_params("parallel"),
        name="retention_lat" if latent else "retention_ctx",
    )(*args)


def _rope_tables(n_tok, dim, reps):
    rows = n_tok // GRID_W
    n_freq = dim // 4
    inv = 1.0 / (ROPE_THETA ** (jnp.arange(n_freq, dtype=F32) / n_freq))
    row = jnp.repeat(jnp.arange(rows, dtype=F32), GRID_W)
    col = jnp.tile(jnp.arange(GRID_W, dtype=F32), rows)
    ang = jnp.concatenate([row[:, None] * inv, col[:, None] * inv], axis=-1)
    cos, sin = jnp.cos(ang), jnp.sin(ang)
    return jnp.tile(jnp.concatenate([cos, cos], axis=1), (1, reps)), \
        jnp.tile(jnp.concatenate([-sin, sin], axis=1), (1, reps))


def kernel(x_prompt, x_sample, cache_k_ab, cache_v_ab, state_rwkv, state_s5_re, state_s5_im, state_ret, c, c_ctx, norm1_g, norm2_g, w_mod, b_mod, w_ff_gate, w_ff_up, w_ff_down, w_in_ab, w_out_ab, qk_gain_a, lambda_qk, subln_g, rwkv_mu, rwkv_k_k, rwkv_k_a, rwkv_r_k, rwkv_w0, rwkv_w_up, rwkv_a0, rwkv_a_up, rwkv_g_up, rwkv_ln_w, rwkv_ln_b, w_in_cd, w_out_cd, s5_lam_re, s5_lam_im, s5_log_dt, s5_b_re, s5_b_im, s5_c_re, s5_c_im, s5_d, s5_w_glu, ret_decay_logit, ret_ln_w, ret_ln_b):
    d = D_MODEL
    xs = [x_prompt.reshape(M_CTX, d), x_sample.reshape(M_LAT, d)]
    cvec = jnp.zeros((MOD_ROWS, d), F32).at[0].set(c_ctx).at[1:1 + DEC_BATCH].set(c)
    mods = _modulation(cvec, w_mod, b_mod)

    lam_init = 0.8 - 0.6 * math.exp(-0.3 * 0)
    p = _norm_linear(xs, norm1_g[0], mods[0], w_in_ab)
    gain2 = jnp.tile(qk_gain_a[0], (1, 2))
    rope_a = _rope_tables(DEC_SEQ, HD_A, 2)
    ck = cache_k_ab[:, 0].reshape(DEC_BATCH * PAST_LEN, MIX)
    cv = cache_v_ab[:, 0].reshape(DEC_BATCH * PAST_LEN, MIX)
    oa, k_ctx, v_ctx = _attention(p, gain2, lambda_qk[0], subln_g[0], lam_init, latent=False)
    oa = _attention(p, gain2, lambda_qk[0], subln_g[0], lam_init, latent=True,
                    cache_k=ck, cache_v=cv, rope=rope_a, dst=oa)
    rw_prm = (rwkv_mu[0], rwkv_k_k[0], rwkv_k_a[0], rwkv_r_k[0], rwkv_w0[0], rwkv_w_up[0],
              rwkv_a0[0], rwkv_a_up[0], rwkv_g_up[0])
    ob, sfin_ctx = _rwkv(p, rw_prm, rwkv_ln_w[0], rwkv_ln_b[0], latent=False)
    ob = _rwkv(p, rw_prm, rwkv_ln_w[0], rwkv_ln_b[0], latent=True, s0=state_rwkv[:, 0], dst=ob)
    x = _mix_ffn(xs, oa, ob, w_out_ab[0].astype(BF16), norm2_g[0], mods[0], w_ff_gate[0].astype(BF16),
                 w_ff_up[0].astype(BF16), w_ff_down[0].astype(BF16), split_out=False)

    p2 = _norm_linear([x], norm1_g[1], mods[1], w_in_cd)
    tables = _s5_tables(s5_lam_re[0], s5_lam_im[0], s5_log_dt[0], s5_b_re[0], s5_b_im[0],
                        s5_c_re[0], s5_c_im[0])
    y_s5, s5_fin_re, s5_fin_im = _s5_mixer(p2, tables, state_s5_re[:, 0], state_s5_im[:, 0])
    oc = (y_s5, p2, s5_d[0], s5_w_glu[0].astype(BF16))
    rope_d = _rope_tables(DEC_SEQ, HD_D, 1)
    od, rfin = _retention(p2, ret_decay_logit[0], ret_ln_w[0], ret_ln_b[0], latent=False)
    od = _retention(p2, ret_decay_logit[0], ret_ln_w[0], ret_ln_b[0], latent=True, r0=state_ret[:, 0],
                    rope=rope_d, dst=od)
    y_ctx, y_lat = _mix_ffn([x], oc, od, w_out_cd[0].astype(BF16), norm2_g[1], mods[1],
                            w_ff_gate[1].astype(BF16), w_ff_up[1].astype(BF16), w_ff_down[1].astype(BF16),
                            split_out=True)
    y_prompt = y_ctx.reshape(BATCH, SEQ, d)
    y_sample = y_lat.reshape(DEC_BATCH, DEC_SEQ, d)
    new_k = jnp.transpose(k_ctx, (0, 4, 1, 2, 3))[:, None]
    new_v = v_ctx.reshape(BATCH, 1, SEQ, H_A, VD_A)
    new_rwkv = sfin_ctx[:, None]
    new_s5_re = s5_fin_re[:, None]
    new_s5_im = s5_fin_im[:, None]
    new_ret = rfin[:, None]
    return (y_prompt, y_sample, new_k, new_v, new_rwkv, new_s5_re, new_s5_im, new_ret)
```

```python
import functools
import math

import numpy as np
import jax
import jax.numpy as jnp
from jax import lax
from jax.experimental import pallas as pl
from jax.experimental.pallas import tpu as pltpu

F32 = jnp.float32
BF16 = jnp.bfloat16
HIGHEST = lax.Precision.HIGHEST

D_MODEL = 1024
BATCH = 32
SEQ = 256
DEC_BATCH = 2
DEC_SEQ = 1024
PAST_LEN = 256
GRID_W = 64
H_A = 4
HD_A = 64
VD_A = 128
H_B = 8
HD_B = 64
MIX = 512
LORA_W = 64
LORA_A = 64
LORA_G = 128
S5_GROUP = 16
G_C = 32
P_C = 64
S5_STATE = G_C * P_C
H_D = 4
HD_D = 128
CHUNK = 128
D_FF = 2816
IN_AB = 3328
IN_CD = 2560
ROPE_THETA = 10000.0
NORM_EPS = 1e-6
RWKV_GN_EPS = 64e-5
RET_GN_EPS = 1e-5

M_CTX = BATCH * SEQ
M_LAT = DEC_BATCH * DEC_SEQ
M_ALL = M_CTX + M_LAT
MOD_ROWS = 8
RWKV_CHUNK = 64
VMEM_LIMIT = 56 * 1024 * 1024

NN = (((1,), (0,)), ((), ()))
NT = (((1,), (1,)), ((), ()))
TN = (((0,), (0,)), ((), ()))


def _params(*sem):
    return pltpu.CompilerParams(dimension_semantics=sem, vmem_limit_bytes=VMEM_LIMIT)


def _bdot(a, b, dims=NN):
    return lax.dot_general(a.astype(BF16), b.astype(BF16), dims, preferred_element_type=F32)


def _hdot(a, b, dims=NN):
    return lax.dot_general(a, b, dims, precision=HIGHEST, preferred_element_type=F32)


def _split_dot(x, m):
    hi = x.astype(BF16)
    lo = (x - hi.astype(F32)).astype(BF16)
    return (jnp.dot(hi, m, preferred_element_type=F32) + jnp.dot(lo, m, preferred_element_type=F32))


def _seg_matrix(n, shift, val):
    r = lax.broadcasted_iota(jnp.int32, (n, n), 0) >> shift
    c = lax.broadcasted_iota(jnp.int32, (n, n), 1) >> shift
    return jnp.where(r == c, val, 0.0).astype(BF16)


def _sigmoid(x):
    return jax.nn.sigmoid(x)


def _silu(x):
    return x * jax.nn.sigmoid(x)


def _softplus(x):
    return jnp.maximum(x, 0.0) + jnp.log(1.0 + jnp.exp(-jnp.abs(x)))


def _mod_row(tile, tm):
    r0 = tile * tm
    return jnp.where(r0 < M_CTX, 0, 1 + (r0 - M_CTX) // DEC_SEQ)


def _norm_mod(x, g, sc_ref, sh_ref, row):
    y = x * lax.rsqrt(jnp.mean(x * x, axis=-1, keepdims=True) + NORM_EPS) * g
    return y * (1.0 + sc_ref[pl.ds(row, 1), :]) + sh_ref[pl.ds(row, 1), :]


def _mod_kernel(c_ref, w_ref, b_ref, o_ref):
    o_ref[0] = _bdot(_silu(c_ref[...]), w_ref[0]) + b_ref[0]


def _modulation(cvec, w_mod, b_mod):
    depth, d, n6 = w_mod.shape
    tn = 1536
    return pl.pallas_call(
        _mod_kernel,
        grid=(depth, n6 // tn),
        in_specs=[pl.BlockSpec((MOD_ROWS, d), lambda l, j: (0, 0)),
                  pl.BlockSpec((1, d, tn), lambda l, j: (l, 0, j)),
                  pl.BlockSpec((1, 1, tn), lambda l, j: (l, 0, j))],
        out_specs=pl.BlockSpec((1, MOD_ROWS, tn), lambda l, j: (l, 0, j)),
        out_shape=jax.ShapeDtypeStruct((depth, MOD_ROWS, n6), F32),
        compiler_params=_params("parallel", "parallel"),
        name="modulation",
    )(cvec, w_mod, b_mod.reshape(depth, 1, n6))


def _tile_specs(tm, d):
    nc = M_CTX // tm
    return [pl.BlockSpec((tm, d), lambda i: (jnp.minimum(i, nc - 1), 0)),
            pl.BlockSpec((tm, d), lambda i: (jnp.maximum(i - nc, 0), 0))]


def _pick_rows(refs, tm):
    if len(refs) == 1:
        return refs[0][...]
    return jnp.where(pl.program_id(0) < M_CTX // tm, refs[0][...], refs[1][...])


def _norm_linear_kernel(*refs, tm):
    *x_refs, g_ref, sc_ref, sh_ref, w_ref, o_ref, wbf = refs

    @pl.when(pl.program_id(0) == 0)
    def _():
        wbf[...] = w_ref[0].astype(BF16)

    row = _mod_row(pl.program_id(0), tm)
    x = _pick_rows(x_refs, tm)
    nsub = 2
    rows = [slice(i * tm // nsub, (i + 1) * tm // nsub) for i in range(nsub)]
    hs = [_norm_mod(x[r], g_ref[...], sc_ref, sh_ref, row).astype(BF16) for r in rows]
    for r, h in zip(rows, hs):
        o_ref[r, :] = jnp.dot(h, wbf[...], preferred_element_type=F32)


def _norm_linear(xs, g, mods, w):
    tm = 512
    d = D_MODEL
    n = w.shape[2]
    x_specs = _tile_specs(tm, d) if len(xs) == 2 else [pl.BlockSpec((tm, d), lambda i: (i, 0))]
    return pl.pallas_call(
        functools.partial(_norm_linear_kernel, tm=tm),
        grid=(M_ALL // tm,),
        in_specs=x_specs + [pl.BlockSpec((1, d), lambda i: (0, 0)),
                            pl.BlockSpec((MOD_ROWS, d), lambda i: (0, 1)),
                            pl.BlockSpec((MOD_ROWS, d), lambda i: (0, 0)),
                            pl.BlockSpec((1, d, n), lambda i: (0, 0, 0), pipeline_mode=pl.Buffered(1))],
        out_specs=pl.BlockSpec((tm, n), lambda i: (i, 0)),
        out_shape=jax.ShapeDtypeStruct((M_ALL, n), F32),
        scratch_shapes=[pltpu.VMEM((d, n), BF16)],
        compiler_params=_params("arbitrary"),
        name="norm_linear",
    )(*xs, g.reshape(1, d), mods, mods, w)


def _mix_ffn_kernel(*refs, tm, ck, n_in, n_out, glu):
    x_refs, refs = refs[:n_in], refs[n_in:]
    if glu:
        (y_ref, u_ref, d_ref, wglu_ref), refs = refs[:4], refs[4:]
        z = _gelu_tanh(y_ref[...] + d_ref[...] * u_ref[...])
        a = z * _sigmoid(jnp.dot(z.astype(BF16), wglu_ref[...], preferred_element_type=F32))
    else:
        a, refs = refs[0][...], refs[1:]
    b_ref, wa_ref, wb_ref, g1_ref, ng_ref, sc_ref, sh_ref, g2_ref, wg_ref, wu_ref, wd_ref = refs[:11]
    o_refs = refs[11:]
    row = _mod_row(pl.program_id(0), tm)
    mix = (jnp.dot(a.astype(BF16), wa_ref[...], preferred_element_type=F32)
           + jnp.dot(b_ref[...].astype(BF16), wb_ref[...], preferred_element_type=F32))
    x = _pick_rows(x_refs, tm) + g1_ref[pl.ds(row, 1), :] * mix
    h = _norm_mod(x, ng_ref[...], sc_ref, sh_ref, row).astype(BF16)
    acc = jnp.zeros((tm, D_MODEL), F32)
    for c in range(D_FF // ck):
        gg = jnp.dot(h, wg_ref[0, :, c * ck:(c + 1) * ck], preferred_element_type=F32)
        uu = jnp.dot(h, wu_ref[0, :, c * ck:(c + 1) * ck], preferred_element_type=F32)
        act = (_silu(gg) * uu).astype(BF16)
        acc = acc + jnp.dot(act, wd_ref[0, c * ck:(c + 1) * ck, :], preferred_element_type=F32)
    y = x + g2_ref[pl.ds(row, 1), :] * acc
    if n_out == 1:
        o_refs[0][...] = y
    else:
        is_ctx = pl.program_id(0) < M_CTX // tm

        @pl.when(is_ctx)
        def _():
            o_refs[0][...] = y

        @pl.when(jnp.logical_not(is_ctx))
        def _():
            o_refs[1][...] = y


def _mix_ffn(xs, oa, ob, w_out_bf16, norm_g, mods, wg, wu, wd, layer, split_out):
    tm, ck = 512, 256
    d = D_MODEL
    row_spec = lambda n: pl.BlockSpec((tm, n), lambda i: (i, 0))
    const = lambda shape, idx: pl.BlockSpec(shape, lambda i: idx, pipeline_mode=pl.Buffered(1))
    mod = lambda j: pl.BlockSpec((MOD_ROWS, d), lambda i: (0, j))
    x_specs = _tile_specs(tm, d) if len(xs) == 2 else [row_spec(d)]
    glu = isinstance(oa, tuple)
    if glu:
        y_tok, p2, d_skip, w_glu = oa
        a_specs = [row_spec(MIX), row_spec(MIX), pl.BlockSpec((1, MIX), lambda i: (0, 0)), const((MIX, MIX), (0, 0))]
        a_args = [y_tok, p2, d_skip.reshape(1, MIX), w_glu]
    else:
        a_specs, a_args = [row_spec(MIX)], [oa]
    if split_out:
        out_specs = _tile_specs(tm, d)
        out_shape = [jax.ShapeDtypeStruct((M_CTX, d), F32), jax.ShapeDtypeStruct((M_LAT, d), F32)]
    else:
        out_specs, out_shape = row_spec(d), jax.ShapeDtypeStruct((M_ALL, d), F32)
    return pl.pallas_call(
        functools.partial(_mix_ffn_kernel, tm=tm, ck=ck, n_in=len(xs), n_out=2 if split_out else 1, glu=glu),
        grid=(M_ALL // tm,),
        in_specs=x_specs + a_specs + [row_spec(MIX),
                            const((MIX, d), (0, 0)), const((MIX, d), (1, 0)),
                            mod(2),
                            pl.BlockSpec((1, d), lambda i: (0, 0)),
                            mod(4), mod(3), mod(5),
                            const((1, d, D_FF), (layer, 0, 0)), const((1, d, D_FF), (layer, 0, 0)),
                            const((1, D_FF, d), (layer, 0, 0))],
        out_specs=out_specs, out_shape=out_shape,
        compiler_params=_params("arbitrary"),
        name="mix_ffn",
    )(*xs, *a_args, ob, w_out_bf16, w_out_bf16, mods, norm_g.reshape(1, d), mods, mods, mods, wg, wu, wd)


def _qk_norm(x, gain, segm):
    ms = _split_dot(x * x, segm)
    return x * lax.rsqrt(ms + NORM_EPS) * gain


def _rope_pairs(x, cosf, sinf, half):
    lane = lax.broadcasted_iota(jnp.int32, x.shape, 1)
    first = (lane & (2 * half - 1)) < half
    n = x.shape[1]
    partner = jnp.where(first, pltpu.roll(x, n - half, axis=1), pltpu.roll(x, half, axis=1))
    return x * cosf + partner * sinf


def _attn_kernel(*refs, latent, lam_init):
    if latent:
        (q_ref, k_ref, v_ref, ck_ref, cv_ref, cosq_ref, sinq_ref, cosk_ref, sink_ref,
         gain_ref, lam_ref, sub_ref, _, o_ref, kall, vall) = refs
    else:
        q_ref, k_ref, v_ref, gain_ref, lam_ref, sub_ref, o_ref, kn_ref, vo_ref, kall, vall = refs
    w = 2 * HD_A
    segm = _seg_matrix(w, 6, 1.0 / HD_A)
    gains = gain_ref[...]
    cols = [slice(h * w, (h + 1) * w) for h in range(H_A)]

    @pl.when(pl.program_id(1) == 0)
    def _():
        k = [_qk_norm(k_ref[:, c], gains[1:2], segm) for c in cols]
        if latent:
            k = [_rope_pairs(x, cosk_ref[...], sink_ref[...], HD_A // 2) for x in k]
            for c, x in zip(cols, k):
                kall[0:PAST_LEN, c] = ck_ref[:, c].astype(BF16)
                kall[PAST_LEN:, c] = x.astype(BF16)
                vall[0:PAST_LEN, c] = cv_ref[:, c].astype(BF16)
                vall[PAST_LEN:, c] = v_ref[:, c].astype(BF16)
        else:
            for h, (c, x) in enumerate(zip(cols, k)):
                xt = x.T
                kn_ref[0, h, 0] = xt[:HD_A]
                kn_ref[0, h, 1] = xt[HD_A:]
                vo_ref[pl.ds(h, v_ref.shape[0], stride=H_A), :] = v_ref[:, c]
                kall[:, c] = x.astype(BF16)
                vall[:, c] = v_ref[:, c].astype(BF16)

    q = [_qk_norm(q_ref[:, c], gains[0:1], segm) for c in cols]
    if latent:
        q = [_rope_pairs(x, cosq_ref[...], sinq_ref[...], HD_A // 2) for x in q]
    lv = lam_ref[...]
    lam = (jnp.exp(jnp.sum(lv[0:1] * lv[1:2], axis=1, keepdims=True))
           - jnp.exp(jnp.sum(lv[2:3] * lv[3:4], axis=1, keepdims=True)) + lam_init)
    scale = HD_A ** -0.5
    comp0 = lax.broadcasted_iota(jnp.int32, q[0].shape, 1) < HD_A
    qc = [jnp.where(comp0, *sel).astype(BF16) for x in q for sel in ((x, 0.0), (0.0, x))]
    s = [lax.dot_general(qc[i], kall[:, cols[i // 2]], NT, preferred_element_type=F32) * scale
         for i in range(2 * H_A)]
    e = [jnp.exp(x - jnp.max(x, axis=-1, keepdims=True)) for x in s]
    p = [x / jnp.sum(x, axis=-1, keepdims=True) for x in e]
    att = [(p[2 * h] - lam * p[2 * h + 1]).astype(BF16) for h in range(H_A)]
    o = [jnp.dot(att[h], vall[:, cols[h]], preferred_element_type=F32) for h in range(H_A)]
    o = [x * lax.rsqrt(jnp.mean(x * x, axis=-1, keepdims=True) + NORM_EPS) * sub_ref[...] for x in o]
    for c, x in zip(cols, o):
        o_ref[:, c] = x * (1.0 - lam_init)


def _attention(p, gain2, lambda_qk, subln_g, lam_init, latent, cache_k=None, cache_v=None, rope=None,
               dst=None):
    w = 2 * HD_A
    if latent:
        nseq, t, tq, rb0, s_len = DEC_BATCH, DEC_SEQ, 128, M_CTX // DEC_SEQ, PAST_LEN + DEC_SEQ
    else:
        nseq, t, tq, rb0, s_len = BATCH, SEQ, SEQ, 0, SEQ
    nq = t // tq
    qoff = rb0 * nq
    full = lambda shape: pl.BlockSpec(shape, lambda b, i: (0,) * len(shape))
    in_specs = [pl.BlockSpec((tq, MIX), lambda b, i: (qoff + b * nq + i, 0)),
                pl.BlockSpec((t, MIX), lambda b, i: (rb0 + b, 1)),
                pl.BlockSpec((t, MIX), lambda b, i: (rb0 + b, 2))]
    args = [p, p, p]
    if latent:
        cosf, sinf = rope
        in_specs += [pl.BlockSpec((PAST_LEN, MIX), lambda b, i: (b, 0)),
                     pl.BlockSpec((PAST_LEN, MIX), lambda b, i: (b, 0)),
                     pl.BlockSpec((tq, w), lambda b, i: (i, 0)),
                     pl.BlockSpec((tq, w), lambda b, i: (i, 0)),
                     full((t, w)), full((t, w))]
        args += [cache_k, cache_v, cosf, sinf, cosf, sinf]
    in_specs += [full((2, w)), full((4, HD_A)), full((1, w))]
    args += [gain2, lambda_qk, subln_g.reshape(1, w)]
    o_spec = pl.BlockSpec((tq, MIX), lambda b, i: (qoff + b * nq + i, 0))
    o_shape = jax.ShapeDtypeStruct((M_ALL, MIX), F32)
    aliases = {}
    if latent:
        out_specs, out_shape = o_spec, o_shape
        in_specs.append(pl.BlockSpec(memory_space=pl.ANY))
        args.append(dst)
        aliases = {len(args) - 1: 0}
    else:
        out_specs = [o_spec, pl.BlockSpec((1, H_A, 2, HD_A, t), lambda b, i: (b, 0, 0, 0, 0)),
                     pl.BlockSpec((t * H_A, VD_A), lambda b, i: (b, 0))]
        out_shape = [o_shape, jax.ShapeDtypeStruct((nseq, H_A, 2, HD_A, t), F32),
                     jax.ShapeDtypeStruct((nseq * t * H_A, VD_A), F32)]
    return pl.pallas_call(
        functools.partial(_attn_kernel, latent=latent, lam_init=lam_init),
        grid=(nseq, nq),
        in_specs=in_specs, out_specs=out_specs, out_shape=out_shape,
        scratch_shapes=[pltpu.VMEM((s_len, MIX), BF16), pltpu.VMEM((s_len, MIX), BF16)],
        input_output_aliases=aliases,
        compiler_params=_params("parallel", "arbitrary"),
        name="diff_attention_lat" if latent else "diff_attention_ctx",
    )(*args)


def _centred_shift(x, mu):
    t = x.shape[0]
    row = lax.broadcasted_iota(jnp.int32, x.shape, 0)
    prev = jnp.where(row == 0, 0.0, pltpu.roll(x, 1, axis=0))
    nxt = jnp.where(row == t - 1, 0.0, pltpu.roll(x, t - 1, axis=0))
    return x + (0.5 * (prev + nxt) - x) * mu


def _seg_sum(x, segm):
    return jnp.concatenate([_split_dot(x[:, j * 128:(j + 1) * 128], segm) for j in range(x.shape[1] // 128)],
                           axis=1)


def _rwkv_prep_kernel(r_ref, k_ref, v_ref, l_ref, mur_ref, muk_ref, muv_ref, mul_ref,
                      kk_ref, ka_ref, rk_ref, w0_ref, wup_ref, a0_ref, aup_ref, gup_ref,
                      ro_ref, ldf_ref, ldb_ref, kbo_ref, vbo_ref, kko_ref, ao_ref, gate_ref, bonus_ref):
    seg1 = _seg_matrix(128, 6, 1.0)
    r = _centred_shift(r_ref[...], mur_ref[...])
    kb = _centred_shift(k_ref[...], muk_ref[...])
    vb = _centred_shift(v_ref[...], muv_ref[...])
    lo = _centred_shift(l_ref[...], mul_ref[...])
    xw = lo[:, 0:LORA_W]
    xa = lo[:, LORA_W:LORA_W + LORA_A]
    xg = lo[:, LORA_W + LORA_A:]
    kk = kb * kk_ref[...]
    kk = kk * lax.rsqrt(_seg_sum(kk * kk, seg1) + 1e-12)
    a = _sigmoid(a0_ref[...] + _bdot(xa, aup_ref[...]))
    kb2 = kb * (1.0 + (a - 1.0) * ka_ref[...])
    lw = jnp.tanh(xw)
    for dr, ld_ref in enumerate((ldf_ref, ldb_ref)):
        z = w0_ref[dr:dr + 1, :] + _bdot(lw, wup_ref[dr])
        logw = -_softplus(-z) - 0.5
        ld_ref[...] = -jnp.exp(logw)
    gate_ref[...] = _bdot(_sigmoid(xg), gup_ref[...])
    bonus_ref[...] = _seg_sum(r * kb2 * rk_ref[...], seg1) * vb
    ro_ref[...] = r
    kbo_ref[...] = kb2
    vbo_ref[...] = vb
    kko_ref[...] = kk
    ao_ref[...] = a


N_PREP_IN = 16
N_PREP_OUT = 9


def _split2(x):
    hi = x.astype(BF16)
    return hi, (x - hi.astype(F32)).astype(BF16)


RWKV_INV_BLOCK = 16
RWKV_UNROLL_MAX = 4


def _rwkv_masks(c, rev):
    ti = lax.broadcasted_iota(jnp.int32, (c, c), 0)
    si = lax.broadcasted_iota(jnp.int32, (c, c), 1)
    tri = jnp.where((si >= ti) if rev else (si <= ti), 1.0, 0.0).astype(BF16)
    t4 = lax.broadcasted_iota(jnp.int32, (4 * c, 4 * c), 0)
    s4 = lax.broadcasted_iota(jnp.int32, (4 * c, 4 * c), 1)
    tm, sm = t4 & (c - 1), s4 & (c - 1)
    strict = (sm > tm) if rev else (sm < tm)
    incl = (sm >= tm) if rev else (sm <= tm)
    same_head = ((t4 // c) & 1) == ((s4 // c) & 1)
    top = t4 < 2 * c
    gmask = same_head & ((top & strict) | (~top & incl))
    t2 = lax.broadcasted_iota(jnp.int32, (2 * c, 2 * c), 0)
    s2 = lax.broadcasted_iota(jnp.int32, (2 * c, 2 * c), 1)
    same = lambda n: (t2 // n) == (s2 // n)
    levels = []
    n = RWKV_INV_BLOCK
    while n < c:
        levels.append(same(2 * n) & ~same(n))
        n *= 2
    f = lambda m: jnp.where(m, 1.0, 0.0)
    return tri, f(gmask), f(same(RWKV_INV_BLOCK)), tuple(f(m) for m in levels), f(same(c)), f(t2 == s2)


def _keep(mask01, x):
    return jnp.where(mask01 > 0.5, x, 0.0)


def _tri_inverse(a, diag_blk, levels, eye):
    n = a[0].shape[0]
    d = [_keep(diag_blk, x) for x in a]
    t = [eye + x for x in d]
    p = [_bdot(x, x) for x in d]
    for _ in range(int(math.log2(RWKV_INV_BLOCK)) - 2):
        res = [_bdot(jnp.concatenate([pi, ti], axis=0), pi) for pi, ti in zip(p, t)]
        p = [x[:n] for x in res]
        t = [ti + x[n:] for ti, x in zip(t, res)]
    t = [ti + _bdot(ti, pi) for ti, pi in zip(t, p)]
    for off in levels:
        half = [_bdot(ti, _keep(off, x)) for ti, x in zip(t, a)]
        t = [ti + _bdot(x, ti) for ti, x in zip(t, half)]
    return t


def _rwkv_pair_chunks(ins, sts, tris, gmasks, diag_blk, levels, same_head, eye):
    c, w = ins[0][0].shape
    hd = w // 2
    nch = range(len(ins))
    r, ld, kb, vb, kk, a = (list(z) for z in zip(*ins))
    split = [_split2(x) for x in ld]
    lcum = [jnp.dot(tris[i], jnp.concatenate(split[i], axis=1), preferred_element_type=F32) for i in nch]
    lcum = [x[:, :w] + x[:, w:] for x in lcum]
    ltot = [jnp.sum(x, axis=0, keepdims=True) for x in ld]
    beta = [kk[i] * a[i] for i in nch]
    eneg = [jnp.exp(-x) for x in lcum]
    abar = [-kk[i] * jnp.exp(lcum[i] - ld[i]) for i in nch]
    rbar = [r[i] * jnp.exp(lcum[i]) for i in nch]
    bt = [(beta[i] * eneg[i]).astype(BF16) for i in nch]
    kt = [(kb[i] * eneg[i]).astype(BF16) for i in nch]
    vbb = [x.astype(BF16) for x in vb]
    head0 = lax.broadcasted_iota(jnp.int32, (c, w), 1) < hd
    pick = lambda res: jnp.where(head0, res[:c], res[c:])
    arst = [_bdot(jnp.concatenate([abar[i], rbar[i]], axis=0), sts[i], NT) for i in nch]
    lhs = [jnp.concatenate([jnp.where(head0, abar[i], 0.0), jnp.where(head0, 0.0, abar[i]),
                            jnp.where(head0, rbar[i], 0.0), jnp.where(head0, 0.0, rbar[i])], axis=0) for i in nch]
    g = [_keep(gmasks[i], _bdot(lhs[i], jnp.concatenate([bt[i], bt[i], kt[i], kt[i]], axis=0), NT))
         for i in nch]
    x = [arst[i][:c] + pick(_bdot(g[i][:2 * c, 2 * c:], jnp.concatenate([vbb[i], vbb[i]], axis=0))) for i in nch]
    tinv = _tri_inverse([gi[:2 * c, :2 * c] for gi in g], diag_blk, levels, eye)
    u = [pick(_bdot(tinv[i], jnp.concatenate([x[i], x[i]], axis=0))) for i in nch]
    ub = [z.astype(BF16) for z in u]
    y = [arst[i][c:] + pick(_bdot(g[i][2 * c:], jnp.concatenate([ub[i], ub[i], vbb[i], vbb[i]], axis=0)))
         for i in nch]
    erem = [jnp.exp(ltot[i] - lcum[i]) for i in nch]
    bkh = [jnp.concatenate([beta[i] * erem[i], kb[i] * erem[i]], axis=0) for i in nch]
    st_new = [jnp.exp(ltot[i]) * sts[i]
              + _keep(same_head, _bdot(jnp.concatenate([ub[i], vbb[i]], axis=0), bkh[i], TN)) for i in nch]
    return y, st_new


def _rwkv_scan_kernel(*refs, t, c, npair, latent):
    prep_in, (lnw_ref, lnb_ref), rest = refs[:N_PREP_IN], refs[N_PREP_IN:N_PREP_IN + 2], refs[N_PREP_IN + 2:]
    if latent:
        s0_ref, _, o_ref, *scratch = rest
    else:
        o_ref, sfin_ref, *scratch = rest
    *prep_out, yb_ref = scratch
    _rwkv_prep_kernel(*prep_in, *prep_out)
    r_ref, ldf_ref, ldb_ref, kb_ref, vb_ref, kk_ref, a_ref, gate_ref, bonus_ref = prep_out
    n = t // c
    w = 2 * HD_B
    masks = [_rwkv_masks(c, rev) for rev in (False, True)]

    def body(ci, states):
        ins, tris, gmasks, dsts = [], [], [], []
        for dr, (ld_ref, dst) in enumerate(((ldf_ref, o_ref), (ldb_ref, yb_ref))):
            cj = ci if dr == 0 else n - 1 - ci
            rows = pl.ds(pl.multiple_of(cj * c, c), c)
            for p in range(npair):
                cols = slice(p * w, (p + 1) * w)
                ins.append(tuple(ref[rows, cols] for ref in (r_ref, ld_ref, kb_ref, vb_ref, kk_ref, a_ref)))
                tris.append(masks[dr][0])
                gmasks.append(masks[dr][1])
                dsts.append((dst, rows, cols))
        ys, new_states = _rwkv_pair_chunks(ins, list(states), tris, gmasks, *masks[0][2:])
        for (dst, rows, cols), y in zip(dsts, ys):
            dst[rows, cols] = y
        return tuple(new_states)

    if latent:
        zero = jnp.zeros((HD_B, HD_B), F32)
        init = tuple(jnp.concatenate([jnp.concatenate([s0_ref[0, dr, 2 * p], zero], axis=1),
                                      jnp.concatenate([zero, s0_ref[0, dr, 2 * p + 1]], axis=1)], axis=0)
                     for dr in range(2) for p in range(npair))
    else:
        init = tuple(jnp.zeros((w, w), F32) for _ in range(2 * npair))
    fin = lax.fori_loop(0, n, body, init, unroll=True if n <= RWKV_UNROLL_MAX else 2)
    segm = _seg_matrix(w, 6, 1.0 / HD_B)
    for p in range(npair):
        if not latent:
            for dr in range(2):
                st = fin[dr * npair + p]
                sfin_ref[0, dr, 2 * p] = st[:HD_B, :HD_B]
                sfin_ref[0, dr, 2 * p + 1] = st[HD_B:, HD_B:]
        cols = slice(p * w, (p + 1) * w)
        y = o_ref[:, cols] + yb_ref[:, cols]
        yc = y - _split_dot(y, segm)
        yn = yc * lax.rsqrt(_split_dot(yc * yc, segm) + RWKV_GN_EPS)
        o_ref[:, cols] = (yn * lnw_ref[:, cols] + lnb_ref[:, cols] + bonus_ref[:, cols]) * gate_ref[:, cols]


def _rwkv(p, prm, ln_w, ln_b, latent, s0=None, dst=None):
    mu, k_k, k_a, r_k, w0, w_up, a0, a_up, g_up = prm
    nseq, t, rb0 = (DEC_BATCH, DEC_SEQ, M_CTX // DEC_SEQ) if latent else (BATCH, SEQ, 0)
    npair = H_B // 2
    lw = LORA_W + LORA_A + LORA_G
    c0 = (IN_AB - 3 * MIX - lw) // MIX
    cl = (IN_AB - lw) // lw
    full = lambda *shape: pl.BlockSpec(shape, lambda b: (0,) * len(shape))
    st_spec = pl.BlockSpec((1, 2, H_B, HD_B, HD_B), lambda b: (b, 0, 0, 0, 0))
    o_spec = pl.BlockSpec((t, MIX), lambda b: (rb0 + b, 0))
    o_shape = jax.ShapeDtypeStruct((M_ALL, MIX), F32)
    in_specs = [pl.BlockSpec((t, MIX), lambda b: (rb0 + b, c0)),
                pl.BlockSpec((t, MIX), lambda b: (rb0 + b, c0 + 1)),
                pl.BlockSpec((t, MIX), lambda b: (rb0 + b, c0 + 2)),
                pl.BlockSpec((t, lw), lambda b: (rb0 + b, cl)),
                full(1, MIX), full(1, MIX), full(1, MIX), full(1, lw),
                full(1, MIX), full(1, MIX), full(1, MIX),
                full(2, MIX), full(2, LORA_W, MIX), full(1, MIX), full(LORA_A, MIX), full(LORA_G, MIX),
                full(1, MIX), full(1, MIX)]
    args = [p, p, p, p,
            mu[None, 0:MIX], mu[None, MIX:2 * MIX], mu[None, 2 * MIX:3 * MIX], mu[None, 3 * MIX:],
            k_k.reshape(1, MIX), k_a.reshape(1, MIX), r_k.reshape(1, MIX), w0, w_up,
            a0.reshape(1, MIX), a_up, g_up, ln_w.reshape(1, MIX), ln_b.reshape(1, MIX)]
    if latent:
        args += [s0, dst]
        in_specs += [st_spec, pl.BlockSpec(memory_space=pl.ANY)]
        out_specs, out_shape, aliases = o_spec, o_shape, {len(args) - 1: 0}
    else:
        out_specs = [o_spec, st_spec]
        out_shape = [o_shape, jax.ShapeDtypeStruct((nseq, 2, H_B, HD_B, HD_B), F32)]
        aliases = {}
    return pl.pallas_call(
        functools.partial(_rwkv_scan_kernel, t=t, c=RWKV_CHUNK, npair=npair, latent=latent),
        grid=(nseq,),
        in_specs=in_specs, out_specs=out_specs, out_shape=out_shape,
        scratch_shapes=[pltpu.VMEM((t, MIX), F32)] * (N_PREP_OUT + 1),
        input_output_aliases=aliases,
        compiler_params=_params("parallel"),
        name="rwkv_lat" if latent else "rwkv_ctx",
    )(*args)


S5_L = 16
S5_ROWS = M_ALL // S5_L
S5_CW = S5_L * S5_GROUP
S5_GS = 8
S5_PW = 2 * P_C
S5_TAB = 24
S5_PRM = 2 * S5_GROUP
S5_NC_CTX = SEQ // S5_L
S5_NC_LAT = DEC_SEQ // S5_L
S5_CTX_ROWS = BATCH * S5_NC_CTX


def _gelu_tanh(x):
    return 0.5 * x * (1.0 + jnp.tanh(math.sqrt(2.0 / math.pi) * (x + 0.044715 * (x * x * x))))


def _cmul(ar, ai, br, bi):
    return ar * br - ai * bi, ar * bi + ai * br


def _dot3(a, b, dims):
    ah, al = _split2(a)
    bh, bl = _split2(b)
    d = lambda x, y: lax.dot_general(x, y, dims, preferred_element_type=F32)
    return d(ah, bh) + d(ah, bl) + d(al, bh)


def _s5_param_kernel(prm_ref, lam_ref, tab_ref, r_ref, st_ref):
    nl, ns = S5_L, S5_GROUP
    lane_blk = lax.broadcasted_iota(jnp.int32, (ns, S5_CW), 1) // ns
    kf = lax.broadcasted_iota(jnp.int32, (S5_TAB, P_C), 0).astype(F32)
    t_rows = [jnp.zeros((ns, S5_CW), F32) for _ in range(nl)]
    e_parts = []
    for d in range(2):
        lr, li = lam_ref[0, d, 0:1, :], lam_ref[0, d, 1:2, :]
        dt = jnp.exp(lam_ref[0, d, 2:3, :])
        mag = jnp.exp(lr * dt * kf)
        pw_re, pw_im = mag * jnp.cos(li * dt * kf), mag * jnp.sin(li * dt * kf)
        tab_ref[0, d, 0] = pw_re
        tab_ref[0, d, 1] = pw_im
        nr, ab_im, den = pw_re[1:2] - 1.0, pw_im[1:2], lr * lr + li * li
        f = ((nr * lr + ab_im * li) / den, (ab_im * lr - nr * li) / den)
        part = lambda lo, n: (prm_ref[0, d, 0, lo:lo + n, :], prm_ref[0, d, 1, lo:lo + n, :])
        bb = _cmul(*f, *part(0, ns))
        cc = part(ns, ns)
        pw = lambda k: (pw_re[k:k + 1], pw_im[k:k + 1])
        steps = range(nl)
        if d == 0:
            ke, ks, kk = [nl - 1 - j for j in steps], [j + 1 for j in steps], list(steps)
        else:
            ke, ks, kk = list(steps), [nl - j for j in steps], [nl - 1 - j for j in steps]
        stack = lambda xs: (jnp.concatenate([x[0] for x in xs], axis=0), jnp.concatenate([x[1] for x in xs], axis=0))
        e_re, e_im = stack([_cmul(*bb, *pw(k)) for k in ke])
        s_re, s_im = stack([_cmul(*cc, *pw(k)) for k in ks])
        k_re, k_im = stack([_cmul(*cc, *pw(k)) for k in kk])
        st_ref[0, d] = jnp.concatenate([s_re, -s_im], axis=1).astype(BF16)
        krow = _dot3(bb[0], k_re, NT) - _dot3(bb[1], k_im, NT)
        for j in steps:
            if d == 0:
                shifted, keep = pltpu.roll(krow, ns * j, axis=1), lane_blk >= j
            else:
                shifted, keep = pltpu.roll(krow, (S5_CW - ns * (nl - 1 - j)) % S5_CW, axis=1), lane_blk <= j
            t_rows[j] = t_rows[j] + jnp.where(keep, shifted, 0.0)
        e_parts += [e_re, e_im]
    r_ref[0] = jnp.concatenate([jnp.concatenate(t_rows, axis=0)] + e_parts, axis=1).astype(BF16)


def _s5_params(prm, lam):
    tab_spec = pl.BlockSpec((1, 2, 2, S5_TAB, P_C), lambda g: (g, 0, 0, 0, 0))
    return pl.pallas_call(
        _s5_param_kernel,
        grid=(G_C,),
        in_specs=[pl.BlockSpec((1, 2, 2, S5_PRM, P_C), lambda g: (g, 0, 0, 0, 0)),
                  pl.BlockSpec((1, 2, 8, P_C), lambda g: (g, 0, 0, 0))],
        out_specs=[tab_spec,
                   pl.BlockSpec((1, S5_CW, 2 * S5_CW), lambda g: (g, 0, 0)),
                   pl.BlockSpec((1, 2, S5_CW, S5_PW), lambda g: (g, 0, 0, 0))],
        out_shape=[jax.ShapeDtypeStruct((G_C, 2, 2, S5_TAB, P_C), F32),
                   jax.ShapeDtypeStruct((G_C, S5_CW, 2 * S5_CW), BF16),
                   jax.ShapeDtypeStruct((G_C, 2, S5_CW, S5_PW), BF16)],
        compiler_params=_params("parallel"),
        name="s5_params",
    )(prm, lam)


def _s5_core_kernel(u_ref, r_ref, st_ref, tab_ref, h0_ref, o_ref, hfin_ref, ug_scr, yg_scr, e_scr, hp_scr):
    ns, nl = S5_GROUP, S5_L
    per_v = 128 // ns
    blk = lax.broadcasted_iota(jnp.int32, (8, 128), 1) // ns

    def merge(select):
        acc = select(0)
        for b in range(1, per_v):
            acc = jnp.where(blk == b, select(b), acc)
        return acc

    def shuffle(srcs):
        rolled = []
        for s in range(per_v):
            m = merge(lambda b: srcs[(b + s) % per_v])
            rolled.append(m if s == 0 else pltpu.roll(m, s * ns, axis=1))
        return [merge(lambda b: rolled[(b - a) % per_v]) for a in range(per_v)]

    def tiles(regroup):
        def ctx_tile(ti, _):
            c, b0 = ti // (BATCH // 8), (ti % (BATCH // 8)) * 8
            regroup(b0 * SEQ + c * nl, SEQ, pl.multiple_of(c * BATCH + b0, 8))
            return 0

        def lat_tile(ti, _):
            b, c0 = ti // (S5_NC_LAT // 8), (ti % (S5_NC_LAT // 8)) * 8
            regroup(M_CTX + b * DEC_SEQ + c0 * nl, nl, pl.multiple_of(S5_CTX_ROWS + b * S5_NC_LAT + c0, 8))
            return 0

        lax.fori_loop(0, S5_CTX_ROWS // 8, ctx_tile, 0, unroll=4)
        lax.fori_loop(0, (S5_ROWS - S5_CTX_ROWS) // 8, lat_tile, 0, unroll=4)

    def regroup_in(tok, stride, row):
        uj = [u_ref[pl.ds(tok + j, 8, stride=stride), :] for j in range(nl)]
        for jh in range(nl // per_v):
            for g, v in enumerate(shuffle(uj[jh * per_v:(jh + 1) * per_v])):
                ug_scr[g, pl.ds(row, 8), jh * 128:(jh + 1) * 128] = v

    tiles(regroup_in)

    chains = [(g, d) for g in range(S5_GS) for d in range(2)]
    for g in range(S5_GS):
        a = jnp.dot(ug_scr[g].astype(BF16), r_ref[g], preferred_element_type=F32)
        yg_scr[g] = a[:, 0:S5_CW]
        for d in range(2):
            e_scr[g, d] = a[:, S5_CW + d * S5_PW:S5_CW + (d + 1) * S5_PW]
    lam = []
    for g, d in chains:
        lr, li = (tab_ref[g, d, ri, nl:nl + 1, :] for ri in range(2))
        lam.append((jnp.concatenate([lr, lr], axis=1), jnp.concatenate([-li, li], axis=1)))

    def advance(c, hs, latent):
        nc = S5_NC_LAT if latent else S5_NC_CTX
        out = []
        for i, (g, d) in enumerate(chains):
            cc = c if d == 0 else nc - 1 - c
            rows = (pl.ds(S5_CTX_ROWS + cc, DEC_BATCH, stride=nc) if latent
                    else pl.ds(pl.multiple_of(cc * BATCH, BATCH), BATCH))
            hp_scr[g, d, rows, :] = hs[i]
            out.append(lam[i][0] * hs[i] + lam[i][1] * pltpu.roll(hs[i], P_C, axis=1) + e_scr[g, d, rows, :])
        return out

    def both(c, carry):
        hc, hl = carry
        return tuple(advance(c, hc, False)), tuple(advance(c, hl, True))

    h_ctx = tuple(jnp.zeros((BATCH, S5_PW), F32) for _ in chains)
    h_lat = tuple(h0_ref[g, d] for g, d in chains)
    h_ctx, h_lat = lax.fori_loop(0, S5_NC_CTX, both, (h_ctx, h_lat), unroll=2)
    lax.fori_loop(S5_NC_CTX, S5_NC_LAT, lambda c, hl: tuple(advance(c, hl, True)), h_lat, unroll=4)
    for i, (g, d) in enumerate(chains):
        hfin_ref[g, d] = h_ctx[i]
    for g in range(S5_GS):
        y = yg_scr[g]
        for d in range(2):
            y = y + lax.dot_general(hp_scr[g, d].astype(BF16), st_ref[g, d], NT, preferred_element_type=F32)
        yg_scr[g] = y

    def regroup_out(tok, stride, row):
        for jh in range(nl // per_v):
            yv = [yg_scr[g, pl.ds(row, 8), jh * 128:(jh + 1) * 128] for g in range(S5_GS)]
            for jj, v in enumerate(shuffle(yv)):
                o_ref[pl.ds(tok + jh * per_v + jj, 8, stride=stride), :] = v

    tiles(regroup_out)


def _s5_core(p2, r, st, tab, h0):
    blk = lambda *tail: pl.BlockSpec((S5_GS,) + tail, lambda s: (s,) + (0,) * len(tail))
    col = pl.BlockSpec((M_ALL, S5_GS * S5_GROUP), lambda s: (0, s))
    return pl.pallas_call(
        _s5_core_kernel,
        grid=(G_C // S5_GS,),
        in_specs=[col, blk(S5_CW, 2 * S5_CW), blk(2, S5_CW, S5_PW),
                  blk(2, 2, S5_TAB, P_C), blk(2, DEC_BATCH, S5_PW)],
        out_specs=[col, blk(2, BATCH, S5_PW)],
        out_shape=[jax.ShapeDtypeStruct((M_ALL, MIX), F32),
                   jax.ShapeDtypeStruct((G_C, 2, BATCH, S5_PW), F32)],
        scratch_shapes=[pltpu.VMEM((S5_GS, S5_ROWS, S5_CW), F32),
                        pltpu.VMEM((S5_GS, S5_ROWS, S5_CW), F32),
                        pltpu.VMEM((S5_GS, 2, S5_ROWS, S5_PW), F32),
                        pltpu.VMEM((S5_GS, 2, S5_ROWS, S5_PW), F32)],
        compiler_params=_params("parallel"),
        name="s5_core",
    )(p2, r, st, tab, h0)


def _s5_tables(lam_re, lam_im, log_dt, b_re, b_im, c_re, c_im):
    rows = lambda b, c: jnp.concatenate([jnp.swapaxes(b, -1, -2), c], axis=2)
    prm = jnp.stack([rows(b_re, c_re), rows(b_im, c_im)], axis=1)
    dt = jnp.broadcast_to(log_dt[:, :, None], lam_re.shape)
    lam = jnp.stack([lam_re, lam_im, dt] + [jnp.zeros_like(dt)] * 5, axis=2)
    return jnp.transpose(prm, (2, 0, 1, 3, 4)), jnp.transpose(lam, (1, 0, 2, 3))


def _s5_mixer(p2, tables, state_re, state_im):
    tab, r, st = _s5_params(*tables)
    h0 = jnp.concatenate([state_re, state_im], axis=-1)
    y_tok, hfin = _s5_core(p2, r, st, tab, jnp.transpose(h0, (2, 1, 0, 3)))
    hfin = jnp.transpose(hfin, (2, 1, 0, 3))
    return y_tok, hfin[..., :P_C], hfin[..., P_C:]


def _ret_kernel(*refs, t, latent):
    if latent:
        (q_ref, k_ref, v_ref, g_ref, dl_ref, lnw_ref, lnb_ref, r0_ref, cos_ref, sin_ref, _,
         o_ref, ob_ref) = refs
    else:
        q_ref, k_ref, v_ref, g_ref, dl_ref, lnw_ref, lnb_ref, o_ref, rfin_ref, ob_ref = refs
    n = t // CHUNK
    w = HD_D
    jf = lax.broadcasted_iota(jnp.int32, (CHUNK, CHUNK), 0).astype(F32)
    kf = lax.broadcasted_iota(jnp.int32, (CHUNK, CHUNK), 1).astype(F32)
    diff = jf - kf
    tabs = []
    for dr in range(2):
        for h in range(H_D):
            l = -_softplus(-dl_ref[h, dr:dr + 1, :])
            if dr == 0:
                dmat = jnp.where(diff >= 0, jnp.exp(l * jnp.maximum(diff, 0.0)), 0.0)
                xi = jnp.exp(l * (jf + 1.0))
                zeta = jnp.exp(l * (CHUNK - 1.0 - jf))
            else:
                dmat = jnp.where(diff < 0, jnp.exp(l * jnp.maximum(-diff, 0.0)), 0.0)
                xi = jnp.exp(l * (CHUNK - jf))
                zeta = jnp.exp(l * jf)
            tabs.append((dmat, xi, zeta, jnp.exp(l * CHUNK)))
    chains = [(dr, h) for dr in range(2) for h in range(H_D)]

    def body(ci, states):
        q, k, v, where = [], [], [], []
        for dr, h in chains:
            cj = ci if dr == 0 else n - 1 - ci
            rows = pl.ds(pl.multiple_of(cj * CHUNK, CHUNK), CHUNK)
            cols = slice(h * w, (h + 1) * w)
            qi = q_ref[rows, cols]
            ki = k_ref[rows, cols] * (HD_D ** -0.5)
            if latent:
                qi = _rope_pairs(qi, cos_ref[rows, :], sin_ref[rows, :], HD_D // 2)
                ki = _rope_pairs(ki, cos_ref[rows, :], sin_ref[rows, :], HD_D // 2)
            q.append(qi.astype(BF16))
            k.append(ki)
            v.append(v_ref[rows, cols].astype(BF16))
            where.append((o_ref if dr == 0 else ob_ref, rows, cols))
        nch = range(len(chains))
        inner = [(_bdot(q[i], k[i], NT) * tabs[i][0]).astype(BF16) for i in nch]
        cross = [_bdot(q[i], states[i]) * tabs[i][1] for i in nch]
        kz = [(k[i] * tabs[i][2]).astype(BF16) for i in nch]
        out = [jnp.dot(inner[i], v[i], preferred_element_type=F32) + cross[i] for i in nch]
        new_states = [states[i] * tabs[i][3] + lax.dot_general(kz[i], v[i], TN, preferred_element_type=F32)
                      for i in nch]
        for (dst, rows, cols), y in zip(where, out):
            dst[rows, cols] = y
        return tuple(new_states)

    if latent:
        lax.fori_loop(0, n, body, tuple(r0_ref[0, dr, h] for dr, h in chains), unroll=2)
    else:
        fin = lax.fori_loop(0, n, body, tuple(jnp.zeros((w, w), F32) for _ in chains), unroll=True)
        for i, (dr, h) in enumerate(chains):
            rfin_ref[0, dr, h] = fin[i]
    for h in range(H_D):
        cols = slice(h * w, (h + 1) * w)
        y = o_ref[:, cols] + ob_ref[:, cols]
        yc = y - jnp.mean(y, axis=-1, keepdims=True)
        yn = yc * lax.rsqrt(jnp.mean(yc * yc, axis=-1, keepdims=True) + RET_GN_EPS)
        o_ref[:, cols] = (yn * lnw_ref[:, cols] + lnb_ref[:, cols]) * _silu(g_ref[:, cols])


def _retention(p, decay_logit, ln_w, ln_b, latent, r0=None, rope=None, dst=None):
    if latent:
        nseq, t, rb0 = DEC_BATCH, DEC_SEQ, M_CTX // DEC_SEQ
    else:
        nseq, t, rb0 = BATCH, SEQ, 0
    w = HD_D
    dl = jnp.broadcast_to(decay_logit.T[:, :, None], (H_D, 2, w))
    full = lambda shape: pl.BlockSpec(shape, lambda b: (0,) * len(shape))
    st_spec = pl.BlockSpec((1, 2, H_D, w, w), lambda b: (b, 0, 0, 0, 0))
    in_specs = [pl.BlockSpec((t, MIX), lambda b, j=j: (rb0 + b, j)) for j in (1, 2, 3, 4)]
    in_specs += [full((H_D, 2, w)), full((1, MIX)), full((1, MIX))]
    args = [p, p, p, p, dl, ln_w.reshape(1, MIX), ln_b.reshape(1, MIX)]
    o_spec = pl.BlockSpec((t, MIX), lambda b: (rb0 + b, 0))
    o_shape = jax.ShapeDtypeStruct((M_ALL, MIX), F32)
    if latent:
        in_specs += [st_spec, full((t, w)), full((t, w)), pl.BlockSpec(memory_space=pl.ANY)]
        args += [r0, *rope, dst]
        out_specs, out_shape, aliases = o_spec, o_shape, {len(args) - 1: 0}
    else:
        out_specs = [o_spec, st_spec]
        out_shape = [o_shape, jax.ShapeDtypeStruct((nseq, 2, H_D, w, w), F32)]
        aliases = {}
    return pl.pallas_call(
        functools.partial(_ret_kernel, t=t, latent=latent),
        grid=(nseq,),
        in_specs=in_specs, out_specs=out_specs, out_shape=out_shape,
        scratch_shapes=[pltpu.VMEM((t, MIX), F32)],
        input_output_aliases=aliases,
        compiler_params=_params("parallel"),
        name="retention_lat" if latent else "retention_ctx",
    )(*args)


def _rope_tables(n_tok, dim, reps):
    rows = n_tok // GRID_W
    n_freq = dim // 4
    inv = 1.0 / (ROPE_THETA ** (jnp.arange(n_freq, dtype=F32) / n_freq))
    row = jnp.repeat(jnp.arange(rows, dtype=F32), GRID_W)
    col = jnp.tile(jnp.arange(GRID_W, dtype=F32), rows)
    ang = jnp.concatenate([row[:, None] * inv, col[:, None] * inv], axis=-1)
    cos, sin = jnp.cos(ang), jnp.sin(ang)
    return jnp.tile(jnp.concatenate([cos, cos], axis=1), (1, reps)), \
        jnp.tile(jnp.concatenate([-sin, sin], axis=1), (1, reps))


def kernel(x_prompt, x_sample, cache_k_ab, cache_v_ab, state_rwkv, state_s5_re, state_s5_im, state_ret, c, c_ctx, norm1_g, norm2_g, w_mod, b_mod, w_ff_gate, w_ff_up, w_ff_down, w_in_ab, w_out_ab, qk_gain_a, lambda_qk, subln_g, rwkv_mu, rwkv_k_k, rwkv_k_a, rwkv_r_k, rwkv_w0, rwkv_w_up, rwkv_a0, rwkv_a_up, rwkv_g_up, rwkv_ln_w, rwkv_ln_b, w_in_cd, w_out_cd, s5_lam_re, s5_lam_im, s5_log_dt, s5_b_re, s5_b_im, s5_c_re, s5_c_im, s5_d, s5_w_glu, ret_decay_logit, ret_ln_w, ret_ln_b):
    d = D_MODEL
    xs = [x_prompt.reshape(M_CTX, d), x_sample.reshape(M_LAT, d)]
    cvec = jnp.zeros((MOD_ROWS, d), F32).at[0].set(c_ctx).at[1:1 + DEC_BATCH].set(c)
    mods = _modulation(cvec, w_mod, b_mod)

    lam_init = 0.8 - 0.6 * math.exp(-0.3 * 0)
    p = _norm_linear(xs, norm1_g[0], mods[0], w_in_ab)
    gain2 = jnp.tile(qk_gain_a[0], (1, 2))
    rope_a = _rope_tables(DEC_SEQ, HD_A, 2)
    ck = cache_k_ab[:, 0].reshape(DEC_BATCH * PAST_LEN, MIX)
    cv = cache_v_ab[:, 0].reshape(DEC_BATCH * PAST_LEN, MIX)
    oa, k_ctx, v_ctx = _attention(p, gain2, lambda_qk[0], subln_g[0], lam_init, latent=False)
    oa = _attention(p, gain2, lambda_qk[0], subln_g[0], lam_init, latent=True,
                    cache_k=ck, cache_v=cv, rope=rope_a, dst=oa)
    rw_prm = (rwkv_mu[0], rwkv_k_k[0], rwkv_k_a[0], rwkv_r_k[0], rwkv_w0[0], rwkv_w_up[0],
              rwkv_a0[0], rwkv_a_up[0], rwkv_g_up[0])
    ob, sfin_ctx = _rwkv(p, rw_prm, rwkv_ln_w[0], rwkv_ln_b[0], latent=False)
    ob = _rwkv(p, rw_prm, rwkv_ln_w[0], rwkv_ln_b[0], latent=True, s0=state_rwkv[:, 0], dst=ob)
    ffn_w = (w_ff_gate.astype(BF16), w_ff_up.astype(BF16), w_ff_down.astype(BF16))
    x = _mix_ffn(xs, oa, ob, w_out_ab[0].astype(BF16), norm2_g[0], mods[0], *ffn_w, layer=0, split_out=False)

    p2 = _norm_linear([x], norm1_g[1], mods[1], w_in_cd)
    tables = _s5_tables(s5_lam_re[0], s5_lam_im[0], s5_log_dt[0], s5_b_re[0], s5_b_im[0],
                        s5_c_re[0], s5_c_im[0])
    y_s5, s5_fin_re, s5_fin_im = _s5_mixer(p2, tables, state_s5_re[:, 0], state_s5_im[:, 0])
    oc = (y_s5, p2, s5_d[0], s5_w_glu[0].astype(BF16))
    rope_d = _rope_tables(DEC_SEQ, HD_D, 1)
    od, rfin = _retention(p2, ret_decay_logit[0], ret_ln_w[0], ret_ln_b[0], latent=False)
    od = _retention(p2, ret_decay_logit[0], ret_ln_w[0], ret_ln_b[0], latent=True, r0=state_ret[:, 0],
                    rope=rope_d, dst=od)
    y_ctx, y_lat = _mix_ffn([x], oc, od, w_out_cd[0].astype(BF16), norm2_g[1], mods[1], *ffn_w, layer=1,
                            split_out=True)
    y_prompt = y_ctx.reshape(BATCH, SEQ, d)
    y_sample = y_lat.reshape(DEC_BATCH, DEC_SEQ, d)
    new_k = jnp.transpose(k_ctx, (0, 4, 1, 2, 3))[:, None]
    new_v = v_ctx.reshape(BATCH, 1, SEQ, H_A, VD_A)
    new_rwkv = sfin_ctx[:, None]
    new_s5_re = s5_fin_re[:, None]
    new_s5_im = s5_fin_im[:, None]
    new_ret = rfin[:, None]
    return (y_prompt, y_sample, new_k, new_v, new_rwkv, new_s5_re, new_s5_im, new_ret)
```

```python
import functools
import math

import numpy as np
import jax
import jax.numpy as jnp
from jax import lax
from jax.experimental import pallas as pl
from jax.experimental.pallas import tpu as pltpu

F32 = jnp.float32
BF16 = jnp.bfloat16
HIGHEST = lax.Precision.HIGHEST

D_MODEL = 1024
BATCH = 32
SEQ = 256
DEC_BATCH = 2
DEC_SEQ = 1024
PAST_LEN = 256
GRID_W = 64
H_A = 4
HD_A = 64
VD_A = 128
H_B = 8
HD_B = 64
MIX = 512
LORA_W = 64
LORA_A = 64
LORA_G = 128
S5_GROUP = 16
G_C = 32
P_C = 64
S5_STATE = G_C * P_C
H_D = 4
HD_D = 128
CHUNK = 128
D_FF = 2816
IN_AB = 3328
IN_CD = 2560
ROPE_THETA = 10000.0
NORM_EPS = 1e-6
RWKV_GN_EPS = 64e-5
RET_GN_EPS = 1e-5

M_CTX = BATCH * SEQ
M_LAT = DEC_BATCH * DEC_SEQ
M_ALL = M_CTX + M_LAT
MOD_ROWS = 8
RWKV_CHUNK = 64
VMEM_LIMIT = 56 * 1024 * 1024

NN = (((1,), (0,)), ((), ()))
NT = (((1,), (1,)), ((), ()))
TN = (((0,), (0,)), ((), ()))


def _params(*sem):
    return pltpu.CompilerParams(dimension_semantics=sem, vmem_limit_bytes=VMEM_LIMIT)


def _bdot(a, b, dims=NN):
    return lax.dot_general(a.astype(BF16), b.astype(BF16), dims, preferred_element_type=F32)


def _hdot(a, b, dims=NN):
    return lax.dot_general(a, b, dims, precision=HIGHEST, preferred_element_type=F32)


def _split_dot(x, m):
    hi = x.astype(BF16)
    lo = (x - hi.astype(F32)).astype(BF16)
    return (jnp.dot(hi, m, preferred_element_type=F32) + jnp.dot(lo, m, preferred_element_type=F32))


def _seg_matrix(n, shift, val):
    r = lax.broadcasted_iota(jnp.int32, (n, n), 0) >> shift
    c = lax.broadcasted_iota(jnp.int32, (n, n), 1) >> shift
    return jnp.where(r == c, val, 0.0).astype(BF16)


def _sigmoid(x):
    return jax.nn.sigmoid(x)


def _silu(x):
    return x * jax.nn.sigmoid(x)


def _softplus(x):
    return jnp.maximum(x, 0.0) + jnp.log(1.0 + jnp.exp(-jnp.abs(x)))


def _mod_row(tile, tm):
    r0 = tile * tm
    return jnp.where(r0 < M_CTX, 0, 1 + (r0 - M_CTX) // DEC_SEQ)


def _norm_mod(x, g, sc_ref, sh_ref, row):
    y = x * lax.rsqrt(jnp.mean(x * x, axis=-1, keepdims=True) + NORM_EPS) * g
    return y * (1.0 + sc_ref[pl.ds(row, 1), :]) + sh_ref[pl.ds(row, 1), :]


def _mod_kernel(c_ref, w_ref, b_ref, o_ref):
    o_ref[0] = _bdot(_silu(c_ref[...]), w_ref[0]) + b_ref[0]


def _modulation(cvec, w_mod, b_mod):
    depth, d, n6 = w_mod.shape
    tn = 1536
    return pl.pallas_call(
        _mod_kernel,
        grid=(depth, n6 // tn),
        in_specs=[pl.BlockSpec((MOD_ROWS, d), lambda l, j: (0, 0)),
                  pl.BlockSpec((1, d, tn), lambda l, j: (l, 0, j)),
                  pl.BlockSpec((1, 1, tn), lambda l, j: (l, 0, j))],
        out_specs=pl.BlockSpec((1, MOD_ROWS, tn), lambda l, j: (l, 0, j)),
        out_shape=jax.ShapeDtypeStruct((depth, MOD_ROWS, n6), F32),
        compiler_params=_params("parallel", "parallel"),
        name="modulation",
    )(cvec, w_mod, b_mod.reshape(depth, 1, n6))


def _tile_specs(tm, d):
    nc = M_CTX // tm
    return [pl.BlockSpec((tm, d), lambda i: (jnp.minimum(i, nc - 1), 0)),
            pl.BlockSpec((tm, d), lambda i: (jnp.maximum(i - nc, 0), 0))]


def _pick_rows(refs, tm):
    if len(refs) == 1:
        return refs[0][...]
    return jnp.where(pl.program_id(0) < M_CTX // tm, refs[0][...], refs[1][...])


def _norm_linear_kernel(*refs, tm):
    *x_refs, g_ref, sc_ref, sh_ref, w_ref, o_ref, wbf = refs

    @pl.when(pl.program_id(0) == 0)
    def _():
        wbf[...] = w_ref[0].astype(BF16)

    row = _mod_row(pl.program_id(0), tm)
    x = _pick_rows(x_refs, tm)
    nsub = 2
    rows = [slice(i * tm // nsub, (i + 1) * tm // nsub) for i in range(nsub)]
    hs = [_norm_mod(x[r], g_ref[...], sc_ref, sh_ref, row).astype(BF16) for r in rows]
    for r, h in zip(rows, hs):
        o_ref[r, :] = jnp.dot(h, wbf[...], preferred_element_type=F32)


def _norm_linear(xs, g, mods, w):
    tm = 512
    d = D_MODEL
    n = w.shape[2]
    x_specs = _tile_specs(tm, d) if len(xs) == 2 else [pl.BlockSpec((tm, d), lambda i: (i, 0))]
    return pl.pallas_call(
        functools.partial(_norm_linear_kernel, tm=tm),
        grid=(M_ALL // tm,),
        in_specs=x_specs + [pl.BlockSpec((1, d), lambda i: (0, 0)),
                            pl.BlockSpec((MOD_ROWS, d), lambda i: (0, 1)),
                            pl.BlockSpec((MOD_ROWS, d), lambda i: (0, 0)),
                            pl.BlockSpec((1, d, n), lambda i: (0, 0, 0), pipeline_mode=pl.Buffered(1))],
        out_specs=pl.BlockSpec((tm, n), lambda i: (i, 0)),
        out_shape=jax.ShapeDtypeStruct((M_ALL, n), F32),
        scratch_shapes=[pltpu.VMEM((d, n), BF16)],
        compiler_params=_params("arbitrary"),
        name="norm_linear",
    )(*xs, g.reshape(1, d), mods, mods, w)


def _mix_ffn_kernel(*refs, tm, ck, n_in, n_out, glu):
    x_refs, refs = refs[:n_in], refs[n_in:]
    if glu:
        (y_ref, u_ref, d_ref, wglu_ref), refs = refs[:4], refs[4:]
        z = _gelu_tanh(y_ref[...] + d_ref[...] * u_ref[...])
        a = z * _sigmoid(jnp.dot(z.astype(BF16), wglu_ref[...], preferred_element_type=F32))
    else:
        a, refs = refs[0][...], refs[1:]
    b_ref, wa_ref, wb_ref, g1_ref, ng_ref, sc_ref, sh_ref, g2_ref, wg_ref, wu_ref, wd_ref = refs[:11]
    o_refs = refs[11:]
    row = _mod_row(pl.program_id(0), tm)
    mix = (jnp.dot(a.astype(BF16), wa_ref[...], preferred_element_type=F32)
           + jnp.dot(b_ref[...].astype(BF16), wb_ref[...], preferred_element_type=F32))
    x = _pick_rows(x_refs, tm) + g1_ref[pl.ds(row, 1), :] * mix
    h = _norm_mod(x, ng_ref[...], sc_ref, sh_ref, row).astype(BF16)
    acc = jnp.zeros((tm, D_MODEL), F32)
    for c in range(D_FF // ck):
        gg = jnp.dot(h, wg_ref[0, :, c * ck:(c + 1) * ck], preferred_element_type=F32)
        uu = jnp.dot(h, wu_ref[0, :, c * ck:(c + 1) * ck], preferred_element_type=F32)
        act = (_silu(gg) * uu).astype(BF16)
        acc = acc + jnp.dot(act, wd_ref[0, c * ck:(c + 1) * ck, :], preferred_element_type=F32)
    y = x + g2_ref[pl.ds(row, 1), :] * acc
    if n_out == 1:
        o_refs[0][...] = y
    else:
        is_ctx = pl.program_id(0) < M_CTX // tm

        @pl.when(is_ctx)
        def _():
            o_refs[0][...] = y

        @pl.when(jnp.logical_not(is_ctx))
        def _():
            o_refs[1][...] = y


def _mix_ffn(xs, oa, ob, w_out_bf16, norm_g, mods, wg, wu, wd, layer, split_out):
    tm, ck = 512, 256
    d = D_MODEL
    row_spec = lambda n: pl.BlockSpec((tm, n), lambda i: (i, 0))
    const = lambda shape, idx: pl.BlockSpec(shape, lambda i: idx, pipeline_mode=pl.Buffered(1))
    mod = lambda j: pl.BlockSpec((MOD_ROWS, d), lambda i: (0, j))
    x_specs = _tile_specs(tm, d) if len(xs) == 2 else [row_spec(d)]
    glu = isinstance(oa, tuple)
    if glu:
        y_tok, p2, d_skip, w_glu = oa
        a_specs = [row_spec(MIX), row_spec(MIX), pl.BlockSpec((1, MIX), lambda i: (0, 0)), const((MIX, MIX), (0, 0))]
        a_args = [y_tok, p2, d_skip.reshape(1, MIX), w_glu]
    else:
        a_specs, a_args = [row_spec(MIX)], [oa]
    if split_out:
        out_specs = _tile_specs(tm, d)
        out_shape = [jax.ShapeDtypeStruct((M_CTX, d), F32), jax.ShapeDtypeStruct((M_LAT, d), F32)]
    else:
        out_specs, out_shape = row_spec(d), jax.ShapeDtypeStruct((M_ALL, d), F32)
    return pl.pallas_call(
        functools.partial(_mix_ffn_kernel, tm=tm, ck=ck, n_in=len(xs), n_out=2 if split_out else 1, glu=glu),
        grid=(M_ALL // tm,),
        in_specs=x_specs + a_specs + [row_spec(MIX),
                            const((MIX, d), (0, 0)), const((MIX, d), (1, 0)),
                            mod(2),
                            pl.BlockSpec((1, d), lambda i: (0, 0)),
                            mod(4), mod(3), mod(5),
                            const((1, d, D_FF), (layer, 0, 0)), const((1, d, D_FF), (layer, 0, 0)),
                            const((1, D_FF, d), (layer, 0, 0))],
        out_specs=out_specs, out_shape=out_shape,
        compiler_params=_params("arbitrary"),
        name="mix_ffn",
    )(*xs, *a_args, ob, w_out_bf16, w_out_bf16, mods, norm_g.reshape(1, d), mods, mods, mods, wg, wu, wd)


def _qk_norm(x, gain, segm):
    ms = _split_dot(x * x, segm)
    return x * lax.rsqrt(ms + NORM_EPS) * gain


def _rope_pairs(x, cosf, sinf, half):
    lane = lax.broadcasted_iota(jnp.int32, x.shape, 1)
    first = (lane & (2 * half - 1)) < half
    n = x.shape[1]
    partner = jnp.where(first, pltpu.roll(x, n - half, axis=1), pltpu.roll(x, half, axis=1))
    return x * cosf + partner * sinf


def _attn_kernel(*refs, latent, lam_init):
    if latent:
        (q_ref, k_ref, v_ref, ck_ref, cv_ref, cosq_ref, sinq_ref, cosk_ref, sink_ref,
         gain_ref, lam_ref, sub_ref, _, o_ref, kall, vall) = refs
    else:
        q_ref, k_ref, v_ref, gain_ref, lam_ref, sub_ref, o_ref, kn_ref, vo_ref, kall, vall = refs
    w = 2 * HD_A
    segm = _seg_matrix(w, 6, 1.0 / HD_A)
    gains = gain_ref[...]
    cols = [slice(h * w, (h + 1) * w) for h in range(H_A)]

    @pl.when(pl.program_id(1) == 0)
    def _():
        k = [_qk_norm(k_ref[:, c], gains[1:2], segm) for c in cols]
        if latent:
            k = [_rope_pairs(x, cosk_ref[...], sink_ref[...], HD_A // 2) for x in k]
            for c, x in zip(cols, k):
                kall[0:PAST_LEN, c] = ck_ref[:, c].astype(BF16)
                kall[PAST_LEN:, c] = x.astype(BF16)
                vall[0:PAST_LEN, c] = cv_ref[:, c].astype(BF16)
                vall[PAST_LEN:, c] = v_ref[:, c].astype(BF16)
        else:
            for h, (c, x) in enumerate(zip(cols, k)):
                xt = x.T
                kn_ref[0, h, 0] = xt[:HD_A]
                kn_ref[0, h, 1] = xt[HD_A:]
                vo_ref[pl.ds(h, v_ref.shape[0], stride=H_A), :] = v_ref[:, c]
                kall[:, c] = x.astype(BF16)
                vall[:, c] = v_ref[:, c].astype(BF16)

    q = [_qk_norm(q_ref[:, c], gains[0:1], segm) for c in cols]
    if latent:
        q = [_rope_pairs(x, cosq_ref[...], sinq_ref[...], HD_A // 2) for x in q]
    lv = lam_ref[...]
    lam = (jnp.exp(jnp.sum(lv[0:1] * lv[1:2], axis=1, keepdims=True))
           - jnp.exp(jnp.sum(lv[2:3] * lv[3:4], axis=1, keepdims=True)) + lam_init)
    scale = HD_A ** -0.5
    assert math.frexp(scale)[0] == 0.5
    comp0 = lax.broadcasted_iota(jnp.int32, q[0].shape, 1) < HD_A
    qc = [jnp.where(comp0, *sel).astype(BF16) for x in [y * scale for y in q] for sel in ((x, 0.0), (0.0, x))]
    s = [lax.dot_general(qc[i], kall[:, cols[i // 2]], NT, preferred_element_type=F32) for i in range(2 * H_A)]
    e = [jnp.exp(x - jnp.max(x, axis=-1, keepdims=True)) for x in s]
    w = [(1.0 if i % 2 == 0 else lam) / jnp.sum(x, axis=-1, keepdims=True) for i, x in enumerate(e)]
    att = [(e[2 * h] * w[2 * h] - e[2 * h + 1] * w[2 * h + 1]).astype(BF16) for h in range(H_A)]
    o = [jnp.dot(att[h], vall[:, cols[h]], preferred_element_type=F32) for h in range(H_A)]
    o = [x * lax.rsqrt(jnp.mean(x * x, axis=-1, keepdims=True) + NORM_EPS) * sub_ref[...] for x in o]
    for c, x in zip(cols, o):
        o_ref[:, c] = x * (1.0 - lam_init)


def _attention(p, gain2, lambda_qk, subln_g, lam_init, latent, cache_k=None, cache_v=None, rope=None,
               dst=None):
    w = 2 * HD_A
    if latent:
        nseq, t, tq, rb0, s_len = DEC_BATCH, DEC_SEQ, 128, M_CTX // DEC_SEQ, PAST_LEN + DEC_SEQ
    else:
        nseq, t, tq, rb0, s_len = BATCH, SEQ, SEQ, 0, SEQ
    nq = t // tq
    qoff = rb0 * nq
    full = lambda shape: pl.BlockSpec(shape, lambda b, i: (0,) * len(shape))
    in_specs = [pl.BlockSpec((tq, MIX), lambda b, i: (qoff + b * nq + i, 0)),
                pl.BlockSpec((t, MIX), lambda b, i: (rb0 + b, 1)),
                pl.BlockSpec((t, MIX), lambda b, i: (rb0 + b, 2))]
    args = [p, p, p]
    if latent:
        cosf, sinf = rope
        in_specs += [pl.BlockSpec((PAST_LEN, MIX), lambda b, i: (b, 0)),
                     pl.BlockSpec((PAST_LEN, MIX), lambda b, i: (b, 0)),
                     pl.BlockSpec((tq, w), lambda b, i: (i, 0)),
                     pl.BlockSpec((tq, w), lambda b, i: (i, 0)),
                     full((t, w)), full((t, w))]
        args += [cache_k, cache_v, cosf, sinf, cosf, sinf]
    in_specs += [full((2, w)), full((4, HD_A)), full((1, w))]
    args += [gain2, lambda_qk, subln_g.reshape(1, w)]
    o_spec = pl.BlockSpec((tq, MIX), lambda b, i: (qoff + b * nq + i, 0))
    o_shape = jax.ShapeDtypeStruct((M_ALL, MIX), F32)
    aliases = {}
    if latent:
        out_specs, out_shape = o_spec, o_shape
        in_specs.append(pl.BlockSpec(memory_space=pl.ANY))
        args.append(dst)
        aliases = {len(args) - 1: 0}
    else:
        out_specs = [o_spec, pl.BlockSpec((1, H_A, 2, HD_A, t), lambda b, i: (b, 0, 0, 0, 0)),
                     pl.BlockSpec((t * H_A, VD_A), lambda b, i: (b, 0))]
        out_shape = [o_shape, jax.ShapeDtypeStruct((nseq, H_A, 2, HD_A, t), F32),
                     jax.ShapeDtypeStruct((nseq * t * H_A, VD_A), F32)]
    return pl.pallas_call(
        functools.partial(_attn_kernel, latent=latent, lam_init=lam_init),
        grid=(nseq, nq),
        in_specs=in_specs, out_specs=out_specs, out_shape=out_shape,
        scratch_shapes=[pltpu.VMEM((s_len, MIX), BF16), pltpu.VMEM((s_len, MIX), BF16)],
        input_output_aliases=aliases,
        compiler_params=_params("parallel", "arbitrary"),
        name="diff_attention_lat" if latent else "diff_attention_ctx",
    )(*args)


def _centred_shift(x, mu):
    t = x.shape[0]
    row = lax.broadcasted_iota(jnp.int32, x.shape, 0)
    prev = jnp.where(row == 0, 0.0, pltpu.roll(x, 1, axis=0))
    nxt = jnp.where(row == t - 1, 0.0, pltpu.roll(x, t - 1, axis=0))
    return x + (0.5 * (prev + nxt) - x) * mu


def _seg_sum(x, segm):
    return jnp.concatenate([_split_dot(x[:, j * 128:(j + 1) * 128], segm) for j in range(x.shape[1] // 128)],
                           axis=1)


def _rwkv_prep_kernel(r_ref, k_ref, v_ref, l_ref, mur_ref, muk_ref, muv_ref, mul_ref,
                      kk_ref, ka_ref, rk_ref, w0_ref, wup_ref, a0_ref, aup_ref, gup_ref,
                      ro_ref, ldf_ref, ldb_ref, kbo_ref, vbo_ref, kko_ref, ao_ref, gate_ref, bonus_ref):
    seg1 = _seg_matrix(128, 6, 1.0)
    r = _centred_shift(r_ref[...], mur_ref[...])
    kb = _centred_shift(k_ref[...], muk_ref[...])
    vb = _centred_shift(v_ref[...], muv_ref[...])
    lo = _centred_shift(l_ref[...], mul_ref[...])
    xw = lo[:, 0:LORA_W]
    xa = lo[:, LORA_W:LORA_W + LORA_A]
    xg = lo[:, LORA_W + LORA_A:]
    kk = kb * kk_ref[...]
    kk = kk * lax.rsqrt(_seg_sum(kk * kk, seg1) + 1e-12)
    a = _sigmoid(a0_ref[...] + _bdot(xa, aup_ref[...]))
    kb2 = kb * (1.0 + (a - 1.0) * ka_ref[...])
    lw = jnp.tanh(xw)
    for dr, ld_ref in enumerate((ldf_ref, ldb_ref)):
        z = w0_ref[dr:dr + 1, :] + _bdot(lw, wup_ref[dr])
        logw = -_softplus(-z) - 0.5
        ld_ref[...] = -jnp.exp(logw)
    gate_ref[...] = _bdot(_sigmoid(xg), gup_ref[...])
    bonus_ref[...] = _seg_sum(r * kb2 * rk_ref[...], seg1) * vb
    ro_ref[...] = r
    kbo_ref[...] = kb2
    vbo_ref[...] = vb
    kko_ref[...] = kk
    ao_ref[...] = a


N_PREP_IN = 16
N_PREP_OUT = 9


def _split2(x):
    hi = x.astype(BF16)
    return hi, (x - hi.astype(F32)).astype(BF16)


RWKV_INV_BLOCK = 16
RWKV_UNROLL_MAX = 4


def _rwkv_masks(c, rev):
    ti = lax.broadcasted_iota(jnp.int32, (c, c), 0)
    si = lax.broadcasted_iota(jnp.int32, (c, c), 1)
    tri = jnp.where((si >= ti) if rev else (si <= ti), 1.0, 0.0).astype(BF16)
    t4 = lax.broadcasted_iota(jnp.int32, (4 * c, 4 * c), 0)
    s4 = lax.broadcasted_iota(jnp.int32, (4 * c, 4 * c), 1)
    tm, sm = t4 & (c - 1), s4 & (c - 1)
    strict = (sm > tm) if rev else (sm < tm)
    incl = (sm >= tm) if rev else (sm <= tm)
    same_head = ((t4 // c) & 1) == ((s4 // c) & 1)
    top = t4 < 2 * c
    gmask = same_head & ((top & strict) | (~top & incl))
    t2 = lax.broadcasted_iota(jnp.int32, (2 * c, 2 * c), 0)
    s2 = lax.broadcasted_iota(jnp.int32, (2 * c, 2 * c), 1)
    same = lambda n: (t2 // n) == (s2 // n)
    levels = []
    n = RWKV_INV_BLOCK
    while n < c:
        levels.append(same(2 * n) & ~same(n))
        n *= 2
    f = lambda m: jnp.where(m, 1.0, 0.0)
    return tri, f(gmask), f(same(RWKV_INV_BLOCK)), tuple(f(m) for m in levels), f(same(c)), f(t2 == s2)


def _keep(mask01, x):
    return jnp.where(mask01 > 0.5, x, 0.0)


def _tri_inverse(a, diag_blk, levels, eye):
    n = a[0].shape[0]
    d = [_keep(diag_blk, x) for x in a]
    t = [eye + x for x in d]
    p = [_bdot(x, x) for x in d]
    for _ in range(int(math.log2(RWKV_INV_BLOCK)) - 2):
        res = [_bdot(jnp.concatenate([pi, ti], axis=0), pi) for pi, ti in zip(p, t)]
        p = [x[:n] for x in res]
        t = [ti + x[n:] for ti, x in zip(t, res)]
    t = [ti + _bdot(ti, pi) for ti, pi in zip(t, p)]
    for off in levels:
        half = [_bdot(ti, _keep(off, x)) for ti, x in zip(t, a)]
        t = [ti + _bdot(x, ti) for ti, x in zip(t, half)]
    return t


def _rwkv_pair_chunks(ins, sts, tris, gmasks, diag_blk, levels, same_head, eye):
    c, w = ins[0][0].shape
    hd = w // 2
    nch = range(len(ins))
    r, ld, kb, vb, kk, a = (list(z) for z in zip(*ins))
    split = [_split2(x) for x in ld]
    lcum = [jnp.dot(tris[i], jnp.concatenate(split[i], axis=1), preferred_element_type=F32) for i in nch]
    lcum = [x[:, :w] + x[:, w:] for x in lcum]
    ltot = [jnp.sum(x, axis=0, keepdims=True) for x in ld]
    beta = [kk[i] * a[i] for i in nch]
    eneg = [jnp.exp(-x) for x in lcum]
    abar = [-kk[i] * jnp.exp(lcum[i] - ld[i]) for i in nch]
    rbar = [r[i] * jnp.exp(lcum[i]) for i in nch]
    bt = [(beta[i] * eneg[i]).astype(BF16) for i in nch]
    kt = [(kb[i] * eneg[i]).astype(BF16) for i in nch]
    vbb = [x.astype(BF16) for x in vb]
    head0 = lax.broadcasted_iota(jnp.int32, (c, w), 1) < hd
    pick = lambda res: jnp.where(head0, res[:c], res[c:])
    arst = [_bdot(jnp.concatenate([abar[i], rbar[i]], axis=0), sts[i], NT) for i in nch]
    lhs = [jnp.concatenate([jnp.where(head0, abar[i], 0.0), jnp.where(head0, 0.0, abar[i]),
                            jnp.where(head0, rbar[i], 0.0), jnp.where(head0, 0.0, rbar[i])], axis=0) for i in nch]
    g = [_keep(gmasks[i], _bdot(lhs[i], jnp.concatenate([bt[i], bt[i], kt[i], kt[i]], axis=0), NT))
         for i in nch]
    x = [arst[i][:c] + pick(_bdot(g[i][:2 * c, 2 * c:], jnp.concatenate([vbb[i], vbb[i]], axis=0))) for i in nch]
    tinv = _tri_inverse([gi[:2 * c, :2 * c] for gi in g], diag_blk, levels, eye)
    u = [pick(_bdot(tinv[i], jnp.concatenate([x[i], x[i]], axis=0))) for i in nch]
    ub = [z.astype(BF16) for z in u]
    y = [arst[i][c:] + pick(_bdot(g[i][2 * c:], jnp.concatenate([ub[i], ub[i], vbb[i], vbb[i]], axis=0)))
         for i in nch]
    erem = [jnp.exp(ltot[i] - lcum[i]) for i in nch]
    bkh = [jnp.concatenate([beta[i] * erem[i], kb[i] * erem[i]], axis=0) for i in nch]
    st_new = [jnp.exp(ltot[i]) * sts[i]
              + _keep(same_head, _bdot(jnp.concatenate([ub[i], vbb[i]], axis=0), bkh[i], TN)) for i in nch]
    return y, st_new


def _rwkv_scan_kernel(*refs, t, c, npair, latent):
    prep_in, (lnw_ref, lnb_ref), rest = refs[:N_PREP_IN], refs[N_PREP_IN:N_PREP_IN + 2], refs[N_PREP_IN + 2:]
    if latent:
        s0_ref, _, o_ref, *scratch = rest
    else:
        o_ref, sfin_ref, *scratch = rest
    *prep_out, yb_ref = scratch
    _rwkv_prep_kernel(*prep_in, *prep_out)
    r_ref, ldf_ref, ldb_ref, kb_ref, vb_ref, kk_ref, a_ref, gate_ref, bonus_ref = prep_out
    n = t // c
    w = 2 * HD_B
    masks = [_rwkv_masks(c, rev) for rev in (False, True)]

    def body(ci, states):
        ins, tris, gmasks, dsts = [], [], [], []
        for dr, (ld_ref, dst) in enumerate(((ldf_ref, o_ref), (ldb_ref, yb_ref))):
            cj = ci if dr == 0 else n - 1 - ci
            rows = pl.ds(pl.multiple_of(cj * c, c), c)
            for p in range(npair):
                cols = slice(p * w, (p + 1) * w)
                ins.append(tuple(ref[rows, cols] for ref in (r_ref, ld_ref, kb_ref, vb_ref, kk_ref, a_ref)))
                tris.append(masks[dr][0])
                gmasks.append(masks[dr][1])
                dsts.append((dst, rows, cols))
        ys, new_states = _rwkv_pair_chunks(ins, list(states), tris, gmasks, *masks[0][2:])
        for (dst, rows, cols), y in zip(dsts, ys):
            dst[rows, cols] = y
        return tuple(new_states)

    if latent:
        zero = jnp.zeros((HD_B, HD_B), F32)
        init = tuple(jnp.concatenate([jnp.concatenate([s0_ref[0, dr, 2 * p], zero], axis=1),
                                      jnp.concatenate([zero, s0_ref[0, dr, 2 * p + 1]], axis=1)], axis=0)
                     for dr in range(2) for p in range(npair))
    else:
        init = tuple(jnp.zeros((w, w), F32) for _ in range(2 * npair))
    fin = lax.fori_loop(0, n, body, init, unroll=True if n <= RWKV_UNROLL_MAX else 2)
    segm = _seg_matrix(w, 6, 1.0 / HD_B)
    for p in range(npair):
        if not latent:
            for dr in range(2):
                st = fin[dr * npair + p]
                sfin_ref[0, dr, 2 * p] = st[:HD_B, :HD_B]
                sfin_ref[0, dr, 2 * p + 1] = st[HD_B:, HD_B:]
        cols = slice(p * w, (p + 1) * w)
        y = o_ref[:, cols] + yb_ref[:, cols]
        yc = y - _split_dot(y, segm)
        yn = yc * lax.rsqrt(_split_dot(yc * yc, segm) + RWKV_GN_EPS)
        o_ref[:, cols] = (yn * lnw_ref[:, cols] + lnb_ref[:, cols] + bonus_ref[:, cols]) * gate_ref[:, cols]


def _rwkv(p, prm, ln_w, ln_b, latent, s0=None, dst=None):
    mu, k_k, k_a, r_k, w0, w_up, a0, a_up, g_up = prm
    nseq, t, rb0 = (DEC_BATCH, DEC_SEQ, M_CTX // DEC_SEQ) if latent else (BATCH, SEQ, 0)
    npair = H_B // 2
    lw = LORA_W + LORA_A + LORA_G
    c0 = (IN_AB - 3 * MIX - lw) // MIX
    cl = (IN_AB - lw) // lw
    full = lambda *shape: pl.BlockSpec(shape, lambda b: (0,) * len(shape))
    st_spec = pl.BlockSpec((1, 2, H_B, HD_B, HD_B), lambda b: (b, 0, 0, 0, 0))
    o_spec = pl.BlockSpec((t, MIX), lambda b: (rb0 + b, 0))
    o_shape = jax.ShapeDtypeStruct((M_ALL, MIX), F32)
    in_specs = [pl.BlockSpec((t, MIX), lambda b: (rb0 + b, c0)),
                pl.BlockSpec((t, MIX), lambda b: (rb0 + b, c0 + 1)),
                pl.BlockSpec((t, MIX), lambda b: (rb0 + b, c0 + 2)),
                pl.BlockSpec((t, lw), lambda b: (rb0 + b, cl)),
                full(1, MIX), full(1, MIX), full(1, MIX), full(1, lw),
                full(1, MIX), full(1, MIX), full(1, MIX),
                full(2, MIX), full(2, LORA_W, MIX), full(1, MIX), full(LORA_A, MIX), full(LORA_G, MIX),
                full(1, MIX), full(1, MIX)]
    args = [p, p, p, p,
            mu[None, 0:MIX], mu[None, MIX:2 * MIX], mu[None, 2 * MIX:3 * MIX], mu[None, 3 * MIX:],
            k_k.reshape(1, MIX), k_a.reshape(1, MIX), r_k.reshape(1, MIX), w0, w_up,
            a0.reshape(1, MIX), a_up, g_up, ln_w.reshape(1, MIX), ln_b.reshape(1, MIX)]
    if latent:
        args += [s0, dst]
        in_specs += [st_spec, pl.BlockSpec(memory_space=pl.ANY)]
        out_specs, out_shape, aliases = o_spec, o_shape, {len(args) - 1: 0}
    else:
        out_specs = [o_spec, st_spec]
        out_shape = [o_shape, jax.ShapeDtypeStruct((nseq, 2, H_B, HD_B, HD_B), F32)]
        aliases = {}
    return pl.pallas_call(
        functools.partial(_rwkv_scan_kernel, t=t, c=RWKV_CHUNK, npair=npair, latent=latent),
        grid=(nseq,),
        in_specs=in_specs, out_specs=out_specs, out_shape=out_shape,
        scratch_shapes=[pltpu.VMEM((t, MIX), F32)] * (N_PREP_OUT + 1),
        input_output_aliases=aliases,
        compiler_params=_params("parallel"),
        name="rwkv_lat" if latent else "rwkv_ctx",
    )(*args)


S5_L = 16
S5_ROWS = M_ALL // S5_L
S5_CW = S5_L * S5_GROUP
S5_GS = 8
S5_PW = 2 * P_C
S5_TAB = 24
S5_PRM = 2 * S5_GROUP
S5_NC_CTX = SEQ // S5_L
S5_NC_LAT = DEC_SEQ // S5_L
S5_CTX_ROWS = BATCH * S5_NC_CTX


def _gelu_tanh(x):
    return 0.5 * x * (1.0 + jnp.tanh(math.sqrt(2.0 / math.pi) * (x + 0.044715 * (x * x * x))))


def _cmul(ar, ai, br, bi):
    return ar * br - ai * bi, ar * bi + ai * br


def _dot3(a, b, dims):
    ah, al = _split2(a)
    bh, bl = _split2(b)
    d = lambda x, y: lax.dot_general(x, y, dims, preferred_element_type=F32)
    return d(ah, bh) + d(ah, bl) + d(al, bh)


def _s5_param_kernel(prm_ref, lam_ref, tab_ref, r_ref, st_ref):
    nl, ns = S5_L, S5_GROUP
    lane_blk = lax.broadcasted_iota(jnp.int32, (ns, S5_CW), 1) // ns
    kf = lax.broadcasted_iota(jnp.int32, (S5_TAB, P_C), 0).astype(F32)
    t_rows = [jnp.zeros((ns, S5_CW), F32) for _ in range(nl)]
    e_parts = []
    for d in range(2):
        lr, li = lam_ref[0, d, 0:1, :], lam_ref[0, d, 1:2, :]
        dt = jnp.exp(lam_ref[0, d, 2:3, :])
        mag = jnp.exp(lr * dt * kf)
        pw_re, pw_im = mag * jnp.cos(li * dt * kf), mag * jnp.sin(li * dt * kf)
        tab_ref[0, d, 0] = pw_re
        tab_ref[0, d, 1] = pw_im
        nr, ab_im, den = pw_re[1:2] - 1.0, pw_im[1:2], lr * lr + li * li
        f = ((nr * lr + ab_im * li) / den, (ab_im * lr - nr * li) / den)
        part = lambda lo, n: (prm_ref[0, d, 0, lo:lo + n, :], prm_ref[0, d, 1, lo:lo + n, :])
        bb = _cmul(*f, *part(0, ns))
        cc = part(ns, ns)
        pw = lambda k: (pw_re[k:k + 1], pw_im[k:k + 1])
        steps = range(nl)
        if d == 0:
            ke, ks, kk = [nl - 1 - j for j in steps], [j + 1 for j in steps], list(steps)
        else:
            ke, ks, kk = list(steps), [nl - j for j in steps], [nl - 1 - j for j in steps]
        stack = lambda xs: (jnp.concatenate([x[0] for x in xs], axis=0), jnp.concatenate([x[1] for x in xs], axis=0))
        e_re, e_im = stack([_cmul(*bb, *pw(k)) for k in ke])
        s_re, s_im = stack([_cmul(*cc, *pw(k)) for k in ks])
        k_re, k_im = stack([_cmul(*cc, *pw(k)) for k in kk])
        st_ref[0, d] = jnp.concatenate([s_re, -s_im], axis=1).astype(BF16)
        krow = _dot3(bb[0], k_re, NT) - _dot3(bb[1], k_im, NT)
        for j in steps:
            if d == 0:
                shifted, keep = pltpu.roll(krow, ns * j, axis=1), lane_blk >= j
            else:
                shifted, keep = pltpu.roll(krow, (S5_CW - ns * (nl - 1 - j)) % S5_CW, axis=1), lane_blk <= j
            t_rows[j] = t_rows[j] + jnp.where(keep, shifted, 0.0)
        e_parts += [e_re, e_im]
    r_ref[0] = jnp.concatenate([jnp.concatenate(t_rows, axis=0)] + e_parts, axis=1).astype(BF16)


def _s5_params(prm, lam):
    tab_spec = pl.BlockSpec((1, 2, 2, S5_TAB, P_C), lambda g: (g, 0, 0, 0, 0))
    return pl.pallas_call(
        _s5_param_kernel,
        grid=(G_C,),
        in_specs=[pl.BlockSpec((1, 2, 2, S5_PRM, P_C), lambda g: (g, 0, 0, 0, 0)),
                  pl.BlockSpec((1, 2, 8, P_C), lambda g: (g, 0, 0, 0))],
        out_specs=[tab_spec,
                   pl.BlockSpec((1, S5_CW, 2 * S5_CW), lambda g: (g, 0, 0)),
                   pl.BlockSpec((1, 2, S5_CW, S5_PW), lambda g: (g, 0, 0, 0))],
        out_shape=[jax.ShapeDtypeStruct((G_C, 2, 2, S5_TAB, P_C), F32),
                   jax.ShapeDtypeStruct((G_C, S5_CW, 2 * S5_CW), BF16),
                   jax.ShapeDtypeStruct((G_C, 2, S5_CW, S5_PW), BF16)],
        compiler_params=_params("parallel"),
        name="s5_params",
    )(prm, lam)


def _s5_core_kernel(u_ref, r_ref, st_ref, tab_ref, h0_ref, o_ref, hfin_ref, ug_scr, yg_scr, e_scr, hp_scr):
    ns, nl = S5_GROUP, S5_L
    per_v = 128 // ns
    blk = lax.broadcasted_iota(jnp.int32, (8, 128), 1) // ns

    def merge(select):
        acc = select(0)
        for b in range(1, per_v):
            acc = jnp.where(blk == b, select(b), acc)
        return acc

    def shuffle(srcs):
        rolled = []
        for s in range(per_v):
            m = merge(lambda b: srcs[(b + s) % per_v])
            rolled.append(m if s == 0 else pltpu.roll(m, s * ns, axis=1))
        return [merge(lambda b: rolled[(b - a) % per_v]) for a in range(per_v)]

    def tiles(regroup):
        def ctx_tile(ti, _):
            c, b0 = ti // (BATCH // 8), (ti % (BATCH // 8)) * 8
            regroup(b0 * SEQ + c * nl, SEQ, pl.multiple_of(c * BATCH + b0, 8))
            return 0

        def lat_tile(ti, _):
            b, c0 = ti // (S5_NC_LAT // 8), (ti % (S5_NC_LAT // 8)) * 8
            regroup(M_CTX + b * DEC_SEQ + c0 * nl, nl, pl.multiple_of(S5_CTX_ROWS + b * S5_NC_LAT + c0, 8))
            return 0

        lax.fori_loop(0, S5_CTX_ROWS // 8, ctx_tile, 0, unroll=4)
        lax.fori_loop(0, (S5_ROWS - S5_CTX_ROWS) // 8, lat_tile, 0, unroll=4)

    def regroup_in(tok, stride, row):
        uj = [u_ref[pl.ds(tok + j, 8, stride=stride), :] for j in range(nl)]
        for jh in range(nl // per_v):
            for g, v in enumerate(shuffle(uj[jh * per_v:(jh + 1) * per_v])):
                ug_scr[g, pl.ds(row, 8), jh * 128:(jh + 1) * 128] = v

    tiles(regroup_in)

    chains = [(g, d) for g in range(S5_GS) for d in range(2)]
    for g in range(S5_GS):
        a = jnp.dot(ug_scr[g].astype(BF16), r_ref[g], preferred_element_type=F32)
        yg_scr[g] = a[:, 0:S5_CW]
        for d in range(2):
            e_scr[g, d] = a[:, S5_CW + d * S5_PW:S5_CW + (d + 1) * S5_PW]
    lam = []
    for g, d in chains:
        lr, li = (tab_ref[g, d, ri, nl:nl + 1, :] for ri in range(2))
        lam.append((jnp.concatenate([lr, lr], axis=1), jnp.concatenate([-li, li], axis=1)))

    def advance(c, hs, latent):
        nc = S5_NC_LAT if latent else S5_NC_CTX
        out = []
        for i, (g, d) in enumerate(chains):
            cc = c if d == 0 else nc - 1 - c
            rows = (pl.ds(S5_CTX_ROWS + cc, DEC_BATCH, stride=nc) if latent
                    else pl.ds(pl.multiple_of(cc * BATCH, BATCH), BATCH))
            hp_scr[g, d, rows, :] = hs[i]
            out.append(lam[i][0] * hs[i] + lam[i][1] * pltpu.roll(hs[i], P_C, axis=1) + e_scr[g, d, rows, :])
        return out

    def both(c, carry):
        hc, hl = carry
        return tuple(advance(c, hc, False)), tuple(advance(c, hl, True))

    h_ctx = tuple(jnp.zeros((BATCH, S5_PW), F32) for _ in chains)
    h_lat = tuple(h0_ref[g, d] for g, d in chains)
    h_ctx, h_lat = lax.fori_loop(0, S5_NC_CTX, both, (h_ctx, h_lat), unroll=2)
    lax.fori_loop(S5_NC_CTX, S5_NC_LAT, lambda c, hl: tuple(advance(c, hl, True)), h_lat, unroll=4)
    for i, (g, d) in enumerate(chains):
        hfin_ref[g, d] = h_ctx[i]
    for g in range(S5_GS):
        y = yg_scr[g]
        for d in range(2):
            y = y + lax.dot_general(hp_scr[g, d].astype(BF16), st_ref[g, d], NT, preferred_element_type=F32)
        yg_scr[g] = y

    def regroup_out(tok, stride, row):
        for jh in range(nl // per_v):
            yv = [yg_scr[g, pl.ds(row, 8), jh * 128:(jh + 1) * 128] for g in range(S5_GS)]
            for jj, v in enumerate(shuffle(yv)):
                o_ref[pl.ds(tok + jh * per_v + jj, 8, stride=stride), :] = v

    tiles(regroup_out)


def _s5_core(p2, r, st, tab, h0):
    blk = lambda *tail: pl.BlockSpec((S5_GS,) + tail, lambda s: (s,) + (0,) * len(tail))
    col = pl.BlockSpec((M_ALL, S5_GS * S5_GROUP), lambda s: (0, s))
    return pl.pallas_call(
        _s5_core_kernel,
        grid=(G_C // S5_GS,),
        in_specs=[col, blk(S5_CW, 2 * S5_CW), blk(2, S5_CW, S5_PW),
                  blk(2, 2, S5_TAB, P_C), blk(2, DEC_BATCH, S5_PW)],
        out_specs=[col, blk(2, BATCH, S5_PW)],
        out_shape=[jax.ShapeDtypeStruct((M_ALL, MIX), F32),
                   jax.ShapeDtypeStruct((G_C, 2, BATCH, S5_PW), F32)],
        scratch_shapes=[pltpu.VMEM((S5_GS, S5_ROWS, S5_CW), F32),
                        pltpu.VMEM((S5_GS, S5_ROWS, S5_CW), F32),
                        pltpu.VMEM((S5_GS, 2, S5_ROWS, S5_PW), F32),
                        pltpu.VMEM((S5_GS, 2, S5_ROWS, S5_PW), F32)],
        compiler_params=_params("parallel"),
        name="s5_core",
    )(p2, r, st, tab, h0)


def _s5_tables(lam_re, lam_im, log_dt, b_re, b_im, c_re, c_im):
    rows = lambda b, c: jnp.concatenate([jnp.swapaxes(b, -1, -2), c], axis=2)
    prm = jnp.stack([rows(b_re, c_re), rows(b_im, c_im)], axis=1)
    dt = jnp.broadcast_to(log_dt[:, :, None], lam_re.shape)
    lam = jnp.stack([lam_re, lam_im, dt] + [jnp.zeros_like(dt)] * 5, axis=2)
    return jnp.transpose(prm, (2, 0, 1, 3, 4)), jnp.transpose(lam, (1, 0, 2, 3))


def _s5_mixer(p2, tables, state_re, state_im):
    tab, r, st = _s5_params(*tables)
    h0 = jnp.concatenate([state_re, state_im], axis=-1)
    y_tok, hfin = _s5_core(p2, r, st, tab, jnp.transpose(h0, (2, 1, 0, 3)))
    hfin = jnp.transpose(hfin, (2, 1, 0, 3))
    return y_tok, hfin[..., :P_C], hfin[..., P_C:]


def _ret_kernel(*refs, t, latent):
    if latent:
        (q_ref, k_ref, v_ref, g_ref, dl_ref, lnw_ref, lnb_ref, r0_ref, cos_ref, sin_ref, _,
         o_ref, ob_ref) = refs
    else:
        q_ref, k_ref, v_ref, g_ref, dl_ref, lnw_ref, lnb_ref, o_ref, rfin_ref, ob_ref = refs
    n = t // CHUNK
    w = HD_D
    jf = lax.broadcasted_iota(jnp.int32, (CHUNK, CHUNK), 0).astype(F32)
    kf = lax.broadcasted_iota(jnp.int32, (CHUNK, CHUNK), 1).astype(F32)
    diff = jf - kf
    tabs = []
    for dr in range(2):
        for h in range(H_D):
            l = -_softplus(-dl_ref[h, dr:dr + 1, :])
            if dr == 0:
                dmat = jnp.where(diff >= 0, jnp.exp(l * jnp.maximum(diff, 0.0)), 0.0)
                xi = jnp.exp(l * (jf + 1.0))
                zeta = jnp.exp(l * (CHUNK - 1.0 - jf))
            else:
                dmat = jnp.where(diff < 0, jnp.exp(l * jnp.maximum(-diff, 0.0)), 0.0)
                xi = jnp.exp(l * (CHUNK - jf))
                zeta = jnp.exp(l * jf)
            tabs.append((dmat, xi, zeta, jnp.exp(l * CHUNK)))
    chains = [(dr, h) for dr in range(2) for h in range(H_D)]

    def body(ci, states):
        q, k, v, where = [], [], [], []
        for dr, h in chains:
            cj = ci if dr == 0 else n - 1 - ci
            rows = pl.ds(pl.multiple_of(cj * CHUNK, CHUNK), CHUNK)
            cols = slice(h * w, (h + 1) * w)
            qi = q_ref[rows, cols]
            ki = k_ref[rows, cols] * (HD_D ** -0.5)
            if latent:
                qi = _rope_pairs(qi, cos_ref[rows, :], sin_ref[rows, :], HD_D // 2)
                ki = _rope_pairs(ki, cos_ref[rows, :], sin_ref[rows, :], HD_D // 2)
            q.append(qi.astype(BF16))
            k.append(ki)
            v.append(v_ref[rows, cols].astype(BF16))
            where.append((o_ref if dr == 0 else ob_ref, rows, cols))
        nch = range(len(chains))
        inner = [(_bdot(q[i], k[i], NT) * tabs[i][0]).astype(BF16) for i in nch]
        cross = [_bdot(q[i], states[i]) * tabs[i][1] for i in nch]
        kz = [(k[i] * tabs[i][2]).astype(BF16) for i in nch]
        out = [jnp.dot(inner[i], v[i], preferred_element_type=F32) + cross[i] for i in nch]
        new_states = [states[i] * tabs[i][3] + lax.dot_general(kz[i], v[i], TN, preferred_element_type=F32)
                      for i in nch]
        for (dst, rows, cols), y in zip(where, out):
            dst[rows, cols] = y
        return tuple(new_states)

    if latent:
        lax.fori_loop(0, n, body, tuple(r0_ref[0, dr, h] for dr, h in chains), unroll=2)
    else:
        fin = lax.fori_loop(0, n, body, tuple(jnp.zeros((w, w), F32) for _ in chains), unroll=True)
        for i, (dr, h) in enumerate(chains):
            rfin_ref[0, dr, h] = fin[i]
    for h in range(H_D):
        cols = slice(h * w, (h + 1) * w)
        y = o_ref[:, cols] + ob_ref[:, cols]
        yc = y - jnp.mean(y, axis=-1, keepdims=True)
        yn = yc * lax.rsqrt(jnp.mean(yc * yc, axis=-1, keepdims=True) + RET_GN_EPS)
        o_ref[:, cols] = (yn * lnw_ref[:, cols] + lnb_ref[:, cols]) * _silu(g_ref[:, cols])


def _retention(p, decay_logit, ln_w, ln_b, latent, r0=None, rope=None, dst=None):
    if latent:
        nseq, t, rb0 = DEC_BATCH, DEC_SEQ, M_CTX // DEC_SEQ
    else:
        nseq, t, rb0 = BATCH, SEQ, 0
    w = HD_D
    dl = jnp.broadcast_to(decay_logit.T[:, :, None], (H_D, 2, w))
    full = lambda shape: pl.BlockSpec(shape, lambda b: (0,) * len(shape))
    st_spec = pl.BlockSpec((1, 2, H_D, w, w), lambda b: (b, 0, 0, 0, 0))
    in_specs = [pl.BlockSpec((t, MIX), lambda b, j=j: (rb0 + b, j)) for j in (1, 2, 3, 4)]
    in_specs += [full((H_D, 2, w)), full((1, MIX)), full((1, MIX))]
    args = [p, p, p, p, dl, ln_w.reshape(1, MIX), ln_b.reshape(1, MIX)]
    o_spec = pl.BlockSpec((t, MIX), lambda b: (rb0 + b, 0))
    o_shape = jax.ShapeDtypeStruct((M_ALL, MIX), F32)
    if latent:
        in_specs += [st_spec, full((t, w)), full((t, w)), pl.BlockSpec(memory_space=pl.ANY)]
        args += [r0, *rope, dst]
        out_specs, out_shape, aliases = o_spec, o_shape, {len(args) - 1: 0}
    else:
        out_specs = [o_spec, st_spec]
        out_shape = [o_shape, jax.ShapeDtypeStruct((nseq, 2, H_D, w, w), F32)]
        aliases = {}
    return pl.pallas_call(
        functools.partial(_ret_kernel, t=t, latent=latent),
        grid=(nseq,),
        in_specs=in_specs, out_specs=out_specs, out_shape=out_shape,
        scratch_shapes=[pltpu.VMEM((t, MIX), F32)],
        input_output_aliases=aliases,
        compiler_params=_params("parallel"),
        name="retention_lat" if latent else "retention_ctx",
    )(*args)


def _rope_tables(n_tok, dim, reps):
    rows = n_tok // GRID_W
    n_freq = dim // 4
    inv = 1.0 / (ROPE_THETA ** (jnp.arange(n_freq, dtype=F32) / n_freq))
    row = jnp.repeat(jnp.arange(rows, dtype=F32), GRID_W)
    col = jnp.tile(jnp.arange(GRID_W, dtype=F32), rows)
    ang = jnp.concatenate([row[:, None] * inv, col[:, None] * inv], axis=-1)
    cos, sin = jnp.cos(ang), jnp.sin(ang)
    return jnp.tile(jnp.concatenate([cos, cos], axis=1), (1, reps)), \
        jnp.tile(jnp.concatenate([-sin, sin], axis=1), (1, reps))


def kernel(x_prompt, x_sample, cache_k_ab, cache_v_ab, state_rwkv, state_s5_re, state_s5_im, state_ret, c, c_ctx, norm1_g, norm2_g, w_mod, b_mod, w_ff_gate, w_ff_up, w_ff_down, w_in_ab, w_out_ab, qk_gain_a, lambda_qk, subln_g, rwkv_mu, rwkv_k_k, rwkv_k_a, rwkv_r_k, rwkv_w0, rwkv_w_up, rwkv_a0, rwkv_a_up, rwkv_g_up, rwkv_ln_w, rwkv_ln_b, w_in_cd, w_out_cd, s5_lam_re, s5_lam_im, s5_log_dt, s5_b_re, s5_b_im, s5_c_re, s5_c_im, s5_d, s5_w_glu, ret_decay_logit, ret_ln_w, ret_ln_b):
    d = D_MODEL
    xs = [x_prompt.reshape(M_CTX, d), x_sample.reshape(M_LAT, d)]
    cvec = jnp.zeros((MOD_ROWS, d), F32).at[0].set(c_ctx).at[1:1 + DEC_BATCH].set(c)
    mods = _modulation(cvec, w_mod, b_mod)

    lam_init = 0.8 - 0.6 * math.exp(-0.3 * 0)
    p = _norm_linear(xs, norm1_g[0], mods[0], w_in_ab)
    gain2 = jnp.tile(qk_gain_a[0], (1, 2))
    rope_a = _rope_tables(DEC_SEQ, HD_A, 2)
    ck = cache_k_ab[:, 0].reshape(DEC_BATCH * PAST_LEN, MIX)
    cv = cache_v_ab[:, 0].reshape(DEC_BATCH * PAST_LEN, MIX)
    oa, k_ctx, v_ctx = _attention(p, gain2, lambda_qk[0], subln_g[0], lam_init, latent=False)
    oa = _attention(p, gain2, lambda_qk[0], subln_g[0], lam_init, latent=True,
                    cache_k=ck, cache_v=cv, rope=rope_a, dst=oa)
    rw_prm = (rwkv_mu[0], rwkv_k_k[0], rwkv_k_a[0], rwkv_r_k[0], rwkv_w0[0], rwkv_w_up[0],
              rwkv_a0[0], rwkv_a_up[0], rwkv_g_up[0])
    ob, sfin_ctx = _rwkv(p, rw_prm, rwkv_ln_w[0], rwkv_ln_b[0], latent=False)
    ob = _rwkv(p, rw_prm, rwkv_ln_w[0], rwkv_ln_b[0], latent=True, s0=state_rwkv[:, 0], dst=ob)
    ffn_w = (w_ff_gate.astype(BF16), w_ff_up.astype(BF16), w_ff_down.astype(BF16))
    x = _mix_ffn(xs, oa, ob, w_out_ab[0].astype(BF16), norm2_g[0], mods[0], *ffn_w, layer=0, split_out=False)

    p2 = _norm_linear([x], norm1_g[1], mods[1], w_in_cd)
    tables = _s5_tables(s5_lam_re[0], s5_lam_im[0], s5_log_dt[0], s5_b_re[0], s5_b_im[0],
                        s5_c_re[0], s5_c_im[0])
    y_s5, s5_fin_re, s5_fin_im = _s5_mixer(p2, tables, state_s5_re[:, 0], state_s5_im[:, 0])
    oc = (y_s5, p2, s5_d[0], s5_w_glu[0].astype(BF16))
    rope_d = _rope_tables(DEC_SEQ, HD_D, 1)
    od, rfin = _retention(p2, ret_decay_logit[0], ret_ln_w[0], ret_ln_b[0], latent=False)
    od = _retention(p2, ret_decay_logit[0], ret_ln_w[0], ret_ln_b[0], latent=True, r0=state_ret[:, 0],
                    rope=rope_d, dst=od)
    y_ctx, y_lat = _mix_ffn([x], oc, od, w_out_cd[0].astype(BF16), norm2_g[1], mods[1], *ffn_w, layer=1,
                            split_out=True)
    y_prompt = y_ctx.reshape(BATCH, SEQ, d)
    y_sample = y_lat.reshape(DEC_BATCH, DEC_SEQ, d)
    new_k = jnp.transpose(k_ctx, (0, 4, 1, 2, 3))[:, None]
    new_v = v_ctx.reshape(BATCH, 1, SEQ, H_A, VD_A)
    new_rwkv = sfin_ctx[:, None]
    new_s5_re = s5_fin_re[:, None]
    new_s5_im = s5_fin_im[:, None]
    new_ret = rfin[:, None]
    return (y_prompt, y_sample, new_k, new_v, new_rwkv, new_s5_re, new_s5_im, new_ret)
```

```python
import functools
import math

import numpy as np
import jax
import jax.numpy as jnp
from jax import lax
from jax.experimental import pallas as pl
from jax.experimental.pallas import tpu as pltpu

F32 = jnp.float32
BF16 = jnp.bfloat16
HIGHEST = lax.Precision.HIGHEST

D_MODEL = 1024
BATCH = 32
SEQ = 256
DEC_BATCH = 2
DEC_SEQ = 1024
PAST_LEN = 256
GRID_W = 64
H_A = 4
HD_A = 64
VD_A = 128
H_B = 8
HD_B = 64
MIX = 512
LORA_W = 64
LORA_A = 64
LORA_G = 128
S5_GROUP = 16
G_C = 32
P_C = 64
S5_STATE = G_C * P_C
H_D = 4
HD_D = 128
CHUNK = 128
D_FF = 2816
IN_AB = 3328
IN_CD = 2560
ROPE_THETA = 10000.0
NORM_EPS = 1e-6
RWKV_GN_EPS = 64e-5
RET_GN_EPS = 1e-5

M_CTX = BATCH * SEQ
M_LAT = DEC_BATCH * DEC_SEQ
M_ALL = M_CTX + M_LAT
MOD_ROWS = 8
RWKV_CHUNK = 64
VMEM_LIMIT = 56 * 1024 * 1024

NN = (((1,), (0,)), ((), ()))
NT = (((1,), (1,)), ((), ()))
TN = (((0,), (0,)), ((), ()))


def _params(*sem):
    return pltpu.CompilerParams(dimension_semantics=sem, vmem_limit_bytes=VMEM_LIMIT)


def _bdot(a, b, dims=NN):
    return lax.dot_general(a.astype(BF16), b.astype(BF16), dims, preferred_element_type=F32)


def _hdot(a, b, dims=NN):
    return lax.dot_general(a, b, dims, precision=HIGHEST, preferred_element_type=F32)


def _split_dot(x, m):
    hi = x.astype(BF16)
    lo = (x - hi.astype(F32)).astype(BF16)
    return (jnp.dot(hi, m, preferred_element_type=F32) + jnp.dot(lo, m, preferred_element_type=F32))


def _seg_matrix(n, shift, val):
    r = lax.broadcasted_iota(jnp.int32, (n, n), 0) >> shift
    c = lax.broadcasted_iota(jnp.int32, (n, n), 1) >> shift
    return jnp.where(r == c, val, 0.0).astype(BF16)


def _sigmoid(x):
    return jax.nn.sigmoid(x)


def _silu(x):
    return x * jax.nn.sigmoid(x)


def _softplus(x):
    return jnp.maximum(x, 0.0) + jnp.log(1.0 + jnp.exp(-jnp.abs(x)))


def _mod_row(tile, tm):
    r0 = tile * tm
    return jnp.where(r0 < M_CTX, 0, 1 + (r0 - M_CTX) // DEC_SEQ)


def _norm_mod(x, g, sc_ref, sh_ref, row):
    y = x * lax.rsqrt(jnp.mean(x * x, axis=-1, keepdims=True) + NORM_EPS) * g
    return y * (1.0 + sc_ref[pl.ds(row, 1), :]) + sh_ref[pl.ds(row, 1), :]


def _mod_kernel(c_ref, w_ref, b_ref, o_ref):
    o_ref[0] = _bdot(_silu(c_ref[...]), w_ref[0]) + b_ref[0]


def _modulation(cvec, w_mod, b_mod):
    depth, d, n6 = w_mod.shape
    tn = 1536
    return pl.pallas_call(
        _mod_kernel,
        grid=(depth, n6 // tn),
        in_specs=[pl.BlockSpec((MOD_ROWS, d), lambda l, j: (0, 0)),
                  pl.BlockSpec((1, d, tn), lambda l, j: (l, 0, j)),
                  pl.BlockSpec((1, 1, tn), lambda l, j: (l, 0, j))],
        out_specs=pl.BlockSpec((1, MOD_ROWS, tn), lambda l, j: (l, 0, j)),
        out_shape=jax.ShapeDtypeStruct((depth, MOD_ROWS, n6), F32),
        compiler_params=_params("parallel", "parallel"),
        name="modulation",
    )(cvec, w_mod, b_mod.reshape(depth, 1, n6))


def _tile_specs(tm, d):
    nc = M_CTX // tm
    return [pl.BlockSpec((tm, d), lambda i: (jnp.minimum(i, nc - 1), 0)),
            pl.BlockSpec((tm, d), lambda i: (jnp.maximum(i - nc, 0), 0))]


def _pick_rows(refs, tm):
    if len(refs) == 1:
        return refs[0][...]
    return jnp.where(pl.program_id(0) < M_CTX // tm, refs[0][...], refs[1][...])


def _norm_linear_kernel(*refs, tm):
    *x_refs, g_ref, sc_ref, sh_ref, w_ref, o_ref, wbf = refs

    @pl.when(pl.program_id(0) == 0)
    def _():
        wbf[...] = w_ref[0].astype(BF16)

    row = _mod_row(pl.program_id(0), tm)
    x = _pick_rows(x_refs, tm)
    nsub = 2
    rows = [slice(i * tm // nsub, (i + 1) * tm // nsub) for i in range(nsub)]
    hs = [_norm_mod(x[r], g_ref[...], sc_ref, sh_ref, row).astype(BF16) for r in rows]
    for r, h in zip(rows, hs):
        o_ref[r, :] = jnp.dot(h, wbf[...], preferred_element_type=F32)


def _norm_linear(xs, g, mods, w):
    tm = 1024 if len(xs) == 1 else 512
    d = D_MODEL
    n = w.shape[2]
    x_specs = _tile_specs(tm, d) if len(xs) == 2 else [pl.BlockSpec((tm, d), lambda i: (i, 0))]
    return pl.pallas_call(
        functools.partial(_norm_linear_kernel, tm=tm),
        grid=(M_ALL // tm,),
        in_specs=x_specs + [pl.BlockSpec((1, d), lambda i: (0, 0)),
                            pl.BlockSpec((MOD_ROWS, d), lambda i: (0, 1)),
                            pl.BlockSpec((MOD_ROWS, d), lambda i: (0, 0)),
                            pl.BlockSpec((1, d, n), lambda i: (0, 0, 0), pipeline_mode=pl.Buffered(1))],
        out_specs=pl.BlockSpec((tm, n), lambda i: (i, 0)),
        out_shape=jax.ShapeDtypeStruct((M_ALL, n), F32),
        scratch_shapes=[pltpu.VMEM((d, n), BF16)],
        compiler_params=_params("arbitrary"),
        name="norm_linear",
    )(*xs, g.reshape(1, d), mods, mods, w)


def _mix_ffn_kernel(*refs, tm, ck, n_in, n_out, glu):
    x_refs, refs = refs[:n_in], refs[n_in:]
    if glu:
        (y_ref, u_ref, d_ref, wglu_ref), refs = refs[:4], refs[4:]
        z = _gelu_tanh(y_ref[...] + d_ref[...] * u_ref[...])
        a = z * _sigmoid(jnp.dot(z.astype(BF16), wglu_ref[...], preferred_element_type=F32))
    else:
        a, refs = refs[0][...], refs[1:]
    b_ref, wa_ref, wb_ref, g1_ref, ng_ref, sc_ref, sh_ref, g2_ref, wg_ref, wu_ref, wd_ref = refs[:11]
    o_refs = refs[11:]
    row = _mod_row(pl.program_id(0), tm)
    mix = (jnp.dot(a.astype(BF16), wa_ref[...], preferred_element_type=F32)
           + jnp.dot(b_ref[...].astype(BF16), wb_ref[...], preferred_element_type=F32))
    x = _pick_rows(x_refs, tm) + g1_ref[pl.ds(row, 1), :] * mix
    h = _norm_mod(x, ng_ref[...], sc_ref, sh_ref, row).astype(BF16)
    acc = jnp.zeros((tm, D_MODEL), F32)
    for c in range(D_FF // ck):
        gg = jnp.dot(h, wg_ref[0, :, c * ck:(c + 1) * ck], preferred_element_type=F32)
        uu = jnp.dot(h, wu_ref[0, :, c * ck:(c + 1) * ck], preferred_element_type=F32)
        act = (_silu(gg) * uu).astype(BF16)
        acc = acc + jnp.dot(act, wd_ref[0, c * ck:(c + 1) * ck, :], preferred_element_type=F32)
    y = x + g2_ref[pl.ds(row, 1), :] * acc
    if n_out == 1:
        o_refs[0][...] = y
    else:
        is_ctx = pl.program_id(0) < M_CTX // tm

        @pl.when(is_ctx)
        def _():
            o_refs[0][...] = y

        @pl.when(jnp.logical_not(is_ctx))
        def _():
            o_refs[1][...] = y


def _mix_ffn(xs, oa, ob, w_out_bf16, norm_g, mods, wg, wu, wd, layer, split_out):
    tm, ck = 512, 256
    d = D_MODEL
    row_spec = lambda n: pl.BlockSpec((tm, n), lambda i: (i, 0))
    const = lambda shape, idx: pl.BlockSpec(shape, lambda i: idx, pipeline_mode=pl.Buffered(1))
    mod = lambda j: pl.BlockSpec((MOD_ROWS, d), lambda i: (0, j))
    x_specs = _tile_specs(tm, d) if len(xs) == 2 else [row_spec(d)]
    glu = isinstance(oa, tuple)
    if glu:
        y_tok, p2, d_skip, w_glu = oa
        a_specs = [row_spec(MIX), row_spec(MIX), pl.BlockSpec((1, MIX), lambda i: (0, 0)), const((MIX, MIX), (0, 0))]
        a_args = [y_tok, p2, d_skip.reshape(1, MIX), w_glu]
    else:
        a_specs, a_args = [row_spec(MIX)], [oa]
    if split_out:
        out_specs = _tile_specs(tm, d)
        out_shape = [jax.ShapeDtypeStruct((M_CTX, d), F32), jax.ShapeDtypeStruct((M_LAT, d), F32)]
    else:
        out_specs, out_shape = row_spec(d), jax.ShapeDtypeStruct((M_ALL, d), F32)
    return pl.pallas_call(
        functools.partial(_mix_ffn_kernel, tm=tm, ck=ck, n_in=len(xs), n_out=2 if split_out else 1, glu=glu),
        grid=(M_ALL // tm,),
        in_specs=x_specs + a_specs + [row_spec(MIX),
                            const((MIX, d), (0, 0)), const((MIX, d), (1, 0)),
                            mod(2),
                            pl.BlockSpec((1, d), lambda i: (0, 0)),
                            mod(4), mod(3), mod(5),
                            const((1, d, D_FF), (layer, 0, 0)), const((1, d, D_FF), (layer, 0, 0)),
                            const((1, D_FF, d), (layer, 0, 0))],
        out_specs=out_specs, out_shape=out_shape,
        compiler_params=_params("arbitrary"),
        name="mix_ffn",
    )(*xs, *a_args, ob, w_out_bf16, w_out_bf16, mods, norm_g.reshape(1, d), mods, mods, mods, wg, wu, wd)


def _qk_norm(x, gain, segm):
    ms = _split_dot(x * x, segm)
    return x * lax.rsqrt(ms + NORM_EPS) * gain


def _rope_pairs(x, cosf, sinf, half):
    lane = lax.broadcasted_iota(jnp.int32, x.shape, 1)
    first = (lane & (2 * half - 1)) < half
    n = x.shape[1]
    partner = jnp.where(first, pltpu.roll(x, n - half, axis=1), pltpu.roll(x, half, axis=1))
    return x * cosf + partner * sinf


def _attn_kernel(*refs, latent, lam_init):
    if latent:
        (q_ref, k_ref, v_ref, ck_ref, cv_ref, cosq_ref, sinq_ref, cosk_ref, sink_ref,
         gain_ref, lam_ref, sub_ref, _, o_ref, kall, vall) = refs
    else:
        q_ref, k_ref, v_ref, gain_ref, lam_ref, sub_ref, o_ref, kn_ref, vo_ref, kall, vall = refs
    w = 2 * HD_A
    segm = _seg_matrix(w, 6, 1.0 / HD_A)
    gains = gain_ref[...]
    cols = [slice(h * w, (h + 1) * w) for h in range(H_A)]

    @pl.when(pl.program_id(1) == 0)
    def _():
        k = [_qk_norm(k_ref[:, c], gains[1:2], segm) for c in cols]
        if latent:
            k = [_rope_pairs(x, cosk_ref[...], sink_ref[...], HD_A // 2) for x in k]
            for c, x in zip(cols, k):
                kall[0:PAST_LEN, c] = ck_ref[:, c].astype(BF16)
                kall[PAST_LEN:, c] = x.astype(BF16)
                vall[0:PAST_LEN, c] = cv_ref[:, c].astype(BF16)
                vall[PAST_LEN:, c] = v_ref[:, c].astype(BF16)
        else:
            for h, (c, x) in enumerate(zip(cols, k)):
                xt = x.T
                kn_ref[0, h, 0] = xt[:HD_A]
                kn_ref[0, h, 1] = xt[HD_A:]
                vo_ref[pl.ds(h, v_ref.shape[0], stride=H_A), :] = v_ref[:, c]
                kall[:, c] = x.astype(BF16)
                vall[:, c] = v_ref[:, c].astype(BF16)

    q = [_qk_norm(q_ref[:, c], gains[0:1], segm) for c in cols]
    if latent:
        q = [_rope_pairs(x, cosq_ref[...], sinq_ref[...], HD_A // 2) for x in q]
    lv = lam_ref[...]
    lam = (jnp.exp(jnp.sum(lv[0:1] * lv[1:2], axis=1, keepdims=True))
           - jnp.exp(jnp.sum(lv[2:3] * lv[3:4], axis=1, keepdims=True)) + lam_init)
    scale = HD_A ** -0.5
    comp0 = lax.broadcasted_iota(jnp.int32, q[0].shape, 1) < HD_A
    qc = [jnp.where(comp0, *sel).astype(BF16) for x in q for sel in ((x, 0.0), (0.0, x))]
    s = [lax.dot_general(qc[i], kall[:, cols[i // 2]], NT, preferred_element_type=F32) * scale
         for i in range(2 * H_A)]
    e = [jnp.exp(x - jnp.max(x, axis=-1, keepdims=True)) for x in s]
    p = [x / jnp.sum(x, axis=-1, keepdims=True) for x in e]
    att = [(p[2 * h] - lam * p[2 * h + 1]).astype(BF16) for h in range(H_A)]
    o = [jnp.dot(att[h], vall[:, cols[h]], preferred_element_type=F32) for h in range(H_A)]
    o = [x * lax.rsqrt(jnp.mean(x * x, axis=-1, keepdims=True) + NORM_EPS) * sub_ref[...] for x in o]
    for c, x in zip(cols, o):
        o_ref[:, c] = x * (1.0 - lam_init)


def _attention(p, gain2, lambda_qk, subln_g, lam_init, latent, cache_k=None, cache_v=None, rope=None,
               dst=None):
    w = 2 * HD_A
    if latent:
        nseq, t, tq, rb0, s_len = DEC_BATCH, DEC_SEQ, 128, M_CTX // DEC_SEQ, PAST_LEN + DEC_SEQ
    else:
        nseq, t, tq, rb0, s_len = BATCH, SEQ, SEQ, 0, SEQ
    nq = t // tq
    qoff = rb0 * nq
    full = lambda shape: pl.BlockSpec(shape, lambda b, i: (0,) * len(shape))
    in_specs = [pl.BlockSpec((tq, MIX), lambda b, i: (qoff + b * nq + i, 0)),
                pl.BlockSpec((t, MIX), lambda b, i: (rb0 + b, 1)),
                pl.BlockSpec((t, MIX), lambda b, i: (rb0 + b, 2))]
    args = [p, p, p]
    if latent:
        cosf, sinf = rope
        in_specs += [pl.BlockSpec((PAST_LEN, MIX), lambda b, i: (b, 0)),
                     pl.BlockSpec((PAST_LEN, MIX), lambda b, i: (b, 0)),
                     pl.BlockSpec((tq, w), lambda b, i: (i, 0)),
                     pl.BlockSpec((tq, w), lambda b, i: (i, 0)),
                     full((t, w)), full((t, w))]
        args += [cache_k, cache_v, cosf, sinf, cosf, sinf]
    in_specs += [full((2, w)), full((4, HD_A)), full((1, w))]
    args += [gain2, lambda_qk, subln_g.reshape(1, w)]
    o_spec = pl.BlockSpec((tq, MIX), lambda b, i: (qoff + b * nq + i, 0))
    o_shape = jax.ShapeDtypeStruct((M_ALL, MIX), F32)
    aliases = {}
    if latent:
        out_specs, out_shape = o_spec, o_shape
        in_specs.append(pl.BlockSpec(memory_space=pl.ANY))
        args.append(dst)
        aliases = {len(args) - 1: 0}
    else:
        out_specs = [o_spec, pl.BlockSpec((1, H_A, 2, HD_A, t), lambda b, i: (b, 0, 0, 0, 0)),
                     pl.BlockSpec((t * H_A, VD_A), lambda b, i: (b, 0))]
        out_shape = [o_shape, jax.ShapeDtypeStruct((nseq, H_A, 2, HD_A, t), F32),
                     jax.ShapeDtypeStruct((nseq * t * H_A, VD_A), F32)]
    return pl.pallas_call(
        functools.partial(_attn_kernel, latent=latent, lam_init=lam_init),
        grid=(nseq, nq),
        in_specs=in_specs, out_specs=out_specs, out_shape=out_shape,
        scratch_shapes=[pltpu.VMEM((s_len, MIX), BF16), pltpu.VMEM((s_len, MIX), BF16)],
        input_output_aliases=aliases,
        compiler_params=_params("parallel", "arbitrary"),
        name="diff_attention_lat" if latent else "diff_attention_ctx",
    )(*args)


def _centred_shift(x, mu):
    t = x.shape[0]
    row = lax.broadcasted_iota(jnp.int32, x.shape, 0)
    prev = jnp.where(row == 0, 0.0, pltpu.roll(x, 1, axis=0))
    nxt = jnp.where(row == t - 1, 0.0, pltpu.roll(x, t - 1, axis=0))
    return x + (0.5 * (prev + nxt) - x) * mu


def _seg_sum(x, segm):
    return jnp.concatenate([_split_dot(x[:, j * 128:(j + 1) * 128], segm) for j in range(x.shape[1] // 128)],
                           axis=1)


def _rwkv_prep_kernel(r_ref, k_ref, v_ref, l_ref, mur_ref, muk_ref, muv_ref, mul_ref,
                      kk_ref, ka_ref, rk_ref, w0_ref, wup_ref, a0_ref, aup_ref, gup_ref,
                      ro_ref, ldf_ref, ldb_ref, kbo_ref, vbo_ref, kko_ref, ao_ref, gate_ref, bonus_ref):
    seg1 = _seg_matrix(128, 6, 1.0)
    r = _centred_shift(r_ref[...], mur_ref[...])
    kb = _centred_shift(k_ref[...], muk_ref[...])
    vb = _centred_shift(v_ref[...], muv_ref[...])
    lo = _centred_shift(l_ref[...], mul_ref[...])
    xw = lo[:, 0:LORA_W]
    xa = lo[:, LORA_W:LORA_W + LORA_A]
    xg = lo[:, LORA_W + LORA_A:]
    kk = kb * kk_ref[...]
    kk = kk * lax.rsqrt(_seg_sum(kk * kk, seg1) + 1e-12)
    a = _sigmoid(a0_ref[...] + _bdot(xa, aup_ref[...]))
    kb2 = kb * (1.0 + (a - 1.0) * ka_ref[...])
    lw = jnp.tanh(xw)
    for dr, ld_ref in enumerate((ldf_ref, ldb_ref)):
        z = w0_ref[dr:dr + 1, :] + _bdot(lw, wup_ref[dr])
        logw = -_softplus(-z) - 0.5
        ld_ref[...] = -jnp.exp(logw)
    gate_ref[...] = _bdot(_sigmoid(xg), gup_ref[...])
    bonus_ref[...] = _seg_sum(r * kb2 * rk_ref[...], seg1) * vb
    ro_ref[...] = r
    kbo_ref[...] = kb2
    vbo_ref[...] = vb
    kko_ref[...] = kk
    ao_ref[...] = a


N_PREP_IN = 16
N_PREP_OUT = 9


def _split2(x):
    hi = x.astype(BF16)
    return hi, (x - hi.astype(F32)).astype(BF16)


RWKV_INV_BLOCK = 16
RWKV_UNROLL_MAX = 4


def _rwkv_masks(c, rev):
    ti = lax.broadcasted_iota(jnp.int32, (c, c), 0)
    si = lax.broadcasted_iota(jnp.int32, (c, c), 1)
    tri = jnp.where((si >= ti) if rev else (si <= ti), 1.0, 0.0).astype(BF16)
    t4 = lax.broadcasted_iota(jnp.int32, (4 * c, 4 * c), 0)
    s4 = lax.broadcasted_iota(jnp.int32, (4 * c, 4 * c), 1)
    tm, sm = t4 & (c - 1), s4 & (c - 1)
    strict = (sm > tm) if rev else (sm < tm)
    incl = (sm >= tm) if rev else (sm <= tm)
    same_head = ((t4 // c) & 1) == ((s4 // c) & 1)
    top = t4 < 2 * c
    gmask = same_head & ((top & strict) | (~top & incl))
    t2 = lax.broadcasted_iota(jnp.int32, (2 * c, 2 * c), 0)
    s2 = lax.broadcasted_iota(jnp.int32, (2 * c, 2 * c), 1)
    same = lambda n: (t2 // n) == (s2 // n)
    levels = []
    n = RWKV_INV_BLOCK
    while n < c:
        levels.append(same(2 * n) & ~same(n))
        n *= 2
    f = lambda m: jnp.where(m, 1.0, 0.0)
    return tri, f(gmask), f(same(RWKV_INV_BLOCK)), tuple(f(m) for m in levels), f(same(c)), f(t2 == s2)


def _keep(mask01, x):
    return jnp.where(mask01 > 0.5, x, 0.0)


def _tri_inverse(a, diag_blk, levels, eye):
    n = a[0].shape[0]
    d = [_keep(diag_blk, x) for x in a]
    t = [eye + x for x in d]
    p = [_bdot(x, x) for x in d]
    for _ in range(int(math.log2(RWKV_INV_BLOCK)) - 2):
        res = [_bdot(jnp.concatenate([pi, ti], axis=0), pi) for pi, ti in zip(p, t)]
        p = [x[:n] for x in res]
        t = [ti + x[n:] for ti, x in zip(t, res)]
    t = [ti + _bdot(ti, pi) for ti, pi in zip(t, p)]
    for off in levels:
        half = [_bdot(ti, _keep(off, x)) for ti, x in zip(t, a)]
        t = [ti + _bdot(x, ti) for ti, x in zip(t, half)]
    return t


def _rwkv_pair_chunks(ins, sts, tris, gmasks, diag_blk, levels, same_head, eye):
    c, w = ins[0][0].shape
    hd = w // 2
    nch = range(len(ins))
    r, ld, kb, vb, kk, a = (list(z) for z in zip(*ins))
    split = [_split2(x) for x in ld]
    lcum = [jnp.dot(tris[i], jnp.concatenate(split[i], axis=1), preferred_element_type=F32) for i in nch]
    lcum = [x[:, :w] + x[:, w:] for x in lcum]
    ltot = [jnp.sum(x, axis=0, keepdims=True) for x in ld]
    beta = [kk[i] * a[i] for i in nch]
    eneg = [jnp.exp(-x) for x in lcum]
    abar = [-kk[i] * jnp.exp(lcum[i] - ld[i]) for i in nch]
    rbar = [r[i] * jnp.exp(lcum[i]) for i in nch]
    bt = [(beta[i] * eneg[i]).astype(BF16) for i in nch]
    kt = [(kb[i] * eneg[i]).astype(BF16) for i in nch]
    vbb = [x.astype(BF16) for x in vb]
    head0 = lax.broadcasted_iota(jnp.int32, (c, w), 1) < hd
    pick = lambda res: jnp.where(head0, res[:c], res[c:])
    arst = [_bdot(jnp.concatenate([abar[i], rbar[i]], axis=0), sts[i], NT) for i in nch]
    lhs = [jnp.concatenate([jnp.where(head0, abar[i], 0.0), jnp.where(head0, 0.0, abar[i]),
                            jnp.where(head0, rbar[i], 0.0), jnp.where(head0, 0.0, rbar[i])], axis=0) for i in nch]
    g = [_keep(gmasks[i], _bdot(lhs[i], jnp.concatenate([bt[i], bt[i], kt[i], kt[i]], axis=0), NT))
         for i in nch]
    x = [arst[i][:c] + pick(_bdot(g[i][:2 * c, 2 * c:], jnp.concatenate([vbb[i], vbb[i]], axis=0))) for i in nch]
    tinv = _tri_inverse([gi[:2 * c, :2 * c] for gi in g], diag_blk, levels, eye)
    u = [pick(_bdot(tinv[i], jnp.concatenate([x[i], x[i]], axis=0))) for i in nch]
    ub = [z.astype(BF16) for z in u]
    y = [arst[i][c:] + pick(_bdot(g[i][2 * c:], jnp.concatenate([ub[i], ub[i], vbb[i], vbb[i]], axis=0)))
         for i in nch]
    erem = [jnp.exp(ltot[i] - lcum[i]) for i in nch]
    bkh = [jnp.concatenate([beta[i] * erem[i], kb[i] * erem[i]], axis=0) for i in nch]
    st_new = [jnp.exp(ltot[i]) * sts[i]
              + _keep(same_head, _bdot(jnp.concatenate([ub[i], vbb[i]], axis=0), bkh[i], TN)) for i in nch]
    return y, st_new


def _rwkv_scan_kernel(*refs, t, c, npair, latent):
    prep_in, (lnw_ref, lnb_ref), rest = refs[:N_PREP_IN], refs[N_PREP_IN:N_PREP_IN + 2], refs[N_PREP_IN + 2:]
    if latent:
        s0_ref, _, o_ref, *scratch = rest
    else:
        o_ref, sfin_ref, *scratch = rest
    *prep_out, yb_ref = scratch
    _rwkv_prep_kernel(*prep_in, *prep_out)
    r_ref, ldf_ref, ldb_ref, kb_ref, vb_ref, kk_ref, a_ref, gate_ref, bonus_ref = prep_out
    n = t // c
    w = 2 * HD_B
    masks = [_rwkv_masks(c, rev) for rev in (False, True)]

    def body(ci, states):
        ins, tris, gmasks, dsts = [], [], [], []
        for dr, (ld_ref, dst) in enumerate(((ldf_ref, o_ref), (ldb_ref, yb_ref))):
            cj = ci if dr == 0 else n - 1 - ci
            rows = pl.ds(pl.multiple_of(cj * c, c), c)
            for p in range(npair):
                cols = slice(p * w, (p + 1) * w)
                ins.append(tuple(ref[rows, cols] for ref in (r_ref, ld_ref, kb_ref, vb_ref, kk_ref, a_ref)))
                tris.append(masks[dr][0])
                gmasks.append(masks[dr][1])
                dsts.append((dst, rows, cols))
        ys, new_states = _rwkv_pair_chunks(ins, list(states), tris, gmasks, *masks[0][2:])
        for (dst, rows, cols), y in zip(dsts, ys):
            dst[rows, cols] = y
        return tuple(new_states)

    if latent:
        zero = jnp.zeros((HD_B, HD_B), F32)
        init = tuple(jnp.concatenate([jnp.concatenate([s0_ref[0, dr, 2 * p], zero], axis=1),
                                      jnp.concatenate([zero, s0_ref[0, dr, 2 * p + 1]], axis=1)], axis=0)
                     for dr in range(2) for p in range(npair))
    else:
        init = tuple(jnp.zeros((w, w), F32) for _ in range(2 * npair))
    fin = lax.fori_loop(0, n, body, init, unroll=True if n <= RWKV_UNROLL_MAX else 2)
    segm = _seg_matrix(w, 6, 1.0 / HD_B)
    for p in range(npair):
        if not latent:
            for dr in range(2):
                st = fin[dr * npair + p]
                sfin_ref[0, dr, 2 * p] = st[:HD_B, :HD_B]
                sfin_ref[0, dr, 2 * p + 1] = st[HD_B:, HD_B:]
        cols = slice(p * w, (p + 1) * w)
        y = o_ref[:, cols] + yb_ref[:, cols]
        yc = y - _split_dot(y, segm)
        yn = yc * lax.rsqrt(_split_dot(yc * yc, segm) + RWKV_GN_EPS)
        o_ref[:, cols] = (yn * lnw_ref[:, cols] + lnb_ref[:, cols] + bonus_ref[:, cols]) * gate_ref[:, cols]


def _rwkv(p, prm, ln_w, ln_b, latent, s0=None, dst=None):
    mu, k_k, k_a, r_k, w0, w_up, a0, a_up, g_up = prm
    nseq, t, rb0 = (DEC_BATCH, DEC_SEQ, M_CTX // DEC_SEQ) if latent else (BATCH, SEQ, 0)
    npair = H_B // 2
    lw = LORA_W + LORA_A + LORA_G
    c0 = (IN_AB - 3 * MIX - lw) // MIX
    cl = (IN_AB - lw) // lw
    full = lambda *shape: pl.BlockSpec(shape, lambda b: (0,) * len(shape))
    st_spec = pl.BlockSpec((1, 2, H_B, HD_B, HD_B), lambda b: (b, 0, 0, 0, 0))
    o_spec = pl.BlockSpec((t, MIX), lambda b: (rb0 + b, 0))
    o_shape = jax.ShapeDtypeStruct((M_ALL, MIX), F32)
    in_specs = [pl.BlockSpec((t, MIX), lambda b: (rb0 + b, c0)),
                pl.BlockSpec((t, MIX), lambda b: (rb0 + b, c0 + 1)),
                pl.BlockSpec((t, MIX), lambda b: (rb0 + b, c0 + 2)),
                pl.BlockSpec((t, lw), lambda b: (rb0 + b, cl)),
                full(1, MIX), full(1, MIX), full(1, MIX), full(1, lw),
                full(1, MIX), full(1, MIX), full(1, MIX),
                full(2, MIX), full(2, LORA_W, MIX), full(1, MIX), full(LORA_A, MIX), full(LORA_G, MIX),
                full(1, MIX), full(1, MIX)]
    args = [p, p, p, p,
            mu[None, 0:MIX], mu[None, MIX:2 * MIX], mu[None, 2 * MIX:3 * MIX], mu[None, 3 * MIX:],
            k_k.reshape(1, MIX), k_a.reshape(1, MIX), r_k.reshape(1, MIX), w0, w_up,
            a0.reshape(1, MIX), a_up, g_up, ln_w.reshape(1, MIX), ln_b.reshape(1, MIX)]
    if latent:
        args += [s0, dst]
        in_specs += [st_spec, pl.BlockSpec(memory_space=pl.ANY)]
        out_specs, out_shape, aliases = o_spec, o_shape, {len(args) - 1: 0}
    else:
        out_specs = [o_spec, st_spec]
        out_shape = [o_shape, jax.ShapeDtypeStruct((nseq, 2, H_B, HD_B, HD_B), F32)]
        aliases = {}
    return pl.pallas_call(
        functools.partial(_rwkv_scan_kernel, t=t, c=RWKV_CHUNK, npair=npair, latent=latent),
        grid=(nseq,),
        in_specs=in_specs, out_specs=out_specs, out_shape=out_shape,
        scratch_shapes=[pltpu.VMEM((t, MIX), F32)] * (N_PREP_OUT + 1),
        input_output_aliases=aliases,
        compiler_params=_params("parallel"),
        name="rwkv_lat" if latent else "rwkv_ctx",
    )(*args)


S5_L = 16
S5_ROWS = M_ALL // S5_L
S5_CW = S5_L * S5_GROUP
S5_GS = 8
S5_PW = 2 * P_C
S5_TAB = 24
S5_PRM = 2 * S5_GROUP
S5_NC_CTX = SEQ // S5_L
S5_NC_LAT = DEC_SEQ // S5_L
S5_CTX_ROWS = BATCH * S5_NC_CTX


def _gelu_tanh(x):
    return 0.5 * x * (1.0 + jnp.tanh(math.sqrt(2.0 / math.pi) * (x + 0.044715 * (x * x * x))))


def _cmul(ar, ai, br, bi):
    return ar * br - ai * bi, ar * bi + ai * br


def _dot3(a, b, dims):
    ah, al = _split2(a)
    bh, bl = _split2(b)
    d = lambda x, y: lax.dot_general(x, y, dims, preferred_element_type=F32)
    return d(ah, bh) + d(ah, bl) + d(al, bh)


def _s5_param_kernel(prm_ref, lam_ref, tab_ref, r_ref, st_ref):
    nl, ns = S5_L, S5_GROUP
    lane_blk = lax.broadcasted_iota(jnp.int32, (ns, S5_CW), 1) // ns
    kf = lax.broadcasted_iota(jnp.int32, (S5_TAB, P_C), 0).astype(F32)
    t_rows = [jnp.zeros((ns, S5_CW), F32) for _ in range(nl)]
    e_parts = []
    for d in range(2):
        lr, li = lam_ref[0, d, 0:1, :], lam_ref[0, d, 1:2, :]
        dt = jnp.exp(lam_ref[0, d, 2:3, :])
        mag = jnp.exp(lr * dt * kf)
        pw_re, pw_im = mag * jnp.cos(li * dt * kf), mag * jnp.sin(li * dt * kf)
        tab_ref[0, d, 0] = pw_re
        tab_ref[0, d, 1] = pw_im
        nr, ab_im, den = pw_re[1:2] - 1.0, pw_im[1:2], lr * lr + li * li
        f = ((nr * lr + ab_im * li) / den, (ab_im * lr - nr * li) / den)
        part = lambda lo, n: (prm_ref[0, d, 0, lo:lo + n, :], prm_ref[0, d, 1, lo:lo + n, :])
        bb = _cmul(*f, *part(0, ns))
        cc = part(ns, ns)
        pw = lambda k: (pw_re[k:k + 1], pw_im[k:k + 1])
        steps = range(nl)
        if d == 0:
            ke, ks, kk = [nl - 1 - j for j in steps], [j + 1 for j in steps], list(steps)
        else:
            ke, ks, kk = list(steps), [nl - j for j in steps], [nl - 1 - j for j in steps]
        stack = lambda xs: (jnp.concatenate([x[0] for x in xs], axis=0), jnp.concatenate([x[1] for x in xs], axis=0))
        e_re, e_im = stack([_cmul(*bb, *pw(k)) for k in ke])
        s_re, s_im = stack([_cmul(*cc, *pw(k)) for k in ks])
        k_re, k_im = stack([_cmul(*cc, *pw(k)) for k in kk])
        st_ref[0, d] = jnp.concatenate([s_re, -s_im], axis=1).astype(BF16)
        krow = _dot3(bb[0], k_re, NT) - _dot3(bb[1], k_im, NT)
        for j in steps:
            if d == 0:
                shifted, keep = pltpu.roll(krow, ns * j, axis=1), lane_blk >= j
            else:
                shifted, keep = pltpu.roll(krow, (S5_CW - ns * (nl - 1 - j)) % S5_CW, axis=1), lane_blk <= j
            t_rows[j] = t_rows[j] + jnp.where(keep, shifted, 0.0)
        e_parts += [e_re, e_im]
    r_ref[0] = jnp.concatenate([jnp.concatenate(t_rows, axis=0)] + e_parts, axis=1).astype(BF16)


def _s5_params(prm, lam):
    tab_spec = pl.BlockSpec((1, 2, 2, S5_TAB, P_C), lambda g: (g, 0, 0, 0, 0))
    return pl.pallas_call(
        _s5_param_kernel,
        grid=(G_C,),
        in_specs=[pl.BlockSpec((1, 2, 2, S5_PRM, P_C), lambda g: (g, 0, 0, 0, 0)),
                  pl.BlockSpec((1, 2, 8, P_C), lambda g: (g, 0, 0, 0))],
        out_specs=[tab_spec,
                   pl.BlockSpec((1, S5_CW, 2 * S5_CW), lambda g: (g, 0, 0)),
                   pl.BlockSpec((1, 2, S5_CW, S5_PW), lambda g: (g, 0, 0, 0))],
        out_shape=[jax.ShapeDtypeStruct((G_C, 2, 2, S5_TAB, P_C), F32),
                   jax.ShapeDtypeStruct((G_C, S5_CW, 2 * S5_CW), BF16),
                   jax.ShapeDtypeStruct((G_C, 2, S5_CW, S5_PW), BF16)],
        compiler_params=_params("parallel"),
        name="s5_params",
    )(prm, lam)


def _s5_core_kernel(u_ref, r_ref, st_ref, tab_ref, h0_ref, o_ref, hfin_ref, ug_scr, yg_scr, e_scr, hp_scr):
    ns, nl = S5_GROUP, S5_L
    per_v = 128 // ns
    blk = lax.broadcasted_iota(jnp.int32, (8, 128), 1) // ns

    def merge(select):
        acc = select(0)
        for b in range(1, per_v):
            acc = jnp.where(blk == b, select(b), acc)
        return acc

    def shuffle(srcs):
        rolled = []
        for s in range(per_v):
            m = merge(lambda b: srcs[(b + s) % per_v])
            rolled.append(m if s == 0 else pltpu.roll(m, s * ns, axis=1))
        return [merge(lambda b: rolled[(b - a) % per_v]) for a in range(per_v)]

    def tiles(regroup):
        def ctx_tile(ti, _):
            c, b0 = ti // (BATCH // 8), (ti % (BATCH // 8)) * 8
            regroup(b0 * SEQ + c * nl, SEQ, pl.multiple_of(c * BATCH + b0, 8))
            return 0

        def lat_tile(ti, _):
            b, c0 = ti // (S5_NC_LAT // 8), (ti % (S5_NC_LAT // 8)) * 8
            regroup(M_CTX + b * DEC_SEQ + c0 * nl, nl, pl.multiple_of(S5_CTX_ROWS + b * S5_NC_LAT + c0, 8))
            return 0

        lax.fori_loop(0, S5_CTX_ROWS // 8, ctx_tile, 0, unroll=4)
        lax.fori_loop(0, (S5_ROWS - S5_CTX_ROWS) // 8, lat_tile, 0, unroll=4)

    def regroup_in(tok, stride, row):
        uj = [u_ref[pl.ds(tok + j, 8, stride=stride), :] for j in range(nl)]
        for jh in range(nl // per_v):
            for g, v in enumerate(shuffle(uj[jh * per_v:(jh + 1) * per_v])):
                ug_scr[g, pl.ds(row, 8), jh * 128:(jh + 1) * 128] = v

    tiles(regroup_in)

    chains = [(g, d) for g in range(S5_GS) for d in range(2)]
    for g in range(S5_GS):
        a = jnp.dot(ug_scr[g].astype(BF16), r_ref[g], preferred_element_type=F32)
        yg_scr[g] = a[:, 0:S5_CW]
        for d in range(2):
            e_scr[g, d] = a[:, S5_CW + d * S5_PW:S5_CW + (d + 1) * S5_PW]
    lam = []
    for g, d in chains:
        lr, li = (tab_ref[g, d, ri, nl:nl + 1, :] for ri in range(2))
        lam.append((jnp.concatenate([lr, lr], axis=1), jnp.concatenate([-li, li], axis=1)))

    def advance(c, hs, latent):
        nc = S5_NC_LAT if latent else S5_NC_CTX
        out = []
        for i, (g, d) in enumerate(chains):
            cc = c if d == 0 else nc - 1 - c
            rows = (pl.ds(S5_CTX_ROWS + cc, DEC_BATCH, stride=nc) if latent
                    else pl.ds(pl.multiple_of(cc * BATCH, BATCH), BATCH))
            hp_scr[g, d, rows, :] = hs[i]
            out.append(lam[i][0] * hs[i] + lam[i][1] * pltpu.roll(hs[i], P_C, axis=1) + e_scr[g, d, rows, :])
        return out

    def both(c, carry):
        hc, hl = carry
        return tuple(advance(c, hc, False)), tuple(advance(c, hl, True))

    h_ctx = tuple(jnp.zeros((BATCH, S5_PW), F32) for _ in chains)
    h_lat = tuple(h0_ref[g, d] for g, d in chains)
    h_ctx, h_lat = lax.fori_loop(0, S5_NC_CTX, both, (h_ctx, h_lat), unroll=2)
    lax.fori_loop(S5_NC_CTX, S5_NC_LAT, lambda c, hl: tuple(advance(c, hl, True)), h_lat, unroll=4)
    for i, (g, d) in enumerate(chains):
        hfin_ref[g, d] = h_ctx[i]
    for g in range(S5_GS):
        y = yg_scr[g]
        for d in range(2):
            y = y + lax.dot_general(hp_scr[g, d].astype(BF16), st_ref[g, d], NT, preferred_element_type=F32)
        yg_scr[g] = y

    def regroup_out(tok, stride, row):
        for jh in range(nl // per_v):
            yv = [yg_scr[g, pl.ds(row, 8), jh * 128:(jh + 1) * 128] for g in range(S5_GS)]
            for jj, v in enumerate(shuffle(yv)):
                o_ref[pl.ds(tok + jh * per_v + jj, 8, stride=stride), :] = v

    tiles(regroup_out)


def _s5_core(p2, r, st, tab, h0):
    blk = lambda *tail: pl.BlockSpec((S5_GS,) + tail, lambda s: (s,) + (0,) * len(tail))
    col = pl.BlockSpec((M_ALL, S5_GS * S5_GROUP), lambda s: (0, s))
    return pl.pallas_call(
        _s5_core_kernel,
        grid=(G_C // S5_GS,),
        in_specs=[col, blk(S5_CW, 2 * S5_CW), blk(2, S5_CW, S5_PW),
                  blk(2, 2, S5_TAB, P_C), blk(2, DEC_BATCH, S5_PW)],
        out_specs=[col, blk(2, BATCH, S5_PW)],
        out_shape=[jax.ShapeDtypeStruct((M_ALL, MIX), F32),
                   jax.ShapeDtypeStruct((G_C, 2, BATCH, S5_PW), F32)],
        scratch_shapes=[pltpu.VMEM((S5_GS, S5_ROWS, S5_CW), F32),
                        pltpu.VMEM((S5_GS, S5_ROWS, S5_CW), F32),
                        pltpu.VMEM((S5_GS, 2, S5_ROWS, S5_PW), F32),
                        pltpu.VMEM((S5_GS, 2, S5_ROWS, S5_PW), F32)],
        compiler_params=_params("parallel"),
        name="s5_core",
    )(p2, r, st, tab, h0)


def _s5_tables(lam_re, lam_im, log_dt, b_re, b_im, c_re, c_im):
    rows = lambda b, c: jnp.concatenate([jnp.swapaxes(b, -1, -2), c], axis=2)
    prm = jnp.stack([rows(b_re, c_re), rows(b_im, c_im)], axis=1)
    dt = jnp.broadcast_to(log_dt[:, :, None], lam_re.shape)
    lam = jnp.stack([lam_re, lam_im, dt] + [jnp.zeros_like(dt)] * 5, axis=2)
    return jnp.transpose(prm, (2, 0, 1, 3, 4)), jnp.transpose(lam, (1, 0, 2, 3))


def _s5_mixer(p2, tables, state_re, state_im):
    tab, r, st = _s5_params(*tables)
    h0 = jnp.concatenate([state_re, state_im], axis=-1)
    y_tok, hfin = _s5_core(p2, r, st, tab, jnp.transpose(h0, (2, 1, 0, 3)))
    hfin = jnp.transpose(hfin, (2, 1, 0, 3))
    return y_tok, hfin[..., :P_C], hfin[..., P_C:]


def _ret_kernel(*refs, t, latent):
    if latent:
        (q_ref, k_ref, v_ref, g_ref, dl_ref, lnw_ref, lnb_ref, r0_ref, cos_ref, sin_ref, _,
         o_ref, ob_ref) = refs
    else:
        q_ref, k_ref, v_ref, g_ref, dl_ref, lnw_ref, lnb_ref, o_ref, rfin_ref, ob_ref = refs
    n = t // CHUNK
    w = HD_D
    jf = lax.broadcasted_iota(jnp.int32, (CHUNK, CHUNK), 0).astype(F32)
    kf = lax.broadcasted_iota(jnp.int32, (CHUNK, CHUNK), 1).astype(F32)
    diff = jf - kf
    tabs = []
    for dr in range(2):
        for h in range(H_D):
            l = -_softplus(-dl_ref[h, dr:dr + 1, :])
            if dr == 0:
                dmat = jnp.where(diff >= 0, jnp.exp(l * jnp.maximum(diff, 0.0)), 0.0)
                xi = jnp.exp(l * (jf + 1.0))
                zeta = jnp.exp(l * (CHUNK - 1.0 - jf))
            else:
                dmat = jnp.where(diff < 0, jnp.exp(l * jnp.maximum(-diff, 0.0)), 0.0)
                xi = jnp.exp(l * (CHUNK - jf))
                zeta = jnp.exp(l * jf)
            tabs.append((dmat, xi, zeta, jnp.exp(l * CHUNK)))
    chains = [(dr, h) for dr in range(2) for h in range(H_D)]

    def body(ci, states):
        q, k, v, where = [], [], [], []
        for dr, h in chains:
            cj = ci if dr == 0 else n - 1 - ci
            rows = pl.ds(pl.multiple_of(cj * CHUNK, CHUNK), CHUNK)
            cols = slice(h * w, (h + 1) * w)
            qi = q_ref[rows, cols]
            ki = k_ref[rows, cols] * (HD_D ** -0.5)
            if latent:
                qi = _rope_pairs(qi, cos_ref[rows, :], sin_ref[rows, :], HD_D // 2)
                ki = _rope_pairs(ki, cos_ref[rows, :], sin_ref[rows, :], HD_D // 2)
            q.append(qi.astype(BF16))
            k.append(ki)
            v.append(v_ref[rows, cols].astype(BF16))
            where.append((o_ref if dr == 0 else ob_ref, rows, cols))
        nch = range(len(chains))
        inner = [(_bdot(q[i], k[i], NT) * tabs[i][0]).astype(BF16) for i in nch]
        cross = [_bdot(q[i], states[i]) * tabs[i][1] for i in nch]
        kz = [(k[i] * tabs[i][2]).astype(BF16) for i in nch]
        out = [jnp.dot(inner[i], v[i], preferred_element_type=F32) + cross[i] for i in nch]
        new_states = [states[i] * tabs[i][3] + lax.dot_general(kz[i], v[i], TN, preferred_element_type=F32)
                      for i in nch]
        for (dst, rows, cols), y in zip(where, out):
            dst[rows, cols] = y
        return tuple(new_states)

    if latent:
        lax.fori_loop(0, n, body, tuple(r0_ref[0, dr, h] for dr, h in chains), unroll=2)
    else:
        fin = lax.fori_loop(0, n, body, tuple(jnp.zeros((w, w), F32) for _ in chains), unroll=True)
        for i, (dr, h) in enumerate(chains):
            rfin_ref[0, dr, h] = fin[i]
    for h in range(H_D):
        cols = slice(h * w, (h + 1) * w)
        y = o_ref[:, cols] + ob_ref[:, cols]
        yc = y - jnp.mean(y, axis=-1, keepdims=True)
        yn = yc * lax.rsqrt(jnp.mean(yc * yc, axis=-1, keepdims=True) + RET_GN_EPS)
        o_ref[:, cols] = (yn * lnw_ref[:, cols] + lnb_ref[:, cols]) * _silu(g_ref[:, cols])


def _retention(p, decay_logit, ln_w, ln_b, latent, r0=None, rope=None, dst=None):
    if latent:
        nseq, t, rb0 = DEC_BATCH, DEC_SEQ, M_CTX // DEC_SEQ
    else:
        nseq, t, rb0 = BATCH, SEQ, 0
    w = HD_D
    dl = jnp.broadcast_to(decay_logit.T[:, :, None], (H_D, 2, w))
    full = lambda shape: pl.BlockSpec(shape, lambda b: (0,) * len(shape))
    st_spec = pl.BlockSpec((1, 2, H_D, w, w), lambda b: (b, 0, 0, 0, 0))
    in_specs = [pl.BlockSpec((t, MIX), lambda b, j=j: (rb0 + b, j)) for j in (1, 2, 3, 4)]
    in_specs += [full((H_D, 2, w)), full((1, MIX)), full((1, MIX))]
    args = [p, p, p, p, dl, ln_w.reshape(1, MIX), ln_b.reshape(1, MIX)]
    o_spec = pl.BlockSpec((t, MIX), lambda b: (rb0 + b, 0))
    o_shape = jax.ShapeDtypeStruct((M_ALL, MIX), F32)
    if latent:
        in_specs += [st_spec, full((t, w)), full((t, w)), pl.BlockSpec(memory_space=pl.ANY)]
        args += [r0, *rope, dst]
        out_specs, out_shape, aliases = o_spec, o_shape, {len(args) - 1: 0}
    else:
        out_specs = [o_spec, st_spec]
        out_shape = [o_shape, jax.ShapeDtypeStruct((nseq, 2, H_D, w, w), F32)]
        aliases = {}
    return pl.pallas_call(
        functools.partial(_ret_kernel, t=t, latent=latent),
        grid=(nseq,),
        in_specs=in_specs, out_specs=out_specs, out_shape=out_shape,
        scratch_shapes=[pltpu.VMEM((t, MIX), F32)],
        input_output_aliases=aliases,
        compiler_params=_params("parallel"),
        name="retention_lat" if latent else "retention_ctx",
    )(*args)


def _rope_tables(n_tok, dim, reps):
    rows = n_tok // GRID_W
    n_freq = dim // 4
    inv = 1.0 / (ROPE_THETA ** (jnp.arange(n_freq, dtype=F32) / n_freq))
    row = jnp.repeat(jnp.arange(rows, dtype=F32), GRID_W)
    col = jnp.tile(jnp.arange(GRID_W, dtype=F32), rows)
    ang = jnp.concatenate([row[:, None] * inv, col[:, None] * inv], axis=-1)
    cos, sin = jnp.cos(ang), jnp.sin(ang)
    return jnp.tile(jnp.concatenate([cos, cos], axis=1), (1, reps)), \
        jnp.tile(jnp.concatenate([-sin, sin], axis=1), (1, reps))


def kernel(x_prompt, x_sample, cache_k_ab, cache_v_ab, state_rwkv, state_s5_re, state_s5_im, state_ret, c, c_ctx, norm1_g, norm2_g, w_mod, b_mod, w_ff_gate, w_ff_up, w_ff_down, w_in_ab, w_out_ab, qk_gain_a, lambda_qk, subln_g, rwkv_mu, rwkv_k_k, rwkv_k_a, rwkv_r_k, rwkv_w0, rwkv_w_up, rwkv_a0, rwkv_a_up, rwkv_g_up, rwkv_ln_w, rwkv_ln_b, w_in_cd, w_out_cd, s5_lam_re, s5_lam_im, s5_log_dt, s5_b_re, s5_b_im, s5_c_re, s5_c_im, s5_d, s5_w_glu, ret_decay_logit, ret_ln_w, ret_ln_b):
    d = D_MODEL
    xs = [x_prompt.reshape(M_CTX, d), x_sample.reshape(M_LAT, d)]
    cvec = jnp.zeros((MOD_ROWS, d), F32).at[0].set(c_ctx).at[1:1 + DEC_BATCH].set(c)
    mods = _modulation(cvec, w_mod, b_mod)

    lam_init = 0.8 - 0.6 * math.exp(-0.3 * 0)
    p = _norm_linear(xs, norm1_g[0], mods[0], w_in_ab)
    gain2 = jnp.tile(qk_gain_a[0], (1, 2))
    rope_a = _rope_tables(DEC_SEQ, HD_A, 2)
    ck = cache_k_ab[:, 0].reshape(DEC_BATCH * PAST_LEN, MIX)
    cv = cache_v_ab[:, 0].reshape(DEC_BATCH * PAST_LEN, MIX)
    oa, k_ctx, v_ctx = _attention(p, gain2, lambda_qk[0], subln_g[0], lam_init, latent=False)
    oa = _attention(p, gain2, lambda_qk[0], subln_g[0], lam_init, latent=True,
                    cache_k=ck, cache_v=cv, rope=rope_a, dst=oa)
    rw_prm = (rwkv_mu[0], rwkv_k_k[0], rwkv_k_a[0], rwkv_r_k[0], rwkv_w0[0], rwkv_w_up[0],
              rwkv_a0[0], rwkv_a_up[0], rwkv_g_up[0])
    ob, sfin_ctx = _rwkv(p, rw_prm, rwkv_ln_w[0], rwkv_ln_b[0], latent=False)
    ob = _rwkv(p, rw_prm, rwkv_ln_w[0], rwkv_ln_b[0], latent=True, s0=state_rwkv[:, 0], dst=ob)
    ffn_w = (w_ff_gate.astype(BF16), w_ff_up.astype(BF16), w_ff_down.astype(BF16))
    x = _mix_ffn(xs, oa, ob, w_out_ab[0].astype(BF16), norm2_g[0], mods[0], *ffn_w, layer=0, split_out=False)

    p2 = _norm_linear([x], norm1_g[1], mods[1], w_in_cd)
    tables = _s5_tables(s5_lam_re[0], s5_lam_im[0], s5_log_dt[0], s5_b_re[0], s5_b_im[0],
                        s5_c_re[0], s5_c_im[0])
    y_s5, s5_fin_re, s5_fin_im = _s5_mixer(p2, tables, state_s5_re[:, 0], state_s5_im[:, 0])
    oc = (y_s5, p2, s5_d[0], s5_w_glu[0].astype(BF16))
    rope_d = _rope_tables(DEC_SEQ, HD_D, 1)
    od, rfin = _retention(p2, ret_decay_logit[0], ret_ln_w[0], ret_ln_b[0], latent=False)
    od = _retention(p2, ret_decay_logit[0], ret_ln_w[0], ret_ln_b[0], latent=True, r0=state_ret[:, 0],
                    rope=rope_d, dst=od)
    y_ctx, y_lat = _mix_ffn([x], oc, od, w_out_cd[0].astype(BF16), norm2_g[1], mods[1], *ffn_w, layer=1,
                            split_out=True)
    y_prompt = y_ctx.reshape(BATCH, SEQ, d)
    y_sample = y_lat.reshape(DEC_BATCH, DEC_SEQ, d)
    new_k = jnp.transpose(k_ctx, (0, 4, 1, 2, 3))[:, None]
    new_v = v_ctx.reshape(BATCH, 1, SEQ, H_A, VD_A)
    new_rwkv = sfin_ctx[:, None]
    new_s5_re = s5_fin_re[:, None]
    new_s5_im = s5_fin_im[:, None]
    new_ret = rfin[:, None]
    return (y_prompt, y_sample, new_k, new_v, new_rwkv, new_s5_re, new_s5_im, new_ret)
```
